```python
import math
import jax, jax.numpy as jnp
from jax import lax
import numpy as np

D_MODEL = 2048
BATCH = 2
SEQ = 4096
DEPTH = 2

HEAD_DIM = 64
A_Q_HEADS = 16
A_KV_HEADS = 2
A_GROUP = A_Q_HEADS // A_KV_HEADS
A_WINDOW = 128
A_BLOCK = 128
T5_BUCKETS = 32
T5_MAX_DIST = 128
B_HEADS = 16
GRID_W = 64
NA_WIN_H = 8
NA_WIN_W = 16
A_WIDTH = A_Q_HEADS * HEAD_DIM
A_KV_WIDTH = A_KV_HEADS * HEAD_DIM
B_WIDTH = B_HEADS * HEAD_DIM
ATTN_IN = A_WIDTH + 2 * A_KV_WIDTH + 3 * B_WIDTH
MIX_WIDTH = A_WIDTH + B_WIDTH
HYENA_ORDER = 2
HYENA_WIDTH = D_MODEL
HYENA_EMB = 33
HYENA_FILTER_HIDDEN = 64
HYENA_SHORT = 3
HYENA_DECAY_TARGET = 1e-2
HYENA_FAST_PCT = 0.3
HYENA_SLOW_PCT = 1.5
D_FF = 5632
N_EXPERTS = 8
TOP_K = 2
D_FF_EXPERT = 7168
MOE_BLOCK = 128
PLE_DIM = 256
RMS_EPS = 1e-6

kernel_name = 'hybrid_swa_natten_hyena_moe_encoder'

F32 = jnp.float32
NEG_INF = -1e30


def rmsnorm(x, g):
    xf = x.astype(F32)
    y = xf * lax.rsqrt(jnp.mean(xf * xf, axis=-1, keepdims=True) + RMS_EPS)
    return (y * g.astype(F32)).astype(x.dtype)


def t5_bucket(rel):
    half = T5_BUCKETS // 2
    max_exact = half // 2
    n = jnp.abs(rel)
    log_ratio = jnp.log(jnp.maximum(n, 1).astype(F32) / max_exact) / math.log(T5_MAX_DIST / max_exact)
    large = jnp.minimum(max_exact + (log_ratio * (half - max_exact)).astype(jnp.int32), half - 1)
    return jnp.where(rel > 0, half, 0) + jnp.where(n < max_exact, n, large)


def window_gqa(q, k, v, t5_bias, sink):
    B, S = q.shape[0], q.shape[1]
    nb = S // A_BLOCK
    qb = q.reshape(B, nb, A_BLOCK, A_KV_HEADS, A_GROUP, HEAD_DIM)
    pad = ((0, 0), (A_WINDOW, A_WINDOW), (0, 0), (0, 0))

    def band(t):
        tp = jnp.pad(t, pad).reshape(B, nb + 2, A_BLOCK, A_KV_HEADS, HEAD_DIM)
        return jnp.concatenate([tp[:, :-2], tp[:, 1:-1], tp[:, 2:]], axis=2)

    kb, vb = band(k), band(v)
    s = jnp.einsum('bnikgd,bnjkd->bnkgij', qb, kb).astype(F32) * (HEAD_DIM ** -0.5)
    i = jnp.arange(A_BLOCK)[:, None]
    j = jnp.arange(3 * A_BLOCK)[None, :]
    rel = j - A_WINDOW - i
    bias = t5_bias.astype(F32)[t5_bucket(rel)]
    bias = bias.transpose(2, 0, 1).reshape(A_KV_HEADS, A_GROUP, A_BLOCK, 3 * A_BLOCK)
    kpos = jnp.arange(nb)[:, None, None] * A_BLOCK - A_WINDOW + j[None]
    ok = (jnp.abs(rel) <= A_WINDOW)[None] & (kpos >= 0) & (kpos < S)
    s = jnp.where(ok[None, :, None, None], s + bias, NEG_INF)
    sk = sink.astype(F32).reshape(A_KV_HEADS, A_GROUP)[None, None, :, :, None, None]
    m = jnp.maximum(s.max(axis=-1, keepdims=True), sk)
    pr = jnp.exp(s - m)
    pr = pr / (pr.sum(axis=-1, keepdims=True) + jnp.exp(sk - m))
    o = jnp.einsum('bnkgij,bnjkd->bnikgd', pr.astype(v.dtype), vb)
    return o.reshape(B, S, A_WIDTH)


def neighbourhood_attention(q, k, v, rpb):
    B, S = q.shape[0], q.shape[1]
    rows = S // GRID_W
    kh = min(NA_WIN_H, rows)
    kw = NA_WIN_W
    r = jnp.arange(rows)
    row_idx = jnp.clip(r - kh // 2, 0, rows - kh)[:, None] + jnp.arange(kh)[None]
    c = jnp.arange(GRID_W)
    cs = jnp.clip(c - kw // 2, 0, GRID_W - kw)
    col_ok = (c[None] >= cs[:, None]) & (c[None] < cs[:, None] + kw)
    row_off = row_idx - r[:, None] + NA_WIN_H - 1
    col_off = jnp.clip(c[None] - c[:, None], -(kw - 1), kw - 1) + kw - 1
    bias = rpb.astype(F32)[:, row_off[:, None, :, None], col_off[None, :, None, :]]
    bias = bias.transpose(1, 0, 2, 3, 4)
    q5 = q.reshape(B, rows, GRID_W, B_HEADS, HEAD_DIM)
    kg = k.reshape(B, rows, GRID_W, B_HEADS, HEAD_DIM)[:, row_idx]
    vg = v.reshape(B, rows, GRID_W, B_HEADS, HEAD_DIM)[:, row_idx]
    s = jnp.einsum('brqhd,brikhd->brhqik', q5, kg).astype(F32) * (HEAD_DIM ** -0.5) + bias[None]
    s = jnp.where(col_ok[:, None, :], s, NEG_INF)
    pr = jax.nn.softmax(s.reshape(B, rows, B_HEADS, GRID_W, kh * GRID_W), axis=-1).reshape(s.shape)
    o = jnp.einsum('brhqik,brikhd->brqhd', pr.astype(v.dtype), vg)
    return o.reshape(B, S, B_WIDTH)


def hyena_filters(L, fw1, fb1, fw2, fb2, freq, fw3):
    t = jnp.linspace(0.0, 1.0, L, dtype=F32)[:, None]
    bands = (HYENA_EMB - 1) // 2
    w = (2.0 * math.pi / L) * jnp.arange(L, dtype=F32)[:, None]
    f = jnp.linspace(1e-4, bands - 1, bands, dtype=F32)[None]
    z = jnp.concatenate([t, jnp.cos(f * w), -jnp.sin(f * w)], axis=-1)
    fr = freq.astype(F32)
    hid = jnp.sin(fr * (z @ fw1.astype(F32) + fb1.astype(F32)))
    hid = jnp.sin(fr * (hid @ fw2.astype(F32) + fb2.astype(F32)))
    h = (hid @ fw3.astype(F32)).reshape(L, 2, HYENA_ORDER, HYENA_WIDTH)
    max_decay = math.log(HYENA_DECAY_TARGET) / HYENA_FAST_PCT
    min_decay = math.log(HYENA_DECAY_TARGET) / HYENA_SLOW_PCT
    deltas = jnp.linspace(min_decay, max_decay, HYENA_WIDTH, dtype=F32)
    decay = jnp.exp(-t * jnp.abs(deltas)[None])
    return h * decay[:, None, None, :]


def hyena_mixer(u, w_in, b_in, conv_w, conv_b, fw1, fb1, fw2, fb2, freq, fw3, fbias, w_out):
    B, L, _ = u.shape
    proj = u @ w_in + b_in
    pp = jnp.pad(proj, ((0, 0), (1, 1), (0, 0)))
    sc = pp[:, :-2] * conv_w[0] + pp[:, 1:-1] * conv_w[1] + pp[:, 2:] * conv_w[2] + conv_b
    x1, x2, z = jnp.split(sc, 3, axis=-1)
    h = hyena_filters(L, fw1, fb1, fw2, fb2, freq, fw3)
    two_sided = jnp.concatenate([h[:1, 0] + h[:1, 1], h[1:, 0],
                                 jnp.zeros((1, HYENA_ORDER, HYENA_WIDTH), F32), h[:0:-1, 1]], axis=0)
    hf = jnp.fft.rfft(two_sided, axis=0)
    zf = z.astype(F32)
    for o, gate in enumerate((x1, x2)):
        conv = jnp.fft.irfft(jnp.fft.rfft(zf, n=2 * L, axis=1) * hf[:, o][None], n=2 * L, axis=1)[:, :L]
        zf = gate.astype(F32) * (conv + zf * fbias[o].astype(F32))
    return zf.astype(u.dtype) @ w_out


def swiglu(x, wg, wu, wd):
    return (jax.nn.silu(x @ wg) * (x @ wu)) @ wd


def moe_swiglu(x2, w_router, wg, wu, wd):
    N, D = x2.shape
    logits = (x2 @ w_router).astype(F32)
    top_v, top_i = lax.top_k(logits, TOP_K)
    gates = jax.nn.softmax(top_v, axis=-1)
    e_flat = top_i.reshape(-1)
    tok_flat = jnp.repeat(jnp.arange(N, dtype=jnp.int32), TOP_K)
    g_flat = gates.reshape(-1)
    order = jnp.argsort(e_flat)
    e_s, tok_s, g_s = e_flat[order], tok_flat[order], g_flat[order]
    counts = jnp.bincount(e_flat, length=N_EXPERTS)
    padded = ((counts + MOE_BLOCK - 1) // MOE_BLOCK) * MOE_BLOCK
    pad_end = jnp.cumsum(padded)
    pad_start = pad_end - padded
    start = jnp.cumsum(counts) - counts
    nk = N * TOP_K
    dest = pad_start[e_s] + jnp.arange(nk) - start[e_s]
    P = ((nk + MOE_BLOCK - 1) // MOE_BLOCK) * MOE_BLOCK + N_EXPERTS * MOE_BLOCK
    row_tok = jnp.full((P,), N, jnp.int32).at[dest].set(tok_s)
    row_gate = jnp.zeros((P,), F32).at[dest].set(g_s)
    nblk = P // MOE_BLOCK
    blk_exp = jnp.minimum(jnp.searchsorted(pad_end, jnp.arange(nblk) * MOE_BLOCK, side='right'),
                          N_EXPERTS - 1)
    x_ext = jnp.concatenate([x2, jnp.zeros((1, D), x2.dtype)], axis=0)
    xs = x_ext[row_tok].reshape(nblk, MOE_BLOCK, D)

    def expert_block(args):
        xb, e = args
        return (jax.nn.silu(xb @ wg[e]) * (xb @ wu[e])) @ wd[e]

    ys = lax.map(expert_block, (xs, blk_exp)).reshape(P, D)
    out = jnp.zeros((N + 1, D), ys.dtype).at[row_tok].add(ys * row_gate[:, None].astype(ys.dtype))
    return out[:N]


def setup_inputs(seed: int = 0) -> dict:
    key = jax.random.key(seed)
    ks = iter(jax.random.split(key, 40))

    def nrm(shape, scale):
        return jax.random.normal(next(ks), shape, F32) * scale

    ne = (DEPTH + 1) // 2
    no = DEPTH // 2
    D = D_MODEL
    W = HYENA_WIDTH
    FH = HYENA_FILTER_HIDDEN
    return {
        'x': nrm((BATCH, SEQ, D), 1.0),
        'p': nrm((DEPTH, BATCH, SEQ, PLE_DIM), 1.0),
        'ln_mix': 1.0 + nrm((DEPTH, D), 0.05),
        'ln_ffn': 1.0 + nrm((DEPTH, D), 0.05),
        'ln_ple': 1.0 + nrm((DEPTH, D), 0.05),
        'final_norm': 1.0 + nrm((D,), 0.05),
        't5_bias': nrm((T5_BUCKETS, A_Q_HEADS), 0.5),
        'w_attn_in': nrm((ne, D, ATTN_IN), D ** -0.5),
        'w_attn_out': nrm((ne, MIX_WIDTH, D), MIX_WIDTH ** -0.5),
        'attn_sink': nrm((ne, A_Q_HEADS), 0.5),
        'na_rpb': nrm((ne, B_HEADS, 2 * NA_WIN_H - 1, 2 * NA_WIN_W - 1), 0.5),
        'w_ffn_gate': nrm((ne, D, D_FF), D ** -0.5),
        'w_ffn_up': nrm((ne, D, D_FF), D ** -0.5),
        'w_ffn_down': nrm((ne, D_FF, D), D_FF ** -0.5),
        'w_hy_in': nrm((no, D, 3 * W), D ** -0.5),
        'b_hy_in': nrm((no, 3 * W), 0.02),
        'w_hy_conv': nrm((no, HYENA_SHORT, 3 * W), HYENA_SHORT ** -0.5),
        'b_hy_conv': nrm((no, 3 * W), 0.02),
        'w_hy_f1': nrm((no, HYENA_EMB, FH), HYENA_EMB ** -0.5),
        'b_hy_f1': nrm((no, FH), 0.1),
        'w_hy_f2': nrm((no, FH, FH), FH ** -0.5),
        'b_hy_f2': nrm((no, FH), 0.1),
        'hy_freq': 1.0 + nrm((no, FH), 0.1),
        'w_hy_f3': nrm((no, FH, 2 * HYENA_ORDER * W), 0.05 * FH ** -0.5),
        'hy_bias': nrm((no, HYENA_ORDER, W), 0.5),
        'w_hy_out': nrm((no, W, D), W ** -0.5),
        'w_router': nrm((no, D, N_EXPERTS), D ** -0.5),
        'w_exp_gate': nrm((no, N_EXPERTS, D, D_FF_EXPERT), D ** -0.5),
        'w_exp_up': nrm((no, N_EXPERTS, D, D_FF_EXPERT), D ** -0.5),
        'w_exp_down': nrm((no, N_EXPERTS, D_FF_EXPERT, D), D_FF_EXPERT ** -0.5),
        'w_ple_proj': nrm((DEPTH, PLE_DIM, D), PLE_DIM ** -0.5),
        'w_ple_gate': nrm((DEPTH, D, D), D ** -0.5),
    }


def reference(x, p, ln_mix, ln_ffn, ln_ple, final_norm, t5_bias, w_attn_in, w_attn_out,
              attn_sink, na_rpb, w_ffn_gate, w_ffn_up, w_ffn_down, w_hy_in, b_hy_in,
              w_hy_conv, b_hy_conv, w_hy_f1, b_hy_f1, w_hy_f2, b_hy_f2, hy_freq, w_hy_f3,
              hy_bias, w_hy_out, w_router, w_exp_gate, w_exp_up, w_exp_down,
              w_ple_proj, w_ple_gate):
    B, S, D = x.shape
    o1 = A_WIDTH
    o2 = o1 + A_KV_WIDTH
    o3 = o2 + A_KV_WIDTH
    o4 = o3 + B_WIDTH
    o5 = o4 + B_WIDTH
    h = x
    for i in range(DEPTH):
        li = i // 2
        hn = rmsnorm(h, ln_mix[i])
        if i % 2 == 0:
            proj = hn @ w_attn_in[li]
            qa, ka, va, qn, kn, vn = jnp.split(proj, [o1, o2, o3, o4, o5], axis=-1)
            oa = window_gqa(qa.reshape(B, S, A_Q_HEADS, HEAD_DIM),
                            ka.reshape(B, S, A_KV_HEADS, HEAD_DIM),
                            va.reshape(B, S, A_KV_HEADS, HEAD_DIM), t5_bias, attn_sink[li])
            ob = neighbourhood_attention(qn.reshape(B, S, B_HEADS, HEAD_DIM),
                                         kn.reshape(B, S, B_HEADS, HEAD_DIM),
                                         vn.reshape(B, S, B_HEADS, HEAD_DIM), na_rpb[li])
            h = h + jnp.concatenate([oa, ob], axis=-1) @ w_attn_out[li]
            h = h + swiglu(rmsnorm(h, ln_ffn[i]), w_ffn_gate[li], w_ffn_up[li], w_ffn_down[li])
        else:
            h = h + hyena_mixer(hn, w_hy_in[li], b_hy_in[li], w_hy_conv[li], b_hy_conv[li],
                                w_hy_f1[li], b_hy_f1[li], w_hy_f2[li], b_hy_f2[li], hy_freq[li],
                                w_hy_f3[li], hy_bias[li], w_hy_out[li])
            hn2 = rmsnorm(h, ln_ffn[i]).reshape(B * S, D)
            h = h + moe_swiglu(hn2, w_router[li], w_exp_gate[li], w_exp_up[li],
                               w_exp_down[li]).reshape(B, S, D)
        gate = jax.nn.sigmoid(rmsnorm(h, ln_ple[i]) @ w_ple_gate[i])
        h = h + gate * (p[i] @ w_ple_proj[i])
    return rmsnorm(h, final_norm)
```

```python
import functools
import math

import jax
import jax.numpy as jnp
import numpy as np
from jax import lax
from jax.experimental import pallas as pl
from jax.experimental.pallas import tpu as pltpu

F32 = jnp.float32
BF16 = jnp.bfloat16
NEG_INF = -1e30
RMS_EPS = 1e-6

V7X_VMEM_LIMIT_BYTES = 56 * 1024 * 1024

HEAD_DIM = 64
A_Q_HEADS = 16
A_KV_HEADS = 2
A_GROUP = A_Q_HEADS // A_KV_HEADS
A_BLOCK = 128
T5_BUCKETS = 32
T5_MAX_DIST = 128
B_HEADS = 16
GRID_W = 64
NA_WIN_H = 8
NA_WIN_W = 16
A_WIDTH = A_Q_HEADS * HEAD_DIM
A_KV_WIDTH = A_KV_HEADS * HEAD_DIM
B_WIDTH = B_HEADS * HEAD_DIM
N_EXPERTS = 8
TOP_K = 2
HYENA_EMB = 33
HYENA_DECAY_TARGET = 1e-2
HYENA_FAST_PCT = 0.3
HYENA_SLOW_PCT = 1.5

FFT_N1 = 64
FFT_N2 = 128
FFT_CHUNK = 8

MOE_TILE = 1024
FFN_F_TILE = 256


def _params(semantics, big=False):
    return pltpu.CompilerParams(
        dimension_semantics=semantics,
        vmem_limit_bytes=V7X_VMEM_LIMIT_BYTES if big else None)


def _rmsnorm_kernel(x_ref, g_ref, o_ref):
    x = x_ref[...]
    y = x * lax.rsqrt(jnp.mean(x * x, axis=-1, keepdims=True) + RMS_EPS)
    o_ref[...] = (y * g_ref[...]).astype(o_ref.dtype)


def rmsnorm(x2, g, out_dtype, tm=512):
    n, d = x2.shape
    return pl.pallas_call(
        _rmsnorm_kernel,
        out_shape=jax.ShapeDtypeStruct((n, d), out_dtype),
        grid=(n // tm,),
        in_specs=[pl.BlockSpec((tm, d), lambda i: (i, 0)),
                  pl.BlockSpec((1, d), lambda i: (0, 0))],
        out_specs=pl.BlockSpec((tm, d), lambda i: (i, 0)),
        compiler_params=_params(("parallel",)),
        name="rmsnorm",
    )(x2, g.reshape(1, d))


def _add_rmsnorm_kernel(x_ref, d_ref, g_ref, h_ref, o_ref):
    x = x_ref[...] + d_ref[...]
    h_ref[...] = x
    y = x * lax.rsqrt(jnp.mean(x * x, axis=-1, keepdims=True) + RMS_EPS)
    o_ref[...] = (y * g_ref[...]).astype(o_ref.dtype)


def add_rmsnorm(x2, delta, g, tm=512):
    n, d = x2.shape
    return pl.pallas_call(
        _add_rmsnorm_kernel,
        out_shape=(jax.ShapeDtypeStruct((n, d), F32), jax.ShapeDtypeStruct((n, d), BF16)),
        grid=(n // tm,),
        in_specs=[pl.BlockSpec((tm, d), lambda i: (i, 0)),
                  pl.BlockSpec((tm, d), lambda i: (i, 0)),
                  pl.BlockSpec((1, d), lambda i: (0, 0))],
        out_specs=(pl.BlockSpec((tm, d), lambda i: (i, 0)),
                   pl.BlockSpec((tm, d), lambda i: (i, 0))),
        compiler_params=_params(("parallel",)),
        name="add_rmsnorm",
    )(x2, delta, g.reshape(1, d))


def _rmsnorm_router_kernel(x_ref, g_ref, wr_ref, o_ref, l_ref):
    x = x_ref[...]
    y = x * lax.rsqrt(jnp.mean(x * x, axis=-1, keepdims=True) + RMS_EPS)
    y = y * g_ref[...]
    o_ref[...] = y.astype(o_ref.dtype)
    l_ref[...] = jnp.dot(y, wr_ref[...], preferred_element_type=F32,
                         precision=lax.Precision.HIGHEST)


def rmsnorm_router(x2, g, w_router_pad, tm=512):
    n, d = x2.shape
    ne = w_router_pad.shape[1]
    return pl.pallas_call(
        _rmsnorm_router_kernel,
        out_shape=(jax.ShapeDtypeStruct((n, d), BF16), jax.ShapeDtypeStruct((n, ne), F32)),
        grid=(n // tm,),
        in_specs=[pl.BlockSpec((tm, d), lambda i: (i, 0)),
                  pl.BlockSpec((1, d), lambda i: (0, 0)),
                  pl.BlockSpec((d, ne), lambda i: (0, 0))],
        out_specs=(pl.BlockSpec((tm, d), lambda i: (i, 0)),
                   pl.BlockSpec((tm, ne), lambda i: (i, 0))),
        compiler_params=_params(("parallel",)),
        name="rmsnorm_router",
    )(x2, g.reshape(1, d), w_router_pad)


def _mm_kernel(*refs, n_a, has_bias, has_res):
    a_refs = refs[:n_a]
    w_refs = refs[n_a:2 * n_a]
    idx = 2 * n_a
    bias_ref = refs[idx] if has_bias else None
    idx += int(has_bias)
    res_ref = refs[idx] if has_res else None
    idx += int(has_res)
    o_ref = refs[idx]
    wbf_refs = refs[idx + 1:idx + 1 + n_a]

    @pl.when(pl.program_id(1) == 0)
    def _():
        for w_ref, wbf_ref in zip(w_refs, wbf_refs):
            wbf_ref[...] = w_ref[...].astype(BF16)

    acc = None
    for a_ref, wbf_ref in zip(a_refs, wbf_refs):
        d = jnp.dot(a_ref[...].astype(BF16), wbf_ref[...], preferred_element_type=F32)
        acc = d if acc is None else acc + d
    if has_bias:
        acc = acc + bias_ref[...]
    if has_res:
        acc = acc + res_ref[...]
    o_ref[...] = acc.astype(o_ref.dtype)


def matmul(a_list, w, *, bias=None, res=None, out_dtype=F32, tm=512, tn=512, name="matmul"):
    m = a_list[0].shape[0]
    n = w.shape[1]
    n_a = len(a_list)
    k_each = a_list[0].shape[1]
    assert all(a.shape == (m, k_each) for a in a_list) and w.shape[0] == n_a * k_each
    in_specs = [pl.BlockSpec((tm, k_each), lambda j, i: (i, 0)) for _ in a_list]
    in_specs += [pl.BlockSpec((k_each, tn), functools.partial(lambda j, i, kb: (kb, j), kb=kb))
                 for kb in range(n_a)]
    args = list(a_list) + [w] * n_a
    if bias is not None:
        in_specs.append(pl.BlockSpec((1, tn), lambda j, i: (0, j)))
        args.append(bias.reshape(1, n))
    if res is not None:
        in_specs.append(pl.BlockSpec((tm, tn), lambda j, i: (i, j)))
        args.append(res)
    return pl.pallas_call(
        functools.partial(_mm_kernel, n_a=n_a, has_bias=bias is not None, has_res=res is not None),
        out_shape=jax.ShapeDtypeStruct((m, n), out_dtype),
        grid=(n // tn, m // tm),
        in_specs=in_specs,
        out_specs=pl.BlockSpec((tm, tn), lambda j, i: (i, j)),
        scratch_shapes=[pltpu.VMEM((k_each, tn), BF16) for _ in a_list],
        compiler_params=_params(("parallel", "arbitrary"), big=True),
        name=name,
    )(*args)


def _ple_kernel(hn_ref, p_ref, h_ref, wg_ref, wp_ref, o_ref, wg_bf, wp_bf):
    @pl.when(pl.program_id(1) == 0)
    def _():
        wg_bf[...] = wg_ref[...].astype(BF16)
        wp_bf[...] = wp_ref[...].astype(BF16)

    a = jnp.dot(hn_ref[...], wg_bf[...], preferred_element_type=F32)
    pp = jnp.dot(p_ref[...].astype(BF16), wp_bf[...], preferred_element_type=F32)
    o_ref[...] = h_ref[...] + jax.nn.sigmoid(a) * pp


def ple(hn, p2, h2, w_gate, w_proj, tm=512, tn=512):
    m, d = h2.shape
    pd = p2.shape[1]
    return pl.pallas_call(
        _ple_kernel,
        out_shape=jax.ShapeDtypeStruct((m, d), F32),
        grid=(d // tn, m // tm),
        in_specs=[pl.BlockSpec((tm, d), lambda j, i: (i, 0)),
                  pl.BlockSpec((tm, pd), lambda j, i: (i, 0)),
                  pl.BlockSpec((tm, tn), lambda j, i: (i, j)),
                  pl.BlockSpec((d, tn), lambda j, i: (0, j)),
                  pl.BlockSpec((pd, tn), lambda j, i: (0, j))],
        out_specs=pl.BlockSpec((tm, tn), lambda j, i: (i, j)),
        scratch_shapes=[pltpu.VMEM((d, tn), BF16), pltpu.VMEM((pd, tn), BF16)],
        compiler_params=_params(("parallel", "arbitrary"), big=True),
        name="ple",
    )(hn, p2, h2, w_gate, w_proj)


def _swiglu_kernel(exp_ref, used_ref, x_ref, wg_ref, wu_ref, wd_ref, gate_ref, o_ref, *, n_col):
    del exp_ref
    t = pl.program_id(0)
    f = pl.program_id(1)
    used = used_ref[t] > 0

    @pl.when(f == 0)
    def _():
        o_ref[...] = jnp.zeros_like(o_ref)

    @pl.when(used)
    def _():
        x = x_ref[...]
        g = jnp.dot(x, wg_ref[...].astype(BF16), preferred_element_type=F32)
        u = jnp.dot(x, wu_ref[...].astype(BF16), preferred_element_type=F32)
        hmid = (g * jax.nn.sigmoid(g) * u).astype(BF16)
        wd = wd_ref[...].astype(BF16)
        cw = o_ref.shape[1] // n_col
        for c in range(n_col):
            y = jnp.dot(hmid, wd[:, c * cw:(c + 1) * cw], preferred_element_type=F32)
            o_ref[:, c * cw:(c + 1) * cw] += y

    @pl.when(jnp.logical_and(used, f == pl.num_programs(1) - 1))
    def _():
        o_ref[...] = o_ref[...] * gate_ref[...]


def swiglu(x, w_gate, w_up, w_down, tile_exp, tile_used, row_gate, *, tm=MOE_TILE, tf=FFN_F_TILE,
           name="swiglu"):
    m, d = x.shape
    dff = w_gate.shape[2]
    nf = dff // tf
    nt = m // tm

    def w_in_map(t, f, exp_ref, used_ref):
        return (exp_ref[t], 0, jnp.where(used_ref[t] > 0, f, nf - 1))

    def w_out_map(t, f, exp_ref, used_ref):
        return (exp_ref[t], jnp.where(used_ref[t] > 0, f, nf - 1), 0)

    grid_spec = pltpu.PrefetchScalarGridSpec(
        num_scalar_prefetch=2,
        grid=(nt, nf),
        in_specs=[pl.BlockSpec((tm, d), lambda t, f, e, u: (t, 0)),
                  pl.BlockSpec((None, d, tf), w_in_map),
                  pl.BlockSpec((None, d, tf), w_in_map),
                  pl.BlockSpec((None, tf, d), w_out_map),
                  pl.BlockSpec((tm, 1), lambda t, f, e, u: (t, 0))],
        out_specs=pl.BlockSpec((tm, d), lambda t, f, e, u: (t, 0)),
    )
    return pl.pallas_call(
        functools.partial(_swiglu_kernel, n_col=4),
        out_shape=jax.ShapeDtypeStruct((m, d), F32),
        grid_spec=grid_spec,
        compiler_params=_params(("parallel", "arbitrary"), big=True),
        name=name,
    )(tile_exp, tile_used, x, w_gate, w_up, w_down, row_gate)


def _t5_bucket(rel):
    half = T5_BUCKETS // 2
    max_exact = half // 2
    n = jnp.abs(rel)
    log_ratio = jnp.log(jnp.maximum(n, 1).astype(F32) / max_exact) / math.log(T5_MAX_DIST / max_exact)
    large = jnp.minimum(max_exact + (log_ratio * (half - max_exact)).astype(jnp.int32), half - 1)
    return jnp.where(rel > 0, half, 0) + jnp.where(n < max_exact, n, large)


def _window_bias_table(t5_bias):
    i = jnp.arange(A_BLOCK)[:, None]
    j = jnp.arange(3 * A_BLOCK)[None, :]
    rel = j - A_BLOCK - i
    bias = t5_bias.astype(F32)[_t5_bucket(rel)].transpose(2, 0, 1)
    return jnp.where((jnp.abs(rel) <= A_BLOCK)[None], bias, NEG_INF)


def _window_kernel(sink_ref, q_ref, kv_ref, bias_ref, o_ref, *, nb):
    n = pl.program_id(1)
    scale = HEAD_DIM ** -0.5
    starts = (jnp.maximum(n - 1, 0), n, jnp.minimum(n + 1, nb - 1))
    kv = [kv_ref[pl.ds(pl.multiple_of(s * A_BLOCK, A_BLOCK), A_BLOCK), :] for s in starts]
    col = lax.broadcasted_iota(jnp.int32, (A_BLOCK, 3 * A_BLOCK), 1)
    edge_ok = jnp.logical_and(jnp.logical_or(n > 0, col >= A_BLOCK),
                              jnp.logical_or(n < nb - 1, col < 2 * A_BLOCK))
    q = q_ref[...]
    for kh in range(A_KV_HEADS):
        ks = [blk[:, kh * HEAD_DIM:(kh + 1) * HEAD_DIM] for blk in kv]
        v = jnp.concatenate(
            [blk[:, A_KV_WIDTH + kh * HEAD_DIM:A_KV_WIDTH + (kh + 1) * HEAD_DIM] for blk in kv], axis=0)
        for g in range(A_GROUP):
            h = kh * A_GROUP + g
            qh = q[:, h * HEAD_DIM:(h + 1) * HEAD_DIM]
            s = jnp.concatenate(
                [lax.dot_general(qh, kb, (((1,), (1,)), ((), ())), preferred_element_type=F32)
                 for kb in ks], axis=1)
            s = jnp.where(edge_ok, s * scale + bias_ref[h], NEG_INF)
            sink = sink_ref[h]
            mx = jnp.maximum(jnp.max(s, axis=-1, keepdims=True), sink)
            p = jnp.exp(s - mx)
            den = jnp.sum(p, axis=-1, keepdims=True) + jnp.exp(sink - mx)
            o = jnp.dot(p.astype(BF16), v, preferred_element_type=F32) / den
            o_ref[:, h * HEAD_DIM:(h + 1) * HEAD_DIM] = o.astype(o_ref.dtype)


def window_attention(proj_a, t5_bias, sink):
    b, s, _ = proj_a.shape
    nb = s // A_BLOCK
    kvw = 2 * A_KV_WIDTH
    grid_spec = pltpu.PrefetchScalarGridSpec(
        num_scalar_prefetch=0,
        grid=(b, nb),
        in_specs=[pl.BlockSpec(memory_space=pltpu.SMEM),
                  pl.BlockSpec((None, A_BLOCK, A_WIDTH), lambda bi, n: (bi, n, 0)),
                  pl.BlockSpec((None, s, kvw), lambda bi, n: (bi, 0, A_WIDTH // kvw)),
                  pl.BlockSpec((A_Q_HEADS, A_BLOCK, 3 * A_BLOCK), lambda bi, n: (0, 0, 0))],
        out_specs=pl.BlockSpec((None, A_BLOCK, A_WIDTH), lambda bi, n: (bi, n, 0)),
    )
    return pl.pallas_call(
        functools.partial(_window_kernel, nb=nb),
        out_shape=jax.ShapeDtypeStruct((b, s, A_WIDTH), BF16),
        grid_spec=grid_spec,
        compiler_params=_params(("parallel", "arbitrary")),
        name="window_attention",
    )(sink.astype(F32), proj_a, proj_a, _window_bias_table(t5_bias))


NA_HEAD_GROUP = 4


def _na_bias_table(rpb, rows):
    kh = min(NA_WIN_H, rows)
    kw = NA_WIN_W
    c = jnp.arange(GRID_W)
    cs = jnp.clip(c - kw // 2, 0, GRID_W - kw)
    col_ok = (c[None] >= cs[:, None]) & (c[None] < cs[:, None] + kw)
    col_off = jnp.clip(c[None] - c[:, None], -(kw - 1), kw - 1) + kw - 1
    d = jnp.arange(kh)[:, None]
    row_off = jnp.arange(kh)[None, :] - d + NA_WIN_H - 1
    bias = rpb.astype(F32)[:, row_off[:, None, :, None], col_off[None, :, None, :]]
    bias = jnp.where(col_ok[None, None, :, None, :], bias, NEG_INF)
    return bias.transpose(1, 0, 2, 3, 4).reshape(kh, rpb.shape[0], GRID_W, kh * GRID_W)


def _na_kernel(q_ref, k_ref, v_ref, bias_ref, o_ref, *, rows):
    r = pl.program_id(2)
    kh = min(NA_WIN_H, rows)
    scale = HEAD_DIM ** -0.5
    start = pl.multiple_of(jnp.clip(r - kh // 2, 0, rows - kh) * GRID_W, GRID_W)
    k = k_ref[pl.ds(start, kh * GRID_W), :]
    v = v_ref[pl.ds(start, kh * GRID_W), :]
    q = q_ref[...]
    for h in range(NA_HEAD_GROUP):
        sl = slice(h * HEAD_DIM, (h + 1) * HEAD_DIM)
        s = lax.dot_general(q[:, sl], k[:, sl], (((1,), (1,)), ((), ())), preferred_element_type=F32)
        s = s * scale + bias_ref[h]
        mx = jnp.max(s, axis=-1, keepdims=True)
        p = jnp.exp(s - mx)
        den = jnp.sum(p, axis=-1, keepdims=True)
        o = jnp.dot(p.astype(BF16), v[:, sl], preferred_element_type=F32) / den
        o_ref[:, sl] = o.astype(o_ref.dtype)


def neighbourhood_attention(proj_n, rpb):
    b, s, _ = proj_n.shape
    rows = s // GRID_W
    kh = min(NA_WIN_H, rows)
    gw = NA_HEAD_GROUP * HEAD_DIM
    ng = B_WIDTH // gw

    def bias_map(bi, g, r):
        return (r - jnp.clip(r - kh // 2, 0, rows - kh), g, 0, 0)

    return pl.pallas_call(
        functools.partial(_na_kernel, rows=rows),
        out_shape=jax.ShapeDtypeStruct((b, s, B_WIDTH), BF16),
        grid=(b, ng, rows),
        in_specs=[pl.BlockSpec((None, GRID_W, gw), lambda bi, g, r: (bi, r, g)),
                  pl.BlockSpec((None, s, gw), lambda bi, g, r: (bi, 0, ng + g)),
                  pl.BlockSpec((None, s, gw), lambda bi, g, r: (bi, 0, 2 * ng + g)),
                  pl.BlockSpec((None, NA_HEAD_GROUP, GRID_W, kh * GRID_W), bias_map)],
        out_specs=pl.BlockSpec((None, GRID_W, gw), lambda bi, g, r: (bi, r, g)),
        compiler_params=_params(("parallel", "parallel", "arbitrary")),
        name="neighbourhood_attention",
    )(proj_n, proj_n, proj_n, _na_bias_table(rpb, rows))


def _short_conv_kernel(prev_ref, cur_ref, next_ref, w_ref, b_ref, o_ref):
    i = pl.program_id(1)
    nt = pl.num_programs(1)
    x = cur_ref[...]
    tl = x.shape[0]
    row = lax.broadcasted_iota(jnp.int32, x.shape, 0)
    prev_row = jnp.where(i > 0, prev_ref[7:8, :], 0.0)
    next_row = jnp.where(i < nt - 1, next_ref[0:1, :], 0.0)
    up = jnp.where(row == 0, prev_row, pltpu.roll(x, 1, 0))
    dn = jnp.where(row == tl - 1, next_row, pltpu.roll(x, tl - 1, 0))
    o_ref[...] = up * w_ref[0:1, :] + x * w_ref[1:2, :] + dn * w_ref[2:3, :] + b_ref[...]


def short_conv(proj, conv_w, conv_b, tl=512, tc=1024):
    b, l, c = proj.shape
    nl8 = l // 8
    return pl.pallas_call(
        _short_conv_kernel,
        out_shape=jax.ShapeDtypeStruct((b, l, c), F32),
        grid=(b, l // tl, c // tc),
        in_specs=[pl.BlockSpec((None, 8, tc), lambda bi, i, j: (bi, jnp.maximum(i * (tl // 8) - 1, 0), j)),
                  pl.BlockSpec((None, tl, tc), lambda bi, i, j: (bi, i, j)),
                  pl.BlockSpec((None, 8, tc), lambda bi, i, j: (bi, jnp.minimum((i + 1) * (tl // 8), nl8 - 1), j)),
                  pl.BlockSpec((3, tc), lambda bi, i, j: (0, j)),
                  pl.BlockSpec((1, tc), lambda bi, i, j: (0, j))],
        out_specs=pl.BlockSpec((None, tl, tc), lambda bi, i, j: (bi, i, j)),
        compiler_params=_params(("parallel", "parallel", "parallel")),
        name="short_conv",
    )(proj, proj, proj, conv_w, conv_b.reshape(1, c))


def _filter_kernel(z_ref, t_ref, fw1_ref, fb1_ref, fw2_ref, fb2_ref, fr_ref, w3_ref, w3b_ref,
                   delta_ref, o_ref, *, seq):
    m = pl.program_id(1)
    hi = lax.Precision.HIGHEST
    fr = fr_ref[...]
    hid = jnp.sin(fr * (jnp.dot(z_ref[...], fw1_ref[...], preferred_element_type=F32, precision=hi)
                        + fb1_ref[...]))
    hid = jnp.sin(fr * (jnp.dot(hid, fw2_ref[...], preferred_element_type=F32, precision=hi)
                        + fb2_ref[...]))
    hid_bf = hid.astype(BF16)
    h = jnp.dot(hid_bf, w3_ref[...].astype(BF16), preferred_element_type=F32)
    tm = h.shape[0]
    row = lax.broadcasted_iota(jnp.int32, h.shape, 0) + m * tm
    h_back = jnp.dot(hid_bf, w3b_ref[...].astype(BF16), preferred_element_type=F32)
    h = h + jnp.where(row == 0, h_back, 0.0)
    decay = jnp.exp(-t_ref[...] * delta_ref[...])
    o_ref[...] = jnp.where(row == seq, 0.0, h * decay)


def hyena_two_sided_filters(seq, fw1, fb1, fw2, fb2, freq, fw3, width, tm=512):
    n = 2 * seq
    u = jnp.arange(n)
    pos = jnp.where(u <= seq, u, n - u)
    pos = jnp.where(u == seq, 0, pos)
    t_all = jnp.linspace(0.0, 1.0, seq, dtype=F32)
    bands = (HYENA_EMB - 1) // 2
    w_all = (2.0 * math.pi / seq) * jnp.arange(seq, dtype=F32)
    f = jnp.linspace(1e-4, bands - 1, bands, dtype=F32)[None]
    t = t_all[pos][:, None]
    w = w_all[pos][:, None]
    z = jnp.concatenate([t, jnp.cos(f * w), -jnp.sin(f * w)], axis=-1)
    emb_pad = 40
    z = jnp.pad(z, ((0, 0), (0, emb_pad - HYENA_EMB)))
    fw1p = jnp.pad(fw1.astype(F32), ((0, emb_pad - HYENA_EMB), (0, 0)))
    fh = fw1.shape[1]
    max_decay = math.log(HYENA_DECAY_TARGET) / HYENA_FAST_PCT
    min_decay = math.log(HYENA_DECAY_TARGET) / HYENA_SLOW_PCT
    deltas = jnp.abs(jnp.linspace(min_decay, max_decay, width, dtype=F32))[None]
    half_tiles = seq // tm

    def w3_map(o, m):
        return (0, jnp.where(m >= half_tiles, 2, 0) + o)

    return pl.pallas_call(
        functools.partial(_filter_kernel, seq=seq),
        out_shape=jax.ShapeDtypeStruct((2, n, width), F32),
        grid=(2, n // tm),
        in_specs=[pl.BlockSpec((tm, emb_pad), lambda o, m: (m, 0)),
                  pl.BlockSpec((tm, 1), lambda o, m: (m, 0)),
                  pl.BlockSpec((emb_pad, fh), lambda o, m: (0, 0)),
                  pl.BlockSpec((1, fh), lambda o, m: (0, 0)),
                  pl.BlockSpec((fh, fh), lambda o, m: (0, 0)),
                  pl.BlockSpec((1, fh), lambda o, m: (0, 0)),
                  pl.BlockSpec((1, fh), lambda o, m: (0, 0)),
                  pl.BlockSpec((fh, width), w3_map),
                  pl.BlockSpec((fh, width), lambda o, m: (0, 2 + o)),
                  pl.BlockSpec((1, width), lambda o, m: (0, 0))],
        out_specs=pl.BlockSpec((None, tm, width), lambda o, m: (o, m, 0)),
        compiler_params=_params(("parallel", "arbitrary")),
        name="hyena_filters",
    )(z, t, fw1p, fb1.reshape(1, fh).astype(F32), fw2.astype(F32), fb2.reshape(1, fh).astype(F32),
      freq.reshape(1, fh).astype(F32), fw3, fw3, deltas)


def _dft_constants():
    n1, n2 = FFT_N1, FFT_N2
    n = n1 * n2
    a1 = np.arange(n1)
    a2 = np.arange(n2)
    half = n2 // 2

    def cplx_block(w):
        return np.block([[w.real, -w.imag], [w.imag, w.real]])

    ang = (a2[None, :, None] * a2[None, None, :] / n2) + (a1[:, None, None] * a2[None, :, None] / n)
    w1 = np.exp(-2j * np.pi * ang)
    f1_pad = np.stack([cplx_block(w1[i][:, :half]) for i in range(n1)])
    f1_real = np.stack([np.concatenate([w1[i].real, w1[i].imag], axis=0) for i in range(n1)])
    w2 = np.exp(-2j * np.pi * (a1[:, None] * a1[None, :]) / n1)
    f2 = cplx_block(w2)
    ang = (a1[None, :, None] * a1[None, None, :] / n1) + (a2[:, None, None] * a1[None, :, None] / n)
    g2 = np.stack([cplx_block(m_) for m_ in np.exp(2j * np.pi * ang)])
    wg1 = np.exp(2j * np.pi * (a2[:half, None] * a2[None, :]) / n2) / n
    g1 = cplx_block(wg1)
    to = lambda x: jnp.asarray(x.astype(np.float32)).astype(BF16)
    return to(f1_pad), to(f1_real), to(f2), to(g2), to(g1)


def _fft_stage1_kernel(x_ref, f_ref, y_ref):
    xr = pltpu.einshape("mjc->jmc", x_ref[0])
    xi = pltpu.einshape("mjc->jmc", x_ref[1])
    for j in range(FFT_CHUNK):
        xc = jnp.concatenate([xr[j], xi[j]], axis=0).astype(BF16)
        y_ref[j] = jnp.dot(f_ref[j], xc, preferred_element_type=F32)


def _fft_stage1_real_kernel(x_ref, f_ref, y_ref):
    x = pltpu.einshape("mjc->jmc", x_ref[...])
    for j in range(FFT_CHUNK):
        y_ref[j] = jnp.dot(f_ref[j], x[j].astype(BF16), preferred_element_type=F32)


def _fft_stage2_kernel(yr_ref, yi_ref, f2_ref, h_ref):
    yr = pltpu.einshape("nkc->knc", yr_ref[...])
    yi = pltpu.einshape("nkc->knc", yi_ref[...])
    for j in range(FFT_CHUNK):
        yc = jnp.concatenate([yr[j], yi[j]], axis=0).astype(BF16)
        h_ref[j] = jnp.dot(f2_ref[...], yc, preferred_element_type=F32)


def _fft_mid_kernel(yr_ref, yi_ref, f2_ref, h_ref, g2_ref, q_ref):
    yr = pltpu.einshape("nkc->knc", yr_ref[...])
    yi = pltpu.einshape("nkc->knc", yi_ref[...])
    n1 = FFT_N1
    for j in range(FFT_CHUNK):
        yc = jnp.concatenate([yr[j], yi[j]], axis=0).astype(BF16)
        z = jnp.dot(f2_ref[...], yc, preferred_element_type=F32)
        hf = h_ref[j]
        zr, zi, hr, hi = z[:n1], z[n1:], hf[:n1], hf[n1:]
        pc = jnp.concatenate([zr * hr - zi * hi, zr * hi + zi * hr], axis=0).astype(BF16)
        q_ref[j] = jnp.dot(g2_ref[j], pc, preferred_element_type=F32)


def _fft_last_kernel(qr_ref, qi_ref, g1_ref, gate_ref, zf_ref, fb_ref, o_ref):
    qr = pltpu.einshape("kjc->jkc", qr_ref[...])
    qi = pltpu.einshape("kjc->jkc", qi_ref[...])
    half = FFT_N2 // 2
    conv = []
    for j in range(FFT_CHUNK):
        qc = jnp.concatenate([qr[j], qi[j]], axis=0).astype(BF16)
        conv.append(jnp.dot(g1_ref[...], qc, preferred_element_type=F32))
    conv = jnp.stack(conv, axis=0)
    fb = fb_ref[...]
    for bi in range(2):
        cb = pltpu.einshape("jmc->mjc", conv[:, bi * half:(bi + 1) * half])
        zf = zf_ref[bi]
        o_ref[bi] = gate_ref[bi] * (cb + zf * fb)


def hyena_filter_spectrum(ts, consts, cb=512):
    _, f1_real, f2, _, _ = consts
    no, n, c = ts.shape
    n1, n2, ch = FFT_N1, FFT_N2, FFT_CHUNK
    ts4 = ts.reshape(no, n2, n1, c)
    y = pl.pallas_call(
        _fft_stage1_real_kernel,
        out_shape=jax.ShapeDtypeStruct((no, n1, 2 * n2, c), F32),
        grid=(no, n1 // ch, c // cb),
        in_specs=[pl.BlockSpec((None, n2, ch, cb), lambda o, j, k: (o, 0, j, k)),
                  pl.BlockSpec((ch, 2 * n2, n2), lambda o, j, k: (j, 0, 0))],
        out_specs=pl.BlockSpec((None, ch, 2 * n2, cb), lambda o, j, k: (o, j, 0, k)),
        compiler_params=_params(("parallel", "parallel", "parallel"), big=True),
        name="fft_filter_stage1",
    )(ts4, f1_real)
    nk = n2 // ch
    return pl.pallas_call(
        _fft_stage2_kernel,
        out_shape=jax.ShapeDtypeStruct((no, n2, 2 * n1, c), F32),
        grid=(no, nk, c // cb),
        in_specs=[pl.BlockSpec((None, n1, ch, cb), lambda o, j, k: (o, 0, j, k)),
                  pl.BlockSpec((None, n1, ch, cb), lambda o, j, k: (o, 0, nk + j, k)),
                  pl.BlockSpec((2 * n1, 2 * n1), lambda o, j, k: (0, 0))],
        out_specs=pl.BlockSpec((None, ch, 2 * n1, cb), lambda o, j, k: (o, j, 0, k)),
        compiler_params=_params(("parallel", "parallel", "parallel"), big=True),
        name="fft_filter_stage2",
    )(y, y, f2)


def hyena_long_conv_gate(zsrc, z_col, gate_src, gate_col, hf, fbias, consts, cb=512):
    f1_pad, _, f2, g2, g1 = consts
    n1, n2, ch = FFT_N1, FFT_N2, FFT_CHUNK
    c = hf.shape[-1]
    ncb = c // cb
    half = n2 // 2
    y = pl.pallas_call(
        _fft_stage1_kernel,
        out_shape=jax.ShapeDtypeStruct((n1, 2 * n2, c), F32),
        grid=(n1 // ch, ncb),
        in_specs=[pl.BlockSpec((2, half, ch, cb), lambda j, k: (0, 0, j, z_col * ncb + k)),
                  pl.BlockSpec((ch, 2 * n2, n2), lambda j, k: (j, 0, 0))],
        out_specs=pl.BlockSpec((ch, 2 * n2, cb), lambda j, k: (j, 0, k)),
        compiler_params=_params(("parallel", "parallel"), big=True),
        name="fft_stage1",
    )(zsrc, f1_pad)
    nk = n2 // ch
    q = pl.pallas_call(
        _fft_mid_kernel,
        out_shape=jax.ShapeDtypeStruct((n2, 2 * n1, c), F32),
        grid=(nk, ncb),
        in_specs=[pl.BlockSpec((n1, ch, cb), lambda j, k: (0, j, k)),
                  pl.BlockSpec((n1, ch, cb), lambda j, k: (0, nk + j, k)),
                  pl.BlockSpec((2 * n1, 2 * n1), lambda j, k: (0, 0)),
                  pl.BlockSpec((ch, 2 * n1, cb), lambda j, k: (j, 0, k)),
                  pl.BlockSpec((ch, 2 * n1, 2 * n1), lambda j, k: (j, 0, 0))],
        out_specs=pl.BlockSpec((ch, 2 * n1, cb), lambda j, k: (j, 0, k)),
        compiler_params=_params(("parallel", "parallel"), big=True),
        name="fft_mid",
    )(y, y, f2, hf, g2)
    nj = n1 // ch
    return pl.pallas_call(
        _fft_last_kernel,
        out_shape=jax.ShapeDtypeStruct((2, half, n1, c), F32),
        grid=(nj, ncb),
        in_specs=[pl.BlockSpec((n2, ch, cb), lambda j, k: (0, j, k)),
                  pl.BlockSpec((n2, ch, cb), lambda j, k: (0, nj + j, k)),
                  pl.BlockSpec((2 * half, 2 * n2), lambda j, k: (0, 0)),
                  pl.BlockSpec((2, half, ch, cb), lambda j, k: (0, 0, j, gate_col * ncb + k)),
                  pl.BlockSpec((2, half, ch, cb), lambda j, k: (0, 0, j, z_col * ncb + k)),
                  pl.BlockSpec((1, cb), lambda j, k: (0, k))],
        out_specs=pl.BlockSpec((2, half, ch, cb), lambda j, k: (0, 0, j, k)),
        compiler_params=_params(("parallel", "parallel"), big=True),
        name="fft_last",
    )(q, q, g1, gate_src, zsrc, fbias.reshape(1, c))


def hyena_mixer(hn, h_res, w_in, b_in, conv_w, conv_b, fw1, fb1, fw2, fb2, freq, fw3, fbias, w_out,
                batch, seq):
    width = w_out.shape[0]
    assert batch == 2 and 2 * seq == FFT_N1 * FFT_N2
    proj = matmul([hn], w_in, bias=b_in, name="hyena_in")
    sc = short_conv(proj.reshape(batch, seq, 3 * width), conv_w, conv_b)
    consts = _dft_constants()
    ts = hyena_two_sided_filters(seq, fw1, fb1, fw2, fb2, freq, fw3, width)
    hf = hyena_filter_spectrum(ts, consts)
    sc4 = sc.reshape(batch, FFT_N2 // 2, FFT_N1, 3 * width)
    zf1 = hyena_long_conv_gate(sc4, 2, sc4, 0, hf[0], fbias[0], consts)
    zf2 = _long_conv_second(zf1, sc4, hf[1], fbias[1], consts)
    return matmul([zf2.reshape(batch * seq, width)], w_out, res=h_res, name="hyena_out")


def _long_conv_second(zf1, sc4, hf, fbias, consts, cb=512):
    f1_pad, _, f2, g2, g1 = consts
    n1, n2, ch = FFT_N1, FFT_N2, FFT_CHUNK
    c = hf.shape[-1]
    ncb = c // cb
    half = n2 // 2
    y = pl.pallas_call(
        _fft_stage1_kernel,
        out_shape=jax.ShapeDtypeStruct((n1, 2 * n2, c), F32),
        grid=(n1 // ch, ncb),
        in_specs=[pl.BlockSpec((2, half, ch, cb), lambda j, k: (0, 0, j, k)),
                  pl.BlockSpec((ch, 2 * n2, n2), lambda j, k: (j, 0, 0))],
        out_specs=pl.BlockSpec((ch, 2 * n2, cb), lambda j, k: (j, 0, k)),
        compiler_params=_params(("parallel", "parallel"), big=True),
        name="fft_stage1_b",
    )(zf1, f1_pad)
    nk = n2 // ch
    q = pl.pallas_call(
        _fft_mid_kernel,
        out_shape=jax.ShapeDtypeStruct((n2, 2 * n1, c), F32),
        grid=(nk, ncb),
        in_specs=[pl.BlockSpec((n1, ch, cb), lambda j, k: (0, j, k)),
                  pl.BlockSpec((n1, ch, cb), lambda j, k: (0, nk + j, k)),
                  pl.BlockSpec((2 * n1, 2 * n1), lambda j, k: (0, 0)),
                  pl.BlockSpec((ch, 2 * n1, cb), lambda j, k: (j, 0, k)),
                  pl.BlockSpec((ch, 2 * n1, 2 * n1), lambda j, k: (j, 0, 0))],
        out_specs=pl.BlockSpec((ch, 2 * n1, cb), lambda j, k: (j, 0, k)),
        compiler_params=_params(("parallel", "parallel"), big=True),
        name="fft_mid_b",
    )(y, y, f2, hf, g2)
    nj = n1 // ch
    return pl.pallas_call(
        _fft_last_kernel,
        out_shape=jax.ShapeDtypeStruct((2, half, n1, c), F32),
        grid=(nj, ncb),
        in_specs=[pl.BlockSpec((n2, ch, cb), lambda j, k: (0, j, k)),
                  pl.BlockSpec((n2, ch, cb), lambda j, k: (0, nj + j, k)),
                  pl.BlockSpec((2 * half, 2 * n2), lambda j, k: (0, 0)),
                  pl.BlockSpec((2, half, ch, cb), lambda j, k: (0, 0, j, ncb + k)),
                  pl.BlockSpec((2, half, ch, cb), lambda j, k: (0, 0, j, k)),
                  pl.BlockSpec((1, cb), lambda j, k: (0, k))],
        out_specs=pl.BlockSpec((2, half, ch, cb), lambda j, k: (0, 0, j, k)),
        compiler_params=_params(("parallel", "parallel"), big=True),
        name="fft_last_b",
    )(q, q, g1, sc4, zf1, fbias.reshape(1, c))


def moe_swiglu(hn2, logits, wg, wu, wd):
    n, d = hn2.shape
    top_v, top_i = lax.top_k(logits, TOP_K)
    gates = jax.nn.softmax(top_v, axis=-1)
    e_flat = top_i.reshape(-1).astype(jnp.int32)
    g_flat = gates.reshape(-1)
    tok_flat = jnp.repeat(jnp.arange(n, dtype=jnp.int32), TOP_K)
    nk = n * TOP_K
    onehot = (e_flat[:, None] == jnp.arange(N_EXPERTS, dtype=jnp.int32)[None]).astype(jnp.int32)
    csum = jnp.cumsum(onehot, axis=0)
    rank = jnp.take_along_axis(csum, e_flat[:, None], axis=1)[:, 0] - 1
    counts = csum[-1]
    padded = ((counts + MOE_TILE - 1) // MOE_TILE) * MOE_TILE
    pad_end = jnp.cumsum(padded)
    pad_start = pad_end - padded
    dest = pad_start[e_flat] + rank
    p_rows = nk + N_EXPERTS * MOE_TILE
    row_tok = jnp.zeros((p_rows,), jnp.int32).at[dest].set(tok_flat)
    row_gate = jnp.zeros((p_rows,), F32).at[dest].set(g_flat)
    nt = p_rows // MOE_TILE
    tile_start = jnp.arange(nt, dtype=jnp.int32) * MOE_TILE
    tile_used = (tile_start < pad_end[-1]).astype(jnp.int32)
    tile_exp = jnp.minimum(jnp.searchsorted(pad_end, tile_start, side='right'), N_EXPERTS - 1).astype(jnp.int32)
    last_exp = tile_exp[jnp.maximum(jnp.sum(tile_used) - 1, 0)]
    tile_exp = jnp.where(tile_used > 0, tile_exp, last_exp)
    xs = hn2[row_tok]
    ys = swiglu(xs, wg, wu, wd, tile_exp, tile_used, row_gate[:, None], name="swiglu_experts")
    dest2 = dest.reshape(n, TOP_K)
    return ys[dest2[:, 0]] + ys[dest2[:, 1]]


def kernel(x, p, ln_mix, ln_ffn, ln_ple, final_norm, t5_bias, w_attn_in, w_attn_out, attn_sink, na_rpb, w_ffn_gate, w_ffn_up, w_ffn_down, w_hy_in, b_hy_in, w_hy_conv, b_hy_conv, w_hy_f1, b_hy_f1, w_hy_f2, b_hy_f2, hy_freq, w_hy_f3, hy_bias, w_hy_out, w_router, w_exp_gate, w_exp_up, w_exp_down, w_ple_proj, w_ple_gate):
    batch, seq, d = x.shape
    n = batch * seq
    depth = ln_mix.shape[0]
    h = x.reshape(n, d)
    for i in range(depth):
        li = i // 2
        hn = rmsnorm(h, ln_mix[i], BF16)
        if i % 2 == 0:
            na_off = A_WIDTH + 2 * A_KV_WIDTH
            w_in = w_attn_in[li]
            proj_a = matmul([hn], w_in[:, :na_off], out_dtype=BF16, tn=256, name="attn_in_a")
            proj_n = matmul([hn], w_in[:, na_off:], out_dtype=BF16, name="attn_in_n")
            oa = window_attention(proj_a.reshape(batch, seq, -1), t5_bias, attn_sink[li])
            ob = neighbourhood_attention(proj_n.reshape(batch, seq, -1), na_rpb[li])
            h = matmul([oa.reshape(n, A_WIDTH), ob.reshape(n, B_WIDTH)], w_attn_out[li], res=h,
                       name="attn_out")
            hn2 = rmsnorm(h, ln_ffn[i], BF16)
            nt = n // MOE_TILE
            delta = swiglu(hn2, w_ffn_gate, w_ffn_up, w_ffn_down,
                           jnp.full((nt,), li, jnp.int32), jnp.ones((nt,), jnp.int32),
                           jnp.ones((n, 1), F32), name="swiglu_dense")
        else:
            h = hyena_mixer(hn, h, w_hy_in[li], b_hy_in[li], w_hy_conv[li], b_hy_conv[li],
                            w_hy_f1[li], b_hy_f1[li], w_hy_f2[li], b_hy_f2[li], hy_freq[li],
                            w_hy_f3[li], hy_bias[li], w_hy_out[li], batch, seq)
            wr_pad = jnp.pad(w_router[li].astype(F32), ((0, 0), (0, 128 - N_EXPERTS)))
            hn2, logits = rmsnorm_router(h, ln_ffn[i], wr_pad)
            delta = moe_swiglu(hn2, logits[:, :N_EXPERTS], w_exp_gate[li], w_exp_up[li], w_exp_down[li])
        h, hn3 = add_rmsnorm(h, delta, ln_ple[i])
        h = ple(hn3, p[i].reshape(n, -1), h, w_ple_gate[i], w_ple_proj[i])
    return rmsnorm(h, final_norm, F32).reshape(batch, seq, d)
```

```python
import functools
import math

import jax
import jax.numpy as jnp
import numpy as np
from jax import lax
from jax.experimental import pallas as pl
from jax.experimental.pallas import tpu as pltpu

F32 = jnp.float32
BF16 = jnp.bfloat16
NEG_INF = -1e30
RMS_EPS = 1e-6

V7X_VMEM_LIMIT_BYTES = 56 * 1024 * 1024

HEAD_DIM = 64
A_Q_HEADS = 16
A_KV_HEADS = 2
A_GROUP = A_Q_HEADS // A_KV_HEADS
A_BLOCK = 128
T5_BUCKETS = 32
T5_MAX_DIST = 128
B_HEADS = 16
GRID_W = 64
NA_WIN_H = 8
NA_WIN_W = 16
A_WIDTH = A_Q_HEADS * HEAD_DIM
A_KV_WIDTH = A_KV_HEADS * HEAD_DIM
B_WIDTH = B_HEADS * HEAD_DIM
N_EXPERTS = 8
TOP_K = 2
HYENA_EMB = 33
HYENA_DECAY_TARGET = 1e-2
HYENA_FAST_PCT = 0.3
HYENA_SLOW_PCT = 1.5

FFT_N1 = 64
FFT_N2 = 128
FFT_CHUNK = 8

MOE_TILE = 1024
FFN_F_TILE = 256


def _params(semantics, big=False):
    return pltpu.CompilerParams(
        dimension_semantics=semantics,
        vmem_limit_bytes=V7X_VMEM_LIMIT_BYTES if big else None)


def _rmsnorm_kernel(x_ref, g_ref, o_ref):
    x = x_ref[...]
    y = x * lax.rsqrt(jnp.mean(x * x, axis=-1, keepdims=True) + RMS_EPS)
    o_ref[...] = (y * g_ref[...]).astype(o_ref.dtype)


def rmsnorm(x2, g, out_dtype, tm=512):
    n, d = x2.shape
    return pl.pallas_call(
        _rmsnorm_kernel,
        out_shape=jax.ShapeDtypeStruct((n, d), out_dtype),
        grid=(n // tm,),
        in_specs=[pl.BlockSpec((tm, d), lambda i: (i, 0)),
                  pl.BlockSpec((1, d), lambda i: (0, 0))],
        out_specs=pl.BlockSpec((tm, d), lambda i: (i, 0)),
        compiler_params=_params(("parallel",)),
        name="rmsnorm",
    )(x2, g.reshape(1, d))


def _add_rmsnorm_kernel(x_ref, d_ref, g_ref, h_ref, o_ref):
    x = x_ref[...] + d_ref[...]
    h_ref[...] = x
    y = x * lax.rsqrt(jnp.mean(x * x, axis=-1, keepdims=True) + RMS_EPS)
    o_ref[...] = (y * g_ref[...]).astype(o_ref.dtype)


def add_rmsnorm(x2, delta, g, tm=512):
    n, d = x2.shape
    return pl.pallas_call(
        _add_rmsnorm_kernel,
        out_shape=(jax.ShapeDtypeStruct((n, d), F32), jax.ShapeDtypeStruct((n, d), BF16)),
        grid=(n // tm,),
        in_specs=[pl.BlockSpec((tm, d), lambda i: (i, 0)),
                  pl.BlockSpec((tm, d), lambda i: (i, 0)),
                  pl.BlockSpec((1, d), lambda i: (0, 0))],
        out_specs=(pl.BlockSpec((tm, d), lambda i: (i, 0)),
                   pl.BlockSpec((tm, d), lambda i: (i, 0))),
        compiler_params=_params(("parallel",)),
        name="add_rmsnorm",
    )(x2, delta, g.reshape(1, d))


def _pack_bf16_pairs(y):
    w = y.shape[1] // 2
    bits = pltpu.bitcast(y.astype(BF16).astype(F32), jnp.uint32)
    return (bits[:, :w] >> 16) | (bits[:, w:] & jnp.uint32(0xFFFF0000))


def _unpack_bf16_pairs(p):
    lo = pltpu.bitcast(p << 16, F32)
    hi = pltpu.bitcast(p & jnp.uint32(0xFFFF0000), F32)
    return jnp.concatenate([lo, hi], axis=1).astype(BF16)


def _rmsnorm_router_kernel(x_ref, g_ref, wr_ref, o_ref, l_ref):
    x = x_ref[...]
    y = x * lax.rsqrt(jnp.mean(x * x, axis=-1, keepdims=True) + RMS_EPS)
    y = y * g_ref[...]
    o_ref[...] = _pack_bf16_pairs(y)
    l_ref[...] = jnp.dot(y, wr_ref[...], preferred_element_type=F32,
                         precision=lax.Precision.HIGHEST)


def rmsnorm_router(x2, g, w_router_pad, tm=512):
    n, d = x2.shape
    ne = w_router_pad.shape[1]
    return pl.pallas_call(
        _rmsnorm_router_kernel,
        out_shape=(jax.ShapeDtypeStruct((n, d // 2), jnp.uint32), jax.ShapeDtypeStruct((n, ne), F32)),
        grid=(n // tm,),
        in_specs=[pl.BlockSpec((tm, d), lambda i: (i, 0)),
                  pl.BlockSpec((1, d), lambda i: (0, 0)),
                  pl.BlockSpec((d, ne), lambda i: (0, 0))],
        out_specs=(pl.BlockSpec((tm, d // 2), lambda i: (i, 0)),
                   pl.BlockSpec((tm, ne), lambda i: (i, 0))),
        compiler_params=_params(("parallel",)),
        name="rmsnorm_router",
    )(x2, g.reshape(1, d), w_router_pad)


def _mm_kernel(*refs, n_a, has_bias, has_res):
    a_refs = refs[:n_a]
    w_refs = refs[n_a:2 * n_a]
    idx = 2 * n_a
    bias_ref = refs[idx] if has_bias else None
    idx += int(has_bias)
    res_ref = refs[idx] if has_res else None
    idx += int(has_res)
    o_ref = refs[idx]
    wbf_refs = refs[idx + 1:idx + 1 + n_a]

    @pl.when(pl.program_id(1) == 0)
    def _():
        for w_ref, wbf_ref in zip(w_refs, wbf_refs):
            wbf_ref[...] = w_ref[...].astype(BF16)

    acc = None
    for a_ref, wbf_ref in zip(a_refs, wbf_refs):
        d = jnp.dot(a_ref[...].astype(BF16), wbf_ref[...], preferred_element_type=F32)
        acc = d if acc is None else acc + d
    if has_bias:
        acc = acc + bias_ref[...]
    if has_res:
        acc = acc + res_ref[...]
    o_ref[...] = acc.astype(o_ref.dtype)


def matmul(a_list, w, *, bias=None, res=None, out_dtype=F32, tm=512, tn=512, name="matmul"):
    m = a_list[0].shape[0]
    n = w.shape[1]
    n_a = len(a_list)
    k_each = a_list[0].shape[1]
    assert all(a.shape == (m, k_each) for a in a_list) and w.shape[0] == n_a * k_each
    in_specs = [pl.BlockSpec((tm, k_each), lambda j, i: (i, 0)) for _ in a_list]
    in_specs += [pl.BlockSpec((k_each, tn), functools.partial(lambda j, i, kb: (kb, j), kb=kb))
                 for kb in range(n_a)]
    args = list(a_list) + [w] * n_a
    if bias is not None:
        in_specs.append(pl.BlockSpec((1, tn), lambda j, i: (0, j)))
        args.append(bias.reshape(1, n))
    if res is not None:
        in_specs.append(pl.BlockSpec((tm, tn), lambda j, i: (i, j)))
        args.append(res)
    return pl.pallas_call(
        functools.partial(_mm_kernel, n_a=n_a, has_bias=bias is not None, has_res=res is not None),
        out_shape=jax.ShapeDtypeStruct((m, n), out_dtype),
        grid=(n // tn, m // tm),
        in_specs=in_specs,
        out_specs=pl.BlockSpec((tm, tn), lambda j, i: (i, j)),
        scratch_shapes=[pltpu.VMEM((k_each, tn), BF16) for _ in a_list],
        compiler_params=_params(("parallel", "arbitrary"), big=True),
        name=name,
    )(*args)


def _ple_kernel(hn_ref, p_ref, h_ref, wg_ref, wp_ref, o_ref, wg_bf, wp_bf):
    @pl.when(pl.program_id(1) == 0)
    def _():
        wg_bf[...] = wg_ref[...].astype(BF16)
        wp_bf[...] = wp_ref[...].astype(BF16)

    a = jnp.dot(hn_ref[...], wg_bf[...], preferred_element_type=F32)
    pp = jnp.dot(p_ref[...].astype(BF16), wp_bf[...], preferred_element_type=F32)
    o_ref[...] = h_ref[...] + jax.nn.sigmoid(a) * pp


def ple(hn, p2, h2, w_gate, w_proj, tm=512, tn=512):
    m, d = h2.shape
    pd = p2.shape[1]
    return pl.pallas_call(
        _ple_kernel,
        out_shape=jax.ShapeDtypeStruct((m, d), F32),
        grid=(d // tn, m // tm),
        in_specs=[pl.BlockSpec((tm, d), lambda j, i: (i, 0)),
                  pl.BlockSpec((tm, pd), lambda j, i: (i, 0)),
                  pl.BlockSpec((tm, tn), lambda j, i: (i, j)),
                  pl.BlockSpec((d, tn), lambda j, i: (0, j)),
                  pl.BlockSpec((pd, tn), lambda j, i: (0, j))],
        out_specs=pl.BlockSpec((tm, tn), lambda j, i: (i, j)),
        scratch_shapes=[pltpu.VMEM((d, tn), BF16), pltpu.VMEM((pd, tn), BF16)],
        compiler_params=_params(("parallel", "arbitrary"), big=True),
        name="ple",
    )(hn, p2, h2, w_gate, w_proj)


def _row_copy(src_hbm, row, dst, i, sem):
    return pltpu.make_async_copy(src_hbm.at[pl.ds(row, 1), :], dst.at[pl.ds(i, 1), :], sem)


def _swiglu_kernel(exp_ref, used_ref, tok_ref, x_ref, wg_ref, wu_ref, wd_ref, gate_ref, o_ref,
                   *scratch, n_col, gather):
    del exp_ref
    t = pl.program_id(0)
    f = pl.program_id(1)
    nt = pl.num_programs(0)
    used = used_ref[t] > 0
    tm = o_ref.shape[0]

    @pl.when(f == 0)
    def _():
        o_ref[...] = jnp.zeros_like(o_ref)

    if gather:
        xbuf, xbf, sems = scratch
        slot = t % 2

        def start_tile(tile, sl):
            def body(i, c):
                _row_copy(x_ref, tok_ref[tile * tm + i], xbuf.at[sl], i, sems.at[sl, i]).start()
                return c
            lax.fori_loop(0, tm, body, 0)

        @pl.when(jnp.logical_and(f == 0, jnp.logical_and(t == 0, used)))
        def _():
            start_tile(0, 0)

        @pl.when(jnp.logical_and(f == 0, used))
        def _():
            def body(i, c):
                _row_copy(x_ref, 0, xbuf.at[slot], i, sems.at[slot, i]).wait()
                return c
            lax.fori_loop(0, tm, body, 0)
            xbf[...] = _unpack_bf16_pairs(xbuf[slot])

        nxt = jnp.minimum(t + 1, nt - 1)
        @pl.when(jnp.logical_and(f == 1, jnp.logical_and(t + 1 < nt, used_ref[nxt] > 0)))
        def _():
            start_tile(t + 1, 1 - slot)

        x_src = xbf
    else:
        x_src = x_ref

    @pl.when(used)
    def _():
        x = x_src[...]
        g = jnp.dot(x, wg_ref[...].astype(BF16), preferred_element_type=F32)
        u = jnp.dot(x, wu_ref[...].astype(BF16), preferred_element_type=F32)
        hmid = (g * jax.nn.sigmoid(g) * u).astype(BF16)
        wd = wd_ref[...].astype(BF16)
        cw = o_ref.shape[1] // n_col
        for c in range(n_col):
            y = jnp.dot(hmid, wd[:, c * cw:(c + 1) * cw], preferred_element_type=F32)
            o_ref[:, c * cw:(c + 1) * cw] += y

    @pl.when(jnp.logical_and(used, f == pl.num_programs(1) - 1))
    def _():
        o_ref[...] = o_ref[...] * gate_ref[...]


def swiglu(x, w_gate, w_up, w_down, tile_exp, tile_used, row_gate, row_src=None, *, tm=MOE_TILE,
           tf=FFN_F_TILE, name="swiglu"):
    gather = row_src is not None
    m = row_gate.shape[0]
    d = w_gate.shape[1]
    dff = w_gate.shape[2]
    nf = dff // tf
    nt = m // tm
    assert nf >= 2

    def w_in_map(t, f, exp_ref, used_ref, tok_ref):
        return (exp_ref[t], 0, jnp.where(used_ref[t] > 0, f, nf - 1))

    def w_out_map(t, f, exp_ref, used_ref, tok_ref):
        return (exp_ref[t], jnp.where(used_ref[t] > 0, f, nf - 1), 0)

    if gather:
        x_spec = pl.BlockSpec(memory_space=pl.ANY)
        scratch = [pltpu.VMEM((2, tm, d // 2), jnp.uint32), pltpu.VMEM((tm, d), BF16),
                   pltpu.SemaphoreType.DMA((2, tm))]
    else:
        x_spec = pl.BlockSpec((tm, d), lambda t, f, e, u, k: (t, 0))
        scratch = []
        row_src = jnp.zeros((1,), jnp.int32)
    grid_spec = pltpu.PrefetchScalarGridSpec(
        num_scalar_prefetch=3,
        grid=(nt, nf),
        in_specs=[x_spec,
                  pl.BlockSpec((None, d, tf), w_in_map),
                  pl.BlockSpec((None, d, tf), w_in_map),
                  pl.BlockSpec((None, tf, d), w_out_map),
                  pl.BlockSpec((tm, 1), lambda t, f, e, u, k: (t, 0))],
        out_specs=pl.BlockSpec((tm, d), lambda t, f, e, u, k: (t, 0)),
        scratch_shapes=scratch,
    )
    return pl.pallas_call(
        functools.partial(_swiglu_kernel, n_col=4, gather=gather),
        out_shape=jax.ShapeDtypeStruct((m, d), F32),
        grid_spec=grid_spec,
        compiler_params=_params(("arbitrary", "arbitrary"), big=True),
        name=name,
    )(tile_exp, tile_used, row_src, x, w_gate, w_up, w_down, row_gate)


COMBINE_TILE = 256


def _combine_kernel(d0_ref, d1_ref, ys_hbm, h_ref, g_ref, hout_ref, hn_ref, buf, sems):
    i = pl.program_id(0)
    n = pl.num_programs(0)
    tt = h_ref.shape[0]
    slot = i % 2
    dests = (d0_ref, d1_ref)

    def start_tile(tile, sl):
        def body(r, c):
            for k, d_ref in enumerate(dests):
                _row_copy(ys_hbm, d_ref[tile * tt + r], buf.at[sl, k], r, sems.at[sl, k, r]).start()
            return c
        lax.fori_loop(0, tt, body, 0)

    @pl.when(i == 0)
    def _():
        start_tile(0, 0)

    @pl.when(i + 1 < n)
    def _():
        start_tile(i + 1, 1 - slot)

    def wait_body(r, c):
        for k in range(TOP_K):
            _row_copy(ys_hbm, 0, buf.at[slot, k], r, sems.at[slot, k, r]).wait()
        return c
    lax.fori_loop(0, tt, wait_body, 0)

    x = h_ref[...] + (buf[slot, 0] + buf[slot, 1])
    hout_ref[...] = x
    y = x * lax.rsqrt(jnp.mean(x * x, axis=-1, keepdims=True) + RMS_EPS)
    hn_ref[...] = (y * g_ref[...]).astype(hn_ref.dtype)


def moe_combine_rmsnorm(h2, ys, dest2, g, tt=COMBINE_TILE):
    n, d = h2.shape
    grid_spec = pltpu.PrefetchScalarGridSpec(
        num_scalar_prefetch=2,
        grid=(n // tt,),
        in_specs=[pl.BlockSpec(memory_space=pl.ANY),
                  pl.BlockSpec((tt, d), lambda i, a, b: (i, 0)),
                  pl.BlockSpec((1, d), lambda i, a, b: (0, 0))],
        out_specs=(pl.BlockSpec((tt, d), lambda i, a, b: (i, 0)),
                   pl.BlockSpec((tt, d), lambda i, a, b: (i, 0))),
        scratch_shapes=[pltpu.VMEM((2, TOP_K, tt, d), F32), pltpu.SemaphoreType.DMA((2, TOP_K, tt))],
    )
    return pl.pallas_call(
        _combine_kernel,
        out_shape=(jax.ShapeDtypeStruct((n, d), F32), jax.ShapeDtypeStruct((n, d), BF16)),
        grid_spec=grid_spec,
        compiler_params=_params(("arbitrary",), big=True),
        name="moe_combine",
    )(dest2[:, 0], dest2[:, 1], ys, h2, g.reshape(1, d))


def _t5_bucket(rel):
    half = T5_BUCKETS // 2
    max_exact = half // 2
    n = jnp.abs(rel)
    log_ratio = jnp.log(jnp.maximum(n, 1).astype(F32) / max_exact) / math.log(T5_MAX_DIST / max_exact)
    large = jnp.minimum(max_exact + (log_ratio * (half - max_exact)).astype(jnp.int32), half - 1)
    return jnp.where(rel > 0, half, 0) + jnp.where(n < max_exact, n, large)


def _window_bias_table(t5_bias):
    i = jnp.arange(A_BLOCK)[:, None]
    j = jnp.arange(3 * A_BLOCK)[None, :]
    rel = j - A_BLOCK - i
    bias = t5_bias.astype(F32)[_t5_bucket(rel)].transpose(2, 0, 1)
    return jnp.where((jnp.abs(rel) <= A_BLOCK)[None], bias, NEG_INF)


def _window_kernel(sink_ref, q_ref, kv_ref, bias_ref, o_ref, *, nb):
    n = pl.program_id(1)
    scale = HEAD_DIM ** -0.5
    starts = (jnp.maximum(n - 1, 0), n, jnp.minimum(n + 1, nb - 1))
    kv = [kv_ref[pl.ds(pl.multiple_of(s * A_BLOCK, A_BLOCK), A_BLOCK), :] for s in starts]
    col = lax.broadcasted_iota(jnp.int32, (A_BLOCK, 3 * A_BLOCK), 1)
    edge_ok = jnp.logical_and(jnp.logical_or(n > 0, col >= A_BLOCK),
                              jnp.logical_or(n < nb - 1, col < 2 * A_BLOCK))
    q = q_ref[...]
    for kh in range(A_KV_HEADS):
        ks = [blk[:, kh * HEAD_DIM:(kh + 1) * HEAD_DIM] for blk in kv]
        v = jnp.concatenate(
            [blk[:, A_KV_WIDTH + kh * HEAD_DIM:A_KV_WIDTH + (kh + 1) * HEAD_DIM] for blk in kv], axis=0)
        for g in range(A_GROUP):
            h = kh * A_GROUP + g
            qh = q[:, h * HEAD_DIM:(h + 1) * HEAD_DIM]
            s = jnp.concatenate(
                [lax.dot_general(qh, kb, (((1,), (1,)), ((), ())), preferred_element_type=F32)
                 for kb in ks], axis=1)
            s = jnp.where(edge_ok, s * scale + bias_ref[h], NEG_INF)
            sink = sink_ref[h]
            mx = jnp.maximum(jnp.max(s, axis=-1, keepdims=True), sink)
            p = jnp.exp(s - mx)
            den = jnp.sum(p, axis=-1, keepdims=True) + jnp.exp(sink - mx)
            o = jnp.dot(p.astype(BF16), v, preferred_element_type=F32) / den
            o_ref[:, h * HEAD_DIM:(h + 1) * HEAD_DIM] = o.astype(o_ref.dtype)


def window_attention(proj_a, t5_bias, sink):
    b, s, _ = proj_a.shape
    nb = s // A_BLOCK
    kvw = 2 * A_KV_WIDTH
    grid_spec = pltpu.PrefetchScalarGridSpec(
        num_scalar_prefetch=0,
        grid=(b, nb),
        in_specs=[pl.BlockSpec(memory_space=pltpu.SMEM),
                  pl.BlockSpec((None, A_BLOCK, A_WIDTH), lambda bi, n: (bi, n, 0)),
                  pl.BlockSpec((None, s, kvw), lambda bi, n: (bi, 0, A_WIDTH // kvw)),
                  pl.BlockSpec((A_Q_HEADS, A_BLOCK, 3 * A_BLOCK), lambda bi, n: (0, 0, 0))],
        out_specs=pl.BlockSpec((None, A_BLOCK, A_WIDTH), lambda bi, n: (bi, n, 0)),
    )
    return pl.pallas_call(
        functools.partial(_window_kernel, nb=nb),
        out_shape=jax.ShapeDtypeStruct((b, s, A_WIDTH), BF16),
        grid_spec=grid_spec,
        compiler_params=_params(("parallel", "arbitrary")),
        name="window_attention",
    )(sink.astype(F32), proj_a, proj_a, _window_bias_table(t5_bias))


NA_HEAD_GROUP = 4


def _na_bias_table(rpb, rows):
    kh = min(NA_WIN_H, rows)
    kw = NA_WIN_W
    c = jnp.arange(GRID_W)
    cs = jnp.clip(c - kw // 2, 0, GRID_W - kw)
    col_ok = (c[None] >= cs[:, None]) & (c[None] < cs[:, None] + kw)
    col_off = jnp.clip(c[None] - c[:, None], -(kw - 1), kw - 1) + kw - 1
    d = jnp.arange(kh)[:, None]
    row_off = jnp.arange(kh)[None, :] - d + NA_WIN_H - 1
    bias = rpb.astype(F32)[:, row_off[:, None, :, None], col_off[None, :, None, :]]
    bias = jnp.where(col_ok[None, None, :, None, :], bias, NEG_INF)
    return bias.transpose(1, 0, 2, 3, 4).reshape(kh, rpb.shape[0], GRID_W, kh * GRID_W)


def _na_kernel(q_ref, k_ref, v_ref, bias_ref, o_ref, *, rows):
    r = pl.program_id(2)
    kh = min(NA_WIN_H, rows)
    scale = HEAD_DIM ** -0.5
    start = pl.multiple_of(jnp.clip(r - kh // 2, 0, rows - kh) * GRID_W, GRID_W)
    k = k_ref[pl.ds(start, kh * GRID_W), :]
    v = v_ref[pl.ds(start, kh * GRID_W), :]
    q = q_ref[...]
    for h in range(NA_HEAD_GROUP):
        sl = slice(h * HEAD_DIM, (h + 1) * HEAD_DIM)
        s = lax.dot_general(q[:, sl], k[:, sl], (((1,), (1,)), ((), ())), preferred_element_type=F32)
        s = s * scale + bias_ref[h]
        mx = jnp.max(s, axis=-1, keepdims=True)
        p = jnp.exp(s - mx)
        den = jnp.sum(p, axis=-1, keepdims=True)
        o = jnp.dot(p.astype(BF16), v[:, sl], preferred_element_type=F32) / den
        o_ref[:, sl] = o.astype(o_ref.dtype)


def neighbourhood_attention(proj_n, rpb):
    b, s, _ = proj_n.shape
    rows = s // GRID_W
    kh = min(NA_WIN_H, rows)
    gw = NA_HEAD_GROUP * HEAD_DIM
    ng = B_WIDTH // gw

    def bias_map(bi, g, r):
        return (r - jnp.clip(r - kh // 2, 0, rows - kh), g, 0, 0)

    return pl.pallas_call(
        functools.partial(_na_kernel, rows=rows),
        out_shape=jax.ShapeDtypeStruct((b, s, B_WIDTH), BF16),
        grid=(b, ng, rows),
        in_specs=[pl.BlockSpec((None, GRID_W, gw), lambda bi, g, r: (bi, r, g)),
                  pl.BlockSpec((None, s, gw), lambda bi, g, r: (bi, 0, ng + g)),
                  pl.BlockSpec((None, s, gw), lambda bi, g, r: (bi, 0, 2 * ng + g)),
                  pl.BlockSpec((None, NA_HEAD_GROUP, GRID_W, kh * GRID_W), bias_map)],
        out_specs=pl.BlockSpec((None, GRID_W, gw), lambda bi, g, r: (bi, r, g)),
        compiler_params=_params(("parallel", "parallel", "arbitrary")),
        name="neighbourhood_attention",
    )(proj_n, proj_n, proj_n, _na_bias_table(rpb, rows))


def _short_conv_kernel(prev_ref, cur_ref, next_ref, w_ref, b_ref, o_ref):
    i = pl.program_id(1)
    nt = pl.num_programs(1)
    x = cur_ref[...]
    tl = x.shape[0]
    row = lax.broadcasted_iota(jnp.int32, x.shape, 0)
    prev_row = jnp.where(i > 0, prev_ref[7:8, :], 0.0)
    next_row = jnp.where(i < nt - 1, next_ref[0:1, :], 0.0)
    up = jnp.where(row == 0, prev_row, pltpu.roll(x, 1, 0))
    dn = jnp.where(row == tl - 1, next_row, pltpu.roll(x, tl - 1, 0))
    o_ref[...] = up * w_ref[0:1, :] + x * w_ref[1:2, :] + dn * w_ref[2:3, :] + b_ref[...]


def short_conv(proj, conv_w, conv_b, tl=512, tc=1024):
    b, l, c = proj.shape
    nl8 = l // 8
    return pl.pallas_call(
        _short_conv_kernel,
        out_shape=jax.ShapeDtypeStruct((b, l, c), F32),
        grid=(b, l // tl, c // tc),
        in_specs=[pl.BlockSpec((None, 8, tc), lambda bi, i, j: (bi, jnp.maximum(i * (tl // 8) - 1, 0), j)),
                  pl.BlockSpec((None, tl, tc), lambda bi, i, j: (bi, i, j)),
                  pl.BlockSpec((None, 8, tc), lambda bi, i, j: (bi, jnp.minimum((i + 1) * (tl // 8), nl8 - 1), j)),
                  pl.BlockSpec((3, tc), lambda bi, i, j: (0, j)),
                  pl.BlockSpec((1, tc), lambda bi, i, j: (0, j))],
        out_specs=pl.BlockSpec((None, tl, tc), lambda bi, i, j: (bi, i, j)),
        compiler_params=_params(("parallel", "parallel", "parallel")),
        name="short_conv",
    )(proj, proj, proj, conv_w, conv_b.reshape(1, c))


def _filter_kernel(z_ref, t_ref, fw1_ref, fb1_ref, fw2_ref, fb2_ref, fr_ref, w3_ref, w3b_ref,
                   delta_ref, o_ref, *, seq):
    m = pl.program_id(1)
    hi = lax.Precision.HIGHEST
    fr = fr_ref[...]
    hid = jnp.sin(fr * (jnp.dot(z_ref[...], fw1_ref[...], preferred_element_type=F32, precision=hi)
                        + fb1_ref[...]))
    hid = jnp.sin(fr * (jnp.dot(hid, fw2_ref[...], preferred_element_type=F32, precision=hi)
                        + fb2_ref[...]))
    hid_bf = hid.astype(BF16)
    h = jnp.dot(hid_bf, w3_ref[...].astype(BF16), preferred_element_type=F32)
    tm = h.shape[0]
    row = lax.broadcasted_iota(jnp.int32, h.shape, 0) + m * tm
    h_back = jnp.dot(hid_bf, w3b_ref[...].astype(BF16), preferred_element_type=F32)
    h = h + jnp.where(row == 0, h_back, 0.0)
    decay = jnp.exp(-t_ref[...] * delta_ref[...])
    o_ref[...] = jnp.where(row == seq, 0.0, h * decay)


def hyena_two_sided_filters(seq, fw1, fb1, fw2, fb2, freq, fw3, width, tm=512):
    n = 2 * seq
    u = jnp.arange(n)
    pos = jnp.where(u <= seq, u, n - u)
    pos = jnp.where(u == seq, 0, pos)
    t_all = jnp.linspace(0.0, 1.0, seq, dtype=F32)
    bands = (HYENA_EMB - 1) // 2
    w_all = (2.0 * math.pi / seq) * jnp.arange(seq, dtype=F32)
    f = jnp.linspace(1e-4, bands - 1, bands, dtype=F32)[None]
    t = t_all[pos][:, None]
    w = w_all[pos][:, None]
    z = jnp.concatenate([t, jnp.cos(f * w), -jnp.sin(f * w)], axis=-1)
    emb_pad = 40
    z = jnp.pad(z, ((0, 0), (0, emb_pad - HYENA_EMB)))
    fw1p = jnp.pad(fw1.astype(F32), ((0, emb_pad - HYENA_EMB), (0, 0)))
    fh = fw1.shape[1]
    max_decay = math.log(HYENA_DECAY_TARGET) / HYENA_FAST_PCT
    min_decay = math.log(HYENA_DECAY_TARGET) / HYENA_SLOW_PCT
    deltas = jnp.abs(jnp.linspace(min_decay, max_decay, width, dtype=F32))[None]
    half_tiles = seq // tm

    def w3_map(o, m):
        return (0, jnp.where(m >= half_tiles, 2, 0) + o)

    return pl.pallas_call(
        functools.partial(_filter_kernel, seq=seq),
        out_shape=jax.ShapeDtypeStruct((2, n, width), F32),
        grid=(2, n // tm),
        in_specs=[pl.BlockSpec((tm, emb_pad), lambda o, m: (m, 0)),
                  pl.BlockSpec((tm, 1), lambda o, m: (m, 0)),
                  pl.BlockSpec((emb_pad, fh), lambda o, m: (0, 0)),
                  pl.BlockSpec((1, fh), lambda o, m: (0, 0)),
                  pl.BlockSpec((fh, fh), lambda o, m: (0, 0)),
                  pl.BlockSpec((1, fh), lambda o, m: (0, 0)),
                  pl.BlockSpec((1, fh), lambda o, m: (0, 0)),
                  pl.BlockSpec((fh, width), w3_map),
                  pl.BlockSpec((fh, width), lambda o, m: (0, 2 + o)),
                  pl.BlockSpec((1, width), lambda o, m: (0, 0))],
        out_specs=pl.BlockSpec((None, tm, width), lambda o, m: (o, m, 0)),
        compiler_params=_params(("parallel", "arbitrary")),
        name="hyena_filters",
    )(z, t, fw1p, fb1.reshape(1, fh).astype(F32), fw2.astype(F32), fb2.reshape(1, fh).astype(F32),
      freq.reshape(1, fh).astype(F32), fw3, fw3, deltas)


def _dft_constants():
    n1, n2 = FFT_N1, FFT_N2
    n = n1 * n2
    a1 = np.arange(n1)
    a2 = np.arange(n2)
    half = n2 // 2

    def cplx_block(w):
        return np.block([[w.real, -w.imag], [w.imag, w.real]])

    ang = (a2[None, :, None] * a2[None, None, :] / n2) + (a1[:, None, None] * a2[None, :, None] / n)
    w1 = np.exp(-2j * np.pi * ang)
    f1_pad = np.stack([cplx_block(w1[i][:, :half]) for i in range(n1)])
    f1_real = np.stack([np.concatenate([w1[i].real, w1[i].imag], axis=0) for i in range(n1)])
    w2 = np.exp(-2j * np.pi * (a1[:, None] * a1[None, :]) / n1)
    f2 = cplx_block(w2)
    ang = (a1[None, :, None] * a1[None, None, :] / n1) + (a2[:, None, None] * a1[None, :, None] / n)
    g2 = np.stack([cplx_block(m_) for m_ in np.exp(2j * np.pi * ang)])
    wg1 = np.exp(2j * np.pi * (a2[:half, None] * a2[None, :]) / n2) / n
    g1 = cplx_block(wg1)
    to = lambda x: jnp.asarray(x.astype(np.float32)).astype(BF16)
    return to(f1_pad), to(f1_real), to(f2), to(g2), to(g1)


def _fft_stage1_kernel(x_ref, f_ref, y_ref):
    xr = pltpu.einshape("mjc->jmc", x_ref[0])
    xi = pltpu.einshape("mjc->jmc", x_ref[1])
    for j in range(FFT_CHUNK):
        xc = jnp.concatenate([xr[j], xi[j]], axis=0).astype(BF16)
        y_ref[j] = jnp.dot(f_ref[j], xc, preferred_element_type=F32)


def _fft_stage1_real_kernel(x_ref, f_ref, y_ref):
    x = pltpu.einshape("mjc->jmc", x_ref[...])
    for j in range(FFT_CHUNK):
        y_ref[j] = jnp.dot(f_ref[j], x[j].astype(BF16), preferred_element_type=F32)


def _fft_stage2_kernel(yr_ref, yi_ref, f2_ref, h_ref):
    yr = pltpu.einshape("nkc->knc", yr_ref[...])
    yi = pltpu.einshape("nkc->knc", yi_ref[...])
    for j in range(FFT_CHUNK):
        yc = jnp.concatenate([yr[j], yi[j]], axis=0).astype(BF16)
        h_ref[j] = jnp.dot(f2_ref[...], yc, preferred_element_type=F32)


def _fft_mid_kernel(yr_ref, yi_ref, f2_ref, h_ref, g2_ref, q_ref):
    yr = pltpu.einshape("nkc->knc", yr_ref[...])
    yi = pltpu.einshape("nkc->knc", yi_ref[...])
    n1 = FFT_N1
    for j in range(FFT_CHUNK):
        yc = jnp.concatenate([yr[j], yi[j]], axis=0).astype(BF16)
        z = jnp.dot(f2_ref[...], yc, preferred_element_type=F32)
        hf = h_ref[j]
        zr, zi, hr, hi = z[:n1], z[n1:], hf[:n1], hf[n1:]
        pc = jnp.concatenate([zr * hr - zi * hi, zr * hi + zi * hr], axis=0).astype(BF16)
        q_ref[j] = jnp.dot(g2_ref[j], pc, preferred_element_type=F32)


def _fft_last_kernel(qr_ref, qi_ref, g1_ref, gate_ref, zf_ref, fb_ref, o_ref):
    qr = pltpu.einshape("kjc->jkc", qr_ref[...])
    qi = pltpu.einshape("kjc->jkc", qi_ref[...])
    half = FFT_N2 // 2
    conv = []
    for j in range(FFT_CHUNK):
        qc = jnp.concatenate([qr[j], qi[j]], axis=0).astype(BF16)
        conv.append(jnp.dot(g1_ref[...], qc, preferred_element_type=F32))
    conv = jnp.stack(conv, axis=0)
    fb = fb_ref[...]
    for bi in range(2):
        cb = pltpu.einshape("jmc->mjc", conv[:, bi * half:(bi + 1) * half])
        zf = zf_ref[bi]
        o_ref[bi] = gate_ref[bi] * (cb + zf * fb)


def hyena_filter_spectrum(ts, consts, cb=512):
    _, f1_real, f2, _, _ = consts
    no, n, c = ts.shape
    n1, n2, ch = FFT_N1, FFT_N2, FFT_CHUNK
    ts4 = ts.reshape(no, n2, n1, c)
    y = pl.pallas_call(
        _fft_stage1_real_kernel,
        out_shape=jax.ShapeDtypeStruct((no, n1, 2 * n2, c), F32),
        grid=(no, n1 // ch, c // cb),
        in_specs=[pl.BlockSpec((None, n2, ch, cb), lambda o, j, k: (o, 0, j, k)),
                  pl.BlockSpec((ch, 2 * n2, n2), lambda o, j, k: (j, 0, 0))],
        out_specs=pl.BlockSpec((None, ch, 2 * n2, cb), lambda o, j, k: (o, j, 0, k)),
        compiler_params=_params(("parallel", "parallel", "parallel"), big=True),
        name="fft_filter_stage1",
    )(ts4, f1_real)
    nk = n2 // ch
    return pl.pallas_call(
        _fft_stage2_kernel,
        out_shape=jax.ShapeDtypeStruct((no, n2, 2 * n1, c), F32),
        grid=(no, nk, c // cb),
        in_specs=[pl.BlockSpec((None, n1, ch, cb), lambda o, j, k: (o, 0, j, k)),
                  pl.BlockSpec((None, n1, ch, cb), lambda o, j, k: (o, 0, nk + j, k)),
                  pl.BlockSpec((2 * n1, 2 * n1), lambda o, j, k: (0, 0))],
        out_specs=pl.BlockSpec((None, ch, 2 * n1, cb), lambda o, j, k: (o, j, 0, k)),
        compiler_params=_params(("parallel", "parallel", "parallel"), big=True),
        name="fft_filter_stage2",
    )(y, y, f2)


def hyena_long_conv_gate(zsrc, z_col, gate_src, gate_col, hf, fbias, consts, cb=512):
    f1_pad, _, f2, g2, g1 = consts
    n1, n2, ch = FFT_N1, FFT_N2, FFT_CHUNK
    c = hf.shape[-1]
    ncb = c // cb
    half = n2 // 2
    y = pl.pallas_call(
        _fft_stage1_kernel,
        out_shape=jax.ShapeDtypeStruct((n1, 2 * n2, c), F32),
        grid=(n1 // ch, ncb),
        in_specs=[pl.BlockSpec((2, half, ch, cb), lambda j, k: (0, 0, j, z_col * ncb + k)),
                  pl.BlockSpec((ch, 2 * n2, n2), lambda j, k: (j, 0, 0))],
        out_specs=pl.BlockSpec((ch, 2 * n2, cb), lambda j, k: (j, 0, k)),
        compiler_params=_params(("parallel", "parallel"), big=True),
        name="fft_stage1",
    )(zsrc, f1_pad)
    nk = n2 // ch
    q = pl.pallas_call(
        _fft_mid_kernel,
        out_shape=jax.ShapeDtypeStruct((n2, 2 * n1, c), F32),
        grid=(nk, ncb),
        in_specs=[pl.BlockSpec((n1, ch, cb), lambda j, k: (0, j, k)),
                  pl.BlockSpec((n1, ch, cb), lambda j, k: (0, nk + j, k)),
                  pl.BlockSpec((2 * n1, 2 * n1), lambda j, k: (0, 0)),
                  pl.BlockSpec((ch, 2 * n1, cb), lambda j, k: (j, 0, k)),
                  pl.BlockSpec((ch, 2 * n1, 2 * n1), lambda j, k: (j, 0, 0))],
        out_specs=pl.BlockSpec((ch, 2 * n1, cb), lambda j, k: (j, 0, k)),
        compiler_params=_params(("parallel", "parallel"), big=True),
        name="fft_mid",
    )(y, y, f2, hf, g2)
    nj = n1 // ch
    return pl.pallas_call(
        _fft_last_kernel,
        out_shape=jax.ShapeDtypeStruct((2, half, n1, c), F32),
        grid=(nj, ncb),
        in_specs=[pl.BlockSpec((n2, ch, cb), lambda j, k: (0, j, k)),
                  pl.BlockSpec((n2, ch, cb), lambda j, k: (0, nj + j, k)),
                  pl.BlockSpec((2 * half, 2 * n2), lambda j, k: (0, 0)),
                  pl.BlockSpec((2, half, ch, cb), lambda j, k: (0, 0, j, gate_col * ncb + k)),
                  pl.BlockSpec((2, half, ch, cb), lambda j, k: (0, 0, j, z_col * ncb + k)),
                  pl.BlockSpec((1, cb), lambda j, k: (0, k))],
        out_specs=pl.BlockSpec((2, half, ch, cb), lambda j, k: (0, 0, j, k)),
        compiler_params=_params(("parallel", "parallel"), big=True),
        name="fft_last",
    )(q, q, g1, gate_src, zsrc, fbias.reshape(1, c))


def hyena_mixer(hn, h_res, w_in, b_in, conv_w, conv_b, fw1, fb1, fw2, fb2, freq, fw3, fbias, w_out,
                batch, seq):
    width = w_out.shape[0]
    assert batch == 2 and 2 * seq == FFT_N1 * FFT_N2
    proj = matmul([hn], w_in, bias=b_in, name="hyena_in")
    sc = short_conv(proj.reshape(batch, seq, 3 * width), conv_w, conv_b)
    consts = _dft_constants()
    ts = hyena_two_sided_filters(seq, fw1, fb1, fw2, fb2, freq, fw3, width)
    hf = hyena_filter_spectrum(ts, consts)
    sc4 = sc.reshape(batch, FFT_N2 // 2, FFT_N1, 3 * width)
    zf1 = hyena_long_conv_gate(sc4, 2, sc4, 0, hf[0], fbias[0], consts)
    zf2 = _long_conv_second(zf1, sc4, hf[1], fbias[1], consts)
    return matmul([zf2.reshape(batch * seq, width)], w_out, res=h_res, name="hyena_out")


def _long_conv_second(zf1, sc4, hf, fbias, consts, cb=512):
    f1_pad, _, f2, g2, g1 = consts
    n1, n2, ch = FFT_N1, FFT_N2, FFT_CHUNK
    c = hf.shape[-1]
    ncb = c // cb
    half = n2 // 2
    y = pl.pallas_call(
        _fft_stage1_kernel,
        out_shape=jax.ShapeDtypeStruct((n1, 2 * n2, c), F32),
        grid=(n1 // ch, ncb),
        in_specs=[pl.BlockSpec((2, half, ch, cb), lambda j, k: (0, 0, j, k)),
                  pl.BlockSpec((ch, 2 * n2, n2), lambda j, k: (j, 0, 0))],
        out_specs=pl.BlockSpec((ch, 2 * n2, cb), lambda j, k: (j, 0, k)),
        compiler_params=_params(("parallel", "parallel"), big=True),
        name="fft_stage1_b",
    )(zf1, f1_pad)
    nk = n2 // ch
    q = pl.pallas_call(
        _fft_mid_kernel,
        out_shape=jax.ShapeDtypeStruct((n2, 2 * n1, c), F32),
        grid=(nk, ncb),
        in_specs=[pl.BlockSpec((n1, ch, cb), lambda j, k: (0, j, k)),
                  pl.BlockSpec((n1, ch, cb), lambda j, k: (0, nk + j, k)),
                  pl.BlockSpec((2 * n1, 2 * n1), lambda j, k: (0, 0)),
                  pl.BlockSpec((ch, 2 * n1, cb), lambda j, k: (j, 0, k)),
                  pl.BlockSpec((ch, 2 * n1, 2 * n1), lambda j, k: (j, 0, 0))],
        out_specs=pl.BlockSpec((ch, 2 * n1, cb), lambda j, k: (j, 0, k)),
        compiler_params=_params(("parallel", "parallel"), big=True),
        name="fft_mid_b",
    )(y, y, f2, hf, g2)
    nj = n1 // ch
    return pl.pallas_call(
        _fft_last_kernel,
        out_shape=jax.ShapeDtypeStruct((2, half, n1, c), F32),
        grid=(nj, ncb),
        in_specs=[pl.BlockSpec((n2, ch, cb), lambda j, k: (0, j, k)),
                  pl.BlockSpec((n2, ch, cb), lambda j, k: (0, nj + j, k)),
                  pl.BlockSpec((2 * half, 2 * n2), lambda j, k: (0, 0)),
                  pl.BlockSpec((2, half, ch, cb), lambda j, k: (0, 0, j, ncb + k)),
                  pl.BlockSpec((2, half, ch, cb), lambda j, k: (0, 0, j, k)),
                  pl.BlockSpec((1, cb), lambda j, k: (0, k))],
        out_specs=pl.BlockSpec((2, half, ch, cb), lambda j, k: (0, 0, j, k)),
        compiler_params=_params(("parallel", "parallel"), big=True),
        name="fft_last_b",
    )(q, q, g1, sc4, zf1, fbias.reshape(1, c))


def moe_swiglu(hn2_packed, logits, wg, wu, wd):
    n = hn2_packed.shape[0]
    top_v, top_i = lax.top_k(logits, TOP_K)
    gates = jax.nn.softmax(top_v, axis=-1)
    e_flat = top_i.reshape(-1).astype(jnp.int32)
    g_flat = gates.reshape(-1)
    tok_flat = jnp.repeat(jnp.arange(n, dtype=jnp.int32), TOP_K)
    nk = n * TOP_K
    onehot = (e_flat[:, None] == jnp.arange(N_EXPERTS, dtype=jnp.int32)[None]).astype(jnp.int32)
    csum = jnp.cumsum(onehot, axis=0)
    rank = jnp.take_along_axis(csum, e_flat[:, None], axis=1)[:, 0] - 1
    counts = csum[-1]
    padded = ((counts + MOE_TILE - 1) // MOE_TILE) * MOE_TILE
    pad_end = jnp.cumsum(padded)
    pad_start = pad_end - padded
    dest = pad_start[e_flat] + rank
    p_rows = nk + N_EXPERTS * MOE_TILE
    row_tok = jnp.zeros((p_rows,), jnp.int32).at[dest].set(tok_flat)
    row_gate = jnp.zeros((p_rows,), F32).at[dest].set(g_flat)
    nt = p_rows // MOE_TILE
    tile_start = jnp.arange(nt, dtype=jnp.int32) * MOE_TILE
    tile_used = (tile_start < pad_end[-1]).astype(jnp.int32)
    tile_exp = jnp.minimum(jnp.searchsorted(pad_end, tile_start, side='right'), N_EXPERTS - 1).astype(jnp.int32)
    last_exp = tile_exp[jnp.maximum(jnp.sum(tile_used) - 1, 0)]
    tile_exp = jnp.where(tile_used > 0, tile_exp, last_exp)
    ys = swiglu(hn2_packed, wg, wu, wd, tile_exp, tile_used, row_gate[:, None], row_tok,
                name="swiglu_experts")
    return ys, dest.reshape(n, TOP_K)


def kernel(x, p, ln_mix, ln_ffn, ln_ple, final_norm, t5_bias, w_attn_in, w_attn_out, attn_sink, na_rpb, w_ffn_gate, w_ffn_up, w_ffn_down, w_hy_in, b_hy_in, w_hy_conv, b_hy_conv, w_hy_f1, b_hy_f1, w_hy_f2, b_hy_f2, hy_freq, w_hy_f3, hy_bias, w_hy_out, w_router, w_exp_gate, w_exp_up, w_exp_down, w_ple_proj, w_ple_gate):
    batch, seq, d = x.shape
    n = batch * seq
    depth = ln_mix.shape[0]
    h = x.reshape(n, d)
    for i in range(depth):
        li = i // 2
        hn = rmsnorm(h, ln_mix[i], BF16)
        if i % 2 == 0:
            na_off = A_WIDTH + 2 * A_KV_WIDTH
            w_in = w_attn_in[li]
            proj_a = matmul([hn], w_in[:, :na_off], out_dtype=BF16, tn=256, name="attn_in_a")
            proj_n = matmul([hn], w_in[:, na_off:], out_dtype=BF16, name="attn_in_n")
            oa = window_attention(proj_a.reshape(batch, seq, -1), t5_bias, attn_sink[li])
            ob = neighbourhood_attention(proj_n.reshape(batch, seq, -1), na_rpb[li])
            h = matmul([oa.reshape(n, A_WIDTH), ob.reshape(n, B_WIDTH)], w_attn_out[li], res=h,
                       name="attn_out")
            hn2 = rmsnorm(h, ln_ffn[i], BF16)
            nt = n // MOE_TILE
            delta = swiglu(hn2, w_ffn_gate, w_ffn_up, w_ffn_down,
                           jnp.full((nt,), li, jnp.int32), jnp.ones((nt,), jnp.int32),
                           jnp.ones((n, 1), F32), name="swiglu_dense")
            h, hn3 = add_rmsnorm(h, delta, ln_ple[i])
        else:
            h = hyena_mixer(hn, h, w_hy_in[li], b_hy_in[li], w_hy_conv[li], b_hy_conv[li],
                            w_hy_f1[li], b_hy_f1[li], w_hy_f2[li], b_hy_f2[li], hy_freq[li],
                            w_hy_f3[li], hy_bias[li], w_hy_out[li], batch, seq)
            wr_pad = jnp.pad(w_router[li].astype(F32), ((0, 0), (0, 128 - N_EXPERTS)))
            hn2_packed, logits = rmsnorm_router(h, ln_ffn[i], wr_pad)
            ys, dest2 = moe_swiglu(hn2_packed, logits[:, :N_EXPERTS], w_exp_gate[li], w_exp_up[li],
                                   w_exp_down[li])
            h, hn3 = moe_combine_rmsnorm(h, ys, dest2, ln_ple[i])
        h = ple(hn3, p[i].reshape(n, -1), h, w_ple_gate[i], w_ple_proj[i])
    return rmsnorm(h, final_norm, F32).reshape(batch, seq, d)
```

```python
import functools
import math

import jax
import jax.numpy as jnp
import numpy as np
from jax import lax
from jax.experimental import pallas as pl
from jax.experimental.pallas import tpu as pltpu

F32 = jnp.float32
BF16 = jnp.bfloat16
NEG_INF = -1e30
RMS_EPS = 1e-6

V7X_VMEM_LIMIT_BYTES = 56 * 1024 * 1024

HEAD_DIM = 64
A_Q_HEADS = 16
A_KV_HEADS = 2
A_GROUP = A_Q_HEADS // A_KV_HEADS
A_BLOCK = 128
T5_BUCKETS = 32
T5_MAX_DIST = 128
B_HEADS = 16
GRID_W = 64
NA_WIN_H = 8
NA_WIN_W = 16
A_WIDTH = A_Q_HEADS * HEAD_DIM
A_KV_WIDTH = A_KV_HEADS * HEAD_DIM
B_WIDTH = B_HEADS * HEAD_DIM
N_EXPERTS = 8
TOP_K = 2
HYENA_EMB = 33
HYENA_DECAY_TARGET = 1e-2
HYENA_FAST_PCT = 0.3
HYENA_SLOW_PCT = 1.5

FFT_N1 = 64
FFT_N2 = 128
FFT_CHUNK = 8

MOE_TILE = 1024
FFN_F_TILE = 256
DMA_LOOP_UNROLL = 8


def _params(semantics, big=False):
    return pltpu.CompilerParams(
        dimension_semantics=semantics,
        vmem_limit_bytes=V7X_VMEM_LIMIT_BYTES if big else None)


def _rmsnorm_kernel(x_ref, g_ref, o_ref):
    x = x_ref[...]
    y = x * lax.rsqrt(jnp.mean(x * x, axis=-1, keepdims=True) + RMS_EPS)
    o_ref[...] = (y * g_ref[...]).astype(o_ref.dtype)


def rmsnorm(x2, g, out_dtype, tm=512):
    n, d = x2.shape
    return pl.pallas_call(
        _rmsnorm_kernel,
        out_shape=jax.ShapeDtypeStruct((n, d), out_dtype),
        grid=(n // tm,),
        in_specs=[pl.BlockSpec((tm, d), lambda i: (i, 0)),
                  pl.BlockSpec((1, d), lambda i: (0, 0))],
        out_specs=pl.BlockSpec((tm, d), lambda i: (i, 0)),
        compiler_params=_params(("parallel",)),
        name="rmsnorm",
    )(x2, g.reshape(1, d))


def _add_rmsnorm_kernel(x_ref, d_ref, g_ref, h_ref, o_ref):
    x = x_ref[...] + d_ref[...]
    h_ref[...] = x
    y = x * lax.rsqrt(jnp.mean(x * x, axis=-1, keepdims=True) + RMS_EPS)
    o_ref[...] = (y * g_ref[...]).astype(o_ref.dtype)


def add_rmsnorm(x2, delta, g, tm=512):
    n, d = x2.shape
    return pl.pallas_call(
        _add_rmsnorm_kernel,
        out_shape=(jax.ShapeDtypeStruct((n, d), F32), jax.ShapeDtypeStruct((n, d), BF16)),
        grid=(n // tm,),
        in_specs=[pl.BlockSpec((tm, d), lambda i: (i, 0)),
                  pl.BlockSpec((tm, d), lambda i: (i, 0)),
                  pl.BlockSpec((1, d), lambda i: (0, 0))],
        out_specs=(pl.BlockSpec((tm, d), lambda i: (i, 0)),
                   pl.BlockSpec((tm, d), lambda i: (i, 0))),
        compiler_params=_params(("parallel",)),
        name="add_rmsnorm",
    )(x2, delta, g.reshape(1, d))


def _pack_bf16_pairs(y):
    w = y.shape[1] // 2
    bits = pltpu.bitcast(y.astype(BF16).astype(F32), jnp.uint32)
    return (bits[:, :w] >> 16) | (bits[:, w:] & jnp.uint32(0xFFFF0000))


def _unpack_bf16_pairs(p):
    lo = pltpu.bitcast(p << 16, F32)
    hi = pltpu.bitcast(p & jnp.uint32(0xFFFF0000), F32)
    return jnp.concatenate([lo, hi], axis=1).astype(BF16)


def _rmsnorm_router_kernel(x_ref, g_ref, wr_ref, o_ref, l_ref):
    x = x_ref[...]
    y = x * lax.rsqrt(jnp.mean(x * x, axis=-1, keepdims=True) + RMS_EPS)
    y = y * g_ref[...]
    o_ref[...] = _pack_bf16_pairs(y)
    l_ref[...] = jnp.dot(y, wr_ref[...], preferred_element_type=F32,
                         precision=lax.Precision.HIGHEST)


def rmsnorm_router(x2, g, w_router_pad, tm=512):
    n, d = x2.shape
    ne = w_router_pad.shape[1]
    return pl.pallas_call(
        _rmsnorm_router_kernel,
        out_shape=(jax.ShapeDtypeStruct((n, d // 2), jnp.uint32), jax.ShapeDtypeStruct((n, ne), F32)),
        grid=(n // tm,),
        in_specs=[pl.BlockSpec((tm, d), lambda i: (i, 0)),
                  pl.BlockSpec((1, d), lambda i: (0, 0)),
                  pl.BlockSpec((d, ne), lambda i: (0, 0))],
        out_specs=(pl.BlockSpec((tm, d // 2), lambda i: (i, 0)),
                   pl.BlockSpec((tm, ne), lambda i: (i, 0))),
        compiler_params=_params(("parallel",)),
        name="rmsnorm_router",
    )(x2, g.reshape(1, d), w_router_pad)


def _mm_kernel(*refs, n_a, has_bias, has_res):
    a_refs = refs[:n_a]
    w_refs = refs[n_a:2 * n_a]
    idx = 2 * n_a
    bias_ref = refs[idx] if has_bias else None
    idx += int(has_bias)
    res_ref = refs[idx] if has_res else None
    idx += int(has_res)
    o_ref = refs[idx]
    wbf_refs = refs[idx + 1:idx + 1 + n_a]

    @pl.when(pl.program_id(1) == 0)
    def _():
        for w_ref, wbf_ref in zip(w_refs, wbf_refs):
            wbf_ref[...] = w_ref[...].astype(BF16)

    acc = None
    for a_ref, wbf_ref in zip(a_refs, wbf_refs):
        d = jnp.dot(a_ref[...].astype(BF16), wbf_ref[...], preferred_element_type=F32)
        acc = d if acc is None else acc + d
    if has_bias:
        acc = acc + bias_ref[...]
    if has_res:
        acc = acc + res_ref[...]
    o_ref[...] = acc.astype(o_ref.dtype)


def matmul(a_list, w, *, bias=None, res=None, out_dtype=F32, tm=1024, tn=1024, name="matmul"):
    m = a_list[0].shape[0]
    n = w.shape[1]
    n_a = len(a_list)
    k_each = a_list[0].shape[1]
    assert all(a.shape == (m, k_each) for a in a_list) and w.shape[0] == n_a * k_each
    in_specs = [pl.BlockSpec((tm, k_each), lambda j, i: (i, 0)) for _ in a_list]
    in_specs += [pl.BlockSpec((k_each, tn), functools.partial(lambda j, i, kb: (kb, j), kb=kb))
                 for kb in range(n_a)]
    args = list(a_list) + [w] * n_a
    if bias is not None:
        in_specs.append(pl.BlockSpec((1, tn), lambda j, i: (0, j)))
        args.append(bias.reshape(1, n))
    if res is not None:
        in_specs.append(pl.BlockSpec((tm, tn), lambda j, i: (i, j)))
        args.append(res)
    return pl.pallas_call(
        functools.partial(_mm_kernel, n_a=n_a, has_bias=bias is not None, has_res=res is not None),
        out_shape=jax.ShapeDtypeStruct((m, n), out_dtype),
        grid=(n // tn, m // tm),
        in_specs=in_specs,
        out_specs=pl.BlockSpec((tm, tn), lambda j, i: (i, j)),
        scratch_shapes=[pltpu.VMEM((k_each, tn), BF16) for _ in a_list],
        compiler_params=_params(("parallel", "arbitrary"), big=True),
        name=name,
    )(*args)


def _ple_kernel(hn_ref, p_ref, h_ref, wg_ref, wp_ref, o_ref, wg_bf, wp_bf):
    @pl.when(pl.program_id(1) == 0)
    def _():
        wg_bf[...] = wg_ref[...].astype(BF16)
        wp_bf[...] = wp_ref[...].astype(BF16)

    a = jnp.dot(hn_ref[...], wg_bf[...], preferred_element_type=F32)
    pp = jnp.dot(p_ref[...].astype(BF16), wp_bf[...], preferred_element_type=F32)
    o_ref[...] = h_ref[...] + jax.nn.sigmoid(a) * pp


def ple(hn, p2, h2, w_gate, w_proj, tm=1024, tn=1024):
    m, d = h2.shape
    pd = p2.shape[1]
    return pl.pallas_call(
        _ple_kernel,
        out_shape=jax.ShapeDtypeStruct((m, d), F32),
        grid=(d // tn, m // tm),
        in_specs=[pl.BlockSpec((tm, d), lambda j, i: (i, 0)),
                  pl.BlockSpec((tm, pd), lambda j, i: (i, 0)),
                  pl.BlockSpec((tm, tn), lambda j, i: (i, j)),
                  pl.BlockSpec((d, tn), lambda j, i: (0, j)),
                  pl.BlockSpec((pd, tn), lambda j, i: (0, j))],
        out_specs=pl.BlockSpec((tm, tn), lambda j, i: (i, j)),
        scratch_shapes=[pltpu.VMEM((d, tn), BF16), pltpu.VMEM((pd, tn), BF16)],
        compiler_params=_params(("parallel", "arbitrary"), big=True),
        name="ple",
    )(hn, p2, h2, w_gate, w_proj)


def _row_copy(src_hbm, row, dst, i, sem):
    return pltpu.make_async_copy(src_hbm.at[pl.ds(row, 1), :], dst.at[pl.ds(i, 1), :], sem)


def _swiglu_kernel(exp_ref, used_ref, tok_ref, x_ref, wg_ref, wu_ref, wd_ref, gate_ref, o_ref,
                   *scratch, n_col, gather):
    del exp_ref
    t = pl.program_id(0)
    f = pl.program_id(1)
    nt = pl.num_programs(0)
    used = used_ref[t] > 0
    tm = o_ref.shape[0]

    @pl.when(f == 0)
    def _():
        o_ref[...] = jnp.zeros_like(o_ref)

    if gather:
        xbuf, xbf, sems = scratch
        slot = t % 2

        def start_tile(tile, sl):
            def body(i, c):
                _row_copy(x_ref, tok_ref[tile * tm + i], xbuf.at[sl], i, sems.at[sl, i]).start()
                return c
            lax.fori_loop(0, tm, body, 0, unroll=DMA_LOOP_UNROLL)

        @pl.when(jnp.logical_and(f == 0, jnp.logical_and(t == 0, used)))
        def _():
            start_tile(0, 0)

        @pl.when(jnp.logical_and(f == 0, used))
        def _():
            def body(i, c):
                _row_copy(x_ref, 0, xbuf.at[slot], i, sems.at[slot, i]).wait()
                return c
            lax.fori_loop(0, tm, body, 0, unroll=DMA_LOOP_UNROLL)
            xbf[...] = _unpack_bf16_pairs(xbuf[slot])

        nxt = jnp.minimum(t + 1, nt - 1)
        @pl.when(jnp.logical_and(f == 1, jnp.logical_and(t + 1 < nt, used_ref[nxt] > 0)))
        def _():
            start_tile(t + 1, 1 - slot)

        x_src = xbf
    else:
        x_src = x_ref

    @pl.when(used)
    def _():
        x = x_src[...]
        g = jnp.dot(x, wg_ref[...].astype(BF16), preferred_element_type=F32)
        u = jnp.dot(x, wu_ref[...].astype(BF16), preferred_element_type=F32)
        hmid = (g * jax.nn.sigmoid(g) * u).astype(BF16)
        wd = wd_ref[...].astype(BF16)
        cw = o_ref.shape[1] // n_col
        for c in range(n_col):
            y = jnp.dot(hmid, wd[:, c * cw:(c + 1) * cw], preferred_element_type=F32)
            o_ref[:, c * cw:(c + 1) * cw] += y

    @pl.when(jnp.logical_and(used, f == pl.num_programs(1) - 1))
    def _():
        o_ref[...] = o_ref[...] * gate_ref[...]


def swiglu(x, w_gate, w_up, w_down, tile_exp, tile_used, row_gate, row_src=None, *, tm=MOE_TILE,
           tf=FFN_F_TILE, name="swiglu"):
    gather = row_src is not None
    m = row_gate.shape[0]
    d = w_gate.shape[1]
    dff = w_gate.shape[2]
    nf = dff // tf
    nt = m // tm
    assert nf >= 2

    def w_in_map(t, f, exp_ref, used_ref, tok_ref):
        return (exp_ref[t], 0, jnp.where(used_ref[t] > 0, f, nf - 1))

    def w_out_map(t, f, exp_ref, used_ref, tok_ref):
        return (exp_ref[t], jnp.where(used_ref[t] > 0, f, nf - 1), 0)

    if gather:
        x_spec = pl.BlockSpec(memory_space=pl.ANY)
        scratch = [pltpu.VMEM((2, tm, d // 2), jnp.uint32), pltpu.VMEM((tm, d), BF16),
                   pltpu.SemaphoreType.DMA((2, tm))]
    else:
        x_spec = pl.BlockSpec((tm, d), lambda t, f, e, u, k: (t, 0))
        scratch = []
        row_src = jnp.zeros((1,), jnp.int32)
    grid_spec = pltpu.PrefetchScalarGridSpec(
        num_scalar_prefetch=3,
        grid=(nt, nf),
        in_specs=[x_spec,
                  pl.BlockSpec((None, d, tf), w_in_map),
                  pl.BlockSpec((None, d, tf), w_in_map),
                  pl.BlockSpec((None, tf, d), w_out_map),
                  pl.BlockSpec((tm, 1), lambda t, f, e, u, k: (t, 0))],
        out_specs=pl.BlockSpec((tm, d), lambda t, f, e, u, k: (t, 0)),
        scratch_shapes=scratch,
    )
    return pl.pallas_call(
        functools.partial(_swiglu_kernel, n_col=4, gather=gather),
        out_shape=jax.ShapeDtypeStruct((m, d), F32),
        grid_spec=grid_spec,
        compiler_params=_params(("arbitrary", "arbitrary"), big=True),
        name=name,
    )(tile_exp, tile_used, row_src, x, w_gate, w_up, w_down, row_gate)


COMBINE_TILE = 256


def _combine_kernel(d0_ref, d1_ref, ys_hbm, h_ref, g_ref, hout_ref, hn_ref, buf, sems):
    i = pl.program_id(0)
    n = pl.num_programs(0)
    tt = h_ref.shape[0]
    slot = i % 2
    dests = (d0_ref, d1_ref)

    def start_tile(tile, sl):
        def body(r, c):
            for k, d_ref in enumerate(dests):
                _row_copy(ys_hbm, d_ref[tile * tt + r], buf.at[sl, k], r, sems.at[sl, k, r]).start()
            return c
        lax.fori_loop(0, tt, body, 0, unroll=DMA_LOOP_UNROLL)

    @pl.when(i == 0)
    def _():
        start_tile(0, 0)

    @pl.when(i + 1 < n)
    def _():
        start_tile(i + 1, 1 - slot)

    def wait_body(r, c):
        for k in range(TOP_K):
            _row_copy(ys_hbm, 0, buf.at[slot, k], r, sems.at[slot, k, r]).wait()
        return c
    lax.fori_loop(0, tt, wait_body, 0, unroll=DMA_LOOP_UNROLL)

    x = h_ref[...] + (buf[slot, 0] + buf[slot, 1])
    hout_ref[...] = x
    y = x * lax.rsqrt(jnp.mean(x * x, axis=-1, keepdims=True) + RMS_EPS)
    hn_ref[...] = (y * g_ref[...]).astype(hn_ref.dtype)


def moe_combine_rmsnorm(h2, ys, dest2, g, tt=COMBINE_TILE):
    n, d = h2.shape
    grid_spec = pltpu.PrefetchScalarGridSpec(
        num_scalar_prefetch=2,
        grid=(n // tt,),
        in_specs=[pl.BlockSpec(memory_space=pl.ANY),
                  pl.BlockSpec((tt, d), lambda i, a, b: (i, 0)),
                  pl.BlockSpec((1, d), lambda i, a, b: (0, 0))],
        out_specs=(pl.BlockSpec((tt, d), lambda i, a, b: (i, 0)),
                   pl.BlockSpec((tt, d), lambda i, a, b: (i, 0))),
        scratch_shapes=[pltpu.VMEM((2, TOP_K, tt, d), F32), pltpu.SemaphoreType.DMA((2, TOP_K, tt))],
    )
    return pl.pallas_call(
        _combine_kernel,
        out_shape=(jax.ShapeDtypeStruct((n, d), F32), jax.ShapeDtypeStruct((n, d), BF16)),
        grid_spec=grid_spec,
        compiler_params=_params(("arbitrary",), big=True),
        name="moe_combine",
    )(dest2[:, 0], dest2[:, 1], ys, h2, g.reshape(1, d))


def _t5_bucket(rel):
    half = T5_BUCKETS // 2
    max_exact = half // 2
    n = jnp.abs(rel)
    log_ratio = jnp.log(jnp.maximum(n, 1).astype(F32) / max_exact) / math.log(T5_MAX_DIST / max_exact)
    large = jnp.minimum(max_exact + (log_ratio * (half - max_exact)).astype(jnp.int32), half - 1)
    return jnp.where(rel > 0, half, 0) + jnp.where(n < max_exact, n, large)


def _window_bias_table(t5_bias):
    i = jnp.arange(A_BLOCK)[:, None]
    j = jnp.arange(3 * A_BLOCK)[None, :]
    rel = j - A_BLOCK - i
    onehot = (_t5_bucket(rel)[None] == jnp.arange(T5_BUCKETS)[:, None, None]).astype(F32)
    bias = jnp.einsum('bh,bij->hij', t5_bias.astype(F32), onehot, precision=lax.Precision.HIGHEST)
    return jnp.where((jnp.abs(rel) <= A_BLOCK)[None], bias, NEG_INF)


def _window_kernel(sink_ref, q_ref, kv_ref, bias_ref, o_ref, *, nb):
    n = pl.program_id(1)
    scale = HEAD_DIM ** -0.5
    starts = (jnp.maximum(n - 1, 0), n, jnp.minimum(n + 1, nb - 1))
    kv = [kv_ref[pl.ds(pl.multiple_of(s * A_BLOCK, A_BLOCK), A_BLOCK), :] for s in starts]
    col = lax.broadcasted_iota(jnp.int32, (A_BLOCK, 3 * A_BLOCK), 1)
    edge_ok = jnp.logical_and(jnp.logical_or(n > 0, col >= A_BLOCK),
                              jnp.logical_or(n < nb - 1, col < 2 * A_BLOCK))
    q = q_ref[...]
    for kh in range(A_KV_HEADS):
        ks = [blk[:, kh * HEAD_DIM:(kh + 1) * HEAD_DIM] for blk in kv]
        v = jnp.concatenate(
            [blk[:, A_KV_WIDTH + kh * HEAD_DIM:A_KV_WIDTH + (kh + 1) * HEAD_DIM] for blk in kv], axis=0)
        for g in range(A_GROUP):
            h = kh * A_GROUP + g
            qh = q[:, h * HEAD_DIM:(h + 1) * HEAD_DIM]
            s = jnp.concatenate(
                [lax.dot_general(qh, kb, (((1,), (1,)), ((), ())), preferred_element_type=F32)
                 for kb in ks], axis=1)
            s = jnp.where(edge_ok, s * scale + bias_ref[h], NEG_INF)
            sink = sink_ref[h]
            mx = jnp.maximum(jnp.max(s, axis=-1, keepdims=True), sink)
            p = jnp.exp(s - mx)
            den = jnp.sum(p, axis=-1, keepdims=True) + jnp.exp(sink - mx)
            o = jnp.dot(p.astype(BF16), v, preferred_element_type=F32) / den
            o_ref[:, h * HEAD_DIM:(h + 1) * HEAD_DIM] = o.astype(o_ref.dtype)


def window_attention(proj_a, t5_bias, sink):
    b, s, _ = proj_a.shape
    nb = s // A_BLOCK
    kvw = 2 * A_KV_WIDTH
    grid_spec = pltpu.PrefetchScalarGridSpec(
        num_scalar_prefetch=0,
        grid=(b, nb),
        in_specs=[pl.BlockSpec(memory_space=pltpu.SMEM),
                  pl.BlockSpec((None, A_BLOCK, A_WIDTH), lambda bi, n: (bi, n, 0)),
                  pl.BlockSpec((None, s, kvw), lambda bi, n: (bi, 0, A_WIDTH // kvw)),
                  pl.BlockSpec((A_Q_HEADS, A_BLOCK, 3 * A_BLOCK), lambda bi, n: (0, 0, 0))],
        out_specs=pl.BlockSpec((None, A_BLOCK, A_WIDTH), lambda bi, n: (bi, n, 0)),
    )
    return pl.pallas_call(
        functools.partial(_window_kernel, nb=nb),
        out_shape=jax.ShapeDtypeStruct((b, s, A_WIDTH), BF16),
        grid_spec=grid_spec,
        compiler_params=_params(("parallel", "arbitrary")),
        name="window_attention",
    )(sink.astype(F32), proj_a, proj_a, _window_bias_table(t5_bias))


NA_HEAD_GROUP = 16


def _na_bias_table(rpb, rows):
    kh = min(NA_WIN_H, rows)
    kw = NA_WIN_W
    c = jnp.arange(GRID_W)
    cs = jnp.clip(c - kw // 2, 0, GRID_W - kw)
    col_ok = (c[None] >= cs[:, None]) & (c[None] < cs[:, None] + kw)
    col_off = jnp.clip(c[None] - c[:, None], -(kw - 1), kw - 1) + kw - 1
    onehot = (col_off[None] == jnp.arange(2 * kw - 1)[:, None, None]).astype(F32)
    by_col = jnp.einsum('hrc,cqk->hrqk', rpb.astype(F32), onehot, precision=lax.Precision.HIGHEST)
    by_col = jnp.where(col_ok[None, None], by_col, NEG_INF)
    tabs = [by_col[:, NA_WIN_H - 1 - d:NA_WIN_H - 1 - d + kh] for d in range(kh)]
    bias = jnp.stack(tabs, axis=0).transpose(0, 1, 3, 2, 4)
    return bias.reshape(kh, rpb.shape[0], GRID_W, kh * GRID_W)


def _na_kernel(q_ref, k_ref, v_ref, bias_ref, o_ref, *, rows):
    r = pl.program_id(2)
    kh = min(NA_WIN_H, rows)
    scale = HEAD_DIM ** -0.5
    start = pl.multiple_of(jnp.clip(r - kh // 2, 0, rows - kh) * GRID_W, GRID_W)
    k = k_ref[pl.ds(start, kh * GRID_W), :]
    v = v_ref[pl.ds(start, kh * GRID_W), :]
    q = q_ref[...]
    for h in range(NA_HEAD_GROUP):
        sl = slice(h * HEAD_DIM, (h + 1) * HEAD_DIM)
        s = lax.dot_general(q[:, sl], k[:, sl], (((1,), (1,)), ((), ())), preferred_element_type=F32)
        s = s * scale + bias_ref[h]
        mx = jnp.max(s, axis=-1, keepdims=True)
        p = jnp.exp(s - mx)
        den = jnp.sum(p, axis=-1, keepdims=True)
        o = jnp.dot(p.astype(BF16), v[:, sl], preferred_element_type=F32) / den
        o_ref[:, sl] = o.astype(o_ref.dtype)


def neighbourhood_attention(proj_n, rpb):
    b, s, _ = proj_n.shape
    rows = s // GRID_W
    kh = min(NA_WIN_H, rows)
    gw = NA_HEAD_GROUP * HEAD_DIM
    ng = B_WIDTH // gw

    def bias_map(bi, g, r):
        return (r - jnp.clip(r - kh // 2, 0, rows - kh), g, 0, 0)

    return pl.pallas_call(
        functools.partial(_na_kernel, rows=rows),
        out_shape=jax.ShapeDtypeStruct((b, s, B_WIDTH), BF16),
        grid=(b, ng, rows),
        in_specs=[pl.BlockSpec((None, GRID_W, gw), lambda bi, g, r: (bi, r, g)),
                  pl.BlockSpec((None, s, gw), lambda bi, g, r: (bi, 0, ng + g)),
                  pl.BlockSpec((None, s, gw), lambda bi, g, r: (bi, 0, 2 * ng + g)),
                  pl.BlockSpec((None, NA_HEAD_GROUP, GRID_W, kh * GRID_W), bias_map)],
        out_specs=pl.BlockSpec((None, GRID_W, gw), lambda bi, g, r: (bi, r, g)),
        compiler_params=_params(("parallel", "parallel", "arbitrary"), big=True),
        name="neighbourhood_attention",
    )(proj_n, proj_n, proj_n, _na_bias_table(rpb, rows))


def _short_conv_kernel(prev_ref, cur_ref, next_ref, w_ref, b_ref, o_ref):
    i = pl.program_id(1)
    nt = pl.num_programs(1)
    x = cur_ref[...]
    tl = x.shape[0]
    row = lax.broadcasted_iota(jnp.int32, x.shape, 0)
    prev_row = jnp.where(i > 0, prev_ref[7:8, :], 0.0)
    next_row = jnp.where(i < nt - 1, next_ref[0:1, :], 0.0)
    up = jnp.where(row == 0, prev_row, pltpu.roll(x, 1, 0))
    dn = jnp.where(row == tl - 1, next_row, pltpu.roll(x, tl - 1, 0))
    o_ref[...] = up * w_ref[0:1, :] + x * w_ref[1:2, :] + dn * w_ref[2:3, :] + b_ref[...]


def short_conv(proj, conv_w, conv_b, tl=512, tc=1024):
    b, l, c = proj.shape
    nl8 = l // 8
    return pl.pallas_call(
        _short_conv_kernel,
        out_shape=jax.ShapeDtypeStruct((b, l, c), F32),
        grid=(b, l // tl, c // tc),
        in_specs=[pl.BlockSpec((None, 8, tc), lambda bi, i, j: (bi, jnp.maximum(i * (tl // 8) - 1, 0), j)),
                  pl.BlockSpec((None, tl, tc), lambda bi, i, j: (bi, i, j)),
                  pl.BlockSpec((None, 8, tc), lambda bi, i, j: (bi, jnp.minimum((i + 1) * (tl // 8), nl8 - 1), j)),
                  pl.BlockSpec((3, tc), lambda bi, i, j: (0, j)),
                  pl.BlockSpec((1, tc), lambda bi, i, j: (0, j))],
        out_specs=pl.BlockSpec((None, tl, tc), lambda bi, i, j: (bi, i, j)),
        compiler_params=_params(("parallel", "parallel", "parallel")),
        name="short_conv",
    )(proj, proj, proj, conv_w, conv_b.reshape(1, c))


def _filter_kernel(z_ref, t_ref, fw1_ref, fb1_ref, fw2_ref, fb2_ref, fr_ref, w3_ref, w3b_ref,
                   delta_ref, o_ref, *, seq):
    m = pl.program_id(1)
    hi = lax.Precision.HIGHEST
    fr = fr_ref[...]
    hid = jnp.sin(fr * (jnp.dot(z_ref[...], fw1_ref[...], preferred_element_type=F32, precision=hi)
                        + fb1_ref[...]))
    hid = jnp.sin(fr * (jnp.dot(hid, fw2_ref[...], preferred_element_type=F32, precision=hi)
                        + fb2_ref[...]))
    hid_bf = hid.astype(BF16)
    h = jnp.dot(hid_bf, w3_ref[...].astype(BF16), preferred_element_type=F32)
    tm = h.shape[0]
    row = lax.broadcasted_iota(jnp.int32, h.shape, 0) + m * tm
    h_back = jnp.dot(hid_bf, w3b_ref[...].astype(BF16), preferred_element_type=F32)
    h = h + jnp.where(row == 0, h_back, 0.0)
    decay = jnp.exp(-t_ref[...] * delta_ref[...])
    o_ref[...] = jnp.where(row == seq, 0.0, h * decay)


def hyena_two_sided_filters(seq, fw1, fb1, fw2, fb2, freq, fw3, width, tm=512):
    n = 2 * seq
    u = jnp.arange(n)
    pos = jnp.where(u <= seq, u, n - u)
    pos = jnp.where(u == seq, 0, pos)
    t_all = jnp.linspace(0.0, 1.0, seq, dtype=F32)
    bands = (HYENA_EMB - 1) // 2
    w_all = (2.0 * math.pi / seq) * jnp.arange(seq, dtype=F32)
    f = jnp.linspace(1e-4, bands - 1, bands, dtype=F32)[None]
    t = t_all[pos][:, None]
    w = w_all[pos][:, None]
    z = jnp.concatenate([t, jnp.cos(f * w), -jnp.sin(f * w)], axis=-1)
    emb_pad = 40
    z = jnp.pad(z, ((0, 0), (0, emb_pad - HYENA_EMB)))
    fw1p = jnp.pad(fw1.astype(F32), ((0, emb_pad - HYENA_EMB), (0, 0)))
    fh = fw1.shape[1]
    max_decay = math.log(HYENA_DECAY_TARGET) / HYENA_FAST_PCT
    min_decay = math.log(HYENA_DECAY_TARGET) / HYENA_SLOW_PCT
    deltas = jnp.abs(jnp.linspace(min_decay, max_decay, width, dtype=F32))[None]
    half_tiles = seq // tm

    def w3_map(o, m):
        return (0, jnp.where(m >= half_tiles, 2, 0) + o)

    return pl.pallas_call(
        functools.partial(_filter_kernel, seq=seq),
        out_shape=jax.ShapeDtypeStruct((2, n, width), F32),
        grid=(2, n // tm),
        in_specs=[pl.BlockSpec((tm, emb_pad), lambda o, m: (m, 0)),
                  pl.BlockSpec((tm, 1), lambda o, m: (m, 0)),
                  pl.BlockSpec((emb_pad, fh), lambda o, m: (0, 0)),
                  pl.BlockSpec((1, fh), lambda o, m: (0, 0)),
                  pl.BlockSpec((fh, fh), lambda o, m: (0, 0)),
                  pl.BlockSpec((1, fh), lambda o, m: (0, 0)),
                  pl.BlockSpec((1, fh), lambda o, m: (0, 0)),
                  pl.BlockSpec((fh, width), w3_map),
                  pl.BlockSpec((fh, width), lambda o, m: (0, 2 + o)),
                  pl.BlockSpec((1, width), lambda o, m: (0, 0))],
        out_specs=pl.BlockSpec((None, tm, width), lambda o, m: (o, m, 0)),
        compiler_params=_params(("parallel", "arbitrary")),
        name="hyena_filters",
    )(z, t, fw1p, fb1.reshape(1, fh).astype(F32), fw2.astype(F32), fb2.reshape(1, fh).astype(F32),
      freq.reshape(1, fh).astype(F32), fw3, fw3, deltas)


def _dft_constants():
    n1, n2 = FFT_N1, FFT_N2
    n = n1 * n2
    a1 = np.arange(n1)
    a2 = np.arange(n2)
    half = n2 // 2

    def cplx_block(w):
        return np.block([[w.real, -w.imag], [w.imag, w.real]])

    ang = (a2[None, :, None] * a2[None, None, :] / n2) + (a1[:, None, None] * a2[None, :, None] / n)
    w1 = np.exp(-2j * np.pi * ang)
    f1_pad = np.stack([cplx_block(w1[i][:, :half]) for i in range(n1)])
    f1_real = np.stack([np.concatenate([w1[i].real, w1[i].imag], axis=0) for i in range(n1)])
    w2 = np.exp(-2j * np.pi * (a1[:, None] * a1[None, :]) / n1)
    f2 = cplx_block(w2)
    ang = (a1[None, :, None] * a1[None, None, :] / n1) + (a2[:, None, None] * a1[None, :, None] / n)
    g2 = np.stack([cplx_block(m_) for m_ in np.exp(2j * np.pi * ang)])
    wg1 = np.exp(2j * np.pi * (a2[:half, None] * a2[None, :]) / n2) / n
    g1 = cplx_block(wg1)
    to = lambda x: jnp.asarray(x.astype(np.float32)).astype(BF16)
    return to(f1_pad), to(f1_real), to(f2), to(g2), to(g1)


def _fft_stage1_kernel(x_ref, f_ref, y_ref):
    xr = pltpu.einshape("mjc->jmc", x_ref[0])
    xi = pltpu.einshape("mjc->jmc", x_ref[1])
    for j in range(FFT_CHUNK):
        xc = jnp.concatenate([xr[j], xi[j]], axis=0).astype(BF16)
        y_ref[j] = jnp.dot(f_ref[j], xc, preferred_element_type=F32)


def _fft_stage1_real_kernel(x_ref, f_ref, y_ref):
    x = pltpu.einshape("mjc->jmc", x_ref[...])
    for j in range(FFT_CHUNK):
        y_ref[j] = jnp.dot(f_ref[j], x[j].astype(BF16), preferred_element_type=F32)


def _fft_stage2_kernel(yr_ref, yi_ref, f2_ref, h_ref):
    yr = pltpu.einshape("nkc->knc", yr_ref[...])
    yi = pltpu.einshape("nkc->knc", yi_ref[...])
    for j in range(FFT_CHUNK):
        yc = jnp.concatenate([yr[j], yi[j]], axis=0).astype(BF16)
        h_ref[j] = jnp.dot(f2_ref[...], yc, preferred_element_type=F32)


def _fft_mid_kernel(yr_ref, yi_ref, f2_ref, h_ref, g2_ref, q_ref):
    yr = pltpu.einshape("nkc->knc", yr_ref[...])
    yi = pltpu.einshape("nkc->knc", yi_ref[...])
    n1 = FFT_N1
    for j in range(FFT_CHUNK):
        yc = jnp.concatenate([yr[j], yi[j]], axis=0).astype(BF16)
        z = jnp.dot(f2_ref[...], yc, preferred_element_type=F32)
        hf = h_ref[j]
        zr, zi, hr, hi = z[:n1], z[n1:], hf[:n1], hf[n1:]
        pc = jnp.concatenate([zr * hr - zi * hi, zr * hi + zi * hr], axis=0).astype(BF16)
        q_ref[j] = jnp.dot(g2_ref[j], pc, preferred_element_type=F32)


def _fft_last_kernel(qr_ref, qi_ref, g1_ref, gate_ref, zf_ref, fb_ref, o_ref):
    qr = pltpu.einshape("kjc->jkc", qr_ref[...])
    qi = pltpu.einshape("kjc->jkc", qi_ref[...])
    half = FFT_N2 // 2
    conv = []
    for j in range(FFT_CHUNK):
        qc = jnp.concatenate([qr[j], qi[j]], axis=0).astype(BF16)
        conv.append(jnp.dot(g1_ref[...], qc, preferred_element_type=F32))
    conv = jnp.stack(conv, axis=0)
    fb = fb_ref[...]
    for bi in range(2):
        cb = pltpu.einshape("jmc->mjc", conv[:, bi * half:(bi + 1) * half])
        zf = zf_ref[bi]
        o_ref[bi] = gate_ref[bi] * (cb + zf * fb)


def hyena_filter_spectrum(ts, consts, cb=512):
    _, f1_real, f2, _, _ = consts
    no, n, c = ts.shape
    n1, n2, ch = FFT_N1, FFT_N2, FFT_CHUNK
    ts4 = ts.reshape(no, n2, n1, c)
    y = pl.pallas_call(
        _fft_stage1_real_kernel,
        out_shape=jax.ShapeDtypeStruct((no, n1, 2 * n2, c), F32),
        grid=(no, n1 // ch, c // cb),
        in_specs=[pl.BlockSpec((None, n2, ch, cb), lambda o, j, k: (o, 0, j, k)),
                  pl.BlockSpec((ch, 2 * n2, n2), lambda o, j, k: (j, 0, 0))],
        out_specs=pl.BlockSpec((None, ch, 2 * n2, cb), lambda o, j, k: (o, j, 0, k)),
        compiler_params=_params(("parallel", "parallel", "parallel"), big=True),
        name="fft_filter_stage1",
    )(ts4, f1_real)
    nk = n2 // ch
    return pl.pallas_call(
        _fft_stage2_kernel,
        out_shape=jax.ShapeDtypeStruct((no, n2, 2 * n1, c), F32),
        grid=(no, nk, c // cb),
        in_specs=[pl.BlockSpec((None, n1, ch, cb), lambda o, j, k: (o, 0, j, k)),
                  pl.BlockSpec((None, n1, ch, cb), lambda o, j, k: (o, 0, nk + j, k)),
                  pl.BlockSpec((2 * n1, 2 * n1), lambda o, j, k: (0, 0))],
        out_specs=pl.BlockSpec((None, ch, 2 * n1, cb), lambda o, j, k: (o, j, 0, k)),
        compiler_params=_params(("parallel", "parallel", "parallel"), big=True),
        name="fft_filter_stage2",
    )(y, y, f2)


def hyena_long_conv_gate(zsrc, z_col, gate_src, gate_col, hf, order, fbias, consts, cb=512):
    f1_pad, _, f2, g2, g1 = consts
    n1, n2, ch = FFT_N1, FFT_N2, FFT_CHUNK
    c = hf.shape[-1]
    ncb = c // cb
    half = n2 // 2
    y = pl.pallas_call(
        _fft_stage1_kernel,
        out_shape=jax.ShapeDtypeStruct((n1, 2 * n2, c), F32),
        grid=(n1 // ch, ncb),
        in_specs=[pl.BlockSpec((2, half, ch, cb), lambda j, k: (0, 0, j, z_col * ncb + k)),
                  pl.BlockSpec((ch, 2 * n2, n2), lambda j, k: (j, 0, 0))],
        out_specs=pl.BlockSpec((ch, 2 * n2, cb), lambda j, k: (j, 0, k)),
        compiler_params=_params(("parallel", "parallel"), big=True),
        name="fft_stage1",
    )(zsrc, f1_pad)
    nk = n2 // ch
    q = pl.pallas_call(
        _fft_mid_kernel,
        out_shape=jax.ShapeDtypeStruct((n2, 2 * n1, c), F32),
        grid=(nk, ncb),
        in_specs=[pl.BlockSpec((n1, ch, cb), lambda j, k: (0, j, k)),
                  pl.BlockSpec((n1, ch, cb), lambda j, k: (0, nk + j, k)),
                  pl.BlockSpec((2 * n1, 2 * n1), lambda j, k: (0, 0)),
                  pl.BlockSpec((None, ch, 2 * n1, cb), lambda j, k: (order, j, 0, k)),
                  pl.BlockSpec((ch, 2 * n1, 2 * n1), lambda j, k: (j, 0, 0))],
        out_specs=pl.BlockSpec((ch, 2 * n1, cb), lambda j, k: (j, 0, k)),
        compiler_params=_params(("parallel", "parallel"), big=True),
        name="fft_mid",
    )(y, y, f2, hf, g2)
    nj = n1 // ch
    return pl.pallas_call(
        _fft_last_kernel,
        out_shape=jax.ShapeDtypeStruct((2, half, n1, c), F32),
        grid=(nj, ncb),
        in_specs=[pl.BlockSpec((n2, ch, cb), lambda j, k: (0, j, k)),
                  pl.BlockSpec((n2, ch, cb), lambda j, k: (0, nj + j, k)),
                  pl.BlockSpec((2 * half, 2 * n2), lambda j, k: (0, 0)),
                  pl.BlockSpec((2, half, ch, cb), lambda j, k: (0, 0, j, gate_col * ncb + k)),
                  pl.BlockSpec((2, half, ch, cb), lambda j, k: (0, 0, j, z_col * ncb + k)),
                  pl.BlockSpec((None, 1, cb), lambda j, k: (order, 0, k))],
        out_specs=pl.BlockSpec((2, half, ch, cb), lambda j, k: (0, 0, j, k)),
        compiler_params=_params(("parallel", "parallel"), big=True),
        name="fft_last",
    )(q, q, g1, gate_src, zsrc, fbias.reshape(fbias.shape[0], 1, c))


def hyena_mixer(hn, h_res, w_in, b_in, conv_w, conv_b, fw1, fb1, fw2, fb2, freq, fw3, fbias, w_out,
                batch, seq):
    width = w_out.shape[0]
    assert batch == 2 and 2 * seq == FFT_N1 * FFT_N2
    proj = matmul([hn], w_in, bias=b_in, name="hyena_in")
    sc = short_conv(proj.reshape(batch, seq, 3 * width), conv_w, conv_b)
    consts = _dft_constants()
    ts = hyena_two_sided_filters(seq, fw1, fb1, fw2, fb2, freq, fw3, width)
    hf = hyena_filter_spectrum(ts, consts)
    sc4 = sc.reshape(batch, FFT_N2 // 2, FFT_N1, 3 * width)
    zf1 = hyena_long_conv_gate(sc4, 2, sc4, 0, hf, 0, fbias, consts)
    zf2 = hyena_long_conv_gate(zf1, 0, sc4, 1, hf, 1, fbias, consts)
    return matmul([zf2.reshape(batch * seq, width)], w_out, res=h_res, tm=512, tn=1024,
                  name="hyena_out")


def moe_swiglu(hn2_packed, logits, wg, wu, wd):
    n = hn2_packed.shape[0]
    top_v, top_i = lax.top_k(logits, TOP_K)
    gates = jax.nn.softmax(top_v, axis=-1)
    e_flat = top_i.reshape(-1).astype(jnp.int32)
    g_flat = gates.reshape(-1)
    tok_flat = jnp.repeat(jnp.arange(n, dtype=jnp.int32), TOP_K)
    nk = n * TOP_K
    onehot = (e_flat[:, None] == jnp.arange(N_EXPERTS, dtype=jnp.int32)[None]).astype(jnp.int32)
    csum = jnp.cumsum(onehot, axis=0)
    rank = jnp.take_along_axis(csum, e_flat[:, None], axis=1)[:, 0] - 1
    counts = csum[-1]
    padded = ((counts + MOE_TILE - 1) // MOE_TILE) * MOE_TILE
    pad_end = jnp.cumsum(padded)
    pad_start = pad_end - padded
    dest = pad_start[e_flat] + rank
    p_rows = nk + N_EXPERTS * MOE_TILE
    row_tok = jnp.zeros((p_rows,), jnp.int32).at[dest].set(tok_flat)
    row_gate = jnp.zeros((p_rows,), F32).at[dest].set(g_flat)
    nt = p_rows // MOE_TILE
    tile_start = jnp.arange(nt, dtype=jnp.int32) * MOE_TILE
    tile_used = (tile_start < pad_end[-1]).astype(jnp.int32)
    tile_exp = jnp.minimum(jnp.searchsorted(pad_end, tile_start, side='right'), N_EXPERTS - 1).astype(jnp.int32)
    last_exp = tile_exp[jnp.maximum(jnp.sum(tile_used) - 1, 0)]
    tile_exp = jnp.where(tile_used > 0, tile_exp, last_exp)
    ys = swiglu(hn2_packed, wg, wu, wd, tile_exp, tile_used, row_gate[:, None], row_tok,
                name="swiglu_experts")
    return ys, dest.reshape(n, TOP_K)


def kernel(x, p, ln_mix, ln_ffn, ln_ple, final_norm, t5_bias, w_attn_in, w_attn_out, attn_sink, na_rpb, w_ffn_gate, w_ffn_up, w_ffn_down, w_hy_in, b_hy_in, w_hy_conv, b_hy_conv, w_hy_f1, b_hy_f1, w_hy_f2, b_hy_f2, hy_freq, w_hy_f3, hy_bias, w_hy_out, w_router, w_exp_gate, w_exp_up, w_exp_down, w_ple_proj, w_ple_gate):
    batch, seq, d = x.shape
    n = batch * seq
    depth = ln_mix.shape[0]
    h = x.reshape(n, d)
    for i in range(depth):
        li = i // 2
        hn = rmsnorm(h, ln_mix[i], BF16)
        if i % 2 == 0:
            na_off = A_WIDTH + 2 * A_KV_WIDTH
            w_in = w_attn_in[li]
            proj_a = matmul([hn], w_in[:, :na_off], out_dtype=BF16, tn=640, name="attn_in_a")
            proj_n = matmul([hn], w_in[:, na_off:], out_dtype=BF16, name="attn_in_n")
            oa = window_attention(proj_a.reshape(batch, seq, -1), t5_bias, attn_sink[li])
            ob = neighbourhood_attention(proj_n.reshape(batch, seq, -1), na_rpb[li])
            h = matmul([oa.reshape(n, A_WIDTH), ob.reshape(n, B_WIDTH)], w_attn_out[li], res=h,
                       name="attn_out")
            hn2 = rmsnorm(h, ln_ffn[i], BF16)
            nt = n // MOE_TILE
            delta = swiglu(hn2, w_ffn_gate, w_ffn_up, w_ffn_down,
                           jnp.full((nt,), li, jnp.int32), jnp.ones((nt,), jnp.int32),
                           jnp.ones((n, 1), F32), name="swiglu_dense")
            h, hn3 = add_rmsnorm(h, delta, ln_ple[i])
        else:
            h = hyena_mixer(hn, h, w_hy_in[li], b_hy_in[li], w_hy_conv[li], b_hy_conv[li],
                            w_hy_f1[li], b_hy_f1[li], w_hy_f2[li], b_hy_f2[li], hy_freq[li],
                            w_hy_f3[li], hy_bias[li], w_hy_out[li], batch, seq)
            wr_pad = jnp.pad(w_router[li].astype(F32), ((0, 0), (0, 128 - N_EXPERTS)))
            hn2_packed, logits = rmsnorm_router(h, ln_ffn[i], wr_pad)
            ys, dest2 = moe_swiglu(hn2_packed, logits[:, :N_EXPERTS], w_exp_gate[li], w_exp_up[li],
                                   w_exp_down[li])
            h, hn3 = moe_combine_rmsnorm(h, ys, dest2, ln_ple[i])
        h = ple(hn3, p[i].reshape(n, -1), h, w_ple_gate[i], w_ple_proj[i])
    return rmsnorm(h, final_norm, F32).reshape(batch, seq, d)
```

```python
import functools
import math

import jax
import jax.numpy as jnp
import numpy as np
from jax import lax
from jax.experimental import pallas as pl
from jax.experimental.pallas import tpu as pltpu

F32 = jnp.float32
BF16 = jnp.bfloat16
NEG_INF = -1e30
RMS_EPS = 1e-6

V7X_VMEM_LIMIT_BYTES = 56 * 1024 * 1024

HEAD_DIM = 64
A_Q_HEADS = 16
A_KV_HEADS = 2
A_GROUP = A_Q_HEADS // A_KV_HEADS
A_BLOCK = 128
T5_BUCKETS = 32
T5_MAX_DIST = 128
B_HEADS = 16
GRID_W = 64
NA_WIN_H = 8
NA_WIN_W = 16
A_WIDTH = A_Q_HEADS * HEAD_DIM
A_KV_WIDTH = A_KV_HEADS * HEAD_DIM
B_WIDTH = B_HEADS * HEAD_DIM
N_EXPERTS = 8
TOP_K = 2
HYENA_EMB = 33
HYENA_DECAY_TARGET = 1e-2
HYENA_FAST_PCT = 0.3
HYENA_SLOW_PCT = 1.5

FFT_N1 = 64
FFT_N2 = 128
FFT_CHUNK = 8

MOE_TILE = 1024
FFN_F_TILE = 256
FFN_SUB_ROWS = 512
DMA_LOOP_UNROLL = 8


def _params(semantics, big=False):
    return pltpu.CompilerParams(
        dimension_semantics=semantics,
        vmem_limit_bytes=V7X_VMEM_LIMIT_BYTES if big else None)


def _rmsnorm_kernel(x_ref, g_ref, o_ref):
    x = x_ref[...]
    y = x * lax.rsqrt(jnp.mean(x * x, axis=-1, keepdims=True) + RMS_EPS)
    o_ref[...] = (y * g_ref[...]).astype(o_ref.dtype)


def rmsnorm(x2, g, out_dtype, tm=512):
    n, d = x2.shape
    return pl.pallas_call(
        _rmsnorm_kernel,
        out_shape=jax.ShapeDtypeStruct((n, d), out_dtype),
        grid=(n // tm,),
        in_specs=[pl.BlockSpec((tm, d), lambda i: (i, 0)),
                  pl.BlockSpec((1, d), lambda i: (0, 0))],
        out_specs=pl.BlockSpec((tm, d), lambda i: (i, 0)),
        compiler_params=_params(("parallel",)),
        name="rmsnorm",
    )(x2, g.reshape(1, d))


def _add_rmsnorm_kernel(x_ref, d_ref, g_ref, h_ref, o_ref):
    x = x_ref[...] + d_ref[...]
    h_ref[...] = x
    y = x * lax.rsqrt(jnp.mean(x * x, axis=-1, keepdims=True) + RMS_EPS)
    o_ref[...] = (y * g_ref[...]).astype(o_ref.dtype)


def add_rmsnorm(x2, delta, g, tm=512):
    n, d = x2.shape
    return pl.pallas_call(
        _add_rmsnorm_kernel,
        out_shape=(jax.ShapeDtypeStruct((n, d), F32), jax.ShapeDtypeStruct((n, d), BF16)),
        grid=(n // tm,),
        in_specs=[pl.BlockSpec((tm, d), lambda i: (i, 0)),
                  pl.BlockSpec((tm, d), lambda i: (i, 0)),
                  pl.BlockSpec((1, d), lambda i: (0, 0))],
        out_specs=(pl.BlockSpec((tm, d), lambda i: (i, 0)),
                   pl.BlockSpec((tm, d), lambda i: (i, 0))),
        compiler_params=_params(("parallel",)),
        name="add_rmsnorm",
    )(x2, delta, g.reshape(1, d))


def _pack_bf16_pairs(y):
    w = y.shape[1] // 2
    bits = pltpu.bitcast(y.astype(BF16).astype(F32), jnp.uint32)
    return (bits[:, :w] >> 16) | (bits[:, w:] & jnp.uint32(0xFFFF0000))


def _unpack_bf16_pairs(p):
    lo = pltpu.bitcast(p << 16, F32)
    hi = pltpu.bitcast(p & jnp.uint32(0xFFFF0000), F32)
    return jnp.concatenate([lo, hi], axis=1).astype(BF16)


def _rmsnorm_router_kernel(x_ref, g_ref, wr_ref, o_ref, l_ref):
    x = x_ref[...]
    y = x * lax.rsqrt(jnp.mean(x * x, axis=-1, keepdims=True) + RMS_EPS)
    y = y * g_ref[...]
    o_ref[...] = _pack_bf16_pairs(y)
    l_ref[...] = jnp.dot(y, wr_ref[...], preferred_element_type=F32,
                         precision=lax.Precision.HIGHEST)


def rmsnorm_router(x2, g, w_router_pad, tm=512):
    n, d = x2.shape
    ne = w_router_pad.shape[1]
    return pl.pallas_call(
        _rmsnorm_router_kernel,
        out_shape=(jax.ShapeDtypeStruct((n, d // 2), jnp.uint32), jax.ShapeDtypeStruct((n, ne), F32)),
        grid=(n // tm,),
        in_specs=[pl.BlockSpec((tm, d), lambda i: (i, 0)),
                  pl.BlockSpec((1, d), lambda i: (0, 0)),
                  pl.BlockSpec((d, ne), lambda i: (0, 0))],
        out_specs=(pl.BlockSpec((tm, d // 2), lambda i: (i, 0)),
                   pl.BlockSpec((tm, ne), lambda i: (i, 0))),
        compiler_params=_params(("parallel",)),
        name="rmsnorm_router",
    )(x2, g.reshape(1, d), w_router_pad)


def _mm_kernel(*refs, n_a, has_bias, has_res):
    a_refs = refs[:n_a]
    w_refs = refs[n_a:2 * n_a]
    idx = 2 * n_a
    bias_ref = refs[idx] if has_bias else None
    idx += int(has_bias)
    res_ref = refs[idx] if has_res else None
    idx += int(has_res)
    o_ref = refs[idx]
    wbf_refs = refs[idx + 1:idx + 1 + n_a]

    @pl.when(pl.program_id(1) == 0)
    def _():
        for w_ref, wbf_ref in zip(w_refs, wbf_refs):
            wbf_ref[...] = w_ref[...].astype(BF16)

    acc = None
    for a_ref, wbf_ref in zip(a_refs, wbf_refs):
        d = jnp.dot(a_ref[...].astype(BF16), wbf_ref[...], preferred_element_type=F32)
        acc = d if acc is None else acc + d
    if has_bias:
        acc = acc + bias_ref[...]
    if has_res:
        acc = acc + res_ref[...]
    o_ref[...] = acc.astype(o_ref.dtype)


def matmul(a_list, w, *, bias=None, res=None, out_dtype=F32, tm=1024, tn=1024, name="matmul"):
    m = a_list[0].shape[0]
    n = w.shape[1]
    n_a = len(a_list)
    k_each = a_list[0].shape[1]
    assert all(a.shape == (m, k_each) for a in a_list) and w.shape[0] == n_a * k_each
    in_specs = [pl.BlockSpec((tm, k_each), lambda j, i: (i, 0)) for _ in a_list]
    in_specs += [pl.BlockSpec((k_each, tn), functools.partial(lambda j, i, kb: (kb, j), kb=kb))
                 for kb in range(n_a)]
    args = list(a_list) + [w] * n_a
    if bias is not None:
        in_specs.append(pl.BlockSpec((1, tn), lambda j, i: (0, j)))
        args.append(bias.reshape(1, n))
    if res is not None:
        in_specs.append(pl.BlockSpec((tm, tn), lambda j, i: (i, j)))
        args.append(res)
    return pl.pallas_call(
        functools.partial(_mm_kernel, n_a=n_a, has_bias=bias is not None, has_res=res is not None),
        out_shape=jax.ShapeDtypeStruct((m, n), out_dtype),
        grid=(n // tn, m // tm),
        in_specs=in_specs,
        out_specs=pl.BlockSpec((tm, tn), lambda j, i: (i, j)),
        scratch_shapes=[pltpu.VMEM((k_each, tn), BF16) for _ in a_list],
        compiler_params=_params(("parallel", "arbitrary"), big=True),
        name=name,
    )(*args)


def _ple_kernel(hn_ref, p_ref, h_ref, wg_ref, wp_ref, o_ref, wg_bf, wp_bf):
    @pl.when(pl.program_id(1) == 0)
    def _():
        wg_bf[...] = wg_ref[...].astype(BF16)
        wp_bf[...] = wp_ref[...].astype(BF16)

    a = jnp.dot(hn_ref[...], wg_bf[...], preferred_element_type=F32)
    pp = jnp.dot(p_ref[...].astype(BF16), wp_bf[...], preferred_element_type=F32)
    o_ref[...] = h_ref[...] + jax.nn.sigmoid(a) * pp


def ple(hn, p2, h2, w_gate, w_proj, tm=1024, tn=1024):
    m, d = h2.shape
    pd = p2.shape[1]
    return pl.pallas_call(
        _ple_kernel,
        out_shape=jax.ShapeDtypeStruct((m, d), F32),
        grid=(d // tn, m // tm),
        in_specs=[pl.BlockSpec((tm, d), lambda j, i: (i, 0)),
                  pl.BlockSpec((tm, pd), lambda j, i: (i, 0)),
                  pl.BlockSpec((tm, tn), lambda j, i: (i, j)),
                  pl.BlockSpec((d, tn), lambda j, i: (0, j)),
                  pl.BlockSpec((pd, tn), lambda j, i: (0, j))],
        out_specs=pl.BlockSpec((tm, tn), lambda j, i: (i, j)),
        scratch_shapes=[pltpu.VMEM((d, tn), BF16), pltpu.VMEM((pd, tn), BF16)],
        compiler_params=_params(("parallel", "arbitrary"), big=True),
        name="ple",
    )(hn, p2, h2, w_gate, w_proj)


def _row_copy(src_hbm, row, dst, i, sem):
    return pltpu.make_async_copy(src_hbm.at[pl.ds(row, 1), :], dst.at[pl.ds(i, 1), :], sem)


def _swiglu_kernel(exp_ref, rows_ref, tok_ref, x_ref, wg_ref, wu_ref, wd_ref, gate_ref, o_ref,
                   *scratch, n_col, gather):
    del exp_ref
    t = pl.program_id(0)
    f = pl.program_id(1)
    nt = pl.num_programs(0)
    n_rows = rows_ref[t]
    used = n_rows > 0
    tm = o_ref.shape[0]

    @pl.when(f == 0)
    def _():
        o_ref[...] = jnp.zeros_like(o_ref)

    if gather:
        xbuf, xbf, sems = scratch
        slot = t % 2

        def for_each_row(tile, fn):
            def body(c, carry):
                for r in range(DMA_LOOP_UNROLL):
                    fn(c * DMA_LOOP_UNROLL + r)
                return carry
            lax.fori_loop(0, (rows_ref[tile] + DMA_LOOP_UNROLL - 1) // DMA_LOOP_UNROLL, body, 0)

        def start_tile(tile, sl):
            for_each_row(tile, lambda i: _row_copy(x_ref, tok_ref[tile * tm + i], xbuf.at[sl], i,
                                                   sems.at[sl, i]).start())

        @pl.when(jnp.logical_and(f == 0, t == 0))
        def _():
            xbuf[...] = jnp.zeros_like(xbuf)
            start_tile(0, 0)

        @pl.when(jnp.logical_and(f == 0, used))
        def _():
            for_each_row(t, lambda i: _row_copy(x_ref, 0, xbuf.at[slot], i, sems.at[slot, i]).wait())
            xbf[...] = _unpack_bf16_pairs(xbuf[slot])

        @pl.when(jnp.logical_and(f == 1, t + 1 < nt))
        def _():
            start_tile(jnp.minimum(t + 1, nt - 1), 1 - slot)

        x_src = xbf
    else:
        x_src = x_ref

    for sb in range(tm // FFN_SUB_ROWS):
        rs = slice(sb * FFN_SUB_ROWS, (sb + 1) * FFN_SUB_ROWS)

        @pl.when(sb * FFN_SUB_ROWS < n_rows)
        def _():
            x = x_src[rs, :]
            g = jnp.dot(x, wg_ref[...].astype(BF16), preferred_element_type=F32)
            u = jnp.dot(x, wu_ref[...].astype(BF16), preferred_element_type=F32)
            hmid = (g * jax.nn.sigmoid(g) * u).astype(BF16)
            wd = wd_ref[...].astype(BF16)
            cw = o_ref.shape[1] // n_col
            for c in range(n_col):
                y = jnp.dot(hmid, wd[:, c * cw:(c + 1) * cw], preferred_element_type=F32)
                o_ref[rs, c * cw:(c + 1) * cw] += y

    @pl.when(jnp.logical_and(used, f == pl.num_programs(1) - 1))
    def _():
        o_ref[...] = o_ref[...] * gate_ref[...]


def swiglu(x, w_gate, w_up, w_down, tile_exp, tile_rows, row_gate, row_src=None, *, tm=MOE_TILE,
           tf=FFN_F_TILE, name="swiglu"):
    gather = row_src is not None
    m = row_gate.shape[0]
    d = w_gate.shape[1]
    dff = w_gate.shape[2]
    nf = dff // tf
    nt = m // tm
    assert nf >= 2

    def w_in_map(t, f, exp_ref, used_ref, tok_ref):
        return (exp_ref[t], 0, jnp.where(used_ref[t] > 0, f, nf - 1))

    def w_out_map(t, f, exp_ref, used_ref, tok_ref):
        return (exp_ref[t], jnp.where(used_ref[t] > 0, f, nf - 1), 0)

    if gather:
        x_spec = pl.BlockSpec(memory_space=pl.ANY)
        scratch = [pltpu.VMEM((2, tm, d // 2), jnp.uint32), pltpu.VMEM((tm, d), BF16),
                   pltpu.SemaphoreType.DMA((2, tm))]
    else:
        x_spec = pl.BlockSpec((tm, d), lambda t, f, e, u, k: (t, 0))
        scratch = []
        row_src = jnp.zeros((1,), jnp.int32)
    grid_spec = pltpu.PrefetchScalarGridSpec(
        num_scalar_prefetch=3,
        grid=(nt, nf),
        in_specs=[x_spec,
                  pl.BlockSpec((None, d, tf), w_in_map),
                  pl.BlockSpec((None, d, tf), w_in_map),
                  pl.BlockSpec((None, tf, d), w_out_map),
                  pl.BlockSpec((tm, 1), lambda t, f, e, u, k: (t, 0))],
        out_specs=pl.BlockSpec((tm, d), lambda t, f, e, u, k: (t, 0)),
        scratch_shapes=scratch,
    )
    return pl.pallas_call(
        functools.partial(_swiglu_kernel, n_col=4, gather=gather),
        out_shape=jax.ShapeDtypeStruct((m, d), F32),
        grid_spec=grid_spec,
        compiler_params=_params(("arbitrary", "arbitrary"), big=True),
        name=name,
    )(tile_exp, tile_rows, row_src, x, w_gate, w_up, w_down, row_gate)


COMBINE_TILE = 256


def _combine_kernel(d0_ref, d1_ref, ys_hbm, h_ref, g_ref, hout_ref, hn_ref, buf, sems):
    i = pl.program_id(0)
    n = pl.num_programs(0)
    tt = h_ref.shape[0]
    slot = i % 2
    dests = (d0_ref, d1_ref)

    def start_tile(tile, sl):
        def body(r, c):
            for k, d_ref in enumerate(dests):
                _row_copy(ys_hbm, d_ref[tile * tt + r], buf.at[sl, k], r, sems.at[sl, k, r]).start()
            return c
        lax.fori_loop(0, tt, body, 0, unroll=DMA_LOOP_UNROLL)

    @pl.when(i == 0)
    def _():
        start_tile(0, 0)

    @pl.when(i + 1 < n)
    def _():
        start_tile(i + 1, 1 - slot)

    def wait_body(r, c):
        for k in range(TOP_K):
            _row_copy(ys_hbm, 0, buf.at[slot, k], r, sems.at[slot, k, r]).wait()
        return c
    lax.fori_loop(0, tt, wait_body, 0, unroll=DMA_LOOP_UNROLL)

    x = h_ref[...] + (buf[slot, 0] + buf[slot, 1])
    hout_ref[...] = x
    y = x * lax.rsqrt(jnp.mean(x * x, axis=-1, keepdims=True) + RMS_EPS)
    hn_ref[...] = (y * g_ref[...]).astype(hn_ref.dtype)


def moe_combine_rmsnorm(h2, ys, dest2, g, tt=COMBINE_TILE):
    n, d = h2.shape
    grid_spec = pltpu.PrefetchScalarGridSpec(
        num_scalar_prefetch=2,
        grid=(n // tt,),
        in_specs=[pl.BlockSpec(memory_space=pl.ANY),
                  pl.BlockSpec((tt, d), lambda i, a, b: (i, 0)),
                  pl.BlockSpec((1, d), lambda i, a, b: (0, 0))],
        out_specs=(pl.BlockSpec((tt, d), lambda i, a, b: (i, 0)),
                   pl.BlockSpec((tt, d), lambda i, a, b: (i, 0))),
        scratch_shapes=[pltpu.VMEM((2, TOP_K, tt, d), F32), pltpu.SemaphoreType.DMA((2, TOP_K, tt))],
    )
    return pl.pallas_call(
        _combine_kernel,
        out_shape=(jax.ShapeDtypeStruct((n, d), F32), jax.ShapeDtypeStruct((n, d), BF16)),
        grid_spec=grid_spec,
        compiler_params=_params(("arbitrary",), big=True),
        name="moe_combine",
    )(dest2[:, 0], dest2[:, 1], ys, h2, g.reshape(1, d))


def _t5_bucket(rel):
    half = T5_BUCKETS // 2
    max_exact = half // 2
    n = jnp.abs(rel)
    log_ratio = jnp.log(jnp.maximum(n, 1).astype(F32) / max_exact) / math.log(T5_MAX_DIST / max_exact)
    large = jnp.minimum(max_exact + (log_ratio * (half - max_exact)).astype(jnp.int32), half - 1)
    return jnp.where(rel > 0, half, 0) + jnp.where(n < max_exact, n, large)


def _window_bias_table(t5_bias):
    i = jnp.arange(A_BLOCK)[:, None]
    j = jnp.arange(3 * A_BLOCK)[None, :]
    rel = j - A_BLOCK - i
    onehot = (_t5_bucket(rel)[None] == jnp.arange(T5_BUCKETS)[:, None, None]).astype(F32)
    bias = jnp.einsum('bh,bij->hij', t5_bias.astype(F32), onehot, precision=lax.Precision.HIGHEST)
    return jnp.where((jnp.abs(rel) <= A_BLOCK)[None], bias, NEG_INF)


def _window_kernel(sink_ref, q_ref, kv_ref, bias_ref, o_ref, *, nb):
    n = pl.program_id(1)
    scale = HEAD_DIM ** -0.5
    starts = (jnp.maximum(n - 1, 0), n, jnp.minimum(n + 1, nb - 1))
    kv = [kv_ref[pl.ds(pl.multiple_of(s * A_BLOCK, A_BLOCK), A_BLOCK), :] for s in starts]
    col = lax.broadcasted_iota(jnp.int32, (A_BLOCK, 3 * A_BLOCK), 1)
    edge_ok = jnp.logical_and(jnp.logical_or(n > 0, col >= A_BLOCK),
                              jnp.logical_or(n < nb - 1, col < 2 * A_BLOCK))
    kv = jnp.concatenate(kv, axis=0)
    q = q_ref[...]
    lo = lax.broadcasted_iota(jnp.int32, (1, 2 * HEAD_DIM), 1) < HEAD_DIM
    sel = (jnp.where(lo, scale, 0.0).astype(BF16), jnp.where(lo, 0.0, scale).astype(BF16))
    pairs_per_kv = A_GROUP // 2
    kk, vv = [], []
    for kh in range(A_KV_HEADS):
        k1 = kv[:, kh * HEAD_DIM:(kh + 1) * HEAD_DIM]
        v1 = kv[:, A_KV_WIDTH + kh * HEAD_DIM:A_KV_WIDTH + (kh + 1) * HEAD_DIM]
        kk.append(jnp.concatenate([k1, k1], axis=1))
        vv.append(jnp.concatenate([v1, v1], axis=1))
    scores = []
    for pr in range(A_Q_HEADS // 2):
        q2 = q[:, pr * 2 * HEAD_DIM:(pr + 1) * 2 * HEAD_DIM]
        qq = jnp.concatenate([q2 * sel[0], q2 * sel[1]], axis=0)
        s2 = lax.dot_general(qq, kk[pr // pairs_per_kv], (((1,), (1,)), ((), ())),
                             preferred_element_type=F32)
        for i in range(2):
            scores.append(jnp.where(edge_ok, s2[i * A_BLOCK:(i + 1) * A_BLOCK] + bias_ref[2 * pr + i],
                                    NEG_INF))
    probs, dens = [], []
    for h, s in enumerate(scores):
        sink = sink_ref[h]
        mx = jnp.maximum(jnp.max(s, axis=-1, keepdims=True), sink)
        p = jnp.exp(s - mx)
        dens.append(jnp.sum(p, axis=-1, keepdims=True) + jnp.exp(sink - mx))
        probs.append(p.astype(BF16))
    for pr in range(A_Q_HEADS // 2):
        v2 = vv[pr // pairs_per_kv]
        oa = jnp.dot(probs[2 * pr], v2, preferred_element_type=F32) / dens[2 * pr]
        ob = jnp.dot(probs[2 * pr + 1], v2, preferred_element_type=F32) / dens[2 * pr + 1]
        o_ref[:, pr * 2 * HEAD_DIM:(pr + 1) * 2 * HEAD_DIM] = jnp.where(lo, oa, ob).astype(o_ref.dtype)


def window_attention(proj_a, t5_bias, sink):
    b, s, _ = proj_a.shape
    nb = s // A_BLOCK
    kvw = 2 * A_KV_WIDTH
    grid_spec = pltpu.PrefetchScalarGridSpec(
        num_scalar_prefetch=0,
        grid=(b, nb),
        in_specs=[pl.BlockSpec(memory_space=pltpu.SMEM),
                  pl.BlockSpec((None, A_BLOCK, A_WIDTH), lambda bi, n: (bi, n, 0)),
                  pl.BlockSpec((None, s, kvw), lambda bi, n: (bi, 0, A_WIDTH // kvw)),
                  pl.BlockSpec((A_Q_HEADS, A_BLOCK, 3 * A_BLOCK), lambda bi, n: (0, 0, 0))],
        out_specs=pl.BlockSpec((None, A_BLOCK, A_WIDTH), lambda bi, n: (bi, n, 0)),
    )
    return pl.pallas_call(
        functools.partial(_window_kernel, nb=nb),
        out_shape=jax.ShapeDtypeStruct((b, s, A_WIDTH), BF16),
        grid_spec=grid_spec,
        compiler_params=_params(("parallel", "arbitrary")),
        name="window_attention",
    )(sink.astype(F32), proj_a, proj_a, _window_bias_table(t5_bias))


NA_HEAD_GROUP = 16


def _na_bias_table(rpb, rows):
    kh = min(NA_WIN_H, rows)
    kw = NA_WIN_W
    c = jnp.arange(GRID_W)
    cs = jnp.clip(c - kw // 2, 0, GRID_W - kw)
    col_ok = (c[None] >= cs[:, None]) & (c[None] < cs[:, None] + kw)
    col_off = jnp.clip(c[None] - c[:, None], -(kw - 1), kw - 1) + kw - 1
    onehot = (col_off[None] == jnp.arange(2 * kw - 1)[:, None, None]).astype(F32)
    by_col = jnp.einsum('hrc,cqk->hrqk', rpb.astype(F32), onehot, precision=lax.Precision.HIGHEST)
    by_col = jnp.where(col_ok[None, None], by_col, NEG_INF)
    tabs = [by_col[:, NA_WIN_H - 1 - d:NA_WIN_H - 1 - d + kh] for d in range(kh)]
    bias = jnp.stack(tabs, axis=0).transpose(0, 1, 3, 2, 4)
    return bias.reshape(kh, rpb.shape[0], GRID_W, kh * GRID_W)


def _na_kernel(q_ref, k_ref, v_ref, bias_ref, o_ref, *, rows):
    r = pl.program_id(2)
    kh = min(NA_WIN_H, rows)
    scale = HEAD_DIM ** -0.5
    start = pl.multiple_of(jnp.clip(r - kh // 2, 0, rows - kh) * GRID_W, GRID_W)
    k = k_ref[pl.ds(start, kh * GRID_W), :]
    v = v_ref[pl.ds(start, kh * GRID_W), :]
    q = q_ref[...]
    lo = lax.broadcasted_iota(jnp.int32, (1, 2 * HEAD_DIM), 1) < HEAD_DIM
    sel = (jnp.where(lo, scale, 0.0).astype(BF16), jnp.where(lo, 0.0, scale).astype(BF16))
    n_pairs = NA_HEAD_GROUP // 2
    scores = []
    for pr in range(n_pairs):
        sl = slice(pr * 2 * HEAD_DIM, (pr + 1) * 2 * HEAD_DIM)
        q2 = q[:, sl]
        qq = jnp.concatenate([q2 * sel[0], q2 * sel[1]], axis=0)
        s2 = lax.dot_general(qq, k[:, sl], (((1,), (1,)), ((), ())), preferred_element_type=F32)
        for i in range(2):
            scores.append(s2[i * GRID_W:(i + 1) * GRID_W] + bias_ref[2 * pr + i])
    probs, dens = [], []
    for s in scores:
        mx = jnp.max(s, axis=-1, keepdims=True)
        p = jnp.exp(s - mx)
        dens.append(jnp.sum(p, axis=-1, keepdims=True))
        probs.append(p.astype(BF16))
    for pr in range(n_pairs):
        sl = slice(pr * 2 * HEAD_DIM, (pr + 1) * 2 * HEAD_DIM)
        oa = jnp.dot(probs[2 * pr], v[:, sl], preferred_element_type=F32) / dens[2 * pr]
        ob = jnp.dot(probs[2 * pr + 1], v[:, sl], preferred_element_type=F32) / dens[2 * pr + 1]
        o_ref[:, sl] = jnp.where(lo, oa, ob).astype(o_ref.dtype)


def neighbourhood_attention(proj_n, rpb):
    b, s, _ = proj_n.shape
    rows = s // GRID_W
    kh = min(NA_WIN_H, rows)
    gw = NA_HEAD_GROUP * HEAD_DIM
    ng = B_WIDTH // gw

    def bias_map(bi, g, r):
        return (r - jnp.clip(r - kh // 2, 0, rows - kh), g, 0, 0)

    return pl.pallas_call(
        functools.partial(_na_kernel, rows=rows),
        out_shape=jax.ShapeDtypeStruct((b, s, B_WIDTH), BF16),
        grid=(b, ng, rows),
        in_specs=[pl.BlockSpec((None, GRID_W, gw), lambda bi, g, r: (bi, r, g)),
                  pl.BlockSpec((None, s, gw), lambda bi, g, r: (bi, 0, ng + g)),
                  pl.BlockSpec((None, s, gw), lambda bi, g, r: (bi, 0, 2 * ng + g)),
                  pl.BlockSpec((None, NA_HEAD_GROUP, GRID_W, kh * GRID_W), bias_map)],
        out_specs=pl.BlockSpec((None, GRID_W, gw), lambda bi, g, r: (bi, r, g)),
        compiler_params=_params(("parallel", "parallel", "arbitrary"), big=True),
        name="neighbourhood_attention",
    )(proj_n, proj_n, proj_n, _na_bias_table(rpb, rows))


def _short_conv_kernel(prev_ref, cur_ref, next_ref, w_ref, b_ref, o_ref):
    i = pl.program_id(1)
    nt = pl.num_programs(1)
    x = cur_ref[...]
    tl = x.shape[0]
    row = lax.broadcasted_iota(jnp.int32, x.shape, 0)
    prev_row = jnp.where(i > 0, prev_ref[7:8, :], 0.0)
    next_row = jnp.where(i < nt - 1, next_ref[0:1, :], 0.0)
    up = jnp.where(row == 0, prev_row, pltpu.roll(x, 1, 0))
    dn = jnp.where(row == tl - 1, next_row, pltpu.roll(x, tl - 1, 0))
    o_ref[...] = up * w_ref[0:1, :] + x * w_ref[1:2, :] + dn * w_ref[2:3, :] + b_ref[...]


def short_conv(proj, conv_w, conv_b, tl=512, tc=1024):
    b, l, c = proj.shape
    nl8 = l // 8
    return pl.pallas_call(
        _short_conv_kernel,
        out_shape=jax.ShapeDtypeStruct((b, l, c), F32),
        grid=(b, l // tl, c // tc),
        in_specs=[pl.BlockSpec((None, 8, tc), lambda bi, i, j: (bi, jnp.maximum(i * (tl // 8) - 1, 0), j)),
                  pl.BlockSpec((None, tl, tc), lambda bi, i, j: (bi, i, j)),
                  pl.BlockSpec((None, 8, tc), lambda bi, i, j: (bi, jnp.minimum((i + 1) * (tl // 8), nl8 - 1), j)),
                  pl.BlockSpec((3, tc), lambda bi, i, j: (0, j)),
                  pl.BlockSpec((1, tc), lambda bi, i, j: (0, j))],
        out_specs=pl.BlockSpec((None, tl, tc), lambda bi, i, j: (bi, i, j)),
        compiler_params=_params(("parallel", "parallel", "parallel")),
        name="short_conv",
    )(proj, proj, proj, conv_w, conv_b.reshape(1, c))


def _filter_kernel(z_ref, t_ref, fw1_ref, fb1_ref, fw2_ref, fb2_ref, fr_ref, w3_ref, w3b_ref,
                   delta_ref, o_ref, *, seq):
    m = pl.program_id(0)
    hi = lax.Precision.HIGHEST
    fr = fr_ref[...]
    hid = jnp.sin(fr * (jnp.dot(z_ref[...], fw1_ref[...], preferred_element_type=F32, precision=hi)
                        + fb1_ref[...]))
    hid = jnp.sin(fr * (jnp.dot(hid, fw2_ref[...], preferred_element_type=F32, precision=hi)
                        + fb2_ref[...]))
    hid_bf = hid.astype(BF16)
    h = jnp.dot(hid_bf, w3_ref[...].astype(BF16), preferred_element_type=F32)
    tm = h.shape[0]
    width = delta_ref.shape[1]
    row = lax.broadcasted_iota(jnp.int32, (tm, width), 0) + m * tm
    decay = jnp.where(row == seq, 0.0, jnp.exp(-t_ref[...] * delta_ref[...]))
    for o in range(o_ref.shape[0]):
        o_ref[o] = h[:, o * width:(o + 1) * width] * decay

    @pl.when(m == 0)
    def _():
        h_back = jnp.dot(hid_bf[0:8], w3b_ref[...].astype(BF16), preferred_element_type=F32)
        first = lax.broadcasted_iota(jnp.int32, (8, width), 0) == 0
        for o in range(o_ref.shape[0]):
            o_ref[o, 0:8, :] = o_ref[o, 0:8, :] + jnp.where(first, h_back[:, o * width:(o + 1) * width], 0.0)


def hyena_two_sided_filters(seq, fw1, fb1, fw2, fb2, freq, fw3, width, tm=512):
    n = 2 * seq
    u = jnp.arange(n)
    pos = jnp.where(u <= seq, u, n - u)
    pos = jnp.where(u == seq, 0, pos)
    bands = (HYENA_EMB - 1) // 2
    f = jnp.linspace(1e-4, bands - 1, bands, dtype=F32)[None]
    t = (pos.astype(F32) / (seq - 1))[:, None]
    w = ((2.0 * math.pi / seq) * pos.astype(F32))[:, None]
    z = jnp.concatenate([t, jnp.cos(f * w), -jnp.sin(f * w)], axis=-1)
    emb_pad = 40
    z = jnp.pad(z, ((0, 0), (0, emb_pad - HYENA_EMB)))
    fw1p = jnp.pad(fw1.astype(F32), ((0, emb_pad - HYENA_EMB), (0, 0)))
    fh = fw1.shape[1]
    max_decay = math.log(HYENA_DECAY_TARGET) / HYENA_FAST_PCT
    min_decay = math.log(HYENA_DECAY_TARGET) / HYENA_SLOW_PCT
    deltas = jnp.abs(jnp.linspace(min_decay, max_decay, width, dtype=F32))[None]
    half_tiles = seq // tm
    orders = fw3.shape[1] // (2 * width)
    ow = orders * width

    return pl.pallas_call(
        functools.partial(_filter_kernel, seq=seq),
        out_shape=jax.ShapeDtypeStruct((orders, n, width), F32),
        grid=(n // tm,),
        in_specs=[pl.BlockSpec((tm, emb_pad), lambda m: (m, 0)),
                  pl.BlockSpec((tm, 1), lambda m: (m, 0)),
                  pl.BlockSpec((emb_pad, fh), lambda m: (0, 0)),
                  pl.BlockSpec((1, fh), lambda m: (0, 0)),
                  pl.BlockSpec((fh, fh), lambda m: (0, 0)),
                  pl.BlockSpec((1, fh), lambda m: (0, 0)),
                  pl.BlockSpec((1, fh), lambda m: (0, 0)),
                  pl.BlockSpec((fh, ow), lambda m: (0, jnp.where(m >= half_tiles, 1, 0))),
                  pl.BlockSpec((fh, ow), lambda m: (0, 1)),
                  pl.BlockSpec((1, width), lambda m: (0, 0))],
        out_specs=pl.BlockSpec((orders, tm, width), lambda m: (0, m, 0)),
        compiler_params=_params(("arbitrary",), big=True),
        name="hyena_filters",
    )(z, t, fw1p, fb1.reshape(1, fh).astype(F32), fw2.astype(F32), fb2.reshape(1, fh).astype(F32),
      freq.reshape(1, fh).astype(F32), fw3, fw3, deltas)


def _dft_constants():
    n1, n2 = FFT_N1, FFT_N2
    n = n1 * n2
    a1 = np.arange(n1)
    a2 = np.arange(n2)
    half = n2 // 2

    def cplx_block(w):
        return np.block([[w.real, -w.imag], [w.imag, w.real]])

    ang = (a2[None, :, None] * a2[None, None, :] / n2) + (a1[:, None, None] * a2[None, :, None] / n)
    w1 = np.exp(-2j * np.pi * ang)
    f1_pad = np.stack([cplx_block(w1[i][:, :half]) for i in range(n1)])
    f1_real = np.stack([np.concatenate([w1[i].real, w1[i].imag], axis=0) for i in range(n1)])
    w2 = np.exp(-2j * np.pi * (a1[:, None] * a1[None, :]) / n1)
    f2 = cplx_block(w2)
    ang = (a1[None, :, None] * a1[None, None, :] / n1) + (a2[:, None, None] * a1[None, :, None] / n)
    g2 = np.stack([cplx_block(m_) for m_ in np.exp(2j * np.pi * ang)])
    wg1 = np.exp(2j * np.pi * (a2[:half, None] * a2[None, :]) / n2) / n
    g1 = cplx_block(wg1)
    to = lambda x: jnp.asarray(x.astype(np.float32)).astype(BF16)
    return to(f1_pad), to(f1_real), to(f2), to(g2), to(g1)


def _fft_stage1_kernel(x_ref, f_ref, y_ref):
    xr = pltpu.einshape("mjc->jmc", x_ref[0])
    xi = pltpu.einshape("mjc->jmc", x_ref[1])
    for j in range(FFT_CHUNK):
        xc = jnp.concatenate([xr[j], xi[j]], axis=0).astype(BF16)
        y_ref[j] = _pack_bf16_pairs(jnp.dot(f_ref[j], xc, preferred_element_type=F32))


def _fft_stage1_real_kernel(x_ref, f_ref, y_ref):
    x = pltpu.einshape("mjc->jmc", x_ref[...])
    for j in range(FFT_CHUNK):
        y_ref[j] = _pack_bf16_pairs(jnp.dot(f_ref[j], x[j].astype(BF16), preferred_element_type=F32))


def _fft_stage2_kernel(yr_ref, yi_ref, f2_ref, h_ref):
    yr = pltpu.einshape("nkc->knc", yr_ref[...])
    yi = pltpu.einshape("nkc->knc", yi_ref[...])
    for j in range(FFT_CHUNK):
        yc = _unpack_bf16_pairs(jnp.concatenate([yr[j], yi[j]], axis=0))
        h_ref[j] = jnp.dot(f2_ref[...], yc, preferred_element_type=F32)


def _fft_mid_kernel(yr_ref, yi_ref, f2_ref, h_ref, g2_ref, q_ref):
    yr = pltpu.einshape("nkc->knc", yr_ref[...])
    yi = pltpu.einshape("nkc->knc", yi_ref[...])
    n1 = FFT_N1
    for j in range(FFT_CHUNK):
        yc = _unpack_bf16_pairs(jnp.concatenate([yr[j], yi[j]], axis=0))
        z = jnp.dot(f2_ref[...], yc, preferred_element_type=F32)
        hf = h_ref[j]
        zr, zi, hr, hi = z[:n1], z[n1:], hf[:n1], hf[n1:]
        pc = jnp.concatenate([zr * hr - zi * hi, zr * hi + zi * hr], axis=0).astype(BF16)
        q_ref[j] = _pack_bf16_pairs(jnp.dot(g2_ref[j], pc, preferred_element_type=F32))


def _fft_last_kernel(qr_ref, qi_ref, g1_ref, gate_ref, zf_ref, fb_ref, o_ref):
    qr = pltpu.einshape("kjc->jkc", qr_ref[...])
    qi = pltpu.einshape("kjc->jkc", qi_ref[...])
    half = FFT_N2 // 2
    conv = []
    for j in range(FFT_CHUNK):
        qc = _unpack_bf16_pairs(jnp.concatenate([qr[j], qi[j]], axis=0))
        conv.append(jnp.dot(g1_ref[...], qc, preferred_element_type=F32))
    conv = jnp.stack(conv, axis=0)
    fb = fb_ref[...]
    for bi in range(2):
        cb = pltpu.einshape("jmc->mjc", conv[:, bi * half:(bi + 1) * half])
        zf = zf_ref[bi]
        o_ref[bi] = gate_ref[bi] * (cb + zf * fb)


def hyena_filter_spectrum(ts, consts, cb=512):
    _, f1_real, f2, _, _ = consts
    no, n, c = ts.shape
    n1, n2, ch = FFT_N1, FFT_N2, FFT_CHUNK
    ts4 = ts.reshape(no, n2, n1, c)
    y = pl.pallas_call(
        _fft_stage1_real_kernel,
        out_shape=jax.ShapeDtypeStruct((no, n1, 2 * n2, c // 2), jnp.uint32),
        grid=(no, n1 // ch, c // cb),
        in_specs=[pl.BlockSpec((None, n2, ch, cb), lambda o, j, k: (o, 0, j, k)),
                  pl.BlockSpec((ch, 2 * n2, n2), lambda o, j, k: (j, 0, 0))],
        out_specs=pl.BlockSpec((None, ch, 2 * n2, cb // 2), lambda o, j, k: (o, j, 0, k)),
        compiler_params=_params(("parallel", "parallel", "parallel"), big=True),
        name="fft_filter_stage1",
    )(ts4, f1_real)
    nk = n2 // ch
    return pl.pallas_call(
        _fft_stage2_kernel,
        out_shape=jax.ShapeDtypeStruct((no, n2, 2 * n1, c), F32),
        grid=(no, nk, c // cb),
        in_specs=[pl.BlockSpec((None, n1, ch, cb // 2), lambda o, j, k: (o, 0, j, k)),
                  pl.BlockSpec((None, n1, ch, cb // 2), lambda o, j, k: (o, 0, nk + j, k)),
                  pl.BlockSpec((2 * n1, 2 * n1), lambda o, j, k: (0, 0))],
        out_specs=pl.BlockSpec((None, ch, 2 * n1, cb), lambda o, j, k: (o, j, 0, k)),
        compiler_params=_params(("parallel", "parallel", "parallel"), big=True),
        name="fft_filter_stage2",
    )(y, y, f2)


def hyena_long_conv_gate(zsrc, z_col, gate_src, gate_col, hf, order, fbias, consts, cb=512):
    f1_pad, _, f2, g2, g1 = consts
    n1, n2, ch = FFT_N1, FFT_N2, FFT_CHUNK
    c = hf.shape[-1]
    ncb = c // cb
    half = n2 // 2
    y = pl.pallas_call(
        _fft_stage1_kernel,
        out_shape=jax.ShapeDtypeStruct((n1, 2 * n2, c // 2), jnp.uint32),
        grid=(n1 // ch, ncb),
        in_specs=[pl.BlockSpec((2, half, ch, cb), lambda j, k: (0, 0, j, z_col * ncb + k)),
                  pl.BlockSpec((ch, 2 * n2, n2), lambda j, k: (j, 0, 0))],
        out_specs=pl.BlockSpec((ch, 2 * n2, cb // 2), lambda j, k: (j, 0, k)),
        compiler_params=_params(("parallel", "parallel"), big=True),
        name="fft_stage1",
    )(zsrc, f1_pad)
    nk = n2 // ch
    q = pl.pallas_call(
        _fft_mid_kernel,
        out_shape=jax.ShapeDtypeStruct((n2, 2 * n1, c // 2), jnp.uint32),
        grid=(nk, ncb),
        in_specs=[pl.BlockSpec((n1, ch, cb // 2), lambda j, k: (0, j, k)),
                  pl.BlockSpec((n1, ch, cb // 2), lambda j, k: (0, nk + j, k)),
                  pl.BlockSpec((2 * n1, 2 * n1), lambda j, k: (0, 0)),
                  pl.BlockSpec((None, ch, 2 * n1, cb), lambda j, k: (order, j, 0, k)),
                  pl.BlockSpec((ch, 2 * n1, 2 * n1), lambda j, k: (j, 0, 0))],
        out_specs=pl.BlockSpec((ch, 2 * n1, cb // 2), lambda j, k: (j, 0, k)),
        compiler_params=_params(("parallel", "parallel"), big=True),
        name="fft_mid",
    )(y, y, f2, hf, g2)
    nj = n1 // ch
    return pl.pallas_call(
        _fft_last_kernel,
        out_shape=jax.ShapeDtypeStruct((2, half, n1, c), F32),
        grid=(nj, ncb),
        in_specs=[pl.BlockSpec((n2, ch, cb // 2), lambda j, k: (0, j, k)),
                  pl.BlockSpec((n2, ch, cb // 2), lambda j, k: (0, nj + j, k)),
                  pl.BlockSpec((2 * half, 2 * n2), lambda j, k: (0, 0)),
                  pl.BlockSpec((2, half, ch, cb), lambda j, k: (0, 0, j, gate_col * ncb + k)),
                  pl.BlockSpec((2, half, ch, cb), lambda j, k: (0, 0, j, z_col * ncb + k)),
                  pl.BlockSpec((None, 1, cb), lambda j, k: (order, 0, k))],
        out_specs=pl.BlockSpec((2, half, ch, cb), lambda j, k: (0, 0, j, k)),
        compiler_params=_params(("parallel", "parallel"), big=True),
        name="fft_last",
    )(q, q, g1, gate_src, zsrc, fbias.reshape(fbias.shape[0], 1, c))


def hyena_mixer(hn, h_res, w_in, b_in, conv_w, conv_b, fw1, fb1, fw2, fb2, freq, fw3, fbias, w_out,
                batch, seq):
    width = w_out.shape[0]
    assert batch == 2 and 2 * seq == FFT_N1 * FFT_N2
    proj = matmul([hn], w_in, bias=b_in, name="hyena_in")
    sc = short_conv(proj.reshape(batch, seq, 3 * width), conv_w, conv_b)
    consts = _dft_constants()
    ts = hyena_two_sided_filters(seq, fw1, fb1, fw2, fb2, freq, fw3, width)
    hf = hyena_filter_spectrum(ts, consts)
    sc4 = sc.reshape(batch, FFT_N2 // 2, FFT_N1, 3 * width)
    zf1 = hyena_long_conv_gate(sc4, 2, sc4, 0, hf, 0, fbias, consts)
    zf2 = hyena_long_conv_gate(zf1, 0, sc4, 1, hf, 1, fbias, consts)
    return matmul([zf2.reshape(batch * seq, width)], w_out, res=h_res, tm=512, tn=1024,
                  name="hyena_out")


def moe_swiglu(hn2_packed, logits, wg, wu, wd):
    n = hn2_packed.shape[0]
    top_v, top_i = lax.top_k(logits, TOP_K)
    gates = jax.nn.softmax(top_v, axis=-1)
    e_flat = top_i.reshape(-1).astype(jnp.int32)
    g_flat = gates.reshape(-1)
    tok_flat = jnp.repeat(jnp.arange(n, dtype=jnp.int32), TOP_K)
    nk = n * TOP_K
    onehot = (e_flat[:, None] == jnp.arange(N_EXPERTS, dtype=jnp.int32)[None]).astype(jnp.int32)
    csum = jnp.cumsum(onehot, axis=0)
    rank = jnp.take_along_axis(csum, e_flat[:, None], axis=1)[:, 0] - 1
    counts = csum[-1]
    padded = ((counts + MOE_TILE - 1) // MOE_TILE) * MOE_TILE
    pad_end = jnp.cumsum(padded)
    pad_start = pad_end - padded
    dest = pad_start[e_flat] + rank
    p_rows = nk + N_EXPERTS * MOE_TILE
    row_tok = jnp.zeros((p_rows,), jnp.int32).at[dest].set(tok_flat)
    row_gate = jnp.zeros((p_rows,), F32).at[dest].set(g_flat)
    nt = p_rows // MOE_TILE
    tile_start = jnp.arange(nt, dtype=jnp.int32) * MOE_TILE
    tile_used = tile_start < pad_end[-1]
    tile_exp = jnp.minimum(jnp.searchsorted(pad_end, tile_start, side='right'), N_EXPERTS - 1).astype(jnp.int32)
    tile_rows = jnp.clip(counts[tile_exp] - (tile_start - pad_start[tile_exp]), 0, MOE_TILE)
    tile_rows = jnp.where(tile_used, tile_rows, 0).astype(jnp.int32)
    last_exp = tile_exp[jnp.maximum(jnp.sum(tile_used.astype(jnp.int32)) - 1, 0)]
    tile_exp = jnp.where(tile_used, tile_exp, last_exp)
    ys = swiglu(hn2_packed, wg, wu, wd, tile_exp, tile_rows, row_gate[:, None], row_tok,
                name="swiglu_experts")
    return ys, dest.reshape(n, TOP_K)


def kernel(x, p, ln_mix, ln_ffn, ln_ple, final_norm, t5_bias, w_attn_in, w_attn_out, attn_sink, na_rpb, w_ffn_gate, w_ffn_up, w_ffn_down, w_hy_in, b_hy_in, w_hy_conv, b_hy_conv, w_hy_f1, b_hy_f1, w_hy_f2, b_hy_f2, hy_freq, w_hy_f3, hy_bias, w_hy_out, w_router, w_exp_gate, w_exp_up, w_exp_down, w_ple_proj, w_ple_gate):
    batch, seq, d = x.shape
    n = batch * seq
    depth = ln_mix.shape[0]
    h = x.reshape(n, d)
    for i in range(depth):
        li = i // 2
        hn = rmsnorm(h, ln_mix[i], BF16)
        if i % 2 == 0:
            na_off = A_WIDTH + 2 * A_KV_WIDTH
            w_in = w_attn_in[li]
            proj_a = matmul([hn], w_in[:, :na_off], out_dtype=BF16, tn=640, name="attn_in_a")
            proj_n = matmul([hn], w_in[:, na_off:], out_dtype=BF16, name="attn_in_n")
            oa = window_attention(proj_a.reshape(batch, seq, -1), t5_bias, attn_sink[li])
            ob = neighbourhood_attention(proj_n.reshape(batch, seq, -1), na_rpb[li])
            h = matmul([oa.reshape(n, A_WIDTH), ob.reshape(n, B_WIDTH)], w_attn_out[li], res=h,
                       name="attn_out")
            hn2 = rmsnorm(h, ln_ffn[i], BF16)
            nt = n // MOE_TILE
            delta = swiglu(hn2, w_ffn_gate, w_ffn_up, w_ffn_down,
                           jnp.full((nt,), li, jnp.int32), jnp.full((nt,), MOE_TILE, jnp.int32),
                           jnp.ones((n, 1), F32), name="swiglu_dense")
            h, hn3 = add_rmsnorm(h, delta, ln_ple[i])
        else:
            h = hyena_mixer(hn, h, w_hy_in[li], b_hy_in[li], w_hy_conv[li], b_hy_conv[li],
                            w_hy_f1[li], b_hy_f1[li], w_hy_f2[li], b_hy_f2[li], hy_freq[li],
                            w_hy_f3[li], hy_bias[li], w_hy_out[li], batch, seq)
            wr_pad = jnp.pad(w_router[li].astype(F32), ((0, 0), (0, 128 - N_EXPERTS)))
            hn2_packed, logits = rmsnorm_router(h, ln_ffn[i], wr_pad)
            ys, dest2 = moe_swiglu(hn2_packed, logits[:, :N_EXPERTS], w_exp_gate[li], w_exp_up[li],
                                   w_exp_down[li])
            h, hn3 = moe_combine_rmsnorm(h, ys, dest2, ln_ple[i])
        h = ple(hn3, p[i].reshape(n, -1), h, w_ple_gate[i], w_ple_proj[i])
    return rmsnorm(h, final_norm, F32).reshape(batch, seq, d)
```

```python
import functools
import math

import jax
import jax.numpy as jnp
import numpy as np
from jax import lax
from jax.experimental import pallas as pl
from jax.experimental.pallas import tpu as pltpu

F32 = jnp.float32
BF16 = jnp.bfloat16
NEG_INF = -1e30
RMS_EPS = 1e-6

V7X_VMEM_LIMIT_BYTES = 56 * 1024 * 1024

HEAD_DIM = 64
A_Q_HEADS = 16
A_KV_HEADS = 2
A_GROUP = A_Q_HEADS // A_KV_HEADS
A_BLOCK = 128
T5_BUCKETS = 32
T5_MAX_DIST = 128
B_HEADS = 16
GRID_W = 64
NA_WIN_H = 8
NA_WIN_W = 16
A_WIDTH = A_Q_HEADS * HEAD_DIM
A_KV_WIDTH = A_KV_HEADS * HEAD_DIM
B_WIDTH = B_HEADS * HEAD_DIM
N_EXPERTS = 8
TOP_K = 2
HYENA_EMB = 33
HYENA_DECAY_TARGET = 1e-2
HYENA_FAST_PCT = 0.3
HYENA_SLOW_PCT = 1.5

FFT_N1 = 64
FFT_N2 = 128
FFT_CHUNK = 8
FFT_CB = 512

MOE_TILE = 1024
FFN_F_TILE = 256
FFN_SUB_ROWS = 512
DMA_LOOP_UNROLL = 8


def _params(semantics, big=False):
    return pltpu.CompilerParams(
        dimension_semantics=semantics,
        vmem_limit_bytes=V7X_VMEM_LIMIT_BYTES if big else None)


def _rmsnorm_kernel(x_ref, g_ref, o_ref):
    x = x_ref[...]
    y = x * lax.rsqrt(jnp.mean(x * x, axis=-1, keepdims=True) + RMS_EPS)
    o_ref[...] = (y * g_ref[...]).astype(o_ref.dtype)


def rmsnorm(x2, g, out_dtype, tm=512):
    n, d = x2.shape
    return pl.pallas_call(
        _rmsnorm_kernel,
        out_shape=jax.ShapeDtypeStruct((n, d), out_dtype),
        grid=(n // tm,),
        in_specs=[pl.BlockSpec((tm, d), lambda i: (i, 0)),
                  pl.BlockSpec((1, d), lambda i: (0, 0))],
        out_specs=pl.BlockSpec((tm, d), lambda i: (i, 0)),
        compiler_params=_params(("parallel",)),
        name="rmsnorm",
    )(x2, g.reshape(1, d))


def _add_rmsnorm_kernel(x_ref, d_ref, g_ref, h_ref, o_ref):
    x = x_ref[...] + d_ref[...]
    h_ref[...] = x
    y = x * lax.rsqrt(jnp.mean(x * x, axis=-1, keepdims=True) + RMS_EPS)
    o_ref[...] = (y * g_ref[...]).astype(o_ref.dtype)


def add_rmsnorm(x2, delta, g, tm=512):
    n, d = x2.shape
    return pl.pallas_call(
        _add_rmsnorm_kernel,
        out_shape=(jax.ShapeDtypeStruct((n, d), F32), jax.ShapeDtypeStruct((n, d), BF16)),
        grid=(n // tm,),
        in_specs=[pl.BlockSpec((tm, d), lambda i: (i, 0)),
                  pl.BlockSpec((tm, d), lambda i: (i, 0)),
                  pl.BlockSpec((1, d), lambda i: (0, 0))],
        out_specs=(pl.BlockSpec((tm, d), lambda i: (i, 0)),
                   pl.BlockSpec((tm, d), lambda i: (i, 0))),
        compiler_params=_params(("parallel",)),
        name="add_rmsnorm",
    )(x2, delta, g.reshape(1, d))


def _pack_bf16_pairs(y):
    w = y.shape[1] // 2
    bits = pltpu.bitcast(y.astype(BF16).astype(F32), jnp.uint32)
    return (bits[:, :w] >> 16) | (bits[:, w:] & jnp.uint32(0xFFFF0000))


def _unpack_pairs_f32(p):
    lo = pltpu.bitcast(p << 16, F32)
    hi = pltpu.bitcast(p & jnp.uint32(0xFFFF0000), F32)
    return jnp.concatenate([lo, hi], axis=1)


def _unpack_bf16_pairs(p):
    return _unpack_pairs_f32(p).astype(BF16)


def _rmsnorm_router_kernel(x_ref, g_ref, wr_ref, o_ref, l_ref):
    x = x_ref[...]
    y = x * lax.rsqrt(jnp.mean(x * x, axis=-1, keepdims=True) + RMS_EPS)
    y = y * g_ref[...]
    o_ref[...] = _pack_bf16_pairs(y)
    l_ref[...] = jnp.dot(y, wr_ref[...], preferred_element_type=F32,
                         precision=lax.Precision.HIGHEST)


def rmsnorm_router(x2, g, w_router_pad, tm=512):
    n, d = x2.shape
    ne = w_router_pad.shape[1]
    return pl.pallas_call(
        _rmsnorm_router_kernel,
        out_shape=(jax.ShapeDtypeStruct((n, d // 2), jnp.uint32), jax.ShapeDtypeStruct((n, ne), F32)),
        grid=(n // tm,),
        in_specs=[pl.BlockSpec((tm, d), lambda i: (i, 0)),
                  pl.BlockSpec((1, d), lambda i: (0, 0)),
                  pl.BlockSpec((d, ne), lambda i: (0, 0))],
        out_specs=(pl.BlockSpec((tm, d // 2), lambda i: (i, 0)),
                   pl.BlockSpec((tm, ne), lambda i: (i, 0))),
        compiler_params=_params(("parallel",)),
        name="rmsnorm_router",
    )(x2, g.reshape(1, d), w_router_pad)


def _mm_kernel(*refs, n_a, has_bias, has_res):
    a_refs = refs[:n_a]
    w_refs = refs[n_a:2 * n_a]
    idx = 2 * n_a
    bias_ref = refs[idx] if has_bias else None
    idx += int(has_bias)
    res_ref = refs[idx] if has_res else None
    idx += int(has_res)
    o_ref = refs[idx]
    wbf_refs = refs[idx + 1:idx + 1 + n_a]

    @pl.when(pl.program_id(1) == 0)
    def _():
        for w_ref, wbf_ref in zip(w_refs, wbf_refs):
            wbf_ref[...] = w_ref[...].astype(BF16)

    acc = None
    for a_ref, wbf_ref in zip(a_refs, wbf_refs):
        d = jnp.dot(a_ref[...].astype(BF16), wbf_ref[...], preferred_element_type=F32)
        acc = d if acc is None else acc + d
    if has_bias:
        acc = acc + bias_ref[...]
    if has_res:
        acc = acc + res_ref[...]
    o_ref[...] = acc.astype(o_ref.dtype)


def matmul(a_list, w, *, bias=None, res=None, out_dtype=F32, tm=1024, tn=1024, name="matmul"):
    m = a_list[0].shape[0]
    n = w.shape[1]
    n_a = len(a_list)
    k_each = a_list[0].shape[1]
    assert all(a.shape == (m, k_each) for a in a_list) and w.shape[0] == n_a * k_each
    in_specs = [pl.BlockSpec((tm, k_each), lambda j, i: (i, 0)) for _ in a_list]
    in_specs += [pl.BlockSpec((k_each, tn), functools.partial(lambda j, i, kb: (kb, j), kb=kb))
                 for kb in range(n_a)]
    args = list(a_list) + [w] * n_a
    if bias is not None:
        in_specs.append(pl.BlockSpec((1, tn), lambda j, i: (0, j)))
        args.append(bias.reshape(1, n))
    if res is not None:
        in_specs.append(pl.BlockSpec((tm, tn), lambda j, i: (i, j)))
        args.append(res)
    return pl.pallas_call(
        functools.partial(_mm_kernel, n_a=n_a, has_bias=bias is not None, has_res=res is not None),
        out_shape=jax.ShapeDtypeStruct((m, n), out_dtype),
        grid=(n // tn, m // tm),
        in_specs=in_specs,
        out_specs=pl.BlockSpec((tm, tn), lambda j, i: (i, j)),
        scratch_shapes=[pltpu.VMEM((k_each, tn), BF16) for _ in a_list],
        compiler_params=_params(("parallel", "arbitrary"), big=True),
        name=name,
    )(*args)


def _ple_kernel(hn_ref, p_ref, h_ref, wg_ref, wp_ref, o_ref, wg_bf, wp_bf):
    @pl.when(pl.program_id(1) == 0)
    def _():
        wg_bf[...] = wg_ref[...].astype(BF16)
        wp_bf[...] = wp_ref[...].astype(BF16)

    a = jnp.dot(hn_ref[...], wg_bf[...], preferred_element_type=F32)
    pp = jnp.dot(p_ref[...].astype(BF16), wp_bf[...], preferred_element_type=F32)
    o_ref[...] = h_ref[...] + jax.nn.sigmoid(a) * pp


def ple(hn, p2, h2, w_gate, w_proj, tm=1024, tn=1024):
    m, d = h2.shape
    pd = p2.shape[1]
    return pl.pallas_call(
        _ple_kernel,
        out_shape=jax.ShapeDtypeStruct((m, d), F32),
        grid=(d // tn, m // tm),
        in_specs=[pl.BlockSpec((tm, d), lambda j, i: (i, 0)),
                  pl.BlockSpec((tm, pd), lambda j, i: (i, 0)),
                  pl.BlockSpec((tm, tn), lambda j, i: (i, j)),
                  pl.BlockSpec((d, tn), lambda j, i: (0, j)),
                  pl.BlockSpec((pd, tn), lambda j, i: (0, j))],
        out_specs=pl.BlockSpec((tm, tn), lambda j, i: (i, j)),
        scratch_shapes=[pltpu.VMEM((d, tn), BF16), pltpu.VMEM((pd, tn), BF16)],
        compiler_params=_params(("parallel", "arbitrary"), big=True),
        name="ple",
    )(hn, p2, h2, w_gate, w_proj)


def _row_copy(src_hbm, row, dst, i, sem):
    return pltpu.make_async_copy(src_hbm.at[pl.ds(row, 1), :], dst.at[pl.ds(i, 1), :], sem)


def _swiglu_kernel(exp_ref, rows_ref, tok_ref, x_ref, wg_ref, wu_ref, wd_ref, o_ref,
                   *scratch, n_col, gather, sub_rows):
    del exp_ref
    t = pl.program_id(0)
    f = pl.program_id(1)
    nt = pl.num_programs(0)
    n_rows = rows_ref[t]
    used = n_rows > 0
    tm = o_ref.shape[0]

    @pl.when(f == 0)
    def _():
        o_ref[...] = jnp.zeros_like(o_ref)

    if gather:
        xbuf, xbf, sems = scratch
        slot = t % 2

        def for_each_row(tile, fn):
            def body(c, carry):
                for r in range(DMA_LOOP_UNROLL):
                    fn(c * DMA_LOOP_UNROLL + r)
                return carry
            lax.fori_loop(0, (rows_ref[tile] + DMA_LOOP_UNROLL - 1) // DMA_LOOP_UNROLL, body, 0)

        def start_tile(tile, sl):
            for_each_row(tile, lambda i: _row_copy(x_ref, tok_ref[tile * tm + i], xbuf.at[sl], i,
                                                   sems.at[sl, i]).start())

        @pl.when(jnp.logical_and(f == 0, t == 0))
        def _():
            xbuf[...] = jnp.zeros_like(xbuf)
            start_tile(0, 0)

        @pl.when(jnp.logical_and(f == 0, used))
        def _():
            for_each_row(t, lambda i: _row_copy(x_ref, 0, xbuf.at[slot], i, sems.at[slot, i]).wait())
            xbf[...] = _unpack_bf16_pairs(xbuf[slot])

        @pl.when(jnp.logical_and(f == 1, t + 1 < nt))
        def _():
            start_tile(jnp.minimum(t + 1, nt - 1), 1 - slot)

        x_src = xbf
    else:
        x_src = x_ref

    for sb in range(tm // sub_rows):
        rs = slice(sb * sub_rows, (sb + 1) * sub_rows)

        @pl.when(sb * sub_rows < n_rows)
        def _():
            x = x_src[rs, :]
            g = jnp.dot(x, wg_ref[...].astype(BF16), preferred_element_type=F32)
            u = jnp.dot(x, wu_ref[...].astype(BF16), preferred_element_type=F32)
            hmid = (g * jax.nn.sigmoid(g) * u).astype(BF16)
            wd = wd_ref[...].astype(BF16)
            cw = o_ref.shape[1] // n_col
            for c in range(n_col):
                y = jnp.dot(hmid, wd[:, c * cw:(c + 1) * cw], preferred_element_type=F32)
                o_ref[rs, c * cw:(c + 1) * cw] += y


def swiglu(x, w_gate, w_up, w_down, tile_exp, tile_rows, row_src=None, *, tm=MOE_TILE,
           tf=FFN_F_TILE, sub_rows=MOE_TILE, name="swiglu"):
    gather = row_src is not None
    m = tile_rows.shape[0] * tm
    d = w_gate.shape[1]
    dff = w_gate.shape[2]
    nf = dff // tf
    nt = m // tm
    assert nf >= 2

    def w_in_map(t, f, exp_ref, used_ref, tok_ref):
        return (exp_ref[t], 0, jnp.where(used_ref[t] > 0, f, nf - 1))

    def w_out_map(t, f, exp_ref, used_ref, tok_ref):
        return (exp_ref[t], jnp.where(used_ref[t] > 0, f, nf - 1), 0)

    if gather:
        x_spec = pl.BlockSpec(memory_space=pl.ANY)
        scratch = [pltpu.VMEM((2, tm, d // 2), jnp.uint32), pltpu.VMEM((tm, d), BF16),
                   pltpu.SemaphoreType.DMA((2, tm))]
    else:
        x_spec = pl.BlockSpec((tm, d), lambda t, f, e, u, k: (t, 0))
        scratch = []
        row_src = jnp.zeros((1,), jnp.int32)
    grid_spec = pltpu.PrefetchScalarGridSpec(
        num_scalar_prefetch=3,
        grid=(nt, nf),
        in_specs=[x_spec,
                  pl.BlockSpec((None, d, tf), w_in_map),
                  pl.BlockSpec((None, d, tf), w_in_map),
                  pl.BlockSpec((None, tf, d), w_out_map)],
        out_specs=pl.BlockSpec((tm, d), lambda t, f, e, u, k: (t, 0)),
        scratch_shapes=scratch,
    )
    return pl.pallas_call(
        functools.partial(_swiglu_kernel, n_col=4, gather=gather, sub_rows=sub_rows),
        out_shape=jax.ShapeDtypeStruct((m, d), F32),
        grid_spec=grid_spec,
        compiler_params=_params(("arbitrary", "arbitrary"), big=True),
        name=name,
    )(tile_exp, tile_rows, row_src, x, w_gate, w_up, w_down)


COMBINE_TILE = 256


def _combine_kernel(d0_ref, d1_ref, ys_hbm, gate_ref, h_ref, g_ref, hout_ref, hn_ref, buf, sems):
    i = pl.program_id(0)
    n = pl.num_programs(0)
    tt = h_ref.shape[0]
    slot = i % 2
    dests = (d0_ref, d1_ref)

    def start_tile(tile, sl):
        def body(r, c):
            for k, d_ref in enumerate(dests):
                _row_copy(ys_hbm, d_ref[tile * tt + r], buf.at[sl, k], r, sems.at[sl, k, r]).start()
            return c
        lax.fori_loop(0, tt, body, 0, unroll=DMA_LOOP_UNROLL)

    @pl.when(i == 0)
    def _():
        start_tile(0, 0)

    @pl.when(i + 1 < n)
    def _():
        start_tile(i + 1, 1 - slot)

    def wait_body(r, c):
        for k in range(TOP_K):
            _row_copy(ys_hbm, 0, buf.at[slot, k], r, sems.at[slot, k, r]).wait()
        return c
    lax.fori_loop(0, tt, wait_body, 0, unroll=DMA_LOOP_UNROLL)

    gates = gate_ref[...]
    x = h_ref[...] + (buf[slot, 0] * gates[:, 0:1] + buf[slot, 1] * gates[:, 1:2])
    hout_ref[...] = x
    y = x * lax.rsqrt(jnp.mean(x * x, axis=-1, keepdims=True) + RMS_EPS)
    hn_ref[...] = (y * g_ref[...]).astype(hn_ref.dtype)


def moe_combine_rmsnorm(h2, ys, dest2, gates, g, tt=COMBINE_TILE):
    n, d = h2.shape
    grid_spec = pltpu.PrefetchScalarGridSpec(
        num_scalar_prefetch=2,
        grid=(n // tt,),
        in_specs=[pl.BlockSpec(memory_space=pl.ANY),
                  pl.BlockSpec((tt, TOP_K), lambda i, a, b: (i, 0)),
                  pl.BlockSpec((tt, d), lambda i, a, b: (i, 0)),
                  pl.BlockSpec((1, d), lambda i, a, b: (0, 0))],
        out_specs=(pl.BlockSpec((tt, d), lambda i, a, b: (i, 0)),
                   pl.BlockSpec((tt, d), lambda i, a, b: (i, 0))),
        scratch_shapes=[pltpu.VMEM((2, TOP_K, tt, d), F32), pltpu.SemaphoreType.DMA((2, TOP_K, tt))],
    )
    return pl.pallas_call(
        _combine_kernel,
        out_shape=(jax.ShapeDtypeStruct((n, d), F32), jax.ShapeDtypeStruct((n, d), BF16)),
        grid_spec=grid_spec,
        compiler_params=_params(("arbitrary",), big=True),
        name="moe_combine",
    )(dest2[:, 0], dest2[:, 1], ys, gates, h2, g.reshape(1, d))


def _t5_bucket(rel):
    half = T5_BUCKETS // 2
    max_exact = half // 2
    n = jnp.abs(rel)
    log_ratio = jnp.log(jnp.maximum(n, 1).astype(F32) / max_exact) / math.log(T5_MAX_DIST / max_exact)
    large = jnp.minimum(max_exact + (log_ratio * (half - max_exact)).astype(jnp.int32), half - 1)
    return jnp.where(rel > 0, half, 0) + jnp.where(n < max_exact, n, large)


def _window_bias_table(t5_bias):
    i = jnp.arange(A_BLOCK)[:, None]
    j = jnp.arange(3 * A_BLOCK)[None, :]
    rel = j - A_BLOCK - i
    onehot = (_t5_bucket(rel)[None] == jnp.arange(T5_BUCKETS)[:, None, None]).astype(F32)
    bias = jnp.einsum('bh,bij->hij', t5_bias.astype(F32), onehot, precision=lax.Precision.HIGHEST)
    return jnp.where((jnp.abs(rel) <= A_BLOCK)[None], bias, NEG_INF)


def _window_kernel(sink_ref, q_ref, kv_ref, bias_ref, o_ref, *, nb):
    n = pl.program_id(1)
    scale = HEAD_DIM ** -0.5
    starts = (jnp.maximum(n - 1, 0), n, jnp.minimum(n + 1, nb - 1))
    kv = [kv_ref[pl.ds(pl.multiple_of(s * A_BLOCK, A_BLOCK), A_BLOCK), :] for s in starts]
    col = lax.broadcasted_iota(jnp.int32, (A_BLOCK, 3 * A_BLOCK), 1)
    edge_ok = jnp.logical_and(jnp.logical_or(n > 0, col >= A_BLOCK),
                              jnp.logical_or(n < nb - 1, col < 2 * A_BLOCK))
    kv = jnp.concatenate(kv, axis=0)
    q = q_ref[...]
    lo = lax.broadcasted_iota(jnp.int32, (1, 2 * HEAD_DIM), 1) < HEAD_DIM
    sel = (jnp.where(lo, scale, 0.0).astype(BF16), jnp.where(lo, 0.0, scale).astype(BF16))
    pairs_per_kv = A_GROUP // 2
    kk, vv = [], []
    for kh in range(A_KV_HEADS):
        k1 = kv[:, kh * HEAD_DIM:(kh + 1) * HEAD_DIM]
        v1 = kv[:, A_KV_WIDTH + kh * HEAD_DIM:A_KV_WIDTH + (kh + 1) * HEAD_DIM]
        kk.append(jnp.concatenate([k1, k1], axis=1))
        vv.append(jnp.concatenate([v1, v1], axis=1))
    scores = []
    for pr in range(A_Q_HEADS // 2):
        q2 = q[:, pr * 2 * HEAD_DIM:(pr + 1) * 2 * HEAD_DIM]
        qq = jnp.concatenate([q2 * sel[0], q2 * sel[1]], axis=0)
        s2 = lax.dot_general(qq, kk[pr // pairs_per_kv], (((1,), (1,)), ((), ())),
                             preferred_element_type=F32)
        for i in range(2):
            scores.append(jnp.where(edge_ok, s2[i * A_BLOCK:(i + 1) * A_BLOCK] + bias_ref[2 * pr + i],
                                    NEG_INF))
    probs, dens = [], []
    for h, s in enumerate(scores):
        sink = sink_ref[h]
        mx = jnp.maximum(jnp.max(s, axis=-1, keepdims=True), sink)
        p = jnp.exp(s - mx)
        dens.append(jnp.sum(p, axis=-1, keepdims=True) + jnp.exp(sink - mx))
        probs.append(p.astype(BF16))
    for pr in range(A_Q_HEADS // 2):
        v2 = vv[pr // pairs_per_kv]
        oa = jnp.dot(probs[2 * pr], v2, preferred_element_type=F32) / dens[2 * pr]
        ob = jnp.dot(probs[2 * pr + 1], v2, preferred_element_type=F32) / dens[2 * pr + 1]
        o_ref[:, pr * 2 * HEAD_DIM:(pr + 1) * 2 * HEAD_DIM] = jnp.where(lo, oa, ob).astype(o_ref.dtype)


def window_attention(proj_a, t5_bias, sink):
    b, s, _ = proj_a.shape
    nb = s // A_BLOCK
    kvw = 2 * A_KV_WIDTH
    grid_spec = pltpu.PrefetchScalarGridSpec(
        num_scalar_prefetch=0,
        grid=(b, nb),
        in_specs=[pl.BlockSpec(memory_space=pltpu.SMEM),
                  pl.BlockSpec((None, A_BLOCK, A_WIDTH), lambda bi, n: (bi, n, 0)),
                  pl.BlockSpec((None, s, kvw), lambda bi, n: (bi, 0, A_WIDTH // kvw)),
                  pl.BlockSpec((A_Q_HEADS, A_BLOCK, 3 * A_BLOCK), lambda bi, n: (0, 0, 0))],
        out_specs=pl.BlockSpec((None, A_BLOCK, A_WIDTH), lambda bi, n: (bi, n, 0)),
    )
    return pl.pallas_call(
        functools.partial(_window_kernel, nb=nb),
        out_shape=jax.ShapeDtypeStruct((b, s, A_WIDTH), BF16),
        grid_spec=grid_spec,
        compiler_params=_params(("parallel", "arbitrary")),
        name="window_attention",
    )(sink.astype(F32), proj_a, proj_a, _window_bias_table(t5_bias))


NA_HEAD_GROUP = 16


def _na_bias_table(rpb, rows):
    kh = min(NA_WIN_H, rows)
    kw = NA_WIN_W
    c = jnp.arange(GRID_W)
    cs = jnp.clip(c - kw // 2, 0, GRID_W - kw)
    col_ok = (c[None] >= cs[:, None]) & (c[None] < cs[:, None] + kw)
    col_off = jnp.clip(c[None] - c[:, None], -(kw - 1), kw - 1) + kw - 1
    onehot = (col_off[None] == jnp.arange(2 * kw - 1)[:, None, None]).astype(F32)
    by_col = jnp.einsum('hrc,cqk->hrqk', rpb.astype(F32), onehot, precision=lax.Precision.HIGHEST)
    by_col = jnp.where(col_ok[None, None], by_col, NEG_INF)
    tabs = [by_col[:, NA_WIN_H - 1 - d:NA_WIN_H - 1 - d + kh] for d in range(kh)]
    bias = jnp.stack(tabs, axis=0).transpose(0, 1, 3, 2, 4)
    return bias.reshape(kh, rpb.shape[0], GRID_W, kh * GRID_W)


def _na_kernel(q_ref, k_ref, v_ref, bias_ref, o_ref, *, rows):
    r = pl.program_id(2)
    kh = min(NA_WIN_H, rows)
    scale = HEAD_DIM ** -0.5
    start = pl.multiple_of(jnp.clip(r - kh // 2, 0, rows - kh) * GRID_W, GRID_W)
    k = k_ref[pl.ds(start, kh * GRID_W), :]
    v = v_ref[pl.ds(start, kh * GRID_W), :]
    q = q_ref[...]
    lo = lax.broadcasted_iota(jnp.int32, (1, 2 * HEAD_DIM), 1) < HEAD_DIM
    sel = (jnp.where(lo, scale, 0.0).astype(BF16), jnp.where(lo, 0.0, scale).astype(BF16))
    n_pairs = NA_HEAD_GROUP // 2
    scores = []
    for pr in range(n_pairs):
        sl = slice(pr * 2 * HEAD_DIM, (pr + 1) * 2 * HEAD_DIM)
        q2 = q[:, sl]
        qq = jnp.concatenate([q2 * sel[0], q2 * sel[1]], axis=0)
        s2 = lax.dot_general(qq, k[:, sl], (((1,), (1,)), ((), ())), preferred_element_type=F32)
        for i in range(2):
            scores.append(s2[i * GRID_W:(i + 1) * GRID_W] + bias_ref[2 * pr + i])
    probs, dens = [], []
    for s in scores:
        mx = jnp.max(s, axis=-1, keepdims=True)
        p = jnp.exp(s - mx)
        dens.append(jnp.sum(p, axis=-1, keepdims=True))
        probs.append(p.astype(BF16))
    for pr in range(n_pairs):
        sl = slice(pr * 2 * HEAD_DIM, (pr + 1) * 2 * HEAD_DIM)
        oa = jnp.dot(probs[2 * pr], v[:, sl], preferred_element_type=F32) / dens[2 * pr]
        ob = jnp.dot(probs[2 * pr + 1], v[:, sl], preferred_element_type=F32) / dens[2 * pr + 1]
        o_ref[:, sl] = jnp.where(lo, oa, ob).astype(o_ref.dtype)


def neighbourhood_attention(proj_n, rpb):
    b, s, _ = proj_n.shape
    rows = s // GRID_W
    kh = min(NA_WIN_H, rows)
    gw = NA_HEAD_GROUP * HEAD_DIM
    ng = B_WIDTH // gw

    def bias_map(bi, g, r):
        return (r - jnp.clip(r - kh // 2, 0, rows - kh), g, 0, 0)

    return pl.pallas_call(
        functools.partial(_na_kernel, rows=rows),
        out_shape=jax.ShapeDtypeStruct((b, s, B_WIDTH), BF16),
        grid=(b, ng, rows),
        in_specs=[pl.BlockSpec((None, GRID_W, gw), lambda bi, g, r: (bi, r, g)),
                  pl.BlockSpec((None, s, gw), lambda bi, g, r: (bi, 0, ng + g)),
                  pl.BlockSpec((None, s, gw), lambda bi, g, r: (bi, 0, 2 * ng + g)),
                  pl.BlockSpec((None, NA_HEAD_GROUP, GRID_W, kh * GRID_W), bias_map)],
        out_specs=pl.BlockSpec((None, GRID_W, gw), lambda bi, g, r: (bi, r, g)),
        compiler_params=_params(("parallel", "parallel", "arbitrary"), big=True),
        name="neighbourhood_attention",
    )(proj_n, proj_n, proj_n, _na_bias_table(rpb, rows))


def _short_conv_kernel(prev_ref, cur_ref, next_ref, w_ref, b_ref, o_ref):
    i = pl.program_id(1)
    nt = pl.num_programs(1)
    x = cur_ref[...]
    tl = x.shape[0]
    row = lax.broadcasted_iota(jnp.int32, x.shape, 0)
    prev_row = jnp.where(i > 0, prev_ref[7:8, :], 0.0)
    next_row = jnp.where(i < nt - 1, next_ref[0:1, :], 0.0)
    up = jnp.where(row == 0, prev_row, pltpu.roll(x, 1, 0))
    dn = jnp.where(row == tl - 1, next_row, pltpu.roll(x, tl - 1, 0))
    o_ref[...] = up * w_ref[0:1, :] + x * w_ref[1:2, :] + dn * w_ref[2:3, :] + b_ref[...]


def short_conv(proj, conv_w, conv_b, tl=512, tc=1024):
    b, l, c = proj.shape
    nl8 = l // 8
    return pl.pallas_call(
        _short_conv_kernel,
        out_shape=jax.ShapeDtypeStruct((b, l, c), F32),
        grid=(b, l // tl, c // tc),
        in_specs=[pl.BlockSpec((None, 8, tc), lambda bi, i, j: (bi, jnp.maximum(i * (tl // 8) - 1, 0), j)),
                  pl.BlockSpec((None, tl, tc), lambda bi, i, j: (bi, i, j)),
                  pl.BlockSpec((None, 8, tc), lambda bi, i, j: (bi, jnp.minimum((i + 1) * (tl // 8), nl8 - 1), j)),
                  pl.BlockSpec((3, tc), lambda bi, i, j: (0, j)),
                  pl.BlockSpec((1, tc), lambda bi, i, j: (0, j))],
        out_specs=pl.BlockSpec((None, tl, tc), lambda bi, i, j: (bi, i, j)),
        compiler_params=_params(("parallel", "parallel", "parallel")),
        name="short_conv",
    )(proj, proj, proj, conv_w, conv_b.reshape(1, c))


def _filter_kernel(z_ref, t_ref, fw1_ref, fb1_ref, fw2_ref, fb2_ref, fr_ref, w3_ref, w3b_ref,
                   delta_ref, o_ref, *, seq):
    m = pl.program_id(0)
    hi = lax.Precision.HIGHEST
    fr = fr_ref[...]
    hid = jnp.sin(fr * (jnp.dot(z_ref[...], fw1_ref[...], preferred_element_type=F32, precision=hi)
                        + fb1_ref[...]))
    hid = jnp.sin(fr * (jnp.dot(hid, fw2_ref[...], preferred_element_type=F32, precision=hi)
                        + fb2_ref[...]))
    hid_bf = hid.astype(BF16)
    h = jnp.dot(hid_bf, w3_ref[...].astype(BF16), preferred_element_type=F32)
    tm = h.shape[0]
    width = delta_ref.shape[1]
    row = lax.broadcasted_iota(jnp.int32, (tm, width), 0) + m * tm
    decay = jnp.where(row == seq, 0.0, jnp.exp(-t_ref[...] * delta_ref[...]))
    def emit(taps):
        for o in range(o_ref.shape[0]):
            for kb in range(width // FFT_CB):
                cols = slice(kb * FFT_CB, (kb + 1) * FFT_CB)
                blk = taps[:, o * width:(o + 1) * width][:, cols] * decay[:, cols]
                o_ref[o, :, kb * (FFT_CB // 2):(kb + 1) * (FFT_CB // 2)] = _pack_bf16_pairs(blk)

    @pl.when(m != 0)
    def _():
        emit(h)

    @pl.when(m == 0)
    def _():
        h_back = jnp.dot(hid_bf, w3b_ref[...].astype(BF16), preferred_element_type=F32)
        first = lax.broadcasted_iota(jnp.int32, h.shape, 0) == 0
        emit(h + jnp.where(first, h_back, 0.0))


def hyena_two_sided_filters(seq, fw1, fb1, fw2, fb2, freq, fw3, width, tm=512):
    n = 2 * seq
    u = jnp.arange(n)
    pos = jnp.where(u <= seq, u, n - u)
    pos = jnp.where(u == seq, 0, pos)
    bands = (HYENA_EMB - 1) // 2
    f = jnp.linspace(1e-4, bands - 1, bands, dtype=F32)[None]
    t = (pos.astype(F32) / (seq - 1))[:, None]
    w = ((2.0 * math.pi / seq) * pos.astype(F32))[:, None]
    z = jnp.concatenate([t, jnp.cos(f * w), -jnp.sin(f * w)], axis=-1)
    emb_pad = 40
    z = jnp.pad(z, ((0, 0), (0, emb_pad - HYENA_EMB)))
    fw1p = jnp.pad(fw1.astype(F32), ((0, emb_pad - HYENA_EMB), (0, 0)))
    fh = fw1.shape[1]
    max_decay = math.log(HYENA_DECAY_TARGET) / HYENA_FAST_PCT
    min_decay = math.log(HYENA_DECAY_TARGET) / HYENA_SLOW_PCT
    deltas = jnp.abs(jnp.linspace(min_decay, max_decay, width, dtype=F32))[None]
    half_tiles = seq // tm
    orders = fw3.shape[1] // (2 * width)
    ow = orders * width

    return pl.pallas_call(
        functools.partial(_filter_kernel, seq=seq),
        out_shape=jax.ShapeDtypeStruct((orders, n, width // 2), jnp.uint32),
        grid=(n // tm,),
        in_specs=[pl.BlockSpec((tm, emb_pad), lambda m: (m, 0)),
                  pl.BlockSpec((tm, 1), lambda m: (m, 0)),
                  pl.BlockSpec((emb_pad, fh), lambda m: (0, 0)),
                  pl.BlockSpec((1, fh), lambda m: (0, 0)),
                  pl.BlockSpec((fh, fh), lambda m: (0, 0)),
                  pl.BlockSpec((1, fh), lambda m: (0, 0)),
                  pl.BlockSpec((1, fh), lambda m: (0, 0)),
                  pl.BlockSpec((fh, ow), lambda m: (0, jnp.where(m >= half_tiles, 1, 0))),
                  pl.BlockSpec((fh, ow), lambda m: (0, 1)),
                  pl.BlockSpec((1, width), lambda m: (0, 0))],
        out_specs=pl.BlockSpec((orders, tm, width // 2), lambda m: (0, m, 0)),
        compiler_params=_params(("arbitrary",), big=True),
        name="hyena_filters",
    )(z, t, fw1p, fb1.reshape(1, fh).astype(F32), fw2.astype(F32), fb2.reshape(1, fh).astype(F32),
      freq.reshape(1, fh).astype(F32), fw3, fw3, deltas)


def _dft_constants():
    n1, n2 = FFT_N1, FFT_N2
    n = n1 * n2
    a1 = np.arange(n1)
    a2 = np.arange(n2)
    half = n2 // 2

    def cplx_block(w):
        return np.block([[w.real, -w.imag], [w.imag, w.real]])

    ang = (a2[None, :, None] * a2[None, None, :] / n2) + (a1[:, None, None] * a2[None, :, None] / n)
    w1 = np.exp(-2j * np.pi * ang)
    f1_pad = np.stack([cplx_block(w1[i][:, :half]) for i in range(n1)])
    f1_real = np.stack([np.concatenate([w1[i].real, w1[i].imag], axis=0) for i in range(n1)])
    w2 = np.exp(-2j * np.pi * (a1[:, None] * a1[None, :]) / n1)
    f2 = cplx_block(w2)
    ang = (a1[None, :, None] * a1[None, None, :] / n1) + (a2[:, None, None] * a1[None, :, None] / n)
    g2 = np.stack([cplx_block(m_) for m_ in np.exp(2j * np.pi * ang)])
    wg1 = np.exp(2j * np.pi * (a2[:half, None] * a2[None, :]) / n2) / n
    g1 = cplx_block(wg1)
    to = lambda x: jnp.asarray(x.astype(np.float32)).astype(BF16)
    return to(f1_pad), to(f1_real), to(f2), to(g2), to(g1)


def _fft_stage1_kernel(x_ref, f_ref, y_ref):
    xr = pltpu.einshape("mjc->jmc", x_ref[0])
    xi = pltpu.einshape("mjc->jmc", x_ref[1])
    for j in range(FFT_CHUNK):
        xc = jnp.concatenate([xr[j], xi[j]], axis=0).astype(BF16)
        y_ref[j] = _pack_bf16_pairs(jnp.dot(f_ref[j], xc, preferred_element_type=F32))


def _fft_stage1_real_kernel(x_ref, f_ref, y_ref):
    x = pltpu.einshape("mjc->jmc", x_ref[...])
    for j in range(FFT_CHUNK):
        y_ref[j] = _pack_bf16_pairs(jnp.dot(f_ref[j], _unpack_bf16_pairs(x[j]),
                                            preferred_element_type=F32))


def _fft_stage2_kernel(yr_ref, yi_ref, f2_ref, h_ref):
    yr = pltpu.einshape("nkc->knc", yr_ref[...])
    yi = pltpu.einshape("nkc->knc", yi_ref[...])
    for j in range(FFT_CHUNK):
        yc = _unpack_bf16_pairs(jnp.concatenate([yr[j], yi[j]], axis=0))
        h_ref[j] = _pack_bf16_pairs(jnp.dot(f2_ref[...], yc, preferred_element_type=F32))


def _fft_mid_kernel(yr_ref, yi_ref, f2_ref, h_ref, g2_ref, q_ref):
    yr = pltpu.einshape("nkc->knc", yr_ref[...])
    yi = pltpu.einshape("nkc->knc", yi_ref[...])
    n1 = FFT_N1
    spec = [jnp.dot(f2_ref[...], _unpack_bf16_pairs(jnp.concatenate([yr[j], yi[j]], axis=0)),
                    preferred_element_type=F32) for j in range(FFT_CHUNK)]
    prod = []
    for j, z in enumerate(spec):
        hf = _unpack_pairs_f32(h_ref[j])
        zr, zi, hr, hi = z[:n1], z[n1:], hf[:n1], hf[n1:]
        prod.append(jnp.concatenate([zr * hr - zi * hi, zr * hi + zi * hr], axis=0).astype(BF16))
    for j, pc in enumerate(prod):
        q_ref[j] = _pack_bf16_pairs(jnp.dot(g2_ref[j], pc, preferred_element_type=F32))


def _fft_last_kernel(qr_ref, qi_ref, g1_ref, gate_ref, zf_ref, fb_ref, o_ref):
    qr = pltpu.einshape("kjc->jkc", qr_ref[...])
    qi = pltpu.einshape("kjc->jkc", qi_ref[...])
    half = FFT_N2 // 2
    conv = []
    for j in range(FFT_CHUNK):
        qc = _unpack_bf16_pairs(jnp.concatenate([qr[j], qi[j]], axis=0))
        conv.append(jnp.dot(g1_ref[...], qc, preferred_element_type=F32))
    conv = jnp.stack(conv, axis=0)
    fb = fb_ref[...]
    for bi in range(2):
        cb = pltpu.einshape("jmc->mjc", conv[:, bi * half:(bi + 1) * half])
        zf = zf_ref[bi]
        o_ref[bi] = gate_ref[bi] * (cb + zf * fb)


def hyena_filter_spectrum(ts, consts, cb=FFT_CB):
    _, f1_real, f2, _, _ = consts
    no, n, c = ts.shape
    c = 2 * c
    n1, n2, ch = FFT_N1, FFT_N2, FFT_CHUNK
    ts4 = ts.reshape(no, n2, n1, c // 2)
    y = pl.pallas_call(
        _fft_stage1_real_kernel,
        out_shape=jax.ShapeDtypeStruct((no, n1, 2 * n2, c // 2), jnp.uint32),
        grid=(no, n1 // ch, c // cb),
        in_specs=[pl.BlockSpec((None, n2, ch, cb // 2), lambda o, j, k: (o, 0, j, k)),
                  pl.BlockSpec((ch, 2 * n2, n2), lambda o, j, k: (j, 0, 0))],
        out_specs=pl.BlockSpec((None, ch, 2 * n2, cb // 2), lambda o, j, k: (o, j, 0, k)),
        compiler_params=_params(("parallel", "parallel", "parallel"), big=True),
        name="fft_filter_stage1",
    )(ts4, f1_real)
    nk = n2 // ch
    return pl.pallas_call(
        _fft_stage2_kernel,
        out_shape=jax.ShapeDtypeStruct((no, n2, 2 * n1, c // 2), jnp.uint32),
        grid=(no, nk, c // cb),
        in_specs=[pl.BlockSpec((None, n1, ch, cb // 2), lambda o, j, k: (o, 0, j, k)),
                  pl.BlockSpec((None, n1, ch, cb // 2), lambda o, j, k: (o, 0, nk + j, k)),
                  pl.BlockSpec((2 * n1, 2 * n1), lambda o, j, k: (0, 0))],
        out_specs=pl.BlockSpec((None, ch, 2 * n1, cb // 2), lambda o, j, k: (o, j, 0, k)),
        compiler_params=_params(("parallel", "parallel", "parallel"), big=True),
        name="fft_filter_stage2",
    )(y, y, f2)


def hyena_long_conv_gate(zsrc, z_col, gate_src, gate_col, hf, order, fbias, consts, cb=FFT_CB):
    f1_pad, _, f2, g2, g1 = consts
    n1, n2, ch = FFT_N1, FFT_N2, FFT_CHUNK
    c = 2 * hf.shape[-1]
    ncb = c // cb
    half = n2 // 2
    y = pl.pallas_call(
        _fft_stage1_kernel,
        out_shape=jax.ShapeDtypeStruct((n1, 2 * n2, c // 2), jnp.uint32),
        grid=(n1 // ch, ncb),
        in_specs=[pl.BlockSpec((2, half, ch, cb), lambda j, k: (0, 0, j, z_col * ncb + k)),
                  pl.BlockSpec((ch, 2 * n2, n2), lambda j, k: (j, 0, 0))],
        out_specs=pl.BlockSpec((ch, 2 * n2, cb // 2), lambda j, k: (j, 0, k)),
        compiler_params=_params(("parallel", "parallel"), big=True),
        name="fft_stage1",
    )(zsrc, f1_pad)
    nk = n2 // ch
    q = pl.pallas_call(
        _fft_mid_kernel,
        out_shape=jax.ShapeDtypeStruct((n2, 2 * n1, c // 2), jnp.uint32),
        grid=(nk, ncb),
        in_specs=[pl.BlockSpec((n1, ch, cb // 2), lambda j, k: (0, j, k)),
                  pl.BlockSpec((n1, ch, cb // 2), lambda j, k: (0, nk + j, k)),
                  pl.BlockSpec((2 * n1, 2 * n1), lambda j, k: (0, 0)),
                  pl.BlockSpec((None, ch, 2 * n1, cb // 2), lambda j, k: (order, j, 0, k)),
                  pl.BlockSpec((ch, 2 * n1, 2 * n1), lambda j, k: (j, 0, 0))],
        out_specs=pl.BlockSpec((ch, 2 * n1, cb // 2), lambda j, k: (j, 0, k)),
        compiler_params=_params(("parallel", "parallel"), big=True),
        name="fft_mid",
    )(y, y, f2, hf, g2)
    nj = n1 // ch
    return pl.pallas_call(
        _fft_last_kernel,
        out_shape=jax.ShapeDtypeStruct((2, half, n1, c), F32),
        grid=(nj, ncb),
        in_specs=[pl.BlockSpec((n2, ch, cb // 2), lambda j, k: (0, j, k)),
                  pl.BlockSpec((n2, ch, cb // 2), lambda j, k: (0, nj + j, k)),
                  pl.BlockSpec((2 * half, 2 * n2), lambda j, k: (0, 0)),
                  pl.BlockSpec((2, half, ch, cb), lambda j, k: (0, 0, j, gate_col * ncb + k)),
                  pl.BlockSpec((2, half, ch, cb), lambda j, k: (0, 0, j, z_col * ncb + k)),
                  pl.BlockSpec((None, 1, cb), lambda j, k: (order, 0, k))],
        out_specs=pl.BlockSpec((2, half, ch, cb), lambda j, k: (0, 0, j, k)),
        compiler_params=_params(("parallel", "parallel"), big=True),
        name="fft_last",
    )(q, q, g1, gate_src, zsrc, fbias.reshape(fbias.shape[0], 1, c))


def hyena_mixer(hn, h_res, w_in, b_in, conv_w, conv_b, fw1, fb1, fw2, fb2, freq, fw3, fbias, w_out,
                batch, seq):
    width = w_out.shape[0]
    assert batch == 2 and 2 * seq == FFT_N1 * FFT_N2
    proj = matmul([hn], w_in, bias=b_in, name="hyena_in")
    sc = short_conv(proj.reshape(batch, seq, 3 * width), conv_w, conv_b)
    consts = _dft_constants()
    ts = hyena_two_sided_filters(seq, fw1, fb1, fw2, fb2, freq, fw3, width)
    hf = hyena_filter_spectrum(ts, consts)
    sc4 = sc.reshape(batch, FFT_N2 // 2, FFT_N1, 3 * width)
    zf1 = hyena_long_conv_gate(sc4, 2, sc4, 0, hf, 0, fbias, consts)
    zf2 = hyena_long_conv_gate(zf1, 0, sc4, 1, hf, 1, fbias, consts)
    return matmul([zf2.reshape(batch * seq, width)], w_out, res=h_res, tm=512, tn=1024,
                  name="hyena_out")


def moe_swiglu(hn2_packed, logits, wg, wu, wd):
    n = hn2_packed.shape[0]
    top_v, top_i = lax.top_k(logits, TOP_K)
    gates = jax.nn.softmax(top_v, axis=-1)
    e_flat = top_i.reshape(-1).astype(jnp.int32)
    tok_flat = jnp.repeat(jnp.arange(n, dtype=jnp.int32), TOP_K)
    nk = n * TOP_K
    onehot = (e_flat[:, None] == jnp.arange(N_EXPERTS, dtype=jnp.int32)[None]).astype(jnp.int32)
    csum = jnp.cumsum(onehot, axis=0)
    rank = jnp.take_along_axis(csum, e_flat[:, None], axis=1)[:, 0] - 1
    counts = csum[-1]
    padded = ((counts + MOE_TILE - 1) // MOE_TILE) * MOE_TILE
    pad_end = jnp.cumsum(padded)
    pad_start = pad_end - padded
    dest = pad_start[e_flat] + rank
    p_rows = nk + N_EXPERTS * MOE_TILE
    row_tok = jnp.zeros((p_rows,), jnp.int32).at[dest].set(tok_flat)
    nt = p_rows // MOE_TILE
    tile_start = jnp.arange(nt, dtype=jnp.int32) * MOE_TILE
    tile_used = tile_start < pad_end[-1]
    tile_exp = jnp.minimum(jnp.searchsorted(pad_end, tile_start, side='right'), N_EXPERTS - 1).astype(jnp.int32)
    tile_rows = jnp.clip(counts[tile_exp] - (tile_start - pad_start[tile_exp]), 0, MOE_TILE)
    tile_rows = jnp.where(tile_used, tile_rows, 0).astype(jnp.int32)
    last_exp = tile_exp[jnp.maximum(jnp.sum(tile_used.astype(jnp.int32)) - 1, 0)]
    tile_exp = jnp.where(tile_used, tile_exp, last_exp)
    ys = swiglu(hn2_packed, wg, wu, wd, tile_exp, tile_rows, row_tok, sub_rows=FFN_SUB_ROWS,
                name="swiglu_experts")
    return ys, dest.reshape(n, TOP_K), gates


def kernel(x, p, ln_mix, ln_ffn, ln_ple, final_norm, t5_bias, w_attn_in, w_attn_out, attn_sink, na_rpb, w_ffn_gate, w_ffn_up, w_ffn_down, w_hy_in, b_hy_in, w_hy_conv, b_hy_conv, w_hy_f1, b_hy_f1, w_hy_f2, b_hy_f2, hy_freq, w_hy_f3, hy_bias, w_hy_out, w_router, w_exp_gate, w_exp_up, w_exp_down, w_ple_proj, w_ple_gate):
    batch, seq, d = x.shape
    n = batch * seq
    depth = ln_mix.shape[0]
    h = x.reshape(n, d)
    for i in range(depth):
        li = i // 2
        hn = rmsnorm(h, ln_mix[i], BF16)
        if i % 2 == 0:
            na_off = A_WIDTH + 2 * A_KV_WIDTH
            w_in = w_attn_in[li]
            proj_a = matmul([hn], w_in[:, :na_off], out_dtype=BF16, tn=640, name="attn_in_a")
            proj_n = matmul([hn], w_in[:, na_off:], out_dtype=BF16, name="attn_in_n")
            oa = window_attention(proj_a.reshape(batch, seq, -1), t5_bias, attn_sink[li])
            ob = neighbourhood_attention(proj_n.reshape(batch, seq, -1), na_rpb[li])
            h = matmul([oa.reshape(n, A_WIDTH), ob.reshape(n, B_WIDTH)], w_attn_out[li], res=h,
                       name="attn_out")
            hn2 = rmsnorm(h, ln_ffn[i], BF16)
            nt = n // MOE_TILE
            delta = swiglu(hn2, w_ffn_gate, w_ffn_up, w_ffn_down,
                           jnp.full((nt,), li, jnp.int32), jnp.full((nt,), MOE_TILE, jnp.int32),
                           name="swiglu_dense")
            h, hn3 = add_rmsnorm(h, delta, ln_ple[i])
        else:
            h = hyena_mixer(hn, h, w_hy_in[li], b_hy_in[li], w_hy_conv[li], b_hy_conv[li],
                            w_hy_f1[li], b_hy_f1[li], w_hy_f2[li], b_hy_f2[li], hy_freq[li],
                            w_hy_f3[li], hy_bias[li], w_hy_out[li], batch, seq)
            wr_pad = jnp.pad(w_router[li].astype(F32), ((0, 0), (0, 128 - N_EXPERTS)))
            hn2_packed, logits = rmsnorm_router(h, ln_ffn[i], wr_pad)
            ys, dest2, gates = moe_swiglu(hn2_packed, logits[:, :N_EXPERTS], w_exp_gate[li],
                                          w_exp_up[li], w_exp_down[li])
            h, hn3 = moe_combine_rmsnorm(h, ys, dest2, gates, ln_ple[i])
        h = ple(hn3, p[i].reshape(n, -1), h, w_ple_gate[i], w_ple_proj[i])
    return rmsnorm(h, final_norm, F32).reshape(batch, seq, d)
```

```python
import functools
import math

import jax
import jax.numpy as jnp
import numpy as np
from jax import lax
from jax.experimental import pallas as pl
from jax.experimental.pallas import tpu as pltpu

F32 = jnp.float32
BF16 = jnp.bfloat16
NEG_INF = -1e30
RMS_EPS = 1e-6

V7X_VMEM_LIMIT_BYTES = 56 * 1024 * 1024

HEAD_DIM = 64
A_Q_HEADS = 16
A_KV_HEADS = 2
A_GROUP = A_Q_HEADS // A_KV_HEADS
A_BLOCK = 128
T5_BUCKETS = 32
T5_MAX_DIST = 128
B_HEADS = 16
GRID_W = 64
NA_WIN_H = 8
NA_WIN_W = 16
A_WIDTH = A_Q_HEADS * HEAD_DIM
A_KV_WIDTH = A_KV_HEADS * HEAD_DIM
B_WIDTH = B_HEADS * HEAD_DIM
N_EXPERTS = 8
TOP_K = 2
HYENA_EMB = 33
HYENA_DECAY_TARGET = 1e-2
HYENA_FAST_PCT = 0.3
HYENA_SLOW_PCT = 1.5

FFT_N1 = 64
FFT_N2 = 128
FFT_CHUNK = 8
FFT_CB = 1024

MOE_TILE = 1024
FFN_F_TILE = 256
FFN_SUB_ROWS = 512
DMA_LOOP_UNROLL = 8


def _params(semantics, big=False):
    return pltpu.CompilerParams(
        dimension_semantics=semantics,
        vmem_limit_bytes=V7X_VMEM_LIMIT_BYTES if big else None)


def _rmsnorm_kernel(x_ref, g_ref, o_ref):
    x = x_ref[...]
    y = x * lax.rsqrt(jnp.mean(x * x, axis=-1, keepdims=True) + RMS_EPS)
    o_ref[...] = (y * g_ref[...]).astype(o_ref.dtype)


def rmsnorm(x2, g, out_dtype, tm=512):
    n, d = x2.shape
    return pl.pallas_call(
        _rmsnorm_kernel,
        out_shape=jax.ShapeDtypeStruct((n, d), out_dtype),
        grid=(n // tm,),
        in_specs=[pl.BlockSpec((tm, d), lambda i: (i, 0)),
                  pl.BlockSpec((1, d), lambda i: (0, 0))],
        out_specs=pl.BlockSpec((tm, d), lambda i: (i, 0)),
        compiler_params=_params(("parallel",)),
        name="rmsnorm",
    )(x2, g.reshape(1, d))


def _pack_bf16_pairs(y):
    w = y.shape[1] // 2
    bits = pltpu.bitcast(y.astype(BF16).astype(F32), jnp.uint32)
    return (bits[:, :w] >> 16) | (bits[:, w:] & jnp.uint32(0xFFFF0000))


def _unpack_pairs_f32(p):
    lo = pltpu.bitcast(p << 16, F32)
    hi = pltpu.bitcast(p & jnp.uint32(0xFFFF0000), F32)
    return jnp.concatenate([lo, hi], axis=1)


def _unpack_bf16_pairs(p):
    return _unpack_pairs_f32(p).astype(BF16)


def _rmsnorm_router_kernel(x_ref, g_ref, wr_ref, o_ref, l_ref):
    x = x_ref[...]
    y = x * lax.rsqrt(jnp.mean(x * x, axis=-1, keepdims=True) + RMS_EPS)
    y = y * g_ref[...]
    o_ref[...] = _pack_bf16_pairs(y)
    l_ref[...] = jnp.dot(y, wr_ref[...], preferred_element_type=F32,
                         precision=lax.Precision.HIGHEST)


def rmsnorm_router(x2, g, w_router_pad, tm=512):
    n, d = x2.shape
    ne = w_router_pad.shape[1]
    return pl.pallas_call(
        _rmsnorm_router_kernel,
        out_shape=(jax.ShapeDtypeStruct((n, d // 2), jnp.uint32), jax.ShapeDtypeStruct((n, ne), F32)),
        grid=(n // tm,),
        in_specs=[pl.BlockSpec((tm, d), lambda i: (i, 0)),
                  pl.BlockSpec((1, d), lambda i: (0, 0)),
                  pl.BlockSpec((d, ne), lambda i: (0, 0))],
        out_specs=(pl.BlockSpec((tm, d // 2), lambda i: (i, 0)),
                   pl.BlockSpec((tm, ne), lambda i: (i, 0))),
        compiler_params=_params(("parallel",)),
        name="rmsnorm_router",
    )(x2, g.reshape(1, d), w_router_pad)


def _mm_kernel(*refs, n_a, has_bias, has_res):
    a_refs = refs[:n_a]
    w_refs = refs[n_a:2 * n_a]
    idx = 2 * n_a
    bias_ref = refs[idx] if has_bias else None
    idx += int(has_bias)
    res_ref = refs[idx] if has_res else None
    idx += int(has_res)
    o_ref = refs[idx]
    wbf_refs = refs[idx + 1:idx + 1 + n_a]

    @pl.when(pl.program_id(1) == 0)
    def _():
        for w_ref, wbf_ref in zip(w_refs, wbf_refs):
            wbf_ref[...] = w_ref[...].astype(BF16)

    acc = None
    for a_ref, wbf_ref in zip(a_refs, wbf_refs):
        d = jnp.dot(a_ref[...].astype(BF16), wbf_ref[...], preferred_element_type=F32)
        acc = d if acc is None else acc + d
    if has_bias:
        acc = acc + bias_ref[...]
    if has_res:
        acc = acc + res_ref[...]
    o_ref[...] = acc.astype(o_ref.dtype)


def matmul(a_list, w, *, bias=None, res=None, out_dtype=F32, tm=1024, tn=1024, name="matmul"):
    m = a_list[0].shape[0]
    n = w.shape[1]
    n_a = len(a_list)
    k_each = a_list[0].shape[1]
    assert all(a.shape == (m, k_each) for a in a_list) and w.shape[0] == n_a * k_each
    in_specs = [pl.BlockSpec((tm, k_each), lambda j, i: (i, 0)) for _ in a_list]
    in_specs += [pl.BlockSpec((k_each, tn), functools.partial(lambda j, i, kb: (kb, j), kb=kb))
                 for kb in range(n_a)]
    args = list(a_list) + [w] * n_a
    if bias is not None:
        in_specs.append(pl.BlockSpec((1, tn), lambda j, i: (0, j)))
        args.append(bias.reshape(1, n))
    if res is not None:
        in_specs.append(pl.BlockSpec((tm, tn), lambda j, i: (i, j)))
        args.append(res)
    return pl.pallas_call(
        functools.partial(_mm_kernel, n_a=n_a, has_bias=bias is not None, has_res=res is not None),
        out_shape=jax.ShapeDtypeStruct((m, n), out_dtype),
        grid=(n // tn, m // tm),
        in_specs=in_specs,
        out_specs=pl.BlockSpec((tm, tn), lambda j, i: (i, j)),
        scratch_shapes=[pltpu.VMEM((k_each, tn), BF16) for _ in a_list],
        compiler_params=_params(("parallel", "arbitrary"), big=True),
        name=name,
    )(*args)


def _ple_kernel(hn_ref, p_ref, h_ref, wg_ref, wp_ref, o_ref, wg_bf, wp_bf):
    @pl.when(pl.program_id(1) == 0)
    def _():
        wg_bf[...] = wg_ref[...].astype(BF16)
        wp_bf[...] = wp_ref[...].astype(BF16)

    a = jnp.dot(hn_ref[...], wg_bf[...], preferred_element_type=F32)
    pp = jnp.dot(p_ref[...].astype(BF16), wp_bf[...], preferred_element_type=F32)
    o_ref[...] = h_ref[...] + jax.nn.sigmoid(a) * pp


def ple(hn, p2, h2, w_gate, w_proj, tm=1024, tn=1024):
    m, d = h2.shape
    pd = p2.shape[1]
    return pl.pallas_call(
        _ple_kernel,
        out_shape=jax.ShapeDtypeStruct((m, d), F32),
        grid=(d // tn, m // tm),
        in_specs=[pl.BlockSpec((tm, d), lambda j, i: (i, 0)),
                  pl.BlockSpec((tm, pd), lambda j, i: (i, 0)),
                  pl.BlockSpec((tm, tn), lambda j, i: (i, j)),
                  pl.BlockSpec((d, tn), lambda j, i: (0, j)),
                  pl.BlockSpec((pd, tn), lambda j, i: (0, j))],
        out_specs=pl.BlockSpec((tm, tn), lambda j, i: (i, j)),
        scratch_shapes=[pltpu.VMEM((d, tn), BF16), pltpu.VMEM((pd, tn), BF16)],
        compiler_params=_params(("parallel", "arbitrary"), big=True),
        name="ple",
    )(hn, p2, h2, w_gate, w_proj)


def _row_copy(src_hbm, row, dst, i, sem):
    return pltpu.make_async_copy(src_hbm.at[pl.ds(row, 1), :], dst.at[pl.ds(i, 1), :], sem)


def _swiglu_kernel(exp_ref, rows_ref, src_ref, tok_ref, x_ref, wg_ref, wu_ref, wd_ref, *rest,
                   n_col, gather, sub_rows, has_res):
    del exp_ref
    res_ref = rest[0] if has_res else None
    o_ref = rest[int(has_res)]
    scratch = rest[int(has_res) + 1:]
    t = pl.program_id(0)
    f = pl.program_id(1)
    nt = pl.num_programs(0)
    n_rows = rows_ref[t]
    used = n_rows > 0
    tm = o_ref.shape[0]

    @pl.when(f == 0)
    def _():
        o_ref[...] = res_ref[...] if has_res else jnp.zeros_like(o_ref)

    if gather:
        xbuf, xbf, sems = scratch
        slot = t % 2

        def for_each_row(tile, fn):
            def body(c, carry):
                for r in range(DMA_LOOP_UNROLL):
                    fn(c * DMA_LOOP_UNROLL + r)
                return carry
            lax.fori_loop(0, (rows_ref[tile] + DMA_LOOP_UNROLL - 1) // DMA_LOOP_UNROLL, body, 0)

        def start_tile(tile, sl):
            for_each_row(tile, lambda i: _row_copy(x_ref, tok_ref[src_ref[tile] + i], xbuf.at[sl], i,
                                                   sems.at[sl, i]).start())

        @pl.when(jnp.logical_and(f == 0, t == 0))
        def _():
            xbuf[...] = jnp.zeros_like(xbuf)
            start_tile(0, 0)

        @pl.when(jnp.logical_and(f == 0, used))
        def _():
            for_each_row(t, lambda i: _row_copy(x_ref, 0, xbuf.at[slot], i, sems.at[slot, i]).wait())
            xbf[...] = _unpack_bf16_pairs(xbuf[slot])

        @pl.when(jnp.logical_and(f == 1, t + 1 < nt))
        def _():
            start_tile(jnp.minimum(t + 1, nt - 1), 1 - slot)

        x_src = xbf
    else:
        x_src = x_ref

    for sb in range(tm // sub_rows):
        rs = slice(sb * sub_rows, (sb + 1) * sub_rows)

        @pl.when(sb * sub_rows < n_rows)
        def _():
            x = x_src[rs, :]
            g = jnp.dot(x, wg_ref[...].astype(BF16), preferred_element_type=F32)
            u = jnp.dot(x, wu_ref[...].astype(BF16), preferred_element_type=F32)
            hmid = (g * jax.nn.sigmoid(g) * u).astype(BF16)
            wd = wd_ref[...].astype(BF16)
            cw = o_ref.shape[1] // n_col
            for c in range(n_col):
                y = jnp.dot(hmid, wd[:, c * cw:(c + 1) * cw], preferred_element_type=F32)
                o_ref[rs, c * cw:(c + 1) * cw] += y


def swiglu(x, w_gate, w_up, w_down, tile_exp, tile_rows, tile_src=None, src_rows=None, *, res=None,
           tm=MOE_TILE, tf=FFN_F_TILE, sub_rows=MOE_TILE, name="swiglu"):
    gather = src_rows is not None
    has_res = res is not None
    m = tile_rows.shape[0] * tm
    d = w_gate.shape[1]
    dff = w_gate.shape[2]
    nf = dff // tf
    nt = m // tm
    assert nf >= 2

    def w_in_map(t, f, exp_ref, rows_ref, src_ref, tok_ref):
        return (exp_ref[t], 0, jnp.where(rows_ref[t] > 0, f, nf - 1))

    def w_out_map(t, f, exp_ref, rows_ref, src_ref, tok_ref):
        return (exp_ref[t], jnp.where(rows_ref[t] > 0, f, nf - 1), 0)

    def tile_map(t, f, exp_ref, rows_ref, src_ref, tok_ref):
        return (t, 0)

    if gather:
        x_spec = pl.BlockSpec(memory_space=pl.ANY)
        scratch = [pltpu.VMEM((2, tm, d // 2), jnp.uint32), pltpu.VMEM((tm, d), BF16),
                   pltpu.SemaphoreType.DMA((2, tm))]
    else:
        x_spec = pl.BlockSpec((tm, d), tile_map)
        scratch = []
        tile_src = jnp.zeros((nt,), jnp.int32)
        src_rows = jnp.zeros((1,), jnp.int32)
    in_specs = [x_spec,
                pl.BlockSpec((None, d, tf), w_in_map),
                pl.BlockSpec((None, d, tf), w_in_map),
                pl.BlockSpec((None, tf, d), w_out_map)]
    args = [x, w_gate, w_up, w_down]
    if has_res:
        in_specs.append(pl.BlockSpec((tm, d), tile_map, pipeline_mode=pl.Buffered(1)))
        args.append(res)
    grid_spec = pltpu.PrefetchScalarGridSpec(
        num_scalar_prefetch=4,
        grid=(nt, nf),
        in_specs=in_specs,
        out_specs=pl.BlockSpec((tm, d), tile_map),
        scratch_shapes=scratch,
    )
    return pl.pallas_call(
        functools.partial(_swiglu_kernel, n_col=4, gather=gather, sub_rows=sub_rows, has_res=has_res),
        out_shape=jax.ShapeDtypeStruct((m, d), F32),
        grid_spec=grid_spec,
        compiler_params=_params(("arbitrary", "arbitrary"), big=True),
        name=name,
    )(tile_exp, tile_rows, tile_src, src_rows, *args)


COMBINE_TILE = 256


def _combine_kernel(d0_ref, d1_ref, ys_hbm, gate_ref, h_ref, g_ref, hout_ref, hn_ref, buf, sems):
    i = pl.program_id(0)
    n = pl.num_programs(0)
    tt = h_ref.shape[0]
    slot = i % 2
    dests = (d0_ref, d1_ref)

    def start_tile(tile, sl):
        def body(r, c):
            for k, d_ref in enumerate(dests):
                _row_copy(ys_hbm, d_ref[tile * tt + r], buf.at[sl, k], r, sems.at[sl, k, r]).start()
            return c
        lax.fori_loop(0, tt, body, 0, unroll=DMA_LOOP_UNROLL)

    @pl.when(i == 0)
    def _():
        start_tile(0, 0)

    @pl.when(i + 1 < n)
    def _():
        start_tile(i + 1, 1 - slot)

    def wait_body(r, c):
        for k in range(TOP_K):
            _row_copy(ys_hbm, 0, buf.at[slot, k], r, sems.at[slot, k, r]).wait()
        return c
    lax.fori_loop(0, tt, wait_body, 0, unroll=DMA_LOOP_UNROLL)

    gates = gate_ref[...]
    x = h_ref[...] + (buf[slot, 0] * gates[:, 0:1] + buf[slot, 1] * gates[:, 1:2])
    hout_ref[...] = x
    y = x * lax.rsqrt(jnp.mean(x * x, axis=-1, keepdims=True) + RMS_EPS)
    hn_ref[...] = (y * g_ref[...]).astype(hn_ref.dtype)


def moe_combine_rmsnorm(h2, ys, dest2, gates, g, tt=COMBINE_TILE):
    n, d = h2.shape
    grid_spec = pltpu.PrefetchScalarGridSpec(
        num_scalar_prefetch=2,
        grid=(n // tt,),
        in_specs=[pl.BlockSpec(memory_space=pl.ANY),
                  pl.BlockSpec((tt, TOP_K), lambda i, a, b: (i, 0)),
                  pl.BlockSpec((tt, d), lambda i, a, b: (i, 0)),
                  pl.BlockSpec((1, d), lambda i, a, b: (0, 0))],
        out_specs=(pl.BlockSpec((tt, d), lambda i, a, b: (i, 0)),
                   pl.BlockSpec((tt, d), lambda i, a, b: (i, 0))),
        scratch_shapes=[pltpu.VMEM((2, TOP_K, tt, d), F32), pltpu.SemaphoreType.DMA((2, TOP_K, tt))],
    )
    return pl.pallas_call(
        _combine_kernel,
        out_shape=(jax.ShapeDtypeStruct((n, d), F32), jax.ShapeDtypeStruct((n, d), BF16)),
        grid_spec=grid_spec,
        compiler_params=_params(("arbitrary",), big=True),
        name="moe_combine",
    )(dest2[:, 0], dest2[:, 1], ys, gates, h2, g.reshape(1, d))


def _t5_bucket(rel):
    half = T5_BUCKETS // 2
    max_exact = half // 2
    n = jnp.abs(rel)
    log_ratio = jnp.log(jnp.maximum(n, 1).astype(F32) / max_exact) / math.log(T5_MAX_DIST / max_exact)
    large = jnp.minimum(max_exact + (log_ratio * (half - max_exact)).astype(jnp.int32), half - 1)
    return jnp.where(rel > 0, half, 0) + jnp.where(n < max_exact, n, large)


def _window_bias_table(t5_bias):
    i = jnp.arange(A_BLOCK)[:, None]
    j = jnp.arange(3 * A_BLOCK)[None, :]
    rel = j - A_BLOCK - i
    onehot = (_t5_bucket(rel)[None] == jnp.arange(T5_BUCKETS)[:, None, None]).astype(F32)
    bias = jnp.einsum('bh,bij->hij', t5_bias.astype(F32), onehot, precision=lax.Precision.HIGHEST)
    return jnp.where((jnp.abs(rel) <= A_BLOCK)[None], bias, NEG_INF)


def _window_kernel(sink_ref, q_ref, kv_ref, bias_ref, o_ref, *, nb):
    n = pl.program_id(1)
    scale = HEAD_DIM ** -0.5
    starts = (jnp.maximum(n - 1, 0), n, jnp.minimum(n + 1, nb - 1))
    kv = [kv_ref[pl.ds(pl.multiple_of(s * A_BLOCK, A_BLOCK), A_BLOCK), :] for s in starts]
    col = lax.broadcasted_iota(jnp.int32, (A_BLOCK, 3 * A_BLOCK), 1)
    edge_ok = jnp.logical_and(jnp.logical_or(n > 0, col >= A_BLOCK),
                              jnp.logical_or(n < nb - 1, col < 2 * A_BLOCK))
    kv = jnp.concatenate(kv, axis=0)
    q = q_ref[...]
    lo = lax.broadcasted_iota(jnp.int32, (1, 2 * HEAD_DIM), 1) < HEAD_DIM
    sel = (jnp.where(lo, scale, 0.0).astype(BF16), jnp.where(lo, 0.0, scale).astype(BF16))
    pairs_per_kv = A_GROUP // 2
    kk, vv = [], []
    for kh in range(A_KV_HEADS):
        k1 = kv[:, kh * HEAD_DIM:(kh + 1) * HEAD_DIM]
        v1 = kv[:, A_KV_WIDTH + kh * HEAD_DIM:A_KV_WIDTH + (kh + 1) * HEAD_DIM]
        kk.append(jnp.concatenate([k1, k1], axis=1))
        vv.append(jnp.concatenate([v1, v1], axis=1))
    scores = []
    for pr in range(A_Q_HEADS // 2):
        q2 = q[:, pr * 2 * HEAD_DIM:(pr + 1) * 2 * HEAD_DIM]
        qq = jnp.concatenate([q2 * sel[0], q2 * sel[1]], axis=0)
        s2 = lax.dot_general(qq, kk[pr // pairs_per_kv], (((1,), (1,)), ((), ())),
                             preferred_element_type=F32)
        for i in range(2):
            scores.append(jnp.where(edge_ok, s2[i * A_BLOCK:(i + 1) * A_BLOCK] + bias_ref[2 * pr + i],
                                    NEG_INF))
    probs, dens = [], []
    for h, s in enumerate(scores):
        sink = sink_ref[h]
        mx = jnp.maximum(jnp.max(s, axis=-1, keepdims=True), sink)
        p = jnp.exp(s - mx)
        dens.append(jnp.sum(p, axis=-1, keepdims=True) + jnp.exp(sink - mx))
        probs.append(p.astype(BF16))
    for pr in range(A_Q_HEADS // 2):
        v2 = vv[pr // pairs_per_kv]
        oa = jnp.dot(probs[2 * pr], v2, preferred_element_type=F32) / dens[2 * pr]
        ob = jnp.dot(probs[2 * pr + 1], v2, preferred_element_type=F32) / dens[2 * pr + 1]
        o_ref[:, pr * 2 * HEAD_DIM:(pr + 1) * 2 * HEAD_DIM] = jnp.where(lo, oa, ob).astype(o_ref.dtype)


def window_attention(proj_a, t5_bias, sink):
    b, s, _ = proj_a.shape
    nb = s // A_BLOCK
    kvw = 2 * A_KV_WIDTH
    grid_spec = pltpu.PrefetchScalarGridSpec(
        num_scalar_prefetch=0,
        grid=(b, nb),
        in_specs=[pl.BlockSpec(memory_space=pltpu.SMEM),
                  pl.BlockSpec((None, A_BLOCK, A_WIDTH), lambda bi, n: (bi, n, 0)),
                  pl.BlockSpec((None, s, kvw), lambda bi, n: (bi, 0, A_WIDTH // kvw)),
                  pl.BlockSpec((A_Q_HEADS, A_BLOCK, 3 * A_BLOCK), lambda bi, n: (0, 0, 0))],
        out_specs=pl.BlockSpec((None, A_BLOCK, A_WIDTH), lambda bi, n: (bi, n, 0)),
    )
    return pl.pallas_call(
        functools.partial(_window_kernel, nb=nb),
        out_shape=jax.ShapeDtypeStruct((b, s, A_WIDTH), BF16),
        grid_spec=grid_spec,
        compiler_params=_params(("parallel", "arbitrary")),
        name="window_attention",
    )(sink.astype(F32), proj_a, proj_a, _window_bias_table(t5_bias))


NA_HEAD_GROUP = 16


def _na_bias_table(rpb, rows):
    kh = min(NA_WIN_H, rows)
    kw = NA_WIN_W
    c = jnp.arange(GRID_W)
    cs = jnp.clip(c - kw // 2, 0, GRID_W - kw)
    col_ok = (c[None] >= cs[:, None]) & (c[None] < cs[:, None] + kw)
    col_off = jnp.clip(c[None] - c[:, None], -(kw - 1), kw - 1) + kw - 1
    onehot = (col_off[None] == jnp.arange(2 * kw - 1)[:, None, None]).astype(F32)
    by_col = jnp.einsum('hrc,cqk->hrqk', rpb.astype(F32), onehot, precision=lax.Precision.HIGHEST)
    by_col = jnp.where(col_ok[None, None], by_col, NEG_INF)
    tabs = [by_col[:, NA_WIN_H - 1 - d:NA_WIN_H - 1 - d + kh] for d in range(kh)]
    bias = jnp.stack(tabs, axis=0).transpose(0, 1, 3, 2, 4)
    return bias.reshape(kh, rpb.shape[0], GRID_W, kh * GRID_W)


def _na_kernel(q_ref, k_ref, v_ref, bias_ref, o_ref, *, rows):
    r = pl.program_id(2)
    kh = min(NA_WIN_H, rows)
    scale = HEAD_DIM ** -0.5
    start = pl.multiple_of(jnp.clip(r - kh // 2, 0, rows - kh) * GRID_W, GRID_W)
    k = k_ref[pl.ds(start, kh * GRID_W), :]
    v = v_ref[pl.ds(start, kh * GRID_W), :]
    q = q_ref[...]
    lo = lax.broadcasted_iota(jnp.int32, (1, 2 * HEAD_DIM), 1) < HEAD_DIM
    sel = (jnp.where(lo, scale, 0.0).astype(BF16), jnp.where(lo, 0.0, scale).astype(BF16))
    n_pairs = NA_HEAD_GROUP // 2
    scores = []
    for pr in range(n_pairs):
        sl = slice(pr * 2 * HEAD_DIM, (pr + 1) * 2 * HEAD_DIM)
        q2 = q[:, sl]
        qq = jnp.concatenate([q2 * sel[0], q2 * sel[1]], axis=0)
        s2 = lax.dot_general(qq, k[:, sl], (((1,), (1,)), ((), ())), preferred_element_type=F32)
        for i in range(2):
            scores.append(s2[i * GRID_W:(i + 1) * GRID_W] + bias_ref[2 * pr + i])
    probs, dens = [], []
    for s in scores:
        mx = jnp.max(s, axis=-1, keepdims=True)
        p = jnp.exp(s - mx)
        dens.append(jnp.sum(p, axis=-1, keepdims=True))
        probs.append(p.astype(BF16))
    for pr in range(n_pairs):
        sl = slice(pr * 2 * HEAD_DIM, (pr + 1) * 2 * HEAD_DIM)
        oa = jnp.dot(probs[2 * pr], v[:, sl], preferred_element_type=F32) / dens[2 * pr]
        ob = jnp.dot(probs[2 * pr + 1], v[:, sl], preferred_element_type=F32) / dens[2 * pr + 1]
        o_ref[:, sl] = jnp.where(lo, oa, ob).astype(o_ref.dtype)


def neighbourhood_attention(proj_n, rpb):
    b, s, _ = proj_n.shape
    rows = s // GRID_W
    kh = min(NA_WIN_H, rows)
    gw = NA_HEAD_GROUP * HEAD_DIM
    ng = B_WIDTH // gw

    def bias_map(bi, g, r):
        return (r - jnp.clip(r - kh // 2, 0, rows - kh), g, 0, 0)

    return pl.pallas_call(
        functools.partial(_na_kernel, rows=rows),
        out_shape=jax.ShapeDtypeStruct((b, s, B_WIDTH), BF16),
        grid=(b, ng, rows),
        in_specs=[pl.BlockSpec((None, GRID_W, gw), lambda bi, g, r: (bi, r, g)),
                  pl.BlockSpec((None, s, gw), lambda bi, g, r: (bi, 0, ng + g)),
                  pl.BlockSpec((None, s, gw), lambda bi, g, r: (bi, 0, 2 * ng + g)),
                  pl.BlockSpec((None, NA_HEAD_GROUP, GRID_W, kh * GRID_W), bias_map)],
        out_specs=pl.BlockSpec((None, GRID_W, gw), lambda bi, g, r: (bi, r, g)),
        compiler_params=_params(("parallel", "parallel", "arbitrary"), big=True),
        name="neighbourhood_attention",
    )(proj_n, proj_n, proj_n, _na_bias_table(rpb, rows))


def _short_conv_kernel(prev_ref, cur_ref, next_ref, w_ref, b_ref, o_ref):
    i = pl.program_id(1)
    nt = pl.num_programs(1)
    x = cur_ref[...]
    tl = x.shape[0]
    row = lax.broadcasted_iota(jnp.int32, x.shape, 0)
    prev_row = jnp.where(i > 0, prev_ref[7:8, :], 0.0)
    next_row = jnp.where(i < nt - 1, next_ref[0:1, :], 0.0)
    up = jnp.where(row == 0, prev_row, pltpu.roll(x, 1, 0))
    dn = jnp.where(row == tl - 1, next_row, pltpu.roll(x, tl - 1, 0))
    o_ref[...] = up * w_ref[0:1, :] + x * w_ref[1:2, :] + dn * w_ref[2:3, :] + b_ref[...]


def short_conv(proj, conv_w, conv_b, tl=512, tc=1024):
    b, l, c = proj.shape
    nl8 = l // 8
    return pl.pallas_call(
        _short_conv_kernel,
        out_shape=jax.ShapeDtypeStruct((b, l, c), F32),
        grid=(b, l // tl, c // tc),
        in_specs=[pl.BlockSpec((None, 8, tc), lambda bi, i, j: (bi, jnp.maximum(i * (tl // 8) - 1, 0), j)),
                  pl.BlockSpec((None, tl, tc), lambda bi, i, j: (bi, i, j)),
                  pl.BlockSpec((None, 8, tc), lambda bi, i, j: (bi, jnp.minimum((i + 1) * (tl // 8), nl8 - 1), j)),
                  pl.BlockSpec((3, tc), lambda bi, i, j: (0, j)),
                  pl.BlockSpec((1, tc), lambda bi, i, j: (0, j))],
        out_specs=pl.BlockSpec((None, tl, tc), lambda bi, i, j: (bi, i, j)),
        compiler_params=_params(("parallel", "parallel", "parallel")),
        name="short_conv",
    )(proj, proj, proj, conv_w, conv_b.reshape(1, c))


def _filter_kernel(z_ref, t_ref, fw1_ref, fb1_ref, fw2_ref, fb2_ref, fr_ref, w3_ref, w3b_ref,
                   delta_ref, o_ref, *, seq):
    m = pl.program_id(0)
    hi = lax.Precision.HIGHEST
    fr = fr_ref[...]
    hid = jnp.sin(fr * (jnp.dot(z_ref[...], fw1_ref[...], preferred_element_type=F32, precision=hi)
                        + fb1_ref[...]))
    hid = jnp.sin(fr * (jnp.dot(hid, fw2_ref[...], preferred_element_type=F32, precision=hi)
                        + fb2_ref[...]))
    hid_bf = hid.astype(BF16)
    h = jnp.dot(hid_bf, w3_ref[...].astype(BF16), preferred_element_type=F32)
    tm = h.shape[0]
    width = delta_ref.shape[1]
    row = lax.broadcasted_iota(jnp.int32, (tm, width), 0) + m * tm
    decay = jnp.where(row == seq, 0.0, jnp.exp(-t_ref[...] * delta_ref[...]))
    def emit(taps):
        for o in range(o_ref.shape[0]):
            for kb in range(width // FFT_CB):
                cols = slice(kb * FFT_CB, (kb + 1) * FFT_CB)
                blk = taps[:, o * width:(o + 1) * width][:, cols] * decay[:, cols]
                o_ref[o, :, kb * (FFT_CB // 2):(kb + 1) * (FFT_CB // 2)] = _pack_bf16_pairs(blk)

    @pl.when(m != 0)
    def _():
        emit(h)

    @pl.when(m == 0)
    def _():
        h_back = jnp.dot(hid_bf, w3b_ref[...].astype(BF16), preferred_element_type=F32)
        first = lax.broadcasted_iota(jnp.int32, h.shape, 0) == 0
        emit(h + jnp.where(first, h_back, 0.0))


def hyena_two_sided_filters(seq, fw1, fb1, fw2, fb2, freq, fw3, width, tm=512):
    n = 2 * seq
    u = jnp.arange(n)
    pos = jnp.where(u <= seq, u, n - u)
    pos = jnp.where(u == seq, 0, pos)
    bands = (HYENA_EMB - 1) // 2
    f = jnp.linspace(1e-4, bands - 1, bands, dtype=F32)[None]
    t = (pos.astype(F32) / (seq - 1))[:, None]
    w = ((2.0 * math.pi / seq) * pos.astype(F32))[:, None]
    z = jnp.concatenate([t, jnp.cos(f * w), -jnp.sin(f * w)], axis=-1)
    emb_pad = 40
    z = jnp.pad(z, ((0, 0), (0, emb_pad - HYENA_EMB)))
    fw1p = jnp.pad(fw1.astype(F32), ((0, emb_pad - HYENA_EMB), (0, 0)))
    fh = fw1.shape[1]
    max_decay = math.log(HYENA_DECAY_TARGET) / HYENA_FAST_PCT
    min_decay = math.log(HYENA_DECAY_TARGET) / HYENA_SLOW_PCT
    deltas = jnp.abs(jnp.linspace(min_decay, max_decay, width, dtype=F32))[None]
    half_tiles = seq // tm
    orders = fw3.shape[1] // (2 * width)
    ow = orders * width

    return pl.pallas_call(
        functools.partial(_filter_kernel, seq=seq),
        out_shape=jax.ShapeDtypeStruct((orders, n, width // 2), jnp.uint32),
        grid=(n // tm,),
        in_specs=[pl.BlockSpec((tm, emb_pad), lambda m: (m, 0)),
                  pl.BlockSpec((tm, 1), lambda m: (m, 0)),
                  pl.BlockSpec((emb_pad, fh), lambda m: (0, 0)),
                  pl.BlockSpec((1, fh), lambda m: (0, 0)),
                  pl.BlockSpec((fh, fh), lambda m: (0, 0)),
                  pl.BlockSpec((1, fh), lambda m: (0, 0)),
                  pl.BlockSpec((1, fh), lambda m: (0, 0)),
                  pl.BlockSpec((fh, ow), lambda m: (0, jnp.where(m >= half_tiles, 1, 0))),
                  pl.BlockSpec((fh, ow), lambda m: (0, 1)),
                  pl.BlockSpec((1, width), lambda m: (0, 0))],
        out_specs=pl.BlockSpec((orders, tm, width // 2), lambda m: (0, m, 0)),
        compiler_params=_params(("arbitrary",), big=True),
        name="hyena_filters",
    )(z, t, fw1p, fb1.reshape(1, fh).astype(F32), fw2.astype(F32), fb2.reshape(1, fh).astype(F32),
      freq.reshape(1, fh).astype(F32), fw3, fw3, deltas)


def _dft_constants():
    n1, n2 = FFT_N1, FFT_N2
    n = n1 * n2
    a1 = np.arange(n1)
    a2 = np.arange(n2)
    half = n2 // 2

    def cplx_block(w):
        return np.block([[w.real, -w.imag], [w.imag, w.real]])

    ang = (a2[None, :, None] * a2[None, None, :] / n2) + (a1[:, None, None] * a2[None, :, None] / n)
    w1 = np.exp(-2j * np.pi * ang)
    f1_pad = np.stack([cplx_block(w1[i][:, :half]) for i in range(n1)])
    f1_real = np.stack([np.concatenate([w1[i].real, w1[i].imag], axis=0) for i in range(n1)])
    w2 = np.exp(-2j * np.pi * (a1[:, None] * a1[None, :]) / n1)
    f2 = cplx_block(w2)
    ang = (a1[None, :, None] * a1[None, None, :] / n1) + (a2[:, None, None] * a1[None, :, None] / n)
    g2 = np.stack([cplx_block(m_) for m_ in np.exp(2j * np.pi * ang)])
    wg1 = np.exp(2j * np.pi * (a2[:half, None] * a2[None, :]) / n2) / n
    g1 = cplx_block(wg1)
    to = lambda x: jnp.asarray(x.astype(np.float32)).astype(BF16)
    return to(f1_pad), to(f1_real), to(f2), to(g2), to(g1)


def _fft_stage1_kernel(x_ref, f_ref, y_ref):
    xr = pltpu.einshape("mjc->jmc", x_ref[0])
    xi = pltpu.einshape("mjc->jmc", x_ref[1])
    for j in range(FFT_CHUNK):
        xc = jnp.concatenate([xr[j], xi[j]], axis=0).astype(BF16)
        y_ref[j] = _pack_bf16_pairs(jnp.dot(f_ref[j], xc, preferred_element_type=F32))


def _fft_stage1_real_kernel(x_ref, f_ref, y_ref):
    x = pltpu.einshape("mjc->jmc", x_ref[...])
    for j in range(FFT_CHUNK):
        y_ref[j] = _pack_bf16_pairs(jnp.dot(f_ref[j], _unpack_bf16_pairs(x[j]),
                                            preferred_element_type=F32))


def _fft_stage2_kernel(yr_ref, yi_ref, f2_ref, h_ref):
    yr = pltpu.einshape("nkc->knc", yr_ref[...])
    yi = pltpu.einshape("nkc->knc", yi_ref[...])
    for j in range(FFT_CHUNK):
        yc = _unpack_bf16_pairs(jnp.concatenate([yr[j], yi[j]], axis=0))
        h_ref[j] = _pack_bf16_pairs(jnp.dot(f2_ref[...], yc, preferred_element_type=F32))


def _fft_mid_kernel(yr_ref, yi_ref, f2_ref, h_ref, g2_ref, q_ref):
    yr = pltpu.einshape("nkc->knc", yr_ref[...])
    yi = pltpu.einshape("nkc->knc", yi_ref[...])
    n1 = FFT_N1
    spec = [jnp.dot(f2_ref[...], _unpack_bf16_pairs(jnp.concatenate([yr[j], yi[j]], axis=0)),
                    preferred_element_type=F32) for j in range(FFT_CHUNK)]
    prod = []
    for j, z in enumerate(spec):
        hf = _unpack_pairs_f32(h_ref[j])
        zr, zi, hr, hi = z[:n1], z[n1:], hf[:n1], hf[n1:]
        prod.append(jnp.concatenate([zr * hr - zi * hi, zr * hi + zi * hr], axis=0).astype(BF16))
    for j, pc in enumerate(prod):
        q_ref[j] = _pack_bf16_pairs(jnp.dot(g2_ref[j], pc, preferred_element_type=F32))


def _fft_last_kernel(qr_ref, qi_ref, g1_ref, gate_ref, zf_ref, fb_ref, o_ref):
    qr = pltpu.einshape("kjc->jkc", qr_ref[...])
    qi = pltpu.einshape("kjc->jkc", qi_ref[...])
    half = FFT_N2 // 2
    conv = []
    for j in range(FFT_CHUNK):
        qc = _unpack_bf16_pairs(jnp.concatenate([qr[j], qi[j]], axis=0))
        conv.append(jnp.dot(g1_ref[...], qc, preferred_element_type=F32))
    conv = jnp.stack(conv, axis=0)
    fb = fb_ref[...]
    for bi in range(2):
        cb = pltpu.einshape("jmc->mjc", conv[:, bi * half:(bi + 1) * half])
        zf = zf_ref[bi]
        o_ref[bi] = gate_ref[bi] * (cb + zf * fb)


def hyena_filter_spectrum(ts, consts, cb=FFT_CB):
    _, f1_real, f2, _, _ = consts
    no, n, c = ts.shape
    c = 2 * c
    n1, n2, ch = FFT_N1, FFT_N2, FFT_CHUNK
    ts4 = ts.reshape(no, n2, n1, c // 2)
    y = pl.pallas_call(
        _fft_stage1_real_kernel,
        out_shape=jax.ShapeDtypeStruct((no, n1, 2 * n2, c // 2), jnp.uint32),
        grid=(no, n1 // ch, c // cb),
        in_specs=[pl.BlockSpec((None, n2, ch, cb // 2), lambda o, j, k: (o, 0, j, k)),
                  pl.BlockSpec((ch, 2 * n2, n2), lambda o, j, k: (j, 0, 0))],
        out_specs=pl.BlockSpec((None, ch, 2 * n2, cb // 2), lambda o, j, k: (o, j, 0, k)),
        compiler_params=_params(("parallel", "parallel", "parallel"), big=True),
        name="fft_filter_stage1",
    )(ts4, f1_real)
    nk = n2 // ch
    return pl.pallas_call(
        _fft_stage2_kernel,
        out_shape=jax.ShapeDtypeStruct((no, n2, 2 * n1, c // 2), jnp.uint32),
        grid=(no, nk, c // cb),
        in_specs=[pl.BlockSpec((None, n1, ch, cb // 2), lambda o, j, k: (o, 0, j, k)),
                  pl.BlockSpec((None, n1, ch, cb // 2), lambda o, j, k: (o, 0, nk + j, k)),
                  pl.BlockSpec((2 * n1, 2 * n1), lambda o, j, k: (0, 0))],
        out_specs=pl.BlockSpec((None, ch, 2 * n1, cb // 2), lambda o, j, k: (o, j, 0, k)),
        compiler_params=_params(("parallel", "parallel", "parallel"), big=True),
        name="fft_filter_stage2",
    )(y, y, f2)


def hyena_long_conv_gate(zsrc, z_col, gate_src, gate_col, hf, order, fbias, consts, cb=FFT_CB):
    f1_pad, _, f2, g2, g1 = consts
    n1, n2, ch = FFT_N1, FFT_N2, FFT_CHUNK
    c = 2 * hf.shape[-1]
    ncb = c // cb
    half = n2 // 2
    y = pl.pallas_call(
        _fft_stage1_kernel,
        out_shape=jax.ShapeDtypeStruct((n1, 2 * n2, c // 2), jnp.uint32),
        grid=(n1 // ch, ncb),
        in_specs=[pl.BlockSpec((2, half, ch, cb), lambda j, k: (0, 0, j, z_col * ncb + k)),
                  pl.BlockSpec((ch, 2 * n2, n2), lambda j, k: (j, 0, 0))],
        out_specs=pl.BlockSpec((ch, 2 * n2, cb // 2), lambda j, k: (j, 0, k)),
        compiler_params=_params(("parallel", "parallel"), big=True),
        name="fft_stage1",
    )(zsrc, f1_pad)
    nk = n2 // ch
    q = pl.pallas_call(
        _fft_mid_kernel,
        out_shape=jax.ShapeDtypeStruct((n2, 2 * n1, c // 2), jnp.uint32),
        grid=(nk, ncb),
        in_specs=[pl.BlockSpec((n1, ch, cb // 2), lambda j, k: (0, j, k)),
                  pl.BlockSpec((n1, ch, cb // 2), lambda j, k: (0, nk + j, k)),
                  pl.BlockSpec((2 * n1, 2 * n1), lambda j, k: (0, 0)),
                  pl.BlockSpec((None, ch, 2 * n1, cb // 2), lambda j, k: (order, j, 0, k)),
                  pl.BlockSpec((ch, 2 * n1, 2 * n1), lambda j, k: (j, 0, 0))],
        out_specs=pl.BlockSpec((ch, 2 * n1, cb // 2), lambda j, k: (j, 0, k)),
        compiler_params=_params(("parallel", "parallel"), big=True),
        name="fft_mid",
    )(y, y, f2, hf, g2)
    nj = n1 // ch
    return pl.pallas_call(
        _fft_last_kernel,
        out_shape=jax.ShapeDtypeStruct((2, half, n1, c), F32),
        grid=(nj, ncb),
        in_specs=[pl.BlockSpec((n2, ch, cb // 2), lambda j, k: (0, j, k)),
                  pl.BlockSpec((n2, ch, cb // 2), lambda j, k: (0, nj + j, k)),
                  pl.BlockSpec((2 * half, 2 * n2), lambda j, k: (0, 0)),
                  pl.BlockSpec((2, half, ch, cb), lambda j, k: (0, 0, j, gate_col * ncb + k)),
                  pl.BlockSpec((2, half, ch, cb), lambda j, k: (0, 0, j, z_col * ncb + k)),
                  pl.BlockSpec((None, 1, cb), lambda j, k: (order, 0, k))],
        out_specs=pl.BlockSpec((2, half, ch, cb), lambda j, k: (0, 0, j, k)),
        compiler_params=_params(("parallel", "parallel"), big=True),
        name="fft_last",
    )(q, q, g1, gate_src, zsrc, fbias.reshape(fbias.shape[0], 1, c))


def hyena_mixer(hn, h_res, w_in, b_in, conv_w, conv_b, fw1, fb1, fw2, fb2, freq, fw3, fbias, w_out,
                batch, seq):
    width = w_out.shape[0]
    assert batch == 2 and 2 * seq == FFT_N1 * FFT_N2
    proj = matmul([hn], w_in, bias=b_in, name="hyena_in")
    sc = short_conv(proj.reshape(batch, seq, 3 * width), conv_w, conv_b)
    consts = _dft_constants()
    ts = hyena_two_sided_filters(seq, fw1, fb1, fw2, fb2, freq, fw3, width)
    hf = hyena_filter_spectrum(ts, consts)
    sc4 = sc.reshape(batch, FFT_N2 // 2, FFT_N1, 3 * width)
    zf1 = hyena_long_conv_gate(sc4, 2, sc4, 0, hf, 0, fbias, consts)
    zf2 = hyena_long_conv_gate(zf1, 0, sc4, 1, hf, 1, fbias, consts)
    return matmul([zf2.reshape(batch * seq, width)], w_out, res=h_res, tm=512, tn=1024,
                  name="hyena_out")


def moe_swiglu(hn2_packed, logits, wg, wu, wd):
    n = hn2_packed.shape[0]
    top_v, top_i = lax.top_k(logits, TOP_K)
    gates = jax.nn.softmax(top_v, axis=-1)
    e_flat = top_i.reshape(-1).astype(jnp.int32)
    nk = n * TOP_K
    onehot = (e_flat[:, None] == jnp.arange(N_EXPERTS, dtype=jnp.int32)[None]).astype(jnp.int32)
    csum = jnp.cumsum(onehot, axis=0)
    rank = jnp.take_along_axis(csum, e_flat[:, None], axis=1)[:, 0] - 1
    counts = csum[-1]
    padded = ((counts + MOE_TILE - 1) // MOE_TILE) * MOE_TILE
    pad_end = jnp.cumsum(padded)
    pad_start = pad_end - padded
    dest = pad_start[e_flat] + rank
    p_rows = nk + N_EXPERTS * MOE_TILE
    order = jnp.argsort(e_flat, stable=True).astype(jnp.int32)
    sorted_tok = jnp.pad(order // TOP_K, (0, DMA_LOOP_UNROLL))
    start = jnp.cumsum(counts) - counts
    nt = p_rows // MOE_TILE
    tile_start = jnp.arange(nt, dtype=jnp.int32) * MOE_TILE
    tile_used = tile_start < pad_end[-1]
    tile_exp = jnp.minimum(jnp.searchsorted(pad_end, tile_start, side='right'), N_EXPERTS - 1).astype(jnp.int32)
    tile_rows = jnp.clip(counts[tile_exp] - (tile_start - pad_start[tile_exp]), 0, MOE_TILE)
    tile_rows = jnp.where(tile_used, tile_rows, 0).astype(jnp.int32)
    last_exp = tile_exp[jnp.maximum(jnp.sum(tile_used.astype(jnp.int32)) - 1, 0)]
    tile_src = jnp.where(tile_used, start[tile_exp] + (tile_start - pad_start[tile_exp]), 0)
    tile_exp = jnp.where(tile_used, tile_exp, last_exp)
    ys = swiglu(hn2_packed, wg, wu, wd, tile_exp, tile_rows, tile_src.astype(jnp.int32), sorted_tok,
                sub_rows=FFN_SUB_ROWS, name="swiglu_experts")
    return ys, dest.reshape(n, TOP_K), gates


def kernel(x, p, ln_mix, ln_ffn, ln_ple, final_norm, t5_bias, w_attn_in, w_attn_out, attn_sink, na_rpb, w_ffn_gate, w_ffn_up, w_ffn_down, w_hy_in, b_hy_in, w_hy_conv, b_hy_conv, w_hy_f1, b_hy_f1, w_hy_f2, b_hy_f2, hy_freq, w_hy_f3, hy_bias, w_hy_out, w_router, w_exp_gate, w_exp_up, w_exp_down, w_ple_proj, w_ple_gate):
    batch, seq, d = x.shape
    n = batch * seq
    depth = ln_mix.shape[0]
    h = x.reshape(n, d)
    for i in range(depth):
        li = i // 2
        hn = rmsnorm(h, ln_mix[i], BF16)
        if i % 2 == 0:
            na_off = A_WIDTH + 2 * A_KV_WIDTH
            w_in = w_attn_in[li]
            proj_a = matmul([hn], w_in[:, :na_off], out_dtype=BF16, tn=640, name="attn_in_a")
            proj_n = matmul([hn], w_in[:, na_off:], out_dtype=BF16, name="attn_in_n")
            oa = window_attention(proj_a.reshape(batch, seq, -1), t5_bias, attn_sink[li])
            ob = neighbourhood_attention(proj_n.reshape(batch, seq, -1), na_rpb[li])
            h = matmul([oa.reshape(n, A_WIDTH), ob.reshape(n, B_WIDTH)], w_attn_out[li], res=h,
                       name="attn_out")
            hn2 = rmsnorm(h, ln_ffn[i], BF16)
            nt = n // MOE_TILE
            h = swiglu(hn2, w_ffn_gate, w_ffn_up, w_ffn_down,
                       jnp.full((nt,), li, jnp.int32), jnp.full((nt,), MOE_TILE, jnp.int32),
                       res=h, name="swiglu_dense")
            hn3 = rmsnorm(h, ln_ple[i], BF16)
        else:
            h = hyena_mixer(hn, h, w_hy_in[li], b_hy_in[li], w_hy_conv[li], b_hy_conv[li],
                            w_hy_f1[li], b_hy_f1[li], w_hy_f2[li], b_hy_f2[li], hy_freq[li],
                            w_hy_f3[li], hy_bias[li], w_hy_out[li], batch, seq)
            wr_pad = jnp.pad(w_router[li].astype(F32), ((0, 0), (0, 128 - N_EXPERTS)))
            hn2_packed, logits = rmsnorm_router(h, ln_ffn[i], wr_pad)
            ys, dest2, gates = moe_swiglu(hn2_packed, logits[:, :N_EXPERTS], w_exp_gate[li],
                                          w_exp_up[li], w_exp_down[li])
            h, hn3 = moe_combine_rmsnorm(h, ys, dest2, gates, ln_ple[i])
        h = ple(hn3, p[i].reshape(n, -1), h, w_ple_gate[i], w_ple_proj[i])
    return rmsnorm(h, final_norm, F32).reshape(batch, seq, d)
```

```python
import functools
import math

import jax
import jax.numpy as jnp
import numpy as np
from jax import lax
from jax.experimental import pallas as pl
from jax.experimental.pallas import tpu as pltpu

F32 = jnp.float32
BF16 = jnp.bfloat16
NEG_INF = -1e30
RMS_EPS = 1e-6

V7X_VMEM_LIMIT_BYTES = 56 * 1024 * 1024

HEAD_DIM = 64
A_Q_HEADS = 16
A_KV_HEADS = 2
A_GROUP = A_Q_HEADS // A_KV_HEADS
A_BLOCK = 128
T5_BUCKETS = 32
T5_MAX_DIST = 128
B_HEADS = 16
GRID_W = 64
NA_WIN_H = 8
NA_WIN_W = 16
A_WIDTH = A_Q_HEADS * HEAD_DIM
A_KV_WIDTH = A_KV_HEADS * HEAD_DIM
B_WIDTH = B_HEADS * HEAD_DIM
N_EXPERTS = 8
TOP_K = 2
HYENA_EMB = 33
HYENA_DECAY_TARGET = 1e-2
HYENA_FAST_PCT = 0.3
HYENA_SLOW_PCT = 1.5

FFT_N1 = 64
FFT_N2 = 128
FFT_CHUNK = 8
FFT_CB = 1024

MOE_TILE = 1024
FFN_F_TILE = 256
MOE_F_TILE = 512
FFN_SUB_ROWS = 512
DMA_LOOP_UNROLL = 8


def _params(semantics, big=False):
    return pltpu.CompilerParams(
        dimension_semantics=semantics,
        vmem_limit_bytes=V7X_VMEM_LIMIT_BYTES if big else None)


def _rmsnorm_kernel(x_ref, g_ref, o_ref):
    x = x_ref[...]
    y = x * lax.rsqrt(jnp.mean(x * x, axis=-1, keepdims=True) + RMS_EPS)
    o_ref[...] = (y * g_ref[...]).astype(o_ref.dtype)


def rmsnorm(x2, g, out_dtype, tm=512):
    n, d = x2.shape
    return pl.pallas_call(
        _rmsnorm_kernel,
        out_shape=jax.ShapeDtypeStruct((n, d), out_dtype),
        grid=(n // tm,),
        in_specs=[pl.BlockSpec((tm, d), lambda i: (i, 0)),
                  pl.BlockSpec((1, d), lambda i: (0, 0))],
        out_specs=pl.BlockSpec((tm, d), lambda i: (i, 0)),
        compiler_params=_params(("parallel",)),
        name="rmsnorm",
    )(x2, g.reshape(1, d))


def _pack_bf16_pairs(y):
    w = y.shape[1] // 2
    bits = pltpu.bitcast(y.astype(BF16).astype(F32), jnp.uint32)
    return (bits[:, :w] >> 16) | (bits[:, w:] & jnp.uint32(0xFFFF0000))


def _unpack_pairs_f32(p):
    lo = pltpu.bitcast(p << 16, F32)
    hi = pltpu.bitcast(p & jnp.uint32(0xFFFF0000), F32)
    return jnp.concatenate([lo, hi], axis=1)


def _unpack_bf16_pairs(p):
    return _unpack_pairs_f32(p).astype(BF16)


def _rmsnorm_router_kernel(x_ref, g_ref, wr_ref, o_ref, l_ref):
    x = x_ref[...]
    y = x * lax.rsqrt(jnp.mean(x * x, axis=-1, keepdims=True) + RMS_EPS)
    y = y * g_ref[...]
    o_ref[...] = _pack_bf16_pairs(y)
    l_ref[...] = jnp.dot(y, wr_ref[...], preferred_element_type=F32,
                         precision=lax.Precision.HIGHEST)


def rmsnorm_router(x2, g, w_router_pad, tm=512):
    n, d = x2.shape
    ne = w_router_pad.shape[1]
    return pl.pallas_call(
        _rmsnorm_router_kernel,
        out_shape=(jax.ShapeDtypeStruct((n, d // 2), jnp.uint32), jax.ShapeDtypeStruct((n, ne), F32)),
        grid=(n // tm,),
        in_specs=[pl.BlockSpec((tm, d), lambda i: (i, 0)),
                  pl.BlockSpec((1, d), lambda i: (0, 0)),
                  pl.BlockSpec((d, ne), lambda i: (0, 0))],
        out_specs=(pl.BlockSpec((tm, d // 2), lambda i: (i, 0)),
                   pl.BlockSpec((tm, ne), lambda i: (i, 0))),
        compiler_params=_params(("parallel",)),
        name="rmsnorm_router",
    )(x2, g.reshape(1, d), w_router_pad)


def _mm_kernel(*refs, n_a, has_bias, has_res):
    a_refs = refs[:n_a]
    w_refs = refs[n_a:2 * n_a]
    idx = 2 * n_a
    bias_ref = refs[idx] if has_bias else None
    idx += int(has_bias)
    res_ref = refs[idx] if has_res else None
    idx += int(has_res)
    o_ref = refs[idx]
    wbf_refs = refs[idx + 1:idx + 1 + n_a]

    @pl.when(pl.program_id(1) == 0)
    def _():
        for w_ref, wbf_ref in zip(w_refs, wbf_refs):
            wbf_ref[...] = w_ref[...].astype(BF16)

    acc = None
    for a_ref, wbf_ref in zip(a_refs, wbf_refs):
        d = jnp.dot(a_ref[...].astype(BF16), wbf_ref[...], preferred_element_type=F32)
        acc = d if acc is None else acc + d
    if has_bias:
        acc = acc + bias_ref[...]
    if has_res:
        acc = acc + res_ref[...]
    o_ref[...] = acc.astype(o_ref.dtype)


def matmul(a_list, w, *, bias=None, res=None, out_dtype=F32, tm=1024, tn=1024, name="matmul"):
    m = a_list[0].shape[0]
    n = w.shape[1]
    n_a = len(a_list)
    k_each = a_list[0].shape[1]
    assert all(a.shape == (m, k_each) for a in a_list) and w.shape[0] == n_a * k_each
    in_specs = [pl.BlockSpec((tm, k_each), lambda j, i: (i, 0)) for _ in a_list]
    in_specs += [pl.BlockSpec((k_each, tn), functools.partial(lambda j, i, kb: (kb, j), kb=kb))
                 for kb in range(n_a)]
    args = list(a_list) + [w] * n_a
    if bias is not None:
        in_specs.append(pl.BlockSpec((1, tn), lambda j, i: (0, j)))
        args.append(bias.reshape(1, n))
    if res is not None:
        in_specs.append(pl.BlockSpec((tm, tn), lambda j, i: (i, j)))
        args.append(res)
    return pl.pallas_call(
        functools.partial(_mm_kernel, n_a=n_a, has_bias=bias is not None, has_res=res is not None),
        out_shape=jax.ShapeDtypeStruct((m, n), out_dtype),
        grid=(n // tn, m // tm),
        in_specs=in_specs,
        out_specs=pl.BlockSpec((tm, tn), lambda j, i: (i, j)),
        scratch_shapes=[pltpu.VMEM((k_each, tn), BF16) for _ in a_list],
        compiler_params=_params(("parallel", "arbitrary"), big=True),
        name=name,
    )(*args)


def _ple_kernel(hn_ref, p_ref, h_ref, wg_ref, wp_ref, o_ref, wg_bf, wp_bf):
    @pl.when(pl.program_id(1) == 0)
    def _():
        wg_bf[...] = wg_ref[...].astype(BF16)
        wp_bf[...] = wp_ref[...].astype(BF16)

    a = jnp.dot(hn_ref[...], wg_bf[...], preferred_element_type=F32)
    pp = jnp.dot(p_ref[...].astype(BF16), wp_bf[...], preferred_element_type=F32)
    o_ref[...] = h_ref[...] + jax.nn.sigmoid(a) * pp


def ple(hn, p2, h2, w_gate, w_proj, tm=1024, tn=1024):
    m, d = h2.shape
    pd = p2.shape[1]
    return pl.pallas_call(
        _ple_kernel,
        out_shape=jax.ShapeDtypeStruct((m, d), F32),
        grid=(d // tn, m // tm),
        in_specs=[pl.BlockSpec((tm, d), lambda j, i: (i, 0)),
                  pl.BlockSpec((tm, pd), lambda j, i: (i, 0)),
                  pl.BlockSpec((tm, tn), lambda j, i: (i, j)),
                  pl.BlockSpec((d, tn), lambda j, i: (0, j)),
                  pl.BlockSpec((pd, tn), lambda j, i: (0, j))],
        out_specs=pl.BlockSpec((tm, tn), lambda j, i: (i, j)),
        scratch_shapes=[pltpu.VMEM((d, tn), BF16), pltpu.VMEM((pd, tn), BF16)],
        compiler_params=_params(("parallel", "arbitrary"), big=True),
        name="ple",
    )(hn, p2, h2, w_gate, w_proj)


def _row_copy(src_hbm, row, dst, i, sem):
    return pltpu.make_async_copy(src_hbm.at[pl.ds(row, 1), :], dst.at[pl.ds(i, 1), :], sem)


def _swiglu_kernel(exp_ref, rows_ref, src_ref, tok_ref, x_ref, wg_ref, wu_ref, wd_ref, *rest,
                   n_col, gather, sub_rows, has_res):
    del exp_ref
    res_ref = rest[0] if has_res else None
    o_ref = rest[int(has_res)]
    scratch = rest[int(has_res) + 1:]
    t = pl.program_id(0)
    f = pl.program_id(1)
    nt = pl.num_programs(0)
    n_rows = rows_ref[t]
    used = n_rows > 0
    tm = o_ref.shape[0]

    @pl.when(f == 0)
    def _():
        o_ref[...] = res_ref[...] if has_res else jnp.zeros_like(o_ref)

    if gather:
        xbuf, xbf, sems = scratch
        slot = t % 2

        def for_each_row(tile, fn):
            def body(c, carry):
                for r in range(DMA_LOOP_UNROLL):
                    fn(c * DMA_LOOP_UNROLL + r)
                return carry
            lax.fori_loop(0, (rows_ref[tile] + DMA_LOOP_UNROLL - 1) // DMA_LOOP_UNROLL, body, 0)

        def start_tile(tile, sl):
            for_each_row(tile, lambda i: _row_copy(x_ref, tok_ref[src_ref[tile] + i], xbuf.at[sl], i,
                                                   sems.at[sl, i]).start())

        @pl.when(jnp.logical_and(f == 0, t == 0))
        def _():
            xbuf[...] = jnp.zeros_like(xbuf)
            start_tile(0, 0)

        @pl.when(jnp.logical_and(f == 0, used))
        def _():
            for_each_row(t, lambda i: _row_copy(x_ref, 0, xbuf.at[slot], i, sems.at[slot, i]).wait())
            xbf[...] = _unpack_bf16_pairs(xbuf[slot])

        @pl.when(jnp.logical_and(f == 1, t + 1 < nt))
        def _():
            start_tile(jnp.minimum(t + 1, nt - 1), 1 - slot)

        x_src = xbf
    else:
        x_src = x_ref

    for sb in range(tm // sub_rows):
        rs = slice(sb * sub_rows, (sb + 1) * sub_rows)

        @pl.when(sb * sub_rows < n_rows)
        def _():
            x = x_src[rs, :]
            g = jnp.dot(x, wg_ref[...].astype(BF16), preferred_element_type=F32)
            u = jnp.dot(x, wu_ref[...].astype(BF16), preferred_element_type=F32)
            hmid = (g * jax.nn.sigmoid(g) * u).astype(BF16)
            wd = wd_ref[...].astype(BF16)
            cw = o_ref.shape[1] // n_col
            for c in range(n_col):
                y = jnp.dot(hmid, wd[:, c * cw:(c + 1) * cw], preferred_element_type=F32)
                o_ref[rs, c * cw:(c + 1) * cw] += y


def swiglu(x, w_gate, w_up, w_down, tile_exp, tile_rows, tile_src=None, src_rows=None, *, res=None,
           tm=MOE_TILE, tf=FFN_F_TILE, sub_rows=MOE_TILE, name="swiglu"):
    gather = src_rows is not None
    has_res = res is not None
    m = tile_rows.shape[0] * tm
    d = w_gate.shape[1]
    dff = w_gate.shape[2]
    nf = dff // tf
    nt = m // tm
    assert nf >= 2

    def w_in_map(t, f, exp_ref, rows_ref, src_ref, tok_ref):
        return (exp_ref[t], 0, jnp.where(rows_ref[t] > 0, f, nf - 1))

    def w_out_map(t, f, exp_ref, rows_ref, src_ref, tok_ref):
        return (exp_ref[t], jnp.where(rows_ref[t] > 0, f, nf - 1), 0)

    def tile_map(t, f, exp_ref, rows_ref, src_ref, tok_ref):
        return (t, 0)

    if gather:
        x_spec = pl.BlockSpec(memory_space=pl.ANY)
        scratch = [pltpu.VMEM((2, tm, d // 2), jnp.uint32), pltpu.VMEM((tm, d), BF16),
                   pltpu.SemaphoreType.DMA((2, tm))]
    else:
        x_spec = pl.BlockSpec((tm, d), tile_map)
        scratch = []
        tile_src = jnp.zeros((nt,), jnp.int32)
        src_rows = jnp.zeros((1,), jnp.int32)
    in_specs = [x_spec,
                pl.BlockSpec((None, d, tf), w_in_map),
                pl.BlockSpec((None, d, tf), w_in_map),
                pl.BlockSpec((None, tf, d), w_out_map)]
    args = [x, w_gate, w_up, w_down]
    if has_res:
        in_specs.append(pl.BlockSpec((tm, d), tile_map, pipeline_mode=pl.Buffered(1)))
        args.append(res)
    grid_spec = pltpu.PrefetchScalarGridSpec(
        num_scalar_prefetch=4,
        grid=(nt, nf),
        in_specs=in_specs,
        out_specs=pl.BlockSpec((tm, d), tile_map, pipeline_mode=pl.Buffered(1) if gather else None),
        scratch_shapes=scratch,
    )
    return pl.pallas_call(
        functools.partial(_swiglu_kernel, n_col=4, gather=gather, sub_rows=sub_rows, has_res=has_res),
        out_shape=jax.ShapeDtypeStruct((m, d), F32),
        grid_spec=grid_spec,
        compiler_params=_params(("arbitrary", "arbitrary"), big=True),
        name=name,
    )(tile_exp, tile_rows, tile_src, src_rows, *args)


COMBINE_TILE = 256


def _combine_kernel(d0_ref, d1_ref, ys_hbm, gate_ref, h_ref, g_ref, hout_ref, hn_ref, buf, sems):
    i = pl.program_id(0)
    n = pl.num_programs(0)
    tt = h_ref.shape[0]
    slot = i % 2
    dests = (d0_ref, d1_ref)

    def start_tile(tile, sl):
        def body(r, c):
            for k, d_ref in enumerate(dests):
                _row_copy(ys_hbm, d_ref[tile * tt + r], buf.at[sl, k], r, sems.at[sl, k, r]).start()
            return c
        lax.fori_loop(0, tt, body, 0, unroll=DMA_LOOP_UNROLL)

    @pl.when(i == 0)
    def _():
        start_tile(0, 0)

    @pl.when(i + 1 < n)
    def _():
        start_tile(i + 1, 1 - slot)

    def wait_body(r, c):
        for k in range(TOP_K):
            _row_copy(ys_hbm, 0, buf.at[slot, k], r, sems.at[slot, k, r]).wait()
        return c
    lax.fori_loop(0, tt, wait_body, 0, unroll=DMA_LOOP_UNROLL)

    gates = gate_ref[...]
    x = h_ref[...] + (buf[slot, 0] * gates[:, 0:1] + buf[slot, 1] * gates[:, 1:2])
    hout_ref[...] = x
    y = x * lax.rsqrt(jnp.mean(x * x, axis=-1, keepdims=True) + RMS_EPS)
    hn_ref[...] = (y * g_ref[...]).astype(hn_ref.dtype)


def moe_combine_rmsnorm(h2, ys, dest2, gates, g, tt=COMBINE_TILE):
    n, d = h2.shape
    grid_spec = pltpu.PrefetchScalarGridSpec(
        num_scalar_prefetch=2,
        grid=(n // tt,),
        in_specs=[pl.BlockSpec(memory_space=pl.ANY),
                  pl.BlockSpec((tt, TOP_K), lambda i, a, b: (i, 0)),
                  pl.BlockSpec((tt, d), lambda i, a, b: (i, 0)),
                  pl.BlockSpec((1, d), lambda i, a, b: (0, 0))],
        out_specs=(pl.BlockSpec((tt, d), lambda i, a, b: (i, 0)),
                   pl.BlockSpec((tt, d), lambda i, a, b: (i, 0))),
        scratch_shapes=[pltpu.VMEM((2, TOP_K, tt, d), F32), pltpu.SemaphoreType.DMA((2, TOP_K, tt))],
    )
    return pl.pallas_call(
        _combine_kernel,
        out_shape=(jax.ShapeDtypeStruct((n, d), F32), jax.ShapeDtypeStruct((n, d), BF16)),
        grid_spec=grid_spec,
        compiler_params=_params(("arbitrary",), big=True),
        name="moe_combine",
    )(dest2[:, 0], dest2[:, 1], ys, gates, h2, g.reshape(1, d))


def _t5_bucket(rel):
    half = T5_BUCKETS // 2
    max_exact = half // 2
    n = jnp.abs(rel)
    log_ratio = jnp.log(jnp.maximum(n, 1).astype(F32) / max_exact) / math.log(T5_MAX_DIST / max_exact)
    large = jnp.minimum(max_exact + (log_ratio * (half - max_exact)).astype(jnp.int32), half - 1)
    return jnp.where(rel > 0, half, 0) + jnp.where(n < max_exact, n, large)


def _window_bias_table(t5_bias):
    i = jnp.arange(A_BLOCK)[:, None]
    j = jnp.arange(3 * A_BLOCK)[None, :]
    rel = j - A_BLOCK - i
    onehot = (_t5_bucket(rel)[None] == jnp.arange(T5_BUCKETS)[:, None, None]).astype(F32)
    bias = jnp.einsum('bh,bij->hij', t5_bias.astype(F32), onehot, precision=lax.Precision.HIGHEST)
    return jnp.where((jnp.abs(rel) <= A_BLOCK)[None], bias, NEG_INF)


def _window_kernel(sink_ref, q_ref, kv_ref, bias_ref, o_ref, *, nb):
    n = pl.program_id(1)
    scale = HEAD_DIM ** -0.5
    starts = (jnp.maximum(n - 1, 0), n, jnp.minimum(n + 1, nb - 1))
    kv = [kv_ref[pl.ds(pl.multiple_of(s * A_BLOCK, A_BLOCK), A_BLOCK), :] for s in starts]
    col = lax.broadcasted_iota(jnp.int32, (A_BLOCK, 3 * A_BLOCK), 1)
    edge_ok = jnp.logical_and(jnp.logical_or(n > 0, col >= A_BLOCK),
                              jnp.logical_or(n < nb - 1, col < 2 * A_BLOCK))
    kv = jnp.concatenate(kv, axis=0)
    q = q_ref[...]
    lo = lax.broadcasted_iota(jnp.int32, (1, 2 * HEAD_DIM), 1) < HEAD_DIM
    sel = (jnp.where(lo, scale, 0.0).astype(BF16), jnp.where(lo, 0.0, scale).astype(BF16))
    pairs_per_kv = A_GROUP // 2
    kk, vv = [], []
    for kh in range(A_KV_HEADS):
        k1 = kv[:, kh * HEAD_DIM:(kh + 1) * HEAD_DIM]
        v1 = kv[:, A_KV_WIDTH + kh * HEAD_DIM:A_KV_WIDTH + (kh + 1) * HEAD_DIM]
        kk.append(jnp.concatenate([k1, k1], axis=1))
        vv.append(jnp.concatenate([v1, v1], axis=1))
    scores = []
    for pr in range(A_Q_HEADS // 2):
        q2 = q[:, pr * 2 * HEAD_DIM:(pr + 1) * 2 * HEAD_DIM]
        qq = jnp.concatenate([q2 * sel[0], q2 * sel[1]], axis=0)
        s2 = lax.dot_general(qq, kk[pr // pairs_per_kv], (((1,), (1,)), ((), ())),
                             preferred_element_type=F32)
        for i in range(2):
            scores.append(jnp.where(edge_ok, s2[i * A_BLOCK:(i + 1) * A_BLOCK] + bias_ref[2 * pr + i],
                                    NEG_INF))
    probs, dens = [], []
    for h, s in enumerate(scores):
        sink = sink_ref[h]
        mx = jnp.maximum(jnp.max(s, axis=-1, keepdims=True), sink)
        p = jnp.exp(s - mx)
        dens.append(jnp.sum(p, axis=-1, keepdims=True) + jnp.exp(sink - mx))
        probs.append(p.astype(BF16))
    for pr in range(A_Q_HEADS // 2):
        v2 = vv[pr // pairs_per_kv]
        oa = jnp.dot(probs[2 * pr], v2, preferred_element_type=F32) / dens[2 * pr]
        ob = jnp.dot(probs[2 * pr + 1], v2, preferred_element_type=F32) / dens[2 * pr + 1]
        o_ref[:, pr * 2 * HEAD_DIM:(pr + 1) * 2 * HEAD_DIM] = jnp.where(lo, oa, ob).astype(o_ref.dtype)


def window_attention(proj_a, t5_bias, sink):
    b, s, _ = proj_a.shape
    nb = s // A_BLOCK
    kvw = 2 * A_KV_WIDTH
    grid_spec = pltpu.PrefetchScalarGridSpec(
        num_scalar_prefetch=0,
        grid=(b, nb),
        in_specs=[pl.BlockSpec(memory_space=pltpu.SMEM),
                  pl.BlockSpec((None, A_BLOCK, A_WIDTH), lambda bi, n: (bi, n, 0)),
                  pl.BlockSpec((None, s, kvw), lambda bi, n: (bi, 0, A_WIDTH // kvw)),
                  pl.BlockSpec((A_Q_HEADS, A_BLOCK, 3 * A_BLOCK), lambda bi, n: (0, 0, 0))],
        out_specs=pl.BlockSpec((None, A_BLOCK, A_WIDTH), lambda bi, n: (bi, n, 0)),
    )
    return pl.pallas_call(
        functools.partial(_window_kernel, nb=nb),
        out_shape=jax.ShapeDtypeStruct((b, s, A_WIDTH), BF16),
        grid_spec=grid_spec,
        compiler_params=_params(("parallel", "arbitrary")),
        name="window_attention",
    )(sink.astype(F32), proj_a, proj_a, _window_bias_table(t5_bias))


NA_HEAD_GROUP = 16


def _na_bias_table(rpb, rows):
    kh = min(NA_WIN_H, rows)
    kw = NA_WIN_W
    c = jnp.arange(GRID_W)
    cs = jnp.clip(c - kw // 2, 0, GRID_W - kw)
    col_ok = (c[None] >= cs[:, None]) & (c[None] < cs[:, None] + kw)
    col_off = jnp.clip(c[None] - c[:, None], -(kw - 1), kw - 1) + kw - 1
    onehot = (col_off[None] == jnp.arange(2 * kw - 1)[:, None, None]).astype(F32)
    by_col = jnp.einsum('hrc,cqk->hrqk', rpb.astype(F32), onehot, precision=lax.Precision.HIGHEST)
    by_col = jnp.where(col_ok[None, None], by_col, NEG_INF)
    tabs = [by_col[:, NA_WIN_H - 1 - d:NA_WIN_H - 1 - d + kh] for d in range(kh)]
    bias = jnp.stack(tabs, axis=0).transpose(0, 1, 3, 2, 4)
    return bias.reshape(kh, rpb.shape[0], GRID_W, kh * GRID_W)


def _na_kernel(q_ref, k_ref, v_ref, bias_ref, o_ref, *, rows):
    r = pl.program_id(2)
    kh = min(NA_WIN_H, rows)
    scale = HEAD_DIM ** -0.5
    start = pl.multiple_of(jnp.clip(r - kh // 2, 0, rows - kh) * GRID_W, GRID_W)
    k = k_ref[pl.ds(start, kh * GRID_W), :]
    v = v_ref[pl.ds(start, kh * GRID_W), :]
    q = q_ref[...]
    lo = lax.broadcasted_iota(jnp.int32, (1, 2 * HEAD_DIM), 1) < HEAD_DIM
    sel = (jnp.where(lo, scale, 0.0).astype(BF16), jnp.where(lo, 0.0, scale).astype(BF16))
    n_pairs = NA_HEAD_GROUP // 2
    scores = []
    for pr in range(n_pairs):
        sl = slice(pr * 2 * HEAD_DIM, (pr + 1) * 2 * HEAD_DIM)
        q2 = q[:, sl]
        qq = jnp.concatenate([q2 * sel[0], q2 * sel[1]], axis=0)
        s2 = lax.dot_general(qq, k[:, sl], (((1,), (1,)), ((), ())), preferred_element_type=F32)
        for i in range(2):
            scores.append(s2[i * GRID_W:(i + 1) * GRID_W] + bias_ref[2 * pr + i])
    probs, dens = [], []
    for s in scores:
        mx = jnp.max(s, axis=-1, keepdims=True)
        p = jnp.exp(s - mx)
        dens.append(jnp.sum(p, axis=-1, keepdims=True))
        probs.append(p.astype(BF16))
    for pr in range(n_pairs):
        sl = slice(pr * 2 * HEAD_DIM, (pr + 1) * 2 * HEAD_DIM)
        oa = jnp.dot(probs[2 * pr], v[:, sl], preferred_element_type=F32) / dens[2 * pr]
        ob = jnp.dot(probs[2 * pr + 1], v[:, sl], preferred_element_type=F32) / dens[2 * pr + 1]
        o_ref[:, sl] = jnp.where(lo, oa, ob).astype(o_ref.dtype)


def neighbourhood_attention(proj_n, rpb):
    b, s, _ = proj_n.shape
    rows = s // GRID_W
    kh = min(NA_WIN_H, rows)
    gw = NA_HEAD_GROUP * HEAD_DIM
    ng = B_WIDTH // gw

    def bias_map(bi, g, r):
        return (r - jnp.clip(r - kh // 2, 0, rows - kh), g, 0, 0)

    return pl.pallas_call(
        functools.partial(_na_kernel, rows=rows),
        out_shape=jax.ShapeDtypeStruct((b, s, B_WIDTH), BF16),
        grid=(b, ng, rows),
        in_specs=[pl.BlockSpec((None, GRID_W, gw), lambda bi, g, r: (bi, r, g)),
                  pl.BlockSpec((None, s, gw), lambda bi, g, r: (bi, 0, ng + g)),
                  pl.BlockSpec((None, s, gw), lambda bi, g, r: (bi, 0, 2 * ng + g)),
                  pl.BlockSpec((None, NA_HEAD_GROUP, GRID_W, kh * GRID_W), bias_map)],
        out_specs=pl.BlockSpec((None, GRID_W, gw), lambda bi, g, r: (bi, r, g)),
        compiler_params=_params(("parallel", "parallel", "arbitrary"), big=True),
        name="neighbourhood_attention",
    )(proj_n, proj_n, proj_n, _na_bias_table(rpb, rows))


def _short_conv_kernel(prev_ref, cur_ref, next_ref, w_ref, b_ref, o_ref):
    i = pl.program_id(1)
    nt = pl.num_programs(1)
    x = cur_ref[...]
    tl = x.shape[0]
    row = lax.broadcasted_iota(jnp.int32, x.shape, 0)
    prev_row = jnp.where(i > 0, prev_ref[7:8, :], 0.0)
    next_row = jnp.where(i < nt - 1, next_ref[0:1, :], 0.0)
    up = jnp.where(row == 0, prev_row, pltpu.roll(x, 1, 0))
    dn = jnp.where(row == tl - 1, next_row, pltpu.roll(x, tl - 1, 0))
    o_ref[...] = up * w_ref[0:1, :] + x * w_ref[1:2, :] + dn * w_ref[2:3, :] + b_ref[...]


def short_conv(proj, conv_w, conv_b, tl=512, tc=1024):
    b, l, c = proj.shape
    nl8 = l // 8
    return pl.pallas_call(
        _short_conv_kernel,
        out_shape=jax.ShapeDtypeStruct((b, l, c), F32),
        grid=(b, l // tl, c // tc),
        in_specs=[pl.BlockSpec((None, 8, tc), lambda bi, i, j: (bi, jnp.maximum(i * (tl // 8) - 1, 0), j)),
                  pl.BlockSpec((None, tl, tc), lambda bi, i, j: (bi, i, j)),
                  pl.BlockSpec((None, 8, tc), lambda bi, i, j: (bi, jnp.minimum((i + 1) * (tl // 8), nl8 - 1), j)),
                  pl.BlockSpec((3, tc), lambda bi, i, j: (0, j)),
                  pl.BlockSpec((1, tc), lambda bi, i, j: (0, j))],
        out_specs=pl.BlockSpec((None, tl, tc), lambda bi, i, j: (bi, i, j)),
        compiler_params=_params(("parallel", "parallel", "parallel")),
        name="short_conv",
    )(proj, proj, proj, conv_w, conv_b.reshape(1, c))


def _filter_kernel(z_ref, t_ref, fw1_ref, fb1_ref, fw2_ref, fb2_ref, fr_ref, w3_ref, w3b_ref,
                   delta_ref, o_ref, *, seq):
    m = pl.program_id(0)
    hi = lax.Precision.HIGHEST
    fr = fr_ref[...]
    hid = jnp.sin(fr * (jnp.dot(z_ref[...], fw1_ref[...], preferred_element_type=F32, precision=hi)
                        + fb1_ref[...]))
    hid = jnp.sin(fr * (jnp.dot(hid, fw2_ref[...], preferred_element_type=F32, precision=hi)
                        + fb2_ref[...]))
    hid_bf = hid.astype(BF16)
    h = jnp.dot(hid_bf, w3_ref[...].astype(BF16), preferred_element_type=F32)
    tm = h.shape[0]
    width = delta_ref.shape[1]
    row = lax.broadcasted_iota(jnp.int32, (tm, width), 0) + m * tm
    decay = jnp.where(row == seq, 0.0, jnp.exp(-t_ref[...] * delta_ref[...]))
    def emit(taps):
        for o in range(o_ref.shape[0]):
            for kb in range(width // FFT_CB):
                cols = slice(kb * FFT_CB, (kb + 1) * FFT_CB)
                blk = taps[:, o * width:(o + 1) * width][:, cols] * decay[:, cols]
                o_ref[o, :, kb * (FFT_CB // 2):(kb + 1) * (FFT_CB // 2)] = _pack_bf16_pairs(blk)

    @pl.when(m != 0)
    def _():
        emit(h)

    @pl.when(m == 0)
    def _():
        h_back = jnp.dot(hid_bf, w3b_ref[...].astype(BF16), preferred_element_type=F32)
        first = lax.broadcasted_iota(jnp.int32, h.shape, 0) == 0
        emit(h + jnp.where(first, h_back, 0.0))


def hyena_two_sided_filters(seq, fw1, fb1, fw2, fb2, freq, fw3, width, tm=512):
    n = 2 * seq
    u = jnp.arange(n)
    pos = jnp.where(u <= seq, u, n - u)
    pos = jnp.where(u == seq, 0, pos)
    bands = (HYENA_EMB - 1) // 2
    f = jnp.linspace(1e-4, bands - 1, bands, dtype=F32)[None]
    t = (pos.astype(F32) / (seq - 1))[:, None]
    w = ((2.0 * math.pi / seq) * pos.astype(F32))[:, None]
    z = jnp.concatenate([t, jnp.cos(f * w), -jnp.sin(f * w)], axis=-1)
    emb_pad = 40
    z = jnp.pad(z, ((0, 0), (0, emb_pad - HYENA_EMB)))
    fw1p = jnp.pad(fw1.astype(F32), ((0, emb_pad - HYENA_EMB), (0, 0)))
    fh = fw1.shape[1]
    max_decay = math.log(HYENA_DECAY_TARGET) / HYENA_FAST_PCT
    min_decay = math.log(HYENA_DECAY_TARGET) / HYENA_SLOW_PCT
    deltas = jnp.abs(jnp.linspace(min_decay, max_decay, width, dtype=F32))[None]
    half_tiles = seq // tm
    orders = fw3.shape[1] // (2 * width)
    ow = orders * width

    return pl.pallas_call(
        functools.partial(_filter_kernel, seq=seq),
        out_shape=jax.ShapeDtypeStruct((orders, n, width // 2), jnp.uint32),
        grid=(n // tm,),
        in_specs=[pl.BlockSpec((tm, emb_pad), lambda m: (m, 0)),
                  pl.BlockSpec((tm, 1), lambda m: (m, 0)),
                  pl.BlockSpec((emb_pad, fh), lambda m: (0, 0)),
                  pl.BlockSpec((1, fh), lambda m: (0, 0)),
                  pl.BlockSpec((fh, fh), lambda m: (0, 0)),
                  pl.BlockSpec((1, fh), lambda m: (0, 0)),
                  pl.BlockSpec((1, fh), lambda m: (0, 0)),
                  pl.BlockSpec((fh, ow), lambda m: (0, jnp.where(m >= half_tiles, 1, 0))),
                  pl.BlockSpec((fh, ow), lambda m: (0, 1)),
                  pl.BlockSpec((1, width), lambda m: (0, 0))],
        out_specs=pl.BlockSpec((orders, tm, width // 2), lambda m: (0, m, 0)),
        compiler_params=_params(("arbitrary",), big=True),
        name="hyena_filters",
    )(z, t, fw1p, fb1.reshape(1, fh).astype(F32), fw2.astype(F32), fb2.reshape(1, fh).astype(F32),
      freq.reshape(1, fh).astype(F32), fw3, fw3, deltas)


def _dft_constants():
    n1, n2 = FFT_N1, FFT_N2
    n = n1 * n2
    a1 = np.arange(n1)
    a2 = np.arange(n2)
    half = n2 // 2

    def cplx_block(w):
        return np.block([[w.real, -w.imag], [w.imag, w.real]])

    ang = (a2[None, :, None] * a2[None, None, :] / n2) + (a1[:, None, None] * a2[None, :, None] / n)
    w1 = np.exp(-2j * np.pi * ang)
    f1_pad = np.stack([cplx_block(w1[i][:, :half]) for i in range(n1)])
    f1_real = np.stack([np.concatenate([w1[i].real, w1[i].imag], axis=0) for i in range(n1)])
    w2 = np.exp(-2j * np.pi * (a1[:, None] * a1[None, :]) / n1)
    f2 = cplx_block(w2)
    ang = (a1[None, :, None] * a1[None, None, :] / n1) + (a2[:, None, None] * a1[None, :, None] / n)
    g2 = np.stack([cplx_block(m_) for m_ in np.exp(2j * np.pi * ang)])
    wg1 = np.exp(2j * np.pi * (a2[:half, None] * a2[None, :]) / n2) / n
    g1 = cplx_block(wg1)
    to = lambda x: jnp.asarray(x.astype(np.float32)).astype(BF16)
    return to(f1_pad), to(f1_real), to(f2), to(g2), to(g1)


def _fft_stage1_kernel(x_ref, f_ref, y_ref):
    xr = pltpu.einshape("mjc->jmc", x_ref[0])
    xi = pltpu.einshape("mjc->jmc", x_ref[1])
    for j in range(FFT_CHUNK):
        xc = jnp.concatenate([xr[j], xi[j]], axis=0).astype(BF16)
        y_ref[j] = _pack_bf16_pairs(jnp.dot(f_ref[j], xc, preferred_element_type=F32))


def _fft_stage1_real_kernel(x_ref, f_ref, y_ref):
    x = pltpu.einshape("mjc->jmc", x_ref[...])
    for j in range(FFT_CHUNK):
        y_ref[j] = _pack_bf16_pairs(jnp.dot(f_ref[j], _unpack_bf16_pairs(x[j]),
                                            preferred_element_type=F32))


def _fft_stage2_kernel(yr_ref, yi_ref, f2_ref, h_ref):
    yr = pltpu.einshape("nkc->knc", yr_ref[...])
    yi = pltpu.einshape("nkc->knc", yi_ref[...])
    for j in range(FFT_CHUNK):
        yc = _unpack_bf16_pairs(jnp.concatenate([yr[j], yi[j]], axis=0))
        h_ref[j] = _pack_bf16_pairs(jnp.dot(f2_ref[...], yc, preferred_element_type=F32))


def _fft_mid_kernel(yr_ref, yi_ref, f2_ref, h_ref, g2_ref, q_ref):
    yr = pltpu.einshape("nkc->knc", yr_ref[...])
    yi = pltpu.einshape("nkc->knc", yi_ref[...])
    n1 = FFT_N1
    spec = [jnp.dot(f2_ref[...], _unpack_bf16_pairs(jnp.concatenate([yr[j], yi[j]], axis=0)),
                    preferred_element_type=F32) for j in range(FFT_CHUNK)]
    prod = []
    for j, z in enumerate(spec):
        hf = _unpack_pairs_f32(h_ref[j])
        zr, zi, hr, hi = z[:n1], z[n1:], hf[:n1], hf[n1:]
        prod.append(jnp.concatenate([zr * hr - zi * hi, zr * hi + zi * hr], axis=0).astype(BF16))
    for j, pc in enumerate(prod):
        q_ref[j] = _pack_bf16_pairs(jnp.dot(g2_ref[j], pc, preferred_element_type=F32))


def _fft_last_kernel(qr_ref, qi_ref, g1_ref, gate_ref, zf_ref, fb_ref, o_ref):
    qr = pltpu.einshape("kjc->jkc", qr_ref[...])
    qi = pltpu.einshape("kjc->jkc", qi_ref[...])
    half = FFT_N2 // 2
    conv = []
    for j in range(FFT_CHUNK):
        qc = _unpack_bf16_pairs(jnp.concatenate([qr[j], qi[j]], axis=0))
        conv.append(jnp.dot(g1_ref[...], qc, preferred_element_type=F32))
    conv = jnp.stack(conv, axis=0)
    fb = fb_ref[...]
    for bi in range(2):
        cb = pltpu.einshape("jmc->mjc", conv[:, bi * half:(bi + 1) * half])
        zf = zf_ref[bi]
        o_ref[bi] = gate_ref[bi] * (cb + zf * fb)


def hyena_filter_spectrum(ts, consts, cb=FFT_CB):
    _, f1_real, f2, _, _ = consts
    no, n, c = ts.shape
    c = 2 * c
    n1, n2, ch = FFT_N1, FFT_N2, FFT_CHUNK
    ts4 = ts.reshape(no, n2, n1, c // 2)
    y = pl.pallas_call(
        _fft_stage1_real_kernel,
        out_shape=jax.ShapeDtypeStruct((no, n1, 2 * n2, c // 2), jnp.uint32),
        grid=(no, n1 // ch, c // cb),
        in_specs=[pl.BlockSpec((None, n2, ch, cb // 2), lambda o, j, k: (o, 0, j, k)),
                  pl.BlockSpec((ch, 2 * n2, n2), lambda o, j, k: (j, 0, 0))],
        out_specs=pl.BlockSpec((None, ch, 2 * n2, cb // 2), lambda o, j, k: (o, j, 0, k)),
        compiler_params=_params(("parallel", "parallel", "parallel"), big=True),
        name="fft_filter_stage1",
    )(ts4, f1_real)
    nk = n2 // ch
    return pl.pallas_call(
        _fft_stage2_kernel,
        out_shape=jax.ShapeDtypeStruct((no, n2, 2 * n1, c // 2), jnp.uint32),
        grid=(no, nk, c // cb),
        in_specs=[pl.BlockSpec((None, n1, ch, cb // 2), lambda o, j, k: (o, 0, j, k)),
                  pl.BlockSpec((None, n1, ch, cb // 2), lambda o, j, k: (o, 0, nk + j, k)),
                  pl.BlockSpec((2 * n1, 2 * n1), lambda o, j, k: (0, 0))],
        out_specs=pl.BlockSpec((None, ch, 2 * n1, cb // 2), lambda o, j, k: (o, j, 0, k)),
        compiler_params=_params(("parallel", "parallel", "parallel"), big=True),
        name="fft_filter_stage2",
    )(y, y, f2)


def hyena_long_conv_gate(zsrc, z_col, gate_src, gate_col, hf, order, fbias, consts, cb=FFT_CB):
    f1_pad, _, f2, g2, g1 = consts
    n1, n2, ch = FFT_N1, FFT_N2, FFT_CHUNK
    c = 2 * hf.shape[-1]
    ncb = c // cb
    half = n2 // 2
    y = pl.pallas_call(
        _fft_stage1_kernel,
        out_shape=jax.ShapeDtypeStruct((n1, 2 * n2, c // 2), jnp.uint32),
        grid=(n1 // ch, ncb),
        in_specs=[pl.BlockSpec((2, half, ch, cb), lambda j, k: (0, 0, j, z_col * ncb + k)),
                  pl.BlockSpec((ch, 2 * n2, n2), lambda j, k: (j, 0, 0))],
        out_specs=pl.BlockSpec((ch, 2 * n2, cb // 2), lambda j, k: (j, 0, k)),
        compiler_params=_params(("parallel", "parallel"), big=True),
        name="fft_stage1",
    )(zsrc, f1_pad)
    nk = n2 // ch
    q = pl.pallas_call(
        _fft_mid_kernel,
        out_shape=jax.ShapeDtypeStruct((n2, 2 * n1, c // 2), jnp.uint32),
        grid=(nk, ncb),
        in_specs=[pl.BlockSpec((n1, ch, cb // 2), lambda j, k: (0, j, k)),
                  pl.BlockSpec((n1, ch, cb // 2), lambda j, k: (0, nk + j, k)),
                  pl.BlockSpec((2 * n1, 2 * n1), lambda j, k: (0, 0)),
                  pl.BlockSpec((None, ch, 2 * n1, cb // 2), lambda j, k: (order, j, 0, k)),
                  pl.BlockSpec((ch, 2 * n1, 2 * n1), lambda j, k: (j, 0, 0))],
        out_specs=pl.BlockSpec((ch, 2 * n1, cb // 2), lambda j, k: (j, 0, k)),
        compiler_params=_params(("parallel", "parallel"), big=True),
        name="fft_mid",
    )(y, y, f2, hf, g2)
    nj = n1 // ch
    return pl.pallas_call(
        _fft_last_kernel,
        out_shape=jax.ShapeDtypeStruct((2, half, n1, c), F32),
        grid=(nj, ncb),
        in_specs=[pl.BlockSpec((n2, ch, cb // 2), lambda j, k: (0, j, k)),
                  pl.BlockSpec((n2, ch, cb // 2), lambda j, k: (0, nj + j, k)),
                  pl.BlockSpec((2 * half, 2 * n2), lambda j, k: (0, 0)),
                  pl.BlockSpec((2, half, ch, cb), lambda j, k: (0, 0, j, gate_col * ncb + k)),
                  pl.BlockSpec((2, half, ch, cb), lambda j, k: (0, 0, j, z_col * ncb + k)),
                  pl.BlockSpec((None, 1, cb), lambda j, k: (order, 0, k))],
        out_specs=pl.BlockSpec((2, half, ch, cb), lambda j, k: (0, 0, j, k)),
        compiler_params=_params(("parallel", "parallel"), big=True),
        name="fft_last",
    )(q, q, g1, gate_src, zsrc, fbias.reshape(fbias.shape[0], 1, c))


def hyena_mixer(hn, h_res, w_in, b_in, conv_w, conv_b, fw1, fb1, fw2, fb2, freq, fw3, fbias, w_out,
                batch, seq):
    width = w_out.shape[0]
    assert batch == 2 and 2 * seq == FFT_N1 * FFT_N2
    proj = matmul([hn], w_in, bias=b_in, name="hyena_in")
    sc = short_conv(proj.reshape(batch, seq, 3 * width), conv_w, conv_b)
    consts = _dft_constants()
    ts = hyena_two_sided_filters(seq, fw1, fb1, fw2, fb2, freq, fw3, width)
    hf = hyena_filter_spectrum(ts, consts)
    sc4 = sc.reshape(batch, FFT_N2 // 2, FFT_N1, 3 * width)
    zf1 = hyena_long_conv_gate(sc4, 2, sc4, 0, hf, 0, fbias, consts)
    zf2 = hyena_long_conv_gate(zf1, 0, sc4, 1, hf, 1, fbias, consts)
    return matmul([zf2.reshape(batch * seq, width)], w_out, res=h_res, tm=512, tn=1024,
                  name="hyena_out")


def moe_swiglu(hn2_packed, logits, wg, wu, wd):
    n = hn2_packed.shape[0]
    top_v, top_i = lax.top_k(logits, TOP_K)
    gates = jax.nn.softmax(top_v, axis=-1)
    e_flat = top_i.reshape(-1).astype(jnp.int32)
    nk = n * TOP_K
    onehot = (e_flat[:, None] == jnp.arange(N_EXPERTS, dtype=jnp.int32)[None]).astype(jnp.int32)
    csum = jnp.cumsum(onehot, axis=0)
    rank = jnp.take_along_axis(csum, e_flat[:, None], axis=1)[:, 0] - 1
    counts = csum[-1]
    padded = ((counts + MOE_TILE - 1) // MOE_TILE) * MOE_TILE
    pad_end = jnp.cumsum(padded)
    pad_start = pad_end - padded
    dest = pad_start[e_flat] + rank
    p_rows = nk + N_EXPERTS * MOE_TILE
    order = jnp.argsort(e_flat, stable=True).astype(jnp.int32)
    sorted_tok = jnp.pad(order // TOP_K, (0, DMA_LOOP_UNROLL))
    start = jnp.cumsum(counts) - counts
    nt = p_rows // MOE_TILE
    tile_start = jnp.arange(nt, dtype=jnp.int32) * MOE_TILE
    tile_used = tile_start < pad_end[-1]
    tile_exp = jnp.minimum(jnp.searchsorted(pad_end, tile_start, side='right'), N_EXPERTS - 1).astype(jnp.int32)
    tile_rows = jnp.clip(counts[tile_exp] - (tile_start - pad_start[tile_exp]), 0, MOE_TILE)
    tile_rows = jnp.where(tile_used, tile_rows, 0).astype(jnp.int32)
    last_exp = tile_exp[jnp.maximum(jnp.sum(tile_used.astype(jnp.int32)) - 1, 0)]
    tile_src = jnp.where(tile_used, start[tile_exp] + (tile_start - pad_start[tile_exp]), 0)
    tile_exp = jnp.where(tile_used, tile_exp, last_exp)
    ys = swiglu(hn2_packed, wg, wu, wd, tile_exp, tile_rows, tile_src.astype(jnp.int32), sorted_tok,
                tf=MOE_F_TILE, sub_rows=FFN_SUB_ROWS, name="swiglu_experts")
    return ys, dest.reshape(n, TOP_K), gates


def kernel(x, p, ln_mix, ln_ffn, ln_ple, final_norm, t5_bias, w_attn_in, w_attn_out, attn_sink, na_rpb, w_ffn_gate, w_ffn_up, w_ffn_down, w_hy_in, b_hy_in, w_hy_conv, b_hy_conv, w_hy_f1, b_hy_f1, w_hy_f2, b_hy_f2, hy_freq, w_hy_f3, hy_bias, w_hy_out, w_router, w_exp_gate, w_exp_up, w_exp_down, w_ple_proj, w_ple_gate):
    batch, seq, d = x.shape
    n = batch * seq
    depth = ln_mix.shape[0]
    h = x.reshape(n, d)
    for i in range(depth):
        li = i // 2
        hn = rmsnorm(h, ln_mix[i], BF16)
        if i % 2 == 0:
            na_off = A_WIDTH + 2 * A_KV_WIDTH
            w_in = w_attn_in[li]
            proj_a = matmul([hn], w_in[:, :na_off], out_dtype=BF16, tn=640, name="attn_in_a")
            proj_n = matmul([hn], w_in[:, na_off:], out_dtype=BF16, name="attn_in_n")
            oa = window_attention(proj_a.reshape(batch, seq, -1), t5_bias, attn_sink[li])
            ob = neighbourhood_attention(proj_n.reshape(batch, seq, -1), na_rpb[li])
            h = matmul([oa.reshape(n, A_WIDTH), ob.reshape(n, B_WIDTH)], w_attn_out[li], res=h,
                       name="attn_out")
            hn2 = rmsnorm(h, ln_ffn[i], BF16)
            nt = n // MOE_TILE
            h = swiglu(hn2, w_ffn_gate, w_ffn_up, w_ffn_down,
                       jnp.full((nt,), li, jnp.int32), jnp.full((nt,), MOE_TILE, jnp.int32),
                       res=h, name="swiglu_dense")
            hn3 = rmsnorm(h, ln_ple[i], BF16)
        else:
            h = hyena_mixer(hn, h, w_hy_in[li], b_hy_in[li], w_hy_conv[li], b_hy_conv[li],
                            w_hy_f1[li], b_hy_f1[li], w_hy_f2[li], b_hy_f2[li], hy_freq[li],
                            w_hy_f3[li], hy_bias[li], w_hy_out[li], batch, seq)
            wr_pad = jnp.pad(w_router[li].astype(F32), ((0, 0), (0, 128 - N_EXPERTS)))
            hn2_packed, logits = rmsnorm_router(h, ln_ffn[i], wr_pad)
            ys, dest2, gates = moe_swiglu(hn2_packed, logits[:, :N_EXPERTS], w_exp_gate[li],
                                          w_exp_up[li], w_exp_down[li])
            h, hn3 = moe_combine_rmsnorm(h, ys, dest2, gates, ln_ple[i])
        h = ple(hn3, p[i].reshape(n, -1), h, w_ple_gate[i], w_ple_proj[i])
    return rmsnorm(h, final_norm, F32).reshape(batch, seq, d)
```

```python
import functools
import math

import jax
import jax.numpy as jnp
import numpy as np
from jax import lax
from jax.experimental import pallas as pl
from jax.experimental.pallas import tpu as pltpu

F32 = jnp.float32
BF16 = jnp.bfloat16
NEG_INF = -1e30
RMS_EPS = 1e-6

V7X_VMEM_LIMIT_BYTES = 56 * 1024 * 1024

HEAD_DIM = 64
A_Q_HEADS = 16
A_KV_HEADS = 2
A_GROUP = A_Q_HEADS // A_KV_HEADS
A_BLOCK = 128
T5_BUCKETS = 32
T5_MAX_DIST = 128
B_HEADS = 16
GRID_W = 64
NA_WIN_H = 8
NA_WIN_W = 16
A_WIDTH = A_Q_HEADS * HEAD_DIM
A_KV_WIDTH = A_KV_HEADS * HEAD_DIM
B_WIDTH = B_HEADS * HEAD_DIM
N_EXPERTS = 8
TOP_K = 2
HYENA_EMB = 33
HYENA_DECAY_TARGET = 1e-2
HYENA_FAST_PCT = 0.3
HYENA_SLOW_PCT = 1.5

FFT_N1 = 64
FFT_N2 = 128
FFT_CHUNK = 8
FFT_CB = 1024

MOE_TILE = 1024
FFN_F_TILE = 256
MOE_F_TILE = 512
FFN_SUB_ROWS = 512
DMA_LOOP_UNROLL = 8


def _params(semantics, big=False):
    return pltpu.CompilerParams(
        dimension_semantics=semantics,
        vmem_limit_bytes=V7X_VMEM_LIMIT_BYTES if big else None)


def _rmsnorm_kernel(x_ref, g_ref, o_ref):
    x = x_ref[...]
    y = x * lax.rsqrt(jnp.mean(x * x, axis=-1, keepdims=True) + RMS_EPS)
    o_ref[...] = (y * g_ref[...]).astype(o_ref.dtype)


def rmsnorm(x2, g, out_dtype, tm=512):
    n, d = x2.shape
    return pl.pallas_call(
        _rmsnorm_kernel,
        out_shape=jax.ShapeDtypeStruct((n, d), out_dtype),
        grid=(n // tm,),
        in_specs=[pl.BlockSpec((tm, d), lambda i: (i, 0)),
                  pl.BlockSpec((1, d), lambda i: (0, 0))],
        out_specs=pl.BlockSpec((tm, d), lambda i: (i, 0)),
        compiler_params=_params(("parallel",)),
        name="rmsnorm",
    )(x2, g.reshape(1, d))


def _pack_bf16_pairs(y):
    w = y.shape[1] // 2
    bits = pltpu.bitcast(y.astype(BF16).astype(F32), jnp.uint32)
    return (bits[:, :w] >> 16) | (bits[:, w:] & jnp.uint32(0xFFFF0000))


def _unpack_pairs_f32(p):
    lo = pltpu.bitcast(p << 16, F32)
    hi = pltpu.bitcast(p & jnp.uint32(0xFFFF0000), F32)
    return jnp.concatenate([lo, hi], axis=1)


def _unpack_bf16_pairs(p):
    return _unpack_pairs_f32(p).astype(BF16)


def _rmsnorm_router_kernel(x_ref, g_ref, wr_ref, o_ref, l_ref):
    x = x_ref[...]
    y = x * lax.rsqrt(jnp.mean(x * x, axis=-1, keepdims=True) + RMS_EPS)
    y = y * g_ref[...]
    o_ref[...] = _pack_bf16_pairs(y)
    l_ref[...] = jnp.dot(y, wr_ref[...], preferred_element_type=F32,
                         precision=lax.Precision.HIGHEST)


def rmsnorm_router(x2, g, w_router_pad, tm=512):
    n, d = x2.shape
    ne = w_router_pad.shape[1]
    return pl.pallas_call(
        _rmsnorm_router_kernel,
        out_shape=(jax.ShapeDtypeStruct((n, d // 2), jnp.uint32), jax.ShapeDtypeStruct((n, ne), F32)),
        grid=(n // tm,),
        in_specs=[pl.BlockSpec((tm, d), lambda i: (i, 0)),
                  pl.BlockSpec((1, d), lambda i: (0, 0)),
                  pl.BlockSpec((d, ne), lambda i: (0, 0))],
        out_specs=(pl.BlockSpec((tm, d // 2), lambda i: (i, 0)),
                   pl.BlockSpec((tm, ne), lambda i: (i, 0))),
        compiler_params=_params(("parallel",)),
        name="rmsnorm_router",
    )(x2, g.reshape(1, d), w_router_pad)


def _mm_kernel(*refs, n_a, has_bias, has_res):
    a_refs = refs[:n_a]
    w_refs = refs[n_a:2 * n_a]
    idx = 2 * n_a
    bias_ref = refs[idx] if has_bias else None
    idx += int(has_bias)
    res_ref = refs[idx] if has_res else None
    idx += int(has_res)
    o_ref = refs[idx]
    wbf_refs = refs[idx + 1:idx + 1 + n_a]

    @pl.when(pl.program_id(1) == 0)
    def _():
        for w_ref, wbf_ref in zip(w_refs, wbf_refs):
            wbf_ref[...] = w_ref[...].astype(BF16)

    acc = None
    for a_ref, wbf_ref in zip(a_refs, wbf_refs):
        d = jnp.dot(a_ref[...].astype(BF16), wbf_ref[...], preferred_element_type=F32)
        acc = d if acc is None else acc + d
    if has_bias:
        acc = acc + bias_ref[...]
    if has_res:
        acc = acc + res_ref[...]
    o_ref[...] = acc.astype(o_ref.dtype)


def matmul(a_list, w, *, bias=None, res=None, out_dtype=F32, tm=1024, tn=1024, name="matmul"):
    m = a_list[0].shape[0]
    n = w.shape[1]
    n_a = len(a_list)
    k_each = a_list[0].shape[1]
    assert all(a.shape == (m, k_each) for a in a_list) and w.shape[0] == n_a * k_each
    in_specs = [pl.BlockSpec((tm, k_each), lambda j, i: (i, 0)) for _ in a_list]
    in_specs += [pl.BlockSpec((k_each, tn), functools.partial(lambda j, i, kb: (kb, j), kb=kb))
                 for kb in range(n_a)]
    args = list(a_list) + [w] * n_a
    if bias is not None:
        in_specs.append(pl.BlockSpec((1, tn), lambda j, i: (0, j)))
        args.append(bias.reshape(1, n))
    if res is not None:
        in_specs.append(pl.BlockSpec((tm, tn), lambda j, i: (i, j)))
        args.append(res)
    return pl.pallas_call(
        functools.partial(_mm_kernel, n_a=n_a, has_bias=bias is not None, has_res=res is not None),
        out_shape=jax.ShapeDtypeStruct((m, n), out_dtype),
        grid=(n // tn, m // tm),
        in_specs=in_specs,
        out_specs=pl.BlockSpec((tm, tn), lambda j, i: (i, j)),
        scratch_shapes=[pltpu.VMEM((k_each, tn), BF16) for _ in a_list],
        compiler_params=_params(("parallel", "arbitrary"), big=True),
        name=name,
    )(*args)


def _ple_kernel(hn_ref, p_ref, h_ref, wg_ref, wp_ref, o_ref, wg_bf, wp_bf):
    @pl.when(pl.program_id(1) == 0)
    def _():
        wg_bf[...] = wg_ref[...].astype(BF16)
        wp_bf[...] = wp_ref[...].astype(BF16)

    a = jnp.dot(hn_ref[...], wg_bf[...], preferred_element_type=F32)
    pp = jnp.dot(p_ref[...].astype(BF16), wp_bf[...], preferred_element_type=F32)
    o_ref[...] = h_ref[...] + jax.nn.sigmoid(a) * pp


def ple(hn, p2, h2, w_gate, w_proj, tm=1024, tn=1024):
    m, d = h2.shape
    pd = p2.shape[1]
    return pl.pallas_call(
        _ple_kernel,
        out_shape=jax.ShapeDtypeStruct((m, d), F32),
        grid=(d // tn, m // tm),
        in_specs=[pl.BlockSpec((tm, d), lambda j, i: (i, 0)),
                  pl.BlockSpec((tm, pd), lambda j, i: (i, 0)),
                  pl.BlockSpec((tm, tn), lambda j, i: (i, j)),
                  pl.BlockSpec((d, tn), lambda j, i: (0, j)),
                  pl.BlockSpec((pd, tn), lambda j, i: (0, j))],
        out_specs=pl.BlockSpec((tm, tn), lambda j, i: (i, j)),
        scratch_shapes=[pltpu.VMEM((d, tn), BF16), pltpu.VMEM((pd, tn), BF16)],
        compiler_params=_params(("parallel", "arbitrary"), big=True),
        name="ple",
    )(hn, p2, h2, w_gate, w_proj)


def _row_copy(src_hbm, row, dst, i, sem):
    return pltpu.make_async_copy(src_hbm.at[pl.ds(row, 1), :], dst.at[pl.ds(i, 1), :], sem)


def _swiglu_kernel(exp_ref, rows_ref, src_ref, tok_ref, x_ref, wg_ref, wu_ref, wd_ref, *rest,
                   n_col, gather, sub_rows, has_res):
    del exp_ref
    res_ref = rest[0] if has_res else None
    o_ref = rest[int(has_res)]
    scratch = rest[int(has_res) + 1:]
    t = pl.program_id(0)
    f = pl.program_id(1)
    nt = pl.num_programs(0)
    n_rows = rows_ref[t]
    used = n_rows > 0
    tm = o_ref.shape[0]

    @pl.when(f == 0)
    def _():
        o_ref[...] = res_ref[...] if has_res else jnp.zeros_like(o_ref)

    if gather:
        xbuf, xbf, sems = scratch
        slot = t % 2

        def for_each_row(tile, fn):
            def body(c, carry):
                for r in range(DMA_LOOP_UNROLL):
                    fn(c * DMA_LOOP_UNROLL + r)
                return carry
            lax.fori_loop(0, (rows_ref[tile] + DMA_LOOP_UNROLL - 1) // DMA_LOOP_UNROLL, body, 0)

        def start_tile(tile, sl):
            for_each_row(tile, lambda i: _row_copy(x_ref, tok_ref[src_ref[tile] + i], xbuf.at[sl], i,
                                                   sems.at[sl, i]).start())

        @pl.when(jnp.logical_and(f == 0, t == 0))
        def _():
            xbuf[...] = jnp.zeros_like(xbuf)
            start_tile(0, 0)

        @pl.when(jnp.logical_and(f == 0, used))
        def _():
            for_each_row(t, lambda i: _row_copy(x_ref, 0, xbuf.at[slot], i, sems.at[slot, i]).wait())
            xbf[...] = _unpack_bf16_pairs(xbuf[slot])

        @pl.when(jnp.logical_and(f == 1, t + 1 < nt))
        def _():
            start_tile(jnp.minimum(t + 1, nt - 1), 1 - slot)

        x_src = xbf
    else:
        x_src = x_ref

    def ffn_rows(n):
        x = x_src[0:n, :]
        g = jnp.dot(x, wg_ref[...].astype(BF16), preferred_element_type=F32)
        u = jnp.dot(x, wu_ref[...].astype(BF16), preferred_element_type=F32)
        hmid = (g * jax.nn.sigmoid(g) * u).astype(BF16)
        wd = wd_ref[...].astype(BF16)
        cw = o_ref.shape[1] // n_col
        for c in range(n_col):
            y = jnp.dot(hmid, wd[:, c * cw:(c + 1) * cw], preferred_element_type=F32)
            o_ref[0:n, c * cw:(c + 1) * cw] += y

    pl.when(n_rows > (sub_rows if sub_rows < tm else 0))(lambda: ffn_rows(tm))
    if sub_rows < tm:
        pl.when(jnp.logical_and(used, n_rows <= sub_rows))(lambda: ffn_rows(sub_rows))


def swiglu(x, w_gate, w_up, w_down, tile_exp, tile_rows, tile_src=None, src_rows=None, *, res=None,
           tm=MOE_TILE, tf=FFN_F_TILE, sub_rows=MOE_TILE, name="swiglu"):
    gather = src_rows is not None
    has_res = res is not None
    m = tile_rows.shape[0] * tm
    d = w_gate.shape[1]
    dff = w_gate.shape[2]
    nf = dff // tf
    nt = m // tm
    assert nf >= 2

    def w_in_map(t, f, exp_ref, rows_ref, src_ref, tok_ref):
        return (exp_ref[t], 0, jnp.where(rows_ref[t] > 0, f, nf - 1))

    def w_out_map(t, f, exp_ref, rows_ref, src_ref, tok_ref):
        return (exp_ref[t], jnp.where(rows_ref[t] > 0, f, nf - 1), 0)

    def tile_map(t, f, exp_ref, rows_ref, src_ref, tok_ref):
        return (t, 0)

    if gather:
        x_spec = pl.BlockSpec(memory_space=pl.ANY)
        scratch = [pltpu.VMEM((2, tm, d // 2), jnp.uint32), pltpu.VMEM((tm, d), BF16),
                   pltpu.SemaphoreType.DMA((2, tm))]
    else:
        x_spec = pl.BlockSpec((tm, d), tile_map)
        scratch = []
        tile_src = jnp.zeros((nt,), jnp.int32)
        src_rows = jnp.zeros((1,), jnp.int32)
    in_specs = [x_spec,
                pl.BlockSpec((None, d, tf), w_in_map),
                pl.BlockSpec((None, d, tf), w_in_map),
                pl.BlockSpec((None, tf, d), w_out_map)]
    args = [x, w_gate, w_up, w_down]
    if has_res:
        in_specs.append(pl.BlockSpec((tm, d), tile_map, pipeline_mode=pl.Buffered(1)))
        args.append(res)
    grid_spec = pltpu.PrefetchScalarGridSpec(
        num_scalar_prefetch=4,
        grid=(nt, nf),
        in_specs=in_specs,
        out_specs=pl.BlockSpec((tm, d), tile_map, pipeline_mode=pl.Buffered(1) if gather else None),
        scratch_shapes=scratch,
    )
    return pl.pallas_call(
        functools.partial(_swiglu_kernel, n_col=4, gather=gather, sub_rows=sub_rows, has_res=has_res),
        out_shape=jax.ShapeDtypeStruct((m, d), F32),
        grid_spec=grid_spec,
        compiler_params=_params(("arbitrary", "arbitrary"), big=True),
        name=name,
    )(tile_exp, tile_rows, tile_src, src_rows, *args)


COMBINE_TILE = 256


def _combine_kernel(d0_ref, d1_ref, ys_hbm, gate_ref, h_ref, g_ref, hout_ref, hn_ref, buf, sems):
    i = pl.program_id(0)
    n = pl.num_programs(0)
    tt = h_ref.shape[0]
    slot = i % 2
    dests = (d0_ref, d1_ref)

    def start_tile(tile, sl):
        def body(r, c):
            for k, d_ref in enumerate(dests):
                _row_copy(ys_hbm, d_ref[tile * tt + r], buf.at[sl, k], r, sems.at[sl, k, r]).start()
            return c
        lax.fori_loop(0, tt, body, 0, unroll=DMA_LOOP_UNROLL)

    @pl.when(i == 0)
    def _():
        start_tile(0, 0)

    @pl.when(i + 1 < n)
    def _():
        start_tile(i + 1, 1 - slot)

    def wait_body(r, c):
        for k in range(TOP_K):
            _row_copy(ys_hbm, 0, buf.at[slot, k], r, sems.at[slot, k, r]).wait()
        return c
    lax.fori_loop(0, tt, wait_body, 0, unroll=DMA_LOOP_UNROLL)

    gates = gate_ref[...]
    x = h_ref[...] + (buf[slot, 0] * gates[:, 0:1] + buf[slot, 1] * gates[:, 1:2])
    hout_ref[...] = x
    y = x * lax.rsqrt(jnp.mean(x * x, axis=-1, keepdims=True) + RMS_EPS)
    hn_ref[...] = (y * g_ref[...]).astype(hn_ref.dtype)


def moe_combine_rmsnorm(h2, ys, dest2, gates, g, tt=COMBINE_TILE):
    n, d = h2.shape
    grid_spec = pltpu.PrefetchScalarGridSpec(
        num_scalar_prefetch=2,
        grid=(n // tt,),
        in_specs=[pl.BlockSpec(memory_space=pl.ANY),
                  pl.BlockSpec((tt, TOP_K), lambda i, a, b: (i, 0)),
                  pl.BlockSpec((tt, d), lambda i, a, b: (i, 0)),
                  pl.BlockSpec((1, d), lambda i, a, b: (0, 0))],
        out_specs=(pl.BlockSpec((tt, d), lambda i, a, b: (i, 0)),
                   pl.BlockSpec((tt, d), lambda i, a, b: (i, 0))),
        scratch_shapes=[pltpu.VMEM((2, TOP_K, tt, d), F32), pltpu.SemaphoreType.DMA((2, TOP_K, tt))],
    )
    return pl.pallas_call(
        _combine_kernel,
        out_shape=(jax.ShapeDtypeStruct((n, d), F32), jax.ShapeDtypeStruct((n, d), BF16)),
        grid_spec=grid_spec,
        compiler_params=_params(("arbitrary",), big=True),
        name="moe_combine",
    )(dest2[:, 0], dest2[:, 1], ys, gates, h2, g.reshape(1, d))


def _t5_bucket(rel):
    half = T5_BUCKETS // 2
    max_exact = half // 2
    n = jnp.abs(rel)
    log_ratio = jnp.log(jnp.maximum(n, 1).astype(F32) / max_exact) / math.log(T5_MAX_DIST / max_exact)
    large = jnp.minimum(max_exact + (log_ratio * (half - max_exact)).astype(jnp.int32), half - 1)
    return jnp.where(rel > 0, half, 0) + jnp.where(n < max_exact, n, large)


def _window_bias_table(t5_bias):
    i = jnp.arange(A_BLOCK)[:, None]
    j = jnp.arange(3 * A_BLOCK)[None, :]
    rel = j - A_BLOCK - i
    onehot = (_t5_bucket(rel)[None] == jnp.arange(T5_BUCKETS)[:, None, None]).astype(F32)
    bias = jnp.einsum('bh,bij->hij', t5_bias.astype(F32), onehot, precision=lax.Precision.HIGHEST)
    return jnp.where((jnp.abs(rel) <= A_BLOCK)[None], bias, NEG_INF)


def _window_kernel(sink_ref, q_ref, kv_ref, bias_ref, o_ref, *, nb):
    n = pl.program_id(1)
    scale = HEAD_DIM ** -0.5
    starts = (jnp.maximum(n - 1, 0), n, jnp.minimum(n + 1, nb - 1))
    kv = [kv_ref[pl.ds(pl.multiple_of(s * A_BLOCK, A_BLOCK), A_BLOCK), :] for s in starts]
    col = lax.broadcasted_iota(jnp.int32, (A_BLOCK, 3 * A_BLOCK), 1)
    edge_ok = jnp.logical_and(jnp.logical_or(n > 0, col >= A_BLOCK),
                              jnp.logical_or(n < nb - 1, col < 2 * A_BLOCK))
    kv = jnp.concatenate(kv, axis=0)
    q = q_ref[...]
    lo = lax.broadcasted_iota(jnp.int32, (1, 2 * HEAD_DIM), 1) < HEAD_DIM
    sel = (jnp.where(lo, scale, 0.0).astype(BF16), jnp.where(lo, 0.0, scale).astype(BF16))
    pairs_per_kv = A_GROUP // 2
    kk, vv = [], []
    for kh in range(A_KV_HEADS):
        k1 = kv[:, kh * HEAD_DIM:(kh + 1) * HEAD_DIM]
        v1 = kv[:, A_KV_WIDTH + kh * HEAD_DIM:A_KV_WIDTH + (kh + 1) * HEAD_DIM]
        kk.append(jnp.concatenate([k1, k1], axis=1))
        vv.append(jnp.concatenate([v1, v1], axis=1))
    scores = []
    for pr in range(A_Q_HEADS // 2):
        q2 = q[:, pr * 2 * HEAD_DIM:(pr + 1) * 2 * HEAD_DIM]
        qq = jnp.concatenate([q2 * sel[0], q2 * sel[1]], axis=0)
        s2 = lax.dot_general(qq, kk[pr // pairs_per_kv], (((1,), (1,)), ((), ())),
                             preferred_element_type=F32)
        for i in range(2):
            scores.append(jnp.where(edge_ok, s2[i * A_BLOCK:(i + 1) * A_BLOCK] + bias_ref[2 * pr + i],
                                    NEG_INF))
    probs, dens = [], []
    for h, s in enumerate(scores):
        sink = sink_ref[h]
        mx = jnp.maximum(jnp.max(s, axis=-1, keepdims=True), sink)
        p = jnp.exp(s - mx)
        dens.append(jnp.sum(p, axis=-1, keepdims=True) + jnp.exp(sink - mx))
        probs.append(p.astype(BF16))
    for pr in range(A_Q_HEADS // 2):
        v2 = vv[pr // pairs_per_kv]
        oa = jnp.dot(probs[2 * pr], v2, preferred_element_type=F32) / dens[2 * pr]
        ob = jnp.dot(probs[2 * pr + 1], v2, preferred_element_type=F32) / dens[2 * pr + 1]
        o_ref[:, pr * 2 * HEAD_DIM:(pr + 1) * 2 * HEAD_DIM] = jnp.where(lo, oa, ob).astype(o_ref.dtype)


def window_attention(proj_a, t5_bias, sink):
    b, s, _ = proj_a.shape
    nb = s // A_BLOCK
    kvw = 2 * A_KV_WIDTH
    grid_spec = pltpu.PrefetchScalarGridSpec(
        num_scalar_prefetch=0,
        grid=(b, nb),
        in_specs=[pl.BlockSpec(memory_space=pltpu.SMEM),
                  pl.BlockSpec((None, A_BLOCK, A_WIDTH), lambda bi, n: (bi, n, 0)),
                  pl.BlockSpec((None, s, kvw), lambda bi, n: (bi, 0, A_WIDTH // kvw)),
                  pl.BlockSpec((A_Q_HEADS, A_BLOCK, 3 * A_BLOCK), lambda bi, n: (0, 0, 0))],
        out_specs=pl.BlockSpec((None, A_BLOCK, A_WIDTH), lambda bi, n: (bi, n, 0)),
    )
    return pl.pallas_call(
        functools.partial(_window_kernel, nb=nb),
        out_shape=jax.ShapeDtypeStruct((b, s, A_WIDTH), BF16),
        grid_spec=grid_spec,
        compiler_params=_params(("parallel", "arbitrary")),
        name="window_attention",
    )(sink.astype(F32), proj_a, proj_a, _window_bias_table(t5_bias))


NA_HEAD_GROUP = 16


def _na_bias_table(rpb, rows):
    kh = min(NA_WIN_H, rows)
    kw = NA_WIN_W
    c = jnp.arange(GRID_W)
    cs = jnp.clip(c - kw // 2, 0, GRID_W - kw)
    col_ok = (c[None] >= cs[:, None]) & (c[None] < cs[:, None] + kw)
    col_off = jnp.clip(c[None] - c[:, None], -(kw - 1), kw - 1) + kw - 1
    onehot = (col_off[None] == jnp.arange(2 * kw - 1)[:, None, None]).astype(F32)
    by_col = jnp.einsum('hrc,cqk->hrqk', rpb.astype(F32), onehot, precision=lax.Precision.HIGHEST)
    by_col = jnp.where(col_ok[None, None], by_col, NEG_INF)
    tabs = [by_col[:, NA_WIN_H - 1 - d:NA_WIN_H - 1 - d + kh] for d in range(kh)]
    bias = jnp.stack(tabs, axis=0).transpose(0, 1, 3, 2, 4)
    return bias.reshape(kh, rpb.shape[0], GRID_W, kh * GRID_W)


def _na_kernel(q_ref, k_ref, v_ref, bias_ref, o_ref, *, rows):
    r = pl.program_id(2)
    kh = min(NA_WIN_H, rows)
    scale = HEAD_DIM ** -0.5
    start = pl.multiple_of(jnp.clip(r - kh // 2, 0, rows - kh) * GRID_W, GRID_W)
    k = k_ref[pl.ds(start, kh * GRID_W), :]
    v = v_ref[pl.ds(start, kh * GRID_W), :]
    q = q_ref[...]
    lo = lax.broadcasted_iota(jnp.int32, (1, 2 * HEAD_DIM), 1) < HEAD_DIM
    sel = (jnp.where(lo, scale, 0.0).astype(BF16), jnp.where(lo, 0.0, scale).astype(BF16))
    n_pairs = NA_HEAD_GROUP // 2
    scores = []
    for pr in range(n_pairs):
        sl = slice(pr * 2 * HEAD_DIM, (pr + 1) * 2 * HEAD_DIM)
        q2 = q[:, sl]
        qq = jnp.concatenate([q2 * sel[0], q2 * sel[1]], axis=0)
        s2 = lax.dot_general(qq, k[:, sl], (((1,), (1,)), ((), ())), preferred_element_type=F32)
        for i in range(2):
            scores.append(s2[i * GRID_W:(i + 1) * GRID_W] + bias_ref[2 * pr + i])
    probs, dens = [], []
    for s in scores:
        mx = jnp.max(s, axis=-1, keepdims=True)
        p = jnp.exp(s - mx)
        dens.append(jnp.sum(p, axis=-1, keepdims=True))
        probs.append(p.astype(BF16))
    for pr in range(n_pairs):
        sl = slice(pr * 2 * HEAD_DIM, (pr + 1) * 2 * HEAD_DIM)
        oa = jnp.dot(probs[2 * pr], v[:, sl], preferred_element_type=F32) / dens[2 * pr]
        ob = jnp.dot(probs[2 * pr + 1], v[:, sl], preferred_element_type=F32) / dens[2 * pr + 1]
        o_ref[:, sl] = jnp.where(lo, oa, ob).astype(o_ref.dtype)


def neighbourhood_attention(proj_n, rpb):
    b, s, _ = proj_n.shape
    rows = s // GRID_W
    kh = min(NA_WIN_H, rows)
    gw = NA_HEAD_GROUP * HEAD_DIM
    ng = B_WIDTH // gw

    def bias_map(bi, g, r):
        return (r - jnp.clip(r - kh // 2, 0, rows - kh), g, 0, 0)

    return pl.pallas_call(
        functools.partial(_na_kernel, rows=rows),
        out_shape=jax.ShapeDtypeStruct((b, s, B_WIDTH), BF16),
        grid=(b, ng, rows),
        in_specs=[pl.BlockSpec((None, GRID_W, gw), lambda bi, g, r: (bi, r, g)),
                  pl.BlockSpec((None, s, gw), lambda bi, g, r: (bi, 0, ng + g)),
                  pl.BlockSpec((None, s, gw), lambda bi, g, r: (bi, 0, 2 * ng + g)),
                  pl.BlockSpec((None, NA_HEAD_GROUP, GRID_W, kh * GRID_W), bias_map)],
        out_specs=pl.BlockSpec((None, GRID_W, gw), lambda bi, g, r: (bi, r, g)),
        compiler_params=_params(("parallel", "parallel", "arbitrary"), big=True),
        name="neighbourhood_attention",
    )(proj_n, proj_n, proj_n, _na_bias_table(rpb, rows))


CONV_HALO = 16
CONV_COL_CHUNK = 256


def _hyena_in_kernel(prev_ref, cur_ref, next_ref, w_ref, b_ref, cw_ref, cb_ref, o_ref, wbf_ref, *,
                     tiles_per_seq):
    i = pl.program_id(1)

    @pl.when(i == 0)
    def _():
        wbf_ref[...] = w_ref[...].astype(BF16)

    tm = cur_ref.shape[0]
    rows = tm + 2 * CONV_HALO
    a = jnp.concatenate([prev_ref[...], cur_ref[...], next_ref[...]], axis=0)
    pos = i % tiles_per_seq
    row = lax.broadcasted_iota(jnp.int32, (rows, 1), 0)
    kill_up = jnp.logical_and(pos == 0, row == CONV_HALO)
    kill_dn = jnp.logical_and(pos == tiles_per_seq - 1, row == CONV_HALO + tm - 1)
    for c in range(o_ref.shape[1] // CONV_COL_CHUNK):
        cs = slice(c * CONV_COL_CHUNK, (c + 1) * CONV_COL_CHUNK)
        p = jnp.dot(a, wbf_ref[:, cs], preferred_element_type=F32) + b_ref[:, cs]
        up = jnp.where(kill_up, 0.0, pltpu.roll(p, 1, 0))
        dn = jnp.where(kill_dn, 0.0, pltpu.roll(p, rows - 1, 0))
        y = up * cw_ref[0:1, cs] + p * cw_ref[1:2, cs] + dn * cw_ref[2:3, cs] + cb_ref[:, cs]
        o_ref[:, cs] = y[CONV_HALO:CONV_HALO + tm]


def hyena_in_conv(hn, w_in, b_in, conv_w, conv_b, seq, tm=1024, tn=1024):
    m, k = hn.shape
    n = w_in.shape[1]
    hb = tm // CONV_HALO
    last_halo = m // CONV_HALO - 1
    return pl.pallas_call(
        functools.partial(_hyena_in_kernel, tiles_per_seq=seq // tm),
        out_shape=jax.ShapeDtypeStruct((m, n), F32),
        grid=(n // tn, m // tm),
        in_specs=[pl.BlockSpec((CONV_HALO, k), lambda j, i: (jnp.maximum(i * hb - 1, 0), 0)),
                  pl.BlockSpec((tm, k), lambda j, i: (i, 0)),
                  pl.BlockSpec((CONV_HALO, k), lambda j, i: (jnp.minimum((i + 1) * hb, last_halo), 0)),
                  pl.BlockSpec((k, tn), lambda j, i: (0, j)),
                  pl.BlockSpec((1, tn), lambda j, i: (0, j)),
                  pl.BlockSpec((3, tn), lambda j, i: (0, j)),
                  pl.BlockSpec((1, tn), lambda j, i: (0, j))],
        out_specs=pl.BlockSpec((tm, tn), lambda j, i: (i, j)),
        scratch_shapes=[pltpu.VMEM((k, tn), BF16)],
        compiler_params=_params(("parallel", "arbitrary"), big=True),
        name="hyena_in_conv",
    )(hn, hn, hn, w_in, b_in.reshape(1, n), conv_w, conv_b.reshape(1, n))


def _filter_kernel(z_ref, t_ref, fw1_ref, fb1_ref, fw2_ref, fb2_ref, fr_ref, w3_ref, w3b_ref,
                   delta_ref, o_ref, *, seq):
    m = pl.program_id(0)
    hi = lax.Precision.HIGHEST
    fr = fr_ref[...]
    hid = jnp.sin(fr * (jnp.dot(z_ref[...], fw1_ref[...], preferred_element_type=F32, precision=hi)
                        + fb1_ref[...]))
    hid = jnp.sin(fr * (jnp.dot(hid, fw2_ref[...], preferred_element_type=F32, precision=hi)
                        + fb2_ref[...]))
    hid_bf = hid.astype(BF16)
    h = jnp.dot(hid_bf, w3_ref[...].astype(BF16), preferred_element_type=F32)
    tm = h.shape[0]
    width = delta_ref.shape[1]
    row = lax.broadcasted_iota(jnp.int32, (tm, width), 0) + m * tm
    decay = jnp.where(row == seq, 0.0, jnp.exp(-t_ref[...] * delta_ref[...]))
    def emit(taps):
        for o in range(o_ref.shape[0]):
            for kb in range(width // FFT_CB):
                cols = slice(kb * FFT_CB, (kb + 1) * FFT_CB)
                blk = taps[:, o * width:(o + 1) * width][:, cols] * decay[:, cols]
                o_ref[o, :, kb * (FFT_CB // 2):(kb + 1) * (FFT_CB // 2)] = _pack_bf16_pairs(blk)

    @pl.when(m != 0)
    def _():
        emit(h)

    @pl.when(m == 0)
    def _():
        h_back = jnp.dot(hid_bf, w3b_ref[...].astype(BF16), preferred_element_type=F32)
        first = lax.broadcasted_iota(jnp.int32, h.shape, 0) == 0
        emit(h + jnp.where(first, h_back, 0.0))


def hyena_two_sided_filters(seq, fw1, fb1, fw2, fb2, freq, fw3, width, tm=512):
    n = 2 * seq
    u = jnp.arange(n)
    pos = jnp.where(u <= seq, u, n - u)
    pos = jnp.where(u == seq, 0, pos)
    bands = (HYENA_EMB - 1) // 2
    f = jnp.linspace(1e-4, bands - 1, bands, dtype=F32)[None]
    t = (pos.astype(F32) / (seq - 1))[:, None]
    w = ((2.0 * math.pi / seq) * pos.astype(F32))[:, None]
    z = jnp.concatenate([t, jnp.cos(f * w), -jnp.sin(f * w)], axis=-1)
    emb_pad = 40
    z = jnp.pad(z, ((0, 0), (0, emb_pad - HYENA_EMB)))
    fw1p = jnp.pad(fw1.astype(F32), ((0, emb_pad - HYENA_EMB), (0, 0)))
    fh = fw1.shape[1]
    max_decay = math.log(HYENA_DECAY_TARGET) / HYENA_FAST_PCT
    min_decay = math.log(HYENA_DECAY_TARGET) / HYENA_SLOW_PCT
    deltas = jnp.abs(jnp.linspace(min_decay, max_decay, width, dtype=F32))[None]
    half_tiles = seq // tm
    orders = fw3.shape[1] // (2 * width)
    ow = orders * width

    return pl.pallas_call(
        functools.partial(_filter_kernel, seq=seq),
        out_shape=jax.ShapeDtypeStruct((orders, n, width // 2), jnp.uint32),
        grid=(n // tm,),
        in_specs=[pl.BlockSpec((tm, emb_pad), lambda m: (m, 0)),
                  pl.BlockSpec((tm, 1), lambda m: (m, 0)),
                  pl.BlockSpec((emb_pad, fh), lambda m: (0, 0)),
                  pl.BlockSpec((1, fh), lambda m: (0, 0)),
                  pl.BlockSpec((fh, fh), lambda m: (0, 0)),
                  pl.BlockSpec((1, fh), lambda m: (0, 0)),
                  pl.BlockSpec((1, fh), lambda m: (0, 0)),
                  pl.BlockSpec((fh, ow), lambda m: (0, jnp.where(m >= half_tiles, 1, 0))),
                  pl.BlockSpec((fh, ow), lambda m: (0, 1)),
                  pl.BlockSpec((1, width), lambda m: (0, 0))],
        out_specs=pl.BlockSpec((orders, tm, width // 2), lambda m: (0, m, 0)),
        compiler_params=_params(("arbitrary",), big=True),
        name="hyena_filters",
    )(z, t, fw1p, fb1.reshape(1, fh).astype(F32), fw2.astype(F32), fb2.reshape(1, fh).astype(F32),
      freq.reshape(1, fh).astype(F32), fw3, fw3, deltas)


def _dft_constants():
    n1, n2 = FFT_N1, FFT_N2
    n = n1 * n2
    a1 = np.arange(n1)
    a2 = np.arange(n2)
    half = n2 // 2

    def cplx_block(w):
        return np.block([[w.real, -w.imag], [w.imag, w.real]])

    ang = (a2[None, :, None] * a2[None, None, :] / n2) + (a1[:, None, None] * a2[None, :, None] / n)
    w1 = np.exp(-2j * np.pi * ang)
    f1_pad = np.stack([cplx_block(w1[i][:, :half]) for i in range(n1)])
    f1_real = np.stack([np.concatenate([w1[i].real, w1[i].imag], axis=0) for i in range(n1)])
    w2 = np.exp(-2j * np.pi * (a1[:, None] * a1[None, :]) / n1)
    f2 = cplx_block(w2)
    ang = (a1[None, :, None] * a1[None, None, :] / n1) + (a2[:, None, None] * a1[None, :, None] / n)
    g2 = np.stack([cplx_block(m_) for m_ in np.exp(2j * np.pi * ang)])
    wg1 = np.exp(2j * np.pi * (a2[:half, None] * a2[None, :]) / n2) / n
    g1 = cplx_block(wg1)
    to = lambda x: jnp.asarray(x.astype(np.float32)).astype(BF16)
    return to(f1_pad), to(f1_real), to(f2), to(g2), to(g1)


def _fft_stage1_kernel(x_ref, f_ref, y_ref):
    xr = pltpu.einshape("mjc->jmc", x_ref[0])
    xi = pltpu.einshape("mjc->jmc", x_ref[1])
    for j in range(FFT_CHUNK):
        xc = jnp.concatenate([xr[j], xi[j]], axis=0).astype(BF16)
        y_ref[j] = _pack_bf16_pairs(jnp.dot(f_ref[j], xc, preferred_element_type=F32))


def _fft_stage1_real_kernel(x_ref, f_ref, y_ref):
    x = pltpu.einshape("mjc->jmc", x_ref[...])
    for j in range(FFT_CHUNK):
        y_ref[j] = _pack_bf16_pairs(jnp.dot(f_ref[j], _unpack_bf16_pairs(x[j]),
                                            preferred_element_type=F32))


def _fft_stage2_kernel(yr_ref, yi_ref, f2_ref, h_ref):
    yr = pltpu.einshape("nkc->knc", yr_ref[...])
    yi = pltpu.einshape("nkc->knc", yi_ref[...])
    for j in range(FFT_CHUNK):
        yc = _unpack_bf16_pairs(jnp.concatenate([yr[j], yi[j]], axis=0))
        h_ref[j] = _pack_bf16_pairs(jnp.dot(f2_ref[...], yc, preferred_element_type=F32))


def _fft_mid_kernel(yr_ref, yi_ref, f2_ref, h_ref, g2_ref, q_ref):
    yr = pltpu.einshape("nkc->knc", yr_ref[...])
    yi = pltpu.einshape("nkc->knc", yi_ref[...])
    n1 = FFT_N1
    spec = [jnp.dot(f2_ref[...], _unpack_bf16_pairs(jnp.concatenate([yr[j], yi[j]], axis=0)),
                    preferred_element_type=F32) for j in range(FFT_CHUNK)]
    prod = []
    for j, z in enumerate(spec):
        hf = _unpack_pairs_f32(h_ref[j])
        zr, zi, hr, hi = z[:n1], z[n1:], hf[:n1], hf[n1:]
        prod.append(jnp.concatenate([zr * hr - zi * hi, zr * hi + zi * hr], axis=0).astype(BF16))
    for j, pc in enumerate(prod):
        q_ref[j] = _pack_bf16_pairs(jnp.dot(g2_ref[j], pc, preferred_element_type=F32))


def _fft_last_kernel(qr_ref, qi_ref, g1_ref, gate_ref, zf_ref, fb_ref, o_ref):
    qr = pltpu.einshape("kjc->jkc", qr_ref[...])
    qi = pltpu.einshape("kjc->jkc", qi_ref[...])
    half = FFT_N2 // 2
    conv = []
    for j in range(FFT_CHUNK):
        qc = _unpack_bf16_pairs(jnp.concatenate([qr[j], qi[j]], axis=0))
        conv.append(jnp.dot(g1_ref[...], qc, preferred_element_type=F32))
    conv = jnp.stack(conv, axis=0)
    fb = fb_ref[...]
    for bi in range(2):
        cb = pltpu.einshape("jmc->mjc", conv[:, bi * half:(bi + 1) * half])
        zf = zf_ref[bi]
        o_ref[bi] = gate_ref[bi] * (cb + zf * fb)


def hyena_filter_spectrum(ts, consts, cb=FFT_CB):
    _, f1_real, f2, _, _ = consts
    no, n, c = ts.shape
    c = 2 * c
    n1, n2, ch = FFT_N1, FFT_N2, FFT_CHUNK
    ts4 = ts.reshape(no, n2, n1, c // 2)
    y = pl.pallas_call(
        _fft_stage1_real_kernel,
        out_shape=jax.ShapeDtypeStruct((no, n1, 2 * n2, c // 2), jnp.uint32),
        grid=(no, n1 // ch, c // cb),
        in_specs=[pl.BlockSpec((None, n2, ch, cb // 2), lambda o, j, k: (o, 0, j, k)),
                  pl.BlockSpec((ch, 2 * n2, n2), lambda o, j, k: (j, 0, 0))],
        out_specs=pl.BlockSpec((None, ch, 2 * n2, cb // 2), lambda o, j, k: (o, j, 0, k)),
        compiler_params=_params(("parallel", "parallel", "parallel"), big=True),
        name="fft_filter_stage1",
    )(ts4, f1_real)
    nk = n2 // ch
    return pl.pallas_call(
        _fft_stage2_kernel,
        out_shape=jax.ShapeDtypeStruct((no, n2, 2 * n1, c // 2), jnp.uint32),
        grid=(no, nk, c // cb),
        in_specs=[pl.BlockSpec((None, n1, ch, cb // 2), lambda o, j, k: (o, 0, j, k)),
                  pl.BlockSpec((None, n1, ch, cb // 2), lambda o, j, k: (o, 0, nk + j, k)),
                  pl.BlockSpec((2 * n1, 2 * n1), lambda o, j, k: (0, 0))],
        out_specs=pl.BlockSpec((None, ch, 2 * n1, cb // 2), lambda o, j, k: (o, j, 0, k)),
        compiler_params=_params(("parallel", "parallel", "parallel"), big=True),
        name="fft_filter_stage2",
    )(y, y, f2)


def hyena_long_conv_gate(zsrc, z_col, gate_src, gate_col, hf, order, fbias, consts, cb=FFT_CB):
    f1_pad, _, f2, g2, g1 = consts
    n1, n2, ch = FFT_N1, FFT_N2, FFT_CHUNK
    c = 2 * hf.shape[-1]
    ncb = c // cb
    half = n2 // 2
    y = pl.pallas_call(
        _fft_stage1_kernel,
        out_shape=jax.ShapeDtypeStruct((n1, 2 * n2, c // 2), jnp.uint32),
        grid=(n1 // ch, ncb),
        in_specs=[pl.BlockSpec((2, half, ch, cb), lambda j, k: (0, 0, j, z_col * ncb + k)),
                  pl.BlockSpec((ch, 2 * n2, n2), lambda j, k: (j, 0, 0))],
        out_specs=pl.BlockSpec((ch, 2 * n2, cb // 2), lambda j, k: (j, 0, k)),
        compiler_params=_params(("parallel", "parallel"), big=True),
        name="fft_stage1",
    )(zsrc, f1_pad)
    nk = n2 // ch
    q = pl.pallas_call(
        _fft_mid_kernel,
        out_shape=jax.ShapeDtypeStruct((n2, 2 * n1, c // 2), jnp.uint32),
        grid=(nk, ncb),
        in_specs=[pl.BlockSpec((n1, ch, cb // 2), lambda j, k: (0, j, k)),
                  pl.BlockSpec((n1, ch, cb // 2), lambda j, k: (0, nk + j, k)),
                  pl.BlockSpec((2 * n1, 2 * n1), lambda j, k: (0, 0)),
                  pl.BlockSpec((None, ch, 2 * n1, cb // 2), lambda j, k: (order, j, 0, k)),
                  pl.BlockSpec((ch, 2 * n1, 2 * n1), lambda j, k: (j, 0, 0))],
        out_specs=pl.BlockSpec((ch, 2 * n1, cb // 2), lambda j, k: (j, 0, k)),
        compiler_params=_params(("parallel", "parallel"), big=True),
        name="fft_mid",
    )(y, y, f2, hf, g2)
    nj = n1 // ch
    return pl.pallas_call(
        _fft_last_kernel,
        out_shape=jax.ShapeDtypeStruct((2, half, n1, c), F32),
        grid=(nj, ncb),
        in_specs=[pl.BlockSpec((n2, ch, cb // 2), lambda j, k: (0, j, k)),
                  pl.BlockSpec((n2, ch, cb // 2), lambda j, k: (0, nj + j, k)),
                  pl.BlockSpec((2 * half, 2 * n2), lambda j, k: (0, 0)),
                  pl.BlockSpec((2, half, ch, cb), lambda j, k: (0, 0, j, gate_col * ncb + k)),
                  pl.BlockSpec((2, half, ch, cb), lambda j, k: (0, 0, j, z_col * ncb + k)),
                  pl.BlockSpec((None, 1, cb), lambda j, k: (order, 0, k))],
        out_specs=pl.BlockSpec((2, half, ch, cb), lambda j, k: (0, 0, j, k)),
        compiler_params=_params(("parallel", "parallel"), big=True),
        name="fft_last",
    )(q, q, g1, gate_src, zsrc, fbias.reshape(fbias.shape[0], 1, c))


def hyena_mixer(hn, h_res, w_in, b_in, conv_w, conv_b, fw1, fb1, fw2, fb2, freq, fw3, fbias, w_out,
                batch, seq):
    width = w_out.shape[0]
    assert batch == 2 and 2 * seq == FFT_N1 * FFT_N2
    sc = hyena_in_conv(hn, w_in, b_in, conv_w, conv_b, seq)
    consts = _dft_constants()
    ts = hyena_two_sided_filters(seq, fw1, fb1, fw2, fb2, freq, fw3, width)
    hf = hyena_filter_spectrum(ts, consts)
    sc4 = sc.reshape(batch, FFT_N2 // 2, FFT_N1, 3 * width)
    zf1 = hyena_long_conv_gate(sc4, 2, sc4, 0, hf, 0, fbias, consts)
    zf2 = hyena_long_conv_gate(zf1, 0, sc4, 1, hf, 1, fbias, consts)
    return matmul([zf2.reshape(batch * seq, width)], w_out, res=h_res, tm=512, tn=1024,
                  name="hyena_out")


def moe_swiglu(hn2_packed, logits, wg, wu, wd):
    n = hn2_packed.shape[0]
    top_v, top_i = lax.top_k(logits, TOP_K)
    gates = jax.nn.softmax(top_v, axis=-1)
    e_flat = top_i.reshape(-1).astype(jnp.int32)
    nk = n * TOP_K
    onehot = (e_flat[:, None] == jnp.arange(N_EXPERTS, dtype=jnp.int32)[None]).astype(jnp.int32)
    csum = jnp.cumsum(onehot, axis=0)
    rank = jnp.take_along_axis(csum, e_flat[:, None], axis=1)[:, 0] - 1
    counts = csum[-1]
    padded = ((counts + MOE_TILE - 1) // MOE_TILE) * MOE_TILE
    pad_end = jnp.cumsum(padded)
    pad_start = pad_end - padded
    dest = pad_start[e_flat] + rank
    p_rows = nk + N_EXPERTS * MOE_TILE
    order = jnp.argsort(e_flat, stable=True).astype(jnp.int32)
    sorted_tok = jnp.pad(order // TOP_K, (0, DMA_LOOP_UNROLL))
    start = jnp.cumsum(counts) - counts
    nt = p_rows // MOE_TILE
    tile_start = jnp.arange(nt, dtype=jnp.int32) * MOE_TILE
    tile_used = tile_start < pad_end[-1]
    tile_exp = jnp.minimum(jnp.searchsorted(pad_end, tile_start, side='right'), N_EXPERTS - 1).astype(jnp.int32)
    tile_rows = jnp.clip(counts[tile_exp] - (tile_start - pad_start[tile_exp]), 0, MOE_TILE)
    tile_rows = jnp.where(tile_used, tile_rows, 0).astype(jnp.int32)
    last_exp = tile_exp[jnp.maximum(jnp.sum(tile_used.astype(jnp.int32)) - 1, 0)]
    tile_src = jnp.where(tile_used, start[tile_exp] + (tile_start - pad_start[tile_exp]), 0)
    tile_exp = jnp.where(tile_used, tile_exp, last_exp)
    ys = swiglu(hn2_packed, wg, wu, wd, tile_exp, tile_rows, tile_src.astype(jnp.int32), sorted_tok,
                tf=MOE_F_TILE, sub_rows=FFN_SUB_ROWS, name="swiglu_experts")
    return ys, dest.reshape(n, TOP_K), gates


def kernel(x, p, ln_mix, ln_ffn, ln_ple, final_norm, t5_bias, w_attn_in, w_attn_out, attn_sink, na_rpb, w_ffn_gate, w_ffn_up, w_ffn_down, w_hy_in, b_hy_in, w_hy_conv, b_hy_conv, w_hy_f1, b_hy_f1, w_hy_f2, b_hy_f2, hy_freq, w_hy_f3, hy_bias, w_hy_out, w_router, w_exp_gate, w_exp_up, w_exp_down, w_ple_proj, w_ple_gate):
    batch, seq, d = x.shape
    n = batch * seq
    depth = ln_mix.shape[0]
    h = x.reshape(n, d)
    for i in range(depth):
        li = i // 2
        hn = rmsnorm(h, ln_mix[i], BF16)
        if i % 2 == 0:
            na_off = A_WIDTH + 2 * A_KV_WIDTH
            w_in = w_attn_in[li]
            proj_a = matmul([hn], w_in[:, :na_off], out_dtype=BF16, tn=640, name="attn_in_a")
            proj_n = matmul([hn], w_in[:, na_off:], out_dtype=BF16, name="attn_in_n")
            oa = window_attention(proj_a.reshape(batch, seq, -1), t5_bias, attn_sink[li])
            ob = neighbourhood_attention(proj_n.reshape(batch, seq, -1), na_rpb[li])
            h = matmul([oa.reshape(n, A_WIDTH), ob.reshape(n, B_WIDTH)], w_attn_out[li], res=h,
                       name="attn_out")
            hn2 = rmsnorm(h, ln_ffn[i], BF16)
            nt = n // MOE_TILE
            h = swiglu(hn2, w_ffn_gate, w_ffn_up, w_ffn_down,
                       jnp.full((nt,), li, jnp.int32), jnp.full((nt,), MOE_TILE, jnp.int32),
                       res=h, name="swiglu_dense")
            hn3 = rmsnorm(h, ln_ple[i], BF16)
        else:
            h = hyena_mixer(hn, h, w_hy_in[li], b_hy_in[li], w_hy_conv[li], b_hy_conv[li],
                            w_hy_f1[li], b_hy_f1[li], w_hy_f2[li], b_hy_f2[li], hy_freq[li],
                            w_hy_f3[li], hy_bias[li], w_hy_out[li], batch, seq)
            wr_pad = jnp.pad(w_router[li].astype(F32), ((0, 0), (0, 128 - N_EXPERTS)))
            hn2_packed, logits = rmsnorm_router(h, ln_ffn[i], wr_pad)
            ys, dest2, gates = moe_swiglu(hn2_packed, logits[:, :N_EXPERTS], w_exp_gate[li],
                                          w_exp_up[li], w_exp_down[li])
            h, hn3 = moe_combine_rmsnorm(h, ys, dest2, gates, ln_ple[i])
        h = ple(hn3, p[i].reshape(n, -1), h, w_ple_gate[i], w_ple_proj[i])
    return rmsnorm(h, final_norm, F32).reshape(batch, seq, d)
```

```python
import functools
import math

import jax
import jax.numpy as jnp
import numpy as np
from jax import lax
from jax.experimental import pallas as pl
from jax.experimental.pallas import tpu as pltpu

F32 = jnp.float32
BF16 = jnp.bfloat16
NEG_INF = -1e30
RMS_EPS = 1e-6

V7X_VMEM_LIMIT_BYTES = 56 * 1024 * 1024

HEAD_DIM = 64
A_Q_HEADS = 16
A_KV_HEADS = 2
A_GROUP = A_Q_HEADS // A_KV_HEADS
A_BLOCK = 128
T5_BUCKETS = 32
T5_MAX_DIST = 128
B_HEADS = 16
GRID_W = 64
NA_WIN_H = 8
NA_WIN_W = 16
A_WIDTH = A_Q_HEADS * HEAD_DIM
A_KV_WIDTH = A_KV_HEADS * HEAD_DIM
B_WIDTH = B_HEADS * HEAD_DIM
N_EXPERTS = 8
TOP_K = 2
HYENA_EMB = 33
HYENA_DECAY_TARGET = 1e-2
HYENA_FAST_PCT = 0.3
HYENA_SLOW_PCT = 1.5

FFT_N1 = 64
FFT_N2 = 128
FFT_CHUNK = 8
FFT_CB = 1024

MOE_TILE = 1024
FFN_F_TILE = 256
MOE_F_TILE = 512
FFN_SUB_ROWS = 256
DMA_LOOP_UNROLL = 8


def _params(semantics, big=False):
    return pltpu.CompilerParams(
        dimension_semantics=semantics,
        vmem_limit_bytes=V7X_VMEM_LIMIT_BYTES if big else None)


def _rmsnorm_kernel(x_ref, g_ref, o_ref):
    x = x_ref[...]
    y = x * lax.rsqrt(jnp.mean(x * x, axis=-1, keepdims=True) + RMS_EPS)
    o_ref[...] = (y * g_ref[...]).astype(o_ref.dtype)


def rmsnorm(x2, g, out_dtype, tm=512):
    n, d = x2.shape
    return pl.pallas_call(
        _rmsnorm_kernel,
        out_shape=jax.ShapeDtypeStruct((n, d), out_dtype),
        grid=(n // tm,),
        in_specs=[pl.BlockSpec((tm, d), lambda i: (i, 0)),
                  pl.BlockSpec((1, d), lambda i: (0, 0))],
        out_specs=pl.BlockSpec((tm, d), lambda i: (i, 0)),
        compiler_params=_params(("parallel",)),
        name="rmsnorm",
    )(x2, g.reshape(1, d))


def _pack_bf16_pairs(y):
    w = y.shape[1] // 2
    bits = pltpu.bitcast(y.astype(BF16).astype(F32), jnp.uint32)
    return (bits[:, :w] >> 16) | (bits[:, w:] & jnp.uint32(0xFFFF0000))


def _unpack_pairs_f32(p):
    lo = pltpu.bitcast(p << 16, F32)
    hi = pltpu.bitcast(p & jnp.uint32(0xFFFF0000), F32)
    return jnp.concatenate([lo, hi], axis=1)


def _unpack_bf16_pairs(p):
    return _unpack_pairs_f32(p).astype(BF16)


def _rmsnorm_router_kernel(x_ref, g_ref, wr_ref, o_ref, l_ref):
    x = x_ref[...]
    y = x * lax.rsqrt(jnp.mean(x * x, axis=-1, keepdims=True) + RMS_EPS)
    y = y * g_ref[...]
    o_ref[...] = _pack_bf16_pairs(y)
    l_ref[...] = jnp.dot(y, wr_ref[...], preferred_element_type=F32,
                         precision=lax.Precision.HIGHEST)


def rmsnorm_router(x2, g, w_router_pad, tm=512):
    n, d = x2.shape
    ne = w_router_pad.shape[1]
    return pl.pallas_call(
        _rmsnorm_router_kernel,
        out_shape=(jax.ShapeDtypeStruct((n, d // 2), jnp.uint32), jax.ShapeDtypeStruct((n, ne), F32)),
        grid=(n // tm,),
        in_specs=[pl.BlockSpec((tm, d), lambda i: (i, 0)),
                  pl.BlockSpec((1, d), lambda i: (0, 0)),
                  pl.BlockSpec((d, ne), lambda i: (0, 0))],
        out_specs=(pl.BlockSpec((tm, d // 2), lambda i: (i, 0)),
                   pl.BlockSpec((tm, ne), lambda i: (i, 0))),
        compiler_params=_params(("parallel",)),
        name="rmsnorm_router",
    )(x2, g.reshape(1, d), w_router_pad)


def _mm_kernel(*refs, n_a, has_bias, has_res):
    a_refs = refs[:n_a]
    w_refs = refs[n_a:2 * n_a]
    idx = 2 * n_a
    bias_ref = refs[idx] if has_bias else None
    idx += int(has_bias)
    res_ref = refs[idx] if has_res else None
    idx += int(has_res)
    o_ref = refs[idx]
    wbf_refs = refs[idx + 1:idx + 1 + n_a]

    @pl.when(pl.program_id(1) == 0)
    def _():
        for w_ref, wbf_ref in zip(w_refs, wbf_refs):
            wbf_ref[...] = w_ref[...].astype(BF16)

    acc = None
    for a_ref, wbf_ref in zip(a_refs, wbf_refs):
        d = jnp.dot(a_ref[...].astype(BF16), wbf_ref[...], preferred_element_type=F32)
        acc = d if acc is None else acc + d
    if has_bias:
        acc = acc + bias_ref[...]
    if has_res:
        acc = acc + res_ref[...]
    o_ref[...] = acc.astype(o_ref.dtype)


def matmul(a_list, w, *, bias=None, res=None, out_dtype=F32, tm=1024, tn=1024, name="matmul"):
    m = a_list[0].shape[0]
    n = w.shape[1]
    n_a = len(a_list)
    k_each = a_list[0].shape[1]
    assert all(a.shape == (m, k_each) for a in a_list) and w.shape[0] == n_a * k_each
    in_specs = [pl.BlockSpec((tm, k_each), lambda j, i: (i, 0)) for _ in a_list]
    in_specs += [pl.BlockSpec((k_each, tn), functools.partial(lambda j, i, kb: (kb, j), kb=kb))
                 for kb in range(n_a)]
    args = list(a_list) + [w] * n_a
    if bias is not None:
        in_specs.append(pl.BlockSpec((1, tn), lambda j, i: (0, j)))
        args.append(bias.reshape(1, n))
    if res is not None:
        in_specs.append(pl.BlockSpec((tm, tn), lambda j, i: (i, j)))
        args.append(res)
    return pl.pallas_call(
        functools.partial(_mm_kernel, n_a=n_a, has_bias=bias is not None, has_res=res is not None),
        out_shape=jax.ShapeDtypeStruct((m, n), out_dtype),
        grid=(n // tn, m // tm),
        in_specs=in_specs,
        out_specs=pl.BlockSpec((tm, tn), lambda j, i: (i, j)),
        scratch_shapes=[pltpu.VMEM((k_each, tn), BF16) for _ in a_list],
        compiler_params=_params(("parallel", "arbitrary"), big=True),
        name=name,
    )(*args)


def _rms_scale_bf16(x, g):
    y = x * lax.rsqrt(jnp.mean(x * x, axis=-1, keepdims=True) + RMS_EPS)
    return (y * g).astype(BF16)


def _ple_kernel(h_ref, p_ref, g_ref, wg_ref, wp_ref, o_ref, wg_bf, wp_bf):
    @pl.when(pl.program_id(1) == 0)
    def _():
        wg_bf[...] = wg_ref[...].astype(BF16)
        wp_bf[...] = wp_ref[...].astype(BF16)

    tn = o_ref.shape[1]
    a = jnp.dot(_rms_scale_bf16(h_ref[...], g_ref[...]), wg_bf[...], preferred_element_type=F32)
    pp = jnp.dot(p_ref[...].astype(BF16), wp_bf[...], preferred_element_type=F32)
    col = pl.multiple_of(pl.program_id(0) * tn, tn)
    o_ref[...] = h_ref[:, pl.ds(col, tn)] + jax.nn.sigmoid(a) * pp


def ple(h2, p2, g, w_gate, w_proj, tm=512, tn=1024):
    m, d = h2.shape
    pd = p2.shape[1]
    return pl.pallas_call(
        _ple_kernel,
        out_shape=jax.ShapeDtypeStruct((m, d), F32),
        grid=(d // tn, m // tm),
        in_specs=[pl.BlockSpec((tm, d), lambda j, i: (i, 0)),
                  pl.BlockSpec((tm, pd), lambda j, i: (i, 0)),
                  pl.BlockSpec((1, d), lambda j, i: (0, 0)),
                  pl.BlockSpec((d, tn), lambda j, i: (0, j)),
                  pl.BlockSpec((pd, tn), lambda j, i: (0, j))],
        out_specs=pl.BlockSpec((tm, tn), lambda j, i: (i, j)),
        scratch_shapes=[pltpu.VMEM((d, tn), BF16), pltpu.VMEM((pd, tn), BF16)],
        compiler_params=_params(("parallel", "arbitrary"), big=True),
        name="ple",
    )(h2, p2, g.reshape(1, d), w_gate, w_proj)


def _row_copy(src_hbm, row, dst, i, sem):
    return pltpu.make_async_copy(src_hbm.at[pl.ds(row, 1), :], dst.at[pl.ds(i, 1), :], sem)


def _swiglu_kernel(exp_ref, rows_ref, src_ref, tok_ref, x_ref, *rest, n_col, gather, sub_rows):
    del exp_ref
    gain_ref = None if gather else rest[0]
    wg_ref, wu_ref, wd_ref, o_ref = rest[0 if gather else 1:][:4]
    scratch = rest[(4 if gather else 5):]
    t = pl.program_id(0)
    f = pl.program_id(1)
    nt = pl.num_programs(0)
    n_rows = rows_ref[t]
    used = n_rows > 0
    tm = o_ref.shape[0]

    if not gather:
        (xbf,) = scratch

        @pl.when(f == 0)
        def _():
            h = x_ref[...]
            o_ref[...] = h
            xbf[...] = _rms_scale_bf16(h, gain_ref[...])

    if gather:
        @pl.when(f == 0)
        def _():
            o_ref[...] = jnp.zeros_like(o_ref)

        xbuf, xbf, sems = scratch
        slot = t % 2

        def for_each_row(tile, fn):
            def body(c, carry):
                for r in range(DMA_LOOP_UNROLL):
                    fn(c * DMA_LOOP_UNROLL + r)
                return carry
            lax.fori_loop(0, (rows_ref[tile] + DMA_LOOP_UNROLL - 1) // DMA_LOOP_UNROLL, body, 0)

        def start_tile(tile, sl):
            for_each_row(tile, lambda i: _row_copy(x_ref, tok_ref[src_ref[tile] + i], xbuf.at[sl], i,
                                                   sems.at[sl, i]).start())

        @pl.when(jnp.logical_and(f == 0, t == 0))
        def _():
            xbuf[...] = jnp.zeros_like(xbuf)
            start_tile(0, 0)

        @pl.when(jnp.logical_and(f == 0, used))
        def _():
            for_each_row(t, lambda i: _row_copy(x_ref, 0, xbuf.at[slot], i, sems.at[slot, i]).wait())
            xbf[...] = _unpack_bf16_pairs(xbuf[slot])

        @pl.when(jnp.logical_and(f == 1, t + 1 < nt))
        def _():
            start_tile(jnp.minimum(t + 1, nt - 1), 1 - slot)

    def ffn_rows(n):
        x = xbf[0:n, :]
        g = jnp.dot(x, wg_ref[...].astype(BF16), preferred_element_type=F32)
        u = jnp.dot(x, wu_ref[...].astype(BF16), preferred_element_type=F32)
        hmid = (g * jax.nn.sigmoid(g) * u).astype(BF16)
        wd = wd_ref[...].astype(BF16)
        cw = o_ref.shape[1] // n_col
        for c in range(n_col):
            y = jnp.dot(hmid, wd[:, c * cw:(c + 1) * cw], preferred_element_type=F32)
            o_ref[0:n, c * cw:(c + 1) * cw] += y

    for k in range(1, tm // sub_rows + 1):
        covers = jnp.logical_and(n_rows > (k - 1) * sub_rows, n_rows <= k * sub_rows)
        pl.when(covers)(functools.partial(ffn_rows, k * sub_rows))


def swiglu(x, w_gate, w_up, w_down, tile_exp, tile_rows, tile_src=None, src_rows=None, *, gain=None,
           tm=MOE_TILE, tf=FFN_F_TILE, sub_rows=MOE_TILE, name="swiglu"):
    gather = src_rows is not None
    assert gather != (gain is not None)
    m = tile_rows.shape[0] * tm
    d = w_gate.shape[1]
    dff = w_gate.shape[2]
    nf = dff // tf
    nt = m // tm
    assert nf >= 2

    def w_in_map(t, f, exp_ref, rows_ref, src_ref, tok_ref):
        return (exp_ref[t], 0, jnp.where(rows_ref[t] > 0, f, nf - 1))

    def w_out_map(t, f, exp_ref, rows_ref, src_ref, tok_ref):
        return (exp_ref[t], jnp.where(rows_ref[t] > 0, f, nf - 1), 0)

    def tile_map(t, f, exp_ref, rows_ref, src_ref, tok_ref):
        return (t, 0)

    w_specs = [pl.BlockSpec((None, d, tf), w_in_map),
               pl.BlockSpec((None, d, tf), w_in_map),
               pl.BlockSpec((None, tf, d), w_out_map)]
    if gather:
        in_specs = [pl.BlockSpec(memory_space=pl.ANY)] + w_specs
        args = [x, w_gate, w_up, w_down]
        scratch = [pltpu.VMEM((2, tm, d // 2), jnp.uint32), pltpu.VMEM((tm, d), BF16),
                   pltpu.SemaphoreType.DMA((2, tm))]
    else:
        in_specs = [pl.BlockSpec((tm, d), tile_map, pipeline_mode=pl.Buffered(1)),
                    pl.BlockSpec((1, d), lambda t, f, e, r, s, k: (0, 0))] + w_specs
        args = [x, gain.reshape(1, d), w_gate, w_up, w_down]
        scratch = [pltpu.VMEM((tm, d), BF16)]
        tile_src = jnp.zeros((nt,), jnp.int32)
        src_rows = jnp.zeros((1,), jnp.int32)
    grid_spec = pltpu.PrefetchScalarGridSpec(
        num_scalar_prefetch=4,
        grid=(nt, nf),
        in_specs=in_specs,
        out_specs=pl.BlockSpec((tm, d), tile_map, pipeline_mode=pl.Buffered(1) if gather else None),
        scratch_shapes=scratch,
    )
    return pl.pallas_call(
        functools.partial(_swiglu_kernel, n_col=4, gather=gather, sub_rows=sub_rows),
        out_shape=jax.ShapeDtypeStruct((m, d), F32),
        grid_spec=grid_spec,
        compiler_params=_params(("arbitrary", "arbitrary"), big=True),
        name=name,
    )(tile_exp, tile_rows, tile_src, src_rows, *args)


COMBINE_TILE = 256


def _combine_kernel(d0_ref, d1_ref, ys_hbm, gate_ref, h_ref, hout_ref, buf, sems):
    i = pl.program_id(0)
    n = pl.num_programs(0)
    tt = h_ref.shape[0]
    slot = i % 2
    dests = (d0_ref, d1_ref)

    def start_tile(tile, sl):
        def body(r, c):
            for k, d_ref in enumerate(dests):
                _row_copy(ys_hbm, d_ref[tile * tt + r], buf.at[sl, k], r, sems.at[sl, k, r]).start()
            return c
        lax.fori_loop(0, tt, body, 0, unroll=DMA_LOOP_UNROLL)

    @pl.when(i == 0)
    def _():
        start_tile(0, 0)

    @pl.when(i + 1 < n)
    def _():
        start_tile(i + 1, 1 - slot)

    def wait_body(r, c):
        for k in range(TOP_K):
            _row_copy(ys_hbm, 0, buf.at[slot, k], r, sems.at[slot, k, r]).wait()
        return c
    lax.fori_loop(0, tt, wait_body, 0, unroll=DMA_LOOP_UNROLL)

    gates = gate_ref[...]
    hout_ref[...] = h_ref[...] + (buf[slot, 0] * gates[:, 0:1] + buf[slot, 1] * gates[:, 1:2])


def moe_combine(h2, ys, dest2, gates, tt=COMBINE_TILE):
    n, d = h2.shape
    grid_spec = pltpu.PrefetchScalarGridSpec(
        num_scalar_prefetch=2,
        grid=(n // tt,),
        in_specs=[pl.BlockSpec(memory_space=pl.ANY),
                  pl.BlockSpec((tt, TOP_K), lambda i, a, b: (i, 0)),
                  pl.BlockSpec((tt, d), lambda i, a, b: (i, 0))],
        out_specs=pl.BlockSpec((tt, d), lambda i, a, b: (i, 0)),
        scratch_shapes=[pltpu.VMEM((2, TOP_K, tt, d), F32), pltpu.SemaphoreType.DMA((2, TOP_K, tt))],
    )
    return pl.pallas_call(
        _combine_kernel,
        out_shape=jax.ShapeDtypeStruct((n, d), F32),
        grid_spec=grid_spec,
        compiler_params=_params(("arbitrary",), big=True),
        name="moe_combine",
    )(dest2[:, 0], dest2[:, 1], ys, gates, h2)


def _t5_bucket(rel):
    half = T5_BUCKETS // 2
    max_exact = half // 2
    n = jnp.abs(rel)
    log_ratio = jnp.log(jnp.maximum(n, 1).astype(F32) / max_exact) / math.log(T5_MAX_DIST / max_exact)
    large = jnp.minimum(max_exact + (log_ratio * (half - max_exact)).astype(jnp.int32), half - 1)
    return jnp.where(rel > 0, half, 0) + jnp.where(n < max_exact, n, large)


def _window_bias_table(t5_bias):
    i = jnp.arange(A_BLOCK)[:, None]
    j = jnp.arange(3 * A_BLOCK)[None, :]
    rel = j - A_BLOCK - i
    onehot = (_t5_bucket(rel)[None] == jnp.arange(T5_BUCKETS)[:, None, None]).astype(F32)
    bias = jnp.einsum('bh,bij->hij', t5_bias.astype(F32), onehot, precision=lax.Precision.HIGHEST)
    return jnp.where((jnp.abs(rel) <= A_BLOCK)[None], bias, NEG_INF)


def _window_kernel(sink_ref, q_ref, kv_ref, bias_ref, o_ref, *, nb):
    n = pl.program_id(1)
    scale = HEAD_DIM ** -0.5
    starts = (jnp.maximum(n - 1, 0), n, jnp.minimum(n + 1, nb - 1))
    kv = [kv_ref[pl.ds(pl.multiple_of(s * A_BLOCK, A_BLOCK), A_BLOCK), :] for s in starts]
    col = lax.broadcasted_iota(jnp.int32, (A_BLOCK, 3 * A_BLOCK), 1)
    edge_ok = jnp.logical_and(jnp.logical_or(n > 0, col >= A_BLOCK),
                              jnp.logical_or(n < nb - 1, col < 2 * A_BLOCK))
    kv = jnp.concatenate(kv, axis=0)
    q = q_ref[...]
    lo = lax.broadcasted_iota(jnp.int32, (1, 2 * HEAD_DIM), 1) < HEAD_DIM
    sel = (jnp.where(lo, scale, 0.0).astype(BF16), jnp.where(lo, 0.0, scale).astype(BF16))
    pairs_per_kv = A_GROUP // 2
    kk, vv = [], []
    for kh in range(A_KV_HEADS):
        k1 = kv[:, kh * HEAD_DIM:(kh + 1) * HEAD_DIM]
        v1 = kv[:, A_KV_WIDTH + kh * HEAD_DIM:A_KV_WIDTH + (kh + 1) * HEAD_DIM]
        kk.append(jnp.concatenate([k1, k1], axis=1))
        vv.append(jnp.concatenate([v1, v1], axis=1))
    scores = []
    for pr in range(A_Q_HEADS // 2):
        q2 = q[:, pr * 2 * HEAD_DIM:(pr + 1) * 2 * HEAD_DIM]
        qq = jnp.concatenate([q2 * sel[0], q2 * sel[1]], axis=0)
        s2 = lax.dot_general(qq, kk[pr // pairs_per_kv], (((1,), (1,)), ((), ())),
                             preferred_element_type=F32)
        for i in range(2):
            scores.append(jnp.where(edge_ok, s2[i * A_BLOCK:(i + 1) * A_BLOCK] + bias_ref[2 * pr + i],
                                    NEG_INF))
    probs, dens = [], []
    for h, s in enumerate(scores):
        sink = sink_ref[h]
        mx = jnp.maximum(jnp.max(s, axis=-1, keepdims=True), sink)
        p = jnp.exp(s - mx)
        dens.append(jnp.sum(p, axis=-1, keepdims=True) + jnp.exp(sink - mx))
        probs.append(p.astype(BF16))
    for pr in range(A_Q_HEADS // 2):
        v2 = vv[pr // pairs_per_kv]
        oa = jnp.dot(probs[2 * pr], v2, preferred_element_type=F32) / dens[2 * pr]
        ob = jnp.dot(probs[2 * pr + 1], v2, preferred_element_type=F32) / dens[2 * pr + 1]
        o_ref[:, pr * 2 * HEAD_DIM:(pr + 1) * 2 * HEAD_DIM] = jnp.where(lo, oa, ob).astype(o_ref.dtype)


def window_attention(proj_a, t5_bias, sink):
    b, s, _ = proj_a.shape
    nb = s // A_BLOCK
    kvw = 2 * A_KV_WIDTH
    grid_spec = pltpu.PrefetchScalarGridSpec(
        num_scalar_prefetch=0,
        grid=(b, nb),
        in_specs=[pl.BlockSpec(memory_space=pltpu.SMEM),
                  pl.BlockSpec((None, A_BLOCK, A_WIDTH), lambda bi, n: (bi, n, 0)),
                  pl.BlockSpec((None, s, kvw), lambda bi, n: (bi, 0, A_WIDTH // kvw)),
                  pl.BlockSpec((A_Q_HEADS, A_BLOCK, 3 * A_BLOCK), lambda bi, n: (0, 0, 0))],
        out_specs=pl.BlockSpec((None, A_BLOCK, A_WIDTH), lambda bi, n: (bi, n, 0)),
    )
    return pl.pallas_call(
        functools.partial(_window_kernel, nb=nb),
        out_shape=jax.ShapeDtypeStruct((b, s, A_WIDTH), BF16),
        grid_spec=grid_spec,
        compiler_params=_params(("parallel", "arbitrary")),
        name="window_attention",
    )(sink.astype(F32), proj_a, proj_a, _window_bias_table(t5_bias))


NA_HEAD_GROUP = 16


def _na_bias_table(rpb, rows):
    kh = min(NA_WIN_H, rows)
    kw = NA_WIN_W
    c = jnp.arange(GRID_W)
    cs = jnp.clip(c - kw // 2, 0, GRID_W - kw)
    col_ok = (c[None] >= cs[:, None]) & (c[None] < cs[:, None] + kw)
    col_off = jnp.clip(c[None] - c[:, None], -(kw - 1), kw - 1) + kw - 1
    onehot = (col_off[None] == jnp.arange(2 * kw - 1)[:, None, None]).astype(F32)
    by_col = jnp.einsum('hrc,cqk->hrqk', rpb.astype(F32), onehot, precision=lax.Precision.HIGHEST)
    by_col = jnp.where(col_ok[None, None], by_col, NEG_INF)
    tabs = [by_col[:, NA_WIN_H - 1 - d:NA_WIN_H - 1 - d + kh] for d in range(kh)]
    bias = jnp.stack(tabs, axis=0).transpose(0, 1, 3, 2, 4)
    return bias.reshape(kh, rpb.shape[0], GRID_W, kh * GRID_W)


def _na_kernel(q_ref, k_ref, v_ref, bias_ref, o_ref, *, rows):
    r = pl.program_id(2)
    kh = min(NA_WIN_H, rows)
    scale = HEAD_DIM ** -0.5
    start = pl.multiple_of(jnp.clip(r - kh // 2, 0, rows - kh) * GRID_W, GRID_W)
    k = k_ref[pl.ds(start, kh * GRID_W), :]
    v = v_ref[pl.ds(start, kh * GRID_W), :]
    q = q_ref[...]
    lo = lax.broadcasted_iota(jnp.int32, (1, 2 * HEAD_DIM), 1) < HEAD_DIM
    sel = (jnp.where(lo, scale, 0.0).astype(BF16), jnp.where(lo, 0.0, scale).astype(BF16))
    n_pairs = NA_HEAD_GROUP // 2
    scores = []
    for pr in range(n_pairs):
        sl = slice(pr * 2 * HEAD_DIM, (pr + 1) * 2 * HEAD_DIM)
        q2 = q[:, sl]
        qq = jnp.concatenate([q2 * sel[0], q2 * sel[1]], axis=0)
        s2 = lax.dot_general(qq, k[:, sl], (((1,), (1,)), ((), ())), preferred_element_type=F32)
        for i in range(2):
            scores.append(s2[i * GRID_W:(i + 1) * GRID_W] + bias_ref[2 * pr + i])
    probs, dens = [], []
    for s in scores:
        mx = jnp.max(s, axis=-1, keepdims=True)
        p = jnp.exp(s - mx)
        dens.append(jnp.sum(p, axis=-1, keepdims=True))
        probs.append(p.astype(BF16))
    for pr in range(n_pairs):
        sl = slice(pr * 2 * HEAD_DIM, (pr + 1) * 2 * HEAD_DIM)
        oa = jnp.dot(probs[2 * pr], v[:, sl], preferred_element_type=F32) / dens[2 * pr]
        ob = jnp.dot(probs[2 * pr + 1], v[:, sl], preferred_element_type=F32) / dens[2 * pr + 1]
        o_ref[:, sl] = jnp.where(lo, oa, ob).astype(o_ref.dtype)


def neighbourhood_attention(proj_n, rpb):
    b, s, _ = proj_n.shape
    rows = s // GRID_W
    kh = min(NA_WIN_H, rows)
    gw = NA_HEAD_GROUP * HEAD_DIM
    ng = B_WIDTH // gw

    def bias_map(bi, g, r):
        return (r - jnp.clip(r - kh // 2, 0, rows - kh), g, 0, 0)

    return pl.pallas_call(
        functools.partial(_na_kernel, rows=rows),
        out_shape=jax.ShapeDtypeStruct((b, s, B_WIDTH), BF16),
        grid=(b, ng, rows),
        in_specs=[pl.BlockSpec((None, GRID_W, gw), lambda bi, g, r: (bi, r, g)),
                  pl.BlockSpec((None, s, gw), lambda bi, g, r: (bi, 0, ng + g)),
                  pl.BlockSpec((None, s, gw), lambda bi, g, r: (bi, 0, 2 * ng + g)),
                  pl.BlockSpec((None, NA_HEAD_GROUP, GRID_W, kh * GRID_W), bias_map)],
        out_specs=pl.BlockSpec((None, GRID_W, gw), lambda bi, g, r: (bi, r, g)),
        compiler_params=_params(("parallel", "parallel", "arbitrary"), big=True),
        name="neighbourhood_attention",
    )(proj_n, proj_n, proj_n, _na_bias_table(rpb, rows))


CONV_HALO = 16
CONV_COL_CHUNK = 256


def _hyena_in_kernel(prev_ref, cur_ref, next_ref, w_ref, b_ref, cw_ref, cb_ref, o_ref, wbf_ref, *,
                     tiles_per_seq):
    i = pl.program_id(1)

    @pl.when(i == 0)
    def _():
        wbf_ref[...] = w_ref[...].astype(BF16)

    tm = cur_ref.shape[0]
    rows = tm + 2 * CONV_HALO
    a = jnp.concatenate([prev_ref[...], cur_ref[...], next_ref[...]], axis=0)
    pos = i % tiles_per_seq
    row = lax.broadcasted_iota(jnp.int32, (rows, 1), 0)
    kill_up = jnp.logical_and(pos == 0, row == CONV_HALO)
    kill_dn = jnp.logical_and(pos == tiles_per_seq - 1, row == CONV_HALO + tm - 1)
    for c in range(o_ref.shape[1] // CONV_COL_CHUNK):
        cs = slice(c * CONV_COL_CHUNK, (c + 1) * CONV_COL_CHUNK)
        p = jnp.dot(a, wbf_ref[:, cs], preferred_element_type=F32) + b_ref[:, cs]
        up = jnp.where(kill_up, 0.0, pltpu.roll(p, 1, 0))
        dn = jnp.where(kill_dn, 0.0, pltpu.roll(p, rows - 1, 0))
        y = up * cw_ref[0:1, cs] + p * cw_ref[1:2, cs] + dn * cw_ref[2:3, cs] + cb_ref[:, cs]
        o_ref[:, cs] = y[CONV_HALO:CONV_HALO + tm]


def hyena_in_conv(hn, w_in, b_in, conv_w, conv_b, seq, tm=1024, tn=1024):
    m, k = hn.shape
    n = w_in.shape[1]
    hb = tm // CONV_HALO
    last_halo = m // CONV_HALO - 1
    return pl.pallas_call(
        functools.partial(_hyena_in_kernel, tiles_per_seq=seq // tm),
        out_shape=jax.ShapeDtypeStruct((m, n), F32),
        grid=(n // tn, m // tm),
        in_specs=[pl.BlockSpec((CONV_HALO, k), lambda j, i: (jnp.maximum(i * hb - 1, 0), 0)),
                  pl.BlockSpec((tm, k), lambda j, i: (i, 0)),
                  pl.BlockSpec((CONV_HALO, k), lambda j, i: (jnp.minimum((i + 1) * hb, last_halo), 0)),
                  pl.BlockSpec((k, tn), lambda j, i: (0, j)),
                  pl.BlockSpec((1, tn), lambda j, i: (0, j)),
                  pl.BlockSpec((3, tn), lambda j, i: (0, j)),
                  pl.BlockSpec((1, tn), lambda j, i: (0, j))],
        out_specs=pl.BlockSpec((tm, tn), lambda j, i: (i, j)),
        scratch_shapes=[pltpu.VMEM((k, tn), BF16)],
        compiler_params=_params(("parallel", "arbitrary"), big=True),
        name="hyena_in_conv",
    )(hn, hn, hn, w_in, b_in.reshape(1, n), conv_w, conv_b.reshape(1, n))


def _filter_kernel(z_ref, t_ref, fw1_ref, fb1_ref, fw2_ref, fb2_ref, fr_ref, w3_ref, w3b_ref,
                   delta_ref, o_ref, *, seq):
    m = pl.program_id(0)
    hi = lax.Precision.HIGHEST
    fr = fr_ref[...]
    hid = jnp.sin(fr * (jnp.dot(z_ref[...], fw1_ref[...], preferred_element_type=F32, precision=hi)
                        + fb1_ref[...]))
    hid = jnp.sin(fr * (jnp.dot(hid, fw2_ref[...], preferred_element_type=F32, precision=hi)
                        + fb2_ref[...]))
    hid_bf = hid.astype(BF16)
    h = jnp.dot(hid_bf, w3_ref[...].astype(BF16), preferred_element_type=F32)
    tm = h.shape[0]
    width = delta_ref.shape[1]
    row = lax.broadcasted_iota(jnp.int32, (tm, width), 0) + m * tm
    decay = jnp.where(row == seq, 0.0, jnp.exp(-t_ref[...] * delta_ref[...]))
    def emit(taps):
        for o in range(o_ref.shape[0]):
            for kb in range(width // FFT_CB):
                cols = slice(kb * FFT_CB, (kb + 1) * FFT_CB)
                blk = taps[:, o * width:(o + 1) * width][:, cols] * decay[:, cols]
                o_ref[o, :, kb * (FFT_CB // 2):(kb + 1) * (FFT_CB // 2)] = _pack_bf16_pairs(blk)

    @pl.when(m != 0)
    def _():
        emit(h)

    @pl.when(m == 0)
    def _():
        h_back = jnp.dot(hid_bf, w3b_ref[...].astype(BF16), preferred_element_type=F32)
        first = lax.broadcasted_iota(jnp.int32, h.shape, 0) == 0
        emit(h + jnp.where(first, h_back, 0.0))


def hyena_two_sided_filters(seq, fw1, fb1, fw2, fb2, freq, fw3, width, tm=512):
    n = 2 * seq
    u = jnp.arange(n)
    pos = jnp.where(u <= seq, u, n - u)
    pos = jnp.where(u == seq, 0, pos)
    bands = (HYENA_EMB - 1) // 2
    f = jnp.linspace(1e-4, bands - 1, bands, dtype=F32)[None]
    t = (pos.astype(F32) / (seq - 1))[:, None]
    w = ((2.0 * math.pi / seq) * pos.astype(F32))[:, None]
    z = jnp.concatenate([t, jnp.cos(f * w), -jnp.sin(f * w)], axis=-1)
    emb_pad = 40
    z = jnp.pad(z, ((0, 0), (0, emb_pad - HYENA_EMB)))
    fw1p = jnp.pad(fw1.astype(F32), ((0, emb_pad - HYENA_EMB), (0, 0)))
    fh = fw1.shape[1]
    max_decay = math.log(HYENA_DECAY_TARGET) / HYENA_FAST_PCT
    min_decay = math.log(HYENA_DECAY_TARGET) / HYENA_SLOW_PCT
    deltas = jnp.abs(jnp.linspace(min_decay, max_decay, width, dtype=F32))[None]
    half_tiles = seq // tm
    orders = fw3.shape[1] // (2 * width)
    ow = orders * width

    return pl.pallas_call(
        functools.partial(_filter_kernel, seq=seq),
        out_shape=jax.ShapeDtypeStruct((orders, n, width // 2), jnp.uint32),
        grid=(n // tm,),
        in_specs=[pl.BlockSpec((tm, emb_pad), lambda m: (m, 0)),
                  pl.BlockSpec((tm, 1), lambda m: (m, 0)),
                  pl.BlockSpec((emb_pad, fh), lambda m: (0, 0)),
                  pl.BlockSpec((1, fh), lambda m: (0, 0)),
                  pl.BlockSpec((fh, fh), lambda m: (0, 0)),
                  pl.BlockSpec((1, fh), lambda m: (0, 0)),
                  pl.BlockSpec((1, fh), lambda m: (0, 0)),
                  pl.BlockSpec((fh, ow), lambda m: (0, jnp.where(m >= half_tiles, 1, 0))),
                  pl.BlockSpec((fh, ow), lambda m: (0, 1)),
                  pl.BlockSpec((1, width), lambda m: (0, 0))],
        out_specs=pl.BlockSpec((orders, tm, width // 2), lambda m: (0, m, 0)),
        compiler_params=_params(("arbitrary",), big=True),
        name="hyena_filters",
    )(z, t, fw1p, fb1.reshape(1, fh).astype(F32), fw2.astype(F32), fb2.reshape(1, fh).astype(F32),
      freq.reshape(1, fh).astype(F32), fw3, fw3, deltas)


def _dft_constants():
    n1, n2 = FFT_N1, FFT_N2
    n = n1 * n2
    a1 = np.arange(n1)
    a2 = np.arange(n2)
    half = n2 // 2

    def cplx_block(w):
        return np.block([[w.real, -w.imag], [w.imag, w.real]])

    ang = (a2[None, :, None] * a2[None, None, :] / n2) + (a1[:, None, None] * a2[None, :, None] / n)
    w1 = np.exp(-2j * np.pi * ang)
    f1_pad = np.stack([cplx_block(w1[i][:, :half]) for i in range(n1)])
    f1_real = np.stack([np.concatenate([w1[i].real, w1[i].imag], axis=0) for i in range(n1)])
    w2 = np.exp(-2j * np.pi * (a1[:, None] * a1[None, :]) / n1)
    f2 = cplx_block(w2)
    ang = (a1[None, :, None] * a1[None, None, :] / n1) + (a2[:, None, None] * a1[None, :, None] / n)
    g2 = np.stack([cplx_block(m_) for m_ in np.exp(2j * np.pi * ang)])
    wg1 = np.exp(2j * np.pi * (a2[:half, None] * a2[None, :]) / n2) / n
    g1 = cplx_block(wg1)
    to = lambda x: jnp.asarray(x.astype(np.float32)).astype(BF16)
    return to(f1_pad), to(f1_real), to(f2), to(g2), to(g1)


def _fft_stage1_kernel(x_ref, f_ref, y_ref):
    xr = pltpu.einshape("mjc->jmc", x_ref[0])
    xi = pltpu.einshape("mjc->jmc", x_ref[1])
    for j in range(FFT_CHUNK):
        xc = jnp.concatenate([xr[j], xi[j]], axis=0).astype(BF16)
        y_ref[j] = _pack_bf16_pairs(jnp.dot(f_ref[j], xc, preferred_element_type=F32))


def _fft_stage1_real_kernel(x_ref, f_ref, y_ref):
    x = pltpu.einshape("mjc->jmc", x_ref[...])
    for j in range(FFT_CHUNK):
        y_ref[j] = _pack_bf16_pairs(jnp.dot(f_ref[j], _unpack_bf16_pairs(x[j]),
                                            preferred_element_type=F32))


def _fft_stage2_kernel(yr_ref, yi_ref, f2_ref, h_ref):
    yr = pltpu.einshape("nkc->knc", yr_ref[...])
    yi = pltpu.einshape("nkc->knc", yi_ref[...])
    for j in range(FFT_CHUNK):
        yc = _unpack_bf16_pairs(jnp.concatenate([yr[j], yi[j]], axis=0))
        h_ref[j] = _pack_bf16_pairs(jnp.dot(f2_ref[...], yc, preferred_element_type=F32))


def _fft_mid_kernel(yr_ref, yi_ref, f2_ref, h_ref, g2_ref, q_ref):
    yr = pltpu.einshape("nkc->knc", yr_ref[...])
    yi = pltpu.einshape("nkc->knc", yi_ref[...])
    n1 = FFT_N1
    spec = [jnp.dot(f2_ref[...], _unpack_bf16_pairs(jnp.concatenate([yr[j], yi[j]], axis=0)),
                    preferred_element_type=F32) for j in range(FFT_CHUNK)]
    prod = []
    for j, z in enumerate(spec):
        hf = _unpack_pairs_f32(h_ref[j])
        zr, zi, hr, hi = z[:n1], z[n1:], hf[:n1], hf[n1:]
        prod.append(jnp.concatenate([zr * hr - zi * hi, zr * hi + zi * hr], axis=0).astype(BF16))
    for j, pc in enumerate(prod):
        q_ref[j] = _pack_bf16_pairs(jnp.dot(g2_ref[j], pc, preferred_element_type=F32))


def _fft_last_kernel(qr_ref, qi_ref, g1_ref, gate_ref, zf_ref, fb_ref, o_ref):
    qr = pltpu.einshape("kjc->jkc", qr_ref[...])
    qi = pltpu.einshape("kjc->jkc", qi_ref[...])
    half = FFT_N2 // 2
    conv = []
    for j in range(FFT_CHUNK):
        qc = _unpack_bf16_pairs(jnp.concatenate([qr[j], qi[j]], axis=0))
        conv.append(jnp.dot(g1_ref[...], qc, preferred_element_type=F32))
    conv = jnp.stack(conv, axis=0)
    fb = fb_ref[...]
    for bi in range(2):
        cb = pltpu.einshape("jmc->mjc", conv[:, bi * half:(bi + 1) * half])
        zf = zf_ref[bi]
        o_ref[bi] = gate_ref[bi] * (cb + zf * fb)


def hyena_filter_spectrum(ts, consts, cb=FFT_CB):
    _, f1_real, f2, _, _ = consts
    no, n, c = ts.shape
    c = 2 * c
    n1, n2, ch = FFT_N1, FFT_N2, FFT_CHUNK
    ts4 = ts.reshape(no, n2, n1, c // 2)
    y = pl.pallas_call(
        _fft_stage1_real_kernel,
        out_shape=jax.ShapeDtypeStruct((no, n1, 2 * n2, c // 2), jnp.uint32),
        grid=(no, n1 // ch, c // cb),
        in_specs=[pl.BlockSpec((None, n2, ch, cb // 2), lambda o, j, k: (o, 0, j, k)),
                  pl.BlockSpec((ch, 2 * n2, n2), lambda o, j, k: (j, 0, 0))],
        out_specs=pl.BlockSpec((None, ch, 2 * n2, cb // 2), lambda o, j, k: (o, j, 0, k)),
        compiler_params=_params(("parallel", "parallel", "parallel"), big=True),
        name="fft_filter_stage1",
    )(ts4, f1_real)
    nk = n2 // ch
    return pl.pallas_call(
        _fft_stage2_kernel,
        out_shape=jax.ShapeDtypeStruct((no, n2, 2 * n1, c // 2), jnp.uint32),
        grid=(no, nk, c // cb),
        in_specs=[pl.BlockSpec((None, n1, ch, cb // 2), lambda o, j, k: (o, 0, j, k)),
                  pl.BlockSpec((None, n1, ch, cb // 2), lambda o, j, k: (o, 0, nk + j, k)),
                  pl.BlockSpec((2 * n1, 2 * n1), lambda o, j, k: (0, 0))],
        out_specs=pl.BlockSpec((None, ch, 2 * n1, cb // 2), lambda o, j, k: (o, j, 0, k)),
        compiler_params=_params(("parallel", "parallel", "parallel"), big=True),
        name="fft_filter_stage2",
    )(y, y, f2)


def hyena_long_conv_gate(zsrc, z_col, gate_src, gate_col, hf, order, fbias, consts, cb=FFT_CB):
    f1_pad, _, f2, g2, g1 = consts
    n1, n2, ch = FFT_N1, FFT_N2, FFT_CHUNK
    c = 2 * hf.shape[-1]
    ncb = c // cb
    half = n2 // 2
    y = pl.pallas_call(
        _fft_stage1_kernel,
        out_shape=jax.ShapeDtypeStruct((n1, 2 * n2, c // 2), jnp.uint32),
        grid=(n1 // ch, ncb),
        in_specs=[pl.BlockSpec((2, half, ch, cb), lambda j, k: (0, 0, j, z_col * ncb + k)),
                  pl.BlockSpec((ch, 2 * n2, n2), lambda j, k: (j, 0, 0))],
        out_specs=pl.BlockSpec((ch, 2 * n2, cb // 2), lambda j, k: (j, 0, k)),
        compiler_params=_params(("parallel", "parallel"), big=True),
        name="fft_stage1",
    )(zsrc, f1_pad)
    nk = n2 // ch
    q = pl.pallas_call(
        _fft_mid_kernel,
        out_shape=jax.ShapeDtypeStruct((n2, 2 * n1, c // 2), jnp.uint32),
        grid=(nk, ncb),
        in_specs=[pl.BlockSpec((n1, ch, cb // 2), lambda j, k: (0, j, k)),
                  pl.BlockSpec((n1, ch, cb // 2), lambda j, k: (0, nk + j, k)),
                  pl.BlockSpec((2 * n1, 2 * n1), lambda j, k: (0, 0)),
                  pl.BlockSpec((None, ch, 2 * n1, cb // 2), lambda j, k: (order, j, 0, k)),
                  pl.BlockSpec((ch, 2 * n1, 2 * n1), lambda j, k: (j, 0, 0))],
        out_specs=pl.BlockSpec((ch, 2 * n1, cb // 2), lambda j, k: (j, 0, k)),
        compiler_params=_params(("parallel", "parallel"), big=True),
        name="fft_mid",
    )(y, y, f2, hf, g2)
    nj = n1 // ch
    return pl.pallas_call(
        _fft_last_kernel,
        out_shape=jax.ShapeDtypeStruct((2, half, n1, c), F32),
        grid=(nj, ncb),
        in_specs=[pl.BlockSpec((n2, ch, cb // 2), lambda j, k: (0, j, k)),
                  pl.BlockSpec((n2, ch, cb // 2), lambda j, k: (0, nj + j, k)),
                  pl.BlockSpec((2 * half, 2 * n2), lambda j, k: (0, 0)),
                  pl.BlockSpec((2, half, ch, cb), lambda j, k: (0, 0, j, gate_col * ncb + k)),
                  pl.BlockSpec((2, half, ch, cb), lambda j, k: (0, 0, j, z_col * ncb + k)),
                  pl.BlockSpec((None, 1, cb), lambda j, k: (order, 0, k))],
        out_specs=pl.BlockSpec((2, half, ch, cb), lambda j, k: (0, 0, j, k)),
        compiler_params=_params(("parallel", "parallel"), big=True),
        name="fft_last",
    )(q, q, g1, gate_src, zsrc, fbias.reshape(fbias.shape[0], 1, c))


def hyena_mixer(hn, h_res, w_in, b_in, conv_w, conv_b, fw1, fb1, fw2, fb2, freq, fw3, fbias, w_out,
                batch, seq):
    width = w_out.shape[0]
    assert batch == 2 and 2 * seq == FFT_N1 * FFT_N2
    sc = hyena_in_conv(hn, w_in, b_in, conv_w, conv_b, seq)
    consts = _dft_constants()
    ts = hyena_two_sided_filters(seq, fw1, fb1, fw2, fb2, freq, fw3, width)
    hf = hyena_filter_spectrum(ts, consts)
    sc4 = sc.reshape(batch, FFT_N2 // 2, FFT_N1, 3 * width)
    zf1 = hyena_long_conv_gate(sc4, 2, sc4, 0, hf, 0, fbias, consts)
    zf2 = hyena_long_conv_gate(zf1, 0, sc4, 1, hf, 1, fbias, consts)
    return matmul([zf2.reshape(batch * seq, width)], w_out, res=h_res, tm=512, tn=1024,
                  name="hyena_out")


def moe_swiglu(hn2_packed, logits, wg, wu, wd):
    n = hn2_packed.shape[0]
    top_v, top_i = lax.top_k(logits, TOP_K)
    gates = jax.nn.softmax(top_v, axis=-1)
    e_flat = top_i.reshape(-1).astype(jnp.int32)
    nk = n * TOP_K
    onehot = (e_flat[:, None] == jnp.arange(N_EXPERTS, dtype=jnp.int32)[None]).astype(jnp.int32)
    csum = jnp.cumsum(onehot, axis=0)
    rank = jnp.take_along_axis(csum, e_flat[:, None], axis=1)[:, 0] - 1
    counts = csum[-1]
    padded = ((counts + MOE_TILE - 1) // MOE_TILE) * MOE_TILE
    pad_end = jnp.cumsum(padded)
    pad_start = pad_end - padded
    dest = pad_start[e_flat] + rank
    p_rows = nk + N_EXPERTS * MOE_TILE
    order = jnp.argsort(e_flat, stable=True).astype(jnp.int32)
    sorted_tok = jnp.pad(order // TOP_K, (0, DMA_LOOP_UNROLL))
    start = jnp.cumsum(counts) - counts
    nt = p_rows // MOE_TILE
    tile_start = jnp.arange(nt, dtype=jnp.int32) * MOE_TILE
    tile_used = tile_start < pad_end[-1]
    tile_exp = jnp.minimum(jnp.searchsorted(pad_end, tile_start, side='right'), N_EXPERTS - 1).astype(jnp.int32)
    tile_rows = jnp.clip(counts[tile_exp] - (tile_start - pad_start[tile_exp]), 0, MOE_TILE)
    tile_rows = jnp.where(tile_used, tile_rows, 0).astype(jnp.int32)
    last_exp = tile_exp[jnp.maximum(jnp.sum(tile_used.astype(jnp.int32)) - 1, 0)]
    tile_src = jnp.where(tile_used, start[tile_exp] + (tile_start - pad_start[tile_exp]), 0)
    tile_exp = jnp.where(tile_used, tile_exp, last_exp)
    ys = swiglu(hn2_packed, wg, wu, wd, tile_exp, tile_rows, tile_src.astype(jnp.int32), sorted_tok,
                tf=MOE_F_TILE, sub_rows=FFN_SUB_ROWS, name="swiglu_experts")
    return ys, dest.reshape(n, TOP_K), gates


def kernel(x, p, ln_mix, ln_ffn, ln_ple, final_norm, t5_bias, w_attn_in, w_attn_out, attn_sink, na_rpb, w_ffn_gate, w_ffn_up, w_ffn_down, w_hy_in, b_hy_in, w_hy_conv, b_hy_conv, w_hy_f1, b_hy_f1, w_hy_f2, b_hy_f2, hy_freq, w_hy_f3, hy_bias, w_hy_out, w_router, w_exp_gate, w_exp_up, w_exp_down, w_ple_proj, w_ple_gate):
    batch, seq, d = x.shape
    n = batch * seq
    depth = ln_mix.shape[0]
    h = x.reshape(n, d)
    for i in range(depth):
        li = i // 2
        hn = rmsnorm(h, ln_mix[i], BF16)
        if i % 2 == 0:
            na_off = A_WIDTH + 2 * A_KV_WIDTH
            w_in = w_attn_in[li]
            proj_a = matmul([hn], w_in[:, :na_off], out_dtype=BF16, tn=640, name="attn_in_a")
            proj_n = matmul([hn], w_in[:, na_off:], out_dtype=BF16, name="attn_in_n")
            oa = window_attention(proj_a.reshape(batch, seq, -1), t5_bias, attn_sink[li])
            ob = neighbourhood_attention(proj_n.reshape(batch, seq, -1), na_rpb[li])
            h = matmul([oa.reshape(n, A_WIDTH), ob.reshape(n, B_WIDTH)], w_attn_out[li], res=h,
                       name="attn_out")
            nt = n // MOE_TILE
            h = swiglu(h, w_ffn_gate, w_ffn_up, w_ffn_down,
                       jnp.full((nt,), li, jnp.int32), jnp.full((nt,), MOE_TILE, jnp.int32),
                       gain=ln_ffn[i], name="swiglu_dense")
        else:
            h = hyena_mixer(hn, h, w_hy_in[li], b_hy_in[li], w_hy_conv[li], b_hy_conv[li],
                            w_hy_f1[li], b_hy_f1[li], w_hy_f2[li], b_hy_f2[li], hy_freq[li],
                            w_hy_f3[li], hy_bias[li], w_hy_out[li], batch, seq)
            wr_pad = jnp.pad(w_router[li].astype(F32), ((0, 0), (0, 128 - N_EXPERTS)))
            hn2_packed, logits = rmsnorm_router(h, ln_ffn[i], wr_pad)
            ys, dest2, gates = moe_swiglu(hn2_packed, logits[:, :N_EXPERTS], w_exp_gate[li],
                                          w_exp_up[li], w_exp_down[li])
            h = moe_combine(h, ys, dest2, gates)
        h = ple(h, p[i].reshape(n, -1), ln_ple[i], w_ple_gate[i], w_ple_proj[i])
    return rmsnorm(h, final_norm, F32).reshape(batch, seq, d)
```

```python
import functools
import math

import jax
import jax.numpy as jnp
import numpy as np
from jax import lax
from jax.experimental import pallas as pl
from jax.experimental.pallas import tpu as pltpu

F32 = jnp.float32
BF16 = jnp.bfloat16
NEG_INF = -1e30
RMS_EPS = 1e-6

V7X_VMEM_LIMIT_BYTES = 56 * 1024 * 1024

HEAD_DIM = 64
A_Q_HEADS = 16
A_KV_HEADS = 2
A_GROUP = A_Q_HEADS // A_KV_HEADS
A_BLOCK = 128
T5_BUCKETS = 32
T5_MAX_DIST = 128
B_HEADS = 16
GRID_W = 64
NA_WIN_H = 8
NA_WIN_W = 16
A_WIDTH = A_Q_HEADS * HEAD_DIM
A_KV_WIDTH = A_KV_HEADS * HEAD_DIM
B_WIDTH = B_HEADS * HEAD_DIM
N_EXPERTS = 8
TOP_K = 2
HYENA_EMB = 33
HYENA_DECAY_TARGET = 1e-2
HYENA_FAST_PCT = 0.3
HYENA_SLOW_PCT = 1.5

FFT_N1 = 64
FFT_N2 = 128
FFT_CHUNK = 8
FFT_CB = 1024

MOE_TILE = 1024
FFN_F_TILE = 256
MOE_F_TILE = 512
FFN_SUB_ROWS = 256
DMA_LOOP_UNROLL = 8


def _params(semantics, big=False):
    return pltpu.CompilerParams(
        dimension_semantics=semantics,
        vmem_limit_bytes=V7X_VMEM_LIMIT_BYTES if big else None)


def _rmsnorm_kernel(x_ref, g_ref, o_ref):
    x = x_ref[...]
    y = x * lax.rsqrt(jnp.mean(x * x, axis=-1, keepdims=True) + RMS_EPS)
    o_ref[...] = (y * g_ref[...]).astype(o_ref.dtype)


def rmsnorm(x2, g, out_dtype, tm=512):
    n, d = x2.shape
    return pl.pallas_call(
        _rmsnorm_kernel,
        out_shape=jax.ShapeDtypeStruct((n, d), out_dtype),
        grid=(n // tm,),
        in_specs=[pl.BlockSpec((tm, d), lambda i: (i, 0)),
                  pl.BlockSpec((1, d), lambda i: (0, 0))],
        out_specs=pl.BlockSpec((tm, d), lambda i: (i, 0)),
        compiler_params=_params(("parallel",)),
        name="rmsnorm",
    )(x2, g.reshape(1, d))


def _pack_bf16_pairs(y):
    w = y.shape[1] // 2
    bits = pltpu.bitcast(y.astype(BF16).astype(F32), jnp.uint32)
    return (bits[:, :w] >> 16) | (bits[:, w:] & jnp.uint32(0xFFFF0000))


def _unpack_pairs_f32(p):
    lo = pltpu.bitcast(p << 16, F32)
    hi = pltpu.bitcast(p & jnp.uint32(0xFFFF0000), F32)
    return jnp.concatenate([lo, hi], axis=1)


def _unpack_bf16_pairs(p):
    return _unpack_pairs_f32(p).astype(BF16)


def _rmsnorm_router_kernel(x_ref, g_ref, wr_ref, o_ref, l_ref):
    x = x_ref[...]
    y = x * lax.rsqrt(jnp.mean(x * x, axis=-1, keepdims=True) + RMS_EPS)
    y = y * g_ref[...]
    o_ref[...] = _pack_bf16_pairs(y)
    l_ref[...] = jnp.dot(y, wr_ref[...], preferred_element_type=F32,
                         precision=lax.Precision.HIGHEST)


def rmsnorm_router(x2, g, w_router_pad, tm=512):
    n, d = x2.shape
    ne = w_router_pad.shape[1]
    return pl.pallas_call(
        _rmsnorm_router_kernel,
        out_shape=(jax.ShapeDtypeStruct((n, d // 2), jnp.uint32), jax.ShapeDtypeStruct((n, ne), F32)),
        grid=(n // tm,),
        in_specs=[pl.BlockSpec((tm, d), lambda i: (i, 0)),
                  pl.BlockSpec((1, d), lambda i: (0, 0)),
                  pl.BlockSpec((d, ne), lambda i: (0, 0))],
        out_specs=(pl.BlockSpec((tm, d // 2), lambda i: (i, 0)),
                   pl.BlockSpec((tm, ne), lambda i: (i, 0))),
        compiler_params=_params(("parallel",)),
        name="rmsnorm_router",
    )(x2, g.reshape(1, d), w_router_pad)


def _mm_kernel(*refs, n_a, has_bias, has_res):
    a_refs = refs[:n_a]
    w_refs = refs[n_a:2 * n_a]
    idx = 2 * n_a
    bias_ref = refs[idx] if has_bias else None
    idx += int(has_bias)
    res_ref = refs[idx] if has_res else None
    idx += int(has_res)
    o_ref = refs[idx]
    wbf_refs = refs[idx + 1:idx + 1 + n_a]

    @pl.when(pl.program_id(1) == 0)
    def _():
        for w_ref, wbf_ref in zip(w_refs, wbf_refs):
            wbf_ref[...] = w_ref[...].astype(BF16)

    acc = None
    for a_ref, wbf_ref in zip(a_refs, wbf_refs):
        d = jnp.dot(a_ref[...].astype(BF16), wbf_ref[...], preferred_element_type=F32)
        acc = d if acc is None else acc + d
    if has_bias:
        acc = acc + bias_ref[...]
    if has_res:
        acc = acc + res_ref[...]
    o_ref[...] = acc.astype(o_ref.dtype)


def matmul(a_list, w, *, bias=None, res=None, out_dtype=F32, tm=1024, tn=1024, name="matmul"):
    m = a_list[0].shape[0]
    n = w.shape[1]
    n_a = len(a_list)
    k_each = a_list[0].shape[1]
    assert all(a.shape == (m, k_each) for a in a_list) and w.shape[0] == n_a * k_each
    in_specs = [pl.BlockSpec((tm, k_each), lambda j, i: (i, 0)) for _ in a_list]
    in_specs += [pl.BlockSpec((k_each, tn), functools.partial(lambda j, i, kb: (kb, j), kb=kb))
                 for kb in range(n_a)]
    args = list(a_list) + [w] * n_a
    if bias is not None:
        in_specs.append(pl.BlockSpec((1, tn), lambda j, i: (0, j)))
        args.append(bias.reshape(1, n))
    if res is not None:
        in_specs.append(pl.BlockSpec((tm, tn), lambda j, i: (i, j)))
        args.append(res)
    return pl.pallas_call(
        functools.partial(_mm_kernel, n_a=n_a, has_bias=bias is not None, has_res=res is not None),
        out_shape=jax.ShapeDtypeStruct((m, n), out_dtype),
        grid=(n // tn, m // tm),
        in_specs=in_specs,
        out_specs=pl.BlockSpec((tm, tn), lambda j, i: (i, j)),
        scratch_shapes=[pltpu.VMEM((k_each, tn), BF16) for _ in a_list],
        compiler_params=_params(("parallel", "arbitrary"), big=True),
        name=name,
    )(*args)


def _rms_scale_bf16(x, g):
    y = x * lax.rsqrt(jnp.mean(x * x, axis=-1, keepdims=True) + RMS_EPS)
    return (y * g).astype(BF16)


def _ple_kernel(h_ref, p_ref, g_ref, wg_ref, wp_ref, o_ref, wg_bf, wp_bf):
    @pl.when(pl.program_id(1) == 0)
    def _():
        wg_bf[...] = wg_ref[...].astype(BF16)
        wp_bf[...] = wp_ref[...].astype(BF16)

    tn = o_ref.shape[1]
    a = jnp.dot(_rms_scale_bf16(h_ref[...], g_ref[...]), wg_bf[...], preferred_element_type=F32)
    pp = jnp.dot(p_ref[...].astype(BF16), wp_bf[...], preferred_element_type=F32)
    col = pl.multiple_of(pl.program_id(0) * tn, tn)
    o_ref[...] = h_ref[:, pl.ds(col, tn)] + jax.nn.sigmoid(a) * pp


def ple(h2, p_all, g, w_gate_all, w_proj_all, layer, tm=512, tn=1024):
    m, d = h2.shape
    pd = p_all.shape[2]
    return pl.pallas_call(
        _ple_kernel,
        out_shape=jax.ShapeDtypeStruct((m, d), F32),
        grid=(d // tn, m // tm),
        in_specs=[pl.BlockSpec((tm, d), lambda j, i: (i, 0)),
                  pl.BlockSpec((None, tm, pd), lambda j, i: (layer, i, 0)),
                  pl.BlockSpec((1, d), lambda j, i: (0, 0)),
                  pl.BlockSpec((None, d, tn), lambda j, i: (layer, 0, j)),
                  pl.BlockSpec((None, pd, tn), lambda j, i: (layer, 0, j))],
        out_specs=pl.BlockSpec((tm, tn), lambda j, i: (i, j)),
        scratch_shapes=[pltpu.VMEM((d, tn), BF16), pltpu.VMEM((pd, tn), BF16)],
        compiler_params=_params(("parallel", "arbitrary"), big=True),
        name="ple",
    )(h2, p_all, g.reshape(1, d), w_gate_all, w_proj_all)


def _row_copy(src_hbm, row, dst, i, sem):
    return pltpu.make_async_copy(src_hbm.at[pl.ds(row, 1), :], dst.at[pl.ds(i, 1), :], sem)


def _swiglu_kernel(exp_ref, rows_ref, src_ref, tok_ref, x_ref, *rest, n_col, gather, sub_rows, nf):
    del exp_ref
    gain_ref = None if gather else rest[0]
    wg_ref, wu_ref, wd_ref, o_ref = rest[0 if gather else 1:][:4]
    scratch = rest[(4 if gather else 5):]
    t = pl.program_id(0)
    f = pl.program_id(1)
    nt = pl.num_programs(0)
    n_rows = rows_ref[t]
    tm = o_ref.shape[0]

    if not gather:
        (xbf,) = scratch

        @pl.when(f == 0)
        def _():
            h = x_ref[...]
            o_ref[...] = h
            xbf[...] = _rms_scale_bf16(h, gain_ref[...])

    if gather:
        @pl.when(f == 0)
        def _():
            o_ref[...] = jnp.zeros_like(o_ref)

        xbuf, xbf, sems = scratch
        slot = t % 2
        n_fetch = xbuf.shape[1]
        per_step = n_fetch // nf

        def start_row(tile, sl, i):
            _row_copy(x_ref, tok_ref[src_ref[tile] + i], xbuf.at[sl], i, sems.at[sl, i]).start()

        def wait_all(sl):
            def body(c, carry):
                for r in range(DMA_LOOP_UNROLL):
                    i = c * DMA_LOOP_UNROLL + r
                    _row_copy(x_ref, 0, xbuf.at[sl], i, sems.at[sl, i]).wait()
                return carry
            lax.fori_loop(0, n_fetch // DMA_LOOP_UNROLL, body, 0)

        @pl.when(jnp.logical_and(f == 0, t == 0))
        def _():
            def body(i, carry):
                start_row(0, 0, i)
                return carry
            lax.fori_loop(0, n_fetch, body, 0, unroll=DMA_LOOP_UNROLL)

        @pl.when(f == 0)
        def _():
            wait_all(slot)
            xbf[...] = _unpack_bf16_pairs(xbuf[slot, 0:tm])

        def fetch_ahead():
            nxt = jnp.minimum(t + 1, nt - 1)
            for r in range(per_step):
                start_row(nxt, 1 - slot, f * per_step + r)
    else:
        def fetch_ahead():
            pass

    def ffn_rows(n):
        fetch_ahead()
        if n == 0:
            return
        x = xbf[0:n, :]
        g = jnp.dot(x, wg_ref[...].astype(BF16), preferred_element_type=F32)
        u = jnp.dot(x, wu_ref[...].astype(BF16), preferred_element_type=F32)
        hmid = (g * jax.nn.sigmoid(g) * u).astype(BF16)
        wd = wd_ref[...].astype(BF16)
        cw = o_ref.shape[1] // n_col
        for c in range(n_col):
            y = jnp.dot(hmid, wd[:, c * cw:(c + 1) * cw], preferred_element_type=F32)
            o_ref[0:n, c * cw:(c + 1) * cw] += y

    for k in range(0 if gather else 1, tm // sub_rows + 1):
        covers = jnp.logical_and(n_rows > (k - 1) * sub_rows, n_rows <= k * sub_rows)
        pl.when(covers)(functools.partial(ffn_rows, k * sub_rows))

    if gather:
        @pl.when(jnp.logical_and(t == nt - 1, f == nf - 1))
        def _():
            wait_all(1 - slot)


def gather_rows_per_tile(tm, nf):
    per_step = -(-tm // nf)
    while (per_step * nf) % DMA_LOOP_UNROLL:
        per_step += 1
    return per_step * nf


def swiglu(x, w_gate, w_up, w_down, tile_exp, tile_rows, tile_src=None, src_rows=None, *, gain=None,
           tm=MOE_TILE, tf=FFN_F_TILE, sub_rows=MOE_TILE, name="swiglu"):
    gather = src_rows is not None
    assert gather != (gain is not None)
    m = tile_rows.shape[0] * tm
    d = w_gate.shape[1]
    dff = w_gate.shape[2]
    nf = dff // tf
    nt = m // tm
    assert nf >= 2

    def w_in_map(t, f, exp_ref, rows_ref, src_ref, tok_ref):
        return (exp_ref[t], 0, jnp.where(rows_ref[t] > 0, f, nf - 1))

    def w_out_map(t, f, exp_ref, rows_ref, src_ref, tok_ref):
        return (exp_ref[t], jnp.where(rows_ref[t] > 0, f, nf - 1), 0)

    def tile_map(t, f, exp_ref, rows_ref, src_ref, tok_ref):
        return (t, 0)

    w_specs = [pl.BlockSpec((None, d, tf), w_in_map),
               pl.BlockSpec((None, d, tf), w_in_map),
               pl.BlockSpec((None, tf, d), w_out_map)]
    if gather:
        in_specs = [pl.BlockSpec(memory_space=pl.ANY)] + w_specs
        args = [x, w_gate, w_up, w_down]
        n_fetch = gather_rows_per_tile(tm, nf)
        scratch = [pltpu.VMEM((2, n_fetch, d // 2), jnp.uint32), pltpu.VMEM((tm, d), BF16),
                   pltpu.SemaphoreType.DMA((2, n_fetch))]
    else:
        in_specs = [pl.BlockSpec((tm, d), tile_map, pipeline_mode=pl.Buffered(1)),
                    pl.BlockSpec((1, d), lambda t, f, e, r, s, k: (0, 0))] + w_specs
        args = [x, gain.reshape(1, d), w_gate, w_up, w_down]
        scratch = [pltpu.VMEM((tm, d), BF16)]
        tile_src = jnp.zeros((nt,), jnp.int32)
        src_rows = jnp.zeros((1,), jnp.int32)
    grid_spec = pltpu.PrefetchScalarGridSpec(
        num_scalar_prefetch=4,
        grid=(nt, nf),
        in_specs=in_specs,
        out_specs=pl.BlockSpec((tm, d), tile_map, pipeline_mode=pl.Buffered(1) if gather else None),
        scratch_shapes=scratch,
    )
    return pl.pallas_call(
        functools.partial(_swiglu_kernel, n_col=4, gather=gather, sub_rows=sub_rows, nf=nf),
        out_shape=jax.ShapeDtypeStruct((m, d), F32),
        grid_spec=grid_spec,
        compiler_params=_params(("arbitrary", "arbitrary"), big=True),
        name=name,
    )(tile_exp, tile_rows, tile_src, src_rows, *args)


COMBINE_TILE = 256


def _combine_kernel(d0_ref, d1_ref, ys_hbm, gate_ref, h_ref, hout_ref, buf, sems):
    i = pl.program_id(0)
    n = pl.num_programs(0)
    tt = h_ref.shape[0]
    slot = i % 2
    dests = (d0_ref, d1_ref)

    def start_tile(tile, sl):
        def body(r, c):
            for k, d_ref in enumerate(dests):
                _row_copy(ys_hbm, d_ref[tile * tt + r], buf.at[sl, k], r, sems.at[sl, k, r]).start()
            return c
        lax.fori_loop(0, tt, body, 0, unroll=DMA_LOOP_UNROLL)

    @pl.when(i == 0)
    def _():
        start_tile(0, 0)

    @pl.when(i + 1 < n)
    def _():
        start_tile(i + 1, 1 - slot)

    def wait_body(r, c):
        for k in range(TOP_K):
            _row_copy(ys_hbm, 0, buf.at[slot, k], r, sems.at[slot, k, r]).wait()
        return c
    lax.fori_loop(0, tt, wait_body, 0, unroll=DMA_LOOP_UNROLL)

    gates = gate_ref[...]
    hout_ref[...] = h_ref[...] + (buf[slot, 0] * gates[:, 0:1] + buf[slot, 1] * gates[:, 1:2])


def moe_combine(h2, ys, dest2, gates, tt=COMBINE_TILE):
    n, d = h2.shape
    grid_spec = pltpu.PrefetchScalarGridSpec(
        num_scalar_prefetch=2,
        grid=(n // tt,),
        in_specs=[pl.BlockSpec(memory_space=pl.ANY),
                  pl.BlockSpec((tt, TOP_K), lambda i, a, b: (i, 0)),
                  pl.BlockSpec((tt, d), lambda i, a, b: (i, 0))],
        out_specs=pl.BlockSpec((tt, d), lambda i, a, b: (i, 0)),
        scratch_shapes=[pltpu.VMEM((2, TOP_K, tt, d), F32), pltpu.SemaphoreType.DMA((2, TOP_K, tt))],
    )
    return pl.pallas_call(
        _combine_kernel,
        out_shape=jax.ShapeDtypeStruct((n, d), F32),
        grid_spec=grid_spec,
        compiler_params=_params(("arbitrary",), big=True),
        name="moe_combine",
    )(dest2[:, 0], dest2[:, 1], ys, gates, h2)


def _t5_bucket(rel):
    half = T5_BUCKETS // 2
    max_exact = half // 2
    n = jnp.abs(rel)
    log_ratio = jnp.log(jnp.maximum(n, 1).astype(F32) / max_exact) / math.log(T5_MAX_DIST / max_exact)
    large = jnp.minimum(max_exact + (log_ratio * (half - max_exact)).astype(jnp.int32), half - 1)
    return jnp.where(rel > 0, half, 0) + jnp.where(n < max_exact, n, large)


def _window_bias_table(t5_bias):
    i = jnp.arange(A_BLOCK)[:, None]
    j = jnp.arange(3 * A_BLOCK)[None, :]
    rel = j - A_BLOCK - i
    onehot = (_t5_bucket(rel)[None] == jnp.arange(T5_BUCKETS)[:, None, None]).astype(F32)
    bias = jnp.einsum('bh,bij->hij', t5_bias.astype(F32), onehot, precision=lax.Precision.HIGHEST)
    return jnp.where((jnp.abs(rel) <= A_BLOCK)[None], bias, NEG_INF)


def _window_kernel(sink_ref, q_ref, kv_ref, bias_ref, o_ref, *, nb):
    n = pl.program_id(1)
    scale = HEAD_DIM ** -0.5
    starts = (jnp.maximum(n - 1, 0), n, jnp.minimum(n + 1, nb - 1))
    kv = [kv_ref[pl.ds(pl.multiple_of(s * A_BLOCK, A_BLOCK), A_BLOCK), :] for s in starts]
    col = lax.broadcasted_iota(jnp.int32, (A_BLOCK, 3 * A_BLOCK), 1)
    edge_ok = jnp.logical_and(jnp.logical_or(n > 0, col >= A_BLOCK),
                              jnp.logical_or(n < nb - 1, col < 2 * A_BLOCK))
    kv = jnp.concatenate(kv, axis=0)
    q = q_ref[...]
    lo = lax.broadcasted_iota(jnp.int32, (1, 2 * HEAD_DIM), 1) < HEAD_DIM
    sel = (jnp.where(lo, scale, 0.0).astype(BF16), jnp.where(lo, 0.0, scale).astype(BF16))
    pairs_per_kv = A_GROUP // 2
    kk, vv = [], []
    for kh in range(A_KV_HEADS):
        k1 = kv[:, kh * HEAD_DIM:(kh + 1) * HEAD_DIM]
        v1 = kv[:, A_KV_WIDTH + kh * HEAD_DIM:A_KV_WIDTH + (kh + 1) * HEAD_DIM]
        kk.append(jnp.concatenate([k1, k1], axis=1))
        vv.append(jnp.concatenate([v1, v1], axis=1))
    scores = []
    for pr in range(A_Q_HEADS // 2):
        q2 = q[:, pr * 2 * HEAD_DIM:(pr + 1) * 2 * HEAD_DIM]
        qq = jnp.concatenate([q2 * sel[0], q2 * sel[1]], axis=0)
        s2 = lax.dot_general(qq, kk[pr // pairs_per_kv], (((1,), (1,)), ((), ())),
                             preferred_element_type=F32)
        for i in range(2):
            scores.append(jnp.where(edge_ok, s2[i * A_BLOCK:(i + 1) * A_BLOCK] + bias_ref[2 * pr + i],
                                    NEG_INF))
    probs, dens = [], []
    for h, s in enumerate(scores):
        sink = sink_ref[h]
        mx = jnp.maximum(jnp.max(s, axis=-1, keepdims=True), sink)
        p = jnp.exp(s - mx)
        dens.append(jnp.sum(p, axis=-1, keepdims=True) + jnp.exp(sink - mx))
        probs.append(p.astype(BF16))
    for pr in range(A_Q_HEADS // 2):
        v2 = vv[pr // pairs_per_kv]
        oa = jnp.dot(probs[2 * pr], v2, preferred_element_type=F32) / dens[2 * pr]
        ob = jnp.dot(probs[2 * pr + 1], v2, preferred_element_type=F32) / dens[2 * pr + 1]
        o_ref[:, pr * 2 * HEAD_DIM:(pr + 1) * 2 * HEAD_DIM] = jnp.where(lo, oa, ob).astype(o_ref.dtype)


def window_attention(proj_a, t5_bias, sink):
    b, s, _ = proj_a.shape
    nb = s // A_BLOCK
    kvw = 2 * A_KV_WIDTH
    grid_spec = pltpu.PrefetchScalarGridSpec(
        num_scalar_prefetch=0,
        grid=(b, nb),
        in_specs=[pl.BlockSpec(memory_space=pltpu.SMEM),
                  pl.BlockSpec((None, A_BLOCK, A_WIDTH), lambda bi, n: (bi, n, 0)),
                  pl.BlockSpec((None, s, kvw), lambda bi, n: (bi, 0, A_WIDTH // kvw)),
                  pl.BlockSpec((A_Q_HEADS, A_BLOCK, 3 * A_BLOCK), lambda bi, n: (0, 0, 0))],
        out_specs=pl.BlockSpec((None, A_BLOCK, A_WIDTH), lambda bi, n: (bi, n, 0)),
    )
    return pl.pallas_call(
        functools.partial(_window_kernel, nb=nb),
        out_shape=jax.ShapeDtypeStruct((b, s, A_WIDTH), BF16),
        grid_spec=grid_spec,
        compiler_params=_params(("parallel", "arbitrary")),
        name="window_attention",
    )(sink.astype(F32), proj_a, proj_a, _window_bias_table(t5_bias))


NA_HEAD_GROUP = 16


def _na_bias_table(rpb, rows):
    kh = min(NA_WIN_H, rows)
    kw = NA_WIN_W
    c = jnp.arange(GRID_W)
    cs = jnp.clip(c - kw // 2, 0, GRID_W - kw)
    col_ok = (c[None] >= cs[:, None]) & (c[None] < cs[:, None] + kw)
    col_off = jnp.clip(c[None] - c[:, None], -(kw - 1), kw - 1) + kw - 1
    onehot = (col_off[None] == jnp.arange(2 * kw - 1)[:, None, None]).astype(F32)
    by_col = jnp.einsum('hrc,cqk->hqrk', rpb.astype(F32), onehot, precision=lax.Precision.HIGHEST)
    by_col = jnp.where(col_ok[None, :, None, :], by_col, NEG_INF)
    tabs = [by_col[:, :, NA_WIN_H - 1 - d:NA_WIN_H - 1 - d + kh] for d in range(kh)]
    return jnp.stack(tabs, axis=0).reshape(kh, rpb.shape[0], GRID_W, kh * GRID_W)


def _na_kernel(q_ref, k_ref, v_ref, bias_ref, o_ref, *, rows):
    r = pl.program_id(2)
    kh = min(NA_WIN_H, rows)
    scale = HEAD_DIM ** -0.5
    start = pl.multiple_of(jnp.clip(r - kh // 2, 0, rows - kh) * GRID_W, GRID_W)
    k = k_ref[pl.ds(start, kh * GRID_W), :]
    v = v_ref[pl.ds(start, kh * GRID_W), :]
    q = q_ref[...]
    lo = lax.broadcasted_iota(jnp.int32, (1, 2 * HEAD_DIM), 1) < HEAD_DIM
    sel = (jnp.where(lo, scale, 0.0).astype(BF16), jnp.where(lo, 0.0, scale).astype(BF16))
    n_pairs = NA_HEAD_GROUP // 2
    scores = []
    for pr in range(n_pairs):
        sl = slice(pr * 2 * HEAD_DIM, (pr + 1) * 2 * HEAD_DIM)
        q2 = q[:, sl]
        qq = jnp.concatenate([q2 * sel[0], q2 * sel[1]], axis=0)
        s2 = lax.dot_general(qq, k[:, sl], (((1,), (1,)), ((), ())), preferred_element_type=F32)
        for i in range(2):
            scores.append(s2[i * GRID_W:(i + 1) * GRID_W] + bias_ref[2 * pr + i])
    probs, dens = [], []
    for s in scores:
        mx = jnp.max(s, axis=-1, keepdims=True)
        p = jnp.exp(s - mx)
        dens.append(jnp.sum(p, axis=-1, keepdims=True))
        probs.append(p.astype(BF16))
    for pr in range(n_pairs):
        sl = slice(pr * 2 * HEAD_DIM, (pr + 1) * 2 * HEAD_DIM)
        oa = jnp.dot(probs[2 * pr], v[:, sl], preferred_element_type=F32) / dens[2 * pr]
        ob = jnp.dot(probs[2 * pr + 1], v[:, sl], preferred_element_type=F32) / dens[2 * pr + 1]
        o_ref[:, sl] = jnp.where(lo, oa, ob).astype(o_ref.dtype)


def neighbourhood_attention(proj_n, rpb):
    b, s, _ = proj_n.shape
    rows = s // GRID_W
    kh = min(NA_WIN_H, rows)
    gw = NA_HEAD_GROUP * HEAD_DIM
    ng = B_WIDTH // gw

    def bias_map(bi, g, r):
        return (r - jnp.clip(r - kh // 2, 0, rows - kh), g, 0, 0)

    return pl.pallas_call(
        functools.partial(_na_kernel, rows=rows),
        out_shape=jax.ShapeDtypeStruct((b, s, B_WIDTH), BF16),
        grid=(b, ng, rows),
        in_specs=[pl.BlockSpec((None, GRID_W, gw), lambda bi, g, r: (bi, r, g)),
                  pl.BlockSpec((None, s, gw), lambda bi, g, r: (bi, 0, ng + g)),
                  pl.BlockSpec((None, s, gw), lambda bi, g, r: (bi, 0, 2 * ng + g)),
                  pl.BlockSpec((None, NA_HEAD_GROUP, GRID_W, kh * GRID_W), bias_map)],
        out_specs=pl.BlockSpec((None, GRID_W, gw), lambda bi, g, r: (bi, r, g)),
        compiler_params=_params(("parallel", "parallel", "arbitrary"), big=True),
        name="neighbourhood_attention",
    )(proj_n, proj_n, proj_n, _na_bias_table(rpb, rows))


CONV_HALO = 16
CONV_COL_CHUNK = 256


def _hyena_in_kernel(prev_ref, cur_ref, next_ref, w_ref, b_ref, cw_ref, cb_ref, o_ref, wbf_ref, *,
                     tiles_per_seq):
    i = pl.program_id(1)

    @pl.when(i == 0)
    def _():
        wbf_ref[...] = w_ref[...].astype(BF16)

    tm = cur_ref.shape[0]
    rows = tm + 2 * CONV_HALO
    a = jnp.concatenate([prev_ref[...], cur_ref[...], next_ref[...]], axis=0)
    pos = i % tiles_per_seq
    row = lax.broadcasted_iota(jnp.int32, (rows, 1), 0)
    kill_up = jnp.logical_and(pos == 0, row == CONV_HALO)
    kill_dn = jnp.logical_and(pos == tiles_per_seq - 1, row == CONV_HALO + tm - 1)
    for c in range(o_ref.shape[1] // CONV_COL_CHUNK):
        cs = slice(c * CONV_COL_CHUNK, (c + 1) * CONV_COL_CHUNK)
        p = jnp.dot(a, wbf_ref[:, cs], preferred_element_type=F32) + b_ref[:, cs]
        up = jnp.where(kill_up, 0.0, pltpu.roll(p, 1, 0))
        dn = jnp.where(kill_dn, 0.0, pltpu.roll(p, rows - 1, 0))
        y = up * cw_ref[0:1, cs] + p * cw_ref[1:2, cs] + dn * cw_ref[2:3, cs] + cb_ref[:, cs]
        o_ref[:, cs] = y[CONV_HALO:CONV_HALO + tm]


def hyena_in_conv(hn, w_in, b_in, conv_w, conv_b, seq, tm=1024, tn=1024):
    m, k = hn.shape
    n = w_in.shape[1]
    hb = tm // CONV_HALO
    last_halo = m // CONV_HALO - 1
    return pl.pallas_call(
        functools.partial(_hyena_in_kernel, tiles_per_seq=seq // tm),
        out_shape=jax.ShapeDtypeStruct((m, n), F32),
        grid=(n // tn, m // tm),
        in_specs=[pl.BlockSpec((CONV_HALO, k), lambda j, i: (jnp.maximum(i * hb - 1, 0), 0)),
                  pl.BlockSpec((tm, k), lambda j, i: (i, 0)),
                  pl.BlockSpec((CONV_HALO, k), lambda j, i: (jnp.minimum((i + 1) * hb, last_halo), 0)),
                  pl.BlockSpec((k, tn), lambda j, i: (0, j)),
                  pl.BlockSpec((1, tn), lambda j, i: (0, j)),
                  pl.BlockSpec((3, tn), lambda j, i: (0, j)),
                  pl.BlockSpec((1, tn), lambda j, i: (0, j))],
        out_specs=pl.BlockSpec((tm, tn), lambda j, i: (i, j)),
        scratch_shapes=[pltpu.VMEM((k, tn), BF16)],
        compiler_params=_params(("parallel", "arbitrary"), big=True),
        name="hyena_in_conv",
    )(hn, hn, hn, w_in, b_in.reshape(1, n), conv_w, conv_b.reshape(1, n))


def _filter_kernel(z_ref, t_ref, fw1_ref, fb1_ref, fw2_ref, fb2_ref, fr_ref, w3_ref, w3b_ref,
                   delta_ref, o_ref, *, seq):
    m = pl.program_id(0)
    hi = lax.Precision.HIGHEST
    fr = fr_ref[...]
    hid = jnp.sin(fr * (jnp.dot(z_ref[...], fw1_ref[...], preferred_element_type=F32, precision=hi)
                        + fb1_ref[...]))
    hid = jnp.sin(fr * (jnp.dot(hid, fw2_ref[...], preferred_element_type=F32, precision=hi)
                        + fb2_ref[...]))
    hid_bf = hid.astype(BF16)
    h = jnp.dot(hid_bf, w3_ref[...].astype(BF16), preferred_element_type=F32)
    tm = h.shape[0]
    width = delta_ref.shape[1]
    row = lax.broadcasted_iota(jnp.int32, (tm, width), 0) + m * tm
    decay = jnp.where(row == seq, 0.0, jnp.exp(-t_ref[...] * delta_ref[...]))
    def emit(taps):
        for o in range(o_ref.shape[0]):
            for kb in range(width // FFT_CB):
                cols = slice(kb * FFT_CB, (kb + 1) * FFT_CB)
                blk = taps[:, o * width:(o + 1) * width][:, cols] * decay[:, cols]
                o_ref[o, :, kb * (FFT_CB // 2):(kb + 1) * (FFT_CB // 2)] = _pack_bf16_pairs(blk)

    @pl.when(m != 0)
    def _():
        emit(h)

    @pl.when(m == 0)
    def _():
        h_back = jnp.dot(hid_bf, w3b_ref[...].astype(BF16), preferred_element_type=F32)
        first = lax.broadcasted_iota(jnp.int32, h.shape, 0) == 0
        emit(h + jnp.where(first, h_back, 0.0))


def hyena_two_sided_filters(seq, fw1, fb1, fw2, fb2, freq, fw3, width, tm=512):
    n = 2 * seq
    u = jnp.arange(n)
    pos = jnp.where(u <= seq, u, n - u)
    pos = jnp.where(u == seq, 0, pos)
    bands = (HYENA_EMB - 1) // 2
    f = jnp.linspace(1e-4, bands - 1, bands, dtype=F32)[None]
    t = (pos.astype(F32) / (seq - 1))[:, None]
    w = ((2.0 * math.pi / seq) * pos.astype(F32))[:, None]
    z = jnp.concatenate([t, jnp.cos(f * w), -jnp.sin(f * w)], axis=-1)
    emb_pad = 40
    z = jnp.pad(z, ((0, 0), (0, emb_pad - HYENA_EMB)))
    fw1p = jnp.pad(fw1.astype(F32), ((0, emb_pad - HYENA_EMB), (0, 0)))
    fh = fw1.shape[1]
    max_decay = math.log(HYENA_DECAY_TARGET) / HYENA_FAST_PCT
    min_decay = math.log(HYENA_DECAY_TARGET) / HYENA_SLOW_PCT
    deltas = jnp.abs(jnp.linspace(min_decay, max_decay, width, dtype=F32))[None]
    half_tiles = seq // tm
    orders = fw3.shape[1] // (2 * width)
    ow = orders * width

    return pl.pallas_call(
        functools.partial(_filter_kernel, seq=seq),
        out_shape=jax.ShapeDtypeStruct((orders, n, width // 2), jnp.uint32),
        grid=(n // tm,),
        in_specs=[pl.BlockSpec((tm, emb_pad), lambda m: (m, 0)),
                  pl.BlockSpec((tm, 1), lambda m: (m, 0)),
                  pl.BlockSpec((emb_pad, fh), lambda m: (0, 0)),
                  pl.BlockSpec((1, fh), lambda m: (0, 0)),
                  pl.BlockSpec((fh, fh), lambda m: (0, 0)),
                  pl.BlockSpec((1, fh), lambda m: (0, 0)),
                  pl.BlockSpec((1, fh), lambda m: (0, 0)),
                  pl.BlockSpec((fh, ow), lambda m: (0, jnp.where(m >= half_tiles, 1, 0))),
                  pl.BlockSpec((fh, ow), lambda m: (0, 1)),
                  pl.BlockSpec((1, width), lambda m: (0, 0))],
        out_specs=pl.BlockSpec((orders, tm, width // 2), lambda m: (0, m, 0)),
        compiler_params=_params(("arbitrary",), big=True),
        name="hyena_filters",
    )(z, t, fw1p, fb1.reshape(1, fh).astype(F32), fw2.astype(F32), fb2.reshape(1, fh).astype(F32),
      freq.reshape(1, fh).astype(F32), fw3, fw3, deltas)


def _dft_constants():
    n1, n2 = FFT_N1, FFT_N2
    n = n1 * n2
    a1 = np.arange(n1)
    a2 = np.arange(n2)
    half = n2 // 2

    def cplx_block(w):
        return np.block([[w.real, -w.imag], [w.imag, w.real]])

    ang = (a2[None, :, None] * a2[None, None, :] / n2) + (a1[:, None, None] * a2[None, :, None] / n)
    w1 = np.exp(-2j * np.pi * ang)
    f1_pad = np.stack([cplx_block(w1[i][:, :half]) for i in range(n1)])
    f1_real = np.stack([np.concatenate([w1[i].real, w1[i].imag], axis=0) for i in range(n1)])
    w2 = np.exp(-2j * np.pi * (a1[:, None] * a1[None, :]) / n1)
    f2 = cplx_block(w2)
    ang = (a1[None, :, None] * a1[None, None, :] / n1) + (a2[:, None, None] * a1[None, :, None] / n)
    g2 = np.stack([cplx_block(m_) for m_ in np.exp(2j * np.pi * ang)])
    wg1 = np.exp(2j * np.pi * (a2[:half, None] * a2[None, :]) / n2) / n
    g1 = cplx_block(wg1)
    to = lambda x: jnp.asarray(x.astype(np.float32)).astype(BF16)
    return to(f1_pad), to(f1_real), to(f2), to(g2), to(g1)


def _fft_stage1_kernel(x_ref, f_ref, y_ref):
    xr = pltpu.einshape("mjc->jmc", x_ref[0])
    xi = pltpu.einshape("mjc->jmc", x_ref[1])
    for j in range(FFT_CHUNK):
        xc = jnp.concatenate([xr[j], xi[j]], axis=0).astype(BF16)
        y_ref[j] = _pack_bf16_pairs(jnp.dot(f_ref[j], xc, preferred_element_type=F32))


def _fft_stage1_real_kernel(x_ref, f_ref, y_ref):
    x = pltpu.einshape("mjc->jmc", x_ref[...])
    for j in range(FFT_CHUNK):
        y_ref[j] = _pack_bf16_pairs(jnp.dot(f_ref[j], _unpack_bf16_pairs(x[j]),
                                            preferred_element_type=F32))


def _fft_stage2_kernel(yr_ref, yi_ref, f2_ref, h_ref):
    yr = pltpu.einshape("nkc->knc", yr_ref[...])
    yi = pltpu.einshape("nkc->knc", yi_ref[...])
    for j in range(FFT_CHUNK):
        yc = _unpack_bf16_pairs(jnp.concatenate([yr[j], yi[j]], axis=0))
        h_ref[j] = _pack_bf16_pairs(jnp.dot(f2_ref[...], yc, preferred_element_type=F32))


def _fft_mid_kernel(yr_ref, yi_ref, f2_ref, h_ref, g2_ref, q_ref):
    yr = pltpu.einshape("nkc->knc", yr_ref[...])
    yi = pltpu.einshape("nkc->knc", yi_ref[...])
    n1 = FFT_N1
    spec = [jnp.dot(f2_ref[...], _unpack_bf16_pairs(jnp.concatenate([yr[j], yi[j]], axis=0)),
                    preferred_element_type=F32) for j in range(FFT_CHUNK)]
    prod = []
    for j, z in enumerate(spec):
        hf = _unpack_pairs_f32(h_ref[j])
        zr, zi, hr, hi = z[:n1], z[n1:], hf[:n1], hf[n1:]
        prod.append(jnp.concatenate([zr * hr - zi * hi, zr * hi + zi * hr], axis=0).astype(BF16))
    for j, pc in enumerate(prod):
        q_ref[j] = _pack_bf16_pairs(jnp.dot(g2_ref[j], pc, preferred_element_type=F32))


def _fft_last_kernel(qr_ref, qi_ref, g1_ref, gate_ref, zf_ref, fb_ref, o_ref):
    qr = pltpu.einshape("kjc->jkc", qr_ref[...])
    qi = pltpu.einshape("kjc->jkc", qi_ref[...])
    half = FFT_N2 // 2
    conv = []
    for j in range(FFT_CHUNK):
        qc = _unpack_bf16_pairs(jnp.concatenate([qr[j], qi[j]], axis=0))
        conv.append(jnp.dot(g1_ref[...], qc, preferred_element_type=F32))
    conv = jnp.stack(conv, axis=0)
    fb = fb_ref[...]
    for bi in range(2):
        cb = pltpu.einshape("jmc->mjc", conv[:, bi * half:(bi + 1) * half])
        zf = zf_ref[bi]
        o_ref[bi] = gate_ref[bi] * (cb + zf * fb)


def hyena_filter_spectrum(ts, consts, cb=FFT_CB):
    _, f1_real, f2, _, _ = consts
    no, n, c = ts.shape
    c = 2 * c
    n1, n2, ch = FFT_N1, FFT_N2, FFT_CHUNK
    ts4 = ts.reshape(no, n2, n1, c // 2)
    y = pl.pallas_call(
        _fft_stage1_real_kernel,
        out_shape=jax.ShapeDtypeStruct((no, n1, 2 * n2, c // 2), jnp.uint32),
        grid=(no, n1 // ch, c // cb),
        in_specs=[pl.BlockSpec((None, n2, ch, cb // 2), lambda o, j, k: (o, 0, j, k)),
                  pl.BlockSpec((ch, 2 * n2, n2), lambda o, j, k: (j, 0, 0))],
        out_specs=pl.BlockSpec((None, ch, 2 * n2, cb // 2), lambda o, j, k: (o, j, 0, k)),
        compiler_params=_params(("parallel", "parallel", "parallel"), big=True),
        name="fft_filter_stage1",
    )(ts4, f1_real)
    nk = n2 // ch
    return pl.pallas_call(
        _fft_stage2_kernel,
        out_shape=jax.ShapeDtypeStruct((no, n2, 2 * n1, c // 2), jnp.uint32),
        grid=(no, nk, c // cb),
        in_specs=[pl.BlockSpec((None, n1, ch, cb // 2), lambda o, j, k: (o, 0, j, k)),
                  pl.BlockSpec((None, n1, ch, cb // 2), lambda o, j, k: (o, 0, nk + j, k)),
                  pl.BlockSpec((2 * n1, 2 * n1), lambda o, j, k: (0, 0))],
        out_specs=pl.BlockSpec((None, ch, 2 * n1, cb // 2), lambda o, j, k: (o, j, 0, k)),
        compiler_params=_params(("parallel", "parallel", "parallel"), big=True),
        name="fft_filter_stage2",
    )(y, y, f2)


def hyena_long_conv_gate(zsrc, z_col, gate_src, gate_col, hf, order, fbias, consts, cb=FFT_CB):
    f1_pad, _, f2, g2, g1 = consts
    n1, n2, ch = FFT_N1, FFT_N2, FFT_CHUNK
    c = 2 * hf.shape[-1]
    ncb = c // cb
    half = n2 // 2
    y = pl.pallas_call(
        _fft_stage1_kernel,
        out_shape=jax.ShapeDtypeStruct((n1, 2 * n2, c // 2), jnp.uint32),
        grid=(n1 // ch, ncb),
        in_specs=[pl.BlockSpec((2, half, ch, cb), lambda j, k: (0, 0, j, z_col * ncb + k)),
                  pl.BlockSpec((ch, 2 * n2, n2), lambda j, k: (j, 0, 0))],
        out_specs=pl.BlockSpec((ch, 2 * n2, cb // 2), lambda j, k: (j, 0, k)),
        compiler_params=_params(("parallel", "parallel"), big=True),
        name="fft_stage1",
    )(zsrc, f1_pad)
    nk = n2 // ch
    q = pl.pallas_call(
        _fft_mid_kernel,
        out_shape=jax.ShapeDtypeStruct((n2, 2 * n1, c // 2), jnp.uint32),
        grid=(nk, ncb),
        in_specs=[pl.BlockSpec((n1, ch, cb // 2), lambda j, k: (0, j, k)),
                  pl.BlockSpec((n1, ch, cb // 2), lambda j, k: (0, nk + j, k)),
                  pl.BlockSpec((2 * n1, 2 * n1), lambda j, k: (0, 0)),
                  pl.BlockSpec((None, ch, 2 * n1, cb // 2), lambda j, k: (order, j, 0, k)),
                  pl.BlockSpec((ch, 2 * n1, 2 * n1), lambda j, k: (j, 0, 0))],
        out_specs=pl.BlockSpec((ch, 2 * n1, cb // 2), lambda j, k: (j, 0, k)),
        compiler_params=_params(("parallel", "parallel"), big=True),
        name="fft_mid",
    )(y, y, f2, hf, g2)
    nj = n1 // ch
    return pl.pallas_call(
        _fft_last_kernel,
        out_shape=jax.ShapeDtypeStruct((2, half, n1, c), F32),
        grid=(nj, ncb),
        in_specs=[pl.BlockSpec((n2, ch, cb // 2), lambda j, k: (0, j, k)),
                  pl.BlockSpec((n2, ch, cb // 2), lambda j, k: (0, nj + j, k)),
                  pl.BlockSpec((2 * half, 2 * n2), lambda j, k: (0, 0)),
                  pl.BlockSpec((2, half, ch, cb), lambda j, k: (0, 0, j, gate_col * ncb + k)),
                  pl.BlockSpec((2, half, ch, cb), lambda j, k: (0, 0, j, z_col * ncb + k)),
                  pl.BlockSpec((None, 1, cb), lambda j, k: (order, 0, k))],
        out_specs=pl.BlockSpec((2, half, ch, cb), lambda j, k: (0, 0, j, k)),
        compiler_params=_params(("parallel", "parallel"), big=True),
        name="fft_last",
    )(q, q, g1, gate_src, zsrc, fbias.reshape(fbias.shape[0], 1, c))


def hyena_mixer(hn, h_res, w_in, b_in, conv_w, conv_b, fw1, fb1, fw2, fb2, freq, fw3, fbias, w_out,
                batch, seq):
    width = w_out.shape[0]
    assert batch == 2 and 2 * seq == FFT_N1 * FFT_N2
    sc = hyena_in_conv(hn, w_in, b_in, conv_w, conv_b, seq)
    consts = _dft_constants()
    ts = hyena_two_sided_filters(seq, fw1, fb1, fw2, fb2, freq, fw3, width)
    hf = hyena_filter_spectrum(ts, consts)
    sc4 = sc.reshape(batch, FFT_N2 // 2, FFT_N1, 3 * width)
    zf1 = hyena_long_conv_gate(sc4, 2, sc4, 0, hf, 0, fbias, consts)
    zf2 = hyena_long_conv_gate(zf1, 0, sc4, 1, hf, 1, fbias, consts)
    return matmul([zf2.reshape(batch * seq, width)], w_out, res=h_res, tm=512, tn=1024,
                  name="hyena_out")


def moe_swiglu(hn2_packed, logits, wg, wu, wd):
    n = hn2_packed.shape[0]
    top_v, top_i = lax.top_k(logits, TOP_K)
    gates = jax.nn.softmax(top_v, axis=-1)
    e_flat = top_i.reshape(-1).astype(jnp.int32)
    nk = n * TOP_K
    onehot = (e_flat[:, None] == jnp.arange(N_EXPERTS, dtype=jnp.int32)[None]).astype(jnp.int32)
    csum = jnp.cumsum(onehot, axis=0)
    rank = jnp.take_along_axis(csum, e_flat[:, None], axis=1)[:, 0] - 1
    counts = csum[-1]
    padded = ((counts + MOE_TILE - 1) // MOE_TILE) * MOE_TILE
    pad_end = jnp.cumsum(padded)
    pad_start = pad_end - padded
    dest = pad_start[e_flat] + rank
    p_rows = nk + N_EXPERTS * MOE_TILE
    order = jnp.argsort(e_flat, stable=True).astype(jnp.int32)
    nf = wg.shape[2] // MOE_F_TILE
    sorted_tok = jnp.pad(order // TOP_K, (0, gather_rows_per_tile(MOE_TILE, nf)))
    start = jnp.cumsum(counts) - counts
    nt = p_rows // MOE_TILE
    tile_start = jnp.arange(nt, dtype=jnp.int32) * MOE_TILE
    tile_used = tile_start < pad_end[-1]
    tile_exp = jnp.minimum(jnp.searchsorted(pad_end, tile_start, side='right'), N_EXPERTS - 1).astype(jnp.int32)
    tile_rows = jnp.clip(counts[tile_exp] - (tile_start - pad_start[tile_exp]), 0, MOE_TILE)
    tile_rows = jnp.where(tile_used, tile_rows, 0).astype(jnp.int32)
    last_exp = tile_exp[jnp.maximum(jnp.sum(tile_used.astype(jnp.int32)) - 1, 0)]
    tile_src = jnp.where(tile_used, start[tile_exp] + (tile_start - pad_start[tile_exp]), 0)
    tile_exp = jnp.where(tile_used, tile_exp, last_exp)
    ys = swiglu(hn2_packed, wg, wu, wd, tile_exp, tile_rows, tile_src.astype(jnp.int32), sorted_tok,
                tf=MOE_F_TILE, sub_rows=FFN_SUB_ROWS, name="swiglu_experts")
    return ys, dest.reshape(n, TOP_K), gates


def kernel(x, p, ln_mix, ln_ffn, ln_ple, final_norm, t5_bias, w_attn_in, w_attn_out, attn_sink, na_rpb, w_ffn_gate, w_ffn_up, w_ffn_down, w_hy_in, b_hy_in, w_hy_conv, b_hy_conv, w_hy_f1, b_hy_f1, w_hy_f2, b_hy_f2, hy_freq, w_hy_f3, hy_bias, w_hy_out, w_router, w_exp_gate, w_exp_up, w_exp_down, w_ple_proj, w_ple_gate):
    batch, seq, d = x.shape
    n = batch * seq
    depth = ln_mix.shape[0]
    h = x.reshape(n, d)
    for i in range(depth):
        li = i // 2
        hn = rmsnorm(h, ln_mix[i], BF16)
        if i % 2 == 0:
            na_off = A_WIDTH + 2 * A_KV_WIDTH
            w_in = w_attn_in[li]
            proj_a = matmul([hn], w_in[:, :na_off], out_dtype=BF16, tn=640, name="attn_in_a")
            proj_n = matmul([hn], w_in[:, na_off:], out_dtype=BF16, name="attn_in_n")
            oa = window_attention(proj_a.reshape(batch, seq, -1), t5_bias, attn_sink[li])
            ob = neighbourhood_attention(proj_n.reshape(batch, seq, -1), na_rpb[li])
            h = matmul([oa.reshape(n, A_WIDTH), ob.reshape(n, B_WIDTH)], w_attn_out[li], res=h,
                       name="attn_out")
            nt = n // MOE_TILE
            h = swiglu(h, w_ffn_gate, w_ffn_up, w_ffn_down,
                       jnp.full((nt,), li, jnp.int32), jnp.full((nt,), MOE_TILE, jnp.int32),
                       gain=ln_ffn[i], name="swiglu_dense")
        else:
            h = hyena_mixer(hn, h, w_hy_in[li], b_hy_in[li], w_hy_conv[li], b_hy_conv[li],
                            w_hy_f1[li], b_hy_f1[li], w_hy_f2[li], b_hy_f2[li], hy_freq[li],
                            w_hy_f3[li], hy_bias[li], w_hy_out[li], batch, seq)
            wr_pad = jnp.pad(w_router[li].astype(F32), ((0, 0), (0, 128 - N_EXPERTS)))
            hn2_packed, logits = rmsnorm_router(h, ln_ffn[i], wr_pad)
            ys, dest2, gates = moe_swiglu(hn2_packed, logits[:, :N_EXPERTS], w_exp_gate[li],
                                          w_exp_up[li], w_exp_down[li])
            h = moe_combine(h, ys, dest2, gates)
        h = ple(h, p.reshape(depth, n, -1), ln_ple[i], w_ple_gate, w_ple_proj, i)
    return rmsnorm(h, final_norm, F32).reshape(batch, seq, d)
```

```python
import functools
import math

import jax
import jax.numpy as jnp
import numpy as np
from jax import lax
from jax.experimental import pallas as pl
from jax.experimental.pallas import tpu as pltpu

F32 = jnp.float32
BF16 = jnp.bfloat16
NEG_INF = -1e30
RMS_EPS = 1e-6

V7X_VMEM_LIMIT_BYTES = 56 * 1024 * 1024

HEAD_DIM = 64
A_Q_HEADS = 16
A_KV_HEADS = 2
A_GROUP = A_Q_HEADS // A_KV_HEADS
A_BLOCK = 128
T5_BUCKETS = 32
T5_MAX_DIST = 128
B_HEADS = 16
GRID_W = 64
NA_WIN_H = 8
NA_WIN_W = 16
A_WIDTH = A_Q_HEADS * HEAD_DIM
A_KV_WIDTH = A_KV_HEADS * HEAD_DIM
B_WIDTH = B_HEADS * HEAD_DIM
N_EXPERTS = 8
TOP_K = 2
HYENA_EMB = 33
HYENA_DECAY_TARGET = 1e-2
HYENA_FAST_PCT = 0.3
HYENA_SLOW_PCT = 1.5

FFT_N1 = 64
FFT_N2 = 128
FFT_CHUNK = 8
FFT_CB = 1024

MOE_TILE = 1024
FFN_F_TILE = 256
MOE_F_TILE = 512
FFN_SUB_ROWS = 256
DMA_LOOP_UNROLL = 8


def _params(semantics, big=False):
    return pltpu.CompilerParams(
        dimension_semantics=semantics,
        vmem_limit_bytes=V7X_VMEM_LIMIT_BYTES if big else None)


def _rmsnorm_kernel(x_ref, g_ref, o_ref):
    x = x_ref[...]
    y = x * lax.rsqrt(jnp.mean(x * x, axis=-1, keepdims=True) + RMS_EPS)
    o_ref[...] = (y * g_ref[...]).astype(o_ref.dtype)


def rmsnorm(x2, g, out_dtype, tm=512):
    n, d = x2.shape
    return pl.pallas_call(
        _rmsnorm_kernel,
        out_shape=jax.ShapeDtypeStruct((n, d), out_dtype),
        grid=(n // tm,),
        in_specs=[pl.BlockSpec((tm, d), lambda i: (i, 0)),
                  pl.BlockSpec((1, d), lambda i: (0, 0))],
        out_specs=pl.BlockSpec((tm, d), lambda i: (i, 0)),
        compiler_params=_params(("parallel",)),
        name="rmsnorm",
    )(x2, g.reshape(1, d))


def _pack_bf16_pairs(y):
    w = y.shape[1] // 2
    bits = pltpu.bitcast(y.astype(BF16).astype(F32), jnp.uint32)
    return (bits[:, :w] >> 16) | (bits[:, w:] & jnp.uint32(0xFFFF0000))


def _unpack_pairs_f32(p):
    lo = pltpu.bitcast(p << 16, F32)
    hi = pltpu.bitcast(p & jnp.uint32(0xFFFF0000), F32)
    return jnp.concatenate([lo, hi], axis=1)


def _unpack_bf16_pairs(p):
    return _unpack_pairs_f32(p).astype(BF16)


def _rmsnorm_router_kernel(x_ref, g_ref, wr_ref, o_ref, l_ref):
    x = x_ref[...]
    y = x * lax.rsqrt(jnp.mean(x * x, axis=-1, keepdims=True) + RMS_EPS)
    y = y * g_ref[...]
    o_ref[...] = _pack_bf16_pairs(y)
    l_ref[...] = jnp.dot(y, wr_ref[...], preferred_element_type=F32,
                         precision=lax.Precision.HIGHEST)


def rmsnorm_router(x2, g, w_router_pad, tm=512):
    n, d = x2.shape
    ne = w_router_pad.shape[1]
    return pl.pallas_call(
        _rmsnorm_router_kernel,
        out_shape=(jax.ShapeDtypeStruct((n, d // 2), jnp.uint32), jax.ShapeDtypeStruct((n, ne), F32)),
        grid=(n // tm,),
        in_specs=[pl.BlockSpec((tm, d), lambda i: (i, 0)),
                  pl.BlockSpec((1, d), lambda i: (0, 0)),
                  pl.BlockSpec((d, ne), lambda i: (0, 0))],
        out_specs=(pl.BlockSpec((tm, d // 2), lambda i: (i, 0)),
                   pl.BlockSpec((tm, ne), lambda i: (i, 0))),
        compiler_params=_params(("parallel",)),
        name="rmsnorm_router",
    )(x2, g.reshape(1, d), w_router_pad)


def _mm_kernel(*refs, n_a, has_bias, has_res):
    a_refs = refs[:n_a]
    w_refs = refs[n_a:2 * n_a]
    idx = 2 * n_a
    bias_ref = refs[idx] if has_bias else None
    idx += int(has_bias)
    res_ref = refs[idx] if has_res else None
    idx += int(has_res)
    o_ref = refs[idx]
    wbf_refs = refs[idx + 1:idx + 1 + n_a]

    @pl.when(pl.program_id(1) == 0)
    def _():
        for w_ref, wbf_ref in zip(w_refs, wbf_refs):
            wbf_ref[...] = w_ref[...].astype(BF16)

    acc = None
    for a_ref, wbf_ref in zip(a_refs, wbf_refs):
        d = jnp.dot(a_ref[...].astype(BF16), wbf_ref[...], preferred_element_type=F32)
        acc = d if acc is None else acc + d
    if has_bias:
        acc = acc + bias_ref[...]
    if has_res:
        acc = acc + res_ref[...]
    o_ref[...] = acc.astype(o_ref.dtype)


def matmul(a_list, w, *, bias=None, res=None, out_dtype=F32, tm=1024, tn=1024, name="matmul"):
    m = a_list[0].shape[0]
    n = w.shape[1]
    n_a = len(a_list)
    k_each = a_list[0].shape[1]
    assert all(a.shape == (m, k_each) for a in a_list) and w.shape[0] == n_a * k_each
    in_specs = [pl.BlockSpec((tm, k_each), lambda j, i: (i, 0)) for _ in a_list]
    in_specs += [pl.BlockSpec((k_each, tn), functools.partial(lambda j, i, kb: (kb, j), kb=kb))
                 for kb in range(n_a)]
    args = list(a_list) + [w] * n_a
    if bias is not None:
        in_specs.append(pl.BlockSpec((1, tn), lambda j, i: (0, j)))
        args.append(bias.reshape(1, n))
    if res is not None:
        in_specs.append(pl.BlockSpec((tm, tn), lambda j, i: (i, j)))
        args.append(res)
    return pl.pallas_call(
        functools.partial(_mm_kernel, n_a=n_a, has_bias=bias is not None, has_res=res is not None),
        out_shape=jax.ShapeDtypeStruct((m, n), out_dtype),
        grid=(n // tn, m // tm),
        in_specs=in_specs,
        out_specs=pl.BlockSpec((tm, tn), lambda j, i: (i, j)),
        scratch_shapes=[pltpu.VMEM((k_each, tn), BF16) for _ in a_list],
        compiler_params=_params(("parallel", "arbitrary"), big=True),
        name=name,
    )(*args)


def _rms_scale_bf16(x, g):
    y = x * lax.rsqrt(jnp.mean(x * x, axis=-1, keepdims=True) + RMS_EPS)
    return (y * g).astype(BF16)


def _ple_kernel(h_ref, p_ref, g_ref, wg_ref, wp_ref, o_ref, wg_bf, wp_bf):
    @pl.when(pl.program_id(1) == 0)
    def _():
        wg_bf[...] = wg_ref[...].astype(BF16)
        wp_bf[...] = wp_ref[...].astype(BF16)

    tn = o_ref.shape[1]
    a = jnp.dot(_rms_scale_bf16(h_ref[...], g_ref[...]), wg_bf[...], preferred_element_type=F32)
    pp = jnp.dot(p_ref[...].astype(BF16), wp_bf[...], preferred_element_type=F32)
    col = pl.multiple_of(pl.program_id(0) * tn, tn)
    o_ref[...] = h_ref[:, pl.ds(col, tn)] + jax.nn.sigmoid(a) * pp


def ple(h2, p_all, g, w_gate_all, w_proj_all, layer, tm=512, tn=1024):
    m, d = h2.shape
    pd = p_all.shape[2]
    return pl.pallas_call(
        _ple_kernel,
        out_shape=jax.ShapeDtypeStruct((m, d), F32),
        grid=(d // tn, m // tm),
        in_specs=[pl.BlockSpec((tm, d), lambda j, i: (i, 0)),
                  pl.BlockSpec((None, tm, pd), lambda j, i: (layer, i, 0)),
                  pl.BlockSpec((1, d), lambda j, i: (0, 0)),
                  pl.BlockSpec((None, d, tn), lambda j, i: (layer, 0, j)),
                  pl.BlockSpec((None, pd, tn), lambda j, i: (layer, 0, j))],
        out_specs=pl.BlockSpec((tm, tn), lambda j, i: (i, j)),
        scratch_shapes=[pltpu.VMEM((d, tn), BF16), pltpu.VMEM((pd, tn), BF16)],
        compiler_params=_params(("parallel", "arbitrary"), big=True),
        name="ple",
    )(h2, p_all, g.reshape(1, d), w_gate_all, w_proj_all)


def _row_copy(src_hbm, row, dst, i, sem):
    return pltpu.make_async_copy(src_hbm.at[pl.ds(row, 1), :], dst.at[pl.ds(i, 1), :], sem)


def _swiglu_kernel(exp_ref, rows_ref, src_ref, tok_ref, x_ref, *rest, n_col, gather, sub_rows, nf):
    del exp_ref
    gain_ref = None if gather else rest[0]
    wg_ref, wu_ref, wd_ref, o_ref = rest[0 if gather else 1:][:4]
    scratch = rest[(4 if gather else 5):]
    t = pl.program_id(0)
    f = pl.program_id(1)
    nt = pl.num_programs(0)
    n_rows = rows_ref[t]
    tm = o_ref.shape[0]

    if not gather:
        (xbf,) = scratch

        @pl.when(f == 0)
        def _():
            h = x_ref[...]
            o_ref[...] = h
            xbf[...] = _rms_scale_bf16(h, gain_ref[...])

    if gather:
        @pl.when(f == 0)
        def _():
            o_ref[...] = jnp.zeros_like(o_ref)

        xbuf, xbf, sems = scratch
        slot = t % 2
        n_fetch = xbuf.shape[1]
        per_step = n_fetch // nf

        def start_row(tile, sl, i):
            _row_copy(x_ref, tok_ref[src_ref[tile] + i], xbuf.at[sl], i, sems.at[sl, i]).start()

        def wait_all(sl):
            def body(c, carry):
                for r in range(DMA_LOOP_UNROLL):
                    i = c * DMA_LOOP_UNROLL + r
                    _row_copy(x_ref, 0, xbuf.at[sl], i, sems.at[sl, i]).wait()
                return carry
            lax.fori_loop(0, n_fetch // DMA_LOOP_UNROLL, body, 0)

        @pl.when(jnp.logical_and(f == 0, t == 0))
        def _():
            def body(i, carry):
                start_row(0, 0, i)
                return carry
            lax.fori_loop(0, n_fetch, body, 0, unroll=DMA_LOOP_UNROLL)

        @pl.when(f == 0)
        def _():
            wait_all(slot)
            xbf[...] = _unpack_bf16_pairs(xbuf[slot, 0:tm])

        def fetch_ahead():
            nxt = jnp.minimum(t + 1, nt - 1)
            for r in range(per_step):
                start_row(nxt, 1 - slot, f * per_step + r)
    else:
        def fetch_ahead():
            pass

    def ffn_rows(n):
        fetch_ahead()
        if n == 0:
            return
        x = xbf[0:n, :]
        g = jnp.dot(x, wg_ref[...].astype(BF16), preferred_element_type=F32)
        u = jnp.dot(x, wu_ref[...].astype(BF16), preferred_element_type=F32)
        hmid = (g * jax.nn.sigmoid(g) * u).astype(BF16)
        wd = wd_ref[...].astype(BF16)
        cw = o_ref.shape[1] // n_col
        for c in range(n_col):
            y = jnp.dot(hmid, wd[:, c * cw:(c + 1) * cw], preferred_element_type=F32)
            o_ref[0:n, c * cw:(c + 1) * cw] += y

    for k in range(0 if gather else 1, tm // sub_rows + 1):
        covers = jnp.logical_and(n_rows > (k - 1) * sub_rows, n_rows <= k * sub_rows)
        pl.when(covers)(functools.partial(ffn_rows, k * sub_rows))

    if gather:
        @pl.when(jnp.logical_and(t == nt - 1, f == nf - 1))
        def _():
            wait_all(1 - slot)


def gather_rows_per_tile(tm, nf):
    per_step = -(-tm // nf)
    while (per_step * nf) % DMA_LOOP_UNROLL:
        per_step += 1
    return per_step * nf


def swiglu(x, w_gate, w_up, w_down, tile_exp, tile_rows, tile_src=None, src_rows=None, *, gain=None,
           tm=MOE_TILE, tf=FFN_F_TILE, sub_rows=MOE_TILE, name="swiglu"):
    gather = src_rows is not None
    assert gather != (gain is not None)
    m = tile_rows.shape[0] * tm
    d = w_gate.shape[1]
    dff = w_gate.shape[2]
    nf = dff // tf
    nt = m // tm
    assert nf >= 2

    def w_in_map(t, f, exp_ref, rows_ref, src_ref, tok_ref):
        return (exp_ref[t], 0, jnp.where(rows_ref[t] > 0, f, nf - 1))

    def w_out_map(t, f, exp_ref, rows_ref, src_ref, tok_ref):
        return (exp_ref[t], jnp.where(rows_ref[t] > 0, f, nf - 1), 0)

    def tile_map(t, f, exp_ref, rows_ref, src_ref, tok_ref):
        return (t, 0)

    w_specs = [pl.BlockSpec((None, d, tf), w_in_map),
               pl.BlockSpec((None, d, tf), w_in_map),
               pl.BlockSpec((None, tf, d), w_out_map)]
    if gather:
        in_specs = [pl.BlockSpec(memory_space=pl.ANY)] + w_specs
        args = [x, w_gate, w_up, w_down]
        n_fetch = gather_rows_per_tile(tm, nf)
        scratch = [pltpu.VMEM((2, n_fetch, d // 2), jnp.uint32), pltpu.VMEM((tm, d), BF16),
                   pltpu.SemaphoreType.DMA((2, n_fetch))]
    else:
        in_specs = [pl.BlockSpec((tm, d), tile_map, pipeline_mode=pl.Buffered(1)),
                    pl.BlockSpec((1, d), lambda t, f, e, r, s, k: (0, 0))] + w_specs
        args = [x, gain.reshape(1, d), w_gate, w_up, w_down]
        scratch = [pltpu.VMEM((tm, d), BF16)]
        tile_src = jnp.zeros((nt,), jnp.int32)
        src_rows = jnp.zeros((1,), jnp.int32)
    grid_spec = pltpu.PrefetchScalarGridSpec(
        num_scalar_prefetch=4,
        grid=(nt, nf),
        in_specs=in_specs,
        out_specs=pl.BlockSpec((tm, d), tile_map, pipeline_mode=pl.Buffered(1) if gather else None),
        scratch_shapes=scratch,
    )
    return pl.pallas_call(
        functools.partial(_swiglu_kernel, n_col=4, gather=gather, sub_rows=sub_rows, nf=nf),
        out_shape=jax.ShapeDtypeStruct((m, d), F32),
        grid_spec=grid_spec,
        compiler_params=_params(("arbitrary", "arbitrary"), big=True),
        name=name,
    )(tile_exp, tile_rows, tile_src, src_rows, *args)


COMBINE_TILE = 256


def _combine_kernel(d0_ref, d1_ref, ys_hbm, gate_ref, h_ref, hout_ref, buf, sems):
    i = pl.program_id(0)
    n = pl.num_programs(0)
    tt = h_ref.shape[0]
    slot = i % 2
    dests = (d0_ref, d1_ref)

    def start_tile(tile, sl):
        def body(r, c):
            for k, d_ref in enumerate(dests):
                _row_copy(ys_hbm, d_ref[tile * tt + r], buf.at[sl, k], r, sems.at[sl, k, r]).start()
            return c
        lax.fori_loop(0, tt, body, 0, unroll=DMA_LOOP_UNROLL)

    @pl.when(i == 0)
    def _():
        start_tile(0, 0)

    @pl.when(i + 1 < n)
    def _():
        start_tile(i + 1, 1 - slot)

    def wait_body(r, c):
        for k in range(TOP_K):
            _row_copy(ys_hbm, 0, buf.at[slot, k], r, sems.at[slot, k, r]).wait()
        return c
    lax.fori_loop(0, tt, wait_body, 0, unroll=DMA_LOOP_UNROLL)

    gates = gate_ref[...]
    hout_ref[...] = h_ref[...] + (buf[slot, 0] * gates[:, 0:1] + buf[slot, 1] * gates[:, 1:2])


def moe_combine(h2, ys, dest2, gates, tt=COMBINE_TILE):
    n, d = h2.shape
    grid_spec = pltpu.PrefetchScalarGridSpec(
        num_scalar_prefetch=2,
        grid=(n // tt,),
        in_specs=[pl.BlockSpec(memory_space=pl.ANY),
                  pl.BlockSpec((tt, TOP_K), lambda i, a, b: (i, 0)),
                  pl.BlockSpec((tt, d), lambda i, a, b: (i, 0))],
        out_specs=pl.BlockSpec((tt, d), lambda i, a, b: (i, 0)),
        scratch_shapes=[pltpu.VMEM((2, TOP_K, tt, d), F32), pltpu.SemaphoreType.DMA((2, TOP_K, tt))],
    )
    return pl.pallas_call(
        _combine_kernel,
        out_shape=jax.ShapeDtypeStruct((n, d), F32),
        grid_spec=grid_spec,
        compiler_params=_params(("arbitrary",), big=True),
        name="moe_combine",
    )(dest2[:, 0], dest2[:, 1], ys, gates, h2)


def _t5_bucket(rel):
    half = T5_BUCKETS // 2
    max_exact = half // 2
    n = jnp.abs(rel)
    log_ratio = jnp.log(jnp.maximum(n, 1).astype(F32) / max_exact) / math.log(T5_MAX_DIST / max_exact)
    large = jnp.minimum(max_exact + (log_ratio * (half - max_exact)).astype(jnp.int32), half - 1)
    return jnp.where(rel > 0, half, 0) + jnp.where(n < max_exact, n, large)


def _window_bias_table(t5_bias):
    i = jnp.arange(A_BLOCK)[:, None]
    j = jnp.arange(3 * A_BLOCK)[None, :]
    rel = j - A_BLOCK - i
    onehot = (_t5_bucket(rel)[None] == jnp.arange(T5_BUCKETS)[:, None, None]).astype(F32)
    bias = jnp.einsum('bh,bij->hij', t5_bias.astype(F32), onehot, precision=lax.Precision.HIGHEST)
    return jnp.where((jnp.abs(rel) <= A_BLOCK)[None], bias, NEG_INF)


def _window_kernel(sink_ref, q_ref, kv_ref, bias_ref, o_ref, *, nb):
    n = pl.program_id(1)
    scale = HEAD_DIM ** -0.5
    starts = (jnp.maximum(n - 1, 0), n, jnp.minimum(n + 1, nb - 1))
    kv = [kv_ref[pl.ds(pl.multiple_of(s * A_BLOCK, A_BLOCK), A_BLOCK), :] for s in starts]
    col = lax.broadcasted_iota(jnp.int32, (A_BLOCK, 3 * A_BLOCK), 1)
    edge_ok = jnp.logical_and(jnp.logical_or(n > 0, col >= A_BLOCK),
                              jnp.logical_or(n < nb - 1, col < 2 * A_BLOCK))
    kv = jnp.concatenate(kv, axis=0)
    q = q_ref[...]
    lo = lax.broadcasted_iota(jnp.int32, (1, 2 * HEAD_DIM), 1) < HEAD_DIM
    sel = (jnp.where(lo, scale, 0.0).astype(BF16), jnp.where(lo, 0.0, scale).astype(BF16))
    pairs_per_kv = A_GROUP // 2
    kk, vv = [], []
    for kh in range(A_KV_HEADS):
        k1 = kv[:, kh * HEAD_DIM:(kh + 1) * HEAD_DIM]
        v1 = kv[:, A_KV_WIDTH + kh * HEAD_DIM:A_KV_WIDTH + (kh + 1) * HEAD_DIM]
        kk.append(jnp.concatenate([k1, k1], axis=1))
        vv.append(jnp.concatenate([v1, v1], axis=1))
    scores = []
    for pr in range(A_Q_HEADS // 2):
        q2 = q[:, pr * 2 * HEAD_DIM:(pr + 1) * 2 * HEAD_DIM]
        qq = jnp.concatenate([q2 * sel[0], q2 * sel[1]], axis=0)
        s2 = lax.dot_general(qq, kk[pr // pairs_per_kv], (((1,), (1,)), ((), ())),
                             preferred_element_type=F32)
        for i in range(2):
            scores.append(jnp.where(edge_ok, s2[i * A_BLOCK:(i + 1) * A_BLOCK] + bias_ref[2 * pr + i],
                                    NEG_INF))
    probs, dens = [], []
    for h, s in enumerate(scores):
        sink = sink_ref[h]
        mx = jnp.maximum(jnp.max(s, axis=-1, keepdims=True), sink)
        p = jnp.exp(s - mx)
        dens.append(jnp.sum(p, axis=-1, keepdims=True) + jnp.exp(sink - mx))
        probs.append(p.astype(BF16))
    for pr in range(A_Q_HEADS // 2):
        v2 = vv[pr // pairs_per_kv]
        oa = jnp.dot(probs[2 * pr], v2, preferred_element_type=F32) / dens[2 * pr]
        ob = jnp.dot(probs[2 * pr + 1], v2, preferred_element_type=F32) / dens[2 * pr + 1]
        o_ref[:, pr * 2 * HEAD_DIM:(pr + 1) * 2 * HEAD_DIM] = jnp.where(lo, oa, ob).astype(o_ref.dtype)


def window_attention(proj_a, t5_bias, sink):
    b, s, _ = proj_a.shape
    nb = s // A_BLOCK
    kvw = 2 * A_KV_WIDTH
    grid_spec = pltpu.PrefetchScalarGridSpec(
        num_scalar_prefetch=0,
        grid=(b, nb),
        in_specs=[pl.BlockSpec(memory_space=pltpu.SMEM),
                  pl.BlockSpec((None, A_BLOCK, A_WIDTH), lambda bi, n: (bi, n, 0)),
                  pl.BlockSpec((None, s, kvw), lambda bi, n: (bi, 0, A_WIDTH // kvw)),
                  pl.BlockSpec((A_Q_HEADS, A_BLOCK, 3 * A_BLOCK), lambda bi, n: (0, 0, 0))],
        out_specs=pl.BlockSpec((None, A_BLOCK, A_WIDTH), lambda bi, n: (bi, n, 0)),
    )
    return pl.pallas_call(
        functools.partial(_window_kernel, nb=nb),
        out_shape=jax.ShapeDtypeStruct((b, s, A_WIDTH), BF16),
        grid_spec=grid_spec,
        compiler_params=_params(("parallel", "arbitrary")),
        name="window_attention",
    )(sink.astype(F32), proj_a, proj_a, _window_bias_table(t5_bias))


NA_HEAD_GROUP = 16


def _na_bias_table(rpb, rows):
    kh = min(NA_WIN_H, rows)
    kw = NA_WIN_W
    c = jnp.arange(GRID_W)
    cs = jnp.clip(c - kw // 2, 0, GRID_W - kw)
    col_ok = (c[None] >= cs[:, None]) & (c[None] < cs[:, None] + kw)
    col_off = jnp.clip(c[None] - c[:, None], -(kw - 1), kw - 1) + kw - 1
    onehot = (col_off[None] == jnp.arange(2 * kw - 1)[:, None, None]).astype(F32)
    by_col = jnp.einsum('hrc,cqk->hqrk', rpb.astype(F32), onehot, precision=lax.Precision.HIGHEST)
    by_col = jnp.where(col_ok[None, :, None, :], by_col, NEG_INF)
    tabs = [by_col[:, :, NA_WIN_H - 1 - d:NA_WIN_H - 1 - d + kh] for d in range(kh)]
    return jnp.stack(tabs, axis=0).reshape(kh, rpb.shape[0], GRID_W, kh * GRID_W)


def _na_kernel(q_ref, k_ref, v_ref, bias_ref, o_ref, *, rows):
    r = pl.program_id(2)
    kh = min(NA_WIN_H, rows)
    scale = HEAD_DIM ** -0.5
    start = pl.multiple_of(jnp.clip(r - kh // 2, 0, rows - kh) * GRID_W, GRID_W)
    k = k_ref[pl.ds(start, kh * GRID_W), :]
    v = v_ref[pl.ds(start, kh * GRID_W), :]
    q = q_ref[...]
    lo = lax.broadcasted_iota(jnp.int32, (1, 2 * HEAD_DIM), 1) < HEAD_DIM
    sel = (jnp.where(lo, scale, 0.0).astype(BF16), jnp.where(lo, 0.0, scale).astype(BF16))
    n_pairs = NA_HEAD_GROUP // 2
    scores = []
    for pr in range(n_pairs):
        sl = slice(pr * 2 * HEAD_DIM, (pr + 1) * 2 * HEAD_DIM)
        q2 = q[:, sl]
        qq = jnp.concatenate([q2 * sel[0], q2 * sel[1]], axis=0)
        s2 = lax.dot_general(qq, k[:, sl], (((1,), (1,)), ((), ())), preferred_element_type=F32)
        for i in range(2):
            scores.append(s2[i * GRID_W:(i + 1) * GRID_W] + bias_ref[2 * pr + i])
    probs, dens = [], []
    for s in scores:
        mx = jnp.max(s, axis=-1, keepdims=True)
        p = jnp.exp(s - mx)
        dens.append(jnp.sum(p, axis=-1, keepdims=True))
        probs.append(p.astype(BF16))
    for pr in range(n_pairs):
        sl = slice(pr * 2 * HEAD_DIM, (pr + 1) * 2 * HEAD_DIM)
        oa = jnp.dot(probs[2 * pr], v[:, sl], preferred_element_type=F32) / dens[2 * pr]
        ob = jnp.dot(probs[2 * pr + 1], v[:, sl], preferred_element_type=F32) / dens[2 * pr + 1]
        o_ref[:, sl] = jnp.where(lo, oa, ob).astype(o_ref.dtype)


def neighbourhood_attention(proj_n, rpb):
    b, s, _ = proj_n.shape
    rows = s // GRID_W
    kh = min(NA_WIN_H, rows)
    gw = NA_HEAD_GROUP * HEAD_DIM
    ng = B_WIDTH // gw

    def bias_map(bi, g, r):
        return (r - jnp.clip(r - kh // 2, 0, rows - kh), g, 0, 0)

    return pl.pallas_call(
        functools.partial(_na_kernel, rows=rows),
        out_shape=jax.ShapeDtypeStruct((b, s, B_WIDTH), BF16),
        grid=(b, ng, rows),
        in_specs=[pl.BlockSpec((None, GRID_W, gw), lambda bi, g, r: (bi, r, g)),
                  pl.BlockSpec((None, s, gw), lambda bi, g, r: (bi, 0, ng + g)),
                  pl.BlockSpec((None, s, gw), lambda bi, g, r: (bi, 0, 2 * ng + g)),
                  pl.BlockSpec((None, NA_HEAD_GROUP, GRID_W, kh * GRID_W), bias_map)],
        out_specs=pl.BlockSpec((None, GRID_W, gw), lambda bi, g, r: (bi, r, g)),
        compiler_params=_params(("parallel", "parallel", "arbitrary"), big=True),
        name="neighbourhood_attention",
    )(proj_n, proj_n, proj_n, _na_bias_table(rpb, rows))


CONV_HALO = 16
CONV_COL_CHUNK = 256


def _hyena_in_kernel(prev_ref, cur_ref, next_ref, w_ref, b_ref, cw_ref, cb_ref, o_ref, wbf_ref, *,
                     tiles_per_seq):
    i = pl.program_id(1)

    @pl.when(i == 0)
    def _():
        wbf_ref[...] = w_ref[...].astype(BF16)

    tm = cur_ref.shape[0]
    rows = tm + 2 * CONV_HALO
    a = jnp.concatenate([prev_ref[...], cur_ref[...], next_ref[...]], axis=0)
    pos = i % tiles_per_seq
    row = lax.broadcasted_iota(jnp.int32, (rows, 1), 0)
    kill_up = jnp.logical_and(pos == 0, row == CONV_HALO)
    kill_dn = jnp.logical_and(pos == tiles_per_seq - 1, row == CONV_HALO + tm - 1)
    for c in range(o_ref.shape[1] // CONV_COL_CHUNK):
        cs = slice(c * CONV_COL_CHUNK, (c + 1) * CONV_COL_CHUNK)
        p = jnp.dot(a, wbf_ref[:, cs], preferred_element_type=F32) + b_ref[:, cs]
        up = jnp.where(kill_up, 0.0, pltpu.roll(p, 1, 0))
        dn = jnp.where(kill_dn, 0.0, pltpu.roll(p, rows - 1, 0))
        y = up * cw_ref[0:1, cs] + p * cw_ref[1:2, cs] + dn * cw_ref[2:3, cs] + cb_ref[:, cs]
        o_ref[:, cs] = y[CONV_HALO:CONV_HALO + tm]


def hyena_in_conv(hn, w_in, b_in, conv_w, conv_b, seq, tm=1024, tn=1024):
    m, k = hn.shape
    n = w_in.shape[1]
    hb = tm // CONV_HALO
    last_halo = m // CONV_HALO - 1
    return pl.pallas_call(
        functools.partial(_hyena_in_kernel, tiles_per_seq=seq // tm),
        out_shape=jax.ShapeDtypeStruct((m, n), F32),
        grid=(n // tn, m // tm),
        in_specs=[pl.BlockSpec((CONV_HALO, k), lambda j, i: (jnp.maximum(i * hb - 1, 0), 0)),
                  pl.BlockSpec((tm, k), lambda j, i: (i, 0)),
                  pl.BlockSpec((CONV_HALO, k), lambda j, i: (jnp.minimum((i + 1) * hb, last_halo), 0)),
                  pl.BlockSpec((k, tn), lambda j, i: (0, j)),
                  pl.BlockSpec((1, tn), lambda j, i: (0, j)),
                  pl.BlockSpec((3, tn), lambda j, i: (0, j)),
                  pl.BlockSpec((1, tn), lambda j, i: (0, j))],
        out_specs=pl.BlockSpec((tm, tn), lambda j, i: (i, j)),
        scratch_shapes=[pltpu.VMEM((k, tn), BF16)],
        compiler_params=_params(("parallel", "arbitrary"), big=True),
        name="hyena_in_conv",
    )(hn, hn, hn, w_in, b_in.reshape(1, n), conv_w, conv_b.reshape(1, n))


def _filter_stage1_kernel(z_ref, t_ref, fw1_ref, fb1_ref, fw2_ref, fb2_ref, fr_ref, w3f0_ref, w3f1_ref,
                          w3b0_ref, w3b1_ref, delta_ref, f1_ref, y_ref, *, seq):
    j = pl.program_id(0)
    hi = lax.Precision.HIGHEST
    fr = fr_ref[...]
    hid = jnp.sin(fr * (jnp.dot(z_ref[...], fw1_ref[...], preferred_element_type=F32, precision=hi)
                        + fb1_ref[...]))
    hid = jnp.sin(fr * (jnp.dot(hid, fw2_ref[...], preferred_element_type=F32, precision=hi)
                        + fb2_ref[...]))
    hid_bf = hid.astype(BF16)
    r = lax.broadcasted_iota(jnp.int32, (hid.shape[0], 1), 0)
    u = j * FFT_CHUNK + r // FFT_N2 + FFT_N1 * (r % FFT_N2)
    decay = jnp.where(u == seq, 0.0, jnp.exp(-t_ref[...] * delta_ref[...]))
    for o, (wf_ref, wb_ref) in enumerate(((w3f0_ref, w3b0_ref), (w3f1_ref, w3b1_ref))):
        fwd = jnp.dot(hid_bf, wf_ref[...].astype(BF16), preferred_element_type=F32)
        bwd = jnp.dot(hid_bf, wb_ref[...].astype(BF16), preferred_element_type=F32)
        taps = (jnp.where(u > seq, bwd, jnp.where(u == 0, fwd + bwd, fwd)) * decay).astype(BF16)
        for jj in range(FFT_CHUNK):
            x = taps[jj * FFT_N2:(jj + 1) * FFT_N2]
            y_ref[o, jj] = _pack_bf16_pairs(jnp.dot(f1_ref[jj], x, preferred_element_type=F32))


def hyena_filter_stage1(seq, fw1, fb1, fw2, fb2, freq, fw3, width, consts, cb=FFT_CB):
    _, f1_real, _, _, _ = consts
    n = 2 * seq
    n1, n2, ch = FFT_N1, FFT_N2, FFT_CHUNK
    assert fw3.shape[1] == 4 * width
    r = jnp.arange(n)
    u = (r // (ch * n2)) * ch + (r // n2) % ch + n1 * (r % n2)
    pos = jnp.where(u <= seq, u, n - u)
    pos = jnp.where(u == seq, 0, pos)
    bands = (HYENA_EMB - 1) // 2
    f = jnp.linspace(1e-4, bands - 1, bands, dtype=F32)[None]
    t = (pos.astype(F32) / (seq - 1))[:, None]
    w = ((2.0 * math.pi / seq) * pos.astype(F32))[:, None]
    z = jnp.concatenate([t, jnp.cos(f * w), -jnp.sin(f * w)], axis=-1)
    emb_pad = 40
    z = jnp.pad(z, ((0, 0), (0, emb_pad - HYENA_EMB)))
    fw1p = jnp.pad(fw1.astype(F32), ((0, emb_pad - HYENA_EMB), (0, 0)))
    fh = fw1.shape[1]
    max_decay = math.log(HYENA_DECAY_TARGET) / HYENA_FAST_PCT
    min_decay = math.log(HYENA_DECAY_TARGET) / HYENA_SLOW_PCT
    deltas = jnp.abs(jnp.linspace(min_decay, max_decay, width, dtype=F32))[None]
    ncb = width // cb
    rows = ch * n2

    def small(shape):
        return pl.BlockSpec(shape, lambda j, k: (0, 0))

    def w3_spec(blk):
        return pl.BlockSpec((fh, cb), lambda j, k: (0, blk * ncb + k))

    return pl.pallas_call(
        functools.partial(_filter_stage1_kernel, seq=seq),
        out_shape=jax.ShapeDtypeStruct((2, n1, 2 * n2, width // 2), jnp.uint32),
        grid=(n1 // ch, ncb),
        in_specs=[pl.BlockSpec((rows, emb_pad), lambda j, k: (j, 0)),
                  pl.BlockSpec((rows, 1), lambda j, k: (j, 0)),
                  small((emb_pad, fh)), small((1, fh)), small((fh, fh)), small((1, fh)), small((1, fh)),
                  w3_spec(0), w3_spec(1), w3_spec(2), w3_spec(3),
                  pl.BlockSpec((1, cb), lambda j, k: (0, k)),
                  pl.BlockSpec((ch, 2 * n2, n2), lambda j, k: (j, 0, 0))],
        out_specs=pl.BlockSpec((2, ch, 2 * n2, cb // 2), lambda j, k: (0, j, 0, k)),
        compiler_params=_params(("parallel", "parallel"), big=True),
        name="hyena_filter_stage1",
    )(z, t, fw1p, fb1.reshape(1, fh).astype(F32), fw2.astype(F32), fb2.reshape(1, fh).astype(F32),
      freq.reshape(1, fh).astype(F32), fw3, fw3, fw3, fw3, deltas, f1_real)


def _dft_constants():
    n1, n2 = FFT_N1, FFT_N2
    n = n1 * n2
    a1 = np.arange(n1)
    a2 = np.arange(n2)
    half = n2 // 2

    def cplx_block(w):
        return np.block([[w.real, -w.imag], [w.imag, w.real]])

    ang = (a2[None, :, None] * a2[None, None, :] / n2) + (a1[:, None, None] * a2[None, :, None] / n)
    w1 = np.exp(-2j * np.pi * ang)
    f1_pad = np.stack([cplx_block(w1[i][:, :half]) for i in range(n1)])
    f1_real = np.stack([np.concatenate([w1[i].real, w1[i].imag], axis=0) for i in range(n1)])
    w2 = np.exp(-2j * np.pi * (a1[:, None] * a1[None, :]) / n1)
    f2 = cplx_block(w2)
    ang = (a1[None, :, None] * a1[None, None, :] / n1) + (a2[:, None, None] * a1[None, :, None] / n)
    g2 = np.stack([cplx_block(m_) for m_ in np.exp(2j * np.pi * ang)])
    wg1 = np.exp(2j * np.pi * (a2[:half, None] * a2[None, :]) / n2) / n
    g1 = cplx_block(wg1)
    to = lambda x: jnp.asarray(x.astype(np.float32)).astype(BF16)
    return to(f1_pad), to(f1_real), to(f2), to(g2), to(g1)


def _fft_stage1_block(x_re, x_im, f_ref, y_ref):
    xr = pltpu.einshape("mjc->jmc", x_re)
    xi = pltpu.einshape("mjc->jmc", x_im)
    for j in range(FFT_CHUNK):
        xc = jnp.concatenate([xr[j], xi[j]], axis=0).astype(BF16)
        y_ref[j] = _pack_bf16_pairs(jnp.dot(f_ref[j], xc, preferred_element_type=F32))


def _fft_stage1_kernel(x_ref, f_ref, y_ref):
    _fft_stage1_block(x_ref[0], x_ref[1], f_ref, y_ref)


def _fft_stage2_kernel(yr_ref, yi_ref, f2_ref, h_ref):
    yr = pltpu.einshape("nkc->knc", yr_ref[...])
    yi = pltpu.einshape("nkc->knc", yi_ref[...])
    for j in range(FFT_CHUNK):
        yc = _unpack_bf16_pairs(jnp.concatenate([yr[j], yi[j]], axis=0))
        h_ref[j] = _pack_bf16_pairs(jnp.dot(f2_ref[...], yc, preferred_element_type=F32))


def _fft_mid_kernel(yr_ref, yi_ref, f2_ref, h_ref, g2_ref, q_ref):
    yr = pltpu.einshape("nkc->knc", yr_ref[...])
    yi = pltpu.einshape("nkc->knc", yi_ref[...])
    n1 = FFT_N1
    spec = [jnp.dot(f2_ref[...], _unpack_bf16_pairs(jnp.concatenate([yr[j], yi[j]], axis=0)),
                    preferred_element_type=F32) for j in range(FFT_CHUNK)]
    prod = []
    for j, z in enumerate(spec):
        hf = _unpack_pairs_f32(h_ref[j])
        zr, zi, hr, hi = z[:n1], z[n1:], hf[:n1], hf[n1:]
        prod.append(jnp.concatenate([zr * hr - zi * hi, zr * hi + zi * hr], axis=0).astype(BF16))
    for j, pc in enumerate(prod):
        q_ref[j] = _pack_bf16_pairs(jnp.dot(g2_ref[j], pc, preferred_element_type=F32))


def _fft_last_kernel(qr_ref, qi_ref, g1_ref, gate_ref, zf_ref, fb_ref, *rest, feeds_next):
    f1_ref, o_ref, y_ref = rest if feeds_next else (None, rest[0], None)
    qr = pltpu.einshape("kjc->jkc", qr_ref[...])
    qi = pltpu.einshape("kjc->jkc", qi_ref[...])
    half = FFT_N2 // 2
    conv = []
    for j in range(FFT_CHUNK):
        qc = _unpack_bf16_pairs(jnp.concatenate([qr[j], qi[j]], axis=0))
        conv.append(jnp.dot(g1_ref[...], qc, preferred_element_type=F32))
    conv = jnp.stack(conv, axis=0)
    fb = fb_ref[...]
    out = []
    for bi in range(2):
        cb = pltpu.einshape("jmc->mjc", conv[:, bi * half:(bi + 1) * half])
        out.append(gate_ref[bi] * (cb + zf_ref[bi] * fb))
        o_ref[bi] = out[bi]
    if feeds_next:
        _fft_stage1_block(out[0], out[1], f1_ref, y_ref)


def hyena_filter_spectrum(y, consts, cb=FFT_CB):
    _, _, f2, _, _ = consts
    no = y.shape[0]
    c = 2 * y.shape[-1]
    n1, n2, ch = FFT_N1, FFT_N2, FFT_CHUNK
    nk = n2 // ch
    return pl.pallas_call(
        _fft_stage2_kernel,
        out_shape=jax.ShapeDtypeStruct((no, n2, 2 * n1, c // 2), jnp.uint32),
        grid=(no, nk, c // cb),
        in_specs=[pl.BlockSpec((None, n1, ch, cb // 2), lambda o, j, k: (o, 0, j, k)),
                  pl.BlockSpec((None, n1, ch, cb // 2), lambda o, j, k: (o, 0, nk + j, k)),
                  pl.BlockSpec((2 * n1, 2 * n1), lambda o, j, k: (0, 0))],
        out_specs=pl.BlockSpec((None, ch, 2 * n1, cb // 2), lambda o, j, k: (o, j, 0, k)),
        compiler_params=_params(("parallel", "parallel", "parallel"), big=True),
        name="fft_filter_stage2",
    )(y, y, f2)


def hyena_long_conv_gate(zsrc, z_col, gate_src, gate_col, hf, order, fbias, consts, y=None,
                         feeds_next=False, cb=FFT_CB):
    f1_pad, _, f2, g2, g1 = consts
    n1, n2, ch = FFT_N1, FFT_N2, FFT_CHUNK
    c = 2 * hf.shape[-1]
    ncb = c // cb
    half = n2 // 2
    y_shape = jax.ShapeDtypeStruct((n1, 2 * n2, c // 2), jnp.uint32)
    f1_spec = pl.BlockSpec((ch, 2 * n2, n2), lambda j, k: (j, 0, 0))
    y_spec = pl.BlockSpec((ch, 2 * n2, cb // 2), lambda j, k: (j, 0, k))
    if y is None:
        y = pl.pallas_call(
            _fft_stage1_kernel,
            out_shape=y_shape,
            grid=(n1 // ch, ncb),
            in_specs=[pl.BlockSpec((2, half, ch, cb), lambda j, k: (0, 0, j, z_col * ncb + k)), f1_spec],
            out_specs=y_spec,
            compiler_params=_params(("parallel", "parallel"), big=True),
            name="fft_stage1",
        )(zsrc, f1_pad)
    nk = n2 // ch
    q = pl.pallas_call(
        _fft_mid_kernel,
        out_shape=jax.ShapeDtypeStruct((n2, 2 * n1, c // 2), jnp.uint32),
        grid=(nk, ncb),
        in_specs=[pl.BlockSpec((n1, ch, cb // 2), lambda j, k: (0, j, k)),
                  pl.BlockSpec((n1, ch, cb // 2), lambda j, k: (0, nk + j, k)),
                  pl.BlockSpec((2 * n1, 2 * n1), lambda j, k: (0, 0)),
                  pl.BlockSpec((None, ch, 2 * n1, cb // 2), lambda j, k: (order, j, 0, k)),
                  pl.BlockSpec((ch, 2 * n1, 2 * n1), lambda j, k: (j, 0, 0))],
        out_specs=pl.BlockSpec((ch, 2 * n1, cb // 2), lambda j, k: (j, 0, k)),
        compiler_params=_params(("parallel", "parallel"), big=True),
        name="fft_mid",
    )(y, y, f2, hf, g2)
    nj = n1 // ch
    in_specs = [pl.BlockSpec((n2, ch, cb // 2), lambda j, k: (0, j, k)),
                pl.BlockSpec((n2, ch, cb // 2), lambda j, k: (0, nj + j, k)),
                pl.BlockSpec((2 * half, 2 * n2), lambda j, k: (0, 0)),
                pl.BlockSpec((2, half, ch, cb), lambda j, k: (0, 0, j, gate_col * ncb + k)),
                pl.BlockSpec((2, half, ch, cb), lambda j, k: (0, 0, j, z_col * ncb + k)),
                pl.BlockSpec((None, 1, cb), lambda j, k: (order, 0, k))]
    args = [q, q, g1, gate_src, zsrc, fbias.reshape(fbias.shape[0], 1, c)]
    out_shape = jax.ShapeDtypeStruct((2, half, n1, c), F32)
    out_spec = pl.BlockSpec((2, half, ch, cb), lambda j, k: (0, 0, j, k))
    if feeds_next:
        in_specs.append(f1_spec)
        args.append(f1_pad)
        out_shape, out_spec = (out_shape, y_shape), (out_spec, y_spec)
    res = pl.pallas_call(
        functools.partial(_fft_last_kernel, feeds_next=feeds_next),
        out_shape=out_shape,
        grid=(nj, ncb),
        in_specs=in_specs,
        out_specs=out_spec,
        compiler_params=_params(("parallel", "parallel"), big=True),
        name="fft_last",
    )(*args)
    return res if feeds_next else (res, None)


def hyena_mixer(hn, h_res, w_in, b_in, conv_w, conv_b, fw1, fb1, fw2, fb2, freq, fw3, fbias, w_out,
                batch, seq):
    width = w_out.shape[0]
    assert batch == 2 and 2 * seq == FFT_N1 * FFT_N2
    sc = hyena_in_conv(hn, w_in, b_in, conv_w, conv_b, seq)
    consts = _dft_constants()
    hf = hyena_filter_spectrum(hyena_filter_stage1(seq, fw1, fb1, fw2, fb2, freq, fw3, width, consts),
                               consts)
    sc4 = sc.reshape(batch, FFT_N2 // 2, FFT_N1, 3 * width)
    zf1, y1 = hyena_long_conv_gate(sc4, 2, sc4, 0, hf, 0, fbias, consts, feeds_next=True)
    zf2, _ = hyena_long_conv_gate(zf1, 0, sc4, 1, hf, 1, fbias, consts, y=y1)
    return matmul([zf2.reshape(batch * seq, width)], w_out, res=h_res, tm=512, tn=1024,
                  name="hyena_out")


def moe_swiglu(hn2_packed, logits, wg, wu, wd):
    n = hn2_packed.shape[0]
    top_v, top_i = lax.top_k(logits, TOP_K)
    gates = jax.nn.softmax(top_v, axis=-1)
    e_flat = top_i.reshape(-1).astype(jnp.int32)
    nk = n * TOP_K
    onehot = (e_flat[:, None] == jnp.arange(N_EXPERTS, dtype=jnp.int32)[None]).astype(jnp.int32)
    csum = jnp.cumsum(onehot, axis=0)
    rank = jnp.take_along_axis(csum, e_flat[:, None], axis=1)[:, 0] - 1
    counts = csum[-1]
    padded = ((counts + MOE_TILE - 1) // MOE_TILE) * MOE_TILE
    pad_end = jnp.cumsum(padded)
    pad_start = pad_end - padded
    dest = pad_start[e_flat] + rank
    p_rows = nk + N_EXPERTS * MOE_TILE
    order = jnp.argsort(e_flat, stable=True).astype(jnp.int32)
    nf = wg.shape[2] // MOE_F_TILE
    sorted_tok = jnp.pad(order // TOP_K, (0, gather_rows_per_tile(MOE_TILE, nf)))
    start = jnp.cumsum(counts) - counts
    nt = p_rows // MOE_TILE
    tile_start = jnp.arange(nt, dtype=jnp.int32) * MOE_TILE
    tile_used = tile_start < pad_end[-1]
    tile_exp = jnp.minimum(jnp.searchsorted(pad_end, tile_start, side='right'), N_EXPERTS - 1).astype(jnp.int32)
    tile_rows = jnp.clip(counts[tile_exp] - (tile_start - pad_start[tile_exp]), 0, MOE_TILE)
    tile_rows = jnp.where(tile_used, tile_rows, 0).astype(jnp.int32)
    last_exp = tile_exp[jnp.maximum(jnp.sum(tile_used.astype(jnp.int32)) - 1, 0)]
    tile_src = jnp.where(tile_used, start[tile_exp] + (tile_start - pad_start[tile_exp]), 0)
    tile_exp = jnp.where(tile_used, tile_exp, last_exp)
    ys = swiglu(hn2_packed, wg, wu, wd, tile_exp, tile_rows, tile_src.astype(jnp.int32), sorted_tok,
                tf=MOE_F_TILE, sub_rows=FFN_SUB_ROWS, name="swiglu_experts")
    return ys, dest.reshape(n, TOP_K), gates


def kernel(x, p, ln_mix, ln_ffn, ln_ple, final_norm, t5_bias, w_attn_in, w_attn_out, attn_sink, na_rpb, w_ffn_gate, w_ffn_up, w_ffn_down, w_hy_in, b_hy_in, w_hy_conv, b_hy_conv, w_hy_f1, b_hy_f1, w_hy_f2, b_hy_f2, hy_freq, w_hy_f3, hy_bias, w_hy_out, w_router, w_exp_gate, w_exp_up, w_exp_down, w_ple_proj, w_ple_gate):
    batch, seq, d = x.shape
    n = batch * seq
    depth = ln_mix.shape[0]
    h = x.reshape(n, d)
    for i in range(depth):
        li = i // 2
        hn = rmsnorm(h, ln_mix[i], BF16)
        if i % 2 == 0:
            na_off = A_WIDTH + 2 * A_KV_WIDTH
            w_in = w_attn_in[li]
            proj_a = matmul([hn], w_in[:, :na_off], out_dtype=BF16, tn=640, name="attn_in_a")
            proj_n = matmul([hn], w_in[:, na_off:], out_dtype=BF16, name="attn_in_n")
            oa = window_attention(proj_a.reshape(batch, seq, -1), t5_bias, attn_sink[li])
            ob = neighbourhood_attention(proj_n.reshape(batch, seq, -1), na_rpb[li])
            h = matmul([oa.reshape(n, A_WIDTH), ob.reshape(n, B_WIDTH)], w_attn_out[li], res=h,
                       name="attn_out")
            nt = n // MOE_TILE
            h = swiglu(h, w_ffn_gate, w_ffn_up, w_ffn_down,
                       jnp.full((nt,), li, jnp.int32), jnp.full((nt,), MOE_TILE, jnp.int32),
                       gain=ln_ffn[i], name="swiglu_dense")
        else:
            h = hyena_mixer(hn, h, w_hy_in[li], b_hy_in[li], w_hy_conv[li], b_hy_conv[li],
                            w_hy_f1[li], b_hy_f1[li], w_hy_f2[li], b_hy_f2[li], hy_freq[li],
                            w_hy_f3[li], hy_bias[li], w_hy_out[li], batch, seq)
            wr_pad = jnp.pad(w_router[li].astype(F32), ((0, 0), (0, 128 - N_EXPERTS)))
            hn2_packed, logits = rmsnorm_router(h, ln_ffn[i], wr_pad)
            ys, dest2, gates = moe_swiglu(hn2_packed, logits[:, :N_EXPERTS], w_exp_gate[li],
                                          w_exp_up[li], w_exp_down[li])
            h = moe_combine(h, ys, dest2, gates)
        h = ple(h, p.reshape(depth, n, -1), ln_ple[i], w_ple_gate, w_ple_proj, i)
    return rmsnorm(h, final_norm, F32).reshape(batch, seq, d)
```

```python
import functools
import math

import jax
import jax.numpy as jnp
import numpy as np
from jax import lax
from jax.experimental import pallas as pl
from jax.experimental.pallas import tpu as pltpu

F32 = jnp.float32
BF16 = jnp.bfloat16
NEG_INF = -1e30
RMS_EPS = 1e-6

V7X_VMEM_LIMIT_BYTES = 56 * 1024 * 1024

HEAD_DIM = 64
A_Q_HEADS = 16
A_KV_HEADS = 2
A_GROUP = A_Q_HEADS // A_KV_HEADS
A_BLOCK = 128
T5_BUCKETS = 32
T5_MAX_DIST = 128
B_HEADS = 16
GRID_W = 64
NA_WIN_H = 8
NA_WIN_W = 16
A_WIDTH = A_Q_HEADS * HEAD_DIM
A_KV_WIDTH = A_KV_HEADS * HEAD_DIM
B_WIDTH = B_HEADS * HEAD_DIM
N_EXPERTS = 8
TOP_K = 2
HYENA_EMB = 33
HYENA_DECAY_TARGET = 1e-2
HYENA_FAST_PCT = 0.3
HYENA_SLOW_PCT = 1.5

FFT_N1 = 64
FFT_N2 = 128
FFT_CHUNK = 8
FFT_CB = 1024

MOE_TILE = 1024
FFN_F_TILE = 512
MOE_F_TILE = 512
FFN_SUB_ROWS = 256
DMA_LOOP_UNROLL = 8


def _params(semantics, big=False):
    return pltpu.CompilerParams(
        dimension_semantics=semantics,
        vmem_limit_bytes=V7X_VMEM_LIMIT_BYTES if big else None)


def _rmsnorm_kernel(x_ref, g_ref, o_ref):
    x = x_ref[...]
    y = x * lax.rsqrt(jnp.mean(x * x, axis=-1, keepdims=True) + RMS_EPS)
    o_ref[...] = (y * g_ref[...]).astype(o_ref.dtype)


def rmsnorm(x2, g, out_dtype, tm=1024):
    n, d = x2.shape
    return pl.pallas_call(
        _rmsnorm_kernel,
        out_shape=jax.ShapeDtypeStruct((n, d), out_dtype),
        grid=(n // tm,),
        in_specs=[pl.BlockSpec((tm, d), lambda i: (i, 0)),
                  pl.BlockSpec((1, d), lambda i: (0, 0))],
        out_specs=pl.BlockSpec((tm, d), lambda i: (i, 0)),
        compiler_params=_params(("parallel",), big=True),
        name="rmsnorm",
    )(x2, g.reshape(1, d))


def _pack_bf16_pairs(y):
    w = y.shape[1] // 2
    bits = pltpu.bitcast(y.astype(BF16).astype(F32), jnp.uint32)
    return (bits[:, :w] >> 16) | (bits[:, w:] & jnp.uint32(0xFFFF0000))


def _unpack_pairs_f32(p):
    lo = pltpu.bitcast(p << 16, F32)
    hi = pltpu.bitcast(p & jnp.uint32(0xFFFF0000), F32)
    return jnp.concatenate([lo, hi], axis=1)


def _unpack_bf16_pairs(p):
    return _unpack_pairs_f32(p).astype(BF16)


def _rmsnorm_router_kernel(x_ref, g_ref, wr_ref, o_ref, l_ref):
    x = x_ref[...]
    y = x * lax.rsqrt(jnp.mean(x * x, axis=-1, keepdims=True) + RMS_EPS)
    y = y * g_ref[...]
    o_ref[...] = _pack_bf16_pairs(y)
    l_ref[...] = jnp.dot(y, wr_ref[...], preferred_element_type=F32,
                         precision=lax.Precision.HIGHEST)


def rmsnorm_router(x2, g, w_router_pad, tm=512):
    n, d = x2.shape
    ne = w_router_pad.shape[1]
    return pl.pallas_call(
        _rmsnorm_router_kernel,
        out_shape=(jax.ShapeDtypeStruct((n, d // 2), jnp.uint32), jax.ShapeDtypeStruct((n, ne), F32)),
        grid=(n // tm,),
        in_specs=[pl.BlockSpec((tm, d), lambda i: (i, 0)),
                  pl.BlockSpec((1, d), lambda i: (0, 0)),
                  pl.BlockSpec((d, ne), lambda i: (0, 0))],
        out_specs=(pl.BlockSpec((tm, d // 2), lambda i: (i, 0)),
                   pl.BlockSpec((tm, ne), lambda i: (i, 0))),
        compiler_params=_params(("parallel",)),
        name="rmsnorm_router",
    )(x2, g.reshape(1, d), w_router_pad)


def _mm_kernel(*refs, n_a, has_bias, has_res):
    a_refs = refs[:n_a]
    w_refs = refs[n_a:2 * n_a]
    idx = 2 * n_a
    bias_ref = refs[idx] if has_bias else None
    idx += int(has_bias)
    res_ref = refs[idx] if has_res else None
    idx += int(has_res)
    o_ref = refs[idx]
    wbf_refs = refs[idx + 1:idx + 1 + n_a]

    @pl.when(pl.program_id(1) == 0)
    def _():
        for w_ref, wbf_ref in zip(w_refs, wbf_refs):
            wbf_ref[...] = w_ref[...].astype(BF16)

    acc = None
    for a_ref, wbf_ref in zip(a_refs, wbf_refs):
        d = jnp.dot(a_ref[...].astype(BF16), wbf_ref[...], preferred_element_type=F32)
        acc = d if acc is None else acc + d
    if has_bias:
        acc = acc + bias_ref[...]
    if has_res:
        acc = acc + res_ref[...]
    o_ref[...] = acc.astype(o_ref.dtype)


def matmul(a_list, w, *, bias=None, res=None, out_dtype=F32, tm=1024, tn=1024, name="matmul"):
    m = a_list[0].shape[0]
    n = w.shape[1]
    n_a = len(a_list)
    k_each = a_list[0].shape[1]
    assert all(a.shape == (m, k_each) for a in a_list) and w.shape[0] == n_a * k_each
    in_specs = [pl.BlockSpec((tm, k_each), lambda j, i: (i, 0)) for _ in a_list]
    in_specs += [pl.BlockSpec((k_each, tn), functools.partial(lambda j, i, kb: (kb, j), kb=kb))
                 for kb in range(n_a)]
    args = list(a_list) + [w] * n_a
    if bias is not None:
        in_specs.append(pl.BlockSpec((1, tn), lambda j, i: (0, j)))
        args.append(bias.reshape(1, n))
    if res is not None:
        in_specs.append(pl.BlockSpec((tm, tn), lambda j, i: (i, j)))
        args.append(res)
    return pl.pallas_call(
        functools.partial(_mm_kernel, n_a=n_a, has_bias=bias is not None, has_res=res is not None),
        out_shape=jax.ShapeDtypeStruct((m, n), out_dtype),
        grid=(n // tn, m // tm),
        in_specs=in_specs,
        out_specs=pl.BlockSpec((tm, tn), lambda j, i: (i, j)),
        scratch_shapes=[pltpu.VMEM((k_each, tn), BF16) for _ in a_list],
        compiler_params=_params(("parallel", "arbitrary"), big=True),
        name=name,
    )(*args)


def _rms_scale_bf16(x, g):
    y = x * lax.rsqrt(jnp.mean(x * x, axis=-1, keepdims=True) + RMS_EPS)
    return (y * g).astype(BF16)


def _ple_kernel(h_ref, p_ref, g_ref, wg_ref, wp_ref, o_ref, wg_bf, wp_bf):
    @pl.when(pl.program_id(1) == 0)
    def _():
        wg_bf[...] = wg_ref[...].astype(BF16)
        wp_bf[...] = wp_ref[...].astype(BF16)

    tn = o_ref.shape[1]
    a = jnp.dot(_rms_scale_bf16(h_ref[...], g_ref[...]), wg_bf[...], preferred_element_type=F32)
    pp = jnp.dot(p_ref[...].astype(BF16), wp_bf[...], preferred_element_type=F32)
    col = pl.multiple_of(pl.program_id(0) * tn, tn)
    o_ref[...] = h_ref[:, pl.ds(col, tn)] + jax.nn.sigmoid(a) * pp


def ple(h2, p_all, g, w_gate_all, w_proj_all, layer, tm=512, tn=1024):
    m, d = h2.shape
    pd = p_all.shape[2]
    return pl.pallas_call(
        _ple_kernel,
        out_shape=jax.ShapeDtypeStruct((m, d), F32),
        grid=(d // tn, m // tm),
        in_specs=[pl.BlockSpec((tm, d), lambda j, i: (i, 0)),
                  pl.BlockSpec((None, tm, pd), lambda j, i: (layer, i, 0)),
                  pl.BlockSpec((1, d), lambda j, i: (0, 0)),
                  pl.BlockSpec((None, d, tn), lambda j, i: (layer, 0, j)),
                  pl.BlockSpec((None, pd, tn), lambda j, i: (layer, 0, j))],
        out_specs=pl.BlockSpec((tm, tn), lambda j, i: (i, j)),
        scratch_shapes=[pltpu.VMEM((d, tn), BF16), pltpu.VMEM((pd, tn), BF16)],
        compiler_params=_params(("parallel", "arbitrary"), big=True),
        name="ple",
    )(h2, p_all, g.reshape(1, d), w_gate_all, w_proj_all)


def _row_copy(src_hbm, row, dst, i, sem):
    return pltpu.make_async_copy(src_hbm.at[pl.ds(row, 1), :], dst.at[pl.ds(i, 1), :], sem)


def _swiglu_kernel(exp_ref, rows_ref, src_ref, tok_ref, x_ref, *rest, n_col, gather, sub_rows, nf):
    del exp_ref
    gain_ref = None if gather else rest[0]
    wg_ref, wu_ref, wd_ref, o_ref = rest[0 if gather else 1:][:4]
    scratch = rest[(4 if gather else 5):]
    t = pl.program_id(0)
    f = pl.program_id(1)
    nt = pl.num_programs(0)
    n_rows = rows_ref[t]
    tm = o_ref.shape[0]

    if not gather:
        (xbf,) = scratch

        @pl.when(f == 0)
        def _():
            h = x_ref[...]
            o_ref[...] = h
            xbf[...] = _rms_scale_bf16(h, gain_ref[...])

    if gather:
        @pl.when(f == 0)
        def _():
            o_ref[...] = jnp.zeros_like(o_ref)

        xbuf, xbf, sems = scratch
        slot = t % 2
        n_fetch = xbuf.shape[1]
        per_step = n_fetch // nf

        def start_row(tile, sl, i):
            _row_copy(x_ref, tok_ref[src_ref[tile] + i], xbuf.at[sl], i, sems.at[sl, i]).start()

        def wait_all(sl):
            def body(c, carry):
                for r in range(DMA_LOOP_UNROLL):
                    i = c * DMA_LOOP_UNROLL + r
                    _row_copy(x_ref, 0, xbuf.at[sl], i, sems.at[sl, i]).wait()
                return carry
            lax.fori_loop(0, n_fetch // DMA_LOOP_UNROLL, body, 0)

        @pl.when(jnp.logical_and(f == 0, t == 0))
        def _():
            def body(i, carry):
                start_row(0, 0, i)
                return carry
            lax.fori_loop(0, n_fetch, body, 0, unroll=DMA_LOOP_UNROLL)

        @pl.when(f == 0)
        def _():
            wait_all(slot)
            xbf[...] = _unpack_bf16_pairs(xbuf[slot, 0:tm])

        def fetch_ahead():
            nxt = jnp.minimum(t + 1, nt - 1)
            for r in range(per_step):
                start_row(nxt, 1 - slot, f * per_step + r)
    else:
        def fetch_ahead():
            pass

    def ffn_rows(n):
        fetch_ahead()
        if n == 0:
            return
        x = xbf[0:n, :]
        g = jnp.dot(x, wg_ref[...].astype(BF16), preferred_element_type=F32)
        u = jnp.dot(x, wu_ref[...].astype(BF16), preferred_element_type=F32)
        hmid = (g * jax.nn.sigmoid(g) * u).astype(BF16)
        wd = wd_ref[...].astype(BF16)
        cw = o_ref.shape[1] // n_col
        for c in range(n_col):
            y = jnp.dot(hmid, wd[:, c * cw:(c + 1) * cw], preferred_element_type=F32)
            o_ref[0:n, c * cw:(c + 1) * cw] += y

    for k in range(0 if gather else 1, tm // sub_rows + 1):
        covers = jnp.logical_and(n_rows > (k - 1) * sub_rows, n_rows <= k * sub_rows)
        pl.when(covers)(functools.partial(ffn_rows, k * sub_rows))

    if gather:
        @pl.when(jnp.logical_and(t == nt - 1, f == nf - 1))
        def _():
            wait_all(1 - slot)


def gather_rows_per_tile(tm, nf):
    per_step = -(-tm // nf)
    while (per_step * nf) % DMA_LOOP_UNROLL:
        per_step += 1
    return per_step * nf


def swiglu(x, w_gate, w_up, w_down, tile_exp, tile_rows, tile_src=None, src_rows=None, *, gain=None,
           tm=MOE_TILE, tf=FFN_F_TILE, sub_rows=MOE_TILE, name="swiglu"):
    gather = src_rows is not None
    assert gather != (gain is not None)
    m = tile_rows.shape[0] * tm
    d = w_gate.shape[1]
    dff = w_gate.shape[2]
    nf = dff // tf
    nt = m // tm
    assert nf >= 2

    def w_in_map(t, f, exp_ref, rows_ref, src_ref, tok_ref):
        return (exp_ref[t], 0, jnp.where(rows_ref[t] > 0, f, nf - 1))

    def w_out_map(t, f, exp_ref, rows_ref, src_ref, tok_ref):
        return (exp_ref[t], jnp.where(rows_ref[t] > 0, f, nf - 1), 0)

    def tile_map(t, f, exp_ref, rows_ref, src_ref, tok_ref):
        return (t, 0)

    w_specs = [pl.BlockSpec((None, d, tf), w_in_map),
               pl.BlockSpec((None, d, tf), w_in_map),
               pl.BlockSpec((None, tf, d), w_out_map)]
    if gather:
        in_specs = [pl.BlockSpec(memory_space=pl.ANY)] + w_specs
        args = [x, w_gate, w_up, w_down]
        n_fetch = gather_rows_per_tile(tm, nf)
        scratch = [pltpu.VMEM((2, n_fetch, d // 2), jnp.uint32), pltpu.VMEM((tm, d), BF16),
                   pltpu.SemaphoreType.DMA((2, n_fetch))]
    else:
        in_specs = [pl.BlockSpec((tm, d), tile_map, pipeline_mode=pl.Buffered(1)),
                    pl.BlockSpec((1, d), lambda t, f, e, r, s, k: (0, 0))] + w_specs
        args = [x, gain.reshape(1, d), w_gate, w_up, w_down]
        scratch = [pltpu.VMEM((tm, d), BF16)]
        tile_src = jnp.zeros((nt,), jnp.int32)
        src_rows = jnp.zeros((1,), jnp.int32)
    grid_spec = pltpu.PrefetchScalarGridSpec(
        num_scalar_prefetch=4,
        grid=(nt, nf),
        in_specs=in_specs,
        out_specs=pl.BlockSpec((tm, d), tile_map, pipeline_mode=pl.Buffered(1)),
        scratch_shapes=scratch,
    )
    return pl.pallas_call(
        functools.partial(_swiglu_kernel, n_col=4, gather=gather, sub_rows=sub_rows, nf=nf),
        out_shape=jax.ShapeDtypeStruct((m, d), F32),
        grid_spec=grid_spec,
        compiler_params=_params(("arbitrary", "arbitrary"), big=True),
        name=name,
    )(tile_exp, tile_rows, tile_src, src_rows, *args)


COMBINE_TILE = 512


def _combine_kernel(d0_ref, d1_ref, ys_hbm, gate_ref, h_ref, hout_ref, buf, sems):
    i = pl.program_id(0)
    n = pl.num_programs(0)
    tt = h_ref.shape[0]
    slot = i % 2
    dests = (d0_ref, d1_ref)

    def start_tile(tile, sl):
        def body(r, c):
            for k, d_ref in enumerate(dests):
                _row_copy(ys_hbm, d_ref[tile * tt + r], buf.at[sl, k], r, sems.at[sl, k, r]).start()
            return c
        lax.fori_loop(0, tt, body, 0, unroll=DMA_LOOP_UNROLL)

    @pl.when(i == 0)
    def _():
        start_tile(0, 0)

    @pl.when(i + 1 < n)
    def _():
        start_tile(i + 1, 1 - slot)

    def wait_body(r, c):
        for k in range(TOP_K):
            _row_copy(ys_hbm, 0, buf.at[slot, k], r, sems.at[slot, k, r]).wait()
        return c
    lax.fori_loop(0, tt, wait_body, 0, unroll=DMA_LOOP_UNROLL)

    gates = gate_ref[...]
    hout_ref[...] = h_ref[...] + (buf[slot, 0] * gates[:, 0:1] + buf[slot, 1] * gates[:, 1:2])


def moe_combine(h2, ys, dest2, gates, tt=COMBINE_TILE):
    n, d = h2.shape
    grid_spec = pltpu.PrefetchScalarGridSpec(
        num_scalar_prefetch=2,
        grid=(n // tt,),
        in_specs=[pl.BlockSpec(memory_space=pl.ANY),
                  pl.BlockSpec((tt, TOP_K), lambda i, a, b: (i, 0)),
                  pl.BlockSpec((tt, d), lambda i, a, b: (i, 0))],
        out_specs=pl.BlockSpec((tt, d), lambda i, a, b: (i, 0)),
        scratch_shapes=[pltpu.VMEM((2, TOP_K, tt, d), F32), pltpu.SemaphoreType.DMA((2, TOP_K, tt))],
    )
    return pl.pallas_call(
        _combine_kernel,
        out_shape=jax.ShapeDtypeStruct((n, d), F32),
        grid_spec=grid_spec,
        compiler_params=_params(("arbitrary",), big=True),
        name="moe_combine",
    )(dest2[:, 0], dest2[:, 1], ys, gates, h2)


def _t5_bucket(rel):
    half = T5_BUCKETS // 2
    max_exact = half // 2
    n = jnp.abs(rel)
    log_ratio = jnp.log(jnp.maximum(n, 1).astype(F32) / max_exact) / math.log(T5_MAX_DIST / max_exact)
    large = jnp.minimum(max_exact + (log_ratio * (half - max_exact)).astype(jnp.int32), half - 1)
    return jnp.where(rel > 0, half, 0) + jnp.where(n < max_exact, n, large)


def _window_bias_table(t5_bias):
    i = jnp.arange(A_BLOCK)[:, None]
    j = jnp.arange(3 * A_BLOCK)[None, :]
    rel = j - A_BLOCK - i
    onehot = (_t5_bucket(rel)[None] == jnp.arange(T5_BUCKETS)[:, None, None]).astype(F32)
    bias = jnp.einsum('bh,bij->hij', t5_bias.astype(F32), onehot, precision=lax.Precision.HIGHEST)
    return jnp.where((jnp.abs(rel) <= A_BLOCK)[None], bias, NEG_INF)


def _window_kernel(sink_ref, q_ref, kv_ref, bias_ref, o_ref, *, nb):
    n = pl.program_id(1)
    scale = HEAD_DIM ** -0.5
    starts = (jnp.maximum(n - 1, 0), n, jnp.minimum(n + 1, nb - 1))
    kv = [kv_ref[pl.ds(pl.multiple_of(s * A_BLOCK, A_BLOCK), A_BLOCK), :] for s in starts]
    col = lax.broadcasted_iota(jnp.int32, (A_BLOCK, 3 * A_BLOCK), 1)
    edge_ok = jnp.logical_and(jnp.logical_or(n > 0, col >= A_BLOCK),
                              jnp.logical_or(n < nb - 1, col < 2 * A_BLOCK))
    kv = jnp.concatenate(kv, axis=0)
    q = q_ref[...]
    lo = lax.broadcasted_iota(jnp.int32, (1, 2 * HEAD_DIM), 1) < HEAD_DIM
    sel = (jnp.where(lo, scale, 0.0).astype(BF16), jnp.where(lo, 0.0, scale).astype(BF16))
    pairs_per_kv = A_GROUP // 2
    kk, vv = [], []
    for kh in range(A_KV_HEADS):
        k1 = kv[:, kh * HEAD_DIM:(kh + 1) * HEAD_DIM]
        v1 = kv[:, A_KV_WIDTH + kh * HEAD_DIM:A_KV_WIDTH + (kh + 1) * HEAD_DIM]
        kk.append(jnp.concatenate([k1, k1], axis=1))
        vv.append(jnp.concatenate([v1, v1], axis=1))
    scores = []
    for pr in range(A_Q_HEADS // 2):
        q2 = q[:, pr * 2 * HEAD_DIM:(pr + 1) * 2 * HEAD_DIM]
        qq = jnp.concatenate([q2 * sel[0], q2 * sel[1]], axis=0)
        s2 = lax.dot_general(qq, kk[pr // pairs_per_kv], (((1,), (1,)), ((), ())),
                             preferred_element_type=F32)
        for i in range(2):
            scores.append(jnp.where(edge_ok, s2[i * A_BLOCK:(i + 1) * A_BLOCK] + bias_ref[2 * pr + i],
                                    NEG_INF))
    probs, dens = [], []
    for h, s in enumerate(scores):
        sink = sink_ref[h]
        mx = jnp.maximum(jnp.max(s, axis=-1, keepdims=True), sink)
        p = jnp.exp(s - mx)
        dens.append(jnp.sum(p, axis=-1, keepdims=True) + jnp.exp(sink - mx))
        probs.append(p.astype(BF16))
    for pr in range(A_Q_HEADS // 2):
        v2 = vv[pr // pairs_per_kv]
        oa = jnp.dot(probs[2 * pr], v2, preferred_element_type=F32) / dens[2 * pr]
        ob = jnp.dot(probs[2 * pr + 1], v2, preferred_element_type=F32) / dens[2 * pr + 1]
        o_ref[:, pr * 2 * HEAD_DIM:(pr + 1) * 2 * HEAD_DIM] = jnp.where(lo, oa, ob).astype(o_ref.dtype)


def window_attention(proj_a, t5_bias, sink):
    b, s, _ = proj_a.shape
    nb = s // A_BLOCK
    kvw = 2 * A_KV_WIDTH
    grid_spec = pltpu.PrefetchScalarGridSpec(
        num_scalar_prefetch=0,
        grid=(b, nb),
        in_specs=[pl.BlockSpec(memory_space=pltpu.SMEM),
                  pl.BlockSpec((None, A_BLOCK, A_WIDTH), lambda bi, n: (bi, n, 0)),
                  pl.BlockSpec((None, s, kvw), lambda bi, n: (bi, 0, A_WIDTH // kvw)),
                  pl.BlockSpec((A_Q_HEADS, A_BLOCK, 3 * A_BLOCK), lambda bi, n: (0, 0, 0))],
        out_specs=pl.BlockSpec((None, A_BLOCK, A_WIDTH), lambda bi, n: (bi, n, 0)),
    )
    return pl.pallas_call(
        functools.partial(_window_kernel, nb=nb),
        out_shape=jax.ShapeDtypeStruct((b, s, A_WIDTH), BF16),
        grid_spec=grid_spec,
        compiler_params=_params(("parallel", "arbitrary")),
        name="window_attention",
    )(sink.astype(F32), proj_a, proj_a, _window_bias_table(t5_bias))


NA_HEAD_GROUP = 16


def _na_bias_table(rpb, rows):
    kh = min(NA_WIN_H, rows)
    kw = NA_WIN_W
    c = jnp.arange(GRID_W)
    cs = jnp.clip(c - kw // 2, 0, GRID_W - kw)
    col_ok = (c[None] >= cs[:, None]) & (c[None] < cs[:, None] + kw)
    col_off = jnp.clip(c[None] - c[:, None], -(kw - 1), kw - 1) + kw - 1
    onehot = (col_off[None] == jnp.arange(2 * kw - 1)[:, None, None]).astype(F32)
    by_col = jnp.einsum('hrc,cqk->hqrk', rpb.astype(F32), onehot, precision=lax.Precision.HIGHEST)
    by_col = jnp.where(col_ok[None, :, None, :], by_col, NEG_INF)
    tabs = [by_col[:, :, NA_WIN_H - 1 - d:NA_WIN_H - 1 - d + kh] for d in range(kh)]
    return jnp.stack(tabs, axis=0).reshape(kh, rpb.shape[0], GRID_W, kh * GRID_W)


def _na_kernel(q_ref, k_ref, v_ref, bias_ref, o_ref, *, rows):
    r = pl.program_id(2)
    kh = min(NA_WIN_H, rows)
    scale = HEAD_DIM ** -0.5
    start = pl.multiple_of(jnp.clip(r - kh // 2, 0, rows - kh) * GRID_W, GRID_W)
    k = k_ref[pl.ds(start, kh * GRID_W), :]
    v = v_ref[pl.ds(start, kh * GRID_W), :]
    q = q_ref[...]
    lo = lax.broadcasted_iota(jnp.int32, (1, 2 * HEAD_DIM), 1) < HEAD_DIM
    sel = (jnp.where(lo, scale, 0.0).astype(BF16), jnp.where(lo, 0.0, scale).astype(BF16))
    n_pairs = NA_HEAD_GROUP // 2
    scores = []
    for pr in range(n_pairs):
        sl = slice(pr * 2 * HEAD_DIM, (pr + 1) * 2 * HEAD_DIM)
        q2 = q[:, sl]
        qq = jnp.concatenate([q2 * sel[0], q2 * sel[1]], axis=0)
        s2 = lax.dot_general(qq, k[:, sl], (((1,), (1,)), ((), ())), preferred_element_type=F32)
        for i in range(2):
            scores.append(s2[i * GRID_W:(i + 1) * GRID_W] + bias_ref[2 * pr + i])
    probs, dens = [], []
    for s in scores:
        mx = jnp.max(s, axis=-1, keepdims=True)
        p = jnp.exp(s - mx)
        dens.append(jnp.sum(p, axis=-1, keepdims=True))
        probs.append(p.astype(BF16))
    for pr in range(n_pairs):
        sl = slice(pr * 2 * HEAD_DIM, (pr + 1) * 2 * HEAD_DIM)
        oa = jnp.dot(probs[2 * pr], v[:, sl], preferred_element_type=F32) / dens[2 * pr]
        ob = jnp.dot(probs[2 * pr + 1], v[:, sl], preferred_element_type=F32) / dens[2 * pr + 1]
        o_ref[:, sl] = jnp.where(lo, oa, ob).astype(o_ref.dtype)


def neighbourhood_attention(proj_n, rpb):
    b, s, _ = proj_n.shape
    rows = s // GRID_W
    kh = min(NA_WIN_H, rows)
    gw = NA_HEAD_GROUP * HEAD_DIM
    ng = B_WIDTH // gw

    def bias_map(bi, g, r):
        return (r - jnp.clip(r - kh // 2, 0, rows - kh), g, 0, 0)

    return pl.pallas_call(
        functools.partial(_na_kernel, rows=rows),
        out_shape=jax.ShapeDtypeStruct((b, s, B_WIDTH), BF16),
        grid=(b, ng, rows),
        in_specs=[pl.BlockSpec((None, GRID_W, gw), lambda bi, g, r: (bi, r, g)),
                  pl.BlockSpec((None, s, gw), lambda bi, g, r: (bi, 0, ng + g)),
                  pl.BlockSpec((None, s, gw), lambda bi, g, r: (bi, 0, 2 * ng + g)),
                  pl.BlockSpec((None, NA_HEAD_GROUP, GRID_W, kh * GRID_W), bias_map)],
        out_specs=pl.BlockSpec((None, GRID_W, gw), lambda bi, g, r: (bi, r, g)),
        compiler_params=_params(("parallel", "parallel", "arbitrary"), big=True),
        name="neighbourhood_attention",
    )(proj_n, proj_n, proj_n, _na_bias_table(rpb, rows))


CONV_HALO = 16
CONV_COL_CHUNK = 512


def _hyena_in_kernel(prev_ref, cur_ref, next_ref, w_ref, b_ref, cw_ref, cb_ref, o_ref, wbf_ref, *,
                     tiles_per_seq):
    i = pl.program_id(1)

    @pl.when(i == 0)
    def _():
        wbf_ref[...] = w_ref[...].astype(BF16)

    tm = cur_ref.shape[0]
    rows = tm + 2 * CONV_HALO
    a = jnp.concatenate([prev_ref[...], cur_ref[...], next_ref[...]], axis=0)
    pos = i % tiles_per_seq
    row = lax.broadcasted_iota(jnp.int32, (rows, 1), 0)
    kill_up = jnp.logical_and(pos == 0, row == CONV_HALO)
    kill_dn = jnp.logical_and(pos == tiles_per_seq - 1, row == CONV_HALO + tm - 1)
    for c in range(o_ref.shape[1] // CONV_COL_CHUNK):
        cs = slice(c * CONV_COL_CHUNK, (c + 1) * CONV_COL_CHUNK)
        p = jnp.dot(a, wbf_ref[:, cs], preferred_element_type=F32) + b_ref[:, cs]
        up = jnp.where(kill_up, 0.0, pltpu.roll(p, 1, 0))
        dn = jnp.where(kill_dn, 0.0, pltpu.roll(p, rows - 1, 0))
        y = up * cw_ref[0:1, cs] + p * cw_ref[1:2, cs] + dn * cw_ref[2:3, cs] + cb_ref[:, cs]
        o_ref[:, cs] = y[CONV_HALO:CONV_HALO + tm]


def hyena_in_conv(hn, w_in, b_in, conv_w, conv_b, seq, tm=1024, tn=1024):
    m, k = hn.shape
    n = w_in.shape[1]
    hb = tm // CONV_HALO
    last_halo = m // CONV_HALO - 1
    return pl.pallas_call(
        functools.partial(_hyena_in_kernel, tiles_per_seq=seq // tm),
        out_shape=jax.ShapeDtypeStruct((m, n), F32),
        grid=(n // tn, m // tm),
        in_specs=[pl.BlockSpec((CONV_HALO, k), lambda j, i: (jnp.maximum(i * hb - 1, 0), 0)),
                  pl.BlockSpec((tm, k), lambda j, i: (i, 0)),
                  pl.BlockSpec((CONV_HALO, k), lambda j, i: (jnp.minimum((i + 1) * hb, last_halo), 0)),
                  pl.BlockSpec((k, tn), lambda j, i: (0, j)),
                  pl.BlockSpec((1, tn), lambda j, i: (0, j)),
                  pl.BlockSpec((3, tn), lambda j, i: (0, j)),
                  pl.BlockSpec((1, tn), lambda j, i: (0, j))],
        out_specs=pl.BlockSpec((tm, tn), lambda j, i: (i, j)),
        scratch_shapes=[pltpu.VMEM((k, tn), BF16)],
        compiler_params=_params(("parallel", "arbitrary"), big=True),
        name="hyena_in_conv",
    )(hn, hn, hn, w_in, b_in.reshape(1, n), conv_w, conv_b.reshape(1, n))


def _filter_stage1_kernel(z_ref, t_ref, fw1_ref, fb1_ref, fw2_ref, fb2_ref, fr_ref, w3f0_ref, w3f1_ref,
                          w3b0_ref, w3b1_ref, delta_ref, f1_ref, y_ref, *, seq):
    j = pl.program_id(0)
    hi = lax.Precision.HIGHEST
    fr = fr_ref[...]
    hid = jnp.sin(fr * (jnp.dot(z_ref[...], fw1_ref[...], preferred_element_type=F32, precision=hi)
                        + fb1_ref[...]))
    hid = jnp.sin(fr * (jnp.dot(hid, fw2_ref[...], preferred_element_type=F32, precision=hi)
                        + fb2_ref[...]))
    hid_bf = hid.astype(BF16)
    r = lax.broadcasted_iota(jnp.int32, (hid.shape[0], 1), 0)
    u = j * FFT_CHUNK + r // FFT_N2 + FFT_N1 * (r % FFT_N2)
    decay = jnp.where(u == seq, 0.0, jnp.exp(-t_ref[...] * delta_ref[...]))
    for o, (wf_ref, wb_ref) in enumerate(((w3f0_ref, w3b0_ref), (w3f1_ref, w3b1_ref))):
        fwd = jnp.dot(hid_bf, wf_ref[...].astype(BF16), preferred_element_type=F32)
        bwd = jnp.dot(hid_bf, wb_ref[...].astype(BF16), preferred_element_type=F32)
        taps = (jnp.where(u > seq, bwd, jnp.where(u == 0, fwd + bwd, fwd)) * decay).astype(BF16)
        for jj in range(FFT_CHUNK):
            x = taps[jj * FFT_N2:(jj + 1) * FFT_N2]
            y_ref[o, jj] = _pack_bf16_pairs(jnp.dot(f1_ref[jj], x, preferred_element_type=F32))


def hyena_filter_stage1(seq, fw1, fb1, fw2, fb2, freq, fw3, width, consts, cb=FFT_CB):
    _, f1_real, _, _, _ = consts
    n = 2 * seq
    n1, n2, ch = FFT_N1, FFT_N2, FFT_CHUNK
    assert fw3.shape[1] == 4 * width
    r = jnp.arange(n)
    u = (r // (ch * n2)) * ch + (r // n2) % ch + n1 * (r % n2)
    pos = jnp.where(u <= seq, u, n - u)
    pos = jnp.where(u == seq, 0, pos)
    bands = (HYENA_EMB - 1) // 2
    f = jnp.linspace(1e-4, bands - 1, bands, dtype=F32)[None]
    t = (pos.astype(F32) / (seq - 1))[:, None]
    w = ((2.0 * math.pi / seq) * pos.astype(F32))[:, None]
    z = jnp.concatenate([t, jnp.cos(f * w), -jnp.sin(f * w)], axis=-1)
    emb_pad = 40
    z = jnp.pad(z, ((0, 0), (0, emb_pad - HYENA_EMB)))
    fw1p = jnp.pad(fw1.astype(F32), ((0, emb_pad - HYENA_EMB), (0, 0)))
    fh = fw1.shape[1]
    max_decay = math.log(HYENA_DECAY_TARGET) / HYENA_FAST_PCT
    min_decay = math.log(HYENA_DECAY_TARGET) / HYENA_SLOW_PCT
    deltas = jnp.abs(jnp.linspace(min_decay, max_decay, width, dtype=F32))[None]
    ncb = width // cb
    rows = ch * n2

    def small(shape):
        return pl.BlockSpec(shape, lambda j, k: (0, 0))

    def w3_spec(blk):
        return pl.BlockSpec((fh, cb), lambda j, k: (0, blk * ncb + k))

    return pl.pallas_call(
        functools.partial(_filter_stage1_kernel, seq=seq),
        out_shape=jax.ShapeDtypeStruct((2, n1, 2 * n2, width // 2), jnp.uint32),
        grid=(n1 // ch, ncb),
        in_specs=[pl.BlockSpec((rows, emb_pad), lambda j, k: (j, 0)),
                  pl.BlockSpec((rows, 1), lambda j, k: (j, 0)),
                  small((emb_pad, fh)), small((1, fh)), small((fh, fh)), small((1, fh)), small((1, fh)),
                  w3_spec(0), w3_spec(1), w3_spec(2), w3_spec(3),
                  pl.BlockSpec((1, cb), lambda j, k: (0, k)),
                  pl.BlockSpec((ch, 2 * n2, n2), lambda j, k: (j, 0, 0))],
        out_specs=pl.BlockSpec((2, ch, 2 * n2, cb // 2), lambda j, k: (0, j, 0, k)),
        compiler_params=_params(("parallel", "parallel"), big=True),
        name="hyena_filter_stage1",
    )(z, t, fw1p, fb1.reshape(1, fh).astype(F32), fw2.astype(F32), fb2.reshape(1, fh).astype(F32),
      freq.reshape(1, fh).astype(F32), fw3, fw3, fw3, fw3, deltas, f1_real)


def _dft_constants():
    n1, n2 = FFT_N1, FFT_N2
    n = n1 * n2
    a1 = np.arange(n1)
    a2 = np.arange(n2)
    half = n2 // 2

    def cplx_block(w):
        return np.block([[w.real, -w.imag], [w.imag, w.real]])

    ang = (a2[None, :, None] * a2[None, None, :] / n2) + (a1[:, None, None] * a2[None, :, None] / n)
    w1 = np.exp(-2j * np.pi * ang)
    f1_pad = np.stack([cplx_block(w1[i][:, :half]) for i in range(n1)])
    f1_real = np.stack([np.concatenate([w1[i].real, w1[i].imag], axis=0) for i in range(n1)])
    w2 = np.exp(-2j * np.pi * (a1[:, None] * a1[None, :]) / n1)
    f2 = cplx_block(w2)
    ang = (a1[None, :, None] * a1[None, None, :] / n1) + (a2[:, None, None] * a1[None, :, None] / n)
    g2 = np.stack([cplx_block(m_) for m_ in np.exp(2j * np.pi * ang)])
    wg1 = np.exp(2j * np.pi * (a2[:half, None] * a2[None, :]) / n2) / n
    g1 = cplx_block(wg1)
    to = lambda x: jnp.asarray(x.astype(np.float32)).astype(BF16)
    return to(f1_pad), to(f1_real), to(f2), to(g2), to(g1)


def _fft_stage1_block(x_re, x_im, f_ref, y_ref):
    xr = pltpu.einshape("mjc->jmc", x_re)
    xi = pltpu.einshape("mjc->jmc", x_im)
    for j in range(FFT_CHUNK):
        xc = jnp.concatenate([xr[j], xi[j]], axis=0).astype(BF16)
        y_ref[j] = _pack_bf16_pairs(jnp.dot(f_ref[j], xc, preferred_element_type=F32))


def _fft_stage1_kernel(x_ref, f_ref, y_ref):
    _fft_stage1_block(x_ref[0], x_ref[1], f_ref, y_ref)


def _fft_stage2_kernel(yr_ref, yi_ref, f2_ref, h_ref):
    yr = pltpu.einshape("nkc->knc", yr_ref[...])
    yi = pltpu.einshape("nkc->knc", yi_ref[...])
    for j in range(FFT_CHUNK):
        yc = _unpack_bf16_pairs(jnp.concatenate([yr[j], yi[j]], axis=0))
        h_ref[j] = _pack_bf16_pairs(jnp.dot(f2_ref[...], yc, preferred_element_type=F32))


def _fft_mid_kernel(yr_ref, yi_ref, f2_ref, h_ref, g2_ref, q_ref):
    yr = pltpu.einshape("nkc->knc", yr_ref[...])
    yi = pltpu.einshape("nkc->knc", yi_ref[...])
    n1 = FFT_N1
    spec = [jnp.dot(f2_ref[...], _unpack_bf16_pairs(jnp.concatenate([yr[j], yi[j]], axis=0)),
                    preferred_element_type=F32) for j in range(FFT_CHUNK)]
    prod = []
    for j, z in enumerate(spec):
        hf = _unpack_pairs_f32(h_ref[j])
        zr, zi, hr, hi = z[:n1], z[n1:], hf[:n1], hf[n1:]
        prod.append(jnp.concatenate([zr * hr - zi * hi, zr * hi + zi * hr], axis=0).astype(BF16))
    for j, pc in enumerate(prod):
        q_ref[j] = _pack_bf16_pairs(jnp.dot(g2_ref[j], pc, preferred_element_type=F32))


def _fft_last_kernel(qr_ref, qi_ref, g1_ref, gate_ref, zf_ref, fb_ref, *rest, feeds_next):
    f1_ref, o_ref, y_ref = rest if feeds_next else (None, rest[0], None)
    qr = pltpu.einshape("kjc->jkc", qr_ref[...])
    qi = pltpu.einshape("kjc->jkc", qi_ref[...])
    half = FFT_N2 // 2
    conv = []
    for j in range(FFT_CHUNK):
        qc = _unpack_bf16_pairs(jnp.concatenate([qr[j], qi[j]], axis=0))
        conv.append(jnp.dot(g1_ref[...], qc, preferred_element_type=F32))
    conv = jnp.stack(conv, axis=0)
    fb = fb_ref[...]
    out = []
    for bi in range(2):
        cb = pltpu.einshape("jmc->mjc", conv[:, bi * half:(bi + 1) * half])
        out.append(gate_ref[bi] * (cb + zf_ref[bi] * fb))
        o_ref[bi] = out[bi]
    if feeds_next:
        _fft_stage1_block(out[0], out[1], f1_ref, y_ref)


def hyena_filter_spectrum(y, consts, cb=FFT_CB):
    _, _, f2, _, _ = consts
    no = y.shape[0]
    c = 2 * y.shape[-1]
    n1, n2, ch = FFT_N1, FFT_N2, FFT_CHUNK
    nk = n2 // ch
    return pl.pallas_call(
        _fft_stage2_kernel,
        out_shape=jax.ShapeDtypeStruct((no, n2, 2 * n1, c // 2), jnp.uint32),
        grid=(no, nk, c // cb),
        in_specs=[pl.BlockSpec((None, n1, ch, cb // 2), lambda o, j, k: (o, 0, j, k)),
                  pl.BlockSpec((None, n1, ch, cb // 2), lambda o, j, k: (o, 0, nk + j, k)),
                  pl.BlockSpec((2 * n1, 2 * n1), lambda o, j, k: (0, 0))],
        out_specs=pl.BlockSpec((None, ch, 2 * n1, cb // 2), lambda o, j, k: (o, j, 0, k)),
        compiler_params=_params(("parallel", "parallel", "parallel"), big=True),
        name="fft_filter_stage2",
    )(y, y, f2)


def hyena_long_conv_gate(zsrc, z_col, gate_src, gate_col, hf, order, fbias, consts, y=None,
                         feeds_next=False, cb=FFT_CB):
    f1_pad, _, f2, g2, g1 = consts
    n1, n2, ch = FFT_N1, FFT_N2, FFT_CHUNK
    c = 2 * hf.shape[-1]
    ncb = c // cb
    half = n2 // 2
    y_shape = jax.ShapeDtypeStruct((n1, 2 * n2, c // 2), jnp.uint32)
    f1_spec = pl.BlockSpec((ch, 2 * n2, n2), lambda j, k: (j, 0, 0))
    y_spec = pl.BlockSpec((ch, 2 * n2, cb // 2), lambda j, k: (j, 0, k))
    if y is None:
        y = pl.pallas_call(
            _fft_stage1_kernel,
            out_shape=y_shape,
            grid=(n1 // ch, ncb),
            in_specs=[pl.BlockSpec((2, half, ch, cb), lambda j, k: (0, 0, j, z_col * ncb + k)), f1_spec],
            out_specs=y_spec,
            compiler_params=_params(("parallel", "parallel"), big=True),
            name="fft_stage1",
        )(zsrc, f1_pad)
    nk = n2 // ch
    q = pl.pallas_call(
        _fft_mid_kernel,
        out_shape=jax.ShapeDtypeStruct((n2, 2 * n1, c // 2), jnp.uint32),
        grid=(nk, ncb),
        in_specs=[pl.BlockSpec((n1, ch, cb // 2), lambda j, k: (0, j, k)),
                  pl.BlockSpec((n1, ch, cb // 2), lambda j, k: (0, nk + j, k)),
                  pl.BlockSpec((2 * n1, 2 * n1), lambda j, k: (0, 0)),
                  pl.BlockSpec((None, ch, 2 * n1, cb // 2), lambda j, k: (order, j, 0, k)),
                  pl.BlockSpec((ch, 2 * n1, 2 * n1), lambda j, k: (j, 0, 0))],
        out_specs=pl.BlockSpec((ch, 2 * n1, cb // 2), lambda j, k: (j, 0, k)),
        compiler_params=_params(("parallel", "parallel"), big=True),
        name="fft_mid",
    )(y, y, f2, hf, g2)
    nj = n1 // ch
    in_specs = [pl.BlockSpec((n2, ch, cb // 2), lambda j, k: (0, j, k)),
                pl.BlockSpec((n2, ch, cb // 2), lambda j, k: (0, nj + j, k)),
                pl.BlockSpec((2 * half, 2 * n2), lambda j, k: (0, 0)),
                pl.BlockSpec((2, half, ch, cb), lambda j, k: (0, 0, j, gate_col * ncb + k)),
                pl.BlockSpec((2, half, ch, cb), lambda j, k: (0, 0, j, z_col * ncb + k)),
                pl.BlockSpec((None, 1, cb), lambda j, k: (order, 0, k))]
    args = [q, q, g1, gate_src, zsrc, fbias.reshape(fbias.shape[0], 1, c)]
    out_shape = jax.ShapeDtypeStruct((2, half, n1, c), F32)
    out_spec = pl.BlockSpec((2, half, ch, cb), lambda j, k: (0, 0, j, k))
    if feeds_next:
        in_specs.append(f1_spec)
        args.append(f1_pad)
        out_shape, out_spec = (out_shape, y_shape), (out_spec, y_spec)
    res = pl.pallas_call(
        functools.partial(_fft_last_kernel, feeds_next=feeds_next),
        out_shape=out_shape,
        grid=(nj, ncb),
        in_specs=in_specs,
        out_specs=out_spec,
        compiler_params=_params(("parallel", "parallel"), big=True),
        name="fft_last",
    )(*args)
    return res if feeds_next else (res, None)


def hyena_mixer(hn, h_res, w_in, b_in, conv_w, conv_b, fw1, fb1, fw2, fb2, freq, fw3, fbias, w_out,
                batch, seq):
    width = w_out.shape[0]
    assert batch == 2 and 2 * seq == FFT_N1 * FFT_N2
    sc = hyena_in_conv(hn, w_in, b_in, conv_w, conv_b, seq)
    consts = _dft_constants()
    hf = hyena_filter_spectrum(hyena_filter_stage1(seq, fw1, fb1, fw2, fb2, freq, fw3, width, consts),
                               consts)
    sc4 = sc.reshape(batch, FFT_N2 // 2, FFT_N1, 3 * width)
    zf1, y1 = hyena_long_conv_gate(sc4, 2, sc4, 0, hf, 0, fbias, consts, feeds_next=True)
    zf2, _ = hyena_long_conv_gate(zf1, 0, sc4, 1, hf, 1, fbias, consts, y=y1)
    return matmul([zf2.reshape(batch * seq, width)], w_out, res=h_res, tm=512, tn=1024,
                  name="hyena_out")


def moe_swiglu(hn2_packed, logits, wg, wu, wd):
    n = hn2_packed.shape[0]
    top_v, top_i = lax.top_k(logits, TOP_K)
    gates = jax.nn.softmax(top_v, axis=-1)
    e_flat = top_i.reshape(-1).astype(jnp.int32)
    nk = n * TOP_K
    onehot = (e_flat[:, None] == jnp.arange(N_EXPERTS, dtype=jnp.int32)[None]).astype(jnp.int32)
    csum = jnp.cumsum(onehot, axis=0)
    rank = jnp.take_along_axis(csum, e_flat[:, None], axis=1)[:, 0] - 1
    counts = csum[-1]
    padded = ((counts + MOE_TILE - 1) // MOE_TILE) * MOE_TILE
    pad_end = jnp.cumsum(padded)
    pad_start = pad_end - padded
    dest = pad_start[e_flat] + rank
    p_rows = nk + N_EXPERTS * MOE_TILE
    order = jnp.argsort(e_flat, stable=True).astype(jnp.int32)
    nf = wg.shape[2] // MOE_F_TILE
    sorted_tok = jnp.pad(order // TOP_K, (0, gather_rows_per_tile(MOE_TILE, nf)))
    start = jnp.cumsum(counts) - counts
    nt = p_rows // MOE_TILE
    tile_start = jnp.arange(nt, dtype=jnp.int32) * MOE_TILE
    tile_used = tile_start < pad_end[-1]
    tile_exp = jnp.minimum(jnp.searchsorted(pad_end, tile_start, side='right'), N_EXPERTS - 1).astype(jnp.int32)
    tile_rows = jnp.clip(counts[tile_exp] - (tile_start - pad_start[tile_exp]), 0, MOE_TILE)
    tile_rows = jnp.where(tile_used, tile_rows, 0).astype(jnp.int32)
    last_exp = tile_exp[jnp.maximum(jnp.sum(tile_used.astype(jnp.int32)) - 1, 0)]
    tile_src = jnp.where(tile_used, start[tile_exp] + (tile_start - pad_start[tile_exp]), 0)
    tile_exp = jnp.where(tile_used, tile_exp, last_exp)
    ys = swiglu(hn2_packed, wg, wu, wd, tile_exp, tile_rows, tile_src.astype(jnp.int32), sorted_tok,
                tf=MOE_F_TILE, sub_rows=FFN_SUB_ROWS, name="swiglu_experts")
    return ys, dest.reshape(n, TOP_K), gates


def kernel(x, p, ln_mix, ln_ffn, ln_ple, final_norm, t5_bias, w_attn_in, w_attn_out, attn_sink, na_rpb, w_ffn_gate, w_ffn_up, w_ffn_down, w_hy_in, b_hy_in, w_hy_conv, b_hy_conv, w_hy_f1, b_hy_f1, w_hy_f2, b_hy_f2, hy_freq, w_hy_f3, hy_bias, w_hy_out, w_router, w_exp_gate, w_exp_up, w_exp_down, w_ple_proj, w_ple_gate):
    batch, seq, d = x.shape
    n = batch * seq
    depth = ln_mix.shape[0]
    h = x.reshape(n, d)
    for i in range(depth):
        li = i // 2
        hn = rmsnorm(h, ln_mix[i], BF16)
        if i % 2 == 0:
            na_off = A_WIDTH + 2 * A_KV_WIDTH
            w_in = w_attn_in[li]
            proj_a = matmul([hn], w_in[:, :na_off], out_dtype=BF16, tn=640, name="attn_in_a")
            proj_n = matmul([hn], w_in[:, na_off:], out_dtype=BF16, name="attn_in_n")
            oa = window_attention(proj_a.reshape(batch, seq, -1), t5_bias, attn_sink[li])
            ob = neighbourhood_attention(proj_n.reshape(batch, seq, -1), na_rpb[li])
            h = matmul([oa.reshape(n, A_WIDTH), ob.reshape(n, B_WIDTH)], w_attn_out[li], res=h,
                       name="attn_out")
            nt = n // MOE_TILE
            h = swiglu(h, w_ffn_gate, w_ffn_up, w_ffn_down,
                       jnp.full((nt,), li, jnp.int32), jnp.full((nt,), MOE_TILE, jnp.int32),
                       gain=ln_ffn[i], name="swiglu_dense")
        else:
            h = hyena_mixer(hn, h, w_hy_in[li], b_hy_in[li], w_hy_conv[li], b_hy_conv[li],
                            w_hy_f1[li], b_hy_f1[li], w_hy_f2[li], b_hy_f2[li], hy_freq[li],
                            w_hy_f3[li], hy_bias[li], w_hy_out[li], batch, seq)
            wr_pad = jnp.pad(w_router[li].astype(F32), ((0, 0), (0, 128 - N_EXPERTS)))
            hn2_packed, logits = rmsnorm_router(h, ln_ffn[i], wr_pad)
            ys, dest2, gates = moe_swiglu(hn2_packed, logits[:, :N_EXPERTS], w_exp_gate[li],
                                          w_exp_up[li], w_exp_down[li])
            h = moe_combine(h, ys, dest2, gates)
        h = ple(h, p.reshape(depth, n, -1), ln_ple[i], w_ple_gate, w_ple_proj, i)
    return rmsnorm(h, final_norm, F32).reshape(batch, seq, d)
```

```python
import functools
import math

import jax
import jax.numpy as jnp
import numpy as np
from jax import lax
from jax.experimental import pallas as pl
from jax.experimental.pallas import tpu as pltpu

F32 = jnp.float32
BF16 = jnp.bfloat16
NEG_INF = -1e30
RMS_EPS = 1e-6

V7X_VMEM_LIMIT_BYTES = 56 * 1024 * 1024

HEAD_DIM = 64
A_Q_HEADS = 16
A_KV_HEADS = 2
A_GROUP = A_Q_HEADS // A_KV_HEADS
A_BLOCK = 128
T5_BUCKETS = 32
T5_MAX_DIST = 128
B_HEADS = 16
GRID_W = 64
NA_WIN_H = 8
NA_WIN_W = 16
A_WIDTH = A_Q_HEADS * HEAD_DIM
A_KV_WIDTH = A_KV_HEADS * HEAD_DIM
B_WIDTH = B_HEADS * HEAD_DIM
N_EXPERTS = 8
TOP_K = 2
HYENA_EMB = 33
HYENA_DECAY_TARGET = 1e-2
HYENA_FAST_PCT = 0.3
HYENA_SLOW_PCT = 1.5

FFT_N1 = 64
FFT_N2 = 128
FFT_CHUNK = 8
FFT_CB = 1024

MOE_TILE = 1024
FFN_F_TILE = 512
MOE_F_TILE = 512
FFN_SUB_ROWS = 256
DMA_LOOP_UNROLL = 8


def _params(semantics, big=False):
    return pltpu.CompilerParams(
        dimension_semantics=semantics,
        vmem_limit_bytes=V7X_VMEM_LIMIT_BYTES if big else None)


def _rmsnorm_kernel(x_ref, g_ref, o_ref):
    x = x_ref[...]
    y = x * lax.rsqrt(jnp.mean(x * x, axis=-1, keepdims=True) + RMS_EPS)
    o_ref[...] = (y * g_ref[...]).astype(o_ref.dtype)


def rmsnorm(x2, g, out_dtype, tm=1024):
    n, d = x2.shape
    return pl.pallas_call(
        _rmsnorm_kernel,
        out_shape=jax.ShapeDtypeStruct((n, d), out_dtype),
        grid=(n // tm,),
        in_specs=[pl.BlockSpec((tm, d), lambda i: (i, 0)),
                  pl.BlockSpec((1, d), lambda i: (0, 0))],
        out_specs=pl.BlockSpec((tm, d), lambda i: (i, 0)),
        compiler_params=_params(("parallel",), big=True),
        name="rmsnorm",
    )(x2, g.reshape(1, d))


def _pack_bf16_pairs(y):
    w = y.shape[1] // 2
    bits = pltpu.bitcast(y.astype(BF16).astype(F32), jnp.uint32)
    return (bits[:, :w] >> 16) | (bits[:, w:] & jnp.uint32(0xFFFF0000))


def _unpack_pairs_f32(p):
    lo = pltpu.bitcast(p << 16, F32)
    hi = pltpu.bitcast(p & jnp.uint32(0xFFFF0000), F32)
    return jnp.concatenate([lo, hi], axis=1)


def _unpack_bf16_pairs(p):
    return _unpack_pairs_f32(p).astype(BF16)


def _rmsnorm_router_kernel(x_ref, g_ref, wr_ref, o_ref, l_ref):
    x = x_ref[...]
    y = x * lax.rsqrt(jnp.mean(x * x, axis=-1, keepdims=True) + RMS_EPS)
    y = y * g_ref[...]
    o_ref[...] = _pack_bf16_pairs(y)
    l_ref[...] = jnp.dot(y, wr_ref[...], preferred_element_type=F32,
                         precision=lax.Precision.HIGHEST)


def rmsnorm_router(x2, g, w_router_pad, tm=512):
    n, d = x2.shape
    ne = w_router_pad.shape[1]
    return pl.pallas_call(
        _rmsnorm_router_kernel,
        out_shape=(jax.ShapeDtypeStruct((n, d // 2), jnp.uint32), jax.ShapeDtypeStruct((n, ne), F32)),
        grid=(n // tm,),
        in_specs=[pl.BlockSpec((tm, d), lambda i: (i, 0)),
                  pl.BlockSpec((1, d), lambda i: (0, 0)),
                  pl.BlockSpec((d, ne), lambda i: (0, 0))],
        out_specs=(pl.BlockSpec((tm, d // 2), lambda i: (i, 0)),
                   pl.BlockSpec((tm, ne), lambda i: (i, 0))),
        compiler_params=_params(("parallel",)),
        name="rmsnorm_router",
    )(x2, g.reshape(1, d), w_router_pad)


def _mm_kernel(*refs, n_a, has_res):
    a_refs = refs[:n_a]
    w_refs = refs[n_a:2 * n_a]
    idx = 2 * n_a
    res_ref = refs[idx] if has_res else None
    idx += int(has_res)
    o_ref = refs[idx]
    wbf_refs = refs[idx + 1:idx + 1 + n_a]

    @pl.when(pl.program_id(1) == 0)
    def _():
        for w_ref, wbf_ref in zip(w_refs, wbf_refs):
            wbf_ref[...] = w_ref[...].astype(BF16)

    acc = None
    for a_ref, wbf_ref in zip(a_refs, wbf_refs):
        d = jnp.dot(a_ref[...].astype(BF16), wbf_ref[...], preferred_element_type=F32)
        acc = d if acc is None else acc + d
    if has_res:
        acc = acc + res_ref[...]
    o_ref[...] = acc.astype(o_ref.dtype)


def matmul(a_list, w, *, res=None, out_dtype=F32, tm=1024, tn=1024, name="matmul"):
    m = a_list[0].shape[0]
    n = w.shape[1]
    n_a = len(a_list)
    k_each = a_list[0].shape[1]
    assert all(a.shape == (m, k_each) for a in a_list) and w.shape[0] == n_a * k_each
    in_specs = [pl.BlockSpec((tm, k_each), lambda j, i: (i, 0)) for _ in a_list]
    in_specs += [pl.BlockSpec((k_each, tn), functools.partial(lambda j, i, kb: (kb, j), kb=kb))
                 for kb in range(n_a)]
    args = list(a_list) + [w] * n_a
    if res is not None:
        in_specs.append(pl.BlockSpec((tm, tn), lambda j, i: (i, j)))
        args.append(res)
    return pl.pallas_call(
        functools.partial(_mm_kernel, n_a=n_a, has_res=res is not None),
        out_shape=jax.ShapeDtypeStruct((m, n), out_dtype),
        grid=(n // tn, m // tm),
        in_specs=in_specs,
        out_specs=pl.BlockSpec((tm, tn), lambda j, i: (i, j)),
        scratch_shapes=[pltpu.VMEM((k_each, tn), BF16) for _ in a_list],
        compiler_params=_params(("parallel", "arbitrary"), big=True),
        name=name,
    )(*args)


def _rms_scale_bf16(x, g):
    y = x * lax.rsqrt(jnp.mean(x * x, axis=-1, keepdims=True) + RMS_EPS)
    return (y * g).astype(BF16)


def _ple_kernel(h_ref, p_ref, g_ref, wg_ref, wp_ref, o_ref, wg_bf, wp_bf):
    @pl.when(pl.program_id(1) == 0)
    def _():
        wg_bf[...] = wg_ref[...].astype(BF16)
        wp_bf[...] = wp_ref[...].astype(BF16)

    tn = o_ref.shape[1]
    a = jnp.dot(_rms_scale_bf16(h_ref[...], g_ref[...]), wg_bf[...], preferred_element_type=F32)
    pp = jnp.dot(p_ref[...].astype(BF16), wp_bf[...], preferred_element_type=F32)
    col = pl.multiple_of(pl.program_id(0) * tn, tn)
    o_ref[...] = h_ref[:, pl.ds(col, tn)] + jax.nn.sigmoid(a) * pp


def ple(h2, p_all, g, w_gate_all, w_proj_all, layer, tm=512, tn=1024):
    m, d = h2.shape
    pd = p_all.shape[2]
    return pl.pallas_call(
        _ple_kernel,
        out_shape=jax.ShapeDtypeStruct((m, d), F32),
        grid=(d // tn, m // tm),
        in_specs=[pl.BlockSpec((tm, d), lambda j, i: (i, 0)),
                  pl.BlockSpec((None, tm, pd), lambda j, i: (layer, i, 0)),
                  pl.BlockSpec((1, d), lambda j, i: (0, 0)),
                  pl.BlockSpec((None, d, tn), lambda j, i: (layer, 0, j)),
                  pl.BlockSpec((None, pd, tn), lambda j, i: (layer, 0, j))],
        out_specs=pl.BlockSpec((tm, tn), lambda j, i: (i, j)),
        scratch_shapes=[pltpu.VMEM((d, tn), BF16), pltpu.VMEM((pd, tn), BF16)],
        compiler_params=_params(("parallel", "arbitrary"), big=True),
        name="ple",
    )(h2, p_all, g.reshape(1, d), w_gate_all, w_proj_all)


def _row_copy(src_hbm, row, dst, i, sem):
    return pltpu.make_async_copy(src_hbm.at[pl.ds(row, 1), :], dst.at[pl.ds(i, 1), :], sem)


def _swiglu_kernel(exp_ref, rows_ref, src_ref, tok_ref, x_ref, *rest, n_col, gather, sub_rows, nf):
    del exp_ref
    gain_ref = None if gather else rest[0]
    wg_ref, wu_ref, wd_ref, o_ref = rest[0 if gather else 1:][:4]
    scratch = rest[(4 if gather else 5):]
    t = pl.program_id(0)
    f = pl.program_id(1)
    nt = pl.num_programs(0)
    n_rows = rows_ref[t]
    tm = o_ref.shape[0]

    if not gather:
        (xbf,) = scratch

        @pl.when(f == 0)
        def _():
            h = x_ref[...]
            o_ref[...] = h
            xbf[...] = _rms_scale_bf16(h, gain_ref[...])

    if gather:
        @pl.when(f == 0)
        def _():
            o_ref[...] = jnp.zeros_like(o_ref)

        xbuf, xbf, sems = scratch
        slot = t % 2
        n_fetch = xbuf.shape[1]
        per_step = n_fetch // nf

        def start_row(tile, sl, i):
            _row_copy(x_ref, tok_ref[src_ref[tile] + i], xbuf.at[sl], i, sems.at[sl, i]).start()

        def wait_all(sl):
            def body(c, carry):
                for r in range(DMA_LOOP_UNROLL):
                    i = c * DMA_LOOP_UNROLL + r
                    _row_copy(x_ref, 0, xbuf.at[sl], i, sems.at[sl, i]).wait()
                return carry
            lax.fori_loop(0, n_fetch // DMA_LOOP_UNROLL, body, 0)

        @pl.when(jnp.logical_and(f == 0, t == 0))
        def _():
            def body(i, carry):
                start_row(0, 0, i)
                return carry
            lax.fori_loop(0, n_fetch, body, 0, unroll=DMA_LOOP_UNROLL)

        @pl.when(f == 0)
        def _():
            wait_all(slot)
            xbf[...] = _unpack_bf16_pairs(xbuf[slot, 0:tm])

        def fetch_ahead():
            nxt = jnp.minimum(t + 1, nt - 1)
            for r in range(per_step):
                start_row(nxt, 1 - slot, f * per_step + r)
    else:
        def fetch_ahead():
            pass

    def ffn_rows(n):
        fetch_ahead()
        if n == 0:
            return
        x = xbf[0:n, :]
        g = jnp.dot(x, wg_ref[...].astype(BF16), preferred_element_type=F32)
        u = jnp.dot(x, wu_ref[...].astype(BF16), preferred_element_type=F32)
        hmid = (g * jax.nn.sigmoid(g) * u).astype(BF16)
        wd = wd_ref[...].astype(BF16)
        cw = o_ref.shape[1] // n_col
        for c in range(n_col):
            y = jnp.dot(hmid, wd[:, c * cw:(c + 1) * cw], preferred_element_type=F32)
            o_ref[0:n, c * cw:(c + 1) * cw] += y

    for k in range(0 if gather else 1, tm // sub_rows + 1):
        covers = jnp.logical_and(n_rows > (k - 1) * sub_rows, n_rows <= k * sub_rows)
        pl.when(covers)(functools.partial(ffn_rows, k * sub_rows))

    if gather:
        @pl.when(jnp.logical_and(t == nt - 1, f == nf - 1))
        def _():
            wait_all(1 - slot)


def gather_rows_per_tile(tm, nf):
    per_step = -(-tm // nf)
    while (per_step * nf) % DMA_LOOP_UNROLL:
        per_step += 1
    return per_step * nf


def swiglu(x, w_gate, w_up, w_down, tile_exp, tile_rows, tile_src=None, src_rows=None, *, gain=None,
           tm=MOE_TILE, tf=FFN_F_TILE, sub_rows=MOE_TILE, name="swiglu"):
    gather = src_rows is not None
    assert gather != (gain is not None)
    m = tile_rows.shape[0] * tm
    d = w_gate.shape[1]
    dff = w_gate.shape[2]
    nf = dff // tf
    nt = m // tm
    assert nf >= 2

    def w_in_map(t, f, exp_ref, rows_ref, src_ref, tok_ref):
        return (exp_ref[t], 0, jnp.where(rows_ref[t] > 0, f, nf - 1))

    def w_out_map(t, f, exp_ref, rows_ref, src_ref, tok_ref):
        return (exp_ref[t], jnp.where(rows_ref[t] > 0, f, nf - 1), 0)

    def tile_map(t, f, exp_ref, rows_ref, src_ref, tok_ref):
        return (t, 0)

    w_specs = [pl.BlockSpec((None, d, tf), w_in_map),
               pl.BlockSpec((None, d, tf), w_in_map),
               pl.BlockSpec((None, tf, d), w_out_map)]
    if gather:
        in_specs = [pl.BlockSpec(memory_space=pl.ANY)] + w_specs
        args = [x, w_gate, w_up, w_down]
        n_fetch = gather_rows_per_tile(tm, nf)
        scratch = [pltpu.VMEM((2, n_fetch, d // 2), jnp.uint32), pltpu.VMEM((tm, d), BF16),
                   pltpu.SemaphoreType.DMA((2, n_fetch))]
    else:
        in_specs = [pl.BlockSpec((tm, d), tile_map, pipeline_mode=pl.Buffered(1)),
                    pl.BlockSpec((1, d), lambda t, f, e, r, s, k: (0, 0))] + w_specs
        args = [x, gain.reshape(1, d), w_gate, w_up, w_down]
        scratch = [pltpu.VMEM((tm, d), BF16)]
        tile_src = jnp.zeros((nt,), jnp.int32)
        src_rows = jnp.zeros((1,), jnp.int32)
    grid_spec = pltpu.PrefetchScalarGridSpec(
        num_scalar_prefetch=4,
        grid=(nt, nf),
        in_specs=in_specs,
        out_specs=pl.BlockSpec((tm, d), tile_map, pipeline_mode=pl.Buffered(1)),
        scratch_shapes=scratch,
    )
    return pl.pallas_call(
        functools.partial(_swiglu_kernel, n_col=4, gather=gather, sub_rows=sub_rows, nf=nf),
        out_shape=jax.ShapeDtypeStruct((m, d), F32),
        grid_spec=grid_spec,
        compiler_params=_params(("arbitrary", "arbitrary"), big=True),
        name=name,
    )(tile_exp, tile_rows, tile_src, src_rows, *args)


COMBINE_TILE = 256


def _combine_kernel(d0_ref, d1_ref, ys_hbm, gate_ref, h_ref, hout_ref, buf, sems):
    i = pl.program_id(0)
    n = pl.num_programs(0)
    tt = h_ref.shape[0]
    slot = i % 2
    dests = (d0_ref, d1_ref)

    def start_tile(tile, sl):
        def body(r, c):
            for k, d_ref in enumerate(dests):
                _row_copy(ys_hbm, d_ref[tile * tt + r], buf.at[sl, k], r, sems.at[sl, k, r]).start()
            return c
        lax.fori_loop(0, tt, body, 0, unroll=DMA_LOOP_UNROLL)

    @pl.when(i == 0)
    def _():
        start_tile(0, 0)

    @pl.when(i + 1 < n)
    def _():
        start_tile(i + 1, 1 - slot)

    def wait_body(r, c):
        for k in range(TOP_K):
            _row_copy(ys_hbm, 0, buf.at[slot, k], r, sems.at[slot, k, r]).wait()
        return c
    lax.fori_loop(0, tt, wait_body, 0, unroll=DMA_LOOP_UNROLL)

    gates = gate_ref[...]
    hout_ref[...] = h_ref[...] + (buf[slot, 0] * gates[:, 0:1] + buf[slot, 1] * gates[:, 1:2])


def moe_combine(h2, ys, dest2, gates, tt=COMBINE_TILE):
    n, d = h2.shape
    grid_spec = pltpu.PrefetchScalarGridSpec(
        num_scalar_prefetch=2,
        grid=(n // tt,),
        in_specs=[pl.BlockSpec(memory_space=pl.ANY),
                  pl.BlockSpec((tt, TOP_K), lambda i, a, b: (i, 0)),
                  pl.BlockSpec((tt, d), lambda i, a, b: (i, 0))],
        out_specs=pl.BlockSpec((tt, d), lambda i, a, b: (i, 0)),
        scratch_shapes=[pltpu.VMEM((2, TOP_K, tt, d), F32), pltpu.SemaphoreType.DMA((2, TOP_K, tt))],
    )
    return pl.pallas_call(
        _combine_kernel,
        out_shape=jax.ShapeDtypeStruct((n, d), F32),
        grid_spec=grid_spec,
        compiler_params=_params(("arbitrary",), big=True),
        name="moe_combine",
    )(dest2[:, 0], dest2[:, 1], ys, gates, h2)


def _t5_bucket(rel):
    half = T5_BUCKETS // 2
    max_exact = half // 2
    n = jnp.abs(rel)
    log_ratio = jnp.log(jnp.maximum(n, 1).astype(F32) / max_exact) / math.log(T5_MAX_DIST / max_exact)
    large = jnp.minimum(max_exact + (log_ratio * (half - max_exact)).astype(jnp.int32), half - 1)
    return jnp.where(rel > 0, half, 0) + jnp.where(n < max_exact, n, large)


def _window_bias_table(t5_bias):
    i = jnp.arange(A_BLOCK)[:, None]
    j = jnp.arange(3 * A_BLOCK)[None, :]
    rel = j - A_BLOCK - i
    onehot = (_t5_bucket(rel)[None] == jnp.arange(T5_BUCKETS)[:, None, None]).astype(F32)
    bias = jnp.einsum('bh,bij->hij', t5_bias.astype(F32), onehot, precision=lax.Precision.HIGHEST)
    return jnp.where((jnp.abs(rel) <= A_BLOCK)[None], bias, NEG_INF)


def _window_kernel(sink_ref, q_ref, kv_ref, bias_ref, o_ref, *, nb):
    n = pl.program_id(1)
    scale = HEAD_DIM ** -0.5
    starts = (jnp.maximum(n - 1, 0), n, jnp.minimum(n + 1, nb - 1))
    kv = [kv_ref[pl.ds(pl.multiple_of(s * A_BLOCK, A_BLOCK), A_BLOCK), :] for s in starts]
    col = lax.broadcasted_iota(jnp.int32, (A_BLOCK, 3 * A_BLOCK), 1)
    edge_ok = jnp.logical_and(jnp.logical_or(n > 0, col >= A_BLOCK),
                              jnp.logical_or(n < nb - 1, col < 2 * A_BLOCK))
    kv = jnp.concatenate(kv, axis=0)
    q = q_ref[...]
    lo = lax.broadcasted_iota(jnp.int32, (1, 2 * HEAD_DIM), 1) < HEAD_DIM
    sel = (jnp.where(lo, scale, 0.0).astype(BF16), jnp.where(lo, 0.0, scale).astype(BF16))
    pairs_per_kv = A_GROUP // 2
    kk, vv = [], []
    for kh in range(A_KV_HEADS):
        k1 = kv[:, kh * HEAD_DIM:(kh + 1) * HEAD_DIM]
        v1 = kv[:, A_KV_WIDTH + kh * HEAD_DIM:A_KV_WIDTH + (kh + 1) * HEAD_DIM]
        kk.append(jnp.concatenate([k1, k1], axis=1))
        vv.append(jnp.concatenate([v1, v1], axis=1))
    scores = []
    for pr in range(A_Q_HEADS // 2):
        q2 = q[:, pr * 2 * HEAD_DIM:(pr + 1) * 2 * HEAD_DIM]
        qq = jnp.concatenate([q2 * sel[0], q2 * sel[1]], axis=0)
        s2 = lax.dot_general(qq, kk[pr // pairs_per_kv], (((1,), (1,)), ((), ())),
                             preferred_element_type=F32)
        for i in range(2):
            scores.append(jnp.where(edge_ok, s2[i * A_BLOCK:(i + 1) * A_BLOCK] + bias_ref[2 * pr + i],
                                    NEG_INF))
    probs, dens = [], []
    for h, s in enumerate(scores):
        sink = sink_ref[h]
        mx = jnp.maximum(jnp.max(s, axis=-1, keepdims=True), sink)
        p = jnp.exp(s - mx)
        dens.append(jnp.sum(p, axis=-1, keepdims=True) + jnp.exp(sink - mx))
        probs.append(p.astype(BF16))
    for pr in range(A_Q_HEADS // 2):
        v2 = vv[pr // pairs_per_kv]
        oa = jnp.dot(probs[2 * pr], v2, preferred_element_type=F32) / dens[2 * pr]
        ob = jnp.dot(probs[2 * pr + 1], v2, preferred_element_type=F32) / dens[2 * pr + 1]
        o_ref[:, pr * 2 * HEAD_DIM:(pr + 1) * 2 * HEAD_DIM] = jnp.where(lo, oa, ob).astype(o_ref.dtype)


def window_attention(proj_a, t5_bias, sink):
    b, s, _ = proj_a.shape
    nb = s // A_BLOCK
    kvw = 2 * A_KV_WIDTH
    grid_spec = pltpu.PrefetchScalarGridSpec(
        num_scalar_prefetch=0,
        grid=(b, nb),
        in_specs=[pl.BlockSpec(memory_space=pltpu.SMEM),
                  pl.BlockSpec((None, A_BLOCK, A_WIDTH), lambda bi, n: (bi, n, 0)),
                  pl.BlockSpec((None, s, kvw), lambda bi, n: (bi, 0, A_WIDTH // kvw)),
                  pl.BlockSpec((A_Q_HEADS, A_BLOCK, 3 * A_BLOCK), lambda bi, n: (0, 0, 0))],
        out_specs=pl.BlockSpec((None, A_BLOCK, A_WIDTH), lambda bi, n: (bi, n, 0)),
    )
    return pl.pallas_call(
        functools.partial(_window_kernel, nb=nb),
        out_shape=jax.ShapeDtypeStruct((b, s, A_WIDTH), BF16),
        grid_spec=grid_spec,
        compiler_params=_params(("parallel", "arbitrary")),
        name="window_attention",
    )(sink.astype(F32), proj_a, proj_a, _window_bias_table(t5_bias))


NA_HEAD_GROUP = 16


def _na_bias_table(rpb, rows):
    kh = min(NA_WIN_H, rows)
    kw = NA_WIN_W
    c = jnp.arange(GRID_W)
    cs = jnp.clip(c - kw // 2, 0, GRID_W - kw)
    col_ok = (c[None] >= cs[:, None]) & (c[None] < cs[:, None] + kw)
    col_off = jnp.clip(c[None] - c[:, None], -(kw - 1), kw - 1) + kw - 1
    onehot = (col_off[None] == jnp.arange(2 * kw - 1)[:, None, None]).astype(F32)
    by_col = jnp.einsum('hrc,cqk->hqrk', rpb.astype(F32), onehot, precision=lax.Precision.HIGHEST)
    by_col = jnp.where(col_ok[None, :, None, :], by_col, NEG_INF)
    tabs = [by_col[:, :, NA_WIN_H - 1 - d:NA_WIN_H - 1 - d + kh] for d in range(kh)]
    return jnp.stack(tabs, axis=0).reshape(kh, rpb.shape[0], GRID_W, kh * GRID_W)


def _na_kernel(q_ref, k_ref, v_ref, bias_ref, o_ref, *, rows):
    r = pl.program_id(2)
    kh = min(NA_WIN_H, rows)
    scale = HEAD_DIM ** -0.5
    start = pl.multiple_of(jnp.clip(r - kh // 2, 0, rows - kh) * GRID_W, GRID_W)
    k = k_ref[pl.ds(start, kh * GRID_W), :]
    v = v_ref[pl.ds(start, kh * GRID_W), :]
    q = q_ref[...]
    lo = lax.broadcasted_iota(jnp.int32, (1, 2 * HEAD_DIM), 1) < HEAD_DIM
    sel = (jnp.where(lo, scale, 0.0).astype(BF16), jnp.where(lo, 0.0, scale).astype(BF16))
    n_pairs = NA_HEAD_GROUP // 2
    scores = []
    for pr in range(n_pairs):
        sl = slice(pr * 2 * HEAD_DIM, (pr + 1) * 2 * HEAD_DIM)
        q2 = q[:, sl]
        qq = jnp.concatenate([q2 * sel[0], q2 * sel[1]], axis=0)
        s2 = lax.dot_general(qq, k[:, sl], (((1,), (1,)), ((), ())), preferred_element_type=F32)
        for i in range(2):
            scores.append(s2[i * GRID_W:(i + 1) * GRID_W] + bias_ref[2 * pr + i])
    probs, dens = [], []
    for s in scores:
        mx = jnp.max(s, axis=-1, keepdims=True)
        p = jnp.exp(s - mx)
        dens.append(jnp.sum(p, axis=-1, keepdims=True))
        probs.append(p.astype(BF16))
    for pr in range(n_pairs):
        sl = slice(pr * 2 * HEAD_DIM, (pr + 1) * 2 * HEAD_DIM)
        oa = jnp.dot(probs[2 * pr], v[:, sl], preferred_element_type=F32) / dens[2 * pr]
        ob = jnp.dot(probs[2 * pr + 1], v[:, sl], preferred_element_type=F32) / dens[2 * pr + 1]
        o_ref[:, sl] = jnp.where(lo, oa, ob).astype(o_ref.dtype)


def neighbourhood_attention(proj_n, rpb):
    b, s, _ = proj_n.shape
    rows = s // GRID_W
    kh = min(NA_WIN_H, rows)
    gw = NA_HEAD_GROUP * HEAD_DIM
    ng = B_WIDTH // gw

    def bias_map(bi, g, r):
        return (r - jnp.clip(r - kh // 2, 0, rows - kh), g, 0, 0)

    return pl.pallas_call(
        functools.partial(_na_kernel, rows=rows),
        out_shape=jax.ShapeDtypeStruct((b, s, B_WIDTH), BF16),
        grid=(b, ng, rows),
        in_specs=[pl.BlockSpec((None, GRID_W, gw), lambda bi, g, r: (bi, r, g)),
                  pl.BlockSpec((None, s, gw), lambda bi, g, r: (bi, 0, ng + g)),
                  pl.BlockSpec((None, s, gw), lambda bi, g, r: (bi, 0, 2 * ng + g)),
                  pl.BlockSpec((None, NA_HEAD_GROUP, GRID_W, kh * GRID_W), bias_map)],
        out_specs=pl.BlockSpec((None, GRID_W, gw), lambda bi, g, r: (bi, r, g)),
        compiler_params=_params(("parallel", "parallel", "arbitrary"), big=True),
        name="neighbourhood_attention",
    )(proj_n, proj_n, proj_n, _na_bias_table(rpb, rows))


CONV_HALO = 16
CONV_COL_CHUNK = 512


def _hyena_in_kernel(prev_ref, cur_ref, next_ref, w_ref, b_ref, cw_ref, cb_ref, o_ref, wbf_ref, *,
                     tiles_per_seq):
    i = pl.program_id(1)

    @pl.when(i == 0)
    def _():
        wbf_ref[...] = w_ref[...].astype(BF16)

    tm = cur_ref.shape[0]
    rows = tm + 2 * CONV_HALO
    a = jnp.concatenate([prev_ref[...], cur_ref[...], next_ref[...]], axis=0)
    pos = i % tiles_per_seq
    row = lax.broadcasted_iota(jnp.int32, (rows, 1), 0)
    kill_up = jnp.logical_and(pos == 0, row == CONV_HALO)
    kill_dn = jnp.logical_and(pos == tiles_per_seq - 1, row == CONV_HALO + tm - 1)
    for c in range(o_ref.shape[1] // CONV_COL_CHUNK):
        cs = slice(c * CONV_COL_CHUNK, (c + 1) * CONV_COL_CHUNK)
        p = jnp.dot(a, wbf_ref[:, cs], preferred_element_type=F32) + b_ref[:, cs]
        up = jnp.where(kill_up, 0.0, pltpu.roll(p, 1, 0))
        dn = jnp.where(kill_dn, 0.0, pltpu.roll(p, rows - 1, 0))
        y = up * cw_ref[0:1, cs] + p * cw_ref[1:2, cs] + dn * cw_ref[2:3, cs] + cb_ref[:, cs]
        o_ref[:, cs] = y[CONV_HALO:CONV_HALO + tm]


def hyena_in_conv(hn, w_in, b_in, conv_w, conv_b, seq, tm=1024, tn=1024):
    m, k = hn.shape
    n = w_in.shape[1]
    hb = tm // CONV_HALO
    last_halo = m // CONV_HALO - 1
    return pl.pallas_call(
        functools.partial(_hyena_in_kernel, tiles_per_seq=seq // tm),
        out_shape=jax.ShapeDtypeStruct((m, n), F32),
        grid=(n // tn, m // tm),
        in_specs=[pl.BlockSpec((CONV_HALO, k), lambda j, i: (jnp.maximum(i * hb - 1, 0), 0)),
                  pl.BlockSpec((tm, k), lambda j, i: (i, 0)),
                  pl.BlockSpec((CONV_HALO, k), lambda j, i: (jnp.minimum((i + 1) * hb, last_halo), 0)),
                  pl.BlockSpec((k, tn), lambda j, i: (0, j)),
                  pl.BlockSpec((1, tn), lambda j, i: (0, j)),
                  pl.BlockSpec((3, tn), lambda j, i: (0, j)),
                  pl.BlockSpec((1, tn), lambda j, i: (0, j))],
        out_specs=pl.BlockSpec((tm, tn), lambda j, i: (i, j)),
        scratch_shapes=[pltpu.VMEM((k, tn), BF16)],
        compiler_params=_params(("parallel", "arbitrary"), big=True),
        name="hyena_in_conv",
    )(hn, hn, hn, w_in, b_in.reshape(1, n), conv_w, conv_b.reshape(1, n))


def _filter_stage1_kernel(z_ref, t_ref, fw1_ref, fb1_ref, fw2_ref, fb2_ref, fr_ref, w3f0_ref, w3f1_ref,
                          w3b0_ref, w3b1_ref, delta_ref, f1_ref, y_ref, *, seq):
    j = pl.program_id(0)
    hi = lax.Precision.HIGHEST
    fr = fr_ref[...]
    hid = jnp.sin(fr * (jnp.dot(z_ref[...], fw1_ref[...], preferred_element_type=F32, precision=hi)
                        + fb1_ref[...]))
    hid = jnp.sin(fr * (jnp.dot(hid, fw2_ref[...], preferred_element_type=F32, precision=hi)
                        + fb2_ref[...]))
    hid_bf = hid.astype(BF16)
    r = lax.broadcasted_iota(jnp.int32, (hid.shape[0], 1), 0)
    u = j * FFT_CHUNK + r // FFT_N2 + FFT_N1 * (r % FFT_N2)
    decay = jnp.where(u == seq, 0.0, jnp.exp(-t_ref[...] * delta_ref[...]))
    for o, (wf_ref, wb_ref) in enumerate(((w3f0_ref, w3b0_ref), (w3f1_ref, w3b1_ref))):
        fwd = jnp.dot(hid_bf, wf_ref[...].astype(BF16), preferred_element_type=F32)
        bwd = jnp.dot(hid_bf, wb_ref[...].astype(BF16), preferred_element_type=F32)
        taps = (jnp.where(u > seq, bwd, jnp.where(u == 0, fwd + bwd, fwd)) * decay).astype(BF16)
        for jj in range(FFT_CHUNK):
            x = taps[jj * FFT_N2:(jj + 1) * FFT_N2]
            y_ref[o, jj] = _pack_bf16_pairs(jnp.dot(f1_ref[jj], x, preferred_element_type=F32))


def hyena_filter_stage1(seq, fw1, fb1, fw2, fb2, freq, fw3, width, consts, cb=FFT_CB):
    _, f1_real, _, _, _ = consts
    n = 2 * seq
    n1, n2, ch = FFT_N1, FFT_N2, FFT_CHUNK
    assert fw3.shape[1] == 4 * width
    r = jnp.arange(n)
    u = (r // (ch * n2)) * ch + (r // n2) % ch + n1 * (r % n2)
    pos = jnp.where(u <= seq, u, n - u)
    pos = jnp.where(u == seq, 0, pos)
    bands = (HYENA_EMB - 1) // 2
    f = jnp.linspace(1e-4, bands - 1, bands, dtype=F32)[None]
    t = (pos.astype(F32) / (seq - 1))[:, None]
    w = ((2.0 * math.pi / seq) * pos.astype(F32))[:, None]
    z = jnp.concatenate([t, jnp.cos(f * w), -jnp.sin(f * w)], axis=-1)
    emb_pad = 40
    z = jnp.pad(z, ((0, 0), (0, emb_pad - HYENA_EMB)))
    fw1p = jnp.pad(fw1.astype(F32), ((0, emb_pad - HYENA_EMB), (0, 0)))
    fh = fw1.shape[1]
    max_decay = math.log(HYENA_DECAY_TARGET) / HYENA_FAST_PCT
    min_decay = math.log(HYENA_DECAY_TARGET) / HYENA_SLOW_PCT
    deltas = jnp.abs(jnp.linspace(min_decay, max_decay, width, dtype=F32))[None]
    ncb = width // cb
    rows = ch * n2

    def small(shape):
        return pl.BlockSpec(shape, lambda j, k: (0, 0))

    def w3_spec(blk):
        return pl.BlockSpec((fh, cb), lambda j, k: (0, blk * ncb + k))

    return pl.pallas_call(
        functools.partial(_filter_stage1_kernel, seq=seq),
        out_shape=jax.ShapeDtypeStruct((2, n1, 2 * n2, width // 2), jnp.uint32),
        grid=(n1 // ch, ncb),
        in_specs=[pl.BlockSpec((rows, emb_pad), lambda j, k: (j, 0)),
                  pl.BlockSpec((rows, 1), lambda j, k: (j, 0)),
                  small((emb_pad, fh)), small((1, fh)), small((fh, fh)), small((1, fh)), small((1, fh)),
                  w3_spec(0), w3_spec(1), w3_spec(2), w3_spec(3),
                  pl.BlockSpec((1, cb), lambda j, k: (0, k)),
                  pl.BlockSpec((ch, 2 * n2, n2), lambda j, k: (j, 0, 0))],
        out_specs=pl.BlockSpec((2, ch, 2 * n2, cb // 2), lambda j, k: (0, j, 0, k)),
        compiler_params=_params(("parallel", "parallel"), big=True),
        name="hyena_filter_stage1",
    )(z, t, fw1p, fb1.reshape(1, fh).astype(F32), fw2.astype(F32), fb2.reshape(1, fh).astype(F32),
      freq.reshape(1, fh).astype(F32), fw3, fw3, fw3, fw3, deltas, f1_real)


def _dft_constants():
    n1, n2 = FFT_N1, FFT_N2
    n = n1 * n2
    a1 = np.arange(n1)
    a2 = np.arange(n2)
    half = n2 // 2

    def cplx_block(w):
        return np.block([[w.real, -w.imag], [w.imag, w.real]])

    ang = (a2[None, :, None] * a2[None, None, :] / n2) + (a1[:, None, None] * a2[None, :, None] / n)
    w1 = np.exp(-2j * np.pi * ang)
    f1_pad = np.stack([cplx_block(w1[i][:, :half]) for i in range(n1)])
    f1_real = np.stack([np.concatenate([w1[i].real, w1[i].imag], axis=0) for i in range(n1)])
    w2 = np.exp(-2j * np.pi * (a1[:, None] * a1[None, :]) / n1)
    f2 = cplx_block(w2)
    ang = (a1[None, :, None] * a1[None, None, :] / n1) + (a2[:, None, None] * a1[None, :, None] / n)
    g2 = np.stack([cplx_block(m_) for m_ in np.exp(2j * np.pi * ang)])
    wg1 = np.exp(2j * np.pi * (a2[:half, None] * a2[None, :]) / n2) / n
    g1 = cplx_block(wg1)
    to = lambda x: jnp.asarray(x.astype(np.float32)).astype(BF16)
    return to(f1_pad), to(f1_real), to(f2), to(g2), to(g1)


def _fft_stage1_block(x_re, x_im, f_ref, y_ref):
    xr = pltpu.einshape("mjc->jmc", x_re)
    xi = pltpu.einshape("mjc->jmc", x_im)
    for j in range(FFT_CHUNK):
        xc = jnp.concatenate([xr[j], xi[j]], axis=0).astype(BF16)
        y_ref[j] = _pack_bf16_pairs(jnp.dot(f_ref[j], xc, preferred_element_type=F32))


def _fft_stage1_kernel(x_ref, f_ref, y_ref):
    _fft_stage1_block(x_ref[0], x_ref[1], f_ref, y_ref)


def _fft_stage2_kernel(yr_ref, yi_ref, f2_ref, h_ref):
    yr = pltpu.einshape("nkc->knc", yr_ref[...])
    yi = pltpu.einshape("nkc->knc", yi_ref[...])
    for j in range(FFT_CHUNK):
        yc = _unpack_bf16_pairs(jnp.concatenate([yr[j], yi[j]], axis=0))
        h_ref[j] = _pack_bf16_pairs(jnp.dot(f2_ref[...], yc, preferred_element_type=F32))


def _fft_mid_kernel(yr_ref, yi_ref, f2_ref, h_ref, g2_ref, q_ref):
    yr = pltpu.einshape("nkc->knc", yr_ref[...])
    yi = pltpu.einshape("nkc->knc", yi_ref[...])
    n1 = FFT_N1
    spec = [jnp.dot(f2_ref[...], _unpack_bf16_pairs(jnp.concatenate([yr[j], yi[j]], axis=0)),
                    preferred_element_type=F32) for j in range(FFT_CHUNK)]
    prod = []
    for j, z in enumerate(spec):
        hf = _unpack_pairs_f32(h_ref[j])
        zr, zi, hr, hi = z[:n1], z[n1:], hf[:n1], hf[n1:]
        prod.append(jnp.concatenate([zr * hr - zi * hi, zr * hi + zi * hr], axis=0).astype(BF16))
    for j, pc in enumerate(prod):
        q_ref[j] = _pack_bf16_pairs(jnp.dot(g2_ref[j], pc, preferred_element_type=F32))


def _fft_last_kernel(qr_ref, qi_ref, g1_ref, gate_ref, zf_ref, fb_ref, *rest, feeds_next):
    f1_ref, o_ref, y_ref = rest if feeds_next else (None, rest[0], None)
    qr = pltpu.einshape("kjc->jkc", qr_ref[...])
    qi = pltpu.einshape("kjc->jkc", qi_ref[...])
    half = FFT_N2 // 2
    conv = []
    for j in range(FFT_CHUNK):
        qc = _unpack_bf16_pairs(jnp.concatenate([qr[j], qi[j]], axis=0))
        conv.append(jnp.dot(g1_ref[...], qc, preferred_element_type=F32))
    conv = jnp.stack(conv, axis=0)
    fb = fb_ref[...]
    out = []
    for bi in range(2):
        cb = pltpu.einshape("jmc->mjc", conv[:, bi * half:(bi + 1) * half])
        out.append(gate_ref[bi] * (cb + zf_ref[bi] * fb))
        o_ref[bi] = out[bi]
    if feeds_next:
        _fft_stage1_block(out[0], out[1], f1_ref, y_ref)


def hyena_filter_spectrum(y, consts, cb=FFT_CB):
    _, _, f2, _, _ = consts
    no = y.shape[0]
    c = 2 * y.shape[-1]
    n1, n2, ch = FFT_N1, FFT_N2, FFT_CHUNK
    nk = n2 // ch
    return pl.pallas_call(
        _fft_stage2_kernel,
        out_shape=jax.ShapeDtypeStruct((no, n2, 2 * n1, c // 2), jnp.uint32),
        grid=(no, nk, c // cb),
        in_specs=[pl.BlockSpec((None, n1, ch, cb // 2), lambda o, j, k: (o, 0, j, k)),
                  pl.BlockSpec((None, n1, ch, cb // 2), lambda o, j, k: (o, 0, nk + j, k)),
                  pl.BlockSpec((2 * n1, 2 * n1), lambda o, j, k: (0, 0))],
        out_specs=pl.BlockSpec((None, ch, 2 * n1, cb // 2), lambda o, j, k: (o, j, 0, k)),
        compiler_params=_params(("parallel", "parallel", "parallel"), big=True),
        name="fft_filter_stage2",
    )(y, y, f2)


def hyena_long_conv_gate(zsrc, z_col, gate_src, gate_col, hf, order, fbias, consts, y=None,
                         feeds_next=False, cb=FFT_CB):
    f1_pad, _, f2, g2, g1 = consts
    n1, n2, ch = FFT_N1, FFT_N2, FFT_CHUNK
    c = 2 * hf.shape[-1]
    ncb = c // cb
    half = n2 // 2
    y_shape = jax.ShapeDtypeStruct((n1, 2 * n2, c // 2), jnp.uint32)
    f1_spec = pl.BlockSpec((ch, 2 * n2, n2), lambda j, k: (j, 0, 0))
    y_spec = pl.BlockSpec((ch, 2 * n2, cb // 2), lambda j, k: (j, 0, k))
    if y is None:
        y = pl.pallas_call(
            _fft_stage1_kernel,
            out_shape=y_shape,
            grid=(n1 // ch, ncb),
            in_specs=[pl.BlockSpec((2, half, ch, cb), lambda j, k: (0, 0, j, z_col * ncb + k)), f1_spec],
            out_specs=y_spec,
            compiler_params=_params(("parallel", "parallel"), big=True),
            name="fft_stage1",
        )(zsrc, f1_pad)
    nk = n2 // ch
    q = pl.pallas_call(
        _fft_mid_kernel,
        out_shape=jax.ShapeDtypeStruct((n2, 2 * n1, c // 2), jnp.uint32),
        grid=(nk, ncb),
        in_specs=[pl.BlockSpec((n1, ch, cb // 2), lambda j, k: (0, j, k)),
                  pl.BlockSpec((n1, ch, cb // 2), lambda j, k: (0, nk + j, k)),
                  pl.BlockSpec((2 * n1, 2 * n1), lambda j, k: (0, 0)),
                  pl.BlockSpec((None, ch, 2 * n1, cb // 2), lambda j, k: (order, j, 0, k)),
                  pl.BlockSpec((ch, 2 * n1, 2 * n1), lambda j, k: (j, 0, 0))],
        out_specs=pl.BlockSpec((ch, 2 * n1, cb // 2), lambda j, k: (j, 0, k)),
        compiler_params=_params(("parallel", "parallel"), big=True),
        name="fft_mid",
    )(y, y, f2, hf, g2)
    nj = n1 // ch
    in_specs = [pl.BlockSpec((n2, ch, cb // 2), lambda j, k: (0, j, k)),
                pl.BlockSpec((n2, ch, cb // 2), lambda j, k: (0, nj + j, k)),
                pl.BlockSpec((2 * half, 2 * n2), lambda j, k: (0, 0)),
                pl.BlockSpec((2, half, ch, cb), lambda j, k: (0, 0, j, gate_col * ncb + k)),
                pl.BlockSpec((2, half, ch, cb), lambda j, k: (0, 0, j, z_col * ncb + k)),
                pl.BlockSpec((None, 1, cb), lambda j, k: (order, 0, k))]
    args = [q, q, g1, gate_src, zsrc, fbias.reshape(fbias.shape[0], 1, c)]
    out_shape = jax.ShapeDtypeStruct((2, half, n1, c), F32)
    out_spec = pl.BlockSpec((2, half, ch, cb), lambda j, k: (0, 0, j, k))
    if feeds_next:
        in_specs.append(f1_spec)
        args.append(f1_pad)
        out_shape, out_spec = (out_shape, y_shape), (out_spec, y_spec)
    res = pl.pallas_call(
        functools.partial(_fft_last_kernel, feeds_next=feeds_next),
        out_shape=out_shape,
        grid=(nj, ncb),
        in_specs=in_specs,
        out_specs=out_spec,
        compiler_params=_params(("parallel", "parallel"), big=True),
        name="fft_last",
    )(*args)
    return res if feeds_next else (res, None)


def hyena_mixer(hn, h_res, w_in, b_in, conv_w, conv_b, fw1, fb1, fw2, fb2, freq, fw3, fbias, w_out,
                batch, seq):
    width = w_out.shape[0]
    assert batch == 2 and 2 * seq == FFT_N1 * FFT_N2
    sc = hyena_in_conv(hn, w_in, b_in, conv_w, conv_b, seq)
    consts = _dft_constants()
    hf = hyena_filter_spectrum(hyena_filter_stage1(seq, fw1, fb1, fw2, fb2, freq, fw3, width, consts),
                               consts)
    sc4 = sc.reshape(batch, FFT_N2 // 2, FFT_N1, 3 * width)
    zf1, y1 = hyena_long_conv_gate(sc4, 2, sc4, 0, hf, 0, fbias, consts, feeds_next=True)
    zf2, _ = hyena_long_conv_gate(zf1, 0, sc4, 1, hf, 1, fbias, consts, y=y1)
    return matmul([zf2.reshape(batch * seq, width)], w_out, res=h_res, tm=512, tn=1024,
                  name="hyena_out")


def moe_swiglu(hn2_packed, logits, wg, wu, wd):
    n = hn2_packed.shape[0]
    top_v, top_i = lax.top_k(logits, TOP_K)
    gates = jax.nn.softmax(top_v, axis=-1)
    e_flat = top_i.reshape(-1).astype(jnp.int32)
    nk = n * TOP_K
    onehot = (e_flat[:, None] == jnp.arange(N_EXPERTS, dtype=jnp.int32)[None]).astype(jnp.int32)
    csum = jnp.cumsum(onehot, axis=0)
    rank = jnp.take_along_axis(csum, e_flat[:, None], axis=1)[:, 0] - 1
    counts = csum[-1]
    padded = ((counts + MOE_TILE - 1) // MOE_TILE) * MOE_TILE
    pad_end = jnp.cumsum(padded)
    pad_start = pad_end - padded
    dest = pad_start[e_flat] + rank
    p_rows = nk + N_EXPERTS * MOE_TILE
    order = jnp.argsort(e_flat, stable=True).astype(jnp.int32)
    nf = wg.shape[2] // MOE_F_TILE
    sorted_tok = jnp.pad(order // TOP_K, (0, gather_rows_per_tile(MOE_TILE, nf)))
    start = jnp.cumsum(counts) - counts
    nt = p_rows // MOE_TILE
    tile_start = jnp.arange(nt, dtype=jnp.int32) * MOE_TILE
    tile_used = tile_start < pad_end[-1]
    tile_exp = jnp.minimum(jnp.searchsorted(pad_end, tile_start, side='right'), N_EXPERTS - 1).astype(jnp.int32)
    tile_rows = jnp.clip(counts[tile_exp] - (tile_start - pad_start[tile_exp]), 0, MOE_TILE)
    tile_rows = jnp.where(tile_used, tile_rows, 0).astype(jnp.int32)
    last_exp = tile_exp[jnp.maximum(jnp.sum(tile_used.astype(jnp.int32)) - 1, 0)]
    tile_src = jnp.where(tile_used, start[tile_exp] + (tile_start - pad_start[tile_exp]), 0)
    tile_exp = jnp.where(tile_used, tile_exp, last_exp)
    ys = swiglu(hn2_packed, wg, wu, wd, tile_exp, tile_rows, tile_src.astype(jnp.int32), sorted_tok,
                tf=MOE_F_TILE, sub_rows=FFN_SUB_ROWS, name="swiglu_experts")
    return ys, dest.reshape(n, TOP_K), gates


def kernel(x, p, ln_mix, ln_ffn, ln_ple, final_norm, t5_bias, w_attn_in, w_attn_out, attn_sink, na_rpb, w_ffn_gate, w_ffn_up, w_ffn_down, w_hy_in, b_hy_in, w_hy_conv, b_hy_conv, w_hy_f1, b_hy_f1, w_hy_f2, b_hy_f2, hy_freq, w_hy_f3, hy_bias, w_hy_out, w_router, w_exp_gate, w_exp_up, w_exp_down, w_ple_proj, w_ple_gate):
    batch, seq, d = x.shape
    n = batch * seq
    depth = ln_mix.shape[0]
    h = x.reshape(n, d)
    for i in range(depth):
        li = i // 2
        hn = rmsnorm(h, ln_mix[i], BF16)
        if i % 2 == 0:
            na_off = A_WIDTH + 2 * A_KV_WIDTH
            w_in = w_attn_in[li]
            proj_a = matmul([hn], w_in[:, :na_off], out_dtype=BF16, tn=na_off, name="attn_in_a")
            proj_n = matmul([hn], w_in[:, na_off:], out_dtype=BF16, name="attn_in_n")
            oa = window_attention(proj_a.reshape(batch, seq, -1), t5_bias, attn_sink[li])
            ob = neighbourhood_attention(proj_n.reshape(batch, seq, -1), na_rpb[li])
            h = matmul([oa.reshape(n, A_WIDTH), ob.reshape(n, B_WIDTH)], w_attn_out[li], res=h,
                       name="attn_out")
            nt = n // MOE_TILE
            h = swiglu(h, w_ffn_gate, w_ffn_up, w_ffn_down,
                       jnp.full((nt,), li, jnp.int32), jnp.full((nt,), MOE_TILE, jnp.int32),
                       gain=ln_ffn[i], name="swiglu_dense")
        else:
            h = hyena_mixer(hn, h, w_hy_in[li], b_hy_in[li], w_hy_conv[li], b_hy_conv[li],
                            w_hy_f1[li], b_hy_f1[li], w_hy_f2[li], b_hy_f2[li], hy_freq[li],
                            w_hy_f3[li], hy_bias[li], w_hy_out[li], batch, seq)
            wr_pad = jnp.pad(w_router[li].astype(F32), ((0, 0), (0, 128 - N_EXPERTS)))
            hn2_packed, logits = rmsnorm_router(h, ln_ffn[i], wr_pad)
            ys, dest2, gates = moe_swiglu(hn2_packed, logits[:, :N_EXPERTS], w_exp_gate[li],
                                          w_exp_up[li], w_exp_down[li])
            h = moe_combine(h, ys, dest2, gates)
        h = ple(h, p.reshape(depth, n, -1), ln_ple[i], w_ple_gate, w_ple_proj, i)
    return rmsnorm(h, final_norm, F32).reshape(batch, seq, d)
```

```python
import functools
import math

import jax
import jax.numpy as jnp
import numpy as np
from jax import lax
from jax.experimental import pallas as pl
from jax.experimental.pallas import tpu as pltpu

F32 = jnp.float32
BF16 = jnp.bfloat16
NEG_INF = -1e30
RMS_EPS = 1e-6

V7X_VMEM_LIMIT_BYTES = 56 * 1024 * 1024

HEAD_DIM = 64
A_Q_HEADS = 16
A_KV_HEADS = 2
A_GROUP = A_Q_HEADS // A_KV_HEADS
A_BLOCK = 128
T5_BUCKETS = 32
T5_MAX_DIST = 128
B_HEADS = 16
GRID_W = 64
NA_WIN_H = 8
NA_WIN_W = 16
A_WIDTH = A_Q_HEADS * HEAD_DIM
A_KV_WIDTH = A_KV_HEADS * HEAD_DIM
B_WIDTH = B_HEADS * HEAD_DIM
N_EXPERTS = 8
TOP_K = 2
HYENA_EMB = 33
HYENA_DECAY_TARGET = 1e-2
HYENA_FAST_PCT = 0.3
HYENA_SLOW_PCT = 1.5

FFT_N1 = 64
FFT_N2 = 128
FFT_CHUNK = 8
FFT_CB = 1024

MOE_TILE = 1024
FFN_F_TILE = 512
MOE_F_TILE = 512
FFN_SUB_ROWS = 128
DMA_LOOP_UNROLL = 8


def _params(semantics, big=False):
    return pltpu.CompilerParams(
        dimension_semantics=semantics,
        vmem_limit_bytes=V7X_VMEM_LIMIT_BYTES if big else None)


def _rmsnorm_kernel(x_ref, g_ref, o_ref):
    x = x_ref[...]
    y = x * lax.rsqrt(jnp.mean(x * x, axis=-1, keepdims=True) + RMS_EPS)
    o_ref[...] = (y * g_ref[...]).astype(o_ref.dtype)


def rmsnorm(x2, g, out_dtype, tm=1024):
    n, d = x2.shape
    return pl.pallas_call(
        _rmsnorm_kernel,
        out_shape=jax.ShapeDtypeStruct((n, d), out_dtype),
        grid=(n // tm,),
        in_specs=[pl.BlockSpec((tm, d), lambda i: (i, 0)),
                  pl.BlockSpec((1, d), lambda i: (0, 0))],
        out_specs=pl.BlockSpec((tm, d), lambda i: (i, 0)),
        compiler_params=_params(("parallel",), big=True),
        name="rmsnorm",
    )(x2, g.reshape(1, d))


def _pack_bf16_pairs(y):
    w = y.shape[1] // 2
    bits = pltpu.bitcast(y.astype(BF16).astype(F32), jnp.uint32)
    return (bits[:, :w] >> 16) | (bits[:, w:] & jnp.uint32(0xFFFF0000))


def _unpack_pairs_f32(p):
    lo = pltpu.bitcast(p << 16, F32)
    hi = pltpu.bitcast(p & jnp.uint32(0xFFFF0000), F32)
    return jnp.concatenate([lo, hi], axis=1)


def _unpack_bf16_pairs(p):
    return _unpack_pairs_f32(p).astype(BF16)


def _rmsnorm_router_kernel(x_ref, g_ref, wr_ref, o_ref, l_ref):
    x = x_ref[...]
    y = x * lax.rsqrt(jnp.mean(x * x, axis=-1, keepdims=True) + RMS_EPS)
    y = y * g_ref[...]
    o_ref[...] = _pack_bf16_pairs(y)
    l_ref[...] = jnp.dot(y, wr_ref[...], preferred_element_type=F32,
                         precision=lax.Precision.HIGHEST)


def rmsnorm_router(x2, g, w_router_pad, tm=512):
    n, d = x2.shape
    ne = w_router_pad.shape[1]
    return pl.pallas_call(
        _rmsnorm_router_kernel,
        out_shape=(jax.ShapeDtypeStruct((n, d // 2), jnp.uint32), jax.ShapeDtypeStruct((n, ne), F32)),
        grid=(n // tm,),
        in_specs=[pl.BlockSpec((tm, d), lambda i: (i, 0)),
                  pl.BlockSpec((1, d), lambda i: (0, 0)),
                  pl.BlockSpec((d, ne), lambda i: (0, 0))],
        out_specs=(pl.BlockSpec((tm, d // 2), lambda i: (i, 0)),
                   pl.BlockSpec((tm, ne), lambda i: (i, 0))),
        compiler_params=_params(("parallel",)),
        name="rmsnorm_router",
    )(x2, g.reshape(1, d), w_router_pad)


def _mm_kernel(*refs, n_a, has_res):
    a_refs = refs[:n_a]
    w_refs = refs[n_a:2 * n_a]
    idx = 2 * n_a
    res_ref = refs[idx] if has_res else None
    idx += int(has_res)
    o_ref = refs[idx]
    wbf_refs = refs[idx + 1:idx + 1 + n_a]

    @pl.when(pl.program_id(1) == 0)
    def _():
        for w_ref, wbf_ref in zip(w_refs, wbf_refs):
            wbf_ref[...] = w_ref[...].astype(BF16)

    acc = None
    for a_ref, wbf_ref in zip(a_refs, wbf_refs):
        d = jnp.dot(a_ref[...].astype(BF16), wbf_ref[...], preferred_element_type=F32)
        acc = d if acc is None else acc + d
    if has_res:
        acc = acc + res_ref[...]
    o_ref[...] = acc.astype(o_ref.dtype)


def matmul(a_list, w, *, res=None, out_dtype=F32, tm=1024, tn=1024, name="matmul"):
    m = a_list[0].shape[0]
    n = w.shape[1]
    n_a = len(a_list)
    k_each = a_list[0].shape[1]
    assert all(a.shape == (m, k_each) for a in a_list) and w.shape[0] == n_a * k_each
    in_specs = [pl.BlockSpec((tm, k_each), lambda j, i: (i, 0)) for _ in a_list]
    in_specs += [pl.BlockSpec((k_each, tn), functools.partial(lambda j, i, kb: (kb, j), kb=kb))
                 for kb in range(n_a)]
    args = list(a_list) + [w] * n_a
    if res is not None:
        in_specs.append(pl.BlockSpec((tm, tn), lambda j, i: (i, j)))
        args.append(res)
    return pl.pallas_call(
        functools.partial(_mm_kernel, n_a=n_a, has_res=res is not None),
        out_shape=jax.ShapeDtypeStruct((m, n), out_dtype),
        grid=(n // tn, m // tm),
        in_specs=in_specs,
        out_specs=pl.BlockSpec((tm, tn), lambda j, i: (i, j)),
        scratch_shapes=[pltpu.VMEM((k_each, tn), BF16) for _ in a_list],
        compiler_params=_params(("parallel", "arbitrary"), big=True),
        name=name,
    )(*args)


def _rms_scale_bf16(x, g):
    y = x * lax.rsqrt(jnp.mean(x * x, axis=-1, keepdims=True) + RMS_EPS)
    return (y * g).astype(BF16)


def _ple_kernel(h_ref, p_ref, g_ref, wg_ref, wp_ref, o_ref, wg_bf, wp_bf):
    @pl.when(pl.program_id(1) == 0)
    def _():
        wg_bf[...] = wg_ref[...].astype(BF16)
        wp_bf[...] = wp_ref[...].astype(BF16)

    tn = o_ref.shape[1]
    a = jnp.dot(_rms_scale_bf16(h_ref[...], g_ref[...]), wg_bf[...], preferred_element_type=F32)
    pp = jnp.dot(p_ref[...].astype(BF16), wp_bf[...], preferred_element_type=F32)
    col = pl.multiple_of(pl.program_id(0) * tn, tn)
    o_ref[...] = h_ref[:, pl.ds(col, tn)] + jax.nn.sigmoid(a) * pp


def ple(h2, p_all, g, w_gate_all, w_proj_all, layer, tm=512, tn=1024):
    m, d = h2.shape
    pd = p_all.shape[2]
    return pl.pallas_call(
        _ple_kernel,
        out_shape=jax.ShapeDtypeStruct((m, d), F32),
        grid=(d // tn, m // tm),
        in_specs=[pl.BlockSpec((tm, d), lambda j, i: (i, 0)),
                  pl.BlockSpec((None, tm, pd), lambda j, i: (layer, i, 0)),
                  pl.BlockSpec((1, d), lambda j, i: (0, 0)),
                  pl.BlockSpec((None, d, tn), lambda j, i: (layer, 0, j)),
                  pl.BlockSpec((None, pd, tn), lambda j, i: (layer, 0, j))],
        out_specs=pl.BlockSpec((tm, tn), lambda j, i: (i, j)),
        scratch_shapes=[pltpu.VMEM((d, tn), BF16), pltpu.VMEM((pd, tn), BF16)],
        compiler_params=_params(("parallel", "arbitrary"), big=True),
        name="ple",
    )(h2, p_all, g.reshape(1, d), w_gate_all, w_proj_all)


def _row_copy(src_hbm, row, dst, i, sem):
    return pltpu.make_async_copy(src_hbm.at[pl.ds(row, 1), :], dst.at[pl.ds(i, 1), :], sem)


def _swiglu_kernel(exp_ref, rows_ref, src_ref, tok_ref, x_ref, *rest, n_col, gather, sub_rows, nf):
    del exp_ref
    gain_ref = None if gather else rest[0]
    wg_ref, wu_ref, wd_ref, o_ref = rest[0 if gather else 1:][:4]
    scratch = rest[(4 if gather else 5):]
    t = pl.program_id(0)
    f = pl.program_id(1)
    nt = pl.num_programs(0)
    n_rows = rows_ref[t]
    tm = o_ref.shape[0]

    if not gather:
        (xbf,) = scratch

        @pl.when(f == 0)
        def _():
            h = x_ref[...]
            o_ref[...] = h
            xbf[...] = _rms_scale_bf16(h, gain_ref[...])

    if gather:
        @pl.when(f == 0)
        def _():
            o_ref[...] = jnp.zeros_like(o_ref)

        xbuf, xbf, sems = scratch
        slot = t % 2
        n_fetch = xbuf.shape[1]
        per_step = n_fetch // nf

        def start_row(tile, sl, i):
            _row_copy(x_ref, tok_ref[src_ref[tile] + i], xbuf.at[sl], i, sems.at[sl, i]).start()

        def wait_all(sl):
            def body(c, carry):
                for r in range(DMA_LOOP_UNROLL):
                    i = c * DMA_LOOP_UNROLL + r
                    _row_copy(x_ref, 0, xbuf.at[sl], i, sems.at[sl, i]).wait()
                return carry
            lax.fori_loop(0, n_fetch // DMA_LOOP_UNROLL, body, 0)

        @pl.when(jnp.logical_and(f == 0, t == 0))
        def _():
            def body(i, carry):
                start_row(0, 0, i)
                return carry
            lax.fori_loop(0, n_fetch, body, 0, unroll=DMA_LOOP_UNROLL)

        @pl.when(f == 0)
        def _():
            wait_all(slot)
            xbf[...] = _unpack_bf16_pairs(xbuf[slot, 0:tm])

        def fetch_ahead():
            nxt = jnp.minimum(t + 1, nt - 1)
            for r in range(per_step):
                start_row(nxt, 1 - slot, f * per_step + r)
    else:
        def fetch_ahead():
            pass

    def ffn_rows(n):
        fetch_ahead()
        if n == 0:
            return
        x = xbf[0:n, :]
        g = jnp.dot(x, wg_ref[...].astype(BF16), preferred_element_type=F32)
        u = jnp.dot(x, wu_ref[...].astype(BF16), preferred_element_type=F32)
        hmid = (g * jax.nn.sigmoid(g) * u).astype(BF16)
        wd = wd_ref[...].astype(BF16)
        cw = o_ref.shape[1] // n_col
        for c in range(n_col):
            y = jnp.dot(hmid, wd[:, c * cw:(c + 1) * cw], preferred_element_type=F32)
            o_ref[0:n, c * cw:(c + 1) * cw] += y

    for k in range(0 if gather else 1, tm // sub_rows + 1):
        covers = jnp.logical_and(n_rows > (k - 1) * sub_rows, n_rows <= k * sub_rows)
        pl.when(covers)(functools.partial(ffn_rows, k * sub_rows))

    if gather:
        @pl.when(jnp.logical_and(t == nt - 1, f == nf - 1))
        def _():
            wait_all(1 - slot)


def gather_rows_per_tile(tm, nf):
    per_step = -(-tm // nf)
    while (per_step * nf) % DMA_LOOP_UNROLL:
        per_step += 1
    return per_step * nf


def swiglu(x, w_gate, w_up, w_down, tile_exp, tile_rows, tile_src=None, src_rows=None, *, gain=None,
           tm=MOE_TILE, tf=FFN_F_TILE, sub_rows=MOE_TILE, name="swiglu"):
    gather = src_rows is not None
    assert gather != (gain is not None)
    m = tile_rows.shape[0] * tm
    d = w_gate.shape[1]
    dff = w_gate.shape[2]
    nf = dff // tf
    nt = m // tm
    assert nf >= 2

    def w_in_map(t, f, exp_ref, rows_ref, src_ref, tok_ref):
        return (exp_ref[t], 0, jnp.where(rows_ref[t] > 0, f, nf - 1))

    def w_out_map(t, f, exp_ref, rows_ref, src_ref, tok_ref):
        return (exp_ref[t], jnp.where(rows_ref[t] > 0, f, nf - 1), 0)

    def tile_map(t, f, exp_ref, rows_ref, src_ref, tok_ref):
        return (t, 0)

    w_specs = [pl.BlockSpec((None, d, tf), w_in_map),
               pl.BlockSpec((None, d, tf), w_in_map),
               pl.BlockSpec((None, tf, d), w_out_map)]
    if gather:
        in_specs = [pl.BlockSpec(memory_space=pl.ANY)] + w_specs
        args = [x, w_gate, w_up, w_down]
        n_fetch = gather_rows_per_tile(tm, nf)
        scratch = [pltpu.VMEM((2, n_fetch, d // 2), jnp.uint32), pltpu.VMEM((tm, d), BF16),
                   pltpu.SemaphoreType.DMA((2, n_fetch))]
    else:
        in_specs = [pl.BlockSpec((tm, d), tile_map, pipeline_mode=pl.Buffered(1)),
                    pl.BlockSpec((1, d), lambda t, f, e, r, s, k: (0, 0))] + w_specs
        args = [x, gain.reshape(1, d), w_gate, w_up, w_down]
        scratch = [pltpu.VMEM((tm, d), BF16)]
        tile_src = jnp.zeros((nt,), jnp.int32)
        src_rows = jnp.zeros((1,), jnp.int32)
    grid_spec = pltpu.PrefetchScalarGridSpec(
        num_scalar_prefetch=4,
        grid=(nt, nf),
        in_specs=in_specs,
        out_specs=pl.BlockSpec((tm, d), tile_map, pipeline_mode=pl.Buffered(1)),
        scratch_shapes=scratch,
    )
    return pl.pallas_call(
        functools.partial(_swiglu_kernel, n_col=4, gather=gather, sub_rows=sub_rows, nf=nf),
        out_shape=jax.ShapeDtypeStruct((m, d), F32),
        grid_spec=grid_spec,
        compiler_params=_params(("arbitrary", "arbitrary"), big=True),
        name=name,
    )(tile_exp, tile_rows, tile_src, src_rows, *args)


COMBINE_TILE = 256


def _combine_kernel(d0_ref, d1_ref, ys_hbm, gate_ref, h_ref, hout_ref, buf, sems):
    i = pl.program_id(0)
    n = pl.num_programs(0)
    tt = h_ref.shape[0]
    slot = i % 2
    dests = (d0_ref, d1_ref)

    def start_tile(tile, sl):
        def body(r, c):
            for k, d_ref in enumerate(dests):
                _row_copy(ys_hbm, d_ref[tile * tt + r], buf.at[sl, k], r, sems.at[sl, k, r]).start()
            return c
        lax.fori_loop(0, tt, body, 0, unroll=DMA_LOOP_UNROLL)

    @pl.when(i == 0)
    def _():
        start_tile(0, 0)

    @pl.when(i + 1 < n)
    def _():
        start_tile(i + 1, 1 - slot)

    def wait_body(r, c):
        for k in range(TOP_K):
            _row_copy(ys_hbm, 0, buf.at[slot, k], r, sems.at[slot, k, r]).wait()
        return c
    lax.fori_loop(0, tt, wait_body, 0, unroll=DMA_LOOP_UNROLL)

    gates = gate_ref[...]
    hout_ref[...] = h_ref[...] + (buf[slot, 0] * gates[:, 0:1] + buf[slot, 1] * gates[:, 1:2])


def moe_combine(h2, ys, dest2, gates, tt=COMBINE_TILE):
    n, d = h2.shape
    grid_spec = pltpu.PrefetchScalarGridSpec(
        num_scalar_prefetch=2,
        grid=(n // tt,),
        in_specs=[pl.BlockSpec(memory_space=pl.ANY),
                  pl.BlockSpec((tt, TOP_K), lambda i, a, b: (i, 0)),
                  pl.BlockSpec((tt, d), lambda i, a, b: (i, 0))],
        out_specs=pl.BlockSpec((tt, d), lambda i, a, b: (i, 0)),
        scratch_shapes=[pltpu.VMEM((2, TOP_K, tt, d), F32), pltpu.SemaphoreType.DMA((2, TOP_K, tt))],
    )
    return pl.pallas_call(
        _combine_kernel,
        out_shape=jax.ShapeDtypeStruct((n, d), F32),
        grid_spec=grid_spec,
        compiler_params=_params(("arbitrary",), big=True),
        name="moe_combine",
    )(dest2[:, 0], dest2[:, 1], ys, gates, h2)


def _t5_bucket(rel):
    half = T5_BUCKETS // 2
    max_exact = half // 2
    n = jnp.abs(rel)
    log_ratio = jnp.log(jnp.maximum(n, 1).astype(F32) / max_exact) / math.log(T5_MAX_DIST / max_exact)
    large = jnp.minimum(max_exact + (log_ratio * (half - max_exact)).astype(jnp.int32), half - 1)
    return jnp.where(rel > 0, half, 0) + jnp.where(n < max_exact, n, large)


def _window_bias_table(t5_bias):
    i = jnp.arange(A_BLOCK)[:, None]
    j = jnp.arange(3 * A_BLOCK)[None, :]
    rel = j - A_BLOCK - i
    onehot = (_t5_bucket(rel)[None] == jnp.arange(T5_BUCKETS)[:, None, None]).astype(F32)
    bias = jnp.einsum('bh,bij->hij', t5_bias.astype(F32), onehot, precision=lax.Precision.HIGHEST)
    return jnp.where((jnp.abs(rel) <= A_BLOCK)[None], bias, NEG_INF)


def _window_kernel(sink_ref, q_ref, kv_ref, bias_ref, o_ref, *, nb):
    n = pl.program_id(1)
    scale = HEAD_DIM ** -0.5
    starts = (jnp.maximum(n - 1, 0), n, jnp.minimum(n + 1, nb - 1))
    kv = [kv_ref[pl.ds(pl.multiple_of(s * A_BLOCK, A_BLOCK), A_BLOCK), :] for s in starts]
    col = lax.broadcasted_iota(jnp.int32, (A_BLOCK, 3 * A_BLOCK), 1)
    edge_ok = jnp.logical_and(jnp.logical_or(n > 0, col >= A_BLOCK),
                              jnp.logical_or(n < nb - 1, col < 2 * A_BLOCK))
    kv = jnp.concatenate(kv, axis=0)
    q = q_ref[...]
    lo = lax.broadcasted_iota(jnp.int32, (1, 2 * HEAD_DIM), 1) < HEAD_DIM
    sel = (jnp.where(lo, scale, 0.0).astype(BF16), jnp.where(lo, 0.0, scale).astype(BF16))
    pairs_per_kv = A_GROUP // 2
    kk, vv = [], []
    for kh in range(A_KV_HEADS):
        k1 = kv[:, kh * HEAD_DIM:(kh + 1) * HEAD_DIM]
        v1 = kv[:, A_KV_WIDTH + kh * HEAD_DIM:A_KV_WIDTH + (kh + 1) * HEAD_DIM]
        kk.append(jnp.concatenate([k1, k1], axis=1))
        vv.append(jnp.concatenate([v1, v1], axis=1))
    scores = []
    for pr in range(A_Q_HEADS // 2):
        q2 = q[:, pr * 2 * HEAD_DIM:(pr + 1) * 2 * HEAD_DIM]
        qq = jnp.concatenate([q2 * sel[0], q2 * sel[1]], axis=0)
        s2 = lax.dot_general(qq, kk[pr // pairs_per_kv], (((1,), (1,)), ((), ())),
                             preferred_element_type=F32)
        for i in range(2):
            scores.append(jnp.where(edge_ok, s2[i * A_BLOCK:(i + 1) * A_BLOCK] + bias_ref[2 * pr + i],
                                    NEG_INF))
    probs, dens = [], []
    for h, s in enumerate(scores):
        sink = sink_ref[h]
        mx = jnp.maximum(jnp.max(s, axis=-1, keepdims=True), sink)
        p = jnp.exp(s - mx)
        dens.append(jnp.sum(p, axis=-1, keepdims=True) + jnp.exp(sink - mx))
        probs.append(p.astype(BF16))
    for pr in range(A_Q_HEADS // 2):
        v2 = vv[pr // pairs_per_kv]
        oa = jnp.dot(probs[2 * pr], v2, preferred_element_type=F32) / dens[2 * pr]
        ob = jnp.dot(probs[2 * pr + 1], v2, preferred_element_type=F32) / dens[2 * pr + 1]
        o_ref[:, pr * 2 * HEAD_DIM:(pr + 1) * 2 * HEAD_DIM] = jnp.where(lo, oa, ob).astype(o_ref.dtype)


def window_attention(proj_a, t5_bias, sink):
    b, s, _ = proj_a.shape
    nb = s // A_BLOCK
    kvw = 2 * A_KV_WIDTH
    grid_spec = pltpu.PrefetchScalarGridSpec(
        num_scalar_prefetch=0,
        grid=(b, nb),
        in_specs=[pl.BlockSpec(memory_space=pltpu.SMEM),
                  pl.BlockSpec((None, A_BLOCK, A_WIDTH), lambda bi, n: (bi, n, 0)),
                  pl.BlockSpec((None, s, kvw), lambda bi, n: (bi, 0, A_WIDTH // kvw)),
                  pl.BlockSpec((A_Q_HEADS, A_BLOCK, 3 * A_BLOCK), lambda bi, n: (0, 0, 0))],
        out_specs=pl.BlockSpec((None, A_BLOCK, A_WIDTH), lambda bi, n: (bi, n, 0)),
    )
    return pl.pallas_call(
        functools.partial(_window_kernel, nb=nb),
        out_shape=jax.ShapeDtypeStruct((b, s, A_WIDTH), BF16),
        grid_spec=grid_spec,
        compiler_params=_params(("parallel", "arbitrary")),
        name="window_attention",
    )(sink.astype(F32), proj_a, proj_a, _window_bias_table(t5_bias))


NA_HEAD_GROUP = 16


def _na_bias_table(rpb, rows):
    kh = min(NA_WIN_H, rows)
    kw = NA_WIN_W
    c = jnp.arange(GRID_W)
    cs = jnp.clip(c - kw // 2, 0, GRID_W - kw)
    col_ok = (c[None] >= cs[:, None]) & (c[None] < cs[:, None] + kw)
    col_off = jnp.clip(c[None] - c[:, None], -(kw - 1), kw - 1) + kw - 1
    onehot = (col_off[None] == jnp.arange(2 * kw - 1)[:, None, None]).astype(F32)
    by_col = jnp.einsum('hrc,cqk->hqrk', rpb.astype(F32), onehot, precision=lax.Precision.HIGHEST)
    by_col = jnp.where(col_ok[None, :, None, :], by_col, NEG_INF)
    tabs = [by_col[:, :, NA_WIN_H - 1 - d:NA_WIN_H - 1 - d + kh] for d in range(kh)]
    return jnp.stack(tabs, axis=0).reshape(kh, rpb.shape[0], GRID_W, kh * GRID_W)


def _na_kernel(q_ref, k_ref, v_ref, bias_ref, o_ref, *, rows):
    r = pl.program_id(2)
    kh = min(NA_WIN_H, rows)
    scale = HEAD_DIM ** -0.5
    start = pl.multiple_of(jnp.clip(r - kh // 2, 0, rows - kh) * GRID_W, GRID_W)
    k = k_ref[pl.ds(start, kh * GRID_W), :]
    v = v_ref[pl.ds(start, kh * GRID_W), :]
    q = q_ref[...]
    lo = lax.broadcasted_iota(jnp.int32, (1, 2 * HEAD_DIM), 1) < HEAD_DIM
    sel = (jnp.where(lo, scale, 0.0).astype(BF16), jnp.where(lo, 0.0, scale).astype(BF16))
    n_pairs = NA_HEAD_GROUP // 2
    scores = []
    for pr in range(n_pairs):
        sl = slice(pr * 2 * HEAD_DIM, (pr + 1) * 2 * HEAD_DIM)
        q2 = q[:, sl]
        qq = jnp.concatenate([q2 * sel[0], q2 * sel[1]], axis=0)
        s2 = lax.dot_general(qq, k[:, sl], (((1,), (1,)), ((), ())), preferred_element_type=F32)
        for i in range(2):
            scores.append(s2[i * GRID_W:(i + 1) * GRID_W] + bias_ref[2 * pr + i])
    probs, dens = [], []
    for s in scores:
        mx = jnp.max(s, axis=-1, keepdims=True)
        p = jnp.exp(s - mx)
        dens.append(jnp.sum(p, axis=-1, keepdims=True))
        probs.append(p.astype(BF16))
    for pr in range(n_pairs):
        sl = slice(pr * 2 * HEAD_DIM, (pr + 1) * 2 * HEAD_DIM)
        oa = jnp.dot(probs[2 * pr], v[:, sl], preferred_element_type=F32) / dens[2 * pr]
        ob = jnp.dot(probs[2 * pr + 1], v[:, sl], preferred_element_type=F32) / dens[2 * pr + 1]
        o_ref[:, sl] = jnp.where(lo, oa, ob).astype(o_ref.dtype)


def neighbourhood_attention(proj_n, rpb):
    b, s, _ = proj_n.shape
    rows = s // GRID_W
    kh = min(NA_WIN_H, rows)
    gw = NA_HEAD_GROUP * HEAD_DIM
    ng = B_WIDTH // gw

    def bias_map(bi, g, r):
        return (r - jnp.clip(r - kh // 2, 0, rows - kh), g, 0, 0)

    return pl.pallas_call(
        functools.partial(_na_kernel, rows=rows),
        out_shape=jax.ShapeDtypeStruct((b, s, B_WIDTH), BF16),
        grid=(b, ng, rows),
        in_specs=[pl.BlockSpec((None, GRID_W, gw), lambda bi, g, r: (bi, r, g)),
                  pl.BlockSpec((None, s, gw), lambda bi, g, r: (bi, 0, ng + g)),
                  pl.BlockSpec((None, s, gw), lambda bi, g, r: (bi, 0, 2 * ng + g)),
                  pl.BlockSpec((None, NA_HEAD_GROUP, GRID_W, kh * GRID_W), bias_map)],
        out_specs=pl.BlockSpec((None, GRID_W, gw), lambda bi, g, r: (bi, r, g)),
        compiler_params=_params(("parallel", "parallel", "arbitrary"), big=True),
        name="neighbourhood_attention",
    )(proj_n, proj_n, proj_n, _na_bias_table(rpb, rows))


CONV_HALO = 16
CONV_COL_CHUNK = 512


def _hyena_in_kernel(prev_ref, cur_ref, next_ref, w_ref, b_ref, cw_ref, cb_ref, o_ref, wbf_ref, *,
                     tiles_per_seq):
    i = pl.program_id(1)

    @pl.when(i == 0)
    def _():
        wbf_ref[...] = w_ref[...].astype(BF16)

    tm = cur_ref.shape[0]
    rows = tm + 2 * CONV_HALO
    a = jnp.concatenate([prev_ref[...], cur_ref[...], next_ref[...]], axis=0)
    pos = i % tiles_per_seq
    row = lax.broadcasted_iota(jnp.int32, (rows, 1), 0)
    kill_up = jnp.logical_and(pos == 0, row == CONV_HALO)
    kill_dn = jnp.logical_and(pos == tiles_per_seq - 1, row == CONV_HALO + tm - 1)
    for c in range(o_ref.shape[1] // CONV_COL_CHUNK):
        cs = slice(c * CONV_COL_CHUNK, (c + 1) * CONV_COL_CHUNK)
        p = jnp.dot(a, wbf_ref[:, cs], preferred_element_type=F32) + b_ref[:, cs]
        up = jnp.where(kill_up, 0.0, pltpu.roll(p, 1, 0))
        dn = jnp.where(kill_dn, 0.0, pltpu.roll(p, rows - 1, 0))
        y = up * cw_ref[0:1, cs] + p * cw_ref[1:2, cs] + dn * cw_ref[2:3, cs] + cb_ref[:, cs]
        o_ref[:, cs] = y[CONV_HALO:CONV_HALO + tm]


def hyena_in_conv(hn, w_in, b_in, conv_w, conv_b, seq, tm=1024, tn=1024):
    m, k = hn.shape
    n = w_in.shape[1]
    hb = tm // CONV_HALO
    last_halo = m // CONV_HALO - 1
    return pl.pallas_call(
        functools.partial(_hyena_in_kernel, tiles_per_seq=seq // tm),
        out_shape=jax.ShapeDtypeStruct((m, n), F32),
        grid=(n // tn, m // tm),
        in_specs=[pl.BlockSpec((CONV_HALO, k), lambda j, i: (jnp.maximum(i * hb - 1, 0), 0)),
                  pl.BlockSpec((tm, k), lambda j, i: (i, 0)),
                  pl.BlockSpec((CONV_HALO, k), lambda j, i: (jnp.minimum((i + 1) * hb, last_halo), 0)),
                  pl.BlockSpec((k, tn), lambda j, i: (0, j)),
                  pl.BlockSpec((1, tn), lambda j, i: (0, j)),
                  pl.BlockSpec((3, tn), lambda j, i: (0, j)),
                  pl.BlockSpec((1, tn), lambda j, i: (0, j))],
        out_specs=pl.BlockSpec((tm, tn), lambda j, i: (i, j)),
        scratch_shapes=[pltpu.VMEM((k, tn), BF16)],
        compiler_params=_params(("parallel", "arbitrary"), big=True),
        name="hyena_in_conv",
    )(hn, hn, hn, w_in, b_in.reshape(1, n), conv_w, conv_b.reshape(1, n))


def _filter_stage1_kernel(z_ref, t_ref, fw1_ref, fb1_ref, fw2_ref, fb2_ref, fr_ref, w3f0_ref, w3f1_ref,
                          w3b0_ref, w3b1_ref, delta_ref, f1_ref, y_ref, hid_ref, *, seq):
    j = pl.program_id(0)

    @pl.when(pl.program_id(1) == 0)
    def _():
        hi = lax.Precision.HIGHEST
        fr = fr_ref[...]
        hid = jnp.sin(fr * (jnp.dot(z_ref[...], fw1_ref[...], preferred_element_type=F32, precision=hi)
                            + fb1_ref[...]))
        hid = jnp.sin(fr * (jnp.dot(hid, fw2_ref[...], preferred_element_type=F32, precision=hi)
                            + fb2_ref[...]))
        hid_ref[...] = hid.astype(BF16)

    hid_bf = hid_ref[...]
    r = lax.broadcasted_iota(jnp.int32, (hid_bf.shape[0], 1), 0)
    u = j * FFT_CHUNK + r // FFT_N2 + FFT_N1 * (r % FFT_N2)
    decay = jnp.where(u == seq, 0.0, jnp.exp(-t_ref[...] * delta_ref[...]))
    for o, (wf_ref, wb_ref) in enumerate(((w3f0_ref, w3b0_ref), (w3f1_ref, w3b1_ref))):
        fwd = jnp.dot(hid_bf, wf_ref[...].astype(BF16), preferred_element_type=F32)
        bwd = jnp.dot(hid_bf, wb_ref[...].astype(BF16), preferred_element_type=F32)
        taps = (jnp.where(u > seq, bwd, jnp.where(u == 0, fwd + bwd, fwd)) * decay).astype(BF16)
        for jj in range(FFT_CHUNK):
            x = taps[jj * FFT_N2:(jj + 1) * FFT_N2]
            y_ref[o, jj] = _pack_bf16_pairs(jnp.dot(f1_ref[jj], x, preferred_element_type=F32))


def hyena_filter_stage1(seq, fw1, fb1, fw2, fb2, freq, fw3, width, consts, cb=FFT_CB):
    _, f1_real, _, _, _ = consts
    n = 2 * seq
    n1, n2, ch = FFT_N1, FFT_N2, FFT_CHUNK
    assert fw3.shape[1] == 4 * width
    r = jnp.arange(n)
    u = (r // (ch * n2)) * ch + (r // n2) % ch + n1 * (r % n2)
    pos = jnp.where(u <= seq, u, n - u)
    pos = jnp.where(u == seq, 0, pos)
    bands = (HYENA_EMB - 1) // 2
    f = jnp.linspace(1e-4, bands - 1, bands, dtype=F32)[None]
    t = (pos.astype(F32) / (seq - 1))[:, None]
    w = ((2.0 * math.pi / seq) * pos.astype(F32))[:, None]
    z = jnp.concatenate([t, jnp.cos(f * w), -jnp.sin(f * w)], axis=-1)
    emb_pad = 40
    z = jnp.pad(z, ((0, 0), (0, emb_pad - HYENA_EMB)))
    fw1p = jnp.pad(fw1.astype(F32), ((0, emb_pad - HYENA_EMB), (0, 0)))
    fh = fw1.shape[1]
    max_decay = math.log(HYENA_DECAY_TARGET) / HYENA_FAST_PCT
    min_decay = math.log(HYENA_DECAY_TARGET) / HYENA_SLOW_PCT
    deltas = jnp.abs(jnp.linspace(min_decay, max_decay, width, dtype=F32))[None]
    ncb = width // cb
    rows = ch * n2

    def small(shape):
        return pl.BlockSpec(shape, lambda j, k: (0, 0))

    def w3_spec(blk):
        return pl.BlockSpec((fh, cb), lambda j, k: (0, blk * ncb + k))

    return pl.pallas_call(
        functools.partial(_filter_stage1_kernel, seq=seq),
        out_shape=jax.ShapeDtypeStruct((2, n1, 2 * n2, width // 2), jnp.uint32),
        grid=(n1 // ch, ncb),
        in_specs=[pl.BlockSpec((rows, emb_pad), lambda j, k: (j, 0)),
                  pl.BlockSpec((rows, 1), lambda j, k: (j, 0)),
                  small((emb_pad, fh)), small((1, fh)), small((fh, fh)), small((1, fh)), small((1, fh)),
                  w3_spec(0), w3_spec(1), w3_spec(2), w3_spec(3),
                  pl.BlockSpec((1, cb), lambda j, k: (0, k)),
                  pl.BlockSpec((ch, 2 * n2, n2), lambda j, k: (j, 0, 0))],
        out_specs=pl.BlockSpec((2, ch, 2 * n2, cb // 2), lambda j, k: (0, j, 0, k)),
        scratch_shapes=[pltpu.VMEM((rows, fh), BF16)],
        compiler_params=_params(("parallel", "arbitrary"), big=True),
        name="hyena_filter_stage1",
    )(z, t, fw1p, fb1.reshape(1, fh).astype(F32), fw2.astype(F32), fb2.reshape(1, fh).astype(F32),
      freq.reshape(1, fh).astype(F32), fw3, fw3, fw3, fw3, deltas, f1_real)


def _dft_constants():
    n1, n2 = FFT_N1, FFT_N2
    n = n1 * n2
    a1 = np.arange(n1)
    a2 = np.arange(n2)
    half = n2 // 2

    def cplx_block(w):
        return np.block([[w.real, -w.imag], [w.imag, w.real]])

    ang = (a2[None, :, None] * a2[None, None, :] / n2) + (a1[:, None, None] * a2[None, :, None] / n)
    w1 = np.exp(-2j * np.pi * ang)
    f1_pad = np.stack([cplx_block(w1[i][:, :half]) for i in range(n1)])
    f1_real = np.stack([np.concatenate([w1[i].real, w1[i].imag], axis=0) for i in range(n1)])
    w2 = np.exp(-2j * np.pi * (a1[:, None] * a1[None, :]) / n1)
    f2 = cplx_block(w2)
    ang = (a1[None, :, None] * a1[None, None, :] / n1) + (a2[:, None, None] * a1[None, :, None] / n)
    g2 = np.stack([cplx_block(m_) for m_ in np.exp(2j * np.pi * ang)])
    wg1 = np.exp(2j * np.pi * (a2[:half, None] * a2[None, :]) / n2) / n
    g1 = cplx_block(wg1)
    to = lambda x: jnp.asarray(x.astype(np.float32)).astype(BF16)
    return to(f1_pad), to(f1_real), to(f2), to(g2), to(g1)


def _fft_stage1_block(x_re, x_im, f_ref, y_ref):
    xr = pltpu.einshape("mjc->jmc", x_re)
    xi = pltpu.einshape("mjc->jmc", x_im)
    for j in range(FFT_CHUNK):
        xc = jnp.concatenate([xr[j], xi[j]], axis=0).astype(BF16)
        y_ref[j] = _pack_bf16_pairs(jnp.dot(f_ref[j], xc, preferred_element_type=F32))


def _fft_stage1_kernel(x_ref, f_ref, y_ref):
    _fft_stage1_block(x_ref[0], x_ref[1], f_ref, y_ref)


def _fft_stage2_kernel(yr_ref, yi_ref, f2_ref, h_ref):
    yr = pltpu.einshape("nkc->knc", yr_ref[...])
    yi = pltpu.einshape("nkc->knc", yi_ref[...])
    for j in range(FFT_CHUNK):
        yc = _unpack_bf16_pairs(jnp.concatenate([yr[j], yi[j]], axis=0))
        h_ref[j] = _pack_bf16_pairs(jnp.dot(f2_ref[...], yc, preferred_element_type=F32))


def _fft_mid_kernel(yr_ref, yi_ref, f2_ref, h_ref, g2_ref, q_ref):
    yr = pltpu.einshape("nkc->knc", yr_ref[...])
    yi = pltpu.einshape("nkc->knc", yi_ref[...])
    n1 = FFT_N1
    spec = [jnp.dot(f2_ref[...], _unpack_bf16_pairs(jnp.concatenate([yr[j], yi[j]], axis=0)),
                    preferred_element_type=F32) for j in range(FFT_CHUNK)]
    prod = []
    for j, z in enumerate(spec):
        hf = _unpack_pairs_f32(h_ref[j])
        zr, zi, hr, hi = z[:n1], z[n1:], hf[:n1], hf[n1:]
        prod.append(jnp.concatenate([zr * hr - zi * hi, zr * hi + zi * hr], axis=0).astype(BF16))
    for j, pc in enumerate(prod):
        q_ref[j] = _pack_bf16_pairs(jnp.dot(g2_ref[j], pc, preferred_element_type=F32))


def _fft_last_kernel(qr_ref, qi_ref, g1_ref, gate_ref, zf_ref, fb_ref, *rest, feeds_next):
    f1_ref, o_ref, y_ref = rest if feeds_next else (None, rest[0], None)
    qr = pltpu.einshape("kjc->jkc", qr_ref[...])
    qi = pltpu.einshape("kjc->jkc", qi_ref[...])
    half = FFT_N2 // 2
    conv = []
    for j in range(FFT_CHUNK):
        qc = _unpack_bf16_pairs(jnp.concatenate([qr[j], qi[j]], axis=0))
        conv.append(jnp.dot(g1_ref[...], qc, preferred_element_type=F32))
    conv = jnp.stack(conv, axis=0)
    fb = fb_ref[...]
    out = []
    for bi in range(2):
        cb = pltpu.einshape("jmc->mjc", conv[:, bi * half:(bi + 1) * half])
        out.append(gate_ref[bi] * (cb + zf_ref[bi] * fb))
        o_ref[bi] = out[bi]
    if feeds_next:
        _fft_stage1_block(out[0], out[1], f1_ref, y_ref)


def hyena_filter_spectrum(y, consts, cb=FFT_CB):
    _, _, f2, _, _ = consts
    no = y.shape[0]
    c = 2 * y.shape[-1]
    n1, n2, ch = FFT_N1, FFT_N2, FFT_CHUNK
    nk = n2 // ch
    return pl.pallas_call(
        _fft_stage2_kernel,
        out_shape=jax.ShapeDtypeStruct((no, n2, 2 * n1, c // 2), jnp.uint32),
        grid=(no, nk, c // cb),
        in_specs=[pl.BlockSpec((None, n1, ch, cb // 2), lambda o, j, k: (o, 0, j, k)),
                  pl.BlockSpec((None, n1, ch, cb // 2), lambda o, j, k: (o, 0, nk + j, k)),
                  pl.BlockSpec((2 * n1, 2 * n1), lambda o, j, k: (0, 0))],
        out_specs=pl.BlockSpec((None, ch, 2 * n1, cb // 2), lambda o, j, k: (o, j, 0, k)),
        compiler_params=_params(("parallel", "parallel", "parallel"), big=True),
        name="fft_filter_stage2",
    )(y, y, f2)


def hyena_long_conv_gate(zsrc, z_col, gate_src, gate_col, hf, order, fbias, consts, y=None,
                         feeds_next=False, cb=FFT_CB):
    f1_pad, _, f2, g2, g1 = consts
    n1, n2, ch = FFT_N1, FFT_N2, FFT_CHUNK
    c = 2 * hf.shape[-1]
    ncb = c // cb
    half = n2 // 2
    y_shape = jax.ShapeDtypeStruct((n1, 2 * n2, c // 2), jnp.uint32)
    f1_spec = pl.BlockSpec((ch, 2 * n2, n2), lambda j, k: (j, 0, 0))
    y_spec = pl.BlockSpec((ch, 2 * n2, cb // 2), lambda j, k: (j, 0, k))
    if y is None:
        y = pl.pallas_call(
            _fft_stage1_kernel,
            out_shape=y_shape,
            grid=(n1 // ch, ncb),
            in_specs=[pl.BlockSpec((2, half, ch, cb), lambda j, k: (0, 0, j, z_col * ncb + k)), f1_spec],
            out_specs=y_spec,
            compiler_params=_params(("parallel", "parallel"), big=True),
            name="fft_stage1",
        )(zsrc, f1_pad)
    nk = n2 // ch
    q = pl.pallas_call(
        _fft_mid_kernel,
        out_shape=jax.ShapeDtypeStruct((n2, 2 * n1, c // 2), jnp.uint32),
        grid=(nk, ncb),
        in_specs=[pl.BlockSpec((n1, ch, cb // 2), lambda j, k: (0, j, k)),
                  pl.BlockSpec((n1, ch, cb // 2), lambda j, k: (0, nk + j, k)),
                  pl.BlockSpec((2 * n1, 2 * n1), lambda j, k: (0, 0)),
                  pl.BlockSpec((None, ch, 2 * n1, cb // 2), lambda j, k: (order, j, 0, k)),
                  pl.BlockSpec((ch, 2 * n1, 2 * n1), lambda j, k: (j, 0, 0))],
        out_specs=pl.BlockSpec((ch, 2 * n1, cb // 2), lambda j, k: (j, 0, k)),
        compiler_params=_params(("parallel", "parallel"), big=True),
        name="fft_mid",
    )(y, y, f2, hf, g2)
    nj = n1 // ch
    in_specs = [pl.BlockSpec((n2, ch, cb // 2), lambda j, k: (0, j, k)),
                pl.BlockSpec((n2, ch, cb // 2), lambda j, k: (0, nj + j, k)),
                pl.BlockSpec((2 * half, 2 * n2), lambda j, k: (0, 0)),
                pl.BlockSpec((2, half, ch, cb), lambda j, k: (0, 0, j, gate_col * ncb + k)),
                pl.BlockSpec((2, half, ch, cb), lambda j, k: (0, 0, j, z_col * ncb + k)),
                pl.BlockSpec((None, 1, cb), lambda j, k: (order, 0, k))]
    args = [q, q, g1, gate_src, zsrc, fbias.reshape(fbias.shape[0], 1, c)]
    out_shape = jax.ShapeDtypeStruct((2, half, n1, c), F32)
    out_spec = pl.BlockSpec((2, half, ch, cb), lambda j, k: (0, 0, j, k))
    if feeds_next:
        in_specs.append(f1_spec)
        args.append(f1_pad)
        out_shape, out_spec = (out_shape, y_shape), (out_spec, y_spec)
    res = pl.pallas_call(
        functools.partial(_fft_last_kernel, feeds_next=feeds_next),
        out_shape=out_shape,
        grid=(nj, ncb),
        in_specs=in_specs,
        out_specs=out_spec,
        compiler_params=_params(("parallel", "parallel"), big=True),
        name="fft_last",
    )(*args)
    return res if feeds_next else (res, None)


def hyena_mixer(hn, h_res, w_in, b_in, conv_w, conv_b, fw1, fb1, fw2, fb2, freq, fw3, fbias, w_out,
                batch, seq):
    width = w_out.shape[0]
    assert batch == 2 and 2 * seq == FFT_N1 * FFT_N2
    sc = hyena_in_conv(hn, w_in, b_in, conv_w, conv_b, seq)
    consts = _dft_constants()
    hf = hyena_filter_spectrum(hyena_filter_stage1(seq, fw1, fb1, fw2, fb2, freq, fw3, width, consts),
                               consts)
    sc4 = sc.reshape(batch, FFT_N2 // 2, FFT_N1, 3 * width)
    zf1, y1 = hyena_long_conv_gate(sc4, 2, sc4, 0, hf, 0, fbias, consts, feeds_next=True)
    zf2, _ = hyena_long_conv_gate(zf1, 0, sc4, 1, hf, 1, fbias, consts, y=y1)
    return matmul([zf2.reshape(batch * seq, width)], w_out, res=h_res, tm=512, tn=1024,
                  name="hyena_out")


def moe_swiglu(hn2_packed, logits, wg, wu, wd):
    n = hn2_packed.shape[0]
    top_v, top_i = lax.top_k(logits, TOP_K)
    gates = jax.nn.softmax(top_v, axis=-1)
    e_flat = top_i.reshape(-1).astype(jnp.int32)
    nk = n * TOP_K
    onehot = (e_flat[:, None] == jnp.arange(N_EXPERTS, dtype=jnp.int32)[None]).astype(jnp.int32)
    csum = jnp.cumsum(onehot, axis=0)
    rank = jnp.take_along_axis(csum, e_flat[:, None], axis=1)[:, 0] - 1
    counts = csum[-1]
    padded = ((counts + MOE_TILE - 1) // MOE_TILE) * MOE_TILE
    pad_end = jnp.cumsum(padded)
    pad_start = pad_end - padded
    dest = pad_start[e_flat] + rank
    p_rows = nk + N_EXPERTS * MOE_TILE
    order = jnp.argsort(e_flat, stable=True).astype(jnp.int32)
    nf = wg.shape[2] // MOE_F_TILE
    sorted_tok = jnp.pad(order // TOP_K, (0, gather_rows_per_tile(MOE_TILE, nf)))
    start = jnp.cumsum(counts) - counts
    nt = p_rows // MOE_TILE
    tile_start = jnp.arange(nt, dtype=jnp.int32) * MOE_TILE
    tile_used = tile_start < pad_end[-1]
    tile_exp = jnp.minimum(jnp.searchsorted(pad_end, tile_start, side='right'), N_EXPERTS - 1).astype(jnp.int32)
    tile_rows = jnp.clip(counts[tile_exp] - (tile_start - pad_start[tile_exp]), 0, MOE_TILE)
    tile_rows = jnp.where(tile_used, tile_rows, 0).astype(jnp.int32)
    last_exp = tile_exp[jnp.maximum(jnp.sum(tile_used.astype(jnp.int32)) - 1, 0)]
    tile_src = jnp.where(tile_used, start[tile_exp] + (tile_start - pad_start[tile_exp]), 0)
    tile_exp = jnp.where(tile_used, tile_exp, last_exp)
    ys = swiglu(hn2_packed, wg, wu, wd, tile_exp, tile_rows, tile_src.astype(jnp.int32), sorted_tok,
                tf=MOE_F_TILE, sub_rows=FFN_SUB_ROWS, name="swiglu_experts")
    return ys, dest.reshape(n, TOP_K), gates


def kernel(x, p, ln_mix, ln_ffn, ln_ple, final_norm, t5_bias, w_attn_in, w_attn_out, attn_sink, na_rpb, w_ffn_gate, w_ffn_up, w_ffn_down, w_hy_in, b_hy_in, w_hy_conv, b_hy_conv, w_hy_f1, b_hy_f1, w_hy_f2, b_hy_f2, hy_freq, w_hy_f3, hy_bias, w_hy_out, w_router, w_exp_gate, w_exp_up, w_exp_down, w_ple_proj, w_ple_gate):
    batch, seq, d = x.shape
    n = batch * seq
    depth = ln_mix.shape[0]
    h = x.reshape(n, d)
    for i in range(depth):
        li = i // 2
        hn = rmsnorm(h, ln_mix[i], BF16)
        if i % 2 == 0:
            na_off = A_WIDTH + 2 * A_KV_WIDTH
            w_in = w_attn_in[li]
            proj_a = matmul([hn], w_in[:, :na_off], out_dtype=BF16, tn=na_off, name="attn_in_a")
            proj_n = matmul([hn], w_in[:, na_off:], out_dtype=BF16, tn=3 * B_WIDTH // 2, name="attn_in_n")
            oa = window_attention(proj_a.reshape(batch, seq, -1), t5_bias, attn_sink[li])
            ob = neighbourhood_attention(proj_n.reshape(batch, seq, -1), na_rpb[li])
            h = matmul([oa.reshape(n, A_WIDTH), ob.reshape(n, B_WIDTH)], w_attn_out[li], res=h,
                       name="attn_out")
            nt = n // MOE_TILE
            h = swiglu(h, w_ffn_gate, w_ffn_up, w_ffn_down,
                       jnp.full((nt,), li, jnp.int32), jnp.full((nt,), MOE_TILE, jnp.int32),
                       gain=ln_ffn[i], name="swiglu_dense")
        else:
            h = hyena_mixer(hn, h, w_hy_in[li], b_hy_in[li], w_hy_conv[li], b_hy_conv[li],
                            w_hy_f1[li], b_hy_f1[li], w_hy_f2[li], b_hy_f2[li], hy_freq[li],
                            w_hy_f3[li], hy_bias[li], w_hy_out[li], batch, seq)
            wr_pad = jnp.pad(w_router[li].astype(F32), ((0, 0), (0, 128 - N_EXPERTS)))
            hn2_packed, logits = rmsnorm_router(h, ln_ffn[i], wr_pad)
            ys, dest2, gates = moe_swiglu(hn2_packed, logits[:, :N_EXPERTS], w_exp_gate[li],
                                          w_exp_up[li], w_exp_down[li])
            h = moe_combine(h, ys, dest2, gates)
        h = ple(h, p.reshape(depth, n, -1), ln_ple[i], w_ple_gate, w_ple_proj, i)
    return rmsnorm(h, final_norm, F32).reshape(batch, seq, d)
```

```python
import functools
import math

import jax
import jax.numpy as jnp
import numpy as np
from jax import lax
from jax.experimental import pallas as pl
from jax.experimental.pallas import tpu as pltpu

F32 = jnp.float32
BF16 = jnp.bfloat16
NEG_INF = -1e30
RMS_EPS = 1e-6

V7X_VMEM_LIMIT_BYTES = 56 * 1024 * 1024

HEAD_DIM = 64
A_Q_HEADS = 16
A_KV_HEADS = 2
A_GROUP = A_Q_HEADS // A_KV_HEADS
A_BLOCK = 128
T5_BUCKETS = 32
T5_MAX_DIST = 128
B_HEADS = 16
GRID_W = 64
NA_WIN_H = 8
NA_WIN_W = 16
A_WIDTH = A_Q_HEADS * HEAD_DIM
A_KV_WIDTH = A_KV_HEADS * HEAD_DIM
B_WIDTH = B_HEADS * HEAD_DIM
N_EXPERTS = 8
TOP_K = 2
HYENA_EMB = 33
HYENA_DECAY_TARGET = 1e-2
HYENA_FAST_PCT = 0.3
HYENA_SLOW_PCT = 1.5

FFT_N1 = 64
FFT_N2 = 128
FFT_CHUNK = 16
FFT_CB = 512

MOE_TILE = 1024
FFN_F_TILE = 512
MOE_F_TILE = 512
FFN_SUB_ROWS = 256
DMA_LOOP_UNROLL = 8


def _params(semantics, big=False):
    return pltpu.CompilerParams(
        dimension_semantics=semantics,
        vmem_limit_bytes=V7X_VMEM_LIMIT_BYTES if big else None)


def _rmsnorm_kernel(x_ref, g_ref, o_ref):
    x = x_ref[...]
    y = x * lax.rsqrt(jnp.mean(x * x, axis=-1, keepdims=True) + RMS_EPS)
    o_ref[...] = (y * g_ref[...]).astype(o_ref.dtype)


def rmsnorm(x2, g, out_dtype, tm=1024):
    n, d = x2.shape
    return pl.pallas_call(
        _rmsnorm_kernel,
        out_shape=jax.ShapeDtypeStruct((n, d), out_dtype),
        grid=(n // tm,),
        in_specs=[pl.BlockSpec((tm, d), lambda i: (i, 0)),
                  pl.BlockSpec((1, d), lambda i: (0, 0))],
        out_specs=pl.BlockSpec((tm, d), lambda i: (i, 0)),
        compiler_params=_params(("parallel",), big=True),
        name="rmsnorm",
    )(x2, g.reshape(1, d))


def _pack_bf16_pairs(y):
    w = y.shape[1] // 2
    bits = pltpu.bitcast(y.astype(BF16).astype(F32), jnp.uint32)
    return (bits[:, :w] >> 16) | (bits[:, w:] & jnp.uint32(0xFFFF0000))


def _unpack_bf16_pairs(p):
    lo = pltpu.bitcast(p << 16, F32)
    hi = pltpu.bitcast(p & jnp.uint32(0xFFFF0000), F32)
    return jnp.concatenate([lo, hi], axis=1).astype(BF16)


def _rmsnorm_router_kernel(x_ref, g_ref, wr_ref, o_ref, l_ref):
    x = x_ref[...]
    y = x * lax.rsqrt(jnp.mean(x * x, axis=-1, keepdims=True) + RMS_EPS)
    y = y * g_ref[...]
    o_ref[...] = _pack_bf16_pairs(y)
    l_ref[...] = jnp.dot(y, wr_ref[...], preferred_element_type=F32,
                         precision=lax.Precision.HIGHEST)


def rmsnorm_router(x2, g, w_router_pad, tm=512):
    n, d = x2.shape
    ne = w_router_pad.shape[1]
    return pl.pallas_call(
        _rmsnorm_router_kernel,
        out_shape=(jax.ShapeDtypeStruct((n, d // 2), jnp.uint32), jax.ShapeDtypeStruct((n, ne), F32)),
        grid=(n // tm,),
        in_specs=[pl.BlockSpec((tm, d), lambda i: (i, 0)),
                  pl.BlockSpec((1, d), lambda i: (0, 0)),
                  pl.BlockSpec((d, ne), lambda i: (0, 0))],
        out_specs=(pl.BlockSpec((tm, d // 2), lambda i: (i, 0)),
                   pl.BlockSpec((tm, ne), lambda i: (i, 0))),
        compiler_params=_params(("parallel",)),
        name="rmsnorm_router",
    )(x2, g.reshape(1, d), w_router_pad)


def _mm_kernel(*refs, n_a, has_res):
    a_refs = refs[:n_a]
    w_refs = refs[n_a:2 * n_a]
    idx = 2 * n_a
    res_ref = refs[idx] if has_res else None
    idx += int(has_res)
    o_ref = refs[idx]
    wbf_refs = refs[idx + 1:idx + 1 + n_a]

    @pl.when(pl.program_id(1) == 0)
    def _():
        for w_ref, wbf_ref in zip(w_refs, wbf_refs):
            wbf_ref[...] = w_ref[...].astype(BF16)

    acc = None
    for a_ref, wbf_ref in zip(a_refs, wbf_refs):
        d = jnp.dot(a_ref[...].astype(BF16), wbf_ref[...], preferred_element_type=F32)
        acc = d if acc is None else acc + d
    if has_res:
        acc = acc + res_ref[...]
    o_ref[...] = acc.astype(o_ref.dtype)


def matmul(a_list, w, *, res=None, out_dtype=F32, tm=1024, tn=1024, name="matmul"):
    m = a_list[0].shape[0]
    n = w.shape[1]
    n_a = len(a_list)
    k_each = a_list[0].shape[1]
    assert all(a.shape == (m, k_each) for a in a_list) and w.shape[0] == n_a * k_each
    in_specs = [pl.BlockSpec((tm, k_each), lambda j, i: (i, 0)) for _ in a_list]
    in_specs += [pl.BlockSpec((k_each, tn), functools.partial(lambda j, i, kb: (kb, j), kb=kb))
                 for kb in range(n_a)]
    args = list(a_list) + [w] * n_a
    if res is not None:
        in_specs.append(pl.BlockSpec((tm, tn), lambda j, i: (i, j)))
        args.append(res)
    return pl.pallas_call(
        functools.partial(_mm_kernel, n_a=n_a, has_res=res is not None),
        out_shape=jax.ShapeDtypeStruct((m, n), out_dtype),
        grid=(n // tn, m // tm),
        in_specs=in_specs,
        out_specs=pl.BlockSpec((tm, tn), lambda j, i: (i, j)),
        scratch_shapes=[pltpu.VMEM((k_each, tn), BF16) for _ in a_list],
        compiler_params=_params(("parallel", "arbitrary"), big=True),
        name=name,
    )(*args)


def _rms_scale_bf16(x, g):
    y = x * lax.rsqrt(jnp.mean(x * x, axis=-1, keepdims=True) + RMS_EPS)
    return (y * g).astype(BF16)


def _ple_kernel(h_ref, p_ref, g_ref, wg_ref, wp_ref, o_ref, wg_bf, wp_bf):
    @pl.when(pl.program_id(1) == 0)
    def _():
        wg_bf[...] = wg_ref[...].astype(BF16)
        wp_bf[...] = wp_ref[...].astype(BF16)

    tn = o_ref.shape[1]
    a = jnp.dot(_rms_scale_bf16(h_ref[...], g_ref[...]), wg_bf[...], preferred_element_type=F32)
    pp = jnp.dot(p_ref[...].astype(BF16), wp_bf[...], preferred_element_type=F32)
    col = pl.multiple_of(pl.program_id(0) * tn, tn)
    o_ref[...] = h_ref[:, pl.ds(col, tn)] + jax.nn.sigmoid(a) * pp


def ple(h2, p_all, g, w_gate_all, w_proj_all, layer, tm=512, tn=1024):
    m, d = h2.shape
    pd = p_all.shape[2]
    return pl.pallas_call(
        _ple_kernel,
        out_shape=jax.ShapeDtypeStruct((m, d), F32),
        grid=(d // tn, m // tm),
        in_specs=[pl.BlockSpec((tm, d), lambda j, i: (i, 0)),
                  pl.BlockSpec((None, tm, pd), lambda j, i: (layer, i, 0)),
                  pl.BlockSpec((1, d), lambda j, i: (0, 0)),
                  pl.BlockSpec((None, d, tn), lambda j, i: (layer, 0, j)),
                  pl.BlockSpec((None, pd, tn), lambda j, i: (layer, 0, j))],
        out_specs=pl.BlockSpec((tm, tn), lambda j, i: (i, j)),
        scratch_shapes=[pltpu.VMEM((d, tn), BF16), pltpu.VMEM((pd, tn), BF16)],
        compiler_params=_params(("parallel", "arbitrary"), big=True),
        name="ple",
    )(h2, p_all, g.reshape(1, d), w_gate_all, w_proj_all)


def _row_copy(src_hbm, row, dst, i, sem):
    return pltpu.make_async_copy(src_hbm.at[pl.ds(row, 1), :], dst.at[pl.ds(i, 1), :], sem)


def _swiglu_kernel(exp_ref, rows_ref, src_ref, tok_ref, x_ref, *rest, n_col, gather, sub_rows, nf):
    del exp_ref
    gain_ref = None if gather else rest[0]
    wg_ref, wu_ref, wd_ref, o_ref = rest[0 if gather else 1:][:4]
    scratch = rest[(4 if gather else 5):]
    t = pl.program_id(0)
    f = pl.program_id(1)
    nt = pl.num_programs(0)
    n_rows = rows_ref[t]
    tm = o_ref.shape[0]

    if not gather:
        (xbf,) = scratch

        @pl.when(f == 0)
        def _():
            h = x_ref[...]
            o_ref[...] = h
            xbf[...] = _rms_scale_bf16(h, gain_ref[...])

    if gather:
        @pl.when(f == 0)
        def _():
            o_ref[...] = jnp.zeros_like(o_ref)

        xbuf, xbf, sems = scratch
        slot = t % 2
        n_fetch = xbuf.shape[1]
        per_step = n_fetch // nf

        def start_row(tile, sl, i):
            _row_copy(x_ref, tok_ref[src_ref[tile] + i], xbuf.at[sl], i, sems.at[sl, i]).start()

        def wait_all(sl):
            def body(c, carry):
                for r in range(DMA_LOOP_UNROLL):
                    i = c * DMA_LOOP_UNROLL + r
                    _row_copy(x_ref, 0, xbuf.at[sl], i, sems.at[sl, i]).wait()
                return carry
            lax.fori_loop(0, n_fetch // DMA_LOOP_UNROLL, body, 0)

        @pl.when(jnp.logical_and(f == 0, t == 0))
        def _():
            def body(i, carry):
                start_row(0, 0, i)
                return carry
            lax.fori_loop(0, n_fetch, body, 0, unroll=DMA_LOOP_UNROLL)

        @pl.when(f == 0)
        def _():
            wait_all(slot)
            xbf[...] = _unpack_bf16_pairs(xbuf[slot, 0:tm])

        def fetch_ahead():
            nxt = jnp.minimum(t + 1, nt - 1)
            for r in range(per_step):
                start_row(nxt, 1 - slot, f * per_step + r)
    else:
        def fetch_ahead():
            pass

    def ffn_rows(n):
        fetch_ahead()
        if n == 0:
            return
        x = xbf[0:n, :]
        g = jnp.dot(x, wg_ref[...].astype(BF16), preferred_element_type=F32)
        u = jnp.dot(x, wu_ref[...].astype(BF16), preferred_element_type=F32)
        hmid = (g * jax.nn.sigmoid(g) * u).astype(BF16)
        wd = wd_ref[...].astype(BF16)
        cw = o_ref.shape[1] // n_col
        for c in range(n_col):
            y = jnp.dot(hmid, wd[:, c * cw:(c + 1) * cw], preferred_element_type=F32)
            o_ref[0:n, c * cw:(c + 1) * cw] += y

    for k in range(0 if gather else 1, tm // sub_rows + 1):
        covers = jnp.logical_and(n_rows > (k - 1) * sub_rows, n_rows <= k * sub_rows)
        pl.when(covers)(functools.partial(ffn_rows, k * sub_rows))

    if gather:
        @pl.when(jnp.logical_and(t == nt - 1, f == nf - 1))
        def _():
            wait_all(1 - slot)


def gather_rows_per_tile(tm, nf):
    per_step = -(-tm // nf)
    while (per_step * nf) % DMA_LOOP_UNROLL:
        per_step += 1
    return per_step * nf


def swiglu(x, w_gate, w_up, w_down, tile_exp, tile_rows, tile_src=None, src_rows=None, *, gain=None,
           tm=MOE_TILE, tf=FFN_F_TILE, sub_rows=MOE_TILE, name="swiglu"):
    gather = src_rows is not None
    assert gather != (gain is not None)
    m = tile_rows.shape[0] * tm
    d = w_gate.shape[1]
    dff = w_gate.shape[2]
    nf = dff // tf
    nt = m // tm
    assert nf >= 2

    def w_in_map(t, f, exp_ref, rows_ref, src_ref, tok_ref):
        return (exp_ref[t], 0, jnp.where(rows_ref[t] > 0, f, nf - 1))

    def w_out_map(t, f, exp_ref, rows_ref, src_ref, tok_ref):
        return (exp_ref[t], jnp.where(rows_ref[t] > 0, f, nf - 1), 0)

    def tile_map(t, f, exp_ref, rows_ref, src_ref, tok_ref):
        return (t, 0)

    w_specs = [pl.BlockSpec((None, d, tf), w_in_map),
               pl.BlockSpec((None, d, tf), w_in_map),
               pl.BlockSpec((None, tf, d), w_out_map)]
    if gather:
        in_specs = [pl.BlockSpec(memory_space=pl.ANY)] + w_specs
        args = [x, w_gate, w_up, w_down]
        n_fetch = gather_rows_per_tile(tm, nf)
        scratch = [pltpu.VMEM((2, n_fetch, d // 2), jnp.uint32), pltpu.VMEM((tm, d), BF16),
                   pltpu.SemaphoreType.DMA((2, n_fetch))]
    else:
        in_specs = [pl.BlockSpec((tm, d), tile_map, pipeline_mode=pl.Buffered(1)),
                    pl.BlockSpec((1, d), lambda t, f, e, r, s, k: (0, 0))] + w_specs
        args = [x, gain.reshape(1, d), w_gate, w_up, w_down]
        scratch = [pltpu.VMEM((tm, d), BF16)]
        tile_src = jnp.zeros((nt,), jnp.int32)
        src_rows = jnp.zeros((1,), jnp.int32)
    grid_spec = pltpu.PrefetchScalarGridSpec(
        num_scalar_prefetch=4,
        grid=(nt, nf),
        in_specs=in_specs,
        out_specs=pl.BlockSpec((tm, d), tile_map, pipeline_mode=pl.Buffered(1)),
        scratch_shapes=scratch,
    )
    return pl.pallas_call(
        functools.partial(_swiglu_kernel, n_col=4, gather=gather, sub_rows=sub_rows, nf=nf),
        out_shape=jax.ShapeDtypeStruct((m, d), F32),
        grid_spec=grid_spec,
        compiler_params=_params(("arbitrary", "arbitrary"), big=True),
        name=name,
    )(tile_exp, tile_rows, tile_src, src_rows, *args)


COMBINE_TILE = 256


def _combine_kernel(d0_ref, d1_ref, ys_hbm, gate_ref, h_ref, hout_ref, buf, sems):
    i = pl.program_id(0)
    n = pl.num_programs(0)
    tt = h_ref.shape[0]
    slot = i % 2
    dests = (d0_ref, d1_ref)

    def start_tile(tile, sl):
        def body(r, c):
            for k, d_ref in enumerate(dests):
                _row_copy(ys_hbm, d_ref[tile * tt + r], buf.at[sl, k], r, sems.at[sl, k, r]).start()
            return c
        lax.fori_loop(0, tt, body, 0, unroll=DMA_LOOP_UNROLL)

    @pl.when(i == 0)
    def _():
        start_tile(0, 0)

    @pl.when(i + 1 < n)
    def _():
        start_tile(i + 1, 1 - slot)

    def wait_body(r, c):
        for k in range(TOP_K):
            _row_copy(ys_hbm, 0, buf.at[slot, k], r, sems.at[slot, k, r]).wait()
        return c
    lax.fori_loop(0, tt, wait_body, 0, unroll=DMA_LOOP_UNROLL)

    gates = gate_ref[...]
    hout_ref[...] = h_ref[...] + (buf[slot, 0] * gates[:, 0:1] + buf[slot, 1] * gates[:, 1:2])


def moe_combine(h2, ys, dest2, gates, tt=COMBINE_TILE):
    n, d = h2.shape
    grid_spec = pltpu.PrefetchScalarGridSpec(
        num_scalar_prefetch=2,
        grid=(n // tt,),
        in_specs=[pl.BlockSpec(memory_space=pl.ANY),
                  pl.BlockSpec((tt, TOP_K), lambda i, a, b: (i, 0)),
                  pl.BlockSpec((tt, d), lambda i, a, b: (i, 0))],
        out_specs=pl.BlockSpec((tt, d), lambda i, a, b: (i, 0)),
        scratch_shapes=[pltpu.VMEM((2, TOP_K, tt, d), F32), pltpu.SemaphoreType.DMA((2, TOP_K, tt))],
    )
    return pl.pallas_call(
        _combine_kernel,
        out_shape=jax.ShapeDtypeStruct((n, d), F32),
        grid_spec=grid_spec,
        compiler_params=_params(("arbitrary",), big=True),
        name="moe_combine",
    )(dest2[:, 0], dest2[:, 1], ys, gates, h2)


def _t5_bucket(rel):
    half = T5_BUCKETS // 2
    max_exact = half // 2
    n = jnp.abs(rel)
    log_ratio = jnp.log(jnp.maximum(n, 1).astype(F32) / max_exact) / math.log(T5_MAX_DIST / max_exact)
    large = jnp.minimum(max_exact + (log_ratio * (half - max_exact)).astype(jnp.int32), half - 1)
    return jnp.where(rel > 0, half, 0) + jnp.where(n < max_exact, n, large)


def _window_bias_table(t5_bias):
    i = jnp.arange(A_BLOCK)[:, None]
    j = jnp.arange(3 * A_BLOCK)[None, :]
    rel = j - A_BLOCK - i
    onehot = (_t5_bucket(rel)[None] == jnp.arange(T5_BUCKETS)[:, None, None]).astype(F32)
    bias = jnp.einsum('bh,bij->hij', t5_bias.astype(F32), onehot, precision=lax.Precision.HIGHEST)
    return jnp.where((jnp.abs(rel) <= A_BLOCK)[None], bias, NEG_INF)


def _window_kernel(sink_ref, q_ref, kv_ref, bias_ref, o_ref, *, nb):
    n = pl.program_id(1)
    scale = HEAD_DIM ** -0.5
    starts = (jnp.maximum(n - 1, 0), n, jnp.minimum(n + 1, nb - 1))
    kv = [kv_ref[pl.ds(pl.multiple_of(s * A_BLOCK, A_BLOCK), A_BLOCK), :] for s in starts]
    col = lax.broadcasted_iota(jnp.int32, (A_BLOCK, 3 * A_BLOCK), 1)
    edge_ok = jnp.logical_and(jnp.logical_or(n > 0, col >= A_BLOCK),
                              jnp.logical_or(n < nb - 1, col < 2 * A_BLOCK))
    kv = jnp.concatenate(kv, axis=0)
    q = q_ref[...]
    lo = lax.broadcasted_iota(jnp.int32, (1, 2 * HEAD_DIM), 1) < HEAD_DIM
    sel = (jnp.where(lo, scale, 0.0).astype(BF16), jnp.where(lo, 0.0, scale).astype(BF16))
    pairs_per_kv = A_GROUP // 2
    kk, vv = [], []
    for kh in range(A_KV_HEADS):
        k1 = kv[:, kh * HEAD_DIM:(kh + 1) * HEAD_DIM]
        v1 = kv[:, A_KV_WIDTH + kh * HEAD_DIM:A_KV_WIDTH + (kh + 1) * HEAD_DIM]
        kk.append(jnp.concatenate([k1, k1], axis=1))
        vv.append(jnp.concatenate([v1, v1], axis=1))
    scores = []
    for pr in range(A_Q_HEADS // 2):
        q2 = q[:, pr * 2 * HEAD_DIM:(pr + 1) * 2 * HEAD_DIM]
        qq = jnp.concatenate([q2 * sel[0], q2 * sel[1]], axis=0)
        s2 = lax.dot_general(qq, kk[pr // pairs_per_kv], (((1,), (1,)), ((), ())),
                             preferred_element_type=F32)
        for i in range(2):
            scores.append(jnp.where(edge_ok, s2[i * A_BLOCK:(i + 1) * A_BLOCK] + bias_ref[2 * pr + i],
                                    NEG_INF))
    probs, dens = [], []
    for h, s in enumerate(scores):
        sink = sink_ref[h]
        mx = jnp.maximum(jnp.max(s, axis=-1, keepdims=True), sink)
        p = jnp.exp(s - mx)
        dens.append(jnp.sum(p, axis=-1, keepdims=True) + jnp.exp(sink - mx))
        probs.append(p.astype(BF16))
    for pr in range(A_Q_HEADS // 2):
        v2 = vv[pr // pairs_per_kv]
        oa = jnp.dot(probs[2 * pr], v2, preferred_element_type=F32) / dens[2 * pr]
        ob = jnp.dot(probs[2 * pr + 1], v2, preferred_element_type=F32) / dens[2 * pr + 1]
        o_ref[:, pr * 2 * HEAD_DIM:(pr + 1) * 2 * HEAD_DIM] = jnp.where(lo, oa, ob).astype(o_ref.dtype)


def window_attention(proj_a, t5_bias, sink):
    b, s, _ = proj_a.shape
    nb = s // A_BLOCK
    kvw = 2 * A_KV_WIDTH
    grid_spec = pltpu.PrefetchScalarGridSpec(
        num_scalar_prefetch=0,
        grid=(b, nb),
        in_specs=[pl.BlockSpec(memory_space=pltpu.SMEM),
                  pl.BlockSpec((None, A_BLOCK, A_WIDTH), lambda bi, n: (bi, n, 0)),
                  pl.BlockSpec((None, s, kvw), lambda bi, n: (bi, 0, A_WIDTH // kvw)),
                  pl.BlockSpec((A_Q_HEADS, A_BLOCK, 3 * A_BLOCK), lambda bi, n: (0, 0, 0))],
        out_specs=pl.BlockSpec((None, A_BLOCK, A_WIDTH), lambda bi, n: (bi, n, 0)),
    )
    return pl.pallas_call(
        functools.partial(_window_kernel, nb=nb),
        out_shape=jax.ShapeDtypeStruct((b, s, A_WIDTH), BF16),
        grid_spec=grid_spec,
        compiler_params=_params(("parallel", "arbitrary")),
        name="window_attention",
    )(sink.astype(F32), proj_a, proj_a, _window_bias_table(t5_bias))


NA_HEAD_GROUP = 16


def _na_bias_table(rpb, rows):
    kh = min(NA_WIN_H, rows)
    kw = NA_WIN_W
    c = jnp.arange(GRID_W)
    cs = jnp.clip(c - kw // 2, 0, GRID_W - kw)
    col_ok = (c[None] >= cs[:, None]) & (c[None] < cs[:, None] + kw)
    col_off = jnp.clip(c[None] - c[:, None], -(kw - 1), kw - 1) + kw - 1
    onehot = (col_off[None] == jnp.arange(2 * kw - 1)[:, None, None]).astype(F32)
    by_col = jnp.einsum('hrc,cqk->hqrk', rpb.astype(F32), onehot, precision=lax.Precision.HIGHEST)
    by_col = jnp.where(col_ok[None, :, None, :], by_col, NEG_INF)
    tabs = [by_col[:, :, NA_WIN_H - 1 - d:NA_WIN_H - 1 - d + kh] for d in range(kh)]
    return jnp.stack(tabs, axis=0).reshape(kh, rpb.shape[0], GRID_W, kh * GRID_W)


def _na_kernel(q_ref, k_ref, v_ref, bias_ref, o_ref, *, rows):
    r = pl.program_id(2)
    kh = min(NA_WIN_H, rows)
    scale = HEAD_DIM ** -0.5
    start = pl.multiple_of(jnp.clip(r - kh // 2, 0, rows - kh) * GRID_W, GRID_W)
    k = k_ref[pl.ds(start, kh * GRID_W), :]
    v = v_ref[pl.ds(start, kh * GRID_W), :]
    q = q_ref[...]
    lo = lax.broadcasted_iota(jnp.int32, (1, 2 * HEAD_DIM), 1) < HEAD_DIM
    sel = (jnp.where(lo, scale, 0.0).astype(BF16), jnp.where(lo, 0.0, scale).astype(BF16))
    n_pairs = NA_HEAD_GROUP // 2
    scores = []
    for pr in range(n_pairs):
        sl = slice(pr * 2 * HEAD_DIM, (pr + 1) * 2 * HEAD_DIM)
        q2 = q[:, sl]
        qq = jnp.concatenate([q2 * sel[0], q2 * sel[1]], axis=0)
        s2 = lax.dot_general(qq, k[:, sl], (((1,), (1,)), ((), ())), preferred_element_type=F32)
        for i in range(2):
            scores.append(s2[i * GRID_W:(i + 1) * GRID_W] + bias_ref[2 * pr + i])
    probs, dens = [], []
    for s in scores:
        mx = jnp.max(s, axis=-1, keepdims=True)
        p = jnp.exp(s - mx)
        dens.append(jnp.sum(p, axis=-1, keepdims=True))
        probs.append(p.astype(BF16))
    for pr in range(n_pairs):
        sl = slice(pr * 2 * HEAD_DIM, (pr + 1) * 2 * HEAD_DIM)
        oa = jnp.dot(probs[2 * pr], v[:, sl], preferred_element_type=F32) / dens[2 * pr]
        ob = jnp.dot(probs[2 * pr + 1], v[:, sl], preferred_element_type=F32) / dens[2 * pr + 1]
        o_ref[:, sl] = jnp.where(lo, oa, ob).astype(o_ref.dtype)


def neighbourhood_attention(proj_n, rpb):
    b, s, _ = proj_n.shape
    rows = s // GRID_W
    kh = min(NA_WIN_H, rows)
    gw = NA_HEAD_GROUP * HEAD_DIM
    ng = B_WIDTH // gw

    def bias_map(bi, g, r):
        return (r - jnp.clip(r - kh // 2, 0, rows - kh), g, 0, 0)

    return pl.pallas_call(
        functools.partial(_na_kernel, rows=rows),
        out_shape=jax.ShapeDtypeStruct((b, s, B_WIDTH), BF16),
        grid=(b, ng, rows),
        in_specs=[pl.BlockSpec((None, GRID_W, gw), lambda bi, g, r: (bi, r, g)),
                  pl.BlockSpec((None, s, gw), lambda bi, g, r: (bi, 0, ng + g)),
                  pl.BlockSpec((None, s, gw), lambda bi, g, r: (bi, 0, 2 * ng + g)),
                  pl.BlockSpec((None, NA_HEAD_GROUP, GRID_W, kh * GRID_W), bias_map)],
        out_specs=pl.BlockSpec((None, GRID_W, gw), lambda bi, g, r: (bi, r, g)),
        compiler_params=_params(("parallel", "parallel", "arbitrary"), big=True),
        name="neighbourhood_attention",
    )(proj_n, proj_n, proj_n, _na_bias_table(rpb, rows))


CONV_HALO = 16
CONV_COL_CHUNK = 512


def _hyena_in_kernel(prev_ref, cur_ref, next_ref, w_ref, b_ref, cw_ref, cb_ref, o_ref, wbf_ref, *,
                     tiles_per_seq):
    i = pl.program_id(1)

    @pl.when(i == 0)
    def _():
        wbf_ref[...] = w_ref[...].astype(BF16)

    tm = cur_ref.shape[0]
    rows = tm + 2 * CONV_HALO
    a = jnp.concatenate([prev_ref[...], cur_ref[...], next_ref[...]], axis=0)
    pos = i % tiles_per_seq
    row = lax.broadcasted_iota(jnp.int32, (rows, 1), 0)
    kill_up = jnp.logical_and(pos == 0, row == CONV_HALO)
    kill_dn = jnp.logical_and(pos == tiles_per_seq - 1, row == CONV_HALO + tm - 1)
    for c in range(o_ref.shape[1] // CONV_COL_CHUNK):
        cs = slice(c * CONV_COL_CHUNK, (c + 1) * CONV_COL_CHUNK)
        p = jnp.dot(a, wbf_ref[:, cs], preferred_element_type=F32) + b_ref[:, cs]
        up = jnp.where(kill_up, 0.0, pltpu.roll(p, 1, 0))
        dn = jnp.where(kill_dn, 0.0, pltpu.roll(p, rows - 1, 0))
        y = up * cw_ref[0:1, cs] + p * cw_ref[1:2, cs] + dn * cw_ref[2:3, cs] + cb_ref[:, cs]
        o_ref[:, cs] = y[CONV_HALO:CONV_HALO + tm]


def hyena_in_conv(hn, w_in, b_in, conv_w, conv_b, seq, tm=1024, tn=1024):
    m, k = hn.shape
    n = w_in.shape[1]
    hb = tm // CONV_HALO
    last_halo = m // CONV_HALO - 1
    return pl.pallas_call(
        functools.partial(_hyena_in_kernel, tiles_per_seq=seq // tm),
        out_shape=jax.ShapeDtypeStruct((m, n), F32),
        grid=(n // tn, m // tm),
        in_specs=[pl.BlockSpec((CONV_HALO, k), lambda j, i: (jnp.maximum(i * hb - 1, 0), 0)),
                  pl.BlockSpec((tm, k), lambda j, i: (i, 0)),
                  pl.BlockSpec((CONV_HALO, k), lambda j, i: (jnp.minimum((i + 1) * hb, last_halo), 0)),
                  pl.BlockSpec((k, tn), lambda j, i: (0, j)),
                  pl.BlockSpec((1, tn), lambda j, i: (0, j)),
                  pl.BlockSpec((3, tn), lambda j, i: (0, j)),
                  pl.BlockSpec((1, tn), lambda j, i: (0, j))],
        out_specs=pl.BlockSpec((tm, tn), lambda j, i: (i, j)),
        scratch_shapes=[pltpu.VMEM((k, tn), BF16)],
        compiler_params=_params(("parallel", "arbitrary"), big=True),
        name="hyena_in_conv",
    )(hn, hn, hn, w_in, b_in.reshape(1, n), conv_w, conv_b.reshape(1, n))


def _filter_stage1_kernel(z_ref, t_ref, fw1_ref, fb1_ref, fw2_ref, fb2_ref, fr_ref, w3f0_ref, w3f1_ref,
                          w3b0_ref, w3b1_ref, delta_ref, f1_ref, y_ref, hid_ref, *, seq):
    j = pl.program_id(0)

    @pl.when(pl.program_id(1) == 0)
    def _():
        hi = lax.Precision.HIGHEST
        fr = fr_ref[...]
        hid = jnp.sin(fr * (jnp.dot(z_ref[...], fw1_ref[...], preferred_element_type=F32, precision=hi)
                            + fb1_ref[...]))
        hid = jnp.sin(fr * (jnp.dot(hid, fw2_ref[...], preferred_element_type=F32, precision=hi)
                            + fb2_ref[...]))
        hid_ref[...] = hid.astype(BF16)

    hid_bf = hid_ref[...]
    r = lax.broadcasted_iota(jnp.int32, (hid_bf.shape[0], 1), 0)
    u = j * FFT_CHUNK + r // FFT_N2 + FFT_N1 * (r % FFT_N2)
    decay = jnp.where(u == seq, 0.0, jnp.exp(-t_ref[...] * delta_ref[...]))
    for o, (wf_ref, wb_ref) in enumerate(((w3f0_ref, w3b0_ref), (w3f1_ref, w3b1_ref))):
        fwd = jnp.dot(hid_bf, wf_ref[...].astype(BF16), preferred_element_type=F32)
        bwd = jnp.dot(hid_bf, wb_ref[...].astype(BF16), preferred_element_type=F32)
        taps = (jnp.where(u > seq, bwd, jnp.where(u == 0, fwd + bwd, fwd)) * decay).astype(BF16)
        y_ref[o] = _chunk_to_sublanes_bf16([jnp.dot(f1_ref[jj], taps[jj * FFT_N2:(jj + 1) * FFT_N2],
                                           preferred_element_type=F32) for jj in range(FFT_CHUNK)])


def hyena_filter_stage1(seq, fw1, fb1, fw2, fb2, freq, fw3, width, consts, cb=FFT_CB):
    _, f1_real, _, _, _ = consts
    n = 2 * seq
    n1, n2, ch = FFT_N1, FFT_N2, FFT_CHUNK
    assert fw3.shape[1] == 4 * width
    r = jnp.arange(n)
    u = (r // (ch * n2)) * ch + (r // n2) % ch + n1 * (r % n2)
    pos = jnp.where(u <= seq, u, n - u)
    pos = jnp.where(u == seq, 0, pos)
    bands = (HYENA_EMB - 1) // 2
    f = jnp.linspace(1e-4, bands - 1, bands, dtype=F32)[None]
    t = (pos.astype(F32) / (seq - 1))[:, None]
    w = ((2.0 * math.pi / seq) * pos.astype(F32))[:, None]
    z = jnp.concatenate([t, jnp.cos(f * w), -jnp.sin(f * w)], axis=-1)
    emb_pad = 40
    z = jnp.pad(z, ((0, 0), (0, emb_pad - HYENA_EMB)))
    fw1p = jnp.pad(fw1.astype(F32), ((0, emb_pad - HYENA_EMB), (0, 0)))
    fh = fw1.shape[1]
    max_decay = math.log(HYENA_DECAY_TARGET) / HYENA_FAST_PCT
    min_decay = math.log(HYENA_DECAY_TARGET) / HYENA_SLOW_PCT
    deltas = jnp.abs(jnp.linspace(min_decay, max_decay, width, dtype=F32))[None]
    ncb = width // cb
    rows = ch * n2

    def small(shape):
        return pl.BlockSpec(shape, lambda j, k: (0, 0))

    def w3_spec(blk):
        return pl.BlockSpec((fh, cb), lambda j, k: (0, blk * ncb + k))

    return pl.pallas_call(
        functools.partial(_filter_stage1_kernel, seq=seq),
        out_shape=jax.ShapeDtypeStruct((2, 2 * n2, n1, width), BF16),
        grid=(n1 // ch, ncb),
        in_specs=[pl.BlockSpec((rows, emb_pad), lambda j, k: (j, 0)),
                  pl.BlockSpec((rows, 1), lambda j, k: (j, 0)),
                  small((emb_pad, fh)), small((1, fh)), small((fh, fh)), small((1, fh)), small((1, fh)),
                  w3_spec(0), w3_spec(1), w3_spec(2), w3_spec(3),
                  pl.BlockSpec((1, cb), lambda j, k: (0, k)),
                  pl.BlockSpec((ch, 2 * n2, n2), lambda j, k: (j, 0, 0))],
        out_specs=pl.BlockSpec((2, 2 * n2, ch, cb), lambda j, k: (0, 0, j, k)),
        scratch_shapes=[pltpu.VMEM((rows, fh), BF16)],
        compiler_params=_params(("parallel", "arbitrary"), big=True),
        name="hyena_filter_stage1",
    )(z, t, fw1p, fb1.reshape(1, fh).astype(F32), fw2.astype(F32), fb2.reshape(1, fh).astype(F32),
      freq.reshape(1, fh).astype(F32), fw3, fw3, fw3, fw3, deltas, f1_real)


def _dft_constants():
    n1, n2 = FFT_N1, FFT_N2
    n = n1 * n2
    a1 = np.arange(n1)
    a2 = np.arange(n2)
    half = n2 // 2

    def cplx_block(w):
        return np.block([[w.real, -w.imag], [w.imag, w.real]])

    ang = (a2[None, :, None] * a2[None, None, :] / n2) + (a1[:, None, None] * a2[None, :, None] / n)
    w1 = np.exp(-2j * np.pi * ang)
    f1_pad = np.stack([cplx_block(w1[i][:, :half]) for i in range(n1)])
    f1_real = np.stack([np.concatenate([w1[i].real, w1[i].imag], axis=0) for i in range(n1)])
    w2 = np.exp(-2j * np.pi * (a1[:, None] * a1[None, :]) / n1)
    f2 = cplx_block(w2)
    ang = (a1[None, :, None] * a1[None, None, :] / n1) + (a2[:, None, None] * a1[None, :, None] / n)
    g2 = np.stack([cplx_block(m_) for m_ in np.exp(2j * np.pi * ang)])
    wg1 = np.exp(2j * np.pi * (a2[:half, None] * a2[None, :]) / n2) / n
    g1 = cplx_block(wg1)
    to = lambda x: jnp.asarray(x.astype(np.float32)).astype(BF16)
    return to(f1_pad), to(f1_real), to(f2), to(g2), to(g1)


def _chunk_to_sublanes_bf16(per_chunk):
    return pltpu.einshape("jkc->kjc", jnp.stack(per_chunk, axis=0)).astype(BF16)


def _fft_stage1_block(x_re, x_im, f_ref, y_ref):
    xr = pltpu.einshape("mjc->jmc", x_re)
    xi = pltpu.einshape("mjc->jmc", x_im)
    y_ref[...] = _chunk_to_sublanes_bf16([
        jnp.dot(f_ref[j], jnp.concatenate([xr[j], xi[j]], axis=0).astype(BF16), preferred_element_type=F32)
        for j in range(FFT_CHUNK)])


def _fft_stage1_kernel(x_ref, f_ref, y_ref):
    _fft_stage1_block(x_ref[0], x_ref[1], f_ref, y_ref)


def _fft_stage2_kernel(yr_ref, yi_ref, f2_ref, h_ref):
    for j in range(FFT_CHUNK):
        yc = jnp.concatenate([yr_ref[j], yi_ref[j]], axis=0)
        h_ref[j] = jnp.dot(f2_ref[...], yc, preferred_element_type=F32).astype(BF16)


def _fft_mid_kernel(yr_ref, yi_ref, f2_ref, h_ref, g2_ref, q_ref):
    n1 = FFT_N1
    spec = [jnp.dot(f2_ref[...], jnp.concatenate([yr_ref[j], yi_ref[j]], axis=0),
                    preferred_element_type=F32) for j in range(FFT_CHUNK)]
    prod = []
    for j, z in enumerate(spec):
        hf = h_ref[j].astype(F32)
        zr, zi, hr, hi = z[:n1], z[n1:], hf[:n1], hf[n1:]
        prod.append(jnp.concatenate([zr * hr - zi * hi, zr * hi + zi * hr], axis=0).astype(BF16))
    q_ref[...] = _chunk_to_sublanes_bf16([jnp.dot(g2_ref[j], pc, preferred_element_type=F32)
                                 for j, pc in enumerate(prod)])


def _fft_last_kernel(qr_ref, qi_ref, g1_ref, gate_ref, zf_ref, fb_ref, *rest, feeds_next):
    f1_ref, o_ref, y_ref = rest if feeds_next else (None, rest[0], None)
    half = FFT_N2 // 2
    conv = []
    for j in range(FFT_CHUNK):
        qc = jnp.concatenate([qr_ref[j], qi_ref[j]], axis=0)
        conv.append(jnp.dot(g1_ref[...], qc, preferred_element_type=F32))
    conv = jnp.stack(conv, axis=0)
    fb = fb_ref[...]
    out = []
    for bi in range(2):
        cb = pltpu.einshape("jmc->mjc", conv[:, bi * half:(bi + 1) * half])
        out.append(gate_ref[bi] * (cb + zf_ref[bi] * fb))
        o_ref[bi] = out[bi]
    if feeds_next:
        _fft_stage1_block(out[0], out[1], f1_ref, y_ref)


def hyena_filter_spectrum(y, consts, cb=FFT_CB):
    _, _, f2, _, _ = consts
    no = y.shape[0]
    c = y.shape[-1]
    n1, n2, ch = FFT_N1, FFT_N2, FFT_CHUNK
    nk = n2 // ch
    return pl.pallas_call(
        _fft_stage2_kernel,
        out_shape=jax.ShapeDtypeStruct((no, n2, 2 * n1, c), BF16),
        grid=(no, nk, c // cb),
        in_specs=[pl.BlockSpec((None, ch, n1, cb), lambda o, j, k: (o, j, 0, k)),
                  pl.BlockSpec((None, ch, n1, cb), lambda o, j, k: (o, nk + j, 0, k)),
                  pl.BlockSpec((2 * n1, 2 * n1), lambda o, j, k: (0, 0))],
        out_specs=pl.BlockSpec((None, ch, 2 * n1, cb), lambda o, j, k: (o, j, 0, k)),
        compiler_params=_params(("parallel", "parallel", "parallel"), big=True),
        name="fft_filter_stage2",
    )(y, y, f2)


def hyena_long_conv_gate(zsrc, z_col, gate_src, gate_col, hf, order, fbias, consts, y=None,
                         feeds_next=False, cb=FFT_CB):
    f1_pad, _, f2, g2, g1 = consts
    n1, n2, ch = FFT_N1, FFT_N2, FFT_CHUNK
    c = hf.shape[-1]
    ncb = c // cb
    half = n2 // 2
    y_shape = jax.ShapeDtypeStruct((2 * n2, n1, c), BF16)
    f1_spec = pl.BlockSpec((ch, 2 * n2, n2), lambda j, k: (j, 0, 0))
    y_spec = pl.BlockSpec((2 * n2, ch, cb), lambda j, k: (0, j, k))
    if y is None:
        y = pl.pallas_call(
            _fft_stage1_kernel,
            out_shape=y_shape,
            grid=(n1 // ch, ncb),
            in_specs=[pl.BlockSpec((2, half, ch, cb), lambda j, k: (0, 0, j, z_col * ncb + k)), f1_spec],
            out_specs=y_spec,
            compiler_params=_params(("parallel", "parallel"), big=True),
            name="fft_stage1",
        )(zsrc, f1_pad)
    nk = n2 // ch
    q = pl.pallas_call(
        _fft_mid_kernel,
        out_shape=jax.ShapeDtypeStruct((2 * n1, n2, c), BF16),
        grid=(nk, ncb),
        in_specs=[pl.BlockSpec((ch, n1, cb), lambda j, k: (j, 0, k)),
                  pl.BlockSpec((ch, n1, cb), lambda j, k: (nk + j, 0, k)),
                  pl.BlockSpec((2 * n1, 2 * n1), lambda j, k: (0, 0)),
                  pl.BlockSpec((None, ch, 2 * n1, cb), lambda j, k: (order, j, 0, k)),
                  pl.BlockSpec((ch, 2 * n1, 2 * n1), lambda j, k: (j, 0, 0))],
        out_specs=pl.BlockSpec((2 * n1, ch, cb), lambda j, k: (0, j, k)),
        compiler_params=_params(("parallel", "parallel"), big=True),
        name="fft_mid",
    )(y, y, f2, hf, g2)
    nj = n1 // ch
    in_specs = [pl.BlockSpec((ch, n2, cb), lambda j, k: (j, 0, k)),
                pl.BlockSpec((ch, n2, cb), lambda j, k: (nj + j, 0, k)),
                pl.BlockSpec((2 * half, 2 * n2), lambda j, k: (0, 0)),
                pl.BlockSpec((2, half, ch, cb), lambda j, k: (0, 0, j, gate_col * ncb + k)),
                pl.BlockSpec((2, half, ch, cb), lambda j, k: (0, 0, j, z_col * ncb + k)),
                pl.BlockSpec((None, 1, cb), lambda j, k: (order, 0, k))]
    args = [q, q, g1, gate_src, zsrc, fbias.reshape(fbias.shape[0], 1, c)]
    out_shape = jax.ShapeDtypeStruct((2, half, n1, c), F32)
    out_spec = pl.BlockSpec((2, half, ch, cb), lambda j, k: (0, 0, j, k))
    if feeds_next:
        in_specs.append(f1_spec)
        args.append(f1_pad)
        out_shape, out_spec = (out_shape, y_shape), (out_spec, y_spec)
    res = pl.pallas_call(
        functools.partial(_fft_last_kernel, feeds_next=feeds_next),
        out_shape=out_shape,
        grid=(nj, ncb),
        in_specs=in_specs,
        out_specs=out_spec,
        compiler_params=_params(("parallel", "parallel"), big=True),
        name="fft_last",
    )(*args)
    return res if feeds_next else (res, None)


def hyena_mixer(hn, h_res, w_in, b_in, conv_w, conv_b, fw1, fb1, fw2, fb2, freq, fw3, fbias, w_out,
                batch, seq):
    width = w_out.shape[0]
    assert batch == 2 and 2 * seq == FFT_N1 * FFT_N2
    sc = hyena_in_conv(hn, w_in, b_in, conv_w, conv_b, seq)
    consts = _dft_constants()
    hf = hyena_filter_spectrum(hyena_filter_stage1(seq, fw1, fb1, fw2, fb2, freq, fw3, width, consts),
                               consts)
    sc4 = sc.reshape(batch, FFT_N2 // 2, FFT_N1, 3 * width)
    zf1, y1 = hyena_long_conv_gate(sc4, 2, sc4, 0, hf, 0, fbias, consts, feeds_next=True)
    zf2, _ = hyena_long_conv_gate(zf1, 0, sc4, 1, hf, 1, fbias, consts, y=y1)
    return matmul([zf2.reshape(batch * seq, width)], w_out, res=h_res, tm=512, tn=1024,
                  name="hyena_out")


def moe_swiglu(hn2_packed, logits, wg, wu, wd):
    n = hn2_packed.shape[0]
    top_v, top_i = lax.top_k(logits, TOP_K)
    gates = jax.nn.softmax(top_v, axis=-1)
    e_flat = top_i.reshape(-1).astype(jnp.int32)
    nk = n * TOP_K
    onehot = (e_flat[:, None] == jnp.arange(N_EXPERTS, dtype=jnp.int32)[None]).astype(jnp.int32)
    csum = jnp.cumsum(onehot, axis=0)
    rank = jnp.take_along_axis(csum, e_flat[:, None], axis=1)[:, 0] - 1
    counts = csum[-1]
    padded = ((counts + MOE_TILE - 1) // MOE_TILE) * MOE_TILE
    pad_end = jnp.cumsum(padded)
    pad_start = pad_end - padded
    dest = pad_start[e_flat] + rank
    p_rows = nk + N_EXPERTS * MOE_TILE
    order = jnp.argsort(e_flat, stable=True).astype(jnp.int32)
    nf = wg.shape[2] // MOE_F_TILE
    sorted_tok = jnp.pad(order // TOP_K, (0, gather_rows_per_tile(MOE_TILE, nf)))
    start = jnp.cumsum(counts) - counts
    nt = p_rows // MOE_TILE
    tile_start = jnp.arange(nt, dtype=jnp.int32) * MOE_TILE
    tile_used = tile_start < pad_end[-1]
    tile_exp = jnp.minimum(jnp.searchsorted(pad_end, tile_start, side='right'), N_EXPERTS - 1).astype(jnp.int32)
    tile_rows = jnp.clip(counts[tile_exp] - (tile_start - pad_start[tile_exp]), 0, MOE_TILE)
    tile_rows = jnp.where(tile_used, tile_rows, 0).astype(jnp.int32)
    last_exp = tile_exp[jnp.maximum(jnp.sum(tile_used.astype(jnp.int32)) - 1, 0)]
    tile_src = jnp.where(tile_used, start[tile_exp] + (tile_start - pad_start[tile_exp]), 0)
    tile_exp = jnp.where(tile_used, tile_exp, last_exp)
    ys = swiglu(hn2_packed, wg, wu, wd, tile_exp, tile_rows, tile_src.astype(jnp.int32), sorted_tok,
                tf=MOE_F_TILE, sub_rows=FFN_SUB_ROWS, name="swiglu_experts")
    return ys, dest.reshape(n, TOP_K), gates


def kernel(x, p, ln_mix, ln_ffn, ln_ple, final_norm, t5_bias, w_attn_in, w_attn_out, attn_sink, na_rpb, w_ffn_gate, w_ffn_up, w_ffn_down, w_hy_in, b_hy_in, w_hy_conv, b_hy_conv, w_hy_f1, b_hy_f1, w_hy_f2, b_hy_f2, hy_freq, w_hy_f3, hy_bias, w_hy_out, w_router, w_exp_gate, w_exp_up, w_exp_down, w_ple_proj, w_ple_gate):
    batch, seq, d = x.shape
    n = batch * seq
    depth = ln_mix.shape[0]
    h = x.reshape(n, d)
    for i in range(depth):
        li = i // 2
        hn = rmsnorm(h, ln_mix[i], BF16)
        if i % 2 == 0:
            na_off = A_WIDTH + 2 * A_KV_WIDTH
            w_in = w_attn_in[li]
            proj_a = matmul([hn], w_in[:, :na_off], out_dtype=BF16, tn=na_off, name="attn_in_a")
            proj_n = matmul([hn], w_in[:, na_off:], out_dtype=BF16, tn=3 * B_WIDTH // 2, name="attn_in_n")
            oa = window_attention(proj_a.reshape(batch, seq, -1), t5_bias, attn_sink[li])
            ob = neighbourhood_attention(proj_n.reshape(batch, seq, -1), na_rpb[li])
            h = matmul([oa.reshape(n, A_WIDTH), ob.reshape(n, B_WIDTH)], w_attn_out[li], res=h,
                       name="attn_out")
            nt = n // MOE_TILE
            h = swiglu(h, w_ffn_gate, w_ffn_up, w_ffn_down,
                       jnp.full((nt,), li, jnp.int32), jnp.full((nt,), MOE_TILE, jnp.int32),
                       gain=ln_ffn[i], name="swiglu_dense")
        else:
            h = hyena_mixer(hn, h, w_hy_in[li], b_hy_in[li], w_hy_conv[li], b_hy_conv[li],
                            w_hy_f1[li], b_hy_f1[li], w_hy_f2[li], b_hy_f2[li], hy_freq[li],
                            w_hy_f3[li], hy_bias[li], w_hy_out[li], batch, seq)
            wr_pad = jnp.pad(w_router[li].astype(F32), ((0, 0), (0, 128 - N_EXPERTS)))
            hn2_packed, logits = rmsnorm_router(h, ln_ffn[i], wr_pad)
            ys, dest2, gates = moe_swiglu(hn2_packed, logits[:, :N_EXPERTS], w_exp_gate[li],
                                          w_exp_up[li], w_exp_down[li])
            h = moe_combine(h, ys, dest2, gates)
        h = ple(h, p.reshape(depth, n, -1), ln_ple[i], w_ple_gate, w_ple_proj, i)
    return rmsnorm(h, final_norm, F32).reshape(batch, seq, d)
```

```python
import functools
import math

import jax
import jax.numpy as jnp
import numpy as np
from jax import lax
from jax.experimental import pallas as pl
from jax.experimental.pallas import tpu as pltpu

F32 = jnp.float32
BF16 = jnp.bfloat16
NEG_INF = -1e30
RMS_EPS = 1e-6

V7X_VMEM_LIMIT_BYTES = 56 * 1024 * 1024

HEAD_DIM = 64
A_Q_HEADS = 16
A_KV_HEADS = 2
A_GROUP = A_Q_HEADS // A_KV_HEADS
A_BLOCK = 128
T5_BUCKETS = 32
T5_MAX_DIST = 128
B_HEADS = 16
GRID_W = 64
NA_WIN_H = 8
NA_WIN_W = 16
A_WIDTH = A_Q_HEADS * HEAD_DIM
A_KV_WIDTH = A_KV_HEADS * HEAD_DIM
B_WIDTH = B_HEADS * HEAD_DIM
N_EXPERTS = 8
TOP_K = 2
HYENA_EMB = 33
HYENA_DECAY_TARGET = 1e-2
HYENA_FAST_PCT = 0.3
HYENA_SLOW_PCT = 1.5

FFT_N1 = 64
FFT_N2 = 128
FFT_CHUNK = 8
FFT_CB = 1024

MOE_TILE = 1024
FFN_F_TILE = 512
MOE_F_TILE = 512
FFN_SUB_ROWS = 256
DMA_LOOP_UNROLL = 8


def _params(semantics, big=False):
    return pltpu.CompilerParams(
        dimension_semantics=semantics,
        vmem_limit_bytes=V7X_VMEM_LIMIT_BYTES if big else None)


def _rmsnorm_kernel(x_ref, g_ref, o_ref):
    x = x_ref[...]
    y = x * lax.rsqrt(jnp.mean(x * x, axis=-1, keepdims=True) + RMS_EPS)
    o_ref[...] = (y * g_ref[...]).astype(o_ref.dtype)


def rmsnorm(x2, g, out_dtype, tm=1024):
    n, d = x2.shape
    return pl.pallas_call(
        _rmsnorm_kernel,
        out_shape=jax.ShapeDtypeStruct((n, d), out_dtype),
        grid=(n // tm,),
        in_specs=[pl.BlockSpec((tm, d), lambda i: (i, 0)),
                  pl.BlockSpec((1, d), lambda i: (0, 0))],
        out_specs=pl.BlockSpec((tm, d), lambda i: (i, 0)),
        compiler_params=_params(("parallel",), big=True),
        name="rmsnorm",
    )(x2, g.reshape(1, d))


def _pack_bf16_pairs(y):
    w = y.shape[1] // 2
    bits = pltpu.bitcast(y.astype(BF16).astype(F32), jnp.uint32)
    return (bits[:, :w] >> 16) | (bits[:, w:] & jnp.uint32(0xFFFF0000))


def _unpack_pairs_f32(p):
    lo = pltpu.bitcast(p << 16, F32)
    hi = pltpu.bitcast(p & jnp.uint32(0xFFFF0000), F32)
    return jnp.concatenate([lo, hi], axis=1)


def _unpack_bf16_pairs(p):
    return _unpack_pairs_f32(p).astype(BF16)


def _rmsnorm_router_kernel(x_ref, g_ref, wr_ref, o_ref, l_ref):
    x = x_ref[...]
    y = x * lax.rsqrt(jnp.mean(x * x, axis=-1, keepdims=True) + RMS_EPS)
    y = y * g_ref[...]
    o_ref[...] = _pack_bf16_pairs(y)
    w = wr_ref[...]
    y_hi, w_hi = y.astype(BF16), w.astype(BF16)
    y_lo = (y - y_hi.astype(F32)).astype(BF16)
    w_lo = (w - w_hi.astype(F32)).astype(BF16)
    l_ref[...] = (jnp.dot(y_hi, w_hi, preferred_element_type=F32)
                  + jnp.dot(y_lo, w_hi, preferred_element_type=F32)
                  + jnp.dot(y_hi, w_lo, preferred_element_type=F32))


def rmsnorm_router(x2, g, w_router_pad, tm=512):
    n, d = x2.shape
    ne = w_router_pad.shape[1]
    return pl.pallas_call(
        _rmsnorm_router_kernel,
        out_shape=(jax.ShapeDtypeStruct((n, d // 2), jnp.uint32), jax.ShapeDtypeStruct((n, ne), F32)),
        grid=(n // tm,),
        in_specs=[pl.BlockSpec((tm, d), lambda i: (i, 0)),
                  pl.BlockSpec((1, d), lambda i: (0, 0)),
                  pl.BlockSpec((d, ne), lambda i: (0, 0))],
        out_specs=(pl.BlockSpec((tm, d // 2), lambda i: (i, 0)),
                   pl.BlockSpec((tm, ne), lambda i: (i, 0))),
        compiler_params=_params(("parallel",)),
        name="rmsnorm_router",
    )(x2, g.reshape(1, d), w_router_pad)


def _mm_kernel(*refs, n_a, has_res):
    a_refs = refs[:n_a]
    w_refs = refs[n_a:2 * n_a]
    idx = 2 * n_a
    res_ref = refs[idx] if has_res else None
    idx += int(has_res)
    o_ref = refs[idx]
    wbf_refs = refs[idx + 1:idx + 1 + n_a]

    @pl.when(pl.program_id(1) == 0)
    def _():
        for w_ref, wbf_ref in zip(w_refs, wbf_refs):
            wbf_ref[...] = w_ref[...].astype(BF16)

    acc = None
    for a_ref, wbf_ref in zip(a_refs, wbf_refs):
        d = jnp.dot(a_ref[...].astype(BF16), wbf_ref[...], preferred_element_type=F32)
        acc = d if acc is None else acc + d
    if has_res:
        acc = acc + res_ref[...]
    o_ref[...] = acc.astype(o_ref.dtype)


def matmul(a_list, w, *, res=None, out_dtype=F32, tm=1024, tn=1024, name="matmul"):
    m = a_list[0].shape[0]
    n = w.shape[1]
    n_a = len(a_list)
    k_each = a_list[0].shape[1]
    assert all(a.shape == (m, k_each) for a in a_list) and w.shape[0] == n_a * k_each
    in_specs = [pl.BlockSpec((tm, k_each), lambda j, i: (i, 0)) for _ in a_list]
    in_specs += [pl.BlockSpec((k_each, tn), functools.partial(lambda j, i, kb: (kb, j), kb=kb))
                 for kb in range(n_a)]
    args = list(a_list) + [w] * n_a
    if res is not None:
        in_specs.append(pl.BlockSpec((tm, tn), lambda j, i: (i, j)))
        args.append(res)
    return pl.pallas_call(
        functools.partial(_mm_kernel, n_a=n_a, has_res=res is not None),
        out_shape=jax.ShapeDtypeStruct((m, n), out_dtype),
        grid=(n // tn, m // tm),
        in_specs=in_specs,
        out_specs=pl.BlockSpec((tm, tn), lambda j, i: (i, j)),
        scratch_shapes=[pltpu.VMEM((k_each, tn), BF16) for _ in a_list],
        compiler_params=_params(("parallel", "arbitrary"), big=True),
        name=name,
    )(*args)


def _rms_scale_bf16(x, g):
    y = x * lax.rsqrt(jnp.mean(x * x, axis=-1, keepdims=True) + RMS_EPS)
    return (y * g).astype(BF16)


def _ple_kernel(h_ref, p_ref, g_ref, wg_ref, wp_ref, o_ref, wg_bf, wp_bf):
    @pl.when(pl.program_id(1) == 0)
    def _():
        wg_bf[...] = wg_ref[...].astype(BF16)
        wp_bf[...] = wp_ref[...].astype(BF16)

    tn = o_ref.shape[1]
    a = jnp.dot(_rms_scale_bf16(h_ref[...], g_ref[...]), wg_bf[...], preferred_element_type=F32)
    pp = jnp.dot(p_ref[...].astype(BF16), wp_bf[...], preferred_element_type=F32)
    col = pl.multiple_of(pl.program_id(0) * tn, tn)
    o_ref[...] = h_ref[:, pl.ds(col, tn)] + jax.nn.sigmoid(a) * pp


def ple(h2, p_all, g, w_gate_all, w_proj_all, layer, tm=512, tn=1024):
    m, d = h2.shape
    pd = p_all.shape[2]
    return pl.pallas_call(
        _ple_kernel,
        out_shape=jax.ShapeDtypeStruct((m, d), F32),
        grid=(d // tn, m // tm),
        in_specs=[pl.BlockSpec((tm, d), lambda j, i: (i, 0)),
                  pl.BlockSpec((None, tm, pd), lambda j, i: (layer, i, 0)),
                  pl.BlockSpec((1, d), lambda j, i: (0, 0)),
                  pl.BlockSpec((None, d, tn), lambda j, i: (layer, 0, j)),
                  pl.BlockSpec((None, pd, tn), lambda j, i: (layer, 0, j))],
        out_specs=pl.BlockSpec((tm, tn), lambda j, i: (i, j)),
        scratch_shapes=[pltpu.VMEM((d, tn), BF16), pltpu.VMEM((pd, tn), BF16)],
        compiler_params=_params(("parallel", "arbitrary"), big=True),
        name="ple",
    )(h2, p_all, g.reshape(1, d), w_gate_all, w_proj_all)


def _row_copy(src_hbm, row, dst, i, sem):
    return pltpu.make_async_copy(src_hbm.at[pl.ds(row, 1), :], dst.at[pl.ds(i, 1), :], sem)


def _swiglu_kernel(exp_ref, rows_ref, src_ref, tok_ref, x_ref, *rest, n_col, gather, sub_rows, nf):
    del exp_ref
    gain_ref = None if gather else rest[0]
    wg_ref, wu_ref, wd_ref, o_ref = rest[0 if gather else 1:][:4]
    scratch = rest[(4 if gather else 5):]
    t = pl.program_id(0)
    f = pl.program_id(1)
    nt = pl.num_programs(0)
    n_rows = rows_ref[t]
    tm = o_ref.shape[0]

    if not gather:
        (xbf,) = scratch

        @pl.when(f == 0)
        def _():
            h = x_ref[...]
            o_ref[...] = h
            xbf[...] = _rms_scale_bf16(h, gain_ref[...])

    if gather:
        @pl.when(f == 0)
        def _():
            o_ref[...] = jnp.zeros_like(o_ref)

        xbuf, xbf, sems = scratch
        slot = t % 2
        n_fetch = xbuf.shape[1]
        per_step = n_fetch // nf

        def start_row(tile, sl, i):
            _row_copy(x_ref, tok_ref[src_ref[tile] + i], xbuf.at[sl], i, sems.at[sl, i]).start()

        def wait_all(sl):
            def body(c, carry):
                for r in range(DMA_LOOP_UNROLL):
                    i = c * DMA_LOOP_UNROLL + r
                    _row_copy(x_ref, 0, xbuf.at[sl], i, sems.at[sl, i]).wait()
                return carry
            lax.fori_loop(0, n_fetch // DMA_LOOP_UNROLL, body, 0)

        @pl.when(jnp.logical_and(f == 0, t == 0))
        def _():
            def body(i, carry):
                start_row(0, 0, i)
                return carry
            lax.fori_loop(0, n_fetch, body, 0, unroll=DMA_LOOP_UNROLL)

        @pl.when(f == 0)
        def _():
            wait_all(slot)
            xbf[...] = _unpack_bf16_pairs(xbuf[slot, 0:tm])

        def fetch_ahead():
            nxt = jnp.minimum(t + 1, nt - 1)
            for r in range(per_step):
                start_row(nxt, 1 - slot, f * per_step + r)
    else:
        def fetch_ahead():
            pass

    def ffn_rows(n):
        fetch_ahead()
        if n == 0:
            return
        x = xbf[0:n, :]
        g = jnp.dot(x, wg_ref[...].astype(BF16), preferred_element_type=F32)
        u = jnp.dot(x, wu_ref[...].astype(BF16), preferred_element_type=F32)
        hmid = (g * jax.nn.sigmoid(g) * u).astype(BF16)
        wd = wd_ref[...].astype(BF16)
        cw = o_ref.shape[1] // n_col
        for c in range(n_col):
            y = jnp.dot(hmid, wd[:, c * cw:(c + 1) * cw], preferred_element_type=F32)
            o_ref[0:n, c * cw:(c + 1) * cw] += y

    for k in range(0 if gather else 1, tm // sub_rows + 1):
        covers = jnp.logical_and(n_rows > (k - 1) * sub_rows, n_rows <= k * sub_rows)
        pl.when(covers)(functools.partial(ffn_rows, k * sub_rows))

    if gather:
        @pl.when(jnp.logical_and(t == nt - 1, f == nf - 1))
        def _():
            wait_all(1 - slot)


def gather_rows_per_tile(tm, nf):
    per_step = -(-tm // nf)
    while (per_step * nf) % DMA_LOOP_UNROLL:
        per_step += 1
    return per_step * nf


def swiglu(x, w_gate, w_up, w_down, tile_exp, tile_rows, tile_src=None, src_rows=None, *, gain=None,
           tm=MOE_TILE, tf=FFN_F_TILE, sub_rows=MOE_TILE, name="swiglu"):
    gather = src_rows is not None
    assert gather != (gain is not None)
    m = tile_rows.shape[0] * tm
    d = w_gate.shape[1]
    dff = w_gate.shape[2]
    nf = dff // tf
    nt = m // tm
    assert nf >= 2

    def w_in_map(t, f, exp_ref, rows_ref, src_ref, tok_ref):
        return (exp_ref[t], 0, jnp.where(rows_ref[t] > 0, f, nf - 1))

    def w_out_map(t, f, exp_ref, rows_ref, src_ref, tok_ref):
        return (exp_ref[t], jnp.where(rows_ref[t] > 0, f, nf - 1), 0)

    def tile_map(t, f, exp_ref, rows_ref, src_ref, tok_ref):
        return (t, 0)

    w_specs = [pl.BlockSpec((None, d, tf), w_in_map),
               pl.BlockSpec((None, d, tf), w_in_map),
               pl.BlockSpec((None, tf, d), w_out_map)]
    if gather:
        in_specs = [pl.BlockSpec(memory_space=pl.ANY)] + w_specs
        args = [x, w_gate, w_up, w_down]
        n_fetch = gather_rows_per_tile(tm, nf)
        scratch = [pltpu.VMEM((2, n_fetch, d // 2), jnp.uint32), pltpu.VMEM((tm, d), BF16),
                   pltpu.SemaphoreType.DMA((2, n_fetch))]
    else:
        in_specs = [pl.BlockSpec((tm, d), tile_map, pipeline_mode=pl.Buffered(1)),
                    pl.BlockSpec((1, d), lambda t, f, e, r, s, k: (0, 0))] + w_specs
        args = [x, gain.reshape(1, d), w_gate, w_up, w_down]
        scratch = [pltpu.VMEM((tm, d), BF16)]
        tile_src = jnp.zeros((nt,), jnp.int32)
        src_rows = jnp.zeros((1,), jnp.int32)
    grid_spec = pltpu.PrefetchScalarGridSpec(
        num_scalar_prefetch=4,
        grid=(nt, nf),
        in_specs=in_specs,
        out_specs=pl.BlockSpec((tm, d), tile_map, pipeline_mode=pl.Buffered(1)),
        scratch_shapes=scratch,
    )
    return pl.pallas_call(
        functools.partial(_swiglu_kernel, n_col=4, gather=gather, sub_rows=sub_rows, nf=nf),
        out_shape=jax.ShapeDtypeStruct((m, d), F32),
        grid_spec=grid_spec,
        compiler_params=_params(("arbitrary", "arbitrary"), big=True),
        name=name,
    )(tile_exp, tile_rows, tile_src, src_rows, *args)


COMBINE_TILE = 256


def _combine_kernel(d0_ref, d1_ref, ys_hbm, gate_ref, h_ref, hout_ref, buf, sems):
    i = pl.program_id(0)
    n = pl.num_programs(0)
    tt = h_ref.shape[0]
    slot = i % 2
    dests = (d0_ref, d1_ref)

    def start_tile(tile, sl):
        def body(r, c):
            for k, d_ref in enumerate(dests):
                _row_copy(ys_hbm, d_ref[tile * tt + r], buf.at[sl, k], r, sems.at[sl, k, r]).start()
            return c
        lax.fori_loop(0, tt, body, 0, unroll=DMA_LOOP_UNROLL)

    @pl.when(i == 0)
    def _():
        start_tile(0, 0)

    @pl.when(i + 1 < n)
    def _():
        start_tile(i + 1, 1 - slot)

    def wait_body(r, c):
        for k in range(TOP_K):
            _row_copy(ys_hbm, 0, buf.at[slot, k], r, sems.at[slot, k, r]).wait()
        return c
    lax.fori_loop(0, tt, wait_body, 0, unroll=DMA_LOOP_UNROLL)

    gates = gate_ref[...]
    hout_ref[...] = h_ref[...] + (buf[slot, 0] * gates[:, 0:1] + buf[slot, 1] * gates[:, 1:2])


def moe_combine(h2, ys, dest2, gates, tt=COMBINE_TILE):
    n, d = h2.shape
    grid_spec = pltpu.PrefetchScalarGridSpec(
        num_scalar_prefetch=2,
        grid=(n // tt,),
        in_specs=[pl.BlockSpec(memory_space=pl.ANY),
                  pl.BlockSpec((tt, TOP_K), lambda i, a, b: (i, 0)),
                  pl.BlockSpec((tt, d), lambda i, a, b: (i, 0))],
        out_specs=pl.BlockSpec((tt, d), lambda i, a, b: (i, 0)),
        scratch_shapes=[pltpu.VMEM((2, TOP_K, tt, d), F32), pltpu.SemaphoreType.DMA((2, TOP_K, tt))],
    )
    return pl.pallas_call(
        _combine_kernel,
        out_shape=jax.ShapeDtypeStruct((n, d), F32),
        grid_spec=grid_spec,
        compiler_params=_params(("arbitrary",), big=True),
        name="moe_combine",
    )(dest2[:, 0], dest2[:, 1], ys, gates, h2)


def _t5_bucket(rel):
    half = T5_BUCKETS // 2
    max_exact = half // 2
    n = jnp.abs(rel)
    log_ratio = jnp.log(jnp.maximum(n, 1).astype(F32) / max_exact) / math.log(T5_MAX_DIST / max_exact)
    large = jnp.minimum(max_exact + (log_ratio * (half - max_exact)).astype(jnp.int32), half - 1)
    return jnp.where(rel > 0, half, 0) + jnp.where(n < max_exact, n, large)


def _window_bias_table(t5_bias):
    i = jnp.arange(A_BLOCK)[:, None]
    j = jnp.arange(3 * A_BLOCK)[None, :]
    rel = j - A_BLOCK - i
    onehot = (_t5_bucket(rel)[None] == jnp.arange(T5_BUCKETS)[:, None, None]).astype(F32)
    bias = jnp.einsum('bh,bij->hij', t5_bias.astype(F32), onehot, precision=lax.Precision.HIGHEST)
    return jnp.where((jnp.abs(rel) <= A_BLOCK)[None], bias, NEG_INF)


def _window_kernel(sink_ref, q_ref, kv_ref, bias_ref, o_ref, *, nb):
    n = pl.program_id(1)
    scale = HEAD_DIM ** -0.5
    starts = (jnp.maximum(n - 1, 0), n, jnp.minimum(n + 1, nb - 1))
    kv = [kv_ref[pl.ds(pl.multiple_of(s * A_BLOCK, A_BLOCK), A_BLOCK), :] for s in starts]
    col = lax.broadcasted_iota(jnp.int32, (A_BLOCK, 3 * A_BLOCK), 1)
    edge_ok = jnp.logical_and(jnp.logical_or(n > 0, col >= A_BLOCK),
                              jnp.logical_or(n < nb - 1, col < 2 * A_BLOCK))
    kv = jnp.concatenate(kv, axis=0)
    q = q_ref[...]
    lo = lax.broadcasted_iota(jnp.int32, (1, 2 * HEAD_DIM), 1) < HEAD_DIM
    sel = (jnp.where(lo, scale, 0.0).astype(BF16), jnp.where(lo, 0.0, scale).astype(BF16))
    pairs_per_kv = A_GROUP // 2
    kk, vv = [], []
    for kh in range(A_KV_HEADS):
        k1 = kv[:, kh * HEAD_DIM:(kh + 1) * HEAD_DIM]
        v1 = kv[:, A_KV_WIDTH + kh * HEAD_DIM:A_KV_WIDTH + (kh + 1) * HEAD_DIM]
        kk.append(jnp.concatenate([k1, k1], axis=1))
        vv.append(jnp.concatenate([v1, v1], axis=1))
    scores = []
    for pr in range(A_Q_HEADS // 2):
        q2 = q[:, pr * 2 * HEAD_DIM:(pr + 1) * 2 * HEAD_DIM]
        qq = jnp.concatenate([q2 * sel[0], q2 * sel[1]], axis=0)
        s2 = lax.dot_general(qq, kk[pr // pairs_per_kv], (((1,), (1,)), ((), ())),
                             preferred_element_type=F32)
        for i in range(2):
            scores.append(jnp.where(edge_ok, s2[i * A_BLOCK:(i + 1) * A_BLOCK] + bias_ref[2 * pr + i],
                                    NEG_INF))
    probs, dens = [], []
    for h, s in enumerate(scores):
        sink = sink_ref[h]
        mx = jnp.maximum(jnp.max(s, axis=-1, keepdims=True), sink)
        p = jnp.exp(s - mx)
        dens.append(jnp.sum(p, axis=-1, keepdims=True) + jnp.exp(sink - mx))
        probs.append(p.astype(BF16))
    for pr in range(A_Q_HEADS // 2):
        v2 = vv[pr // pairs_per_kv]
        oa = jnp.dot(probs[2 * pr], v2, preferred_element_type=F32) / dens[2 * pr]
        ob = jnp.dot(probs[2 * pr + 1], v2, preferred_element_type=F32) / dens[2 * pr + 1]
        o_ref[:, pr * 2 * HEAD_DIM:(pr + 1) * 2 * HEAD_DIM] = jnp.where(lo, oa, ob).astype(o_ref.dtype)


def window_attention(proj_a, t5_bias, sink):
    b, s, _ = proj_a.shape
    nb = s // A_BLOCK
    kvw = 2 * A_KV_WIDTH
    grid_spec = pltpu.PrefetchScalarGridSpec(
        num_scalar_prefetch=0,
        grid=(b, nb),
        in_specs=[pl.BlockSpec(memory_space=pltpu.SMEM),
                  pl.BlockSpec((None, A_BLOCK, A_WIDTH), lambda bi, n: (bi, n, 0)),
                  pl.BlockSpec((None, s, kvw), lambda bi, n: (bi, 0, A_WIDTH // kvw)),
                  pl.BlockSpec((A_Q_HEADS, A_BLOCK, 3 * A_BLOCK), lambda bi, n: (0, 0, 0))],
        out_specs=pl.BlockSpec((None, A_BLOCK, A_WIDTH), lambda bi, n: (bi, n, 0)),
    )
    return pl.pallas_call(
        functools.partial(_window_kernel, nb=nb),
        out_shape=jax.ShapeDtypeStruct((b, s, A_WIDTH), BF16),
        grid_spec=grid_spec,
        compiler_params=_params(("parallel", "arbitrary")),
        name="window_attention",
    )(sink.astype(F32), proj_a, proj_a, _window_bias_table(t5_bias))


NA_HEAD_GROUP = 16


def _na_bias_table(rpb, rows):
    kh = min(NA_WIN_H, rows)
    kw = NA_WIN_W
    c = jnp.arange(GRID_W)
    cs = jnp.clip(c - kw // 2, 0, GRID_W - kw)
    col_ok = (c[None] >= cs[:, None]) & (c[None] < cs[:, None] + kw)
    col_off = jnp.clip(c[None] - c[:, None], -(kw - 1), kw - 1) + kw - 1
    onehot = (col_off[None] == jnp.arange(2 * kw - 1)[:, None, None]).astype(F32)
    by_col = jnp.einsum('hrc,cqk->hqrk', rpb.astype(F32), onehot, precision=lax.Precision.HIGHEST)
    by_col = jnp.where(col_ok[None, :, None, :], by_col, NEG_INF)
    tabs = [by_col[:, :, NA_WIN_H - 1 - d:NA_WIN_H - 1 - d + kh] for d in range(kh)]
    return jnp.stack(tabs, axis=0).reshape(kh, rpb.shape[0], GRID_W, kh * GRID_W)


def _na_kernel(q_ref, k_ref, v_ref, bias_ref, o_ref, *, rows):
    r = pl.program_id(2)
    kh = min(NA_WIN_H, rows)
    scale = HEAD_DIM ** -0.5
    start = pl.multiple_of(jnp.clip(r - kh // 2, 0, rows - kh) * GRID_W, GRID_W)
    k = k_ref[pl.ds(start, kh * GRID_W), :]
    v = v_ref[pl.ds(start, kh * GRID_W), :]
    q = q_ref[...]
    lo = lax.broadcasted_iota(jnp.int32, (1, 2 * HEAD_DIM), 1) < HEAD_DIM
    sel = (jnp.where(lo, scale, 0.0).astype(BF16), jnp.where(lo, 0.0, scale).astype(BF16))
    n_pairs = NA_HEAD_GROUP // 2
    scores = []
    for pr in range(n_pairs):
        sl = slice(pr * 2 * HEAD_DIM, (pr + 1) * 2 * HEAD_DIM)
        q2 = q[:, sl]
        qq = jnp.concatenate([q2 * sel[0], q2 * sel[1]], axis=0)
        s2 = lax.dot_general(qq, k[:, sl], (((1,), (1,)), ((), ())), preferred_element_type=F32)
        for i in range(2):
            scores.append(s2[i * GRID_W:(i + 1) * GRID_W] + bias_ref[2 * pr + i])
    probs, dens = [], []
    for s in scores:
        mx = jnp.max(s, axis=-1, keepdims=True)
        p = jnp.exp(s - mx)
        dens.append(jnp.sum(p, axis=-1, keepdims=True))
        probs.append(p.astype(BF16))
    for pr in range(n_pairs):
        sl = slice(pr * 2 * HEAD_DIM, (pr + 1) * 2 * HEAD_DIM)
        oa = jnp.dot(probs[2 * pr], v[:, sl], preferred_element_type=F32) / dens[2 * pr]
        ob = jnp.dot(probs[2 * pr + 1], v[:, sl], preferred_element_type=F32) / dens[2 * pr + 1]
        o_ref[:, sl] = jnp.where(lo, oa, ob).astype(o_ref.dtype)


def neighbourhood_attention(proj_n, rpb):
    b, s, _ = proj_n.shape
    rows = s // GRID_W
    kh = min(NA_WIN_H, rows)
    gw = NA_HEAD_GROUP * HEAD_DIM
    ng = B_WIDTH // gw

    def bias_map(bi, g, r):
        return (r - jnp.clip(r - kh // 2, 0, rows - kh), g, 0, 0)

    return pl.pallas_call(
        functools.partial(_na_kernel, rows=rows),
        out_shape=jax.ShapeDtypeStruct((b, s, B_WIDTH), BF16),
        grid=(b, ng, rows),
        in_specs=[pl.BlockSpec((None, GRID_W, gw), lambda bi, g, r: (bi, r, g)),
                  pl.BlockSpec((None, s, gw), lambda bi, g, r: (bi, 0, ng + g)),
                  pl.BlockSpec((None, s, gw), lambda bi, g, r: (bi, 0, 2 * ng + g)),
                  pl.BlockSpec((None, NA_HEAD_GROUP, GRID_W, kh * GRID_W), bias_map)],
        out_specs=pl.BlockSpec((None, GRID_W, gw), lambda bi, g, r: (bi, r, g)),
        compiler_params=_params(("parallel", "parallel", "arbitrary"), big=True),
        name="neighbourhood_attention",
    )(proj_n, proj_n, proj_n, _na_bias_table(rpb, rows))


CONV_HALO = 16
CONV_COL_CHUNK = 512


def _hyena_in_kernel(prev_ref, cur_ref, next_ref, w_ref, b_ref, cw_ref, cb_ref, o_ref, wbf_ref, *,
                     tiles_per_seq):
    i = pl.program_id(1)

    @pl.when(i == 0)
    def _():
        wbf_ref[...] = w_ref[...].astype(BF16)

    tm = cur_ref.shape[0]
    rows = tm + 2 * CONV_HALO
    a = jnp.concatenate([prev_ref[...], cur_ref[...], next_ref[...]], axis=0)
    pos = i % tiles_per_seq
    row = lax.broadcasted_iota(jnp.int32, (rows, 1), 0)
    kill_up = jnp.logical_and(pos == 0, row == CONV_HALO)
    kill_dn = jnp.logical_and(pos == tiles_per_seq - 1, row == CONV_HALO + tm - 1)
    for c in range(o_ref.shape[1] // CONV_COL_CHUNK):
        cs = slice(c * CONV_COL_CHUNK, (c + 1) * CONV_COL_CHUNK)
        p = jnp.dot(a, wbf_ref[:, cs], preferred_element_type=F32) + b_ref[:, cs]
        up = jnp.where(kill_up, 0.0, pltpu.roll(p, 1, 0))
        dn = jnp.where(kill_dn, 0.0, pltpu.roll(p, rows - 1, 0))
        y = up * cw_ref[0:1, cs] + p * cw_ref[1:2, cs] + dn * cw_ref[2:3, cs] + cb_ref[:, cs]
        o_ref[:, cs] = y[CONV_HALO:CONV_HALO + tm]


def hyena_in_conv(hn, w_in, b_in, conv_w, conv_b, seq, tm=1024, tn=1024):
    m, k = hn.shape
    n = w_in.shape[1]
    hb = tm // CONV_HALO
    last_halo = m // CONV_HALO - 1
    return pl.pallas_call(
        functools.partial(_hyena_in_kernel, tiles_per_seq=seq // tm),
        out_shape=jax.ShapeDtypeStruct((m, n), F32),
        grid=(n // tn, m // tm),
        in_specs=[pl.BlockSpec((CONV_HALO, k), lambda j, i: (jnp.maximum(i * hb - 1, 0), 0)),
                  pl.BlockSpec((tm, k), lambda j, i: (i, 0)),
                  pl.BlockSpec((CONV_HALO, k), lambda j, i: (jnp.minimum((i + 1) * hb, last_halo), 0)),
                  pl.BlockSpec((k, tn), lambda j, i: (0, j)),
                  pl.BlockSpec((1, tn), lambda j, i: (0, j)),
                  pl.BlockSpec((3, tn), lambda j, i: (0, j)),
                  pl.BlockSpec((1, tn), lambda j, i: (0, j))],
        out_specs=pl.BlockSpec((tm, tn), lambda j, i: (i, j)),
        scratch_shapes=[pltpu.VMEM((k, tn), BF16)],
        compiler_params=_params(("parallel", "arbitrary"), big=True),
        name="hyena_in_conv",
    )(hn, hn, hn, w_in, b_in.reshape(1, n), conv_w, conv_b.reshape(1, n))


def _filter_stage1_kernel(z_ref, t_ref, fw1_ref, fb1_ref, fw2_ref, fb2_ref, fr_ref, w3f0_ref, w3f1_ref,
                          w3b0_ref, w3b1_ref, delta_ref, f1_ref, y_ref, hid_ref, *, seq):
    j = pl.program_id(0)

    @pl.when(pl.program_id(1) == 0)
    def _():
        hi = lax.Precision.HIGHEST
        fr = fr_ref[...]
        hid = jnp.sin(fr * (jnp.dot(z_ref[...], fw1_ref[...], preferred_element_type=F32, precision=hi)
                            + fb1_ref[...]))
        hid = jnp.sin(fr * (jnp.dot(hid, fw2_ref[...], preferred_element_type=F32, precision=hi)
                            + fb2_ref[...]))
        hid_ref[...] = hid.astype(BF16)

    hid_bf = hid_ref[...]
    r = lax.broadcasted_iota(jnp.int32, (hid_bf.shape[0], 1), 0)
    u = j * FFT_CHUNK + r // FFT_N2 + FFT_N1 * (r % FFT_N2)
    decay = jnp.where(u == seq, 0.0, jnp.exp(-t_ref[...] * delta_ref[...]))
    for o, (wf_ref, wb_ref) in enumerate(((w3f0_ref, w3b0_ref), (w3f1_ref, w3b1_ref))):
        fwd = jnp.dot(hid_bf, wf_ref[...].astype(BF16), preferred_element_type=F32)
        bwd = jnp.dot(hid_bf, wb_ref[...].astype(BF16), preferred_element_type=F32)
        taps = (jnp.where(u > seq, bwd, jnp.where(u == 0, fwd + bwd, fwd)) * decay).astype(BF16)
        for jj in range(FFT_CHUNK):
            x = taps[jj * FFT_N2:(jj + 1) * FFT_N2]
            y_ref[o, jj] = _pack_bf16_pairs(jnp.dot(f1_ref[jj], x, preferred_element_type=F32))


def hyena_filter_stage1(seq, fw1, fb1, fw2, fb2, freq, fw3, width, consts, cb=FFT_CB):
    _, f1_real, _, _, _ = consts
    n = 2 * seq
    n1, n2, ch = FFT_N1, FFT_N2, FFT_CHUNK
    assert fw3.shape[1] == 4 * width
    r = jnp.arange(n)
    u = (r // (ch * n2)) * ch + (r // n2) % ch + n1 * (r % n2)
    pos = jnp.where(u <= seq, u, n - u)
    pos = jnp.where(u == seq, 0, pos)
    bands = (HYENA_EMB - 1) // 2
    f = jnp.linspace(1e-4, bands - 1, bands, dtype=F32)[None]
    t = (pos.astype(F32) / (seq - 1))[:, None]
    w = ((2.0 * math.pi / seq) * pos.astype(F32))[:, None]
    z = jnp.concatenate([t, jnp.cos(f * w), -jnp.sin(f * w)], axis=-1)
    emb_pad = 40
    z = jnp.pad(z, ((0, 0), (0, emb_pad - HYENA_EMB)))
    fw1p = jnp.pad(fw1.astype(F32), ((0, emb_pad - HYENA_EMB), (0, 0)))
    fh = fw1.shape[1]
    max_decay = math.log(HYENA_DECAY_TARGET) / HYENA_FAST_PCT
    min_decay = math.log(HYENA_DECAY_TARGET) / HYENA_SLOW_PCT
    deltas = jnp.abs(jnp.linspace(min_decay, max_decay, width, dtype=F32))[None]
    ncb = width // cb
    rows = ch * n2

    def small(shape):
        return pl.BlockSpec(shape, lambda j, k: (0, 0))

    def w3_spec(blk):
        return pl.BlockSpec((fh, cb), lambda j, k: (0, blk * ncb + k))

    return pl.pallas_call(
        functools.partial(_filter_stage1_kernel, seq=seq),
        out_shape=jax.ShapeDtypeStruct((2, n1, 2 * n2, width // 2), jnp.uint32),
        grid=(n1 // ch, ncb),
        in_specs=[pl.BlockSpec((rows, emb_pad), lambda j, k: (j, 0)),
                  pl.BlockSpec((rows, 1), lambda j, k: (j, 0)),
                  small((emb_pad, fh)), small((1, fh)), small((fh, fh)), small((1, fh)), small((1, fh)),
                  w3_spec(0), w3_spec(1), w3_spec(2), w3_spec(3),
                  pl.BlockSpec((1, cb), lambda j, k: (0, k)),
                  pl.BlockSpec((ch, 2 * n2, n2), lambda j, k: (j, 0, 0))],
        out_specs=pl.BlockSpec((2, ch, 2 * n2, cb // 2), lambda j, k: (0, j, 0, k)),
        scratch_shapes=[pltpu.VMEM((rows, fh), BF16)],
        compiler_params=_params(("parallel", "arbitrary"), big=True),
        name="hyena_filter_stage1",
    )(z, t, fw1p, fb1.reshape(1, fh).astype(F32), fw2.astype(F32), fb2.reshape(1, fh).astype(F32),
      freq.reshape(1, fh).astype(F32), fw3, fw3, fw3, fw3, deltas, f1_real)


def _dft_constants():
    n1, n2 = FFT_N1, FFT_N2
    n = n1 * n2
    a1 = np.arange(n1)
    a2 = np.arange(n2)
    half = n2 // 2

    def cplx_block(w):
        return np.block([[w.real, -w.imag], [w.imag, w.real]])

    ang = (a2[None, :, None] * a2[None, None, :] / n2) + (a1[:, None, None] * a2[None, :, None] / n)
    w1 = np.exp(-2j * np.pi * ang)
    f1_pad = np.stack([cplx_block(w1[i][:, :half]) for i in range(n1)])
    f1_real = np.stack([np.concatenate([w1[i].real, w1[i].imag], axis=0) for i in range(n1)])
    w2 = np.exp(-2j * np.pi * (a1[:, None] * a1[None, :]) / n1)
    f2 = cplx_block(w2)
    ang = (a1[None, :, None] * a1[None, None, :] / n1) + (a2[:, None, None] * a1[None, :, None] / n)
    g2 = np.stack([cplx_block(m_) for m_ in np.exp(2j * np.pi * ang)])
    wg1 = np.exp(2j * np.pi * (a2[:half, None] * a2[None, :]) / n2) / n
    g1 = cplx_block(wg1)
    to = lambda x: jnp.asarray(x.astype(np.float32)).astype(BF16)
    return to(f1_pad), to(f1_real), to(f2), to(g2), to(g1)


def _fft_stage1_block(x_re, x_im, f_ref, y_ref):
    xr = pltpu.einshape("mjc->jmc", x_re)
    xi = pltpu.einshape("mjc->jmc", x_im)
    for j in range(FFT_CHUNK):
        xc = jnp.concatenate([xr[j], xi[j]], axis=0).astype(BF16)
        y_ref[j] = _pack_bf16_pairs(jnp.dot(f_ref[j], xc, preferred_element_type=F32))


def _fft_stage1_kernel(x_ref, f_ref, y_ref):
    _fft_stage1_block(x_ref[0], x_ref[1], f_ref, y_ref)


def _fft_stage2_kernel(yr_ref, yi_ref, f2_ref, h_ref):
    yr = pltpu.einshape("nkc->knc", yr_ref[...])
    yi = pltpu.einshape("nkc->knc", yi_ref[...])
    for j in range(FFT_CHUNK):
        yc = _unpack_bf16_pairs(jnp.concatenate([yr[j], yi[j]], axis=0))
        h_ref[j] = _pack_bf16_pairs(jnp.dot(f2_ref[...], yc, preferred_element_type=F32))


def _fft_mid_kernel(yr_ref, yi_ref, f2_ref, h_ref, g2_ref, q_ref):
    yr = pltpu.einshape("nkc->knc", yr_ref[...])
    yi = pltpu.einshape("nkc->knc", yi_ref[...])
    n1 = FFT_N1
    spec = [jnp.dot(f2_ref[...], _unpack_bf16_pairs(jnp.concatenate([yr[j], yi[j]], axis=0)),
                    preferred_element_type=F32) for j in range(FFT_CHUNK)]
    prod = []
    for j, z in enumerate(spec):
        hf = _unpack_pairs_f32(h_ref[j])
        zr, zi, hr, hi = z[:n1], z[n1:], hf[:n1], hf[n1:]
        prod.append(jnp.concatenate([zr * hr - zi * hi, zr * hi + zi * hr], axis=0).astype(BF16))
    for j, pc in enumerate(prod):
        q_ref[j] = _pack_bf16_pairs(jnp.dot(g2_ref[j], pc, preferred_element_type=F32))


def _fft_last_kernel(qr_ref, qi_ref, g1_ref, gate_ref, zf_ref, fb_ref, *rest, feeds_next):
    f1_ref, o_ref, y_ref = rest if feeds_next else (None, rest[0], None)
    qr = pltpu.einshape("kjc->jkc", qr_ref[...])
    qi = pltpu.einshape("kjc->jkc", qi_ref[...])
    half = FFT_N2 // 2
    conv = []
    for j in range(FFT_CHUNK):
        qc = _unpack_bf16_pairs(jnp.concatenate([qr[j], qi[j]], axis=0))
        conv.append(jnp.dot(g1_ref[...], qc, preferred_element_type=F32))
    conv = jnp.stack(conv, axis=0)
    fb = fb_ref[...]
    out = []
    for bi in range(2):
        cb = pltpu.einshape("jmc->mjc", conv[:, bi * half:(bi + 1) * half])
        out.append(gate_ref[bi] * (cb + zf_ref[bi] * fb))
        o_ref[bi] = out[bi]
    if feeds_next:
        _fft_stage1_block(out[0], out[1], f1_ref, y_ref)


def hyena_filter_spectrum(y, consts, cb=FFT_CB):
    _, _, f2, _, _ = consts
    no = y.shape[0]
    c = 2 * y.shape[-1]
    n1, n2, ch = FFT_N1, FFT_N2, FFT_CHUNK
    nk = n2 // ch
    return pl.pallas_call(
        _fft_stage2_kernel,
        out_shape=jax.ShapeDtypeStruct((no, n2, 2 * n1, c // 2), jnp.uint32),
        grid=(no, nk, c // cb),
        in_specs=[pl.BlockSpec((None, n1, ch, cb // 2), lambda o, j, k: (o, 0, j, k)),
                  pl.BlockSpec((None, n1, ch, cb // 2), lambda o, j, k: (o, 0, nk + j, k)),
                  pl.BlockSpec((2 * n1, 2 * n1), lambda o, j, k: (0, 0))],
        out_specs=pl.BlockSpec((None, ch, 2 * n1, cb // 2), lambda o, j, k: (o, j, 0, k)),
        compiler_params=_params(("parallel", "parallel", "parallel"), big=True),
        name="fft_filter_stage2",
    )(y, y, f2)


def hyena_long_conv_gate(zsrc, z_col, gate_src, gate_col, hf, order, fbias, consts, y=None,
                         feeds_next=False, cb=FFT_CB):
    f1_pad, _, f2, g2, g1 = consts
    n1, n2, ch = FFT_N1, FFT_N2, FFT_CHUNK
    c = 2 * hf.shape[-1]
    ncb = c // cb
    half = n2 // 2
    y_shape = jax.ShapeDtypeStruct((n1, 2 * n2, c // 2), jnp.uint32)
    f1_spec = pl.BlockSpec((ch, 2 * n2, n2), lambda j, k: (j, 0, 0))
    y_spec = pl.BlockSpec((ch, 2 * n2, cb // 2), lambda j, k: (j, 0, k))
    if y is None:
        y = pl.pallas_call(
            _fft_stage1_kernel,
            out_shape=y_shape,
            grid=(n1 // ch, ncb),
            in_specs=[pl.BlockSpec((2, half, ch, cb), lambda j, k: (0, 0, j, z_col * ncb + k)), f1_spec],
            out_specs=y_spec,
            compiler_params=_params(("parallel", "parallel"), big=True),
            name="fft_stage1",
        )(zsrc, f1_pad)
    nk = n2 // ch
    q = pl.pallas_call(
        _fft_mid_kernel,
        out_shape=jax.ShapeDtypeStruct((n2, 2 * n1, c // 2), jnp.uint32),
        grid=(nk, ncb),
        in_specs=[pl.BlockSpec((n1, ch, cb // 2), lambda j, k: (0, j, k)),
                  pl.BlockSpec((n1, ch, cb // 2), lambda j, k: (0, nk + j, k)),
                  pl.BlockSpec((2 * n1, 2 * n1), lambda j, k: (0, 0)),
                  pl.BlockSpec((None, ch, 2 * n1, cb // 2), lambda j, k: (order, j, 0, k)),
                  pl.BlockSpec((ch, 2 * n1, 2 * n1), lambda j, k: (j, 0, 0))],
        out_specs=pl.BlockSpec((ch, 2 * n1, cb // 2), lambda j, k: (j, 0, k)),
        compiler_params=_params(("parallel", "parallel"), big=True),
        name="fft_mid",
    )(y, y, f2, hf, g2)
    nj = n1 // ch
    in_specs = [pl.BlockSpec((n2, ch, cb // 2), lambda j, k: (0, j, k)),
                pl.BlockSpec((n2, ch, cb // 2), lambda j, k: (0, nj + j, k)),
                pl.BlockSpec((2 * half, 2 * n2), lambda j, k: (0, 0)),
                pl.BlockSpec((2, half, ch, cb), lambda j, k: (0, 0, j, gate_col * ncb + k)),
                pl.BlockSpec((2, half, ch, cb), lambda j, k: (0, 0, j, z_col * ncb + k)),
                pl.BlockSpec((None, 1, cb), lambda j, k: (order, 0, k))]
    args = [q, q, g1, gate_src, zsrc, fbias.reshape(fbias.shape[0], 1, c)]
    out_shape = jax.ShapeDtypeStruct((2, half, n1, c), F32)
    out_spec = pl.BlockSpec((2, half, ch, cb), lambda j, k: (0, 0, j, k))
    if feeds_next:
        in_specs.append(f1_spec)
        args.append(f1_pad)
        out_shape, out_spec = (out_shape, y_shape), (out_spec, y_spec)
    res = pl.pallas_call(
        functools.partial(_fft_last_kernel, feeds_next=feeds_next),
        out_shape=out_shape,
        grid=(nj, ncb),
        in_specs=in_specs,
        out_specs=out_spec,
        compiler_params=_params(("parallel", "parallel"), big=True),
        name="fft_last",
    )(*args)
    return res if feeds_next else (res, None)


def hyena_mixer(hn, h_res, w_in, b_in, conv_w, conv_b, fw1, fb1, fw2, fb2, freq, fw3, fbias, w_out,
                batch, seq):
    width = w_out.shape[0]
    assert batch == 2 and 2 * seq == FFT_N1 * FFT_N2
    sc = hyena_in_conv(hn, w_in, b_in, conv_w, conv_b, seq)
    consts = _dft_constants()
    hf = hyena_filter_spectrum(hyena_filter_stage1(seq, fw1, fb1, fw2, fb2, freq, fw3, width, consts),
                               consts)
    sc4 = sc.reshape(batch, FFT_N2 // 2, FFT_N1, 3 * width)
    zf1, y1 = hyena_long_conv_gate(sc4, 2, sc4, 0, hf, 0, fbias, consts, feeds_next=True)
    zf2, _ = hyena_long_conv_gate(zf1, 0, sc4, 1, hf, 1, fbias, consts, y=y1)
    return matmul([zf2.reshape(batch * seq, width)], w_out, res=h_res, tm=512, tn=1024,
                  name="hyena_out")


def moe_swiglu(hn2_packed, logits, wg, wu, wd):
    n = hn2_packed.shape[0]
    top_v, top_i = lax.top_k(logits, TOP_K)
    gates = jax.nn.softmax(top_v, axis=-1)
    e_flat = top_i.reshape(-1).astype(jnp.int32)
    nk = n * TOP_K
    onehot = (e_flat[:, None] == jnp.arange(N_EXPERTS, dtype=jnp.int32)[None]).astype(jnp.int32)
    csum = jnp.cumsum(onehot, axis=0)
    rank = jnp.take_along_axis(csum, e_flat[:, None], axis=1)[:, 0] - 1
    counts = csum[-1]
    padded = ((counts + MOE_TILE - 1) // MOE_TILE) * MOE_TILE
    pad_end = jnp.cumsum(padded)
    pad_start = pad_end - padded
    dest = pad_start[e_flat] + rank
    p_rows = nk + N_EXPERTS * MOE_TILE
    order = jnp.argsort(e_flat, stable=True).astype(jnp.int32)
    nf = wg.shape[2] // MOE_F_TILE
    sorted_tok = jnp.pad(order // TOP_K, (0, gather_rows_per_tile(MOE_TILE, nf)))
    start = jnp.cumsum(counts) - counts
    nt = p_rows // MOE_TILE
    tile_start = jnp.arange(nt, dtype=jnp.int32) * MOE_TILE
    tile_used = tile_start < pad_end[-1]
    tile_exp = jnp.minimum(jnp.searchsorted(pad_end, tile_start, side='right'), N_EXPERTS - 1).astype(jnp.int32)
    tile_rows = jnp.clip(counts[tile_exp] - (tile_start - pad_start[tile_exp]), 0, MOE_TILE)
    tile_rows = jnp.where(tile_used, tile_rows, 0).astype(jnp.int32)
    last_exp = tile_exp[jnp.maximum(jnp.sum(tile_used.astype(jnp.int32)) - 1, 0)]
    tile_src = jnp.where(tile_used, start[tile_exp] + (tile_start - pad_start[tile_exp]), 0)
    tile_exp = jnp.where(tile_used, tile_exp, last_exp)
    ys = swiglu(hn2_packed, wg, wu, wd, tile_exp, tile_rows, tile_src.astype(jnp.int32), sorted_tok,
                tf=MOE_F_TILE, sub_rows=FFN_SUB_ROWS, name="swiglu_experts")
    return ys, dest.reshape(n, TOP_K), gates


def kernel(x, p, ln_mix, ln_ffn, ln_ple, final_norm, t5_bias, w_attn_in, w_attn_out, attn_sink, na_rpb, w_ffn_gate, w_ffn_up, w_ffn_down, w_hy_in, b_hy_in, w_hy_conv, b_hy_conv, w_hy_f1, b_hy_f1, w_hy_f2, b_hy_f2, hy_freq, w_hy_f3, hy_bias, w_hy_out, w_router, w_exp_gate, w_exp_up, w_exp_down, w_ple_proj, w_ple_gate):
    batch, seq, d = x.shape
    n = batch * seq
    depth = ln_mix.shape[0]
    h = x.reshape(n, d)
    for i in range(depth):
        li = i // 2
        hn = rmsnorm(h, ln_mix[i], BF16)
        if i % 2 == 0:
            na_off = A_WIDTH + 2 * A_KV_WIDTH
            w_in = w_attn_in[li]
            proj_a = matmul([hn], w_in[:, :na_off], out_dtype=BF16, tn=na_off, name="attn_in_a")
            proj_n = matmul([hn], w_in[:, na_off:], out_dtype=BF16, tn=3 * B_WIDTH // 2, name="attn_in_n")
            oa = window_attention(proj_a.reshape(batch, seq, -1), t5_bias, attn_sink[li])
            ob = neighbourhood_attention(proj_n.reshape(batch, seq, -1), na_rpb[li])
            h = matmul([oa.reshape(n, A_WIDTH), ob.reshape(n, B_WIDTH)], w_attn_out[li], res=h,
                       name="attn_out")
            nt = n // MOE_TILE
            h = swiglu(h, w_ffn_gate, w_ffn_up, w_ffn_down,
                       jnp.full((nt,), li, jnp.int32), jnp.full((nt,), MOE_TILE, jnp.int32),
                       gain=ln_ffn[i], name="swiglu_dense")
        else:
            h = hyena_mixer(hn, h, w_hy_in[li], b_hy_in[li], w_hy_conv[li], b_hy_conv[li],
                            w_hy_f1[li], b_hy_f1[li], w_hy_f2[li], b_hy_f2[li], hy_freq[li],
                            w_hy_f3[li], hy_bias[li], w_hy_out[li], batch, seq)
            wr_pad = jnp.pad(w_router[li].astype(F32), ((0, 0), (0, 128 - N_EXPERTS)))
            hn2_packed, logits = rmsnorm_router(h, ln_ffn[i], wr_pad)
            ys, dest2, gates = moe_swiglu(hn2_packed, logits[:, :N_EXPERTS], w_exp_gate[li],
                                          w_exp_up[li], w_exp_down[li])
            h = moe_combine(h, ys, dest2, gates)
        h = ple(h, p.reshape(depth, n, -1), ln_ple[i], w_ple_gate, w_ple_proj, i)
    return rmsnorm(h, final_norm, F32).reshape(batch, seq, d)
```

```python
import functools
import math

import jax
import jax.numpy as jnp
import numpy as np
from jax import lax
from jax.experimental import pallas as pl
from jax.experimental.pallas import tpu as pltpu

F32 = jnp.float32
BF16 = jnp.bfloat16
NEG_INF = -1e30
RMS_EPS = 1e-6

V7X_VMEM_LIMIT_BYTES = 56 * 1024 * 1024
V7X_LANES = 128

HEAD_DIM = 64
A_Q_HEADS = 16
A_KV_HEADS = 2
A_GROUP = A_Q_HEADS // A_KV_HEADS
A_BLOCK = 128
T5_BUCKETS = 32
T5_MAX_DIST = 128
B_HEADS = 16
GRID_W = 64
NA_WIN_H = 8
NA_WIN_W = 16
A_WIDTH = A_Q_HEADS * HEAD_DIM
A_KV_WIDTH = A_KV_HEADS * HEAD_DIM
B_WIDTH = B_HEADS * HEAD_DIM
N_EXPERTS = 8
TOP_K = 2
HYENA_EMB = 33
HYENA_DECAY_TARGET = 1e-2
HYENA_FAST_PCT = 0.3
HYENA_SLOW_PCT = 1.5

FFT_N1 = 64
FFT_N2 = 128
FFT_CHUNK = 8
FFT_CB = 1024

MOE_TILE = 1024
FFN_F_TILE = 512
MOE_F_TILE = 512
FFN_SUB_ROWS = 256
FFN_OUT_CHUNKS = 4
DMA_LOOP_UNROLL = 8


def _params(semantics, big=False):
    return pltpu.CompilerParams(
        dimension_semantics=semantics,
        vmem_limit_bytes=V7X_VMEM_LIMIT_BYTES if big else None)


def _rmsnorm_kernel(x_ref, g_ref, o_ref):
    x = x_ref[...]
    y = x * lax.rsqrt(jnp.mean(x * x, axis=-1, keepdims=True) + RMS_EPS)
    o_ref[...] = (y * g_ref[...]).astype(o_ref.dtype)


def rmsnorm(x2, g, out_dtype, tm=1024):
    n, d = x2.shape
    return pl.pallas_call(
        _rmsnorm_kernel,
        out_shape=jax.ShapeDtypeStruct((n, d), out_dtype),
        grid=(n // tm,),
        in_specs=[pl.BlockSpec((tm, d), lambda i: (i, 0)),
                  pl.BlockSpec((1, d), lambda i: (0, 0))],
        out_specs=pl.BlockSpec((tm, d), lambda i: (i, 0)),
        compiler_params=_params(("parallel",), big=True),
        name="rmsnorm",
    )(x2, g.reshape(1, d))


def _pack_bf16_pairs(y):
    w = y.shape[1] // 2
    bits = pltpu.bitcast(y.astype(BF16).astype(F32), jnp.uint32)
    return (bits[:, :w] >> 16) | (bits[:, w:] & jnp.uint32(0xFFFF0000))


def _unpack_pairs_f32(p):
    lo = pltpu.bitcast(p << 16, F32)
    hi = pltpu.bitcast(p & jnp.uint32(0xFFFF0000), F32)
    return jnp.concatenate([lo, hi], axis=1)


def _unpack_bf16_pairs(p):
    return _unpack_pairs_f32(p).astype(BF16)


def _rmsnorm_router_kernel(x_ref, g_ref, wr_ref, o_ref, l_ref):
    x = x_ref[...]
    y = x * lax.rsqrt(jnp.mean(x * x, axis=-1, keepdims=True) + RMS_EPS)
    y = y * g_ref[...]
    o_ref[...] = _pack_bf16_pairs(y)
    w = wr_ref[...]
    y_hi, w_hi = y.astype(BF16), w.astype(BF16)
    y_lo = (y - y_hi.astype(F32)).astype(BF16)
    w_lo = (w - w_hi.astype(F32)).astype(BF16)
    l_ref[...] = (jnp.dot(y_hi, w_hi, preferred_element_type=F32)
                  + jnp.dot(y_lo, w_hi, preferred_element_type=F32)
                  + jnp.dot(y_hi, w_lo, preferred_element_type=F32))


def rmsnorm_router(x2, g, w_router_pad, tm=512):
    n, d = x2.shape
    ne = w_router_pad.shape[1]
    return pl.pallas_call(
        _rmsnorm_router_kernel,
        out_shape=(jax.ShapeDtypeStruct((n, d // 2), jnp.uint32), jax.ShapeDtypeStruct((n, ne), F32)),
        grid=(n // tm,),
        in_specs=[pl.BlockSpec((tm, d), lambda i: (i, 0)),
                  pl.BlockSpec((1, d), lambda i: (0, 0)),
                  pl.BlockSpec((d, ne), lambda i: (0, 0))],
        out_specs=(pl.BlockSpec((tm, d // 2), lambda i: (i, 0)),
                   pl.BlockSpec((tm, ne), lambda i: (i, 0))),
        compiler_params=_params(("parallel",)),
        name="rmsnorm_router",
    )(x2, g.reshape(1, d), w_router_pad)


def _mm_kernel(*refs, n_a, has_res):
    a_refs = refs[:n_a]
    w_refs = refs[n_a:2 * n_a]
    idx = 2 * n_a
    res_ref = refs[idx] if has_res else None
    idx += int(has_res)
    o_ref = refs[idx]
    wbf_refs = refs[idx + 1:idx + 1 + n_a]

    @pl.when(pl.program_id(1) == 0)
    def _():
        for w_ref, wbf_ref in zip(w_refs, wbf_refs):
            wbf_ref[...] = w_ref[...].astype(BF16)

    acc = None
    for a_ref, wbf_ref in zip(a_refs, wbf_refs):
        d = jnp.dot(a_ref[...].astype(BF16), wbf_ref[...], preferred_element_type=F32)
        acc = d if acc is None else acc + d
    if has_res:
        acc = acc + res_ref[...]
    o_ref[...] = acc.astype(o_ref.dtype)


def matmul(a_list, w, *, res=None, out_dtype=F32, tm=1024, tn=1024, name="matmul"):
    m = a_list[0].shape[0]
    n = w.shape[1]
    n_a = len(a_list)
    k_each = a_list[0].shape[1]
    assert all(a.shape == (m, k_each) for a in a_list) and w.shape[0] == n_a * k_each
    in_specs = [pl.BlockSpec((tm, k_each), lambda j, i: (i, 0)) for _ in a_list]
    in_specs += [pl.BlockSpec((k_each, tn), functools.partial(lambda j, i, kb: (kb, j), kb=kb))
                 for kb in range(n_a)]
    args = list(a_list) + [w] * n_a
    if res is not None:
        in_specs.append(pl.BlockSpec((tm, tn), lambda j, i: (i, j)))
        args.append(res)
    return pl.pallas_call(
        functools.partial(_mm_kernel, n_a=n_a, has_res=res is not None),
        out_shape=jax.ShapeDtypeStruct((m, n), out_dtype),
        grid=(n // tn, m // tm),
        in_specs=in_specs,
        out_specs=pl.BlockSpec((tm, tn), lambda j, i: (i, j)),
        scratch_shapes=[pltpu.VMEM((k_each, tn), BF16) for _ in a_list],
        compiler_params=_params(("parallel", "arbitrary"), big=True),
        name=name,
    )(*args)


def _rms_scale_bf16(x, g):
    y = x * lax.rsqrt(jnp.mean(x * x, axis=-1, keepdims=True) + RMS_EPS)
    return (y * g).astype(BF16)


def _ple_kernel(h_ref, p_ref, g_ref, wg_ref, wp_ref, o_ref, wg_bf, wp_bf):
    @pl.when(pl.program_id(1) == 0)
    def _():
        wg_bf[...] = wg_ref[...].astype(BF16)
        wp_bf[...] = wp_ref[...].astype(BF16)

    tn = o_ref.shape[1]
    a = jnp.dot(_rms_scale_bf16(h_ref[...], g_ref[...]), wg_bf[...], preferred_element_type=F32)
    pp = jnp.dot(p_ref[...].astype(BF16), wp_bf[...], preferred_element_type=F32)
    col = pl.multiple_of(pl.program_id(0) * tn, tn)
    o_ref[...] = h_ref[:, pl.ds(col, tn)] + jax.nn.sigmoid(a) * pp


def ple(h2, p_all, g, w_gate_all, w_proj_all, layer, tm=512, tn=1024):
    m, d = h2.shape
    pd = p_all.shape[2]
    return pl.pallas_call(
        _ple_kernel,
        out_shape=jax.ShapeDtypeStruct((m, d), F32),
        grid=(d // tn, m // tm),
        in_specs=[pl.BlockSpec((tm, d), lambda j, i: (i, 0)),
                  pl.BlockSpec((None, tm, pd), lambda j, i: (layer, i, 0)),
                  pl.BlockSpec((1, d), lambda j, i: (0, 0)),
                  pl.BlockSpec((None, d, tn), lambda j, i: (layer, 0, j)),
                  pl.BlockSpec((None, pd, tn), lambda j, i: (layer, 0, j))],
        out_specs=pl.BlockSpec((tm, tn), lambda j, i: (i, j)),
        scratch_shapes=[pltpu.VMEM((d, tn), BF16), pltpu.VMEM((pd, tn), BF16)],
        compiler_params=_params(("parallel", "arbitrary"), big=True),
        name="ple",
    )(h2, p_all, g.reshape(1, d), w_gate_all, w_proj_all)


def _row_copy(src_hbm, row, dst, i, sem):
    return pltpu.make_async_copy(src_hbm.at[pl.ds(row, 1), :], dst.at[pl.ds(i, 1), :], sem)


def _swiglu_kernel(exp_ref, rows_ref, src_ref, tok_ref, x_ref, *rest, n_col, gather, sub_rows, nf):
    del exp_ref
    gain_ref = None if gather else rest[0]
    wg_ref, wu_ref, wd_ref, o_ref = rest[0 if gather else 1:][:4]
    scratch = rest[(4 if gather else 5):]
    t = pl.program_id(0)
    f = pl.program_id(1)
    nt = pl.num_programs(0)
    n_rows = rows_ref[t]
    tm = o_ref.shape[0]

    if not gather:
        (xbf,) = scratch

        @pl.when(f == 0)
        def _():
            h = x_ref[...]
            o_ref[...] = h
            xbf[...] = _rms_scale_bf16(h, gain_ref[...])

    if gather:
        @pl.when(f == 0)
        def _():
            o_ref[...] = jnp.zeros_like(o_ref)

        xbuf, xbf, sems = scratch
        slot = t % 2
        n_fetch = xbuf.shape[1]
        per_step = n_fetch // nf

        def start_row(tile, sl, i):
            _row_copy(x_ref, tok_ref[src_ref[tile] + i], xbuf.at[sl], i, sems.at[sl, i]).start()

        def wait_all(sl):
            def body(c, carry):
                for r in range(DMA_LOOP_UNROLL):
                    i = c * DMA_LOOP_UNROLL + r
                    _row_copy(x_ref, 0, xbuf.at[sl], i, sems.at[sl, i]).wait()
                return carry
            lax.fori_loop(0, n_fetch // DMA_LOOP_UNROLL, body, 0)

        @pl.when(jnp.logical_and(f == 0, t == 0))
        def _():
            def body(i, carry):
                start_row(0, 0, i)
                return carry
            lax.fori_loop(0, n_fetch, body, 0, unroll=DMA_LOOP_UNROLL)

        @pl.when(f == 0)
        def _():
            wait_all(slot)
            xbf[...] = _unpack_bf16_pairs(xbuf[slot, 0:tm])

        def fetch_ahead():
            nxt = jnp.minimum(t + 1, nt - 1)
            for r in range(per_step):
                start_row(nxt, 1 - slot, f * per_step + r)
    else:
        def fetch_ahead():
            pass

    def ffn_rows(n):
        fetch_ahead()
        if n == 0:
            return
        x = xbf[0:n, :]
        g = jnp.dot(x, wg_ref[...].astype(BF16), preferred_element_type=F32)
        u = jnp.dot(x, wu_ref[...].astype(BF16), preferred_element_type=F32)
        hmid = (g * jax.nn.sigmoid(g) * u).astype(BF16)
        wd = wd_ref[...].astype(BF16)
        cw = o_ref.shape[1] // n_col
        for c in range(n_col):
            y = jnp.dot(hmid, wd[:, c * cw:(c + 1) * cw], preferred_element_type=F32)
            o_ref[0:n, c * cw:(c + 1) * cw] += y

    for k in range(0 if gather else 1, tm // sub_rows + 1):
        covers = jnp.logical_and(n_rows > (k - 1) * sub_rows, n_rows <= k * sub_rows)
        pl.when(covers)(functools.partial(ffn_rows, k * sub_rows))

    if gather:
        @pl.when(jnp.logical_and(t == nt - 1, f == nf - 1))
        def _():
            wait_all(1 - slot)


def gather_rows_per_tile(tm, nf):
    per_step = -(-tm // nf)
    while (per_step * nf) % DMA_LOOP_UNROLL:
        per_step += 1
    return per_step * nf


def swiglu(x, w_gate, w_up, w_down, tile_exp, tile_rows, tile_src=None, src_rows=None, *, gain=None,
           tm=MOE_TILE, tf=FFN_F_TILE, sub_rows=MOE_TILE, name="swiglu"):
    gather = src_rows is not None
    assert gather != (gain is not None)
    m = tile_rows.shape[0] * tm
    d = w_gate.shape[1]
    dff = w_gate.shape[2]
    nf = dff // tf
    nt = m // tm
    assert nf >= 2

    def w_in_map(t, f, exp_ref, rows_ref, src_ref, tok_ref):
        return (exp_ref[t], 0, jnp.where(rows_ref[t] > 0, f, nf - 1))

    def w_out_map(t, f, exp_ref, rows_ref, src_ref, tok_ref):
        return (exp_ref[t], jnp.where(rows_ref[t] > 0, f, nf - 1), 0)

    def tile_map(t, f, exp_ref, rows_ref, src_ref, tok_ref):
        return (t, 0)

    w_specs = [pl.BlockSpec((None, d, tf), w_in_map),
               pl.BlockSpec((None, d, tf), w_in_map),
               pl.BlockSpec((None, tf, d), w_out_map)]
    if gather:
        in_specs = [pl.BlockSpec(memory_space=pl.ANY)] + w_specs
        args = [x, w_gate, w_up, w_down]
        n_fetch = gather_rows_per_tile(tm, nf)
        scratch = [pltpu.VMEM((2, n_fetch, d // 2), jnp.uint32), pltpu.VMEM((tm, d), BF16),
                   pltpu.SemaphoreType.DMA((2, n_fetch))]
    else:
        in_specs = [pl.BlockSpec((tm, d), tile_map, pipeline_mode=pl.Buffered(1)),
                    pl.BlockSpec((1, d), lambda t, f, e, r, s, k: (0, 0))] + w_specs
        args = [x, gain.reshape(1, d), w_gate, w_up, w_down]
        scratch = [pltpu.VMEM((tm, d), BF16)]
        tile_src = jnp.zeros((nt,), jnp.int32)
        src_rows = jnp.zeros((1,), jnp.int32)
    grid_spec = pltpu.PrefetchScalarGridSpec(
        num_scalar_prefetch=4,
        grid=(nt, nf),
        in_specs=in_specs,
        out_specs=pl.BlockSpec((tm, d), tile_map, pipeline_mode=pl.Buffered(1)),
        scratch_shapes=scratch,
    )
    return pl.pallas_call(
        functools.partial(_swiglu_kernel, n_col=FFN_OUT_CHUNKS, gather=gather, sub_rows=sub_rows, nf=nf),
        out_shape=jax.ShapeDtypeStruct((m, d), F32),
        grid_spec=grid_spec,
        compiler_params=_params(("arbitrary", "arbitrary"), big=True),
        name=name,
    )(tile_exp, tile_rows, tile_src, src_rows, *args)


COMBINE_TILE = 256


def _combine_kernel(d0_ref, d1_ref, ys_hbm, gate_ref, h_ref, hout_ref, buf, sems):
    i = pl.program_id(0)
    n = pl.num_programs(0)
    tt = h_ref.shape[0]
    slot = i % 2
    dests = (d0_ref, d1_ref)

    def start_tile(tile, sl):
        def body(r, c):
            for k, d_ref in enumerate(dests):
                _row_copy(ys_hbm, d_ref[tile * tt + r], buf.at[sl, k], r, sems.at[sl, k, r]).start()
            return c
        lax.fori_loop(0, tt, body, 0, unroll=DMA_LOOP_UNROLL)

    @pl.when(i == 0)
    def _():
        start_tile(0, 0)

    @pl.when(i + 1 < n)
    def _():
        start_tile(i + 1, 1 - slot)

    def wait_body(r, c):
        for k in range(TOP_K):
            _row_copy(ys_hbm, 0, buf.at[slot, k], r, sems.at[slot, k, r]).wait()
        return c
    lax.fori_loop(0, tt, wait_body, 0, unroll=DMA_LOOP_UNROLL)

    gates = gate_ref[...]
    hout_ref[...] = h_ref[...] + (buf[slot, 0] * gates[:, 0:1] + buf[slot, 1] * gates[:, 1:2])


def moe_combine(h2, ys, dest2, gates, tt=COMBINE_TILE):
    n, d = h2.shape
    grid_spec = pltpu.PrefetchScalarGridSpec(
        num_scalar_prefetch=2,
        grid=(n // tt,),
        in_specs=[pl.BlockSpec(memory_space=pl.ANY),
                  pl.BlockSpec((tt, TOP_K), lambda i, a, b: (i, 0)),
                  pl.BlockSpec((tt, d), lambda i, a, b: (i, 0))],
        out_specs=pl.BlockSpec((tt, d), lambda i, a, b: (i, 0)),
        scratch_shapes=[pltpu.VMEM((2, TOP_K, tt, d), F32), pltpu.SemaphoreType.DMA((2, TOP_K, tt))],
    )
    return pl.pallas_call(
        _combine_kernel,
        out_shape=jax.ShapeDtypeStruct((n, d), F32),
        grid_spec=grid_spec,
        compiler_params=_params(("arbitrary",), big=True),
        name="moe_combine",
    )(dest2[:, 0], dest2[:, 1], ys, gates, h2)


def _t5_bucket(rel):
    half = T5_BUCKETS // 2
    max_exact = half // 2
    n = jnp.abs(rel)
    log_ratio = jnp.log(jnp.maximum(n, 1).astype(F32) / max_exact) / math.log(T5_MAX_DIST / max_exact)
    large = jnp.minimum(max_exact + (log_ratio * (half - max_exact)).astype(jnp.int32), half - 1)
    return jnp.where(rel > 0, half, 0) + jnp.where(n < max_exact, n, large)


def _window_bias_table(t5_bias):
    i = jnp.arange(A_BLOCK)[:, None]
    j = jnp.arange(3 * A_BLOCK)[None, :]
    rel = j - A_BLOCK - i
    onehot = (_t5_bucket(rel)[None] == jnp.arange(T5_BUCKETS)[:, None, None]).astype(F32)
    bias = jnp.einsum('bh,bij->hij', t5_bias.astype(F32), onehot, precision=lax.Precision.HIGHEST)
    return jnp.where((jnp.abs(rel) <= A_BLOCK)[None], bias, NEG_INF)


def _window_kernel(sink_ref, q_ref, kv_ref, bias_ref, o_ref, *, nb):
    n = pl.program_id(1)
    scale = HEAD_DIM ** -0.5
    starts = (jnp.maximum(n - 1, 0), n, jnp.minimum(n + 1, nb - 1))
    kv = [kv_ref[pl.ds(pl.multiple_of(s * A_BLOCK, A_BLOCK), A_BLOCK), :] for s in starts]
    col = lax.broadcasted_iota(jnp.int32, (A_BLOCK, 3 * A_BLOCK), 1)
    edge_ok = jnp.logical_and(jnp.logical_or(n > 0, col >= A_BLOCK),
                              jnp.logical_or(n < nb - 1, col < 2 * A_BLOCK))
    kv = jnp.concatenate(kv, axis=0)
    q = q_ref[...]
    lo = lax.broadcasted_iota(jnp.int32, (1, 2 * HEAD_DIM), 1) < HEAD_DIM
    sel = (jnp.where(lo, scale, 0.0).astype(BF16), jnp.where(lo, 0.0, scale).astype(BF16))
    pairs_per_kv = A_GROUP // 2
    kk, vv = [], []
    for kh in range(A_KV_HEADS):
        k1 = kv[:, kh * HEAD_DIM:(kh + 1) * HEAD_DIM]
        v1 = kv[:, A_KV_WIDTH + kh * HEAD_DIM:A_KV_WIDTH + (kh + 1) * HEAD_DIM]
        kk.append(jnp.concatenate([k1, k1], axis=1))
        vv.append(jnp.concatenate([v1, v1], axis=1))
    scores = []
    for pr in range(A_Q_HEADS // 2):
        q2 = q[:, pr * 2 * HEAD_DIM:(pr + 1) * 2 * HEAD_DIM]
        qq = jnp.concatenate([q2 * sel[0], q2 * sel[1]], axis=0)
        s2 = lax.dot_general(qq, kk[pr // pairs_per_kv], (((1,), (1,)), ((), ())),
                             preferred_element_type=F32)
        for i in range(2):
            scores.append(jnp.where(edge_ok, s2[i * A_BLOCK:(i + 1) * A_BLOCK] + bias_ref[2 * pr + i],
                                    NEG_INF))
    probs, dens = [], []
    for h, s in enumerate(scores):
        sink = sink_ref[h]
        mx = jnp.maximum(jnp.max(s, axis=-1, keepdims=True), sink)
        p = jnp.exp(s - mx)
        dens.append(jnp.sum(p, axis=-1, keepdims=True) + jnp.exp(sink - mx))
        probs.append(p.astype(BF16))
    for pr in range(A_Q_HEADS // 2):
        v2 = vv[pr // pairs_per_kv]
        oa = jnp.dot(probs[2 * pr], v2, preferred_element_type=F32) / dens[2 * pr]
        ob = jnp.dot(probs[2 * pr + 1], v2, preferred_element_type=F32) / dens[2 * pr + 1]
        o_ref[:, pr * 2 * HEAD_DIM:(pr + 1) * 2 * HEAD_DIM] = jnp.where(lo, oa, ob).astype(o_ref.dtype)


def window_attention(proj_a, t5_bias, sink):
    b, s, _ = proj_a.shape
    nb = s // A_BLOCK
    kvw = 2 * A_KV_WIDTH
    grid_spec = pltpu.PrefetchScalarGridSpec(
        num_scalar_prefetch=0,
        grid=(b, nb),
        in_specs=[pl.BlockSpec(memory_space=pltpu.SMEM),
                  pl.BlockSpec((None, A_BLOCK, A_WIDTH), lambda bi, n: (bi, n, 0)),
                  pl.BlockSpec((None, s, kvw), lambda bi, n: (bi, 0, A_WIDTH // kvw)),
                  pl.BlockSpec((A_Q_HEADS, A_BLOCK, 3 * A_BLOCK), lambda bi, n: (0, 0, 0))],
        out_specs=pl.BlockSpec((None, A_BLOCK, A_WIDTH), lambda bi, n: (bi, n, 0)),
    )
    return pl.pallas_call(
        functools.partial(_window_kernel, nb=nb),
        out_shape=jax.ShapeDtypeStruct((b, s, A_WIDTH), BF16),
        grid_spec=grid_spec,
        compiler_params=_params(("parallel", "arbitrary")),
        name="window_attention",
    )(sink.astype(F32), proj_a, proj_a, _window_bias_table(t5_bias))


NA_HEAD_GROUP = 16


def _na_bias_table(rpb, rows):
    kh = min(NA_WIN_H, rows)
    kw = NA_WIN_W
    c = jnp.arange(GRID_W)
    cs = jnp.clip(c - kw // 2, 0, GRID_W - kw)
    col_ok = (c[None] >= cs[:, None]) & (c[None] < cs[:, None] + kw)
    col_off = jnp.clip(c[None] - c[:, None], -(kw - 1), kw - 1) + kw - 1
    hi = lax.Precision.HIGHEST
    row_off = jnp.arange(kh)[None, :] - jnp.arange(kh)[:, None] + NA_WIN_H - 1
    row_sel = (row_off[:, :, None] == jnp.arange(2 * NA_WIN_H - 1)[None, None, :]).astype(F32)
    col_sel = (col_off[None] == jnp.arange(2 * kw - 1)[:, None, None]).astype(F32)
    by_row = jnp.einsum('hrc,dir->dhic', rpb.astype(F32), row_sel, precision=hi)
    bias = jnp.einsum('dhic,cqk->dhqik', by_row, col_sel, precision=hi)
    bias = jnp.where(col_ok[None, None, :, None, :], bias, NEG_INF)
    return bias.reshape(kh, rpb.shape[0], GRID_W, kh * GRID_W)


def _na_kernel(q_ref, k_ref, v_ref, bias_ref, o_ref, *, rows):
    r = pl.program_id(2)
    kh = min(NA_WIN_H, rows)
    scale = HEAD_DIM ** -0.5
    start = pl.multiple_of(jnp.clip(r - kh // 2, 0, rows - kh) * GRID_W, GRID_W)
    k = k_ref[pl.ds(start, kh * GRID_W), :]
    v = v_ref[pl.ds(start, kh * GRID_W), :]
    q = q_ref[...]
    lo = lax.broadcasted_iota(jnp.int32, (1, 2 * HEAD_DIM), 1) < HEAD_DIM
    sel = (jnp.where(lo, scale, 0.0).astype(BF16), jnp.where(lo, 0.0, scale).astype(BF16))
    n_pairs = NA_HEAD_GROUP // 2
    scores = []
    for pr in range(n_pairs):
        sl = slice(pr * 2 * HEAD_DIM, (pr + 1) * 2 * HEAD_DIM)
        q2 = q[:, sl]
        qq = jnp.concatenate([q2 * sel[0], q2 * sel[1]], axis=0)
        s2 = lax.dot_general(qq, k[:, sl], (((1,), (1,)), ((), ())), preferred_element_type=F32)
        for i in range(2):
            scores.append(s2[i * GRID_W:(i + 1) * GRID_W] + bias_ref[2 * pr + i])
    probs, dens = [], []
    for s in scores:
        mx = jnp.max(s, axis=-1, keepdims=True)
        p = jnp.exp(s - mx)
        dens.append(jnp.sum(p, axis=-1, keepdims=True))
        probs.append(p.astype(BF16))
    for pr in range(n_pairs):
        sl = slice(pr * 2 * HEAD_DIM, (pr + 1) * 2 * HEAD_DIM)
        oa = jnp.dot(probs[2 * pr], v[:, sl], preferred_element_type=F32) / dens[2 * pr]
        ob = jnp.dot(probs[2 * pr + 1], v[:, sl], preferred_element_type=F32) / dens[2 * pr + 1]
        o_ref[:, sl] = jnp.where(lo, oa, ob).astype(o_ref.dtype)


def neighbourhood_attention(proj_n, rpb):
    b, s, _ = proj_n.shape
    rows = s // GRID_W
    kh = min(NA_WIN_H, rows)
    gw = NA_HEAD_GROUP * HEAD_DIM
    ng = B_WIDTH // gw

    def bias_map(bi, g, r):
        return (r - jnp.clip(r - kh // 2, 0, rows - kh), g, 0, 0)

    return pl.pallas_call(
        functools.partial(_na_kernel, rows=rows),
        out_shape=jax.ShapeDtypeStruct((b, s, B_WIDTH), BF16),
        grid=(b, ng, rows),
        in_specs=[pl.BlockSpec((None, GRID_W, gw), lambda bi, g, r: (bi, r, g)),
                  pl.BlockSpec((None, s, gw), lambda bi, g, r: (bi, 0, ng + g)),
                  pl.BlockSpec((None, s, gw), lambda bi, g, r: (bi, 0, 2 * ng + g)),
                  pl.BlockSpec((None, NA_HEAD_GROUP, GRID_W, kh * GRID_W), bias_map)],
        out_specs=pl.BlockSpec((None, GRID_W, gw), lambda bi, g, r: (bi, r, g)),
        compiler_params=_params(("parallel", "parallel", "arbitrary"), big=True),
        name="neighbourhood_attention",
    )(proj_n, proj_n, proj_n, _na_bias_table(rpb, rows))


CONV_HALO = 16
CONV_COL_CHUNK = 512


def _hyena_in_kernel(prev_ref, cur_ref, next_ref, w_ref, b_ref, cw_ref, cb_ref, o_ref, wbf_ref, *,
                     tiles_per_seq):
    i = pl.program_id(1)

    @pl.when(i == 0)
    def _():
        wbf_ref[...] = w_ref[...].astype(BF16)

    tm = cur_ref.shape[0]
    rows = tm + 2 * CONV_HALO
    a = jnp.concatenate([prev_ref[...], cur_ref[...], next_ref[...]], axis=0)
    pos = i % tiles_per_seq
    row = lax.broadcasted_iota(jnp.int32, (rows, 1), 0)
    kill_up = jnp.logical_and(pos == 0, row == CONV_HALO)
    kill_dn = jnp.logical_and(pos == tiles_per_seq - 1, row == CONV_HALO + tm - 1)
    for c in range(o_ref.shape[1] // CONV_COL_CHUNK):
        cs = slice(c * CONV_COL_CHUNK, (c + 1) * CONV_COL_CHUNK)
        p = jnp.dot(a, wbf_ref[:, cs], preferred_element_type=F32) + b_ref[:, cs]
        up = jnp.where(kill_up, 0.0, pltpu.roll(p, 1, 0))
        dn = jnp.where(kill_dn, 0.0, pltpu.roll(p, rows - 1, 0))
        y = up * cw_ref[0:1, cs] + p * cw_ref[1:2, cs] + dn * cw_ref[2:3, cs] + cb_ref[:, cs]
        o_ref[:, cs] = y[CONV_HALO:CONV_HALO + tm]


def hyena_in_conv(hn, w_in, b_in, conv_w, conv_b, seq, tm=1024, tn=1024):
    m, k = hn.shape
    n = w_in.shape[1]
    hb = tm // CONV_HALO
    last_halo = m // CONV_HALO - 1
    return pl.pallas_call(
        functools.partial(_hyena_in_kernel, tiles_per_seq=seq // tm),
        out_shape=jax.ShapeDtypeStruct((m, n), F32),
        grid=(n // tn, m // tm),
        in_specs=[pl.BlockSpec((CONV_HALO, k), lambda j, i: (jnp.maximum(i * hb - 1, 0), 0)),
                  pl.BlockSpec((tm, k), lambda j, i: (i, 0)),
                  pl.BlockSpec((CONV_HALO, k), lambda j, i: (jnp.minimum((i + 1) * hb, last_halo), 0)),
                  pl.BlockSpec((k, tn), lambda j, i: (0, j)),
                  pl.BlockSpec((1, tn), lambda j, i: (0, j)),
                  pl.BlockSpec((3, tn), lambda j, i: (0, j)),
                  pl.BlockSpec((1, tn), lambda j, i: (0, j))],
        out_specs=pl.BlockSpec((tm, tn), lambda j, i: (i, j)),
        scratch_shapes=[pltpu.VMEM((k, tn), BF16)],
        compiler_params=_params(("parallel", "arbitrary"), big=True),
        name="hyena_in_conv",
    )(hn, hn, hn, w_in, b_in.reshape(1, n), conv_w, conv_b.reshape(1, n))


def _filter_stage1_kernel(z_ref, t_ref, fw1_ref, fb1_ref, fw2_ref, fb2_ref, fr_ref, w3f0_ref, w3f1_ref,
                          w3b0_ref, w3b1_ref, delta_ref, f1_ref, y_ref, hid_ref, *, seq):
    j = pl.program_id(0)

    @pl.when(pl.program_id(1) == 0)
    def _():
        hi = lax.Precision.HIGHEST
        fr = fr_ref[...]
        hid = jnp.sin(fr * (jnp.dot(z_ref[...], fw1_ref[...], preferred_element_type=F32, precision=hi)
                            + fb1_ref[...]))
        hid = jnp.sin(fr * (jnp.dot(hid, fw2_ref[...], preferred_element_type=F32, precision=hi)
                            + fb2_ref[...]))
        hid_ref[...] = hid.astype(BF16)

    hid_bf = hid_ref[...]
    r = lax.broadcasted_iota(jnp.int32, (hid_bf.shape[0], 1), 0)
    u = j * FFT_CHUNK + r // FFT_N2 + FFT_N1 * (r % FFT_N2)
    decay = jnp.where(u == seq, 0.0, jnp.exp(-t_ref[...] * delta_ref[...]))
    for o, (wf_ref, wb_ref) in enumerate(((w3f0_ref, w3b0_ref), (w3f1_ref, w3b1_ref))):
        fwd = jnp.dot(hid_bf, wf_ref[...].astype(BF16), preferred_element_type=F32)
        bwd = jnp.dot(hid_bf, wb_ref[...].astype(BF16), preferred_element_type=F32)
        taps = (jnp.where(u > seq, bwd, jnp.where(u == 0, fwd + bwd, fwd)) * decay).astype(BF16)
        for jj in range(FFT_CHUNK):
            x = taps[jj * FFT_N2:(jj + 1) * FFT_N2]
            y_ref[o, jj] = _pack_bf16_pairs(jnp.dot(f1_ref[jj], x, preferred_element_type=F32))


def hyena_filter_stage1(seq, fw1, fb1, fw2, fb2, freq, fw3, width, consts, cb=FFT_CB):
    _, f1_real, _, _, _ = consts
    n = 2 * seq
    n1, n2, ch = FFT_N1, FFT_N2, FFT_CHUNK
    assert fw3.shape[1] == 4 * width
    r = jnp.arange(n)
    u = (r // (ch * n2)) * ch + (r // n2) % ch + n1 * (r % n2)
    pos = jnp.where(u <= seq, u, n - u)
    pos = jnp.where(u == seq, 0, pos)
    bands = (HYENA_EMB - 1) // 2
    f = jnp.linspace(1e-4, bands - 1, bands, dtype=F32)[None]
    t = (pos.astype(F32) / (seq - 1))[:, None]
    w = ((2.0 * math.pi / seq) * pos.astype(F32))[:, None]
    z = jnp.concatenate([t, jnp.cos(f * w), -jnp.sin(f * w)], axis=-1)
    emb_pad = 40
    z = jnp.pad(z, ((0, 0), (0, emb_pad - HYENA_EMB)))
    fw1p = jnp.pad(fw1.astype(F32), ((0, emb_pad - HYENA_EMB), (0, 0)))
    fh = fw1.shape[1]
    max_decay = math.log(HYENA_DECAY_TARGET) / HYENA_FAST_PCT
    min_decay = math.log(HYENA_DECAY_TARGET) / HYENA_SLOW_PCT
    deltas = jnp.abs(jnp.linspace(min_decay, max_decay, width, dtype=F32))[None]
    ncb = width // cb
    rows = ch * n2

    def small(shape):
        return pl.BlockSpec(shape, lambda j, k: (0, 0))

    def w3_spec(blk):
        return pl.BlockSpec((fh, cb), lambda j, k: (0, blk * ncb + k))

    return pl.pallas_call(
        functools.partial(_filter_stage1_kernel, seq=seq),
        out_shape=jax.ShapeDtypeStruct((2, n1, 2 * n2, width // 2), jnp.uint32),
        grid=(n1 // ch, ncb),
        in_specs=[pl.BlockSpec((rows, emb_pad), lambda j, k: (j, 0)),
                  pl.BlockSpec((rows, 1), lambda j, k: (j, 0)),
                  small((emb_pad, fh)), small((1, fh)), small((fh, fh)), small((1, fh)), small((1, fh)),
                  w3_spec(0), w3_spec(1), w3_spec(2), w3_spec(3),
                  pl.BlockSpec((1, cb), lambda j, k: (0, k)),
                  pl.BlockSpec((ch, 2 * n2, n2), lambda j, k: (j, 0, 0))],
        out_specs=pl.BlockSpec((2, ch, 2 * n2, cb // 2), lambda j, k: (0, j, 0, k)),
        scratch_shapes=[pltpu.VMEM((rows, fh), BF16)],
        compiler_params=_params(("parallel", "arbitrary"), big=True),
        name="hyena_filter_stage1",
    )(z, t, fw1p, fb1.reshape(1, fh).astype(F32), fw2.astype(F32), fb2.reshape(1, fh).astype(F32),
      freq.reshape(1, fh).astype(F32), fw3, fw3, fw3, fw3, deltas, f1_real)


def _dft_constants():
    n1, n2 = FFT_N1, FFT_N2
    n = n1 * n2
    a1 = np.arange(n1)
    a2 = np.arange(n2)
    half = n2 // 2

    def cplx_block(w):
        return np.block([[w.real, -w.imag], [w.imag, w.real]])

    ang = (a2[None, :, None] * a2[None, None, :] / n2) + (a1[:, None, None] * a2[None, :, None] / n)
    w1 = np.exp(-2j * np.pi * ang)
    f1_pad = np.stack([cplx_block(w1[i][:, :half]) for i in range(n1)])
    f1_real = np.stack([np.concatenate([w1[i].real, w1[i].imag], axis=0) for i in range(n1)])
    w2 = np.exp(-2j * np.pi * (a1[:, None] * a1[None, :]) / n1)
    f2 = cplx_block(w2)
    ang = (a1[None, :, None] * a1[None, None, :] / n1) + (a2[:, None, None] * a1[None, :, None] / n)
    g2 = np.stack([cplx_block(m_) for m_ in np.exp(2j * np.pi * ang)])
    wg1 = np.exp(2j * np.pi * (a2[:half, None] * a2[None, :]) / n2) / n
    g1 = cplx_block(wg1)
    to = lambda x: jnp.asarray(x.astype(np.float32)).astype(BF16)
    return to(f1_pad), to(f1_real), to(f2), to(g2), to(g1)


def _fft_stage1_block(x_re, x_im, f_ref, y_ref):
    xr = pltpu.einshape("mjc->jmc", x_re)
    xi = pltpu.einshape("mjc->jmc", x_im)
    for j in range(FFT_CHUNK):
        xc = jnp.concatenate([xr[j], xi[j]], axis=0).astype(BF16)
        y_ref[j] = _pack_bf16_pairs(jnp.dot(f_ref[j], xc, preferred_element_type=F32))


def _fft_stage1_kernel(x_ref, f_ref, y_ref):
    _fft_stage1_block(x_ref[0], x_ref[1], f_ref, y_ref)


def _fft_stage2_kernel(yr_ref, yi_ref, f2_ref, h_ref):
    yr = pltpu.einshape("nkc->knc", yr_ref[...])
    yi = pltpu.einshape("nkc->knc", yi_ref[...])
    for j in range(FFT_CHUNK):
        yc = _unpack_bf16_pairs(jnp.concatenate([yr[j], yi[j]], axis=0))
        h_ref[j] = _pack_bf16_pairs(jnp.dot(f2_ref[...], yc, preferred_element_type=F32))


def _fft_mid_kernel(yr_ref, yi_ref, f2_ref, h_ref, g2_ref, q_ref):
    yr = pltpu.einshape("nkc->knc", yr_ref[...])
    yi = pltpu.einshape("nkc->knc", yi_ref[...])
    n1 = FFT_N1
    spec = [jnp.dot(f2_ref[...], _unpack_bf16_pairs(jnp.concatenate([yr[j], yi[j]], axis=0)),
                    preferred_element_type=F32) for j in range(FFT_CHUNK)]
    prod = []
    for j, z in enumerate(spec):
        hf = _unpack_pairs_f32(h_ref[j])
        zr, zi, hr, hi = z[:n1], z[n1:], hf[:n1], hf[n1:]
        prod.append(jnp.concatenate([zr * hr - zi * hi, zr * hi + zi * hr], axis=0).astype(BF16))
    for j, pc in enumerate(prod):
        q_ref[j] = _pack_bf16_pairs(jnp.dot(g2_ref[j], pc, preferred_element_type=F32))


def _fft_last_kernel(qr_ref, qi_ref, g1_ref, gate_ref, zf_ref, fb_ref, *rest, feeds_next):
    f1_ref, o_ref, y_ref = rest if feeds_next else (None, rest[0], None)
    qr = pltpu.einshape("kjc->jkc", qr_ref[...])
    qi = pltpu.einshape("kjc->jkc", qi_ref[...])
    half = FFT_N2 // 2
    conv = []
    for j in range(FFT_CHUNK):
        qc = _unpack_bf16_pairs(jnp.concatenate([qr[j], qi[j]], axis=0))
        conv.append(jnp.dot(g1_ref[...], qc, preferred_element_type=F32))
    conv = jnp.stack(conv, axis=0)
    fb = fb_ref[...]
    out = []
    for bi in range(2):
        cb = pltpu.einshape("jmc->mjc", conv[:, bi * half:(bi + 1) * half])
        out.append(gate_ref[bi] * (cb + zf_ref[bi] * fb))
        o_ref[bi] = out[bi]
    if feeds_next:
        _fft_stage1_block(out[0], out[1], f1_ref, y_ref)


def hyena_filter_spectrum(y, consts, cb=FFT_CB):
    _, _, f2, _, _ = consts
    no = y.shape[0]
    c = 2 * y.shape[-1]
    n1, n2, ch = FFT_N1, FFT_N2, FFT_CHUNK
    nk = n2 // ch
    return pl.pallas_call(
        _fft_stage2_kernel,
        out_shape=jax.ShapeDtypeStruct((no, n2, 2 * n1, c // 2), jnp.uint32),
        grid=(no, nk, c // cb),
        in_specs=[pl.BlockSpec((None, n1, ch, cb // 2), lambda o, j, k: (o, 0, j, k)),
                  pl.BlockSpec((None, n1, ch, cb // 2), lambda o, j, k: (o, 0, nk + j, k)),
                  pl.BlockSpec((2 * n1, 2 * n1), lambda o, j, k: (0, 0))],
        out_specs=pl.BlockSpec((None, ch, 2 * n1, cb // 2), lambda o, j, k: (o, j, 0, k)),
        compiler_params=_params(("parallel", "parallel", "parallel"), big=True),
        name="fft_filter_stage2",
    )(y, y, f2)


def hyena_long_conv_gate(zsrc, z_col, gate_src, gate_col, hf, order, fbias, consts, y=None,
                         feeds_next=False, cb=FFT_CB):
    f1_pad, _, f2, g2, g1 = consts
    n1, n2, ch = FFT_N1, FFT_N2, FFT_CHUNK
    c = 2 * hf.shape[-1]
    ncb = c // cb
    half = n2 // 2
    y_shape = jax.ShapeDtypeStruct((n1, 2 * n2, c // 2), jnp.uint32)
    f1_spec = pl.BlockSpec((ch, 2 * n2, n2), lambda j, k: (j, 0, 0))
    y_spec = pl.BlockSpec((ch, 2 * n2, cb // 2), lambda j, k: (j, 0, k))
    if y is None:
        y = pl.pallas_call(
            _fft_stage1_kernel,
            out_shape=y_shape,
            grid=(n1 // ch, ncb),
            in_specs=[pl.BlockSpec((2, half, ch, cb), lambda j, k: (0, 0, j, z_col * ncb + k)), f1_spec],
            out_specs=y_spec,
            compiler_params=_params(("parallel", "parallel"), big=True),
            name="fft_stage1",
        )(zsrc, f1_pad)
    nk = n2 // ch
    q = pl.pallas_call(
        _fft_mid_kernel,
        out_shape=jax.ShapeDtypeStruct((n2, 2 * n1, c // 2), jnp.uint32),
        grid=(nk, ncb),
        in_specs=[pl.BlockSpec((n1, ch, cb // 2), lambda j, k: (0, j, k)),
                  pl.BlockSpec((n1, ch, cb // 2), lambda j, k: (0, nk + j, k)),
                  pl.BlockSpec((2 * n1, 2 * n1), lambda j, k: (0, 0)),
                  pl.BlockSpec((None, ch, 2 * n1, cb // 2), lambda j, k: (order, j, 0, k)),
                  pl.BlockSpec((ch, 2 * n1, 2 * n1), lambda j, k: (j, 0, 0))],
        out_specs=pl.BlockSpec((ch, 2 * n1, cb // 2), lambda j, k: (j, 0, k)),
        compiler_params=_params(("parallel", "parallel"), big=True),
        name="fft_mid",
    )(y, y, f2, hf, g2)
    nj = n1 // ch
    in_specs = [pl.BlockSpec((n2, ch, cb // 2), lambda j, k: (0, j, k)),
                pl.BlockSpec((n2, ch, cb // 2), lambda j, k: (0, nj + j, k)),
                pl.BlockSpec((2 * half, 2 * n2), lambda j, k: (0, 0)),
                pl.BlockSpec((2, half, ch, cb), lambda j, k: (0, 0, j, gate_col * ncb + k)),
                pl.BlockSpec((2, half, ch, cb), lambda j, k: (0, 0, j, z_col * ncb + k)),
                pl.BlockSpec((None, 1, cb), lambda j, k: (order, 0, k))]
    args = [q, q, g1, gate_src, zsrc, fbias.reshape(fbias.shape[0], 1, c)]
    out_shape = jax.ShapeDtypeStruct((2, half, n1, c), F32)
    out_spec = pl.BlockSpec((2, half, ch, cb), lambda j, k: (0, 0, j, k))
    if feeds_next:
        in_specs.append(f1_spec)
        args.append(f1_pad)
        out_shape, out_spec = (out_shape, y_shape), (out_spec, y_spec)
    res = pl.pallas_call(
        functools.partial(_fft_last_kernel, feeds_next=feeds_next),
        out_shape=out_shape,
        grid=(nj, ncb),
        in_specs=in_specs,
        out_specs=out_spec,
        compiler_params=_params(("parallel", "parallel"), big=True),
        name="fft_last",
    )(*args)
    return res if feeds_next else (res, None)


def hyena_mixer(hn, h_res, w_in, b_in, conv_w, conv_b, fw1, fb1, fw2, fb2, freq, fw3, fbias, w_out,
                batch, seq):
    width = w_out.shape[0]
    assert batch == 2 and 2 * seq == FFT_N1 * FFT_N2
    sc = hyena_in_conv(hn, w_in, b_in, conv_w, conv_b, seq)
    consts = _dft_constants()
    hf = hyena_filter_spectrum(hyena_filter_stage1(seq, fw1, fb1, fw2, fb2, freq, fw3, width, consts),
                               consts)
    sc4 = sc.reshape(batch, FFT_N2 // 2, FFT_N1, 3 * width)
    zf1, y1 = hyena_long_conv_gate(sc4, 2, sc4, 0, hf, 0, fbias, consts, feeds_next=True)
    zf2, _ = hyena_long_conv_gate(zf1, 0, sc4, 1, hf, 1, fbias, consts, y=y1)
    return matmul([zf2.reshape(batch * seq, width)], w_out, res=h_res, tm=512, tn=1024,
                  name="hyena_out")


def moe_swiglu(hn2_packed, logits, wg, wu, wd):
    n = hn2_packed.shape[0]
    top_v, top_i = lax.top_k(logits, TOP_K)
    gates = jax.nn.softmax(top_v, axis=-1)
    e_flat = top_i.reshape(-1).astype(jnp.int32)
    nk = n * TOP_K
    onehot = (e_flat[:, None] == jnp.arange(N_EXPERTS, dtype=jnp.int32)[None]).astype(jnp.int32)
    csum = jnp.cumsum(onehot, axis=0)
    rank = jnp.take_along_axis(csum, e_flat[:, None], axis=1)[:, 0] - 1
    counts = csum[-1]
    padded = ((counts + MOE_TILE - 1) // MOE_TILE) * MOE_TILE
    pad_end = jnp.cumsum(padded)
    pad_start = pad_end - padded
    dest = pad_start[e_flat] + rank
    p_rows = nk + N_EXPERTS * MOE_TILE
    order = jnp.argsort(e_flat, stable=True).astype(jnp.int32)
    nf = wg.shape[2] // MOE_F_TILE
    sorted_tok = jnp.pad(order // TOP_K, (0, gather_rows_per_tile(MOE_TILE, nf)))
    start = jnp.cumsum(counts) - counts
    nt = p_rows // MOE_TILE
    tile_start = jnp.arange(nt, dtype=jnp.int32) * MOE_TILE
    tile_used = tile_start < pad_end[-1]
    tile_exp = jnp.minimum(jnp.searchsorted(pad_end, tile_start, side='right'), N_EXPERTS - 1).astype(jnp.int32)
    tile_rows = jnp.clip(counts[tile_exp] - (tile_start - pad_start[tile_exp]), 0, MOE_TILE)
    tile_rows = jnp.where(tile_used, tile_rows, 0).astype(jnp.int32)
    last_exp = tile_exp[jnp.maximum(jnp.sum(tile_used.astype(jnp.int32)) - 1, 0)]
    tile_src = jnp.where(tile_used, start[tile_exp] + (tile_start - pad_start[tile_exp]), 0)
    tile_exp = jnp.where(tile_used, tile_exp, last_exp)
    ys = swiglu(hn2_packed, wg, wu, wd, tile_exp, tile_rows, tile_src.astype(jnp.int32), sorted_tok,
                tf=MOE_F_TILE, sub_rows=FFN_SUB_ROWS, name="swiglu_experts")
    return ys, dest.reshape(n, TOP_K), gates


def kernel(x, p, ln_mix, ln_ffn, ln_ple, final_norm, t5_bias, w_attn_in, w_attn_out, attn_sink, na_rpb, w_ffn_gate, w_ffn_up, w_ffn_down, w_hy_in, b_hy_in, w_hy_conv, b_hy_conv, w_hy_f1, b_hy_f1, w_hy_f2, b_hy_f2, hy_freq, w_hy_f3, hy_bias, w_hy_out, w_router, w_exp_gate, w_exp_up, w_exp_down, w_ple_proj, w_ple_gate):
    batch, seq, d = x.shape
    n = batch * seq
    depth = ln_mix.shape[0]
    h = x.reshape(n, d)
    for i in range(depth):
        li = i // 2
        hn = rmsnorm(h, ln_mix[i], BF16)
        if i % 2 == 0:
            na_off = A_WIDTH + 2 * A_KV_WIDTH
            w_in = w_attn_in[li]
            proj_a = matmul([hn], w_in[:, :na_off], out_dtype=BF16, tn=na_off, name="attn_in_a")
            proj_n = matmul([hn], w_in[:, na_off:], out_dtype=BF16, tn=3 * B_WIDTH // 2, name="attn_in_n")
            oa = window_attention(proj_a.reshape(batch, seq, -1), t5_bias, attn_sink[li])
            ob = neighbourhood_attention(proj_n.reshape(batch, seq, -1), na_rpb[li])
            h = matmul([oa.reshape(n, A_WIDTH), ob.reshape(n, B_WIDTH)], w_attn_out[li], res=h,
                       name="attn_out")
            nt = n // MOE_TILE
            h = swiglu(h, w_ffn_gate, w_ffn_up, w_ffn_down,
                       jnp.full((nt,), li, jnp.int32), jnp.full((nt,), MOE_TILE, jnp.int32),
                       gain=ln_ffn[i], name="swiglu_dense")
        else:
            h = hyena_mixer(hn, h, w_hy_in[li], b_hy_in[li], w_hy_conv[li], b_hy_conv[li],
                            w_hy_f1[li], b_hy_f1[li], w_hy_f2[li], b_hy_f2[li], hy_freq[li],
                            w_hy_f3[li], hy_bias[li], w_hy_out[li], batch, seq)
            wr_pad = jnp.pad(w_router[li].astype(F32), ((0, 0), (0, V7X_LANES - N_EXPERTS)))
            hn2_packed, logits = rmsnorm_router(h, ln_ffn[i], wr_pad)
            ys, dest2, gates = moe_swiglu(hn2_packed, logits[:, :N_EXPERTS], w_exp_gate[li],
                                          w_exp_up[li], w_exp_down[li])
            h = moe_combine(h, ys, dest2, gates)
        h = ple(h, p.reshape(depth, n, -1), ln_ple[i], w_ple_gate, w_ple_proj, i)
    return rmsnorm(h, final_norm, F32).reshape(batch, seq, d)
```

```python
import functools
import math

import jax
import jax.numpy as jnp
import numpy as np
from jax import lax
from jax.experimental import pallas as pl
from jax.experimental.pallas import tpu as pltpu

F32 = jnp.float32
BF16 = jnp.bfloat16
NEG_INF = -1e30
RMS_EPS = 1e-6

V7X_VMEM_LIMIT_BYTES = 56 * 1024 * 1024
V7X_LANES = 128

HEAD_DIM = 64
A_Q_HEADS = 16
A_KV_HEADS = 2
A_GROUP = A_Q_HEADS // A_KV_HEADS
A_BLOCK = 128
T5_BUCKETS = 32
T5_MAX_DIST = 128
B_HEADS = 16
GRID_W = 64
NA_WIN_H = 8
NA_WIN_W = 16
A_WIDTH = A_Q_HEADS * HEAD_DIM
A_KV_WIDTH = A_KV_HEADS * HEAD_DIM
B_WIDTH = B_HEADS * HEAD_DIM
N_EXPERTS = 8
TOP_K = 2
HYENA_EMB = 33
HYENA_DECAY_TARGET = 1e-2
HYENA_FAST_PCT = 0.3
HYENA_SLOW_PCT = 1.5

FFT_N1 = 64
FFT_N2 = 128
FFT_CHUNK = 8
FFT_CB = 1024

MOE_TILE = 1024
FFN_F_TILE = 512
MOE_F_TILE = 512
FFN_SUB_ROWS = 256
FFN_OUT_CHUNKS = 4
DMA_LOOP_UNROLL = 8


def _params(semantics, big=False):
    return pltpu.CompilerParams(
        dimension_semantics=semantics,
        vmem_limit_bytes=V7X_VMEM_LIMIT_BYTES if big else None)


def _rmsnorm_kernel(x_ref, g_ref, o_ref):
    x = x_ref[...]
    y = x * lax.rsqrt(jnp.mean(x * x, axis=-1, keepdims=True) + RMS_EPS)
    o_ref[...] = (y * g_ref[...]).astype(o_ref.dtype)


def rmsnorm(x2, g, out_dtype, tm=1024):
    n, d = x2.shape
    return pl.pallas_call(
        _rmsnorm_kernel,
        out_shape=jax.ShapeDtypeStruct((n, d), out_dtype),
        grid=(n // tm,),
        in_specs=[pl.BlockSpec((tm, d), lambda i: (i, 0)),
                  pl.BlockSpec((1, d), lambda i: (0, 0))],
        out_specs=pl.BlockSpec((tm, d), lambda i: (i, 0)),
        compiler_params=_params(("parallel",), big=True),
        name="rmsnorm",
    )(x2, g.reshape(1, d))


def _pack_bf16_pairs(y):
    w = y.shape[1] // 2
    bits = pltpu.bitcast(y.astype(BF16).astype(F32), jnp.uint32)
    return (bits[:, :w] >> 16) | (bits[:, w:] & jnp.uint32(0xFFFF0000))


def _unpack_pairs_f32(p):
    lo = pltpu.bitcast(p << 16, F32)
    hi = pltpu.bitcast(p & jnp.uint32(0xFFFF0000), F32)
    return jnp.concatenate([lo, hi], axis=1)


def _unpack_bf16_pairs(p):
    return _unpack_pairs_f32(p).astype(BF16)


def _rmsnorm_router_kernel(x_ref, g_ref, wr_ref, o_ref, l_ref):
    x = x_ref[...]
    y = x * lax.rsqrt(jnp.mean(x * x, axis=-1, keepdims=True) + RMS_EPS)
    y = y * g_ref[...]
    o_ref[...] = _pack_bf16_pairs(y)
    w = wr_ref[...]
    y_hi, w_hi = y.astype(BF16), w.astype(BF16)
    y_lo = (y - y_hi.astype(F32)).astype(BF16)
    w_lo = (w - w_hi.astype(F32)).astype(BF16)
    l_ref[...] = (jnp.dot(y_hi, w_hi, preferred_element_type=F32)
                  + jnp.dot(y_lo, w_hi, preferred_element_type=F32)
                  + jnp.dot(y_hi, w_lo, preferred_element_type=F32))


def rmsnorm_router(x2, g, w_router_pad, tm=512):
    n, d = x2.shape
    ne = w_router_pad.shape[1]
    return pl.pallas_call(
        _rmsnorm_router_kernel,
        out_shape=(jax.ShapeDtypeStruct((n, d // 2), jnp.uint32), jax.ShapeDtypeStruct((n, ne), F32)),
        grid=(n // tm,),
        in_specs=[pl.BlockSpec((tm, d), lambda i: (i, 0)),
                  pl.BlockSpec((1, d), lambda i: (0, 0)),
                  pl.BlockSpec((d, ne), lambda i: (0, 0))],
        out_specs=(pl.BlockSpec((tm, d // 2), lambda i: (i, 0)),
                   pl.BlockSpec((tm, ne), lambda i: (i, 0))),
        compiler_params=_params(("parallel",)),
        name="rmsnorm_router",
    )(x2, g.reshape(1, d), w_router_pad)


def _mm_kernel(*refs, n_a, has_res):
    a_refs = refs[:n_a]
    w_refs = refs[n_a:2 * n_a]
    idx = 2 * n_a
    res_ref = refs[idx] if has_res else None
    idx += int(has_res)
    o_ref = refs[idx]
    wbf_refs = refs[idx + 1:idx + 1 + n_a]

    @pl.when(pl.program_id(1) == 0)
    def _():
        for w_ref, wbf_ref in zip(w_refs, wbf_refs):
            wbf_ref[...] = w_ref[...].astype(BF16)

    acc = None
    for a_ref, wbf_ref in zip(a_refs, wbf_refs):
        d = jnp.dot(a_ref[...].astype(BF16), wbf_ref[...], preferred_element_type=F32)
        acc = d if acc is None else acc + d
    if has_res:
        acc = acc + res_ref[...]
    o_ref[...] = acc.astype(o_ref.dtype)


def matmul(a_list, w, *, res=None, out_dtype=F32, tm=1024, tn=1024, name="matmul"):
    m = a_list[0].shape[0]
    n = w.shape[1]
    n_a = len(a_list)
    k_each = a_list[0].shape[1]
    assert all(a.shape == (m, k_each) for a in a_list) and w.shape[0] == n_a * k_each
    in_specs = [pl.BlockSpec((tm, k_each), lambda j, i: (i, 0)) for _ in a_list]
    in_specs += [pl.BlockSpec((k_each, tn), functools.partial(lambda j, i, kb: (kb, j), kb=kb))
                 for kb in range(n_a)]
    args = list(a_list) + [w] * n_a
    if res is not None:
        in_specs.append(pl.BlockSpec((tm, tn), lambda j, i: (i, j)))
        args.append(res)
    return pl.pallas_call(
        functools.partial(_mm_kernel, n_a=n_a, has_res=res is not None),
        out_shape=jax.ShapeDtypeStruct((m, n), out_dtype),
        grid=(n // tn, m // tm),
        in_specs=in_specs,
        out_specs=pl.BlockSpec((tm, tn), lambda j, i: (i, j)),
        scratch_shapes=[pltpu.VMEM((k_each, tn), BF16) for _ in a_list],
        compiler_params=_params(("parallel", "arbitrary"), big=True),
        name=name,
    )(*args)


def _mm_cols_kernel(a_ref, w_hbm, o_ref, w_stage, wbf_ref, sem, *, col0):
    tn = o_ref.shape[1]

    @pl.when(pl.program_id(1) == 0)
    def _():
        start = pl.multiple_of(col0 + pl.program_id(0) * tn, V7X_LANES)
        copy = pltpu.make_async_copy(w_hbm.at[:, pl.ds(start, tn)], w_stage, sem)
        copy.start()
        copy.wait()
        wbf_ref[...] = w_stage[...].astype(BF16)

    o_ref[...] = jnp.dot(a_ref[...], wbf_ref[...], preferred_element_type=F32).astype(o_ref.dtype)


def matmul_cols(a, w, col0, ncols, *, tn, out_dtype, tm=1024, name):
    m, k = a.shape
    assert col0 % V7X_LANES == 0 and ncols % tn == 0 and tn % V7X_LANES == 0
    return pl.pallas_call(
        functools.partial(_mm_cols_kernel, col0=col0),
        out_shape=jax.ShapeDtypeStruct((m, ncols), out_dtype),
        grid=(ncols // tn, m // tm),
        in_specs=[pl.BlockSpec((tm, k), lambda j, i: (i, 0)),
                  pl.BlockSpec(memory_space=pl.ANY)],
        out_specs=pl.BlockSpec((tm, tn), lambda j, i: (i, j)),
        scratch_shapes=[pltpu.VMEM((k, tn), F32), pltpu.VMEM((k, tn), BF16), pltpu.SemaphoreType.DMA(())],
        compiler_params=_params(("arbitrary", "arbitrary"), big=True),
        name=name,
    )(a, w)


def _rms_scale_bf16(x, g):
    y = x * lax.rsqrt(jnp.mean(x * x, axis=-1, keepdims=True) + RMS_EPS)
    return (y * g).astype(BF16)


def _ple_kernel(h_ref, p_ref, g_ref, wg_ref, wp_ref, o_ref, wg_bf, wp_bf):
    @pl.when(pl.program_id(1) == 0)
    def _():
        wg_bf[...] = wg_ref[...].astype(BF16)
        wp_bf[...] = wp_ref[...].astype(BF16)

    tn = o_ref.shape[1]
    a = jnp.dot(_rms_scale_bf16(h_ref[...], g_ref[...]), wg_bf[...], preferred_element_type=F32)
    pp = jnp.dot(p_ref[...].astype(BF16), wp_bf[...], preferred_element_type=F32)
    col = pl.multiple_of(pl.program_id(0) * tn, tn)
    o_ref[...] = h_ref[:, pl.ds(col, tn)] + jax.nn.sigmoid(a) * pp


def ple(h2, p_all, g, w_gate_all, w_proj_all, layer, tm=512, tn=1024):
    m, d = h2.shape
    pd = p_all.shape[2]
    return pl.pallas_call(
        _ple_kernel,
        out_shape=jax.ShapeDtypeStruct((m, d), F32),
        grid=(d // tn, m // tm),
        in_specs=[pl.BlockSpec((tm, d), lambda j, i: (i, 0)),
                  pl.BlockSpec((None, tm, pd), lambda j, i: (layer, i, 0)),
                  pl.BlockSpec((1, d), lambda j, i: (0, 0)),
                  pl.BlockSpec((None, d, tn), lambda j, i: (layer, 0, j)),
                  pl.BlockSpec((None, pd, tn), lambda j, i: (layer, 0, j))],
        out_specs=pl.BlockSpec((tm, tn), lambda j, i: (i, j)),
        scratch_shapes=[pltpu.VMEM((d, tn), BF16), pltpu.VMEM((pd, tn), BF16)],
        compiler_params=_params(("parallel", "arbitrary"), big=True),
        name="ple",
    )(h2, p_all, g.reshape(1, d), w_gate_all, w_proj_all)


def _row_copy(src_hbm, row, dst, i, sem):
    return pltpu.make_async_copy(src_hbm.at[pl.ds(row, 1), :], dst.at[pl.ds(i, 1), :], sem)


def _swiglu_kernel(exp_ref, rows_ref, src_ref, tok_ref, x_ref, *rest, n_col, gather, sub_rows, nf):
    del exp_ref
    gain_ref = None if gather else rest[0]
    wg_ref, wu_ref, wd_ref, o_ref = rest[0 if gather else 1:][:4]
    scratch = rest[(4 if gather else 5):]
    t = pl.program_id(0)
    f = pl.program_id(1)
    nt = pl.num_programs(0)
    n_rows = rows_ref[t]
    tm = o_ref.shape[0]

    if not gather:
        (xbf,) = scratch

        @pl.when(f == 0)
        def _():
            h = x_ref[...]
            o_ref[...] = h
            xbf[...] = _rms_scale_bf16(h, gain_ref[...])

    if gather:
        @pl.when(f == 0)
        def _():
            o_ref[...] = jnp.zeros_like(o_ref)

        xbuf, xbf, sems = scratch
        slot = t % 2
        n_fetch = xbuf.shape[1]
        per_step = n_fetch // nf

        def start_row(tile, sl, i):
            _row_copy(x_ref, tok_ref[src_ref[tile] + i], xbuf.at[sl], i, sems.at[sl, i]).start()

        def wait_all(sl):
            def body(c, carry):
                for r in range(DMA_LOOP_UNROLL):
                    i = c * DMA_LOOP_UNROLL + r
                    _row_copy(x_ref, 0, xbuf.at[sl], i, sems.at[sl, i]).wait()
                return carry
            lax.fori_loop(0, n_fetch // DMA_LOOP_UNROLL, body, 0)

        @pl.when(jnp.logical_and(f == 0, t == 0))
        def _():
            def body(i, carry):
                start_row(0, 0, i)
                return carry
            lax.fori_loop(0, n_fetch, body, 0, unroll=DMA_LOOP_UNROLL)

        @pl.when(f == 0)
        def _():
            wait_all(slot)
            xbf[...] = _unpack_bf16_pairs(xbuf[slot, 0:tm])

        def fetch_ahead():
            nxt = jnp.minimum(t + 1, nt - 1)
            for r in range(per_step):
                start_row(nxt, 1 - slot, f * per_step + r)
    else:
        def fetch_ahead():
            pass

    def ffn_rows(n):
        fetch_ahead()
        if n == 0:
            return
        x = xbf[0:n, :]
        g = jnp.dot(x, wg_ref[...].astype(BF16), preferred_element_type=F32)
        u = jnp.dot(x, wu_ref[...].astype(BF16), preferred_element_type=F32)
        hmid = (g * jax.nn.sigmoid(g) * u).astype(BF16)
        wd = wd_ref[...].astype(BF16)
        cw = o_ref.shape[1] // n_col
        for c in range(n_col):
            y = jnp.dot(hmid, wd[:, c * cw:(c + 1) * cw], preferred_element_type=F32)
            o_ref[0:n, c * cw:(c + 1) * cw] += y

    for k in range(0 if gather else 1, tm // sub_rows + 1):
        covers = jnp.logical_and(n_rows > (k - 1) * sub_rows, n_rows <= k * sub_rows)
        pl.when(covers)(functools.partial(ffn_rows, k * sub_rows))

    if gather:
        @pl.when(jnp.logical_and(t == nt - 1, f == nf - 1))
        def _():
            wait_all(1 - slot)


def gather_rows_per_tile(tm, nf):
    per_step = -(-tm // nf)
    while (per_step * nf) % DMA_LOOP_UNROLL:
        per_step += 1
    return per_step * nf


def swiglu(x, w_gate, w_up, w_down, tile_exp, tile_rows, tile_src=None, src_rows=None, *, gain=None,
           tm=MOE_TILE, tf=FFN_F_TILE, sub_rows=MOE_TILE, name="swiglu"):
    gather = src_rows is not None
    assert gather != (gain is not None)
    m = tile_rows.shape[0] * tm
    d = w_gate.shape[1]
    dff = w_gate.shape[2]
    nf = dff // tf
    nt = m // tm
    assert nf >= 2

    def w_in_map(t, f, exp_ref, rows_ref, src_ref, tok_ref):
        return (exp_ref[t], 0, jnp.where(rows_ref[t] > 0, f, nf - 1))

    def w_out_map(t, f, exp_ref, rows_ref, src_ref, tok_ref):
        return (exp_ref[t], jnp.where(rows_ref[t] > 0, f, nf - 1), 0)

    def tile_map(t, f, exp_ref, rows_ref, src_ref, tok_ref):
        return (t, 0)

    w_specs = [pl.BlockSpec((None, d, tf), w_in_map),
               pl.BlockSpec((None, d, tf), w_in_map),
               pl.BlockSpec((None, tf, d), w_out_map)]
    if gather:
        in_specs = [pl.BlockSpec(memory_space=pl.ANY)] + w_specs
        args = [x, w_gate, w_up, w_down]
        n_fetch = gather_rows_per_tile(tm, nf)
        scratch = [pltpu.VMEM((2, n_fetch, d // 2), jnp.uint32), pltpu.VMEM((tm, d), BF16),
                   pltpu.SemaphoreType.DMA((2, n_fetch))]
    else:
        in_specs = [pl.BlockSpec((tm, d), tile_map, pipeline_mode=pl.Buffered(1)),
                    pl.BlockSpec((1, d), lambda t, f, e, r, s, k: (0, 0))] + w_specs
        args = [x, gain.reshape(1, d), w_gate, w_up, w_down]
        scratch = [pltpu.VMEM((tm, d), BF16)]
        tile_src = jnp.zeros((nt,), jnp.int32)
        src_rows = jnp.zeros((1,), jnp.int32)
    grid_spec = pltpu.PrefetchScalarGridSpec(
        num_scalar_prefetch=4,
        grid=(nt, nf),
        in_specs=in_specs,
        out_specs=pl.BlockSpec((tm, d), tile_map, pipeline_mode=pl.Buffered(1)),
        scratch_shapes=scratch,
    )
    return pl.pallas_call(
        functools.partial(_swiglu_kernel, n_col=FFN_OUT_CHUNKS, gather=gather, sub_rows=sub_rows, nf=nf),
        out_shape=jax.ShapeDtypeStruct((m, d), F32),
        grid_spec=grid_spec,
        compiler_params=_params(("arbitrary", "arbitrary"), big=True),
        name=name,
    )(tile_exp, tile_rows, tile_src, src_rows, *args)


COMBINE_TILE = 256


def _combine_kernel(d0_ref, d1_ref, ys_hbm, gate_ref, h_ref, hout_ref, buf, sems):
    i = pl.program_id(0)
    n = pl.num_programs(0)
    tt = h_ref.shape[0]
    slot = i % 2
    dests = (d0_ref, d1_ref)

    def start_tile(tile, sl):
        def body(r, c):
            for k, d_ref in enumerate(dests):
                _row_copy(ys_hbm, d_ref[tile * tt + r], buf.at[sl, k], r, sems.at[sl, k, r]).start()
            return c
        lax.fori_loop(0, tt, body, 0, unroll=DMA_LOOP_UNROLL)

    @pl.when(i == 0)
    def _():
        start_tile(0, 0)

    @pl.when(i + 1 < n)
    def _():
        start_tile(i + 1, 1 - slot)

    def wait_body(r, c):
        for k in range(TOP_K):
            _row_copy(ys_hbm, 0, buf.at[slot, k], r, sems.at[slot, k, r]).wait()
        return c
    lax.fori_loop(0, tt, wait_body, 0, unroll=DMA_LOOP_UNROLL)

    gates = gate_ref[...]
    hout_ref[...] = h_ref[...] + (buf[slot, 0] * gates[:, 0:1] + buf[slot, 1] * gates[:, 1:2])


def moe_combine(h2, ys, dest2, gates, tt=COMBINE_TILE):
    n, d = h2.shape
    grid_spec = pltpu.PrefetchScalarGridSpec(
        num_scalar_prefetch=2,
        grid=(n // tt,),
        in_specs=[pl.BlockSpec(memory_space=pl.ANY),
                  pl.BlockSpec((tt, TOP_K), lambda i, a, b: (i, 0)),
                  pl.BlockSpec((tt, d), lambda i, a, b: (i, 0))],
        out_specs=pl.BlockSpec((tt, d), lambda i, a, b: (i, 0)),
        scratch_shapes=[pltpu.VMEM((2, TOP_K, tt, d), F32), pltpu.SemaphoreType.DMA((2, TOP_K, tt))],
    )
    return pl.pallas_call(
        _combine_kernel,
        out_shape=jax.ShapeDtypeStruct((n, d), F32),
        grid_spec=grid_spec,
        compiler_params=_params(("arbitrary",), big=True),
        name="moe_combine",
    )(dest2[:, 0], dest2[:, 1], ys, gates, h2)


def _t5_bucket(rel):
    half = T5_BUCKETS // 2
    max_exact = half // 2
    n = jnp.abs(rel)
    log_ratio = jnp.log(jnp.maximum(n, 1).astype(F32) / max_exact) / math.log(T5_MAX_DIST / max_exact)
    large = jnp.minimum(max_exact + (log_ratio * (half - max_exact)).astype(jnp.int32), half - 1)
    return jnp.where(rel > 0, half, 0) + jnp.where(n < max_exact, n, large)


def _window_bias_table(t5_bias):
    i = jnp.arange(A_BLOCK)[:, None]
    j = jnp.arange(3 * A_BLOCK)[None, :]
    rel = j - A_BLOCK - i
    onehot = (_t5_bucket(rel)[None] == jnp.arange(T5_BUCKETS)[:, None, None]).astype(F32)
    bias = jnp.einsum('bh,bij->hij', t5_bias.astype(F32), onehot, precision=lax.Precision.HIGHEST)
    return jnp.where((jnp.abs(rel) <= A_BLOCK)[None], bias, NEG_INF)


def _window_kernel(sink_ref, q_ref, kv_ref, bias_ref, o_ref, *, nb):
    n = pl.program_id(1)
    scale = HEAD_DIM ** -0.5
    starts = (jnp.maximum(n - 1, 0), n, jnp.minimum(n + 1, nb - 1))
    kv = [kv_ref[pl.ds(pl.multiple_of(s * A_BLOCK, A_BLOCK), A_BLOCK), :] for s in starts]
    col = lax.broadcasted_iota(jnp.int32, (A_BLOCK, 3 * A_BLOCK), 1)
    edge_ok = jnp.logical_and(jnp.logical_or(n > 0, col >= A_BLOCK),
                              jnp.logical_or(n < nb - 1, col < 2 * A_BLOCK))
    kv = jnp.concatenate(kv, axis=0)
    q = q_ref[...]
    lo = lax.broadcasted_iota(jnp.int32, (1, 2 * HEAD_DIM), 1) < HEAD_DIM
    sel = (jnp.where(lo, scale, 0.0).astype(BF16), jnp.where(lo, 0.0, scale).astype(BF16))
    pairs_per_kv = A_GROUP // 2
    kk, vv = [], []
    for kh in range(A_KV_HEADS):
        k1 = kv[:, kh * HEAD_DIM:(kh + 1) * HEAD_DIM]
        v1 = kv[:, A_KV_WIDTH + kh * HEAD_DIM:A_KV_WIDTH + (kh + 1) * HEAD_DIM]
        kk.append(jnp.concatenate([k1, k1], axis=1))
        vv.append(jnp.concatenate([v1, v1], axis=1))
    scores = []
    for pr in range(A_Q_HEADS // 2):
        q2 = q[:, pr * 2 * HEAD_DIM:(pr + 1) * 2 * HEAD_DIM]
        qq = jnp.concatenate([q2 * sel[0], q2 * sel[1]], axis=0)
        s2 = lax.dot_general(qq, kk[pr // pairs_per_kv], (((1,), (1,)), ((), ())),
                             preferred_element_type=F32)
        for i in range(2):
            scores.append(jnp.where(edge_ok, s2[i * A_BLOCK:(i + 1) * A_BLOCK] + bias_ref[2 * pr + i],
                                    NEG_INF))
    probs, dens = [], []
    for h, s in enumerate(scores):
        sink = sink_ref[h]
        mx = jnp.maximum(jnp.max(s, axis=-1, keepdims=True), sink)
        p = jnp.exp(s - mx)
        dens.append(jnp.sum(p, axis=-1, keepdims=True) + jnp.exp(sink - mx))
        probs.append(p.astype(BF16))
    for pr in range(A_Q_HEADS // 2):
        v2 = vv[pr // pairs_per_kv]
        oa = jnp.dot(probs[2 * pr], v2, preferred_element_type=F32) / dens[2 * pr]
        ob = jnp.dot(probs[2 * pr + 1], v2, preferred_element_type=F32) / dens[2 * pr + 1]
        o_ref[:, pr * 2 * HEAD_DIM:(pr + 1) * 2 * HEAD_DIM] = jnp.where(lo, oa, ob).astype(o_ref.dtype)


def window_attention(proj_a, t5_bias, sink):
    b, s, _ = proj_a.shape
    nb = s // A_BLOCK
    kvw = 2 * A_KV_WIDTH
    grid_spec = pltpu.PrefetchScalarGridSpec(
        num_scalar_prefetch=0,
        grid=(b, nb),
        in_specs=[pl.BlockSpec(memory_space=pltpu.SMEM),
                  pl.BlockSpec((None, A_BLOCK, A_WIDTH), lambda bi, n: (bi, n, 0)),
                  pl.BlockSpec((None, s, kvw), lambda bi, n: (bi, 0, A_WIDTH // kvw)),
                  pl.BlockSpec((A_Q_HEADS, A_BLOCK, 3 * A_BLOCK), lambda bi, n: (0, 0, 0))],
        out_specs=pl.BlockSpec((None, A_BLOCK, A_WIDTH), lambda bi, n: (bi, n, 0)),
    )
    return pl.pallas_call(
        functools.partial(_window_kernel, nb=nb),
        out_shape=jax.ShapeDtypeStruct((b, s, A_WIDTH), BF16),
        grid_spec=grid_spec,
        compiler_params=_params(("parallel", "arbitrary")),
        name="window_attention",
    )(sink.astype(F32), proj_a, proj_a, _window_bias_table(t5_bias))


NA_HEAD_GROUP = 16


def _na_bias_table(rpb, rows):
    kh = min(NA_WIN_H, rows)
    kw = NA_WIN_W
    c = jnp.arange(GRID_W)
    cs = jnp.clip(c - kw // 2, 0, GRID_W - kw)
    col_ok = (c[None] >= cs[:, None]) & (c[None] < cs[:, None] + kw)
    col_off = jnp.clip(c[None] - c[:, None], -(kw - 1), kw - 1) + kw - 1
    hi = lax.Precision.HIGHEST
    row_off = jnp.arange(kh)[None, :] - jnp.arange(kh)[:, None] + NA_WIN_H - 1
    row_sel = (row_off[:, :, None] == jnp.arange(2 * NA_WIN_H - 1)[None, None, :]).astype(F32)
    col_sel = (col_off[None] == jnp.arange(2 * kw - 1)[:, None, None]).astype(F32)
    by_row = jnp.einsum('hrc,dir->dhic', rpb.astype(F32), row_sel, precision=hi)
    bias = jnp.einsum('dhic,cqk->dhqik', by_row, col_sel, precision=hi)
    bias = jnp.where(col_ok[None, None, :, None, :], bias, NEG_INF)
    return bias.reshape(kh, rpb.shape[0], GRID_W, kh * GRID_W)


def _na_kernel(q_ref, k_ref, v_ref, bias_ref, o_ref, *, rows):
    r = pl.program_id(2)
    kh = min(NA_WIN_H, rows)
    scale = HEAD_DIM ** -0.5
    start = pl.multiple_of(jnp.clip(r - kh // 2, 0, rows - kh) * GRID_W, GRID_W)
    k = k_ref[pl.ds(start, kh * GRID_W), :]
    v = v_ref[pl.ds(start, kh * GRID_W), :]
    q = q_ref[...]
    lo = lax.broadcasted_iota(jnp.int32, (1, 2 * HEAD_DIM), 1) < HEAD_DIM
    sel = (jnp.where(lo, scale, 0.0).astype(BF16), jnp.where(lo, 0.0, scale).astype(BF16))
    n_pairs = NA_HEAD_GROUP // 2
    scores = []
    for pr in range(n_pairs):
        sl = slice(pr * 2 * HEAD_DIM, (pr + 1) * 2 * HEAD_DIM)
        q2 = q[:, sl]
        qq = jnp.concatenate([q2 * sel[0], q2 * sel[1]], axis=0)
        s2 = lax.dot_general(qq, k[:, sl], (((1,), (1,)), ((), ())), preferred_element_type=F32)
        for i in range(2):
            scores.append(s2[i * GRID_W:(i + 1) * GRID_W] + bias_ref[2 * pr + i])
    probs, dens = [], []
    for s in scores:
        mx = jnp.max(s, axis=-1, keepdims=True)
        p = jnp.exp(s - mx)
        dens.append(jnp.sum(p, axis=-1, keepdims=True))
        probs.append(p.astype(BF16))
    for pr in range(n_pairs):
        sl = slice(pr * 2 * HEAD_DIM, (pr + 1) * 2 * HEAD_DIM)
        oa = jnp.dot(probs[2 * pr], v[:, sl], preferred_element_type=F32) / dens[2 * pr]
        ob = jnp.dot(probs[2 * pr + 1], v[:, sl], preferred_element_type=F32) / dens[2 * pr + 1]
        o_ref[:, sl] = jnp.where(lo, oa, ob).astype(o_ref.dtype)


def neighbourhood_attention(proj_n, rpb):
    b, s, _ = proj_n.shape
    rows = s // GRID_W
    kh = min(NA_WIN_H, rows)
    gw = NA_HEAD_GROUP * HEAD_DIM
    ng = B_WIDTH // gw

    def bias_map(bi, g, r):
        return (r - jnp.clip(r - kh // 2, 0, rows - kh), g, 0, 0)

    return pl.pallas_call(
        functools.partial(_na_kernel, rows=rows),
        out_shape=jax.ShapeDtypeStruct((b, s, B_WIDTH), BF16),
        grid=(b, ng, rows),
        in_specs=[pl.BlockSpec((None, GRID_W, gw), lambda bi, g, r: (bi, r, g)),
                  pl.BlockSpec((None, s, gw), lambda bi, g, r: (bi, 0, ng + g)),
                  pl.BlockSpec((None, s, gw), lambda bi, g, r: (bi, 0, 2 * ng + g)),
                  pl.BlockSpec((None, NA_HEAD_GROUP, GRID_W, kh * GRID_W), bias_map)],
        out_specs=pl.BlockSpec((None, GRID_W, gw), lambda bi, g, r: (bi, r, g)),
        compiler_params=_params(("parallel", "parallel", "arbitrary"), big=True),
        name="neighbourhood_attention",
    )(proj_n, proj_n, proj_n, _na_bias_table(rpb, rows))


CONV_HALO = 16
CONV_COL_CHUNK = 512


def _hyena_in_kernel(prev_ref, cur_ref, next_ref, w_ref, b_ref, cw_ref, cb_ref, o_ref, wbf_ref, *,
                     tiles_per_seq):
    i = pl.program_id(1)

    @pl.when(i == 0)
    def _():
        wbf_ref[...] = w_ref[...].astype(BF16)

    tm = cur_ref.shape[0]
    rows = tm + 2 * CONV_HALO
    a = jnp.concatenate([prev_ref[...], cur_ref[...], next_ref[...]], axis=0)
    pos = i % tiles_per_seq
    row = lax.broadcasted_iota(jnp.int32, (rows, 1), 0)
    kill_up = jnp.logical_and(pos == 0, row == CONV_HALO)
    kill_dn = jnp.logical_and(pos == tiles_per_seq - 1, row == CONV_HALO + tm - 1)
    for c in range(o_ref.shape[1] // CONV_COL_CHUNK):
        cs = slice(c * CONV_COL_CHUNK, (c + 1) * CONV_COL_CHUNK)
        p = jnp.dot(a, wbf_ref[:, cs], preferred_element_type=F32) + b_ref[:, cs]
        up = jnp.where(kill_up, 0.0, pltpu.roll(p, 1, 0))
        dn = jnp.where(kill_dn, 0.0, pltpu.roll(p, rows - 1, 0))
        y = up * cw_ref[0:1, cs] + p * cw_ref[1:2, cs] + dn * cw_ref[2:3, cs] + cb_ref[:, cs]
        o_ref[:, cs] = y[CONV_HALO:CONV_HALO + tm]


def hyena_in_conv(hn, w_in, b_in, conv_w, conv_b, seq, tm=1024, tn=1024):
    m, k = hn.shape
    n = w_in.shape[1]
    hb = tm // CONV_HALO
    last_halo = m // CONV_HALO - 1
    return pl.pallas_call(
        functools.partial(_hyena_in_kernel, tiles_per_seq=seq // tm),
        out_shape=jax.ShapeDtypeStruct((m, n), F32),
        grid=(n // tn, m // tm),
        in_specs=[pl.BlockSpec((CONV_HALO, k), lambda j, i: (jnp.maximum(i * hb - 1, 0), 0)),
                  pl.BlockSpec((tm, k), lambda j, i: (i, 0)),
                  pl.BlockSpec((CONV_HALO, k), lambda j, i: (jnp.minimum((i + 1) * hb, last_halo), 0)),
                  pl.BlockSpec((k, tn), lambda j, i: (0, j)),
                  pl.BlockSpec((1, tn), lambda j, i: (0, j)),
                  pl.BlockSpec((3, tn), lambda j, i: (0, j)),
                  pl.BlockSpec((1, tn), lambda j, i: (0, j))],
        out_specs=pl.BlockSpec((tm, tn), lambda j, i: (i, j)),
        scratch_shapes=[pltpu.VMEM((k, tn), BF16)],
        compiler_params=_params(("parallel", "arbitrary"), big=True),
        name="hyena_in_conv",
    )(hn, hn, hn, w_in, b_in.reshape(1, n), conv_w, conv_b.reshape(1, n))


def _filter_stage1_kernel(z_ref, t_ref, fw1_ref, fb1_ref, fw2_ref, fb2_ref, fr_ref, w3f0_ref, w3f1_ref,
                          w3b0_ref, w3b1_ref, delta_ref, f1_ref, y_ref, hid_ref, *, seq):
    j = pl.program_id(0)

    @pl.when(pl.program_id(1) == 0)
    def _():
        hi = lax.Precision.HIGHEST
        fr = fr_ref[...]
        hid = jnp.sin(fr * (jnp.dot(z_ref[...], fw1_ref[...], preferred_element_type=F32, precision=hi)
                            + fb1_ref[...]))
        hid = jnp.sin(fr * (jnp.dot(hid, fw2_ref[...], preferred_element_type=F32, precision=hi)
                            + fb2_ref[...]))
        hid_ref[...] = hid.astype(BF16)

    hid_bf = hid_ref[...]
    r = lax.broadcasted_iota(jnp.int32, (hid_bf.shape[0], 1), 0)
    u = j * FFT_CHUNK + r // FFT_N2 + FFT_N1 * (r % FFT_N2)
    decay = jnp.where(u == seq, 0.0, jnp.exp(-t_ref[...] * delta_ref[...]))
    for o, (wf_ref, wb_ref) in enumerate(((w3f0_ref, w3b0_ref), (w3f1_ref, w3b1_ref))):
        fwd = jnp.dot(hid_bf, wf_ref[...].astype(BF16), preferred_element_type=F32)
        bwd = jnp.dot(hid_bf, wb_ref[...].astype(BF16), preferred_element_type=F32)
        taps = (jnp.where(u > seq, bwd, jnp.where(u == 0, fwd + bwd, fwd)) * decay).astype(BF16)
        for jj in range(FFT_CHUNK):
            x = taps[jj * FFT_N2:(jj + 1) * FFT_N2]
            y_ref[o, jj] = _pack_bf16_pairs(jnp.dot(f1_ref[jj], x, preferred_element_type=F32))


def hyena_filter_stage1(seq, fw1, fb1, fw2, fb2, freq, fw3, width, consts, cb=FFT_CB):
    _, f1_real, _, _, _ = consts
    n = 2 * seq
    n1, n2, ch = FFT_N1, FFT_N2, FFT_CHUNK
    assert fw3.shape[1] == 4 * width
    r = jnp.arange(n)
    u = (r // (ch * n2)) * ch + (r // n2) % ch + n1 * (r % n2)
    pos = jnp.where(u <= seq, u, n - u)
    pos = jnp.where(u == seq, 0, pos)
    bands = (HYENA_EMB - 1) // 2
    f = jnp.linspace(1e-4, bands - 1, bands, dtype=F32)[None]
    t = (pos.astype(F32) / (seq - 1))[:, None]
    w = ((2.0 * math.pi / seq) * pos.astype(F32))[:, None]
    z = jnp.concatenate([t, jnp.cos(f * w), -jnp.sin(f * w)], axis=-1)
    emb_pad = 40
    z = jnp.pad(z, ((0, 0), (0, emb_pad - HYENA_EMB)))
    fw1p = jnp.pad(fw1.astype(F32), ((0, emb_pad - HYENA_EMB), (0, 0)))
    fh = fw1.shape[1]
    max_decay = math.log(HYENA_DECAY_TARGET) / HYENA_FAST_PCT
    min_decay = math.log(HYENA_DECAY_TARGET) / HYENA_SLOW_PCT
    deltas = jnp.abs(jnp.linspace(min_decay, max_decay, width, dtype=F32))[None]
    ncb = width // cb
    rows = ch * n2

    def small(shape):
        return pl.BlockSpec(shape, lambda j, k: (0, 0))

    def w3_spec(blk):
        return pl.BlockSpec((fh, cb), lambda j, k: (0, blk * ncb + k))

    return pl.pallas_call(
        functools.partial(_filter_stage1_kernel, seq=seq),
        out_shape=jax.ShapeDtypeStruct((2, n1, 2 * n2, width // 2), jnp.uint32),
        grid=(n1 // ch, ncb),
        in_specs=[pl.BlockSpec((rows, emb_pad), lambda j, k: (j, 0)),
                  pl.BlockSpec((rows, 1), lambda j, k: (j, 0)),
                  small((emb_pad, fh)), small((1, fh)), small((fh, fh)), small((1, fh)), small((1, fh)),
                  w3_spec(0), w3_spec(1), w3_spec(2), w3_spec(3),
                  pl.BlockSpec((1, cb), lambda j, k: (0, k)),
                  pl.BlockSpec((ch, 2 * n2, n2), lambda j, k: (j, 0, 0))],
        out_specs=pl.BlockSpec((2, ch, 2 * n2, cb // 2), lambda j, k: (0, j, 0, k)),
        scratch_shapes=[pltpu.VMEM((rows, fh), BF16)],
        compiler_params=_params(("parallel", "arbitrary"), big=True),
        name="hyena_filter_stage1",
    )(z, t, fw1p, fb1.reshape(1, fh).astype(F32), fw2.astype(F32), fb2.reshape(1, fh).astype(F32),
      freq.reshape(1, fh).astype(F32), fw3, fw3, fw3, fw3, deltas, f1_real)


def _dft_constants():
    n1, n2 = FFT_N1, FFT_N2
    n = n1 * n2
    a1 = np.arange(n1)
    a2 = np.arange(n2)
    half = n2 // 2

    def cplx_block(w):
        return np.block([[w.real, -w.imag], [w.imag, w.real]])

    ang = (a2[None, :, None] * a2[None, None, :] / n2) + (a1[:, None, None] * a2[None, :, None] / n)
    w1 = np.exp(-2j * np.pi * ang)
    f1_pad = np.stack([cplx_block(w1[i][:, :half]) for i in range(n1)])
    f1_real = np.stack([np.concatenate([w1[i].real, w1[i].imag], axis=0) for i in range(n1)])
    w2 = np.exp(-2j * np.pi * (a1[:, None] * a1[None, :]) / n1)
    f2 = cplx_block(w2)
    ang = (a1[None, :, None] * a1[None, None, :] / n1) + (a2[:, None, None] * a1[None, :, None] / n)
    g2 = np.stack([cplx_block(m_) for m_ in np.exp(2j * np.pi * ang)])
    wg1 = np.exp(2j * np.pi * (a2[:half, None] * a2[None, :]) / n2) / n
    g1 = cplx_block(wg1)
    to = lambda x: jnp.asarray(x.astype(np.float32)).astype(BF16)
    return to(f1_pad), to(f1_real), to(f2), to(g2), to(g1)


def _fft_stage1_block(x_re, x_im, f_ref, y_ref):
    xr = pltpu.einshape("mjc->jmc", x_re)
    xi = pltpu.einshape("mjc->jmc", x_im)
    for j in range(FFT_CHUNK):
        xc = jnp.concatenate([xr[j], xi[j]], axis=0).astype(BF16)
        y_ref[j] = _pack_bf16_pairs(jnp.dot(f_ref[j], xc, preferred_element_type=F32))


def _fft_stage1_kernel(x_ref, f_ref, y_ref):
    _fft_stage1_block(x_ref[0], x_ref[1], f_ref, y_ref)


def _fft_stage2_kernel(yr_ref, yi_ref, f2_ref, h_ref):
    yr = pltpu.einshape("nkc->knc", yr_ref[...])
    yi = pltpu.einshape("nkc->knc", yi_ref[...])
    for j in range(FFT_CHUNK):
        yc = _unpack_bf16_pairs(jnp.concatenate([yr[j], yi[j]], axis=0))
        h_ref[j] = _pack_bf16_pairs(jnp.dot(f2_ref[...], yc, preferred_element_type=F32))


def _fft_mid_kernel(yr_ref, yi_ref, f2_ref, h_ref, g2_ref, q_ref):
    yr = pltpu.einshape("nkc->knc", yr_ref[...])
    yi = pltpu.einshape("nkc->knc", yi_ref[...])
    n1 = FFT_N1
    spec = [jnp.dot(f2_ref[...], _unpack_bf16_pairs(jnp.concatenate([yr[j], yi[j]], axis=0)),
                    preferred_element_type=F32) for j in range(FFT_CHUNK)]
    prod = []
    for j, z in enumerate(spec):
        hf = _unpack_pairs_f32(h_ref[j])
        zr, zi, hr, hi = z[:n1], z[n1:], hf[:n1], hf[n1:]
        prod.append(jnp.concatenate([zr * hr - zi * hi, zr * hi + zi * hr], axis=0).astype(BF16))
    for j, pc in enumerate(prod):
        q_ref[j] = _pack_bf16_pairs(jnp.dot(g2_ref[j], pc, preferred_element_type=F32))


def _fft_last_kernel(qr_ref, qi_ref, g1_ref, gate_ref, zf_ref, fb_ref, *rest, feeds_next):
    f1_ref, o_ref, y_ref = rest if feeds_next else (None, rest[0], None)
    qr = pltpu.einshape("kjc->jkc", qr_ref[...])
    qi = pltpu.einshape("kjc->jkc", qi_ref[...])
    half = FFT_N2 // 2
    conv = []
    for j in range(FFT_CHUNK):
        qc = _unpack_bf16_pairs(jnp.concatenate([qr[j], qi[j]], axis=0))
        conv.append(jnp.dot(g1_ref[...], qc, preferred_element_type=F32))
    conv = jnp.stack(conv, axis=0)
    fb = fb_ref[...]
    out = []
    for bi in range(2):
        cb = pltpu.einshape("jmc->mjc", conv[:, bi * half:(bi + 1) * half])
        out.append(gate_ref[bi] * (cb + zf_ref[bi] * fb))
        o_ref[bi] = out[bi]
    if feeds_next:
        _fft_stage1_block(out[0], out[1], f1_ref, y_ref)


def hyena_filter_spectrum(y, consts, cb=FFT_CB):
    _, _, f2, _, _ = consts
    no = y.shape[0]
    c = 2 * y.shape[-1]
    n1, n2, ch = FFT_N1, FFT_N2, FFT_CHUNK
    nk = n2 // ch
    return pl.pallas_call(
        _fft_stage2_kernel,
        out_shape=jax.ShapeDtypeStruct((no, n2, 2 * n1, c // 2), jnp.uint32),
        grid=(no, nk, c // cb),
        in_specs=[pl.BlockSpec((None, n1, ch, cb // 2), lambda o, j, k: (o, 0, j, k)),
                  pl.BlockSpec((None, n1, ch, cb // 2), lambda o, j, k: (o, 0, nk + j, k)),
                  pl.BlockSpec((2 * n1, 2 * n1), lambda o, j, k: (0, 0))],
        out_specs=pl.BlockSpec((None, ch, 2 * n1, cb // 2), lambda o, j, k: (o, j, 0, k)),
        compiler_params=_params(("parallel", "parallel", "parallel"), big=True),
        name="fft_filter_stage2",
    )(y, y, f2)


def hyena_long_conv_gate(zsrc, z_col, gate_src, gate_col, hf, order, fbias, consts, y=None,
                         feeds_next=False, cb=FFT_CB):
    f1_pad, _, f2, g2, g1 = consts
    n1, n2, ch = FFT_N1, FFT_N2, FFT_CHUNK
    c = 2 * hf.shape[-1]
    ncb = c // cb
    half = n2 // 2
    y_shape = jax.ShapeDtypeStruct((n1, 2 * n2, c // 2), jnp.uint32)
    f1_spec = pl.BlockSpec((ch, 2 * n2, n2), lambda j, k: (j, 0, 0))
    y_spec = pl.BlockSpec((ch, 2 * n2, cb // 2), lambda j, k: (j, 0, k))
    if y is None:
        y = pl.pallas_call(
            _fft_stage1_kernel,
            out_shape=y_shape,
            grid=(n1 // ch, ncb),
            in_specs=[pl.BlockSpec((2, half, ch, cb), lambda j, k: (0, 0, j, z_col * ncb + k)), f1_spec],
            out_specs=y_spec,
            compiler_params=_params(("parallel", "parallel"), big=True),
            name="fft_stage1",
        )(zsrc, f1_pad)
    nk = n2 // ch
    q = pl.pallas_call(
        _fft_mid_kernel,
        out_shape=jax.ShapeDtypeStruct((n2, 2 * n1, c // 2), jnp.uint32),
        grid=(nk, ncb),
        in_specs=[pl.BlockSpec((n1, ch, cb // 2), lambda j, k: (0, j, k)),
                  pl.BlockSpec((n1, ch, cb // 2), lambda j, k: (0, nk + j, k)),
                  pl.BlockSpec((2 * n1, 2 * n1), lambda j, k: (0, 0)),
                  pl.BlockSpec((None, ch, 2 * n1, cb // 2), lambda j, k: (order, j, 0, k)),
                  pl.BlockSpec((ch, 2 * n1, 2 * n1), lambda j, k: (j, 0, 0))],
        out_specs=pl.BlockSpec((ch, 2 * n1, cb // 2), lambda j, k: (j, 0, k)),
        compiler_params=_params(("parallel", "parallel"), big=True),
        name="fft_mid",
    )(y, y, f2, hf, g2)
    nj = n1 // ch
    in_specs = [pl.BlockSpec((n2, ch, cb // 2), lambda j, k: (0, j, k)),
                pl.BlockSpec((n2, ch, cb // 2), lambda j, k: (0, nj + j, k)),
                pl.BlockSpec((2 * half, 2 * n2), lambda j, k: (0, 0)),
                pl.BlockSpec((2, half, ch, cb), lambda j, k: (0, 0, j, gate_col * ncb + k)),
                pl.BlockSpec((2, half, ch, cb), lambda j, k: (0, 0, j, z_col * ncb + k)),
                pl.BlockSpec((None, 1, cb), lambda j, k: (order, 0, k))]
    args = [q, q, g1, gate_src, zsrc, fbias.reshape(fbias.shape[0], 1, c)]
    out_shape = jax.ShapeDtypeStruct((2, half, n1, c), F32)
    out_spec = pl.BlockSpec((2, half, ch, cb), lambda j, k: (0, 0, j, k))
    if feeds_next:
        in_specs.append(f1_spec)
        args.append(f1_pad)
        out_shape, out_spec = (out_shape, y_shape), (out_spec, y_spec)
    res = pl.pallas_call(
        functools.partial(_fft_last_kernel, feeds_next=feeds_next),
        out_shape=out_shape,
        grid=(nj, ncb),
        in_specs=in_specs,
        out_specs=out_spec,
        compiler_params=_params(("parallel", "parallel"), big=True),
        name="fft_last",
    )(*args)
    return res if feeds_next else (res, None)


def hyena_mixer(hn, h_res, w_in, b_in, conv_w, conv_b, fw1, fb1, fw2, fb2, freq, fw3, fbias, w_out,
                batch, seq):
    width = w_out.shape[0]
    assert batch == 2 and 2 * seq == FFT_N1 * FFT_N2
    sc = hyena_in_conv(hn, w_in, b_in, conv_w, conv_b, seq)
    consts = _dft_constants()
    hf = hyena_filter_spectrum(hyena_filter_stage1(seq, fw1, fb1, fw2, fb2, freq, fw3, width, consts),
                               consts)
    sc4 = sc.reshape(batch, FFT_N2 // 2, FFT_N1, 3 * width)
    zf1, y1 = hyena_long_conv_gate(sc4, 2, sc4, 0, hf, 0, fbias, consts, feeds_next=True)
    zf2, _ = hyena_long_conv_gate(zf1, 0, sc4, 1, hf, 1, fbias, consts, y=y1)
    return matmul([zf2.reshape(batch * seq, width)], w_out, res=h_res, tm=512, tn=1024,
                  name="hyena_out")


def moe_swiglu(hn2_packed, logits, wg, wu, wd):
    n = hn2_packed.shape[0]
    top_v, top_i = lax.top_k(logits, TOP_K)
    gates = jax.nn.softmax(top_v, axis=-1)
    e_flat = top_i.reshape(-1).astype(jnp.int32)
    nk = n * TOP_K
    onehot = (e_flat[:, None] == jnp.arange(N_EXPERTS, dtype=jnp.int32)[None]).astype(jnp.int32)
    csum = jnp.cumsum(onehot, axis=0)
    rank = jnp.take_along_axis(csum, e_flat[:, None], axis=1)[:, 0] - 1
    counts = csum[-1]
    padded = ((counts + MOE_TILE - 1) // MOE_TILE) * MOE_TILE
    pad_end = jnp.cumsum(padded)
    pad_start = pad_end - padded
    dest = pad_start[e_flat] + rank
    p_rows = nk + N_EXPERTS * MOE_TILE
    order = jnp.argsort(e_flat, stable=True).astype(jnp.int32)
    nf = wg.shape[2] // MOE_F_TILE
    sorted_tok = jnp.pad(order // TOP_K, (0, gather_rows_per_tile(MOE_TILE, nf)))
    start = jnp.cumsum(counts) - counts
    nt = p_rows // MOE_TILE
    tile_start = jnp.arange(nt, dtype=jnp.int32) * MOE_TILE
    tile_used = tile_start < pad_end[-1]
    tile_exp = jnp.minimum(jnp.searchsorted(pad_end, tile_start, side='right'), N_EXPERTS - 1).astype(jnp.int32)
    tile_rows = jnp.clip(counts[tile_exp] - (tile_start - pad_start[tile_exp]), 0, MOE_TILE)
    tile_rows = jnp.where(tile_used, tile_rows, 0).astype(jnp.int32)
    last_exp = tile_exp[jnp.maximum(jnp.sum(tile_used.astype(jnp.int32)) - 1, 0)]
    tile_src = jnp.where(tile_used, start[tile_exp] + (tile_start - pad_start[tile_exp]), 0)
    tile_exp = jnp.where(tile_used, tile_exp, last_exp)
    ys = swiglu(hn2_packed, wg, wu, wd, tile_exp, tile_rows, tile_src.astype(jnp.int32), sorted_tok,
                tf=MOE_F_TILE, sub_rows=FFN_SUB_ROWS, name="swiglu_experts")
    return ys, dest.reshape(n, TOP_K), gates


def kernel(x, p, ln_mix, ln_ffn, ln_ple, final_norm, t5_bias, w_attn_in, w_attn_out, attn_sink, na_rpb, w_ffn_gate, w_ffn_up, w_ffn_down, w_hy_in, b_hy_in, w_hy_conv, b_hy_conv, w_hy_f1, b_hy_f1, w_hy_f2, b_hy_f2, hy_freq, w_hy_f3, hy_bias, w_hy_out, w_router, w_exp_gate, w_exp_up, w_exp_down, w_ple_proj, w_ple_gate):
    batch, seq, d = x.shape
    n = batch * seq
    depth = ln_mix.shape[0]
    h = x.reshape(n, d)
    for i in range(depth):
        li = i // 2
        hn = rmsnorm(h, ln_mix[i], BF16)
        if i % 2 == 0:
            na_off = A_WIDTH + 2 * A_KV_WIDTH
            w_in = w_attn_in[li]
            proj_a = matmul_cols(hn, w_in, 0, na_off, tn=na_off, out_dtype=BF16, name="attn_in_a")
            proj_n = matmul_cols(hn, w_in, na_off, 3 * B_WIDTH, tn=3 * B_WIDTH // 2, out_dtype=BF16,
                                 name="attn_in_n")
            oa = window_attention(proj_a.reshape(batch, seq, -1), t5_bias, attn_sink[li])
            ob = neighbourhood_attention(proj_n.reshape(batch, seq, -1), na_rpb[li])
            h = matmul([oa.reshape(n, A_WIDTH), ob.reshape(n, B_WIDTH)], w_attn_out[li], res=h,
                       name="attn_out")
            nt = n // MOE_TILE
            h = swiglu(h, w_ffn_gate, w_ffn_up, w_ffn_down,
                       jnp.full((nt,), li, jnp.int32), jnp.full((nt,), MOE_TILE, jnp.int32),
                       gain=ln_ffn[i], name="swiglu_dense")
        else:
            h = hyena_mixer(hn, h, w_hy_in[li], b_hy_in[li], w_hy_conv[li], b_hy_conv[li],
                            w_hy_f1[li], b_hy_f1[li], w_hy_f2[li], b_hy_f2[li], hy_freq[li],
                            w_hy_f3[li], hy_bias[li], w_hy_out[li], batch, seq)
            wr_pad = jnp.pad(w_router[li].astype(F32), ((0, 0), (0, V7X_LANES - N_EXPERTS)))
            hn2_packed, logits = rmsnorm_router(h, ln_ffn[i], wr_pad)
            ys, dest2, gates = moe_swiglu(hn2_packed, logits[:, :N_EXPERTS], w_exp_gate[li],
                                          w_exp_up[li], w_exp_down[li])
            h = moe_combine(h, ys, dest2, gates)
        h = ple(h, p.reshape(depth, n, -1), ln_ple[i], w_ple_gate, w_ple_proj, i)
    return rmsnorm(h, final_norm, F32).reshape(batch, seq, d)
```

```python
import functools
import math

import jax
import jax.numpy as jnp
import numpy as np
from jax import lax
from jax.experimental import pallas as pl
from jax.experimental.pallas import tpu as pltpu

F32 = jnp.float32
BF16 = jnp.bfloat16
NEG_INF = -1e30
RMS_EPS = 1e-6

V7X_VMEM_LIMIT_BYTES = 56 * 1024 * 1024
V7X_LANES = 128

HEAD_DIM = 64
A_Q_HEADS = 16
A_KV_HEADS = 2
A_GROUP = A_Q_HEADS // A_KV_HEADS
A_BLOCK = 128
T5_BUCKETS = 32
T5_MAX_DIST = 128
B_HEADS = 16
GRID_W = 64
NA_WIN_H = 8
NA_WIN_W = 16
A_WIDTH = A_Q_HEADS * HEAD_DIM
A_KV_WIDTH = A_KV_HEADS * HEAD_DIM
B_WIDTH = B_HEADS * HEAD_DIM
N_EXPERTS = 8
TOP_K = 2
HYENA_EMB = 33
HYENA_DECAY_TARGET = 1e-2
HYENA_FAST_PCT = 0.3
HYENA_SLOW_PCT = 1.5

FFT_N1 = 64
FFT_N2 = 128
FFT_CHUNK = 8
FFT_CB = 1024

MOE_TILE = 1024
FFN_F_TILE = 512
MOE_F_TILE = 512
FFN_SUB_ROWS = 256
FFN_OUT_CHUNKS = 4
DMA_LOOP_UNROLL = 8


def _params(semantics, big=False):
    return pltpu.CompilerParams(
        dimension_semantics=semantics,
        vmem_limit_bytes=V7X_VMEM_LIMIT_BYTES if big else None)


def _rmsnorm_kernel(x_ref, g_ref, o_ref):
    x = x_ref[...]
    y = x * lax.rsqrt(jnp.mean(x * x, axis=-1, keepdims=True) + RMS_EPS)
    o_ref[...] = (y * g_ref[...]).astype(o_ref.dtype)


def rmsnorm(x2, g, out_dtype, tm=1024):
    n, d = x2.shape
    return pl.pallas_call(
        _rmsnorm_kernel,
        out_shape=jax.ShapeDtypeStruct((n, d), out_dtype),
        grid=(n // tm,),
        in_specs=[pl.BlockSpec((tm, d), lambda i: (i, 0)),
                  pl.BlockSpec((1, d), lambda i: (0, 0))],
        out_specs=pl.BlockSpec((tm, d), lambda i: (i, 0)),
        compiler_params=_params(("parallel",), big=True),
        name="rmsnorm",
    )(x2, g.reshape(1, d))


def _pack_bf16_pairs(y):
    w = y.shape[1] // 2
    bits = pltpu.bitcast(y.astype(BF16).astype(F32), jnp.uint32)
    return (bits[:, :w] >> 16) | (bits[:, w:] & jnp.uint32(0xFFFF0000))


def _unpack_pairs_f32(p):
    lo = pltpu.bitcast(p << 16, F32)
    hi = pltpu.bitcast(p & jnp.uint32(0xFFFF0000), F32)
    return jnp.concatenate([lo, hi], axis=1)


def _unpack_bf16_pairs(p):
    return _unpack_pairs_f32(p).astype(BF16)


def _rmsnorm_router_kernel(x_ref, g_ref, wr_ref, o_ref, l_ref):
    x = x_ref[...]
    y = x * lax.rsqrt(jnp.mean(x * x, axis=-1, keepdims=True) + RMS_EPS)
    y = y * g_ref[...]
    o_ref[...] = _pack_bf16_pairs(y)
    w = wr_ref[...]
    y_hi, w_hi = y.astype(BF16), w.astype(BF16)
    y_lo = (y - y_hi.astype(F32)).astype(BF16)
    w_lo = (w - w_hi.astype(F32)).astype(BF16)
    l_ref[...] = (jnp.dot(y_hi, w_hi, preferred_element_type=F32)
                  + jnp.dot(y_lo, w_hi, preferred_element_type=F32)
                  + jnp.dot(y_hi, w_lo, preferred_element_type=F32))


def rmsnorm_router(x2, g, w_router_pad, tm=512):
    n, d = x2.shape
    ne = w_router_pad.shape[1]
    return pl.pallas_call(
        _rmsnorm_router_kernel,
        out_shape=(jax.ShapeDtypeStruct((n, d // 2), jnp.uint32), jax.ShapeDtypeStruct((n, ne), F32)),
        grid=(n // tm,),
        in_specs=[pl.BlockSpec((tm, d), lambda i: (i, 0)),
                  pl.BlockSpec((1, d), lambda i: (0, 0)),
                  pl.BlockSpec((d, ne), lambda i: (0, 0))],
        out_specs=(pl.BlockSpec((tm, d // 2), lambda i: (i, 0)),
                   pl.BlockSpec((tm, ne), lambda i: (i, 0))),
        compiler_params=_params(("parallel",)),
        name="rmsnorm_router",
    )(x2, g.reshape(1, d), w_router_pad)


def _mm_kernel(*refs, n_a, has_res):
    a_refs = refs[:n_a]
    w_refs = refs[n_a:2 * n_a]
    idx = 2 * n_a
    res_ref = refs[idx] if has_res else None
    idx += int(has_res)
    o_ref = refs[idx]
    wbf_refs = refs[idx + 1:idx + 1 + n_a]

    @pl.when(pl.program_id(1) == 0)
    def _():
        for w_ref, wbf_ref in zip(w_refs, wbf_refs):
            wbf_ref[...] = w_ref[...].astype(BF16)

    acc = None
    for a_ref, wbf_ref in zip(a_refs, wbf_refs):
        d = jnp.dot(a_ref[...].astype(BF16), wbf_ref[...], preferred_element_type=F32)
        acc = d if acc is None else acc + d
    if has_res:
        acc = acc + res_ref[...]
    o_ref[...] = acc.astype(o_ref.dtype)


def matmul(a_list, w, *, res=None, out_dtype=F32, tm=1024, tn=1024, name="matmul"):
    m = a_list[0].shape[0]
    n = w.shape[1]
    n_a = len(a_list)
    k_each = a_list[0].shape[1]
    assert all(a.shape == (m, k_each) for a in a_list) and w.shape[0] == n_a * k_each
    in_specs = [pl.BlockSpec((tm, k_each), lambda j, i: (i, 0)) for _ in a_list]
    in_specs += [pl.BlockSpec((k_each, tn), functools.partial(lambda j, i, kb: (kb, j), kb=kb))
                 for kb in range(n_a)]
    args = list(a_list) + [w] * n_a
    if res is not None:
        in_specs.append(pl.BlockSpec((tm, tn), lambda j, i: (i, j)))
        args.append(res)
    return pl.pallas_call(
        functools.partial(_mm_kernel, n_a=n_a, has_res=res is not None),
        out_shape=jax.ShapeDtypeStruct((m, n), out_dtype),
        grid=(n // tn, m // tm),
        in_specs=in_specs,
        out_specs=pl.BlockSpec((tm, tn), lambda j, i: (i, j)),
        scratch_shapes=[pltpu.VMEM((k_each, tn), BF16) for _ in a_list],
        compiler_params=_params(("parallel", "arbitrary"), big=True),
        name=name,
    )(*args)


def _rms_scale_bf16(x, g):
    y = x * lax.rsqrt(jnp.mean(x * x, axis=-1, keepdims=True) + RMS_EPS)
    return (y * g).astype(BF16)


def _ple_kernel(h_ref, p_ref, g_ref, wg_ref, wp_ref, o_ref, wg_bf, wp_bf):
    @pl.when(pl.program_id(1) == 0)
    def _():
        wg_bf[...] = wg_ref[...].astype(BF16)
        wp_bf[...] = wp_ref[...].astype(BF16)

    tn = o_ref.shape[1]
    a = jnp.dot(_rms_scale_bf16(h_ref[...], g_ref[...]), wg_bf[...], preferred_element_type=F32)
    pp = jnp.dot(p_ref[...].astype(BF16), wp_bf[...], preferred_element_type=F32)
    col = pl.multiple_of(pl.program_id(0) * tn, tn)
    o_ref[...] = h_ref[:, pl.ds(col, tn)] + jax.nn.sigmoid(a) * pp


def ple(h2, p_all, g, w_gate_all, w_proj_all, layer, tm=512, tn=1024):
    m, d = h2.shape
    pd = p_all.shape[2]
    return pl.pallas_call(
        _ple_kernel,
        out_shape=jax.ShapeDtypeStruct((m, d), F32),
        grid=(d // tn, m // tm),
        in_specs=[pl.BlockSpec((tm, d), lambda j, i: (i, 0)),
                  pl.BlockSpec((None, tm, pd), lambda j, i: (layer, i, 0)),
                  pl.BlockSpec((1, d), lambda j, i: (0, 0)),
                  pl.BlockSpec((None, d, tn), lambda j, i: (layer, 0, j)),
                  pl.BlockSpec((None, pd, tn), lambda j, i: (layer, 0, j))],
        out_specs=pl.BlockSpec((tm, tn), lambda j, i: (i, j)),
        scratch_shapes=[pltpu.VMEM((d, tn), BF16), pltpu.VMEM((pd, tn), BF16)],
        compiler_params=_params(("parallel", "arbitrary"), big=True),
        name="ple",
    )(h2, p_all, g.reshape(1, d), w_gate_all, w_proj_all)


def _row_copy(src_hbm, row, dst, i, sem):
    return pltpu.make_async_copy(src_hbm.at[pl.ds(row, 1), :], dst.at[pl.ds(i, 1), :], sem)


def _swiglu_kernel(exp_ref, rows_ref, src_ref, tok_ref, x_ref, *rest, n_col, gather, sub_rows, nf):
    del exp_ref
    gain_ref = None if gather else rest[0]
    wg_ref, wu_ref, wd_ref, o_ref = rest[0 if gather else 1:][:4]
    scratch = rest[(4 if gather else 5):]
    t = pl.program_id(0)
    f = pl.program_id(1)
    nt = pl.num_programs(0)
    n_rows = rows_ref[t]
    tm = o_ref.shape[0]

    if not gather:
        (xbf,) = scratch

        @pl.when(f == 0)
        def _():
            h = x_ref[...]
            o_ref[...] = h
            xbf[...] = _rms_scale_bf16(h, gain_ref[...])

    if gather:
        @pl.when(f == 0)
        def _():
            o_ref[...] = jnp.zeros_like(o_ref)

        xbuf, xbf_both, sems = scratch
        slot = t % 2
        per_step = xbuf.shape[1]
        xbf = xbf_both.at[slot]

        def start_chunk(tile, c):
            for r in range(per_step):
                _row_copy(x_ref, tok_ref[src_ref[tile] + c * per_step + r], xbuf.at[c % 2], r,
                          sems.at[c % 2, r]).start()

        def wait_chunk(c):
            for r in range(per_step):
                _row_copy(x_ref, 0, xbuf.at[c % 2], r, sems.at[c % 2, r]).wait()

        def land_chunk(buf, c):
            wait_chunk(c)
            xbf_both[buf, pl.ds(pl.multiple_of(c * per_step, per_step), per_step), :] = (
                _unpack_bf16_pairs(xbuf[c % 2]))

        @pl.when(jnp.logical_and(f == 0, t == 0))
        def _():
            def body(c, carry):
                start_chunk(0, c)
                land_chunk(0, c)
                return carry
            lax.fori_loop(0, nf - 1, body, 0)
            start_chunk(0, nf - 1)

        def fetch_ahead():
            land_chunk(jnp.where(f == 0, slot, 1 - slot), (f + nf - 1) % nf)
            start_chunk(jnp.minimum(t + 1, nt - 1), f)
    else:
        def fetch_ahead():
            pass

    def ffn_rows(n):
        fetch_ahead()
        if n == 0:
            return
        x = xbf[0:n, :]
        g = jnp.dot(x, wg_ref[...].astype(BF16), preferred_element_type=F32)
        u = jnp.dot(x, wu_ref[...].astype(BF16), preferred_element_type=F32)
        hmid = (g * jax.nn.sigmoid(g) * u).astype(BF16)
        wd = wd_ref[...].astype(BF16)
        cw = o_ref.shape[1] // n_col
        for c in range(n_col):
            y = jnp.dot(hmid, wd[:, c * cw:(c + 1) * cw], preferred_element_type=F32)
            o_ref[0:n, c * cw:(c + 1) * cw] += y

    for k in range(0 if gather else 1, tm // sub_rows + 1):
        covers = jnp.logical_and(n_rows > (k - 1) * sub_rows, n_rows <= k * sub_rows)
        pl.when(covers)(functools.partial(ffn_rows, k * sub_rows))

    if gather:
        @pl.when(jnp.logical_and(t == nt - 1, f == nf - 1))
        def _():
            wait_chunk(nf - 1)


BF16_SUBLANES = 16


def gather_rows_per_tile(tm, nf):
    per_step = -(-tm // nf)
    per_step = -(-per_step // BF16_SUBLANES) * BF16_SUBLANES
    return per_step * nf


def swiglu(x, w_gate, w_up, w_down, tile_exp, tile_rows, tile_src=None, src_rows=None, *, gain=None,
           tm=MOE_TILE, tf=FFN_F_TILE, sub_rows=MOE_TILE, name="swiglu"):
    gather = src_rows is not None
    assert gather != (gain is not None)
    m = tile_rows.shape[0] * tm
    d = w_gate.shape[1]
    dff = w_gate.shape[2]
    nf = dff // tf
    nt = m // tm
    assert nf >= 2

    def w_in_map(t, f, exp_ref, rows_ref, src_ref, tok_ref):
        return (exp_ref[t], 0, jnp.where(rows_ref[t] > 0, f, nf - 1))

    def w_out_map(t, f, exp_ref, rows_ref, src_ref, tok_ref):
        return (exp_ref[t], jnp.where(rows_ref[t] > 0, f, nf - 1), 0)

    def tile_map(t, f, exp_ref, rows_ref, src_ref, tok_ref):
        return (t, 0)

    w_specs = [pl.BlockSpec((None, d, tf), w_in_map),
               pl.BlockSpec((None, d, tf), w_in_map),
               pl.BlockSpec((None, tf, d), w_out_map)]
    if gather:
        in_specs = [pl.BlockSpec(memory_space=pl.ANY)] + w_specs
        args = [x, w_gate, w_up, w_down]
        n_fetch = gather_rows_per_tile(tm, nf)
        per_step = n_fetch // nf
        scratch = [pltpu.VMEM((2, per_step, d // 2), jnp.uint32), pltpu.VMEM((2, n_fetch, d), BF16),
                   pltpu.SemaphoreType.DMA((2, per_step))]
    else:
        in_specs = [pl.BlockSpec((tm, d), tile_map, pipeline_mode=pl.Buffered(1)),
                    pl.BlockSpec((1, d), lambda t, f, e, r, s, k: (0, 0))] + w_specs
        args = [x, gain.reshape(1, d), w_gate, w_up, w_down]
        scratch = [pltpu.VMEM((tm, d), BF16)]
        tile_src = jnp.zeros((nt,), jnp.int32)
        src_rows = jnp.zeros((1,), jnp.int32)
    grid_spec = pltpu.PrefetchScalarGridSpec(
        num_scalar_prefetch=4,
        grid=(nt, nf),
        in_specs=in_specs,
        out_specs=pl.BlockSpec((tm, d), tile_map, pipeline_mode=pl.Buffered(1)),
        scratch_shapes=scratch,
    )
    return pl.pallas_call(
        functools.partial(_swiglu_kernel, n_col=FFN_OUT_CHUNKS, gather=gather, sub_rows=sub_rows, nf=nf),
        out_shape=jax.ShapeDtypeStruct((m, d), F32),
        grid_spec=grid_spec,
        compiler_params=_params(("arbitrary", "arbitrary"), big=True),
        name=name,
    )(tile_exp, tile_rows, tile_src, src_rows, *args)


COMBINE_TILE = 256


def _combine_kernel(d0_ref, d1_ref, ys_hbm, gate_ref, h_ref, hout_ref, buf, sems):
    i = pl.program_id(0)
    n = pl.num_programs(0)
    tt = h_ref.shape[0]
    slot = i % 2
    dests = (d0_ref, d1_ref)

    def start_tile(tile, sl):
        def body(r, c):
            for k, d_ref in enumerate(dests):
                _row_copy(ys_hbm, d_ref[tile * tt + r], buf.at[sl, k], r, sems.at[sl, k, r]).start()
            return c
        lax.fori_loop(0, tt, body, 0, unroll=DMA_LOOP_UNROLL)

    @pl.when(i == 0)
    def _():
        start_tile(0, 0)

    @pl.when(i + 1 < n)
    def _():
        start_tile(i + 1, 1 - slot)

    def wait_body(r, c):
        for k in range(TOP_K):
            _row_copy(ys_hbm, 0, buf.at[slot, k], r, sems.at[slot, k, r]).wait()
        return c
    lax.fori_loop(0, tt, wait_body, 0, unroll=DMA_LOOP_UNROLL)

    gates = gate_ref[...]
    hout_ref[...] = h_ref[...] + (buf[slot, 0] * gates[:, 0:1] + buf[slot, 1] * gates[:, 1:2])


def moe_combine(h2, ys, dest2, gates, tt=COMBINE_TILE):
    n, d = h2.shape
    grid_spec = pltpu.PrefetchScalarGridSpec(
        num_scalar_prefetch=2,
        grid=(n // tt,),
        in_specs=[pl.BlockSpec(memory_space=pl.ANY),
                  pl.BlockSpec((tt, TOP_K), lambda i, a, b: (i, 0)),
                  pl.BlockSpec((tt, d), lambda i, a, b: (i, 0))],
        out_specs=pl.BlockSpec((tt, d), lambda i, a, b: (i, 0)),
        scratch_shapes=[pltpu.VMEM((2, TOP_K, tt, d), F32), pltpu.SemaphoreType.DMA((2, TOP_K, tt))],
    )
    return pl.pallas_call(
        _combine_kernel,
        out_shape=jax.ShapeDtypeStruct((n, d), F32),
        grid_spec=grid_spec,
        compiler_params=_params(("arbitrary",), big=True),
        name="moe_combine",
    )(dest2[:, 0], dest2[:, 1], ys, gates, h2)


def _t5_bucket(rel):
    half = T5_BUCKETS // 2
    max_exact = half // 2
    n = jnp.abs(rel)
    log_ratio = jnp.log(jnp.maximum(n, 1).astype(F32) / max_exact) / math.log(T5_MAX_DIST / max_exact)
    large = jnp.minimum(max_exact + (log_ratio * (half - max_exact)).astype(jnp.int32), half - 1)
    return jnp.where(rel > 0, half, 0) + jnp.where(n < max_exact, n, large)


def _window_bias_table(t5_bias):
    i = jnp.arange(A_BLOCK)[:, None]
    j = jnp.arange(3 * A_BLOCK)[None, :]
    rel = j - A_BLOCK - i
    onehot = (_t5_bucket(rel)[None] == jnp.arange(T5_BUCKETS)[:, None, None]).astype(F32)
    bias = jnp.einsum('bh,bij->hij', t5_bias.astype(F32), onehot, precision=lax.Precision.HIGHEST)
    return jnp.where((jnp.abs(rel) <= A_BLOCK)[None], bias, NEG_INF)


def _window_kernel(sink_ref, q_ref, kv_ref, bias_ref, o_ref, *, nb):
    n = pl.program_id(1)
    scale = HEAD_DIM ** -0.5
    starts = (jnp.maximum(n - 1, 0), n, jnp.minimum(n + 1, nb - 1))
    kv = [kv_ref[pl.ds(pl.multiple_of(s * A_BLOCK, A_BLOCK), A_BLOCK), :] for s in starts]
    col = lax.broadcasted_iota(jnp.int32, (A_BLOCK, 3 * A_BLOCK), 1)
    edge_ok = jnp.logical_and(jnp.logical_or(n > 0, col >= A_BLOCK),
                              jnp.logical_or(n < nb - 1, col < 2 * A_BLOCK))
    kv = jnp.concatenate(kv, axis=0)
    q = q_ref[...]
    lo = lax.broadcasted_iota(jnp.int32, (1, 2 * HEAD_DIM), 1) < HEAD_DIM
    sel = (jnp.where(lo, scale, 0.0).astype(BF16), jnp.where(lo, 0.0, scale).astype(BF16))
    pairs_per_kv = A_GROUP // 2
    kk, vv = [], []
    for kh in range(A_KV_HEADS):
        k1 = kv[:, kh * HEAD_DIM:(kh + 1) * HEAD_DIM]
        v1 = kv[:, A_KV_WIDTH + kh * HEAD_DIM:A_KV_WIDTH + (kh + 1) * HEAD_DIM]
        kk.append(jnp.concatenate([k1, k1], axis=1))
        vv.append(jnp.concatenate([v1, v1], axis=1))
    scores = []
    for pr in range(A_Q_HEADS // 2):
        q2 = q[:, pr * 2 * HEAD_DIM:(pr + 1) * 2 * HEAD_DIM]
        qq = jnp.concatenate([q2 * sel[0], q2 * sel[1]], axis=0)
        s2 = lax.dot_general(qq, kk[pr // pairs_per_kv], (((1,), (1,)), ((), ())),
                             preferred_element_type=F32)
        for i in range(2):
            scores.append(jnp.where(edge_ok, s2[i * A_BLOCK:(i + 1) * A_BLOCK] + bias_ref[2 * pr + i],
                                    NEG_INF))
    probs, dens = [], []
    for h, s in enumerate(scores):
        sink = sink_ref[h]
        mx = jnp.maximum(jnp.max(s, axis=-1, keepdims=True), sink)
        p = jnp.exp(s - mx)
        dens.append(jnp.sum(p, axis=-1, keepdims=True) + jnp.exp(sink - mx))
        probs.append(p.astype(BF16))
    for pr in range(A_Q_HEADS // 2):
        v2 = vv[pr // pairs_per_kv]
        oa = jnp.dot(probs[2 * pr], v2, preferred_element_type=F32) / dens[2 * pr]
        ob = jnp.dot(probs[2 * pr + 1], v2, preferred_element_type=F32) / dens[2 * pr + 1]
        o_ref[:, pr * 2 * HEAD_DIM:(pr + 1) * 2 * HEAD_DIM] = jnp.where(lo, oa, ob).astype(o_ref.dtype)


def window_attention(proj_a, t5_bias, sink):
    b, s, _ = proj_a.shape
    nb = s // A_BLOCK
    kvw = 2 * A_KV_WIDTH
    grid_spec = pltpu.PrefetchScalarGridSpec(
        num_scalar_prefetch=0,
        grid=(b, nb),
        in_specs=[pl.BlockSpec(memory_space=pltpu.SMEM),
                  pl.BlockSpec((None, A_BLOCK, A_WIDTH), lambda bi, n: (bi, n, 0)),
                  pl.BlockSpec((None, s, kvw), lambda bi, n: (bi, 0, A_WIDTH // kvw)),
                  pl.BlockSpec((A_Q_HEADS, A_BLOCK, 3 * A_BLOCK), lambda bi, n: (0, 0, 0))],
        out_specs=pl.BlockSpec((None, A_BLOCK, A_WIDTH), lambda bi, n: (bi, n, 0)),
    )
    return pl.pallas_call(
        functools.partial(_window_kernel, nb=nb),
        out_shape=jax.ShapeDtypeStruct((b, s, A_WIDTH), BF16),
        grid_spec=grid_spec,
        compiler_params=_params(("parallel", "arbitrary")),
        name="window_attention",
    )(sink.astype(F32), proj_a, proj_a, _window_bias_table(t5_bias))


NA_HEAD_GROUP = 16


def _na_bias_table(rpb, rows):
    kh = min(NA_WIN_H, rows)
    kw = NA_WIN_W
    c = jnp.arange(GRID_W)
    cs = jnp.clip(c - kw // 2, 0, GRID_W - kw)
    col_ok = (c[None] >= cs[:, None]) & (c[None] < cs[:, None] + kw)
    col_off = jnp.clip(c[None] - c[:, None], -(kw - 1), kw - 1) + kw - 1
    hi = lax.Precision.HIGHEST
    row_off = jnp.arange(kh)[None, :] - jnp.arange(kh)[:, None] + NA_WIN_H - 1
    row_sel = (row_off[:, :, None] == jnp.arange(2 * NA_WIN_H - 1)[None, None, :]).astype(F32)
    col_sel = (col_off[None] == jnp.arange(2 * kw - 1)[:, None, None]).astype(F32)
    by_row = jnp.einsum('hrc,dir->dhic', rpb.astype(F32), row_sel, precision=hi)
    bias = jnp.einsum('dhic,cqk->dhqik', by_row, col_sel, precision=hi)
    bias = jnp.where(col_ok[None, None, :, None, :], bias, NEG_INF)
    return bias.reshape(kh, rpb.shape[0], GRID_W, kh * GRID_W)


def _na_kernel(q_ref, k_ref, v_ref, bias_ref, o_ref, *, rows):
    r = pl.program_id(2)
    kh = min(NA_WIN_H, rows)
    scale = HEAD_DIM ** -0.5
    start = pl.multiple_of(jnp.clip(r - kh // 2, 0, rows - kh) * GRID_W, GRID_W)
    k = k_ref[pl.ds(start, kh * GRID_W), :]
    v = v_ref[pl.ds(start, kh * GRID_W), :]
    q = q_ref[...]
    lo = lax.broadcasted_iota(jnp.int32, (1, 2 * HEAD_DIM), 1) < HEAD_DIM
    sel = (jnp.where(lo, scale, 0.0).astype(BF16), jnp.where(lo, 0.0, scale).astype(BF16))
    n_pairs = NA_HEAD_GROUP // 2
    scores = []
    for pr in range(n_pairs):
        sl = slice(pr * 2 * HEAD_DIM, (pr + 1) * 2 * HEAD_DIM)
        q2 = q[:, sl]
        qq = jnp.concatenate([q2 * sel[0], q2 * sel[1]], axis=0)
        s2 = lax.dot_general(qq, k[:, sl], (((1,), (1,)), ((), ())), preferred_element_type=F32)
        for i in range(2):
            scores.append(s2[i * GRID_W:(i + 1) * GRID_W] + bias_ref[2 * pr + i])
    probs, dens = [], []
    for s in scores:
        mx = jnp.max(s, axis=-1, keepdims=True)
        p = jnp.exp(s - mx)
        dens.append(jnp.sum(p, axis=-1, keepdims=True))
        probs.append(p.astype(BF16))
    for pr in range(n_pairs):
        sl = slice(pr * 2 * HEAD_DIM, (pr + 1) * 2 * HEAD_DIM)
        oa = jnp.dot(probs[2 * pr], v[:, sl], preferred_element_type=F32) / dens[2 * pr]
        ob = jnp.dot(probs[2 * pr + 1], v[:, sl], preferred_element_type=F32) / dens[2 * pr + 1]
        o_ref[:, sl] = jnp.where(lo, oa, ob).astype(o_ref.dtype)


def neighbourhood_attention(proj_n, rpb):
    b, s, _ = proj_n.shape
    rows = s // GRID_W
    kh = min(NA_WIN_H, rows)
    gw = NA_HEAD_GROUP * HEAD_DIM
    ng = B_WIDTH // gw

    def bias_map(bi, g, r):
        return (r - jnp.clip(r - kh // 2, 0, rows - kh), g, 0, 0)

    return pl.pallas_call(
        functools.partial(_na_kernel, rows=rows),
        out_shape=jax.ShapeDtypeStruct((b, s, B_WIDTH), BF16),
        grid=(b, ng, rows),
        in_specs=[pl.BlockSpec((None, GRID_W, gw), lambda bi, g, r: (bi, r, g)),
                  pl.BlockSpec((None, s, gw), lambda bi, g, r: (bi, 0, ng + g)),
                  pl.BlockSpec((None, s, gw), lambda bi, g, r: (bi, 0, 2 * ng + g)),
                  pl.BlockSpec((None, NA_HEAD_GROUP, GRID_W, kh * GRID_W), bias_map)],
        out_specs=pl.BlockSpec((None, GRID_W, gw), lambda bi, g, r: (bi, r, g)),
        compiler_params=_params(("parallel", "parallel", "arbitrary"), big=True),
        name="neighbourhood_attention",
    )(proj_n, proj_n, proj_n, _na_bias_table(rpb, rows))


CONV_HALO = 16
CONV_COL_CHUNK = 512


def _hyena_in_kernel(prev_ref, cur_ref, next_ref, w_ref, b_ref, cw_ref, cb_ref, o_ref, wbf_ref, *,
                     tiles_per_seq):
    i = pl.program_id(1)

    @pl.when(i == 0)
    def _():
        wbf_ref[...] = w_ref[...].astype(BF16)

    tm = cur_ref.shape[0]
    rows = tm + 2 * CONV_HALO
    a = jnp.concatenate([prev_ref[...], cur_ref[...], next_ref[...]], axis=0)
    pos = i % tiles_per_seq
    row = lax.broadcasted_iota(jnp.int32, (rows, 1), 0)
    kill_up = jnp.logical_and(pos == 0, row == CONV_HALO)
    kill_dn = jnp.logical_and(pos == tiles_per_seq - 1, row == CONV_HALO + tm - 1)
    for c in range(o_ref.shape[1] // CONV_COL_CHUNK):
        cs = slice(c * CONV_COL_CHUNK, (c + 1) * CONV_COL_CHUNK)
        p = jnp.dot(a, wbf_ref[:, cs], preferred_element_type=F32) + b_ref[:, cs]
        up = jnp.where(kill_up, 0.0, pltpu.roll(p, 1, 0))
        dn = jnp.where(kill_dn, 0.0, pltpu.roll(p, rows - 1, 0))
        y = up * cw_ref[0:1, cs] + p * cw_ref[1:2, cs] + dn * cw_ref[2:3, cs] + cb_ref[:, cs]
        o_ref[:, cs] = y[CONV_HALO:CONV_HALO + tm]


def hyena_in_conv(hn, w_in, b_in, conv_w, conv_b, seq, tm=1024, tn=1024):
    m, k = hn.shape
    n = w_in.shape[1]
    hb = tm // CONV_HALO
    last_halo = m // CONV_HALO - 1
    return pl.pallas_call(
        functools.partial(_hyena_in_kernel, tiles_per_seq=seq // tm),
        out_shape=jax.ShapeDtypeStruct((m, n), F32),
        grid=(n // tn, m // tm),
        in_specs=[pl.BlockSpec((CONV_HALO, k), lambda j, i: (jnp.maximum(i * hb - 1, 0), 0)),
                  pl.BlockSpec((tm, k), lambda j, i: (i, 0)),
                  pl.BlockSpec((CONV_HALO, k), lambda j, i: (jnp.minimum((i + 1) * hb, last_halo), 0)),
                  pl.BlockSpec((k, tn), lambda j, i: (0, j)),
                  pl.BlockSpec((1, tn), lambda j, i: (0, j)),
                  pl.BlockSpec((3, tn), lambda j, i: (0, j)),
                  pl.BlockSpec((1, tn), lambda j, i: (0, j))],
        out_specs=pl.BlockSpec((tm, tn), lambda j, i: (i, j)),
        scratch_shapes=[pltpu.VMEM((k, tn), BF16)],
        compiler_params=_params(("parallel", "arbitrary"), big=True),
        name="hyena_in_conv",
    )(hn, hn, hn, w_in, b_in.reshape(1, n), conv_w, conv_b.reshape(1, n))


def _filter_stage1_kernel(z_ref, t_ref, fw1_ref, fb1_ref, fw2_ref, fb2_ref, fr_ref, w3f0_ref, w3f1_ref,
                          w3b0_ref, w3b1_ref, delta_ref, f1_ref, y_ref, hid_ref, *, seq):
    j = pl.program_id(0)

    @pl.when(pl.program_id(1) == 0)
    def _():
        hi = lax.Precision.HIGHEST
        fr = fr_ref[...]
        hid = jnp.sin(fr * (jnp.dot(z_ref[...], fw1_ref[...], preferred_element_type=F32, precision=hi)
                            + fb1_ref[...]))
        hid = jnp.sin(fr * (jnp.dot(hid, fw2_ref[...], preferred_element_type=F32, precision=hi)
                            + fb2_ref[...]))
        hid_ref[...] = hid.astype(BF16)

    hid_bf = hid_ref[...]
    r = lax.broadcasted_iota(jnp.int32, (hid_bf.shape[0], 1), 0)
    u = j * FFT_CHUNK + r // FFT_N2 + FFT_N1 * (r % FFT_N2)
    decay = jnp.where(u == seq, 0.0, jnp.exp(-t_ref[...] * delta_ref[...]))
    for o, (wf_ref, wb_ref) in enumerate(((w3f0_ref, w3b0_ref), (w3f1_ref, w3b1_ref))):
        fwd = jnp.dot(hid_bf, wf_ref[...].astype(BF16), preferred_element_type=F32)
        bwd = jnp.dot(hid_bf, wb_ref[...].astype(BF16), preferred_element_type=F32)
        taps = (jnp.where(u > seq, bwd, jnp.where(u == 0, fwd + bwd, fwd)) * decay).astype(BF16)
        for jj in range(FFT_CHUNK):
            x = taps[jj * FFT_N2:(jj + 1) * FFT_N2]
            y_ref[o, jj] = _pack_bf16_pairs(jnp.dot(f1_ref[jj], x, preferred_element_type=F32))


def hyena_filter_stage1(seq, fw1, fb1, fw2, fb2, freq, fw3, width, consts, cb=FFT_CB):
    _, f1_real, _, _, _ = consts
    n = 2 * seq
    n1, n2, ch = FFT_N1, FFT_N2, FFT_CHUNK
    assert fw3.shape[1] == 4 * width
    r = jnp.arange(n)
    u = (r // (ch * n2)) * ch + (r // n2) % ch + n1 * (r % n2)
    pos = jnp.where(u <= seq, u, n - u)
    pos = jnp.where(u == seq, 0, pos)
    bands = (HYENA_EMB - 1) // 2
    f = jnp.linspace(1e-4, bands - 1, bands, dtype=F32)[None]
    t = (pos.astype(F32) / (seq - 1))[:, None]
    w = ((2.0 * math.pi / seq) * pos.astype(F32))[:, None]
    z = jnp.concatenate([t, jnp.cos(f * w), -jnp.sin(f * w)], axis=-1)
    emb_pad = 40
    z = jnp.pad(z, ((0, 0), (0, emb_pad - HYENA_EMB)))
    fw1p = jnp.pad(fw1.astype(F32), ((0, emb_pad - HYENA_EMB), (0, 0)))
    fh = fw1.shape[1]
    max_decay = math.log(HYENA_DECAY_TARGET) / HYENA_FAST_PCT
    min_decay = math.log(HYENA_DECAY_TARGET) / HYENA_SLOW_PCT
    deltas = jnp.abs(jnp.linspace(min_decay, max_decay, width, dtype=F32))[None]
    ncb = width // cb
    rows = ch * n2

    def small(shape):
        return pl.BlockSpec(shape, lambda j, k: (0, 0))

    def w3_spec(blk):
        return pl.BlockSpec((fh, cb), lambda j, k: (0, blk * ncb + k))

    return pl.pallas_call(
        functools.partial(_filter_stage1_kernel, seq=seq),
        out_shape=jax.ShapeDtypeStruct((2, n1, 2 * n2, width // 2), jnp.uint32),
        grid=(n1 // ch, ncb),
        in_specs=[pl.BlockSpec((rows, emb_pad), lambda j, k: (j, 0)),
                  pl.BlockSpec((rows, 1), lambda j, k: (j, 0)),
                  small((emb_pad, fh)), small((1, fh)), small((fh, fh)), small((1, fh)), small((1, fh)),
                  w3_spec(0), w3_spec(1), w3_spec(2), w3_spec(3),
                  pl.BlockSpec((1, cb), lambda j, k: (0, k)),
                  pl.BlockSpec((ch, 2 * n2, n2), lambda j, k: (j, 0, 0))],
        out_specs=pl.BlockSpec((2, ch, 2 * n2, cb // 2), lambda j, k: (0, j, 0, k)),
        scratch_shapes=[pltpu.VMEM((rows, fh), BF16)],
        compiler_params=_params(("parallel", "arbitrary"), big=True),
        name="hyena_filter_stage1",
    )(z, t, fw1p, fb1.reshape(1, fh).astype(F32), fw2.astype(F32), fb2.reshape(1, fh).astype(F32),
      freq.reshape(1, fh).astype(F32), fw3, fw3, fw3, fw3, deltas, f1_real)


def _dft_constants():
    n1, n2 = FFT_N1, FFT_N2
    n = n1 * n2
    a1 = np.arange(n1)
    a2 = np.arange(n2)
    half = n2 // 2

    def cplx_block(w):
        return np.block([[w.real, -w.imag], [w.imag, w.real]])

    ang = (a2[None, :, None] * a2[None, None, :] / n2) + (a1[:, None, None] * a2[None, :, None] / n)
    w1 = np.exp(-2j * np.pi * ang)
    f1_pad = np.stack([cplx_block(w1[i][:, :half]) for i in range(n1)])
    f1_real = np.stack([np.concatenate([w1[i].real, w1[i].imag], axis=0) for i in range(n1)])
    w2 = np.exp(-2j * np.pi * (a1[:, None] * a1[None, :]) / n1)
    f2 = cplx_block(w2)
    ang = (a1[None, :, None] * a1[None, None, :] / n1) + (a2[:, None, None] * a1[None, :, None] / n)
    g2 = np.stack([cplx_block(m_) for m_ in np.exp(2j * np.pi * ang)])
    wg1 = np.exp(2j * np.pi * (a2[:half, None] * a2[None, :]) / n2) / n
    g1 = cplx_block(wg1)
    to = lambda x: jnp.asarray(x.astype(np.float32)).astype(BF16)
    return to(f1_pad), to(f1_real), to(f2), to(g2), to(g1)


def _fft_stage1_block(x_re, x_im, f_ref, y_ref):
    xr = pltpu.einshape("mjc->jmc", x_re)
    xi = pltpu.einshape("mjc->jmc", x_im)
    for j in range(FFT_CHUNK):
        xc = jnp.concatenate([xr[j], xi[j]], axis=0).astype(BF16)
        y_ref[j] = _pack_bf16_pairs(jnp.dot(f_ref[j], xc, preferred_element_type=F32))


def _fft_stage1_kernel(x_ref, f_ref, y_ref):
    _fft_stage1_block(x_ref[0], x_ref[1], f_ref, y_ref)


def _fft_stage2_kernel(yr_ref, yi_ref, f2_ref, h_ref):
    yr = pltpu.einshape("nkc->knc", yr_ref[...])
    yi = pltpu.einshape("nkc->knc", yi_ref[...])
    for j in range(FFT_CHUNK):
        yc = _unpack_bf16_pairs(jnp.concatenate([yr[j], yi[j]], axis=0))
        h_ref[j] = _pack_bf16_pairs(jnp.dot(f2_ref[...], yc, preferred_element_type=F32))


def _fft_mid_kernel(yr_ref, yi_ref, f2_ref, h_ref, g2_ref, q_ref):
    yr = pltpu.einshape("nkc->knc", yr_ref[...])
    yi = pltpu.einshape("nkc->knc", yi_ref[...])
    n1 = FFT_N1
    spec = [jnp.dot(f2_ref[...], _unpack_bf16_pairs(jnp.concatenate([yr[j], yi[j]], axis=0)),
                    preferred_element_type=F32) for j in range(FFT_CHUNK)]
    prod = []
    for j, z in enumerate(spec):
        hf = _unpack_pairs_f32(h_ref[j])
        zr, zi, hr, hi = z[:n1], z[n1:], hf[:n1], hf[n1:]
        prod.append(jnp.concatenate([zr * hr - zi * hi, zr * hi + zi * hr], axis=0).astype(BF16))
    for j, pc in enumerate(prod):
        q_ref[j] = _pack_bf16_pairs(jnp.dot(g2_ref[j], pc, preferred_element_type=F32))


def _fft_last_kernel(qr_ref, qi_ref, g1_ref, gate_ref, zf_ref, fb_ref, *rest, feeds_next):
    f1_ref, o_ref, y_ref = rest if feeds_next else (None, rest[0], None)
    qr = pltpu.einshape("kjc->jkc", qr_ref[...])
    qi = pltpu.einshape("kjc->jkc", qi_ref[...])
    half = FFT_N2 // 2
    conv = []
    for j in range(FFT_CHUNK):
        qc = _unpack_bf16_pairs(jnp.concatenate([qr[j], qi[j]], axis=0))
        conv.append(jnp.dot(g1_ref[...], qc, preferred_element_type=F32))
    conv = jnp.stack(conv, axis=0)
    fb = fb_ref[...]
    out = []
    for bi in range(2):
        cb = pltpu.einshape("jmc->mjc", conv[:, bi * half:(bi + 1) * half])
        out.append(gate_ref[bi] * (cb + zf_ref[bi] * fb))
        o_ref[bi] = out[bi]
    if feeds_next:
        _fft_stage1_block(out[0], out[1], f1_ref, y_ref)


def hyena_filter_spectrum(y, consts, cb=FFT_CB):
    _, _, f2, _, _ = consts
    no = y.shape[0]
    c = 2 * y.shape[-1]
    n1, n2, ch = FFT_N1, FFT_N2, FFT_CHUNK
    nk = n2 // ch
    return pl.pallas_call(
        _fft_stage2_kernel,
        out_shape=jax.ShapeDtypeStruct((no, n2, 2 * n1, c // 2), jnp.uint32),
        grid=(no, nk, c // cb),
        in_specs=[pl.BlockSpec((None, n1, ch, cb // 2), lambda o, j, k: (o, 0, j, k)),
                  pl.BlockSpec((None, n1, ch, cb // 2), lambda o, j, k: (o, 0, nk + j, k)),
                  pl.BlockSpec((2 * n1, 2 * n1), lambda o, j, k: (0, 0))],
        out_specs=pl.BlockSpec((None, ch, 2 * n1, cb // 2), lambda o, j, k: (o, j, 0, k)),
        compiler_params=_params(("parallel", "parallel", "parallel"), big=True),
        name="fft_filter_stage2",
    )(y, y, f2)


def hyena_long_conv_gate(zsrc, z_col, gate_src, gate_col, hf, order, fbias, consts, y=None,
                         feeds_next=False, cb=FFT_CB):
    f1_pad, _, f2, g2, g1 = consts
    n1, n2, ch = FFT_N1, FFT_N2, FFT_CHUNK
    c = 2 * hf.shape[-1]
    ncb = c // cb
    half = n2 // 2
    y_shape = jax.ShapeDtypeStruct((n1, 2 * n2, c // 2), jnp.uint32)
    f1_spec = pl.BlockSpec((ch, 2 * n2, n2), lambda j, k: (j, 0, 0))
    y_spec = pl.BlockSpec((ch, 2 * n2, cb // 2), lambda j, k: (j, 0, k))
    if y is None:
        y = pl.pallas_call(
            _fft_stage1_kernel,
            out_shape=y_shape,
            grid=(n1 // ch, ncb),
            in_specs=[pl.BlockSpec((2, half, ch, cb), lambda j, k: (0, 0, j, z_col * ncb + k)), f1_spec],
            out_specs=y_spec,
            compiler_params=_params(("parallel", "parallel"), big=True),
            name="fft_stage1",
        )(zsrc, f1_pad)
    nk = n2 // ch
    q = pl.pallas_call(
        _fft_mid_kernel,
        out_shape=jax.ShapeDtypeStruct((n2, 2 * n1, c // 2), jnp.uint32),
        grid=(nk, ncb),
        in_specs=[pl.BlockSpec((n1, ch, cb // 2), lambda j, k: (0, j, k)),
                  pl.BlockSpec((n1, ch, cb // 2), lambda j, k: (0, nk + j, k)),
                  pl.BlockSpec((2 * n1, 2 * n1), lambda j, k: (0, 0)),
                  pl.BlockSpec((None, ch, 2 * n1, cb // 2), lambda j, k: (order, j, 0, k)),
                  pl.BlockSpec((ch, 2 * n1, 2 * n1), lambda j, k: (j, 0, 0))],
        out_specs=pl.BlockSpec((ch, 2 * n1, cb // 2), lambda j, k: (j, 0, k)),
        compiler_params=_params(("parallel", "parallel"), big=True),
        name="fft_mid",
    )(y, y, f2, hf, g2)
    nj = n1 // ch
    in_specs = [pl.BlockSpec((n2, ch, cb // 2), lambda j, k: (0, j, k)),
                pl.BlockSpec((n2, ch, cb // 2), lambda j, k: (0, nj + j, k)),
                pl.BlockSpec((2 * half, 2 * n2), lambda j, k: (0, 0)),
                pl.BlockSpec((2, half, ch, cb), lambda j, k: (0, 0, j, gate_col * ncb + k)),
                pl.BlockSpec((2, half, ch, cb), lambda j, k: (0, 0, j, z_col * ncb + k)),
                pl.BlockSpec((None, 1, cb), lambda j, k: (order, 0, k))]
    args = [q, q, g1, gate_src, zsrc, fbias.reshape(fbias.shape[0], 1, c)]
    out_shape = jax.ShapeDtypeStruct((2, half, n1, c), F32)
    out_spec = pl.BlockSpec((2, half, ch, cb), lambda j, k: (0, 0, j, k))
    if feeds_next:
        in_specs.append(f1_spec)
        args.append(f1_pad)
        out_shape, out_spec = (out_shape, y_shape), (out_spec, y_spec)
    res = pl.pallas_call(
        functools.partial(_fft_last_kernel, feeds_next=feeds_next),
        out_shape=out_shape,
        grid=(nj, ncb),
        in_specs=in_specs,
        out_specs=out_spec,
        compiler_params=_params(("parallel", "parallel"), big=True),
        name="fft_last",
    )(*args)
    return res if feeds_next else (res, None)


def hyena_mixer(hn, h_res, w_in, b_in, conv_w, conv_b, fw1, fb1, fw2, fb2, freq, fw3, fbias, w_out,
                batch, seq):
    width = w_out.shape[0]
    assert batch == 2 and 2 * seq == FFT_N1 * FFT_N2
    sc = hyena_in_conv(hn, w_in, b_in, conv_w, conv_b, seq)
    consts = _dft_constants()
    hf = hyena_filter_spectrum(hyena_filter_stage1(seq, fw1, fb1, fw2, fb2, freq, fw3, width, consts),
                               consts)
    sc4 = sc.reshape(batch, FFT_N2 // 2, FFT_N1, 3 * width)
    zf1, y1 = hyena_long_conv_gate(sc4, 2, sc4, 0, hf, 0, fbias, consts, feeds_next=True)
    zf2, _ = hyena_long_conv_gate(zf1, 0, sc4, 1, hf, 1, fbias, consts, y=y1)
    return matmul([zf2.reshape(batch * seq, width)], w_out, res=h_res, tm=512, tn=1024,
                  name="hyena_out")


def moe_swiglu(hn2_packed, logits, wg, wu, wd):
    n = hn2_packed.shape[0]
    top_v, top_i = lax.top_k(logits, TOP_K)
    gates = jax.nn.softmax(top_v, axis=-1)
    e_flat = top_i.reshape(-1).astype(jnp.int32)
    nk = n * TOP_K
    onehot = (e_flat[:, None] == jnp.arange(N_EXPERTS, dtype=jnp.int32)[None]).astype(jnp.int32)
    csum = jnp.cumsum(onehot, axis=0)
    rank = jnp.take_along_axis(csum, e_flat[:, None], axis=1)[:, 0] - 1
    counts = csum[-1]
    padded = ((counts + MOE_TILE - 1) // MOE_TILE) * MOE_TILE
    pad_end = jnp.cumsum(padded)
    pad_start = pad_end - padded
    dest = pad_start[e_flat] + rank
    p_rows = nk + N_EXPERTS * MOE_TILE
    order = jnp.argsort(e_flat, stable=True).astype(jnp.int32)
    nf = wg.shape[2] // MOE_F_TILE
    sorted_tok = jnp.pad(order // TOP_K, (0, gather_rows_per_tile(MOE_TILE, nf)))
    start = jnp.cumsum(counts) - counts
    nt = p_rows // MOE_TILE
    tile_start = jnp.arange(nt, dtype=jnp.int32) * MOE_TILE
    tile_used = tile_start < pad_end[-1]
    tile_exp = jnp.minimum(jnp.searchsorted(pad_end, tile_start, side='right'), N_EXPERTS - 1).astype(jnp.int32)
    tile_rows = jnp.clip(counts[tile_exp] - (tile_start - pad_start[tile_exp]), 0, MOE_TILE)
    tile_rows = jnp.where(tile_used, tile_rows, 0).astype(jnp.int32)
    last_exp = tile_exp[jnp.maximum(jnp.sum(tile_used.astype(jnp.int32)) - 1, 0)]
    tile_src = jnp.where(tile_used, start[tile_exp] + (tile_start - pad_start[tile_exp]), 0)
    tile_exp = jnp.where(tile_used, tile_exp, last_exp)
    ys = swiglu(hn2_packed, wg, wu, wd, tile_exp, tile_rows, tile_src.astype(jnp.int32), sorted_tok,
                tf=MOE_F_TILE, sub_rows=FFN_SUB_ROWS, name="swiglu_experts")
    return ys, dest.reshape(n, TOP_K), gates


def kernel(x, p, ln_mix, ln_ffn, ln_ple, final_norm, t5_bias, w_attn_in, w_attn_out, attn_sink, na_rpb, w_ffn_gate, w_ffn_up, w_ffn_down, w_hy_in, b_hy_in, w_hy_conv, b_hy_conv, w_hy_f1, b_hy_f1, w_hy_f2, b_hy_f2, hy_freq, w_hy_f3, hy_bias, w_hy_out, w_router, w_exp_gate, w_exp_up, w_exp_down, w_ple_proj, w_ple_gate):
    batch, seq, d = x.shape
    n = batch * seq
    depth = ln_mix.shape[0]
    h = x.reshape(n, d)
    for i in range(depth):
        li = i // 2
        hn = rmsnorm(h, ln_mix[i], BF16)
        if i % 2 == 0:
            na_off = A_WIDTH + 2 * A_KV_WIDTH
            w_in = w_attn_in[li]
            proj_a = matmul([hn], w_in[:, :na_off], out_dtype=BF16, tn=na_off, name="attn_in_a")
            proj_n = matmul([hn], w_in[:, na_off:], out_dtype=BF16, tn=3 * B_WIDTH // 2, name="attn_in_n")
            oa = window_attention(proj_a.reshape(batch, seq, -1), t5_bias, attn_sink[li])
            ob = neighbourhood_attention(proj_n.reshape(batch, seq, -1), na_rpb[li])
            h = matmul([oa.reshape(n, A_WIDTH), ob.reshape(n, B_WIDTH)], w_attn_out[li], res=h,
                       name="attn_out")
            nt = n // MOE_TILE
            h = swiglu(h, w_ffn_gate, w_ffn_up, w_ffn_down,
                       jnp.full((nt,), li, jnp.int32), jnp.full((nt,), MOE_TILE, jnp.int32),
                       gain=ln_ffn[i], name="swiglu_dense")
        else:
            h = hyena_mixer(hn, h, w_hy_in[li], b_hy_in[li], w_hy_conv[li], b_hy_conv[li],
                            w_hy_f1[li], b_hy_f1[li], w_hy_f2[li], b_hy_f2[li], hy_freq[li],
                            w_hy_f3[li], hy_bias[li], w_hy_out[li], batch, seq)
            wr_pad = jnp.pad(w_router[li].astype(F32), ((0, 0), (0, V7X_LANES - N_EXPERTS)))
            hn2_packed, logits = rmsnorm_router(h, ln_ffn[i], wr_pad)
            ys, dest2, gates = moe_swiglu(hn2_packed, logits[:, :N_EXPERTS], w_exp_gate[li],
                                          w_exp_up[li], w_exp_down[li])
            h = moe_combine(h, ys, dest2, gates)
        h = ple(h, p.reshape(depth, n, -1), ln_ple[i], w_ple_gate, w_ple_proj, i)
    return rmsnorm(h, final_norm, F32).reshape(batch, seq, d)
```

```python
import functools
import math

import jax
import jax.numpy as jnp
import numpy as np
from jax import lax
from jax.experimental import pallas as pl
from jax.experimental.pallas import tpu as pltpu

F32 = jnp.float32
BF16 = jnp.bfloat16
NEG_INF = -1e30
RMS_EPS = 1e-6

V7X_VMEM_LIMIT_BYTES = 56 * 1024 * 1024
V7X_LANES = 128

HEAD_DIM = 64
A_Q_HEADS = 16
A_KV_HEADS = 2
A_GROUP = A_Q_HEADS // A_KV_HEADS
A_BLOCK = 128
T5_BUCKETS = 32
T5_MAX_DIST = 128
B_HEADS = 16
GRID_W = 64
NA_WIN_H = 8
NA_WIN_W = 16
A_WIDTH = A_Q_HEADS * HEAD_DIM
A_KV_WIDTH = A_KV_HEADS * HEAD_DIM
B_WIDTH = B_HEADS * HEAD_DIM
N_EXPERTS = 8
TOP_K = 2
HYENA_EMB = 33
HYENA_DECAY_TARGET = 1e-2
HYENA_FAST_PCT = 0.3
HYENA_SLOW_PCT = 1.5

FFT_N1 = 64
FFT_N2 = 128
FFT_CHUNK = 8
FFT_CB = 1024

MOE_TILE = 1024
FFN_F_TILE = 512
MOE_F_TILE = 512
FFN_SUB_ROWS = 256
FFN_OUT_CHUNKS = 4
DMA_LOOP_UNROLL = 8


def _params(semantics, big=False):
    return pltpu.CompilerParams(
        dimension_semantics=semantics,
        vmem_limit_bytes=V7X_VMEM_LIMIT_BYTES if big else None)


def _rmsnorm_kernel(x_ref, g_ref, o_ref):
    x = x_ref[...]
    y = x * lax.rsqrt(jnp.mean(x * x, axis=-1, keepdims=True) + RMS_EPS)
    o_ref[...] = (y * g_ref[...]).astype(o_ref.dtype)


def rmsnorm(x2, g, out_dtype, tm=1024):
    n, d = x2.shape
    return pl.pallas_call(
        _rmsnorm_kernel,
        out_shape=jax.ShapeDtypeStruct((n, d), out_dtype),
        grid=(n // tm,),
        in_specs=[pl.BlockSpec((tm, d), lambda i: (i, 0)),
                  pl.BlockSpec((1, d), lambda i: (0, 0))],
        out_specs=pl.BlockSpec((tm, d), lambda i: (i, 0)),
        compiler_params=_params(("parallel",), big=True),
        name="rmsnorm",
    )(x2, g.reshape(1, d))


def _pack_bf16_pairs(y):
    w = y.shape[1] // 2
    bits = pltpu.bitcast(y.astype(BF16).astype(F32), jnp.uint32)
    return (bits[:, :w] >> 16) | (bits[:, w:] & jnp.uint32(0xFFFF0000))


def _unpack_pairs_f32(p):
    lo = pltpu.bitcast(p << 16, F32)
    hi = pltpu.bitcast(p & jnp.uint32(0xFFFF0000), F32)
    return jnp.concatenate([lo, hi], axis=1)


def _unpack_bf16_pairs(p):
    return _unpack_pairs_f32(p).astype(BF16)


def _rmsnorm_router_kernel(x_ref, g_ref, wr_ref, o_ref, l_ref):
    x = x_ref[...]
    y = x * lax.rsqrt(jnp.mean(x * x, axis=-1, keepdims=True) + RMS_EPS)
    y = y * g_ref[...]
    o_ref[...] = _pack_bf16_pairs(y)
    w = wr_ref[...]
    y_hi, w_hi = y.astype(BF16), w.astype(BF16)
    y_lo = (y - y_hi.astype(F32)).astype(BF16)
    w_lo = (w - w_hi.astype(F32)).astype(BF16)
    l_ref[...] = (jnp.dot(y_hi, w_hi, preferred_element_type=F32)
                  + jnp.dot(y_lo, w_hi, preferred_element_type=F32)
                  + jnp.dot(y_hi, w_lo, preferred_element_type=F32))


def rmsnorm_router(x2, g, w_router_pad, tm=512):
    n, d = x2.shape
    ne = w_router_pad.shape[1]
    return pl.pallas_call(
        _rmsnorm_router_kernel,
        out_shape=(jax.ShapeDtypeStruct((n, d // 2), jnp.uint32), jax.ShapeDtypeStruct((n, ne), F32)),
        grid=(n // tm,),
        in_specs=[pl.BlockSpec((tm, d), lambda i: (i, 0)),
                  pl.BlockSpec((1, d), lambda i: (0, 0)),
                  pl.BlockSpec((d, ne), lambda i: (0, 0))],
        out_specs=(pl.BlockSpec((tm, d // 2), lambda i: (i, 0)),
                   pl.BlockSpec((tm, ne), lambda i: (i, 0))),
        compiler_params=_params(("parallel",)),
        name="rmsnorm_router",
    )(x2, g.reshape(1, d), w_router_pad)


def _mm_kernel(*refs, n_a, has_res):
    a_refs = refs[:n_a]
    w_refs = refs[n_a:2 * n_a]
    idx = 2 * n_a
    res_ref = refs[idx] if has_res else None
    idx += int(has_res)
    o_ref = refs[idx]
    wbf_refs = refs[idx + 1:idx + 1 + n_a]

    @pl.when(pl.program_id(1) == 0)
    def _():
        for w_ref, wbf_ref in zip(w_refs, wbf_refs):
            wbf_ref[...] = w_ref[...].astype(BF16)

    acc = None
    for a_ref, wbf_ref in zip(a_refs, wbf_refs):
        d = jnp.dot(a_ref[...].astype(BF16), wbf_ref[...], preferred_element_type=F32)
        acc = d if acc is None else acc + d
    if has_res:
        acc = acc + res_ref[...]
    o_ref[...] = acc.astype(o_ref.dtype)


def matmul(a_list, w, *, res=None, out_dtype=F32, tm=1024, tn=1024, name="matmul"):
    m = a_list[0].shape[0]
    n = w.shape[1]
    n_a = len(a_list)
    k_each = a_list[0].shape[1]
    assert all(a.shape == (m, k_each) for a in a_list) and w.shape[0] == n_a * k_each
    in_specs = [pl.BlockSpec((tm, k_each), lambda j, i: (i, 0)) for _ in a_list]
    in_specs += [pl.BlockSpec((k_each, tn), functools.partial(lambda j, i, kb: (kb, j), kb=kb))
                 for kb in range(n_a)]
    args = list(a_list) + [w] * n_a
    if res is not None:
        in_specs.append(pl.BlockSpec((tm, tn), lambda j, i: (i, j)))
        args.append(res)
    return pl.pallas_call(
        functools.partial(_mm_kernel, n_a=n_a, has_res=res is not None),
        out_shape=jax.ShapeDtypeStruct((m, n), out_dtype),
        grid=(n // tn, m // tm),
        in_specs=in_specs,
        out_specs=pl.BlockSpec((tm, tn), lambda j, i: (i, j)),
        scratch_shapes=[pltpu.VMEM((k_each, tn), BF16) for _ in a_list],
        compiler_params=_params(("parallel", "arbitrary"), big=True),
        name=name,
    )(*args)


def _rms_scale_bf16(x, g):
    y = x * lax.rsqrt(jnp.mean(x * x, axis=-1, keepdims=True) + RMS_EPS)
    return (y * g).astype(BF16)


def _ple_kernel(h_ref, p_ref, g_ref, wg_ref, wp_ref, o_ref, wg_bf, wp_bf):
    @pl.when(pl.program_id(1) == 0)
    def _():
        wg_bf[...] = wg_ref[...].astype(BF16)
        wp_bf[...] = wp_ref[...].astype(BF16)

    tn = o_ref.shape[1]
    a = jnp.dot(_rms_scale_bf16(h_ref[...], g_ref[...]), wg_bf[...], preferred_element_type=F32)
    pp = jnp.dot(p_ref[...].astype(BF16), wp_bf[...], preferred_element_type=F32)
    col = pl.multiple_of(pl.program_id(0) * tn, tn)
    o_ref[...] = h_ref[:, pl.ds(col, tn)] + jax.nn.sigmoid(a) * pp


def ple(h2, p_all, g, w_gate_all, w_proj_all, layer, tm=512, tn=1024):
    m, d = h2.shape
    pd = p_all.shape[2]
    return pl.pallas_call(
        _ple_kernel,
        out_shape=jax.ShapeDtypeStruct((m, d), F32),
        grid=(d // tn, m // tm),
        in_specs=[pl.BlockSpec((tm, d), lambda j, i: (i, 0)),
                  pl.BlockSpec((None, tm, pd), lambda j, i: (layer, i, 0)),
                  pl.BlockSpec((1, d), lambda j, i: (0, 0)),
                  pl.BlockSpec((None, d, tn), lambda j, i: (layer, 0, j)),
                  pl.BlockSpec((None, pd, tn), lambda j, i: (layer, 0, j))],
        out_specs=pl.BlockSpec((tm, tn), lambda j, i: (i, j)),
        scratch_shapes=[pltpu.VMEM((d, tn), BF16), pltpu.VMEM((pd, tn), BF16)],
        compiler_params=_params(("parallel", "arbitrary"), big=True),
        name="ple",
    )(h2, p_all, g.reshape(1, d), w_gate_all, w_proj_all)


def _row_copy(src_hbm, row, dst, i, sem):
    return pltpu.make_async_copy(src_hbm.at[pl.ds(row, 1), :], dst.at[pl.ds(i, 1), :], sem)


def _swiglu_kernel(exp_ref, rows_ref, src_ref, tok_ref, x_ref, *rest, n_col, gather, sub_rows, nf):
    del exp_ref
    gain_ref = None if gather else rest[0]
    wg_ref, wu_ref, wd_ref, o_ref = rest[0 if gather else 1:][:4]
    scratch = rest[(4 if gather else 5):]
    t = pl.program_id(0)
    f = pl.program_id(1)
    nt = pl.num_programs(0)
    n_rows = rows_ref[t]
    tm = o_ref.shape[0]

    if not gather:
        (xbf,) = scratch

        @pl.when(f == 0)
        def _():
            h = x_ref[...]
            o_ref[...] = h
            xbf[...] = _rms_scale_bf16(h, gain_ref[...])

    if gather:
        @pl.when(f == 0)
        def _():
            o_ref[...] = jnp.zeros_like(o_ref)

        xbuf, xbf, sems = scratch
        slot = t % 2
        n_fetch = xbuf.shape[1]
        per_step = n_fetch // nf

        def start_row(tile, sl, i):
            _row_copy(x_ref, tok_ref[src_ref[tile] + i], xbuf.at[sl], i, sems.at[sl, i]).start()

        def wait_all(sl):
            def body(c, carry):
                for r in range(DMA_LOOP_UNROLL):
                    i = c * DMA_LOOP_UNROLL + r
                    _row_copy(x_ref, 0, xbuf.at[sl], i, sems.at[sl, i]).wait()
                return carry
            lax.fori_loop(0, n_fetch // DMA_LOOP_UNROLL, body, 0)

        @pl.when(jnp.logical_and(f == 0, t == 0))
        def _():
            def body(i, carry):
                start_row(0, 0, i)
                return carry
            lax.fori_loop(0, n_fetch, body, 0, unroll=DMA_LOOP_UNROLL)

        fed = jnp.logical_or(t == 0, rows_ref[jnp.maximum(t - 1, 0)] > 0)

        @pl.when(jnp.logical_and(f == 0, fed))
        def _():
            wait_all(slot)
            xbf[...] = _unpack_bf16_pairs(xbuf[slot, 0:tm])

        def fetch_ahead():
            nxt = jnp.minimum(t + 1, nt - 1)
            for r in range(per_step):
                start_row(nxt, 1 - slot, f * per_step + r)
    else:
        def fetch_ahead():
            pass

    def ffn_rows(n):
        fetch_ahead()
        x = xbf[0:n, :]
        g = jnp.dot(x, wg_ref[...].astype(BF16), preferred_element_type=F32)
        u = jnp.dot(x, wu_ref[...].astype(BF16), preferred_element_type=F32)
        hmid = (g * jax.nn.sigmoid(g) * u).astype(BF16)
        wd = wd_ref[...].astype(BF16)
        cw = o_ref.shape[1] // n_col
        for c in range(n_col):
            y = jnp.dot(hmid, wd[:, c * cw:(c + 1) * cw], preferred_element_type=F32)
            o_ref[0:n, c * cw:(c + 1) * cw] += y

    for k in range(1, tm // sub_rows + 1):
        covers = jnp.logical_and(n_rows > (k - 1) * sub_rows, n_rows <= k * sub_rows)
        pl.when(covers)(functools.partial(ffn_rows, k * sub_rows))

    if gather:
        @pl.when(jnp.logical_and(n_rows > 0, jnp.logical_and(t == nt - 1, f == nf - 1)))
        def _():
            wait_all(1 - slot)


def gather_rows_per_tile(tm, nf):
    per_step = -(-tm // nf)
    while (per_step * nf) % DMA_LOOP_UNROLL:
        per_step += 1
    return per_step * nf


def swiglu(x, w_gate, w_up, w_down, tile_exp, tile_rows, tile_src=None, src_rows=None, *, gain=None,
           tm=MOE_TILE, tf=FFN_F_TILE, sub_rows=MOE_TILE, name="swiglu"):
    gather = src_rows is not None
    assert gather != (gain is not None)
    m = tile_rows.shape[0] * tm
    d = w_gate.shape[1]
    dff = w_gate.shape[2]
    nf = dff // tf
    nt = m // tm
    assert nf >= 2

    def w_in_map(t, f, exp_ref, rows_ref, src_ref, tok_ref):
        return (exp_ref[t], 0, jnp.where(rows_ref[t] > 0, f, nf - 1))

    def w_out_map(t, f, exp_ref, rows_ref, src_ref, tok_ref):
        return (exp_ref[t], jnp.where(rows_ref[t] > 0, f, nf - 1), 0)

    def tile_map(t, f, exp_ref, rows_ref, src_ref, tok_ref):
        return (t, 0)

    w_specs = [pl.BlockSpec((None, d, tf), w_in_map),
               pl.BlockSpec((None, d, tf), w_in_map),
               pl.BlockSpec((None, tf, d), w_out_map)]
    if gather:
        in_specs = [pl.BlockSpec(memory_space=pl.ANY)] + w_specs
        args = [x, w_gate, w_up, w_down]
        n_fetch = gather_rows_per_tile(tm, nf)
        scratch = [pltpu.VMEM((2, n_fetch, d // 2), jnp.uint32), pltpu.VMEM((tm, d), BF16),
                   pltpu.SemaphoreType.DMA((2, n_fetch))]
    else:
        in_specs = [pl.BlockSpec((tm, d), tile_map, pipeline_mode=pl.Buffered(1)),
                    pl.BlockSpec((1, d), lambda t, f, e, r, s, k: (0, 0))] + w_specs
        args = [x, gain.reshape(1, d), w_gate, w_up, w_down]
        scratch = [pltpu.VMEM((tm, d), BF16)]
        tile_src = jnp.zeros((nt,), jnp.int32)
        src_rows = jnp.zeros((1,), jnp.int32)
    grid_spec = pltpu.PrefetchScalarGridSpec(
        num_scalar_prefetch=4,
        grid=(nt, nf),
        in_specs=in_specs,
        out_specs=pl.BlockSpec((tm, d), tile_map, pipeline_mode=pl.Buffered(1)),
        scratch_shapes=scratch,
    )
    return pl.pallas_call(
        functools.partial(_swiglu_kernel, n_col=FFN_OUT_CHUNKS, gather=gather, sub_rows=sub_rows, nf=nf),
        out_shape=jax.ShapeDtypeStruct((m, d), F32),
        grid_spec=grid_spec,
        compiler_params=_params(("arbitrary", "arbitrary"), big=True),
        name=name,
    )(tile_exp, tile_rows, tile_src, src_rows, *args)


COMBINE_TILE = 256


def _combine_kernel(d0_ref, d1_ref, ys_hbm, gate_ref, h_ref, hout_ref, buf, sems):
    i = pl.program_id(0)
    n = pl.num_programs(0)
    tt = h_ref.shape[0]
    slot = i % 2
    dests = (d0_ref, d1_ref)

    def start_tile(tile, sl):
        def body(r, c):
            for k, d_ref in enumerate(dests):
                _row_copy(ys_hbm, d_ref[tile * tt + r], buf.at[sl, k], r, sems.at[sl, k, r]).start()
            return c
        lax.fori_loop(0, tt, body, 0, unroll=DMA_LOOP_UNROLL)

    @pl.when(i == 0)
    def _():
        start_tile(0, 0)

    @pl.when(i + 1 < n)
    def _():
        start_tile(i + 1, 1 - slot)

    def wait_body(r, c):
        for k in range(TOP_K):
            _row_copy(ys_hbm, 0, buf.at[slot, k], r, sems.at[slot, k, r]).wait()
        return c
    lax.fori_loop(0, tt, wait_body, 0, unroll=DMA_LOOP_UNROLL)

    gates = gate_ref[...]
    hout_ref[...] = h_ref[...] + (buf[slot, 0] * gates[:, 0:1] + buf[slot, 1] * gates[:, 1:2])


def moe_combine(h2, ys, dest2, gates, tt=COMBINE_TILE):
    n, d = h2.shape
    grid_spec = pltpu.PrefetchScalarGridSpec(
        num_scalar_prefetch=2,
        grid=(n // tt,),
        in_specs=[pl.BlockSpec(memory_space=pl.ANY),
                  pl.BlockSpec((tt, TOP_K), lambda i, a, b: (i, 0)),
                  pl.BlockSpec((tt, d), lambda i, a, b: (i, 0))],
        out_specs=pl.BlockSpec((tt, d), lambda i, a, b: (i, 0)),
        scratch_shapes=[pltpu.VMEM((2, TOP_K, tt, d), F32), pltpu.SemaphoreType.DMA((2, TOP_K, tt))],
    )
    return pl.pallas_call(
        _combine_kernel,
        out_shape=jax.ShapeDtypeStruct((n, d), F32),
        grid_spec=grid_spec,
        compiler_params=_params(("arbitrary",), big=True),
        name="moe_combine",
    )(dest2[:, 0], dest2[:, 1], ys, gates, h2)


def _t5_bucket(rel):
    half = T5_BUCKETS // 2
    max_exact = half // 2
    n = jnp.abs(rel)
    log_ratio = jnp.log(jnp.maximum(n, 1).astype(F32) / max_exact) / math.log(T5_MAX_DIST / max_exact)
    large = jnp.minimum(max_exact + (log_ratio * (half - max_exact)).astype(jnp.int32), half - 1)
    return jnp.where(rel > 0, half, 0) + jnp.where(n < max_exact, n, large)


def _window_bias_table(t5_bias):
    i = jnp.arange(A_BLOCK)[:, None]
    j = jnp.arange(3 * A_BLOCK)[None, :]
    rel = j - A_BLOCK - i
    onehot = (_t5_bucket(rel)[None] == jnp.arange(T5_BUCKETS)[:, None, None]).astype(F32)
    bias = jnp.einsum('bh,bij->hij', t5_bias.astype(F32), onehot, precision=lax.Precision.HIGHEST)
    return jnp.where((jnp.abs(rel) <= A_BLOCK)[None], bias, NEG_INF)


def _window_kernel(sink_ref, q_ref, kv_ref, bias_ref, o_ref, *, nb):
    n = pl.program_id(1)
    scale = HEAD_DIM ** -0.5
    starts = (jnp.maximum(n - 1, 0), n, jnp.minimum(n + 1, nb - 1))
    kv = [kv_ref[pl.ds(pl.multiple_of(s * A_BLOCK, A_BLOCK), A_BLOCK), :] for s in starts]
    col = lax.broadcasted_iota(jnp.int32, (A_BLOCK, 3 * A_BLOCK), 1)
    edge_ok = jnp.logical_and(jnp.logical_or(n > 0, col >= A_BLOCK),
                              jnp.logical_or(n < nb - 1, col < 2 * A_BLOCK))
    kv = jnp.concatenate(kv, axis=0)
    q = q_ref[...]
    lo = lax.broadcasted_iota(jnp.int32, (1, 2 * HEAD_DIM), 1) < HEAD_DIM
    sel = (jnp.where(lo, scale, 0.0).astype(BF16), jnp.where(lo, 0.0, scale).astype(BF16))
    pairs_per_kv = A_GROUP // 2
    kk, vv = [], []
    for kh in range(A_KV_HEADS):
        k1 = kv[:, kh * HEAD_DIM:(kh + 1) * HEAD_DIM]
        v1 = kv[:, A_KV_WIDTH + kh * HEAD_DIM:A_KV_WIDTH + (kh + 1) * HEAD_DIM]
        kk.append(jnp.concatenate([k1, k1], axis=1))
        vv.append(jnp.concatenate([v1, v1], axis=1))
    scores = []
    for pr in range(A_Q_HEADS // 2):
        q2 = q[:, pr * 2 * HEAD_DIM:(pr + 1) * 2 * HEAD_DIM]
        qq = jnp.concatenate([q2 * sel[0], q2 * sel[1]], axis=0)
        s2 = lax.dot_general(qq, kk[pr // pairs_per_kv], (((1,), (1,)), ((), ())),
                             preferred_element_type=F32)
        for i in range(2):
            scores.append(jnp.where(edge_ok, s2[i * A_BLOCK:(i + 1) * A_BLOCK] + bias_ref[2 * pr + i],
                                    NEG_INF))
    probs, dens = [], []
    for h, s in enumerate(scores):
        sink = sink_ref[h]
        mx = jnp.maximum(jnp.max(s, axis=-1, keepdims=True), sink)
        p = jnp.exp(s - mx)
        dens.append(jnp.sum(p, axis=-1, keepdims=True) + jnp.exp(sink - mx))
        probs.append(p.astype(BF16))
    for pr in range(A_Q_HEADS // 2):
        v2 = vv[pr // pairs_per_kv]
        oa = jnp.dot(probs[2 * pr], v2, preferred_element_type=F32) / dens[2 * pr]
        ob = jnp.dot(probs[2 * pr + 1], v2, preferred_element_type=F32) / dens[2 * pr + 1]
        o_ref[:, pr * 2 * HEAD_DIM:(pr + 1) * 2 * HEAD_DIM] = jnp.where(lo, oa, ob).astype(o_ref.dtype)


def window_attention(proj_a, t5_bias, sink):
    b, s, _ = proj_a.shape
    nb = s // A_BLOCK
    kvw = 2 * A_KV_WIDTH
    grid_spec = pltpu.PrefetchScalarGridSpec(
        num_scalar_prefetch=0,
        grid=(b, nb),
        in_specs=[pl.BlockSpec(memory_space=pltpu.SMEM),
                  pl.BlockSpec((None, A_BLOCK, A_WIDTH), lambda bi, n: (bi, n, 0)),
                  pl.BlockSpec((None, s, kvw), lambda bi, n: (bi, 0, A_WIDTH // kvw)),
                  pl.BlockSpec((A_Q_HEADS, A_BLOCK, 3 * A_BLOCK), lambda bi, n: (0, 0, 0))],
        out_specs=pl.BlockSpec((None, A_BLOCK, A_WIDTH), lambda bi, n: (bi, n, 0)),
    )
    return pl.pallas_call(
        functools.partial(_window_kernel, nb=nb),
        out_shape=jax.ShapeDtypeStruct((b, s, A_WIDTH), BF16),
        grid_spec=grid_spec,
        compiler_params=_params(("parallel", "arbitrary")),
        name="window_attention",
    )(sink.astype(F32), proj_a, proj_a, _window_bias_table(t5_bias))


NA_HEAD_GROUP = 16


def _na_bias_table(rpb, rows):
    kh = min(NA_WIN_H, rows)
    kw = NA_WIN_W
    c = jnp.arange(GRID_W)
    cs = jnp.clip(c - kw // 2, 0, GRID_W - kw)
    col_ok = (c[None] >= cs[:, None]) & (c[None] < cs[:, None] + kw)
    col_off = jnp.clip(c[None] - c[:, None], -(kw - 1), kw - 1) + kw - 1
    hi = lax.Precision.HIGHEST
    row_off = jnp.arange(kh)[None, :] - jnp.arange(kh)[:, None] + NA_WIN_H - 1
    row_sel = (row_off[:, :, None] == jnp.arange(2 * NA_WIN_H - 1)[None, None, :]).astype(F32)
    col_sel = (col_off[None] == jnp.arange(2 * kw - 1)[:, None, None]).astype(F32)
    by_row = jnp.einsum('hrc,dir->dhic', rpb.astype(F32), row_sel, precision=hi)
    bias = jnp.einsum('dhic,cqk->dhqik', by_row, col_sel, precision=hi)
    bias = jnp.where(col_ok[None, None, :, None, :], bias, NEG_INF)
    return bias.reshape(kh, rpb.shape[0], GRID_W, kh * GRID_W)


def _na_kernel(q_ref, k_ref, v_ref, bias_ref, o_ref, *, rows):
    r = pl.program_id(2)
    kh = min(NA_WIN_H, rows)
    scale = HEAD_DIM ** -0.5
    start = pl.multiple_of(jnp.clip(r - kh // 2, 0, rows - kh) * GRID_W, GRID_W)
    k = k_ref[pl.ds(start, kh * GRID_W), :]
    v = v_ref[pl.ds(start, kh * GRID_W), :]
    q = q_ref[...]
    lo = lax.broadcasted_iota(jnp.int32, (1, 2 * HEAD_DIM), 1) < HEAD_DIM
    sel = (jnp.where(lo, scale, 0.0).astype(BF16), jnp.where(lo, 0.0, scale).astype(BF16))
    n_pairs = NA_HEAD_GROUP // 2
    scores = []
    for pr in range(n_pairs):
        sl = slice(pr * 2 * HEAD_DIM, (pr + 1) * 2 * HEAD_DIM)
        q2 = q[:, sl]
        qq = jnp.concatenate([q2 * sel[0], q2 * sel[1]], axis=0)
        s2 = lax.dot_general(qq, k[:, sl], (((1,), (1,)), ((), ())), preferred_element_type=F32)
        for i in range(2):
            scores.append(s2[i * GRID_W:(i + 1) * GRID_W] + bias_ref[2 * pr + i])
    probs, dens = [], []
    for s in scores:
        mx = jnp.max(s, axis=-1, keepdims=True)
        p = jnp.exp(s - mx)
        dens.append(jnp.sum(p, axis=-1, keepdims=True))
        probs.append(p.astype(BF16))
    for pr in range(n_pairs):
        sl = slice(pr * 2 * HEAD_DIM, (pr + 1) * 2 * HEAD_DIM)
        oa = jnp.dot(probs[2 * pr], v[:, sl], preferred_element_type=F32) / dens[2 * pr]
        ob = jnp.dot(probs[2 * pr + 1], v[:, sl], preferred_element_type=F32) / dens[2 * pr + 1]
        o_ref[:, sl] = jnp.where(lo, oa, ob).astype(o_ref.dtype)


def neighbourhood_attention(proj_n, rpb):
    b, s, _ = proj_n.shape
    rows = s // GRID_W
    kh = min(NA_WIN_H, rows)
    gw = NA_HEAD_GROUP * HEAD_DIM
    ng = B_WIDTH // gw

    def bias_map(bi, g, r):
        return (r - jnp.clip(r - kh // 2, 0, rows - kh), g, 0, 0)

    return pl.pallas_call(
        functools.partial(_na_kernel, rows=rows),
        out_shape=jax.ShapeDtypeStruct((b, s, B_WIDTH), BF16),
        grid=(b, ng, rows),
        in_specs=[pl.BlockSpec((None, GRID_W, gw), lambda bi, g, r: (bi, r, g)),
                  pl.BlockSpec((None, s, gw), lambda bi, g, r: (bi, 0, ng + g)),
                  pl.BlockSpec((None, s, gw), lambda bi, g, r: (bi, 0, 2 * ng + g)),
                  pl.BlockSpec((None, NA_HEAD_GROUP, GRID_W, kh * GRID_W), bias_map)],
        out_specs=pl.BlockSpec((None, GRID_W, gw), lambda bi, g, r: (bi, r, g)),
        compiler_params=_params(("parallel", "parallel", "arbitrary"), big=True),
        name="neighbourhood_attention",
    )(proj_n, proj_n, proj_n, _na_bias_table(rpb, rows))


CONV_HALO = 16
CONV_COL_CHUNK = 512


def _hyena_in_kernel(prev_ref, cur_ref, next_ref, w_ref, b_ref, cw_ref, cb_ref, o_ref, wbf_ref, *,
                     tiles_per_seq):
    i = pl.program_id(1)

    @pl.when(i == 0)
    def _():
        wbf_ref[...] = w_ref[...].astype(BF16)

    tm = cur_ref.shape[0]
    rows = tm + 2 * CONV_HALO
    a = jnp.concatenate([prev_ref[...], cur_ref[...], next_ref[...]], axis=0)
    pos = i % tiles_per_seq
    row = lax.broadcasted_iota(jnp.int32, (rows, 1), 0)
    kill_up = jnp.logical_and(pos == 0, row == CONV_HALO)
    kill_dn = jnp.logical_and(pos == tiles_per_seq - 1, row == CONV_HALO + tm - 1)
    for c in range(o_ref.shape[1] // CONV_COL_CHUNK):
        cs = slice(c * CONV_COL_CHUNK, (c + 1) * CONV_COL_CHUNK)
        p = jnp.dot(a, wbf_ref[:, cs], preferred_element_type=F32) + b_ref[:, cs]
        up = jnp.where(kill_up, 0.0, pltpu.roll(p, 1, 0))
        dn = jnp.where(kill_dn, 0.0, pltpu.roll(p, rows - 1, 0))
        y = up * cw_ref[0:1, cs] + p * cw_ref[1:2, cs] + dn * cw_ref[2:3, cs] + cb_ref[:, cs]
        o_ref[:, cs] = y[CONV_HALO:CONV_HALO + tm]


def hyena_in_conv(hn, w_in, b_in, conv_w, conv_b, seq, tm=1024, tn=1024):
    m, k = hn.shape
    n = w_in.shape[1]
    hb = tm // CONV_HALO
    last_halo = m // CONV_HALO - 1
    return pl.pallas_call(
        functools.partial(_hyena_in_kernel, tiles_per_seq=seq // tm),
        out_shape=jax.ShapeDtypeStruct((m, n), F32),
        grid=(n // tn, m // tm),
        in_specs=[pl.BlockSpec((CONV_HALO, k), lambda j, i: (jnp.maximum(i * hb - 1, 0), 0)),
                  pl.BlockSpec((tm, k), lambda j, i: (i, 0)),
                  pl.BlockSpec((CONV_HALO, k), lambda j, i: (jnp.minimum((i + 1) * hb, last_halo), 0)),
                  pl.BlockSpec((k, tn), lambda j, i: (0, j)),
                  pl.BlockSpec((1, tn), lambda j, i: (0, j)),
                  pl.BlockSpec((3, tn), lambda j, i: (0, j)),
                  pl.BlockSpec((1, tn), lambda j, i: (0, j))],
        out_specs=pl.BlockSpec((tm, tn), lambda j, i: (i, j)),
        scratch_shapes=[pltpu.VMEM((k, tn), BF16)],
        compiler_params=_params(("parallel", "arbitrary"), big=True),
        name="hyena_in_conv",
    )(hn, hn, hn, w_in, b_in.reshape(1, n), conv_w, conv_b.reshape(1, n))


def _filter_stage1_kernel(z_ref, t_ref, fw1_ref, fb1_ref, fw2_ref, fb2_ref, fr_ref, w3f0_ref, w3f1_ref,
                          w3b0_ref, w3b1_ref, delta_ref, f1_ref, y_ref, hid_ref, *, seq):
    j = pl.program_id(0)

    @pl.when(pl.program_id(1) == 0)
    def _():
        hi = lax.Precision.HIGHEST
        fr = fr_ref[...]
        hid = jnp.sin(fr * (jnp.dot(z_ref[...], fw1_ref[...], preferred_element_type=F32, precision=hi)
                            + fb1_ref[...]))
        hid = jnp.sin(fr * (jnp.dot(hid, fw2_ref[...], preferred_element_type=F32, precision=hi)
                            + fb2_ref[...]))
        hid_ref[...] = hid.astype(BF16)

    hid_bf = hid_ref[...]
    r = lax.broadcasted_iota(jnp.int32, (hid_bf.shape[0], 1), 0)
    u = j * FFT_CHUNK + r // FFT_N2 + FFT_N1 * (r % FFT_N2)
    decay = jnp.where(u == seq, 0.0, jnp.exp(-t_ref[...] * delta_ref[...]))
    for o, (wf_ref, wb_ref) in enumerate(((w3f0_ref, w3b0_ref), (w3f1_ref, w3b1_ref))):
        fwd = jnp.dot(hid_bf, wf_ref[...].astype(BF16), preferred_element_type=F32)
        bwd = jnp.dot(hid_bf, wb_ref[...].astype(BF16), preferred_element_type=F32)
        taps = (jnp.where(u > seq, bwd, jnp.where(u == 0, fwd + bwd, fwd)) * decay).astype(BF16)
        for jj in range(FFT_CHUNK):
            x = taps[jj * FFT_N2:(jj + 1) * FFT_N2]
            y_ref[o, jj] = _pack_bf16_pairs(jnp.dot(f1_ref[jj], x, preferred_element_type=F32))


def hyena_filter_stage1(seq, fw1, fb1, fw2, fb2, freq, fw3, width, consts, cb=FFT_CB):
    _, f1_real, _, _, _ = consts
    n = 2 * seq
    n1, n2, ch = FFT_N1, FFT_N2, FFT_CHUNK
    assert fw3.shape[1] == 4 * width
    r = jnp.arange(n)
    u = (r // (ch * n2)) * ch + (r // n2) % ch + n1 * (r % n2)
    pos = jnp.where(u <= seq, u, n - u)
    pos = jnp.where(u == seq, 0, pos)
    bands = (HYENA_EMB - 1) // 2
    f = jnp.linspace(1e-4, bands - 1, bands, dtype=F32)[None]
    t = (pos.astype(F32) / (seq - 1))[:, None]
    w = ((2.0 * math.pi / seq) * pos.astype(F32))[:, None]
    z = jnp.concatenate([t, jnp.cos(f * w), -jnp.sin(f * w)], axis=-1)
    emb_pad = 40
    z = jnp.pad(z, ((0, 0), (0, emb_pad - HYENA_EMB)))
    fw1p = jnp.pad(fw1.astype(F32), ((0, emb_pad - HYENA_EMB), (0, 0)))
    fh = fw1.shape[1]
    max_decay = math.log(HYENA_DECAY_TARGET) / HYENA_FAST_PCT
    min_decay = math.log(HYENA_DECAY_TARGET) / HYENA_SLOW_PCT
    deltas = jnp.abs(jnp.linspace(min_decay, max_decay, width, dtype=F32))[None]
    ncb = width // cb
    rows = ch * n2

    def small(shape):
        return pl.BlockSpec(shape, lambda j, k: (0, 0))

    def w3_spec(blk):
        return pl.BlockSpec((fh, cb), lambda j, k: (0, blk * ncb + k))

    return pl.pallas_call(
        functools.partial(_filter_stage1_kernel, seq=seq),
        out_shape=jax.ShapeDtypeStruct((2, n1, 2 * n2, width // 2), jnp.uint32),
        grid=(n1 // ch, ncb),
        in_specs=[pl.BlockSpec((rows, emb_pad), lambda j, k: (j, 0)),
                  pl.BlockSpec((rows, 1), lambda j, k: (j, 0)),
                  small((emb_pad, fh)), small((1, fh)), small((fh, fh)), small((1, fh)), small((1, fh)),
                  w3_spec(0), w3_spec(1), w3_spec(2), w3_spec(3),
                  pl.BlockSpec((1, cb), lambda j, k: (0, k)),
                  pl.BlockSpec((ch, 2 * n2, n2), lambda j, k: (j, 0, 0))],
        out_specs=pl.BlockSpec((2, ch, 2 * n2, cb // 2), lambda j, k: (0, j, 0, k)),
        scratch_shapes=[pltpu.VMEM((rows, fh), BF16)],
        compiler_params=_params(("parallel", "arbitrary"), big=True),
        name="hyena_filter_stage1",
    )(z, t, fw1p, fb1.reshape(1, fh).astype(F32), fw2.astype(F32), fb2.reshape(1, fh).astype(F32),
      freq.reshape(1, fh).astype(F32), fw3, fw3, fw3, fw3, deltas, f1_real)


def _dft_constants():
    n1, n2 = FFT_N1, FFT_N2
    n = n1 * n2
    a1 = np.arange(n1)
    a2 = np.arange(n2)
    half = n2 // 2

    def cplx_block(w):
        return np.block([[w.real, -w.imag], [w.imag, w.real]])

    ang = (a2[None, :, None] * a2[None, None, :] / n2) + (a1[:, None, None] * a2[None, :, None] / n)
    w1 = np.exp(-2j * np.pi * ang)
    f1_pad = np.stack([cplx_block(w1[i][:, :half]) for i in range(n1)])
    f1_real = np.stack([np.concatenate([w1[i].real, w1[i].imag], axis=0) for i in range(n1)])
    w2 = np.exp(-2j * np.pi * (a1[:, None] * a1[None, :]) / n1)
    f2 = cplx_block(w2)
    ang = (a1[None, :, None] * a1[None, None, :] / n1) + (a2[:, None, None] * a1[None, :, None] / n)
    g2 = np.stack([cplx_block(m_) for m_ in np.exp(2j * np.pi * ang)])
    wg1 = np.exp(2j * np.pi * (a2[:half, None] * a2[None, :]) / n2) / n
    g1 = cplx_block(wg1)
    to = lambda x: jnp.asarray(x.astype(np.float32)).astype(BF16)
    return to(f1_pad), to(f1_real), to(f2), to(g2), to(g1)


def _fft_stage1_block(x_re, x_im, f_ref, y_ref):
    xr = pltpu.einshape("mjc->jmc", x_re)
    xi = pltpu.einshape("mjc->jmc", x_im)
    for j in range(FFT_CHUNK):
        xc = jnp.concatenate([xr[j], xi[j]], axis=0).astype(BF16)
        y_ref[j] = _pack_bf16_pairs(jnp.dot(f_ref[j], xc, preferred_element_type=F32))


def _fft_stage1_kernel(x_ref, f_ref, y_ref):
    _fft_stage1_block(x_ref[0], x_ref[1], f_ref, y_ref)


def _fft_stage2_kernel(yr_ref, yi_ref, f2_ref, h_ref):
    yr = pltpu.einshape("nkc->knc", yr_ref[...])
    yi = pltpu.einshape("nkc->knc", yi_ref[...])
    for j in range(FFT_CHUNK):
        yc = _unpack_bf16_pairs(jnp.concatenate([yr[j], yi[j]], axis=0))
        h_ref[j] = _pack_bf16_pairs(jnp.dot(f2_ref[...], yc, preferred_element_type=F32))


def _fft_mid_kernel(yr_ref, yi_ref, f2_ref, h_ref, g2_ref, q_ref):
    yr = pltpu.einshape("nkc->knc", yr_ref[...])
    yi = pltpu.einshape("nkc->knc", yi_ref[...])
    n1 = FFT_N1
    spec = [jnp.dot(f2_ref[...], _unpack_bf16_pairs(jnp.concatenate([yr[j], yi[j]], axis=0)),
                    preferred_element_type=F32) for j in range(FFT_CHUNK)]
    prod = []
    for j, z in enumerate(spec):
        hf = _unpack_pairs_f32(h_ref[j])
        zr, zi, hr, hi = z[:n1], z[n1:], hf[:n1], hf[n1:]
        prod.append(jnp.concatenate([zr * hr - zi * hi, zr * hi + zi * hr], axis=0).astype(BF16))
    for j, pc in enumerate(prod):
        q_ref[j] = _pack_bf16_pairs(jnp.dot(g2_ref[j], pc, preferred_element_type=F32))


def _fft_last_kernel(qr_ref, qi_ref, g1_ref, gate_ref, zf_ref, fb_ref, *rest, feeds_next):
    f1_ref, o_ref, y_ref = rest if feeds_next else (None, rest[0], None)
    qr = pltpu.einshape("kjc->jkc", qr_ref[...])
    qi = pltpu.einshape("kjc->jkc", qi_ref[...])
    half = FFT_N2 // 2
    conv = []
    for j in range(FFT_CHUNK):
        qc = _unpack_bf16_pairs(jnp.concatenate([qr[j], qi[j]], axis=0))
        conv.append(jnp.dot(g1_ref[...], qc, preferred_element_type=F32))
    conv = jnp.stack(conv, axis=0)
    fb = fb_ref[...]
    out = []
    for bi in range(2):
        cb = pltpu.einshape("jmc->mjc", conv[:, bi * half:(bi + 1) * half])
        out.append(gate_ref[bi] * (cb + zf_ref[bi] * fb))
        o_ref[bi] = out[bi]
    if feeds_next:
        _fft_stage1_block(out[0], out[1], f1_ref, y_ref)


def hyena_filter_spectrum(y, consts, cb=FFT_CB):
    _, _, f2, _, _ = consts
    no = y.shape[0]
    c = 2 * y.shape[-1]
    n1, n2, ch = FFT_N1, FFT_N2, FFT_CHUNK
    nk = n2 // ch
    return pl.pallas_call(
        _fft_stage2_kernel,
        out_shape=jax.ShapeDtypeStruct((no, n2, 2 * n1, c // 2), jnp.uint32),
        grid=(no, nk, c // cb),
        in_specs=[pl.BlockSpec((None, n1, ch, cb // 2), lambda o, j, k: (o, 0, j, k)),
                  pl.BlockSpec((None, n1, ch, cb // 2), lambda o, j, k: (o, 0, nk + j, k)),
                  pl.BlockSpec((2 * n1, 2 * n1), lambda o, j, k: (0, 0))],
        out_specs=pl.BlockSpec((None, ch, 2 * n1, cb // 2), lambda o, j, k: (o, j, 0, k)),
        compiler_params=_params(("parallel", "parallel", "parallel"), big=True),
        name="fft_filter_stage2",
    )(y, y, f2)


def hyena_long_conv_gate(zsrc, z_col, gate_src, gate_col, hf, order, fbias, consts, y=None,
                         feeds_next=False, cb=FFT_CB):
    f1_pad, _, f2, g2, g1 = consts
    n1, n2, ch = FFT_N1, FFT_N2, FFT_CHUNK
    c = 2 * hf.shape[-1]
    ncb = c // cb
    half = n2 // 2
    y_shape = jax.ShapeDtypeStruct((n1, 2 * n2, c // 2), jnp.uint32)
    f1_spec = pl.BlockSpec((ch, 2 * n2, n2), lambda j, k: (j, 0, 0))
    y_spec = pl.BlockSpec((ch, 2 * n2, cb // 2), lambda j, k: (j, 0, k))
    if y is None:
        y = pl.pallas_call(
            _fft_stage1_kernel,
            out_shape=y_shape,
            grid=(n1 // ch, ncb),
            in_specs=[pl.BlockSpec((2, half, ch, cb), lambda j, k: (0, 0, j, z_col * ncb + k)), f1_spec],
            out_specs=y_spec,
            compiler_params=_params(("parallel", "parallel"), big=True),
            name="fft_stage1",
        )(zsrc, f1_pad)
    nk = n2 // ch
    q = pl.pallas_call(
        _fft_mid_kernel,
        out_shape=jax.ShapeDtypeStruct((n2, 2 * n1, c // 2), jnp.uint32),
        grid=(nk, ncb),
        in_specs=[pl.BlockSpec((n1, ch, cb // 2), lambda j, k: (0, j, k)),
                  pl.BlockSpec((n1, ch, cb // 2), lambda j, k: (0, nk + j, k)),
                  pl.BlockSpec((2 * n1, 2 * n1), lambda j, k: (0, 0)),
                  pl.BlockSpec((None, ch, 2 * n1, cb // 2), lambda j, k: (order, j, 0, k)),
                  pl.BlockSpec((ch, 2 * n1, 2 * n1), lambda j, k: (j, 0, 0))],
        out_specs=pl.BlockSpec((ch, 2 * n1, cb // 2), lambda j, k: (j, 0, k)),
        compiler_params=_params(("parallel", "parallel"), big=True),
        name="fft_mid",
    )(y, y, f2, hf, g2)
    nj = n1 // ch
    in_specs = [pl.BlockSpec((n2, ch, cb // 2), lambda j, k: (0, j, k)),
                pl.BlockSpec((n2, ch, cb // 2), lambda j, k: (0, nj + j, k)),
                pl.BlockSpec((2 * half, 2 * n2), lambda j, k: (0, 0)),
                pl.BlockSpec((2, half, ch, cb), lambda j, k: (0, 0, j, gate_col * ncb + k)),
                pl.BlockSpec((2, half, ch, cb), lambda j, k: (0, 0, j, z_col * ncb + k)),
                pl.BlockSpec((None, 1, cb), lambda j, k: (order, 0, k))]
    args = [q, q, g1, gate_src, zsrc, fbias.reshape(fbias.shape[0], 1, c)]
    out_shape = jax.ShapeDtypeStruct((2, half, n1, c), F32)
    out_spec = pl.BlockSpec((2, half, ch, cb), lambda j, k: (0, 0, j, k))
    if feeds_next:
        in_specs.append(f1_spec)
        args.append(f1_pad)
        out_shape, out_spec = (out_shape, y_shape), (out_spec, y_spec)
    res = pl.pallas_call(
        functools.partial(_fft_last_kernel, feeds_next=feeds_next),
        out_shape=out_shape,
        grid=(nj, ncb),
        in_specs=in_specs,
        out_specs=out_spec,
        compiler_params=_params(("parallel", "parallel"), big=True),
        name="fft_last",
    )(*args)
    return res if feeds_next else (res, None)


def hyena_mixer(hn, h_res, w_in, b_in, conv_w, conv_b, fw1, fb1, fw2, fb2, freq, fw3, fbias, w_out,
                batch, seq):
    width = w_out.shape[0]
    assert batch == 2 and 2 * seq == FFT_N1 * FFT_N2
    sc = hyena_in_conv(hn, w_in, b_in, conv_w, conv_b, seq)
    consts = _dft_constants()
    hf = hyena_filter_spectrum(hyena_filter_stage1(seq, fw1, fb1, fw2, fb2, freq, fw3, width, consts),
                               consts)
    sc4 = sc.reshape(batch, FFT_N2 // 2, FFT_N1, 3 * width)
    zf1, y1 = hyena_long_conv_gate(sc4, 2, sc4, 0, hf, 0, fbias, consts, feeds_next=True)
    zf2, _ = hyena_long_conv_gate(zf1, 0, sc4, 1, hf, 1, fbias, consts, y=y1)
    return matmul([zf2.reshape(batch * seq, width)], w_out, res=h_res, tm=512, tn=1024,
                  name="hyena_out")


def moe_swiglu(hn2_packed, logits, wg, wu, wd):
    n = hn2_packed.shape[0]
    top_v, top_i = lax.top_k(logits, TOP_K)
    gates = jax.nn.softmax(top_v, axis=-1)
    e_flat = top_i.reshape(-1).astype(jnp.int32)
    nk = n * TOP_K
    onehot = (e_flat[:, None] == jnp.arange(N_EXPERTS, dtype=jnp.int32)[None]).astype(jnp.int32)
    csum = jnp.cumsum(onehot, axis=0)
    rank = jnp.take_along_axis(csum, e_flat[:, None], axis=1)[:, 0] - 1
    counts = csum[-1]
    padded = ((counts + MOE_TILE - 1) // MOE_TILE) * MOE_TILE
    pad_end = jnp.cumsum(padded)
    pad_start = pad_end - padded
    dest = pad_start[e_flat] + rank
    p_rows = nk + N_EXPERTS * MOE_TILE
    order = jnp.argsort(e_flat, stable=True).astype(jnp.int32)
    nf = wg.shape[2] // MOE_F_TILE
    sorted_tok = jnp.pad(order // TOP_K, (0, gather_rows_per_tile(MOE_TILE, nf)))
    start = jnp.cumsum(counts) - counts
    nt = p_rows // MOE_TILE
    tile_start = jnp.arange(nt, dtype=jnp.int32) * MOE_TILE
    tile_used = tile_start < pad_end[-1]
    tile_exp = jnp.minimum(jnp.searchsorted(pad_end, tile_start, side='right'), N_EXPERTS - 1).astype(jnp.int32)
    tile_rows = jnp.clip(counts[tile_exp] - (tile_start - pad_start[tile_exp]), 0, MOE_TILE)
    tile_rows = jnp.where(tile_used, tile_rows, 0).astype(jnp.int32)
    last_exp = tile_exp[jnp.maximum(jnp.sum(tile_used.astype(jnp.int32)) - 1, 0)]
    tile_src = jnp.where(tile_used, start[tile_exp] + (tile_start - pad_start[tile_exp]), 0)
    tile_exp = jnp.where(tile_used, tile_exp, last_exp)
    ys = swiglu(hn2_packed, wg, wu, wd, tile_exp, tile_rows, tile_src.astype(jnp.int32), sorted_tok,
                tf=MOE_F_TILE, sub_rows=FFN_SUB_ROWS, name="swiglu_experts")
    return ys, dest.reshape(n, TOP_K), gates


def kernel(x, p, ln_mix, ln_ffn, ln_ple, final_norm, t5_bias, w_attn_in, w_attn_out, attn_sink, na_rpb, w_ffn_gate, w_ffn_up, w_ffn_down, w_hy_in, b_hy_in, w_hy_conv, b_hy_conv, w_hy_f1, b_hy_f1, w_hy_f2, b_hy_f2, hy_freq, w_hy_f3, hy_bias, w_hy_out, w_router, w_exp_gate, w_exp_up, w_exp_down, w_ple_proj, w_ple_gate):
    batch, seq, d = x.shape
    n = batch * seq
    depth = ln_mix.shape[0]
    h = x.reshape(n, d)
    for i in range(depth):
        li = i // 2
        hn = rmsnorm(h, ln_mix[i], BF16)
        if i % 2 == 0:
            na_off = A_WIDTH + 2 * A_KV_WIDTH
            w_in = w_attn_in[li]
            proj_a = matmul([hn], w_in[:, :na_off], out_dtype=BF16, tn=na_off, name="attn_in_a")
            proj_n = matmul([hn], w_in[:, na_off:], out_dtype=BF16, tn=3 * B_WIDTH // 2, name="attn_in_n")
            oa = window_attention(proj_a.reshape(batch, seq, -1), t5_bias, attn_sink[li])
            ob = neighbourhood_attention(proj_n.reshape(batch, seq, -1), na_rpb[li])
            h = matmul([oa.reshape(n, A_WIDTH), ob.reshape(n, B_WIDTH)], w_attn_out[li], res=h,
                       name="attn_out")
            nt = n // MOE_TILE
            h = swiglu(h, w_ffn_gate, w_ffn_up, w_ffn_down,
                       jnp.full((nt,), li, jnp.int32), jnp.full((nt,), MOE_TILE, jnp.int32),
                       gain=ln_ffn[i], name="swiglu_dense")
        else:
            h = hyena_mixer(hn, h, w_hy_in[li], b_hy_in[li], w_hy_conv[li], b_hy_conv[li],
                            w_hy_f1[li], b_hy_f1[li], w_hy_f2[li], b_hy_f2[li], hy_freq[li],
                            w_hy_f3[li], hy_bias[li], w_hy_out[li], batch, seq)
            wr_pad = jnp.pad(w_router[li].astype(F32), ((0, 0), (0, V7X_LANES - N_EXPERTS)))
            hn2_packed, logits = rmsnorm_router(h, ln_ffn[i], wr_pad)
            ys, dest2, gates = moe_swiglu(hn2_packed, logits[:, :N_EXPERTS], w_exp_gate[li],
                                          w_exp_up[li], w_exp_down[li])
            h = moe_combine(h, ys, dest2, gates)
        h = ple(h, p.reshape(depth, n, -1), ln_ple[i], w_ple_gate, w_ple_proj, i)
    return rmsnorm(h, final_norm, F32).reshape(batch, seq, d)
```

```python
import functools
import math

import jax
import jax.numpy as jnp
import numpy as np
from jax import lax
from jax.experimental import pallas as pl
from jax.experimental.pallas import tpu as pltpu

F32 = jnp.float32
BF16 = jnp.bfloat16
NEG_INF = -1e30
RMS_EPS = 1e-6

V7X_VMEM_LIMIT_BYTES = 56 * 1024 * 1024
V7X_LANES = 128

HEAD_DIM = 64
A_Q_HEADS = 16
A_KV_HEADS = 2
A_GROUP = A_Q_HEADS // A_KV_HEADS
A_BLOCK = 128
T5_BUCKETS = 32
T5_MAX_DIST = 128
B_HEADS = 16
GRID_W = 64
NA_WIN_H = 8
NA_WIN_W = 16
A_WIDTH = A_Q_HEADS * HEAD_DIM
A_KV_WIDTH = A_KV_HEADS * HEAD_DIM
B_WIDTH = B_HEADS * HEAD_DIM
N_EXPERTS = 8
TOP_K = 2
HYENA_EMB = 33
HYENA_DECAY_TARGET = 1e-2
HYENA_FAST_PCT = 0.3
HYENA_SLOW_PCT = 1.5

FFT_N1 = 64
FFT_N2 = 128
FFT_CHUNK = 8
FFT_CB = 1024

MOE_TILE = 1024
FFN_F_TILE = 512
MOE_F_TILE = 512
FFN_SUB_ROWS = 256
FFN_OUT_CHUNKS = 4
DMA_LOOP_UNROLL = 8


def _params(semantics, big=False):
    return pltpu.CompilerParams(
        dimension_semantics=semantics,
        vmem_limit_bytes=V7X_VMEM_LIMIT_BYTES if big else None)


def _rmsnorm_kernel(x_ref, g_ref, o_ref):
    x = x_ref[...]
    y = x * lax.rsqrt(jnp.mean(x * x, axis=-1, keepdims=True) + RMS_EPS)
    o_ref[...] = (y * g_ref[...]).astype(o_ref.dtype)


def rmsnorm(x2, g, out_dtype, tm=1024):
    n, d = x2.shape
    return pl.pallas_call(
        _rmsnorm_kernel,
        out_shape=jax.ShapeDtypeStruct((n, d), out_dtype),
        grid=(n // tm,),
        in_specs=[pl.BlockSpec((tm, d), lambda i: (i, 0)),
                  pl.BlockSpec((1, d), lambda i: (0, 0))],
        out_specs=pl.BlockSpec((tm, d), lambda i: (i, 0)),
        compiler_params=_params(("parallel",), big=True),
        name="rmsnorm",
    )(x2, g.reshape(1, d))


def _pack_bf16_pairs(y):
    w = y.shape[1] // 2
    bits = pltpu.bitcast(y.astype(BF16).astype(F32), jnp.uint32)
    return (bits[:, :w] >> 16) | (bits[:, w:] & jnp.uint32(0xFFFF0000))


def _unpack_pairs_f32(p):
    lo = pltpu.bitcast(p << 16, F32)
    hi = pltpu.bitcast(p & jnp.uint32(0xFFFF0000), F32)
    return jnp.concatenate([lo, hi], axis=1)


def _unpack_bf16_pairs(p):
    return _unpack_pairs_f32(p).astype(BF16)


def _rmsnorm_router_kernel(x_ref, g_ref, wr_ref, o_ref, l_ref):
    x = x_ref[...]
    y = x * lax.rsqrt(jnp.mean(x * x, axis=-1, keepdims=True) + RMS_EPS)
    y = y * g_ref[...]
    o_ref[...] = _pack_bf16_pairs(y)
    w = wr_ref[...]
    y_hi, w_hi = y.astype(BF16), w.astype(BF16)
    y_lo = (y - y_hi.astype(F32)).astype(BF16)
    w_lo = (w - w_hi.astype(F32)).astype(BF16)
    l_ref[...] = (jnp.dot(y_hi, w_hi, preferred_element_type=F32)
                  + jnp.dot(y_lo, w_hi, preferred_element_type=F32)
                  + jnp.dot(y_hi, w_lo, preferred_element_type=F32))


def rmsnorm_router(x2, g, w_router_pad, tm=512):
    n, d = x2.shape
    ne = w_router_pad.shape[1]
    return pl.pallas_call(
        _rmsnorm_router_kernel,
        out_shape=(jax.ShapeDtypeStruct((n, d // 2), jnp.uint32), jax.ShapeDtypeStruct((n, ne), F32)),
        grid=(n // tm,),
        in_specs=[pl.BlockSpec((tm, d), lambda i: (i, 0)),
                  pl.BlockSpec((1, d), lambda i: (0, 0)),
                  pl.BlockSpec((d, ne), lambda i: (0, 0))],
        out_specs=(pl.BlockSpec((tm, d // 2), lambda i: (i, 0)),
                   pl.BlockSpec((tm, ne), lambda i: (i, 0))),
        compiler_params=_params(("parallel",)),
        name="rmsnorm_router",
    )(x2, g.reshape(1, d), w_router_pad)


def _mm_kernel(*refs, n_a, has_res):
    a_refs = refs[:n_a]
    w_refs = refs[n_a:2 * n_a]
    idx = 2 * n_a
    res_ref = refs[idx] if has_res else None
    idx += int(has_res)
    o_ref = refs[idx]
    wbf_refs = refs[idx + 1:idx + 1 + n_a]

    @pl.when(pl.program_id(1) == 0)
    def _():
        for w_ref, wbf_ref in zip(w_refs, wbf_refs):
            wbf_ref[...] = w_ref[...].astype(BF16)

    acc = None
    for a_ref, wbf_ref in zip(a_refs, wbf_refs):
        d = jnp.dot(a_ref[...].astype(BF16), wbf_ref[...], preferred_element_type=F32)
        acc = d if acc is None else acc + d
    if has_res:
        acc = acc + res_ref[...]
    o_ref[...] = acc.astype(o_ref.dtype)


def matmul(a_list, w, *, res=None, out_dtype=F32, tm=1024, tn=1024, name="matmul"):
    m = a_list[0].shape[0]
    n = w.shape[1]
    n_a = len(a_list)
    k_each = a_list[0].shape[1]
    assert all(a.shape == (m, k_each) for a in a_list) and w.shape[0] == n_a * k_each
    in_specs = [pl.BlockSpec((tm, k_each), lambda j, i: (i, 0)) for _ in a_list]
    in_specs += [pl.BlockSpec((k_each, tn), functools.partial(lambda j, i, kb: (kb, j), kb=kb))
                 for kb in range(n_a)]
    args = list(a_list) + [w] * n_a
    if res is not None:
        in_specs.append(pl.BlockSpec((tm, tn), lambda j, i: (i, j)))
        args.append(res)
    return pl.pallas_call(
        functools.partial(_mm_kernel, n_a=n_a, has_res=res is not None),
        out_shape=jax.ShapeDtypeStruct((m, n), out_dtype),
        grid=(n // tn, m // tm),
        in_specs=in_specs,
        out_specs=pl.BlockSpec((tm, tn), lambda j, i: (i, j)),
        scratch_shapes=[pltpu.VMEM((k_each, tn), BF16) for _ in a_list],
        compiler_params=_params(("parallel", "arbitrary"), big=True),
        name=name,
    )(*args)


def _rms_scale_bf16(x, g):
    y = x * lax.rsqrt(jnp.mean(x * x, axis=-1, keepdims=True) + RMS_EPS)
    return (y * g).astype(BF16)


def _ple_kernel(h_ref, p_ref, g_ref, wg_ref, wp_ref, o_ref, wg_bf, wp_bf):
    @pl.when(pl.program_id(1) == 0)
    def _():
        wg_bf[...] = wg_ref[...].astype(BF16)
        wp_bf[...] = wp_ref[...].astype(BF16)

    tn = o_ref.shape[1]
    a = jnp.dot(_rms_scale_bf16(h_ref[...], g_ref[...]), wg_bf[...], preferred_element_type=F32)
    pp = jnp.dot(p_ref[...].astype(BF16), wp_bf[...], preferred_element_type=F32)
    col = pl.multiple_of(pl.program_id(0) * tn, tn)
    o_ref[...] = h_ref[:, pl.ds(col, tn)] + jax.nn.sigmoid(a) * pp


def ple(h2, p_all, g, w_gate_all, w_proj_all, layer, tm=512, tn=1024):
    m, d = h2.shape
    pd = p_all.shape[2]
    return pl.pallas_call(
        _ple_kernel,
        out_shape=jax.ShapeDtypeStruct((m, d), F32),
        grid=(d // tn, m // tm),
        in_specs=[pl.BlockSpec((tm, d), lambda j, i: (i, 0)),
                  pl.BlockSpec((None, tm, pd), lambda j, i: (layer, i, 0)),
                  pl.BlockSpec((1, d), lambda j, i: (0, 0)),
                  pl.BlockSpec((None, d, tn), lambda j, i: (layer, 0, j)),
                  pl.BlockSpec((None, pd, tn), lambda j, i: (layer, 0, j))],
        out_specs=pl.BlockSpec((tm, tn), lambda j, i: (i, j)),
        scratch_shapes=[pltpu.VMEM((d, tn), BF16), pltpu.VMEM((pd, tn), BF16)],
        compiler_params=_params(("parallel", "arbitrary"), big=True),
        name="ple",
    )(h2, p_all, g.reshape(1, d), w_gate_all, w_proj_all)


def _row_copy(src_hbm, row, dst, i, sem):
    return pltpu.make_async_copy(src_hbm.at[pl.ds(row, 1), :], dst.at[pl.ds(i, 1), :], sem)


def _swiglu_kernel(exp_ref, rows_ref, src_ref, tok_ref, x_ref, *rest, n_col, gather, sub_rows, nf):
    del exp_ref
    gain_ref = None if gather else rest[0]
    wg_ref, wu_ref, wd_ref, o_ref = rest[0 if gather else 1:][:4]
    scratch = rest[(4 if gather else 5):]
    t = pl.program_id(0)
    f = pl.program_id(1)
    nt = pl.num_programs(0)
    n_rows = rows_ref[t]
    tm = o_ref.shape[0]

    if not gather:
        (xbf,) = scratch

        @pl.when(f == 0)
        def _():
            h = x_ref[...]
            o_ref[...] = h
            xbf[...] = _rms_scale_bf16(h, gain_ref[...])

    if gather:
        @pl.when(f == 0)
        def _():
            o_ref[...] = jnp.zeros_like(o_ref)

        xbuf, xbf, sems = scratch
        slot = t % 2
        n_fetch = xbuf.shape[1]
        per_step = n_fetch // nf

        def start_row(tile, sl, i):
            _row_copy(x_ref, tok_ref[src_ref[tile] + i], xbuf.at[sl], i, sems.at[sl, i]).start()

        def wait_all(sl):
            def body(c, carry):
                for r in range(DMA_LOOP_UNROLL):
                    i = c * DMA_LOOP_UNROLL + r
                    _row_copy(x_ref, 0, xbuf.at[sl], i, sems.at[sl, i]).wait()
                return carry
            lax.fori_loop(0, n_fetch // DMA_LOOP_UNROLL, body, 0)

        @pl.when(jnp.logical_and(f == 0, t == 0))
        def _():
            def body(i, carry):
                start_row(0, 0, i)
                return carry
            lax.fori_loop(0, n_fetch, body, 0, unroll=DMA_LOOP_UNROLL)

        fed = jnp.logical_or(t == 0, rows_ref[jnp.maximum(t - 1, 0)] > 0)

        @pl.when(jnp.logical_and(f == 0, fed))
        def _():
            wait_all(slot)
            xbf[...] = _unpack_bf16_pairs(xbuf[slot, 0:tm])

        def fetch_ahead():
            nxt = jnp.minimum(t + 1, nt - 1)
            for r in range(per_step):
                start_row(nxt, 1 - slot, f * per_step + r)
    else:
        def fetch_ahead():
            pass

    def ffn_rows(n):
        fetch_ahead()
        x = xbf[0:n, :]
        g = jnp.dot(x, wg_ref[...].astype(BF16), preferred_element_type=F32)
        u = jnp.dot(x, wu_ref[...].astype(BF16), preferred_element_type=F32)
        hmid = (g * jax.nn.sigmoid(g) * u).astype(BF16)
        wd = wd_ref[...].astype(BF16)
        cw = o_ref.shape[1] // n_col
        for c in range(n_col):
            y = jnp.dot(hmid, wd[:, c * cw:(c + 1) * cw], preferred_element_type=F32)
            o_ref[0:n, c * cw:(c + 1) * cw] += y

    for k in range(1, tm // sub_rows + 1):
        covers = jnp.logical_and(n_rows > (k - 1) * sub_rows, n_rows <= k * sub_rows)
        pl.when(covers)(functools.partial(ffn_rows, k * sub_rows))

    if gather:
        @pl.when(jnp.logical_and(n_rows > 0, jnp.logical_and(t == nt - 1, f == nf - 1)))
        def _():
            wait_all(1 - slot)


def gather_rows_per_tile(tm, nf):
    per_step = -(-tm // nf)
    while (per_step * nf) % DMA_LOOP_UNROLL:
        per_step += 1
    return per_step * nf


def swiglu(x, w_gate, w_up, w_down, tile_exp, tile_rows, tile_src=None, src_rows=None, *, gain=None,
           tm=MOE_TILE, tf=FFN_F_TILE, sub_rows=MOE_TILE, name="swiglu"):
    gather = src_rows is not None
    assert gather != (gain is not None)
    m = tile_rows.shape[0] * tm
    d = w_gate.shape[1]
    dff = w_gate.shape[2]
    nf = dff // tf
    nt = m // tm
    assert nf >= 2

    def w_in_map(t, f, exp_ref, rows_ref, src_ref, tok_ref):
        return (exp_ref[t], 0, jnp.where(rows_ref[t] > 0, f, nf - 1))

    def w_out_map(t, f, exp_ref, rows_ref, src_ref, tok_ref):
        return (exp_ref[t], jnp.where(rows_ref[t] > 0, f, nf - 1), 0)

    def tile_map(t, f, exp_ref, rows_ref, src_ref, tok_ref):
        return (t, 0)

    w_specs = [pl.BlockSpec((None, d, tf), w_in_map),
               pl.BlockSpec((None, d, tf), w_in_map),
               pl.BlockSpec((None, tf, d), w_out_map)]
    if gather:
        in_specs = [pl.BlockSpec(memory_space=pl.ANY)] + w_specs
        args = [x, w_gate, w_up, w_down]
        n_fetch = gather_rows_per_tile(tm, nf)
        scratch = [pltpu.VMEM((2, n_fetch, d // 2), jnp.uint32), pltpu.VMEM((tm, d), BF16),
                   pltpu.SemaphoreType.DMA((2, n_fetch))]
    else:
        in_specs = [pl.BlockSpec((tm, d), tile_map, pipeline_mode=pl.Buffered(1)),
                    pl.BlockSpec((1, d), lambda t, f, e, r, s, k: (0, 0))] + w_specs
        args = [x, gain.reshape(1, d), w_gate, w_up, w_down]
        scratch = [pltpu.VMEM((tm, d), BF16)]
        tile_src = jnp.zeros((nt,), jnp.int32)
        src_rows = jnp.zeros((1,), jnp.int32)
    grid_spec = pltpu.PrefetchScalarGridSpec(
        num_scalar_prefetch=4,
        grid=(nt, nf),
        in_specs=in_specs,
        out_specs=pl.BlockSpec((tm, d), tile_map, pipeline_mode=pl.Buffered(1)),
        scratch_shapes=scratch,
    )
    return pl.pallas_call(
        functools.partial(_swiglu_kernel, n_col=FFN_OUT_CHUNKS, gather=gather, sub_rows=sub_rows, nf=nf),
        out_shape=jax.ShapeDtypeStruct((m, d), F32),
        grid_spec=grid_spec,
        compiler_params=_params(("arbitrary", "arbitrary"), big=True),
        name=name,
    )(tile_exp, tile_rows, tile_src, src_rows, *args)


COMBINE_TILE = 128


def _combine_kernel(d0_ref, d1_ref, ys_hbm, gate_ref, h_ref, hout_ref, buf, sems):
    i = pl.program_id(0)
    n = pl.num_programs(0)
    tt = h_ref.shape[0]
    slot = i % 2
    dests = (d0_ref, d1_ref)

    def start_tile(tile, sl):
        def body(r, c):
            for k, d_ref in enumerate(dests):
                _row_copy(ys_hbm, d_ref[tile * tt + r], buf.at[sl, k], r, sems.at[sl, k, r]).start()
            return c
        lax.fori_loop(0, tt, body, 0, unroll=DMA_LOOP_UNROLL)

    @pl.when(i == 0)
    def _():
        start_tile(0, 0)

    @pl.when(i + 1 < n)
    def _():
        start_tile(i + 1, 1 - slot)

    def wait_body(r, c):
        for k in range(TOP_K):
            _row_copy(ys_hbm, 0, buf.at[slot, k], r, sems.at[slot, k, r]).wait()
        return c
    lax.fori_loop(0, tt, wait_body, 0, unroll=DMA_LOOP_UNROLL)

    gates = gate_ref[...]
    hout_ref[...] = h_ref[...] + (buf[slot, 0] * gates[:, 0:1] + buf[slot, 1] * gates[:, 1:2])


def moe_combine(h2, ys, dest2, gates, tt=COMBINE_TILE):
    n, d = h2.shape
    grid_spec = pltpu.PrefetchScalarGridSpec(
        num_scalar_prefetch=2,
        grid=(n // tt,),
        in_specs=[pl.BlockSpec(memory_space=pl.ANY),
                  pl.BlockSpec((tt, TOP_K), lambda i, a, b: (i, 0)),
                  pl.BlockSpec((tt, d), lambda i, a, b: (i, 0))],
        out_specs=pl.BlockSpec((tt, d), lambda i, a, b: (i, 0)),
        scratch_shapes=[pltpu.VMEM((2, TOP_K, tt, d), F32), pltpu.SemaphoreType.DMA((2, TOP_K, tt))],
    )
    return pl.pallas_call(
        _combine_kernel,
        out_shape=jax.ShapeDtypeStruct((n, d), F32),
        grid_spec=grid_spec,
        compiler_params=_params(("arbitrary",), big=True),
        name="moe_combine",
    )(dest2[:, 0], dest2[:, 1], ys, gates, h2)


def _t5_bucket(rel):
    half = T5_BUCKETS // 2
    max_exact = half // 2
    n = jnp.abs(rel)
    log_ratio = jnp.log(jnp.maximum(n, 1).astype(F32) / max_exact) / math.log(T5_MAX_DIST / max_exact)
    large = jnp.minimum(max_exact + (log_ratio * (half - max_exact)).astype(jnp.int32), half - 1)
    return jnp.where(rel > 0, half, 0) + jnp.where(n < max_exact, n, large)


def _window_bias_table(t5_bias):
    i = jnp.arange(A_BLOCK)[:, None]
    j = jnp.arange(3 * A_BLOCK)[None, :]
    rel = j - A_BLOCK - i
    onehot = (_t5_bucket(rel)[None] == jnp.arange(T5_BUCKETS)[:, None, None]).astype(F32)
    bias = jnp.einsum('bh,bij->hij', t5_bias.astype(F32), onehot, precision=lax.Precision.HIGHEST)
    return jnp.where((jnp.abs(rel) <= A_BLOCK)[None], bias, NEG_INF)


def _window_kernel(sink_ref, q_ref, kv_ref, bias_ref, o_ref, *, nb):
    n = pl.program_id(1)
    scale = HEAD_DIM ** -0.5
    starts = (jnp.maximum(n - 1, 0), n, jnp.minimum(n + 1, nb - 1))
    kv = [kv_ref[pl.ds(pl.multiple_of(s * A_BLOCK, A_BLOCK), A_BLOCK), :] for s in starts]
    col = lax.broadcasted_iota(jnp.int32, (A_BLOCK, 3 * A_BLOCK), 1)
    edge_ok = jnp.logical_and(jnp.logical_or(n > 0, col >= A_BLOCK),
                              jnp.logical_or(n < nb - 1, col < 2 * A_BLOCK))
    kv = jnp.concatenate(kv, axis=0)
    q = q_ref[...]
    lo = lax.broadcasted_iota(jnp.int32, (1, 2 * HEAD_DIM), 1) < HEAD_DIM
    sel = (jnp.where(lo, scale, 0.0).astype(BF16), jnp.where(lo, 0.0, scale).astype(BF16))
    pairs_per_kv = A_GROUP // 2
    kk, vv = [], []
    for kh in range(A_KV_HEADS):
        k1 = kv[:, kh * HEAD_DIM:(kh + 1) * HEAD_DIM]
        v1 = kv[:, A_KV_WIDTH + kh * HEAD_DIM:A_KV_WIDTH + (kh + 1) * HEAD_DIM]
        kk.append(jnp.concatenate([k1, k1], axis=1))
        vv.append(jnp.concatenate([v1, v1], axis=1))
    scores = []
    for pr in range(A_Q_HEADS // 2):
        q2 = q[:, pr * 2 * HEAD_DIM:(pr + 1) * 2 * HEAD_DIM]
        qq = jnp.concatenate([q2 * sel[0], q2 * sel[1]], axis=0)
        s2 = lax.dot_general(qq, kk[pr // pairs_per_kv], (((1,), (1,)), ((), ())),
                             preferred_element_type=F32)
        for i in range(2):
            scores.append(jnp.where(edge_ok, s2[i * A_BLOCK:(i + 1) * A_BLOCK] + bias_ref[2 * pr + i],
                                    NEG_INF))
    probs, dens = [], []
    for h, s in enumerate(scores):
        sink = sink_ref[h]
        mx = jnp.maximum(jnp.max(s, axis=-1, keepdims=True), sink)
        p = jnp.exp(s - mx)
        dens.append(jnp.sum(p, axis=-1, keepdims=True) + jnp.exp(sink - mx))
        probs.append(p.astype(BF16))
    for pr in range(A_Q_HEADS // 2):
        v2 = vv[pr // pairs_per_kv]
        oa = jnp.dot(probs[2 * pr], v2, preferred_element_type=F32) / dens[2 * pr]
        ob = jnp.dot(probs[2 * pr + 1], v2, preferred_element_type=F32) / dens[2 * pr + 1]
        o_ref[:, pr * 2 * HEAD_DIM:(pr + 1) * 2 * HEAD_DIM] = jnp.where(lo, oa, ob).astype(o_ref.dtype)


def window_attention(proj_a, t5_bias, sink):
    b, s, _ = proj_a.shape
    nb = s // A_BLOCK
    kvw = 2 * A_KV_WIDTH
    grid_spec = pltpu.PrefetchScalarGridSpec(
        num_scalar_prefetch=0,
        grid=(b, nb),
        in_specs=[pl.BlockSpec(memory_space=pltpu.SMEM),
                  pl.BlockSpec((None, A_BLOCK, A_WIDTH), lambda bi, n: (bi, n, 0)),
                  pl.BlockSpec((None, s, kvw), lambda bi, n: (bi, 0, A_WIDTH // kvw)),
                  pl.BlockSpec((A_Q_HEADS, A_BLOCK, 3 * A_BLOCK), lambda bi, n: (0, 0, 0))],
        out_specs=pl.BlockSpec((None, A_BLOCK, A_WIDTH), lambda bi, n: (bi, n, 0)),
    )
    return pl.pallas_call(
        functools.partial(_window_kernel, nb=nb),
        out_shape=jax.ShapeDtypeStruct((b, s, A_WIDTH), BF16),
        grid_spec=grid_spec,
        compiler_params=_params(("parallel", "arbitrary")),
        name="window_attention",
    )(sink.astype(F32), proj_a, proj_a, _window_bias_table(t5_bias))


NA_HEAD_GROUP = 16


def _na_bias_table(rpb, rows):
    kh = min(NA_WIN_H, rows)
    kw = NA_WIN_W
    c = jnp.arange(GRID_W)
    cs = jnp.clip(c - kw // 2, 0, GRID_W - kw)
    col_ok = (c[None] >= cs[:, None]) & (c[None] < cs[:, None] + kw)
    col_off = jnp.clip(c[None] - c[:, None], -(kw - 1), kw - 1) + kw - 1
    hi = lax.Precision.HIGHEST
    row_off = jnp.arange(kh)[None, :] - jnp.arange(kh)[:, None] + NA_WIN_H - 1
    row_sel = (row_off[:, :, None] == jnp.arange(2 * NA_WIN_H - 1)[None, None, :]).astype(F32)
    col_sel = (col_off[None] == jnp.arange(2 * kw - 1)[:, None, None]).astype(F32)
    by_row = jnp.einsum('hrc,dir->dhic', rpb.astype(F32), row_sel, precision=hi)
    bias = jnp.einsum('dhic,cqk->dhqik', by_row, col_sel, precision=hi)
    bias = jnp.where(col_ok[None, None, :, None, :], bias, NEG_INF)
    return bias.reshape(kh, rpb.shape[0], GRID_W, kh * GRID_W)


def _na_kernel(q_ref, k_ref, v_ref, bias_ref, o_ref, *, rows):
    r = pl.program_id(2)
    kh = min(NA_WIN_H, rows)
    scale = HEAD_DIM ** -0.5
    start = pl.multiple_of(jnp.clip(r - kh // 2, 0, rows - kh) * GRID_W, GRID_W)
    k = k_ref[pl.ds(start, kh * GRID_W), :]
    v = v_ref[pl.ds(start, kh * GRID_W), :]
    q = q_ref[...]
    lo = lax.broadcasted_iota(jnp.int32, (1, 2 * HEAD_DIM), 1) < HEAD_DIM
    sel = (jnp.where(lo, scale, 0.0).astype(BF16), jnp.where(lo, 0.0, scale).astype(BF16))
    n_pairs = NA_HEAD_GROUP // 2
    scores = []
    for pr in range(n_pairs):
        sl = slice(pr * 2 * HEAD_DIM, (pr + 1) * 2 * HEAD_DIM)
        q2 = q[:, sl]
        qq = jnp.concatenate([q2 * sel[0], q2 * sel[1]], axis=0)
        s2 = lax.dot_general(qq, k[:, sl], (((1,), (1,)), ((), ())), preferred_element_type=F32)
        for i in range(2):
            scores.append(s2[i * GRID_W:(i + 1) * GRID_W] + bias_ref[2 * pr + i])
    probs, dens = [], []
    for s in scores:
        mx = jnp.max(s, axis=-1, keepdims=True)
        p = jnp.exp(s - mx)
        dens.append(jnp.sum(p, axis=-1, keepdims=True))
        probs.append(p.astype(BF16))
    for pr in range(n_pairs):
        sl = slice(pr * 2 * HEAD_DIM, (pr + 1) * 2 * HEAD_DIM)
        oa = jnp.dot(probs[2 * pr], v[:, sl], preferred_element_type=F32) / dens[2 * pr]
        ob = jnp.dot(probs[2 * pr + 1], v[:, sl], preferred_element_type=F32) / dens[2 * pr + 1]
        o_ref[:, sl] = jnp.where(lo, oa, ob).astype(o_ref.dtype)


def neighbourhood_attention(proj_n, rpb):
    b, s, _ = proj_n.shape
    rows = s // GRID_W
    kh = min(NA_WIN_H, rows)
    gw = NA_HEAD_GROUP * HEAD_DIM
    ng = B_WIDTH // gw

    def bias_map(bi, g, r):
        return (r - jnp.clip(r - kh // 2, 0, rows - kh), g, 0, 0)

    return pl.pallas_call(
        functools.partial(_na_kernel, rows=rows),
        out_shape=jax.ShapeDtypeStruct((b, s, B_WIDTH), BF16),
        grid=(b, ng, rows),
        in_specs=[pl.BlockSpec((None, GRID_W, gw), lambda bi, g, r: (bi, r, g)),
                  pl.BlockSpec((None, s, gw), lambda bi, g, r: (bi, 0, ng + g)),
                  pl.BlockSpec((None, s, gw), lambda bi, g, r: (bi, 0, 2 * ng + g)),
                  pl.BlockSpec((None, NA_HEAD_GROUP, GRID_W, kh * GRID_W), bias_map)],
        out_specs=pl.BlockSpec((None, GRID_W, gw), lambda bi, g, r: (bi, r, g)),
        compiler_params=_params(("parallel", "parallel", "arbitrary"), big=True),
        name="neighbourhood_attention",
    )(proj_n, proj_n, proj_n, _na_bias_table(rpb, rows))


CONV_HALO = 16
CONV_COL_CHUNK = 512


def _hyena_in_kernel(prev_ref, cur_ref, next_ref, w_ref, b_ref, cw_ref, cb_ref, o_ref, wbf_ref, *,
                     tiles_per_seq):
    i = pl.program_id(1)

    @pl.when(i == 0)
    def _():
        wbf_ref[...] = w_ref[...].astype(BF16)

    tm = cur_ref.shape[0]
    rows = tm + 2 * CONV_HALO
    a = jnp.concatenate([prev_ref[...], cur_ref[...], next_ref[...]], axis=0)
    pos = i % tiles_per_seq
    row = lax.broadcasted_iota(jnp.int32, (rows, 1), 0)
    kill_up = jnp.logical_and(pos == 0, row == CONV_HALO)
    kill_dn = jnp.logical_and(pos == tiles_per_seq - 1, row == CONV_HALO + tm - 1)
    for c in range(o_ref.shape[1] // CONV_COL_CHUNK):
        cs = slice(c * CONV_COL_CHUNK, (c + 1) * CONV_COL_CHUNK)
        p = jnp.dot(a, wbf_ref[:, cs], preferred_element_type=F32) + b_ref[:, cs]
        up = jnp.where(kill_up, 0.0, pltpu.roll(p, 1, 0))
        dn = jnp.where(kill_dn, 0.0, pltpu.roll(p, rows - 1, 0))
        y = up * cw_ref[0:1, cs] + p * cw_ref[1:2, cs] + dn * cw_ref[2:3, cs] + cb_ref[:, cs]
        o_ref[:, cs] = y[CONV_HALO:CONV_HALO + tm]


def hyena_in_conv(hn, w_in, b_in, conv_w, conv_b, seq, tm=1024, tn=1024):
    m, k = hn.shape
    n = w_in.shape[1]
    hb = tm // CONV_HALO
    last_halo = m // CONV_HALO - 1
    return pl.pallas_call(
        functools.partial(_hyena_in_kernel, tiles_per_seq=seq // tm),
        out_shape=jax.ShapeDtypeStruct((m, n), F32),
        grid=(n // tn, m // tm),
        in_specs=[pl.BlockSpec((CONV_HALO, k), lambda j, i: (jnp.maximum(i * hb - 1, 0), 0)),
                  pl.BlockSpec((tm, k), lambda j, i: (i, 0)),
                  pl.BlockSpec((CONV_HALO, k), lambda j, i: (jnp.minimum((i + 1) * hb, last_halo), 0)),
                  pl.BlockSpec((k, tn), lambda j, i: (0, j)),
                  pl.BlockSpec((1, tn), lambda j, i: (0, j)),
                  pl.BlockSpec((3, tn), lambda j, i: (0, j)),
                  pl.BlockSpec((1, tn), lambda j, i: (0, j))],
        out_specs=pl.BlockSpec((tm, tn), lambda j, i: (i, j)),
        scratch_shapes=[pltpu.VMEM((k, tn), BF16)],
        compiler_params=_params(("parallel", "arbitrary"), big=True),
        name="hyena_in_conv",
    )(hn, hn, hn, w_in, b_in.reshape(1, n), conv_w, conv_b.reshape(1, n))


def _filter_stage1_kernel(z_ref, t_ref, fw1_ref, fb1_ref, fw2_ref, fb2_ref, fr_ref, w3f0_ref, w3f1_ref,
                          w3b0_ref, w3b1_ref, delta_ref, f1_ref, y_ref, hid_ref, *, seq):
    j = pl.program_id(0)

    @pl.when(pl.program_id(1) == 0)
    def _():
        hi = lax.Precision.HIGHEST
        fr = fr_ref[...]
        hid = jnp.sin(fr * (jnp.dot(z_ref[...], fw1_ref[...], preferred_element_type=F32, precision=hi)
                            + fb1_ref[...]))
        hid = jnp.sin(fr * (jnp.dot(hid, fw2_ref[...], preferred_element_type=F32, precision=hi)
                            + fb2_ref[...]))
        hid_ref[...] = hid.astype(BF16)

    hid_bf = hid_ref[...]
    r = lax.broadcasted_iota(jnp.int32, (hid_bf.shape[0], 1), 0)
    u = j * FFT_CHUNK + r // FFT_N2 + FFT_N1 * (r % FFT_N2)
    decay = jnp.where(u == seq, 0.0, jnp.exp(-t_ref[...] * delta_ref[...]))
    for o, (wf_ref, wb_ref) in enumerate(((w3f0_ref, w3b0_ref), (w3f1_ref, w3b1_ref))):
        fwd = jnp.dot(hid_bf, wf_ref[...].astype(BF16), preferred_element_type=F32)
        bwd = jnp.dot(hid_bf, wb_ref[...].astype(BF16), preferred_element_type=F32)
        taps = (jnp.where(u > seq, bwd, jnp.where(u == 0, fwd + bwd, fwd)) * decay).astype(BF16)
        for jj in range(FFT_CHUNK):
            x = taps[jj * FFT_N2:(jj + 1) * FFT_N2]
            y_ref[o, jj] = _pack_bf16_pairs(jnp.dot(f1_ref[jj], x, preferred_element_type=F32))


def hyena_filter_stage1(seq, fw1, fb1, fw2, fb2, freq, fw3, width, consts, cb=FFT_CB):
    _, f1_real, _, _, _ = consts
    n = 2 * seq
    n1, n2, ch = FFT_N1, FFT_N2, FFT_CHUNK
    assert fw3.shape[1] == 4 * width
    r = jnp.arange(n)
    u = (r // (ch * n2)) * ch + (r // n2) % ch + n1 * (r % n2)
    pos = jnp.where(u <= seq, u, n - u)
    pos = jnp.where(u == seq, 0, pos)
    bands = (HYENA_EMB - 1) // 2
    f = jnp.linspace(1e-4, bands - 1, bands, dtype=F32)[None]
    t = (pos.astype(F32) / (seq - 1))[:, None]
    w = ((2.0 * math.pi / seq) * pos.astype(F32))[:, None]
    z = jnp.concatenate([t, jnp.cos(f * w), -jnp.sin(f * w)], axis=-1)
    emb_pad = 40
    z = jnp.pad(z, ((0, 0), (0, emb_pad - HYENA_EMB)))
    fw1p = jnp.pad(fw1.astype(F32), ((0, emb_pad - HYENA_EMB), (0, 0)))
    fh = fw1.shape[1]
    max_decay = math.log(HYENA_DECAY_TARGET) / HYENA_FAST_PCT
    min_decay = math.log(HYENA_DECAY_TARGET) / HYENA_SLOW_PCT
    deltas = jnp.abs(jnp.linspace(min_decay, max_decay, width, dtype=F32))[None]
    ncb = width // cb
    rows = ch * n2

    def small(shape):
        return pl.BlockSpec(shape, lambda j, k: (0, 0))

    def w3_spec(blk):
        return pl.BlockSpec((fh, cb), lambda j, k: (0, blk * ncb + k))

    return pl.pallas_call(
        functools.partial(_filter_stage1_kernel, seq=seq),
        out_shape=jax.ShapeDtypeStruct((2, n1, 2 * n2, width // 2), jnp.uint32),
        grid=(n1 // ch, ncb),
        in_specs=[pl.BlockSpec((rows, emb_pad), lambda j, k: (j, 0)),
                  pl.BlockSpec((rows, 1), lambda j, k: (j, 0)),
                  small((emb_pad, fh)), small((1, fh)), small((fh, fh)), small((1, fh)), small((1, fh)),
                  w3_spec(0), w3_spec(1), w3_spec(2), w3_spec(3),
                  pl.BlockSpec((1, cb), lambda j, k: (0, k)),
                  pl.BlockSpec((ch, 2 * n2, n2), lambda j, k: (j, 0, 0))],
        out_specs=pl.BlockSpec((2, ch, 2 * n2, cb // 2), lambda j, k: (0, j, 0, k)),
        scratch_shapes=[pltpu.VMEM((rows, fh), BF16)],
        compiler_params=_params(("parallel", "arbitrary"), big=True),
        name="hyena_filter_stage1",
    )(z, t, fw1p, fb1.reshape(1, fh).astype(F32), fw2.astype(F32), fb2.reshape(1, fh).astype(F32),
      freq.reshape(1, fh).astype(F32), fw3, fw3, fw3, fw3, deltas, f1_real)


def _dft_constants():
    n1, n2 = FFT_N1, FFT_N2
    n = n1 * n2
    a1 = np.arange(n1)
    a2 = np.arange(n2)
    half = n2 // 2

    def cplx_block(w):
        return np.block([[w.real, -w.imag], [w.imag, w.real]])

    ang = (a2[None, :, None] * a2[None, None, :] / n2) + (a1[:, None, None] * a2[None, :, None] / n)
    w1 = np.exp(-2j * np.pi * ang)
    f1_pad = np.stack([cplx_block(w1[i][:, :half]) for i in range(n1)])
    f1_real = np.stack([np.concatenate([w1[i].real, w1[i].imag], axis=0) for i in range(n1)])
    w2 = np.exp(-2j * np.pi * (a1[:, None] * a1[None, :]) / n1)
    f2 = cplx_block(w2)
    ang = (a1[None, :, None] * a1[None, None, :] / n1) + (a2[:, None, None] * a1[None, :, None] / n)
    g2 = np.stack([cplx_block(m_) for m_ in np.exp(2j * np.pi * ang)])
    wg1 = np.exp(2j * np.pi * (a2[:half, None] * a2[None, :]) / n2) / n
    g1 = cplx_block(wg1)
    to = lambda x: jnp.asarray(x.astype(np.float32)).astype(BF16)
    return to(f1_pad), to(f1_real), to(f2), to(g2), to(g1)


def _fft_stage1_block(x_re, x_im, f_ref, y_ref):
    xr = pltpu.einshape("mjc->jmc", x_re)
    xi = pltpu.einshape("mjc->jmc", x_im)
    for j in range(FFT_CHUNK):
        xc = jnp.concatenate([xr[j], xi[j]], axis=0).astype(BF16)
        y_ref[j] = _pack_bf16_pairs(jnp.dot(f_ref[j], xc, preferred_element_type=F32))


def _fft_stage1_kernel(x_ref, f_ref, y_ref):
    _fft_stage1_block(x_ref[0], x_ref[1], f_ref, y_ref)


def _fft_stage2_kernel(yr_ref, yi_ref, f2_ref, h_ref):
    yr = pltpu.einshape("nkc->knc", yr_ref[...])
    yi = pltpu.einshape("nkc->knc", yi_ref[...])
    for j in range(FFT_CHUNK):
        yc = _unpack_bf16_pairs(jnp.concatenate([yr[j], yi[j]], axis=0))
        h_ref[j] = _pack_bf16_pairs(jnp.dot(f2_ref[...], yc, preferred_element_type=F32))


def _fft_mid_kernel(yr_ref, yi_ref, f2_ref, h_ref, g2_ref, q_ref):
    yr = pltpu.einshape("nkc->knc", yr_ref[...])
    yi = pltpu.einshape("nkc->knc", yi_ref[...])
    n1 = FFT_N1
    spec = [jnp.dot(f2_ref[...], _unpack_bf16_pairs(jnp.concatenate([yr[j], yi[j]], axis=0)),
                    preferred_element_type=F32) for j in range(FFT_CHUNK)]
    prod = []
    for j, z in enumerate(spec):
        hf = _unpack_pairs_f32(h_ref[j])
        zr, zi, hr, hi = z[:n1], z[n1:], hf[:n1], hf[n1:]
        prod.append(jnp.concatenate([zr * hr - zi * hi, zr * hi + zi * hr], axis=0).astype(BF16))
    for j, pc in enumerate(prod):
        q_ref[j] = _pack_bf16_pairs(jnp.dot(g2_ref[j], pc, preferred_element_type=F32))


def _fft_last_kernel(qr_ref, qi_ref, g1_ref, gate_ref, zf_ref, fb_ref, *rest, feeds_next):
    f1_ref, o_ref, y_ref = rest if feeds_next else (None, rest[0], None)
    qr = pltpu.einshape("kjc->jkc", qr_ref[...])
    qi = pltpu.einshape("kjc->jkc", qi_ref[...])
    half = FFT_N2 // 2
    conv = []
    for j in range(FFT_CHUNK):
        qc = _unpack_bf16_pairs(jnp.concatenate([qr[j], qi[j]], axis=0))
        conv.append(jnp.dot(g1_ref[...], qc, preferred_element_type=F32))
    conv = jnp.stack(conv, axis=0)
    fb = fb_ref[...]
    out = []
    for bi in range(2):
        cb = pltpu.einshape("jmc->mjc", conv[:, bi * half:(bi + 1) * half])
        out.append(gate_ref[bi] * (cb + zf_ref[bi] * fb))
        o_ref[bi] = out[bi]
    if feeds_next:
        _fft_stage1_block(out[0], out[1], f1_ref, y_ref)


def hyena_filter_spectrum(y, consts, cb=FFT_CB):
    _, _, f2, _, _ = consts
    no = y.shape[0]
    c = 2 * y.shape[-1]
    n1, n2, ch = FFT_N1, FFT_N2, FFT_CHUNK
    nk = n2 // ch
    return pl.pallas_call(
        _fft_stage2_kernel,
        out_shape=jax.ShapeDtypeStruct((no, n2, 2 * n1, c // 2), jnp.uint32),
        grid=(no, nk, c // cb),
        in_specs=[pl.BlockSpec((None, n1, ch, cb // 2), lambda o, j, k: (o, 0, j, k)),
                  pl.BlockSpec((None, n1, ch, cb // 2), lambda o, j, k: (o, 0, nk + j, k)),
                  pl.BlockSpec((2 * n1, 2 * n1), lambda o, j, k: (0, 0))],
        out_specs=pl.BlockSpec((None, ch, 2 * n1, cb // 2), lambda o, j, k: (o, j, 0, k)),
        compiler_params=_params(("parallel", "parallel", "parallel"), big=True),
        name="fft_filter_stage2",
    )(y, y, f2)


def hyena_long_conv_gate(zsrc, z_col, gate_src, gate_col, hf, order, fbias, consts, y=None,
                         feeds_next=False, cb=FFT_CB):
    f1_pad, _, f2, g2, g1 = consts
    n1, n2, ch = FFT_N1, FFT_N2, FFT_CHUNK
    c = 2 * hf.shape[-1]
    ncb = c // cb
    half = n2 // 2
    y_shape = jax.ShapeDtypeStruct((n1, 2 * n2, c // 2), jnp.uint32)
    f1_spec = pl.BlockSpec((ch, 2 * n2, n2), lambda j, k: (j, 0, 0))
    y_spec = pl.BlockSpec((ch, 2 * n2, cb // 2), lambda j, k: (j, 0, k))
    if y is None:
        y = pl.pallas_call(
            _fft_stage1_kernel,
            out_shape=y_shape,
            grid=(n1 // ch, ncb),
            in_specs=[pl.BlockSpec((2, half, ch, cb), lambda j, k: (0, 0, j, z_col * ncb + k)), f1_spec],
            out_specs=y_spec,
            compiler_params=_params(("parallel", "parallel"), big=True),
            name="fft_stage1",
        )(zsrc, f1_pad)
    nk = n2 // ch
    q = pl.pallas_call(
        _fft_mid_kernel,
        out_shape=jax.ShapeDtypeStruct((n2, 2 * n1, c // 2), jnp.uint32),
        grid=(nk, ncb),
        in_specs=[pl.BlockSpec((n1, ch, cb // 2), lambda j, k: (0, j, k)),
                  pl.BlockSpec((n1, ch, cb // 2), lambda j, k: (0, nk + j, k)),
                  pl.BlockSpec((2 * n1, 2 * n1), lambda j, k: (0, 0)),
                  pl.BlockSpec((None, ch, 2 * n1, cb // 2), lambda j, k: (order, j, 0, k)),
                  pl.BlockSpec((ch, 2 * n1, 2 * n1), lambda j, k: (j, 0, 0))],
        out_specs=pl.BlockSpec((ch, 2 * n1, cb // 2), lambda j, k: (j, 0, k)),
        compiler_params=_params(("parallel", "parallel"), big=True),
        name="fft_mid",
    )(y, y, f2, hf, g2)
    nj = n1 // ch
    in_specs = [pl.BlockSpec((n2, ch, cb // 2), lambda j, k: (0, j, k)),
                pl.BlockSpec((n2, ch, cb // 2), lambda j, k: (0, nj + j, k)),
                pl.BlockSpec((2 * half, 2 * n2), lambda j, k: (0, 0)),
                pl.BlockSpec((2, half, ch, cb), lambda j, k: (0, 0, j, gate_col * ncb + k)),
                pl.BlockSpec((2, half, ch, cb), lambda j, k: (0, 0, j, z_col * ncb + k)),
                pl.BlockSpec((None, 1, cb), lambda j, k: (order, 0, k))]
    args = [q, q, g1, gate_src, zsrc, fbias.reshape(fbias.shape[0], 1, c)]
    out_shape = jax.ShapeDtypeStruct((2, half, n1, c), F32)
    out_spec = pl.BlockSpec((2, half, ch, cb), lambda j, k: (0, 0, j, k))
    if feeds_next:
        in_specs.append(f1_spec)
        args.append(f1_pad)
        out_shape, out_spec = (out_shape, y_shape), (out_spec, y_spec)
    res = pl.pallas_call(
        functools.partial(_fft_last_kernel, feeds_next=feeds_next),
        out_shape=out_shape,
        grid=(nj, ncb),
        in_specs=in_specs,
        out_specs=out_spec,
        compiler_params=_params(("parallel", "parallel"), big=True),
        name="fft_last",
    )(*args)
    return res if feeds_next else (res, None)


def hyena_mixer(hn, h_res, w_in, b_in, conv_w, conv_b, fw1, fb1, fw2, fb2, freq, fw3, fbias, w_out,
                batch, seq):
    width = w_out.shape[0]
    assert batch == 2 and 2 * seq == FFT_N1 * FFT_N2
    sc = hyena_in_conv(hn, w_in, b_in, conv_w, conv_b, seq)
    consts = _dft_constants()
    hf = hyena_filter_spectrum(hyena_filter_stage1(seq, fw1, fb1, fw2, fb2, freq, fw3, width, consts),
                               consts)
    sc4 = sc.reshape(batch, FFT_N2 // 2, FFT_N1, 3 * width)
    zf1, y1 = hyena_long_conv_gate(sc4, 2, sc4, 0, hf, 0, fbias, consts, feeds_next=True)
    zf2, _ = hyena_long_conv_gate(zf1, 0, sc4, 1, hf, 1, fbias, consts, y=y1)
    return matmul([zf2.reshape(batch * seq, width)], w_out, res=h_res, tm=512, tn=1024,
                  name="hyena_out")


def moe_swiglu(hn2_packed, logits, wg, wu, wd):
    n = hn2_packed.shape[0]
    top_v, top_i = lax.top_k(logits, TOP_K)
    gates = jax.nn.softmax(top_v, axis=-1)
    e_flat = top_i.reshape(-1).astype(jnp.int32)
    nk = n * TOP_K
    onehot = (e_flat[:, None] == jnp.arange(N_EXPERTS, dtype=jnp.int32)[None]).astype(jnp.int32)
    csum = jnp.cumsum(onehot, axis=0)
    rank = jnp.take_along_axis(csum, e_flat[:, None], axis=1)[:, 0] - 1
    counts = csum[-1]
    padded = ((counts + MOE_TILE - 1) // MOE_TILE) * MOE_TILE
    pad_end = jnp.cumsum(padded)
    pad_start = pad_end - padded
    dest = pad_start[e_flat] + rank
    p_rows = nk + N_EXPERTS * MOE_TILE
    order = jnp.argsort(e_flat, stable=True).astype(jnp.int32)
    nf = wg.shape[2] // MOE_F_TILE
    sorted_tok = jnp.pad(order // TOP_K, (0, gather_rows_per_tile(MOE_TILE, nf)))
    start = jnp.cumsum(counts) - counts
    nt = p_rows // MOE_TILE
    tile_start = jnp.arange(nt, dtype=jnp.int32) * MOE_TILE
    tile_used = tile_start < pad_end[-1]
    tile_exp = jnp.minimum(jnp.searchsorted(pad_end, tile_start, side='right'), N_EXPERTS - 1).astype(jnp.int32)
    tile_rows = jnp.clip(counts[tile_exp] - (tile_start - pad_start[tile_exp]), 0, MOE_TILE)
    tile_rows = jnp.where(tile_used, tile_rows, 0).astype(jnp.int32)
    last_exp = tile_exp[jnp.maximum(jnp.sum(tile_used.astype(jnp.int32)) - 1, 0)]
    tile_src = jnp.where(tile_used, start[tile_exp] + (tile_start - pad_start[tile_exp]), 0)
    tile_exp = jnp.where(tile_used, tile_exp, last_exp)
    ys = swiglu(hn2_packed, wg, wu, wd, tile_exp, tile_rows, tile_src.astype(jnp.int32), sorted_tok,
                tf=MOE_F_TILE, sub_rows=FFN_SUB_ROWS, name="swiglu_experts")
    return ys, dest.reshape(n, TOP_K), gates


def kernel(x, p, ln_mix, ln_ffn, ln_ple, final_norm, t5_bias, w_attn_in, w_attn_out, attn_sink, na_rpb, w_ffn_gate, w_ffn_up, w_ffn_down, w_hy_in, b_hy_in, w_hy_conv, b_hy_conv, w_hy_f1, b_hy_f1, w_hy_f2, b_hy_f2, hy_freq, w_hy_f3, hy_bias, w_hy_out, w_router, w_exp_gate, w_exp_up, w_exp_down, w_ple_proj, w_ple_gate):
    batch, seq, d = x.shape
    n = batch * seq
    depth = ln_mix.shape[0]
    h = x.reshape(n, d)
    for i in range(depth):
        li = i // 2
        hn = rmsnorm(h, ln_mix[i], BF16)
        if i % 2 == 0:
            na_off = A_WIDTH + 2 * A_KV_WIDTH
            w_in = w_attn_in[li]
            proj_a = matmul([hn], w_in[:, :na_off], out_dtype=BF16, tn=na_off, name="attn_in_a")
            proj_n = matmul([hn], w_in[:, na_off:], out_dtype=BF16, tn=3 * B_WIDTH // 2, name="attn_in_n")
            oa = window_attention(proj_a.reshape(batch, seq, -1), t5_bias, attn_sink[li])
            ob = neighbourhood_attention(proj_n.reshape(batch, seq, -1), na_rpb[li])
            h = matmul([oa.reshape(n, A_WIDTH), ob.reshape(n, B_WIDTH)], w_attn_out[li], res=h,
                       name="attn_out")
            nt = n // MOE_TILE
            h = swiglu(h, w_ffn_gate, w_ffn_up, w_ffn_down,
                       jnp.full((nt,), li, jnp.int32), jnp.full((nt,), MOE_TILE, jnp.int32),
                       gain=ln_ffn[i], name="swiglu_dense")
        else:
            h = hyena_mixer(hn, h, w_hy_in[li], b_hy_in[li], w_hy_conv[li], b_hy_conv[li],
                            w_hy_f1[li], b_hy_f1[li], w_hy_f2[li], b_hy_f2[li], hy_freq[li],
                            w_hy_f3[li], hy_bias[li], w_hy_out[li], batch, seq)
            wr_pad = jnp.pad(w_router[li].astype(F32), ((0, 0), (0, V7X_LANES - N_EXPERTS)))
            hn2_packed, logits = rmsnorm_router(h, ln_ffn[i], wr_pad)
            ys, dest2, gates = moe_swiglu(hn2_packed, logits[:, :N_EXPERTS], w_exp_gate[li],
                                          w_exp_up[li], w_exp_down[li])
            h = moe_combine(h, ys, dest2, gates)
        h = ple(h, p.reshape(depth, n, -1), ln_ple[i], w_ple_gate, w_ple_proj, i)
    return rmsnorm(h, final_norm, F32).reshape(batch, seq, d)
```

```python
import functools
import math

import jax
import jax.numpy as jnp
import numpy as np
from jax import lax
from jax.experimental import pallas as pl
from jax.experimental.pallas import tpu as pltpu

F32 = jnp.float32
BF16 = jnp.bfloat16
NEG_INF = -1e30
RMS_EPS = 1e-6

V7X_VMEM_LIMIT_BYTES = 56 * 1024 * 1024
V7X_LANES = 128

HEAD_DIM = 64
A_Q_HEADS = 16
A_KV_HEADS = 2
A_GROUP = A_Q_HEADS // A_KV_HEADS
A_BLOCK = 128
T5_BUCKETS = 32
T5_MAX_DIST = 128
B_HEADS = 16
GRID_W = 64
NA_WIN_H = 8
NA_WIN_W = 16
A_WIDTH = A_Q_HEADS * HEAD_DIM
A_KV_WIDTH = A_KV_HEADS * HEAD_DIM
B_WIDTH = B_HEADS * HEAD_DIM
N_EXPERTS = 8
TOP_K = 2
HYENA_EMB = 33
HYENA_DECAY_TARGET = 1e-2
HYENA_FAST_PCT = 0.3
HYENA_SLOW_PCT = 1.5

FFT_N1 = 64
FFT_N2 = 128
FFT_CHUNK = 8
FFT_CB = 1024

MOE_TILE = 1024
FFN_F_TILE = 512
MOE_F_TILE = 512
FFN_SUB_ROWS = 256
FFN_OUT_CHUNKS = 4
DMA_LOOP_UNROLL = 8


def _params(semantics, big=False):
    return pltpu.CompilerParams(
        dimension_semantics=semantics,
        vmem_limit_bytes=V7X_VMEM_LIMIT_BYTES if big else None)


def _rmsnorm_kernel(x_ref, g_ref, o_ref):
    x = x_ref[...]
    y = x * lax.rsqrt(jnp.mean(x * x, axis=-1, keepdims=True) + RMS_EPS)
    o_ref[...] = (y * g_ref[...]).astype(o_ref.dtype)


def rmsnorm(x2, g, out_dtype, tm=1024):
    n, d = x2.shape
    return pl.pallas_call(
        _rmsnorm_kernel,
        out_shape=jax.ShapeDtypeStruct((n, d), out_dtype),
        grid=(n // tm,),
        in_specs=[pl.BlockSpec((tm, d), lambda i: (i, 0)),
                  pl.BlockSpec((1, d), lambda i: (0, 0))],
        out_specs=pl.BlockSpec((tm, d), lambda i: (i, 0)),
        compiler_params=_params(("parallel",), big=True),
        name="rmsnorm",
    )(x2, g.reshape(1, d))


def _pack_bf16_pairs(y):
    w = y.shape[1] // 2
    bits = pltpu.bitcast(y.astype(BF16).astype(F32), jnp.uint32)
    return (bits[:, :w] >> 16) | (bits[:, w:] & jnp.uint32(0xFFFF0000))


def _unpack_pairs_f32(p):
    lo = pltpu.bitcast(p << 16, F32)
    hi = pltpu.bitcast(p & jnp.uint32(0xFFFF0000), F32)
    return jnp.concatenate([lo, hi], axis=1)


def _unpack_bf16_pairs(p):
    return _unpack_pairs_f32(p).astype(BF16)


def _rmsnorm_router_kernel(x_ref, g_ref, wr_ref, o_ref, l_ref):
    x = x_ref[...]
    y = x * lax.rsqrt(jnp.mean(x * x, axis=-1, keepdims=True) + RMS_EPS)
    y = y * g_ref[...]
    o_ref[...] = _pack_bf16_pairs(y)
    w = wr_ref[...]
    y_hi, w_hi = y.astype(BF16), w.astype(BF16)
    y_lo = (y - y_hi.astype(F32)).astype(BF16)
    w_lo = (w - w_hi.astype(F32)).astype(BF16)
    l_ref[...] = (jnp.dot(y_hi, w_hi, preferred_element_type=F32)
                  + jnp.dot(y_lo, w_hi, preferred_element_type=F32)
                  + jnp.dot(y_hi, w_lo, preferred_element_type=F32))


def rmsnorm_router(x2, g, w_router_pad, tm=512):
    n, d = x2.shape
    ne = w_router_pad.shape[1]
    return pl.pallas_call(
        _rmsnorm_router_kernel,
        out_shape=(jax.ShapeDtypeStruct((n, d // 2), jnp.uint32), jax.ShapeDtypeStruct((n, ne), F32)),
        grid=(n // tm,),
        in_specs=[pl.BlockSpec((tm, d), lambda i: (i, 0)),
                  pl.BlockSpec((1, d), lambda i: (0, 0)),
                  pl.BlockSpec((d, ne), lambda i: (0, 0))],
        out_specs=(pl.BlockSpec((tm, d // 2), lambda i: (i, 0)),
                   pl.BlockSpec((tm, ne), lambda i: (i, 0))),
        compiler_params=_params(("parallel",)),
        name="rmsnorm_router",
    )(x2, g.reshape(1, d), w_router_pad)


def _mm_kernel(*refs, n_a, has_res):
    a_refs = refs[:n_a]
    w_refs = refs[n_a:2 * n_a]
    idx = 2 * n_a
    res_ref = refs[idx] if has_res else None
    idx += int(has_res)
    o_ref = refs[idx]
    wbf_refs = refs[idx + 1:idx + 1 + n_a]

    @pl.when(pl.program_id(1) == 0)
    def _():
        for w_ref, wbf_ref in zip(w_refs, wbf_refs):
            wbf_ref[...] = w_ref[...].astype(BF16)

    acc = None
    for a_ref, wbf_ref in zip(a_refs, wbf_refs):
        d = jnp.dot(a_ref[...].astype(BF16), wbf_ref[...], preferred_element_type=F32)
        acc = d if acc is None else acc + d
    if has_res:
        acc = acc + res_ref[...]
    o_ref[...] = acc.astype(o_ref.dtype)


def matmul(a_list, w, *, res=None, out_dtype=F32, tm=1024, tn=1024, name="matmul"):
    m = a_list[0].shape[0]
    n = w.shape[1]
    n_a = len(a_list)
    k_each = a_list[0].shape[1]
    assert all(a.shape == (m, k_each) for a in a_list) and w.shape[0] == n_a * k_each
    in_specs = [pl.BlockSpec((tm, k_each), lambda j, i: (i, 0)) for _ in a_list]
    in_specs += [pl.BlockSpec((k_each, tn), functools.partial(lambda j, i, kb: (kb, j), kb=kb))
                 for kb in range(n_a)]
    args = list(a_list) + [w] * n_a
    if res is not None:
        in_specs.append(pl.BlockSpec((tm, tn), lambda j, i: (i, j)))
        args.append(res)
    return pl.pallas_call(
        functools.partial(_mm_kernel, n_a=n_a, has_res=res is not None),
        out_shape=jax.ShapeDtypeStruct((m, n), out_dtype),
        grid=(n // tn, m // tm),
        in_specs=in_specs,
        out_specs=pl.BlockSpec((tm, tn), lambda j, i: (i, j)),
        scratch_shapes=[pltpu.VMEM((k_each, tn), BF16) for _ in a_list],
        compiler_params=_params(("parallel", "arbitrary"), big=True),
        name=name,
    )(*args)


def _rms_scale_bf16(x, g):
    y = x * lax.rsqrt(jnp.mean(x * x, axis=-1, keepdims=True) + RMS_EPS)
    return (y * g).astype(BF16)


def _ple_kernel(h_ref, p_ref, g_ref, wg_ref, wp_ref, o_ref, wg_bf, wp_bf):
    @pl.when(pl.program_id(1) == 0)
    def _():
        wg_bf[...] = wg_ref[...].astype(BF16)
        wp_bf[...] = wp_ref[...].astype(BF16)

    tn = o_ref.shape[1]
    a = jnp.dot(_rms_scale_bf16(h_ref[...], g_ref[...]), wg_bf[...], preferred_element_type=F32)
    pp = jnp.dot(p_ref[...].astype(BF16), wp_bf[...], preferred_element_type=F32)
    col = pl.multiple_of(pl.program_id(0) * tn, tn)
    o_ref[...] = h_ref[:, pl.ds(col, tn)] + jax.nn.sigmoid(a) * pp


def ple(h2, p_all, g, w_gate_all, w_proj_all, layer, tm=512, tn=2048):
    m, d = h2.shape
    pd = p_all.shape[2]
    return pl.pallas_call(
        _ple_kernel,
        out_shape=jax.ShapeDtypeStruct((m, d), F32),
        grid=(d // tn, m // tm),
        in_specs=[pl.BlockSpec((tm, d), lambda j, i: (i, 0)),
                  pl.BlockSpec((None, tm, pd), lambda j, i: (layer, i, 0)),
                  pl.BlockSpec((1, d), lambda j, i: (0, 0)),
                  pl.BlockSpec((None, d, tn), lambda j, i: (layer, 0, j), pipeline_mode=pl.Buffered(1)),
                  pl.BlockSpec((None, pd, tn), lambda j, i: (layer, 0, j), pipeline_mode=pl.Buffered(1))],
        out_specs=pl.BlockSpec((tm, tn), lambda j, i: (i, j)),
        scratch_shapes=[pltpu.VMEM((d, tn), BF16), pltpu.VMEM((pd, tn), BF16)],
        compiler_params=_params(("parallel", "arbitrary"), big=True),
        name="ple",
    )(h2, p_all, g.reshape(1, d), w_gate_all, w_proj_all)


def _row_copy(src_hbm, row, dst, i, sem):
    return pltpu.make_async_copy(src_hbm.at[pl.ds(row, 1), :], dst.at[pl.ds(i, 1), :], sem)


def _swiglu_kernel(exp_ref, rows_ref, src_ref, tok_ref, x_ref, *rest, n_col, gather, sub_rows, nf):
    del exp_ref
    gain_ref = None if gather else rest[0]
    wg_ref, wu_ref, wd_ref, o_ref = rest[0 if gather else 1:][:4]
    scratch = rest[(4 if gather else 5):]
    t = pl.program_id(0)
    f = pl.program_id(1)
    nt = pl.num_programs(0)
    n_rows = rows_ref[t]
    tm = o_ref.shape[0]

    if not gather:
        (xbf,) = scratch

        @pl.when(f == 0)
        def _():
            h = x_ref[...]
            o_ref[...] = h
            xbf[...] = _rms_scale_bf16(h, gain_ref[...])

    if gather:
        @pl.when(f == 0)
        def _():
            o_ref[...] = jnp.zeros_like(o_ref)

        xbuf, xbf, sems = scratch
        slot = t % 2
        n_fetch = xbuf.shape[1]
        per_step = n_fetch // nf

        def start_row(tile, sl, i):
            _row_copy(x_ref, tok_ref[src_ref[tile] + i], xbuf.at[sl], i, sems.at[sl, i]).start()

        def wait_all(sl):
            def body(c, carry):
                for r in range(DMA_LOOP_UNROLL):
                    i = c * DMA_LOOP_UNROLL + r
                    _row_copy(x_ref, 0, xbuf.at[sl], i, sems.at[sl, i]).wait()
                return carry
            lax.fori_loop(0, n_fetch // DMA_LOOP_UNROLL, body, 0)

        @pl.when(jnp.logical_and(f == 0, t == 0))
        def _():
            def body(i, carry):
                start_row(0, 0, i)
                return carry
            lax.fori_loop(0, n_fetch, body, 0, unroll=DMA_LOOP_UNROLL)

        fed = jnp.logical_or(t == 0, rows_ref[jnp.maximum(t - 1, 0)] > 0)

        @pl.when(jnp.logical_and(f == 0, fed))
        def _():
            wait_all(slot)
            xbf[...] = _unpack_bf16_pairs(xbuf[slot, 0:tm])

        def fetch_ahead():
            nxt = jnp.minimum(t + 1, nt - 1)
            for r in range(per_step):
                start_row(nxt, 1 - slot, f * per_step + r)
    else:
        def fetch_ahead():
            pass

    def ffn_rows(n):
        fetch_ahead()
        x = xbf[0:n, :]
        g = jnp.dot(x, wg_ref[...].astype(BF16), preferred_element_type=F32)
        u = jnp.dot(x, wu_ref[...].astype(BF16), preferred_element_type=F32)
        hmid = (g * jax.nn.sigmoid(g) * u).astype(BF16)
        wd = wd_ref[...].astype(BF16)
        cw = o_ref.shape[1] // n_col
        for c in range(n_col):
            y = jnp.dot(hmid, wd[:, c * cw:(c + 1) * cw], preferred_element_type=F32)
            o_ref[0:n, c * cw:(c + 1) * cw] += y

    for k in range(1, tm // sub_rows + 1):
        covers = jnp.logical_and(n_rows > (k - 1) * sub_rows, n_rows <= k * sub_rows)
        pl.when(covers)(functools.partial(ffn_rows, k * sub_rows))

    if gather:
        @pl.when(jnp.logical_and(n_rows > 0, jnp.logical_and(t == nt - 1, f == nf - 1)))
        def _():
            wait_all(1 - slot)


def gather_rows_per_tile(tm, nf):
    per_step = -(-tm // nf)
    while (per_step * nf) % DMA_LOOP_UNROLL:
        per_step += 1
    return per_step * nf


def swiglu(x, w_gate, w_up, w_down, tile_exp, tile_rows, tile_src=None, src_rows=None, *, gain=None,
           tm=MOE_TILE, tf=FFN_F_TILE, sub_rows=MOE_TILE, name="swiglu"):
    gather = src_rows is not None
    assert gather != (gain is not None)
    m = tile_rows.shape[0] * tm
    d = w_gate.shape[1]
    dff = w_gate.shape[2]
    nf = dff // tf
    nt = m // tm
    assert nf >= 2

    def w_in_map(t, f, exp_ref, rows_ref, src_ref, tok_ref):
        return (exp_ref[t], 0, jnp.where(rows_ref[t] > 0, f, nf - 1))

    def w_out_map(t, f, exp_ref, rows_ref, src_ref, tok_ref):
        return (exp_ref[t], jnp.where(rows_ref[t] > 0, f, nf - 1), 0)

    def tile_map(t, f, exp_ref, rows_ref, src_ref, tok_ref):
        return (t, 0)

    w_specs = [pl.BlockSpec((None, d, tf), w_in_map),
               pl.BlockSpec((None, d, tf), w_in_map),
               pl.BlockSpec((None, tf, d), w_out_map)]
    if gather:
        in_specs = [pl.BlockSpec(memory_space=pl.ANY)] + w_specs
        args = [x, w_gate, w_up, w_down]
        n_fetch = gather_rows_per_tile(tm, nf)
        scratch = [pltpu.VMEM((2, n_fetch, d // 2), jnp.uint32), pltpu.VMEM((tm, d), BF16),
                   pltpu.SemaphoreType.DMA((2, n_fetch))]
    else:
        in_specs = [pl.BlockSpec((tm, d), tile_map, pipeline_mode=pl.Buffered(1)),
                    pl.BlockSpec((1, d), lambda t, f, e, r, s, k: (0, 0))] + w_specs
        args = [x, gain.reshape(1, d), w_gate, w_up, w_down]
        scratch = [pltpu.VMEM((tm, d), BF16)]
        tile_src = jnp.zeros((nt,), jnp.int32)
        src_rows = jnp.zeros((1,), jnp.int32)
    grid_spec = pltpu.PrefetchScalarGridSpec(
        num_scalar_prefetch=4,
        grid=(nt, nf),
        in_specs=in_specs,
        out_specs=pl.BlockSpec((tm, d), tile_map, pipeline_mode=pl.Buffered(1)),
        scratch_shapes=scratch,
    )
    return pl.pallas_call(
        functools.partial(_swiglu_kernel, n_col=FFN_OUT_CHUNKS, gather=gather, sub_rows=sub_rows, nf=nf),
        out_shape=jax.ShapeDtypeStruct((m, d), F32),
        grid_spec=grid_spec,
        compiler_params=_params(("arbitrary", "arbitrary"), big=True),
        name=name,
    )(tile_exp, tile_rows, tile_src, src_rows, *args)


COMBINE_TILE = 128


def _combine_kernel(d0_ref, d1_ref, ys_hbm, gate_ref, h_ref, hout_ref, buf, sems):
    i = pl.program_id(0)
    n = pl.num_programs(0)
    tt = h_ref.shape[0]
    slot = i % 2
    dests = (d0_ref, d1_ref)

    def start_tile(tile, sl):
        def body(r, c):
            for k, d_ref in enumerate(dests):
                _row_copy(ys_hbm, d_ref[tile * tt + r], buf.at[sl, k], r, sems.at[sl, k, r]).start()
            return c
        lax.fori_loop(0, tt, body, 0, unroll=DMA_LOOP_UNROLL)

    @pl.when(i == 0)
    def _():
        start_tile(0, 0)

    @pl.when(i + 1 < n)
    def _():
        start_tile(i + 1, 1 - slot)

    def wait_body(r, c):
        for k in range(TOP_K):
            _row_copy(ys_hbm, 0, buf.at[slot, k], r, sems.at[slot, k, r]).wait()
        return c
    lax.fori_loop(0, tt, wait_body, 0, unroll=DMA_LOOP_UNROLL)

    gates = gate_ref[...]
    hout_ref[...] = h_ref[...] + (buf[slot, 0] * gates[:, 0:1] + buf[slot, 1] * gates[:, 1:2])


def moe_combine(h2, ys, dest2, gates, tt=COMBINE_TILE):
    n, d = h2.shape
    grid_spec = pltpu.PrefetchScalarGridSpec(
        num_scalar_prefetch=2,
        grid=(n // tt,),
        in_specs=[pl.BlockSpec(memory_space=pl.ANY),
                  pl.BlockSpec((tt, TOP_K), lambda i, a, b: (i, 0)),
                  pl.BlockSpec((tt, d), lambda i, a, b: (i, 0))],
        out_specs=pl.BlockSpec((tt, d), lambda i, a, b: (i, 0)),
        scratch_shapes=[pltpu.VMEM((2, TOP_K, tt, d), F32), pltpu.SemaphoreType.DMA((2, TOP_K, tt))],
    )
    return pl.pallas_call(
        _combine_kernel,
        out_shape=jax.ShapeDtypeStruct((n, d), F32),
        grid_spec=grid_spec,
        compiler_params=_params(("arbitrary",), big=True),
        name="moe_combine",
    )(dest2[:, 0], dest2[:, 1], ys, gates, h2)


def _t5_bucket(rel):
    half = T5_BUCKETS // 2
    max_exact = half // 2
    n = jnp.abs(rel)
    log_ratio = jnp.log(jnp.maximum(n, 1).astype(F32) / max_exact) / math.log(T5_MAX_DIST / max_exact)
    large = jnp.minimum(max_exact + (log_ratio * (half - max_exact)).astype(jnp.int32), half - 1)
    return jnp.where(rel > 0, half, 0) + jnp.where(n < max_exact, n, large)


def _window_bias_table(t5_bias):
    i = jnp.arange(A_BLOCK)[:, None]
    j = jnp.arange(3 * A_BLOCK)[None, :]
    rel = j - A_BLOCK - i
    onehot = (_t5_bucket(rel)[None] == jnp.arange(T5_BUCKETS)[:, None, None]).astype(F32)
    bias = jnp.einsum('bh,bij->hij', t5_bias.astype(F32), onehot, precision=lax.Precision.HIGHEST)
    return jnp.where((jnp.abs(rel) <= A_BLOCK)[None], bias, NEG_INF)


def _window_kernel(sink_ref, q_ref, kv_ref, bias_ref, o_ref, *, nb):
    n = pl.program_id(1)
    scale = HEAD_DIM ** -0.5
    starts = (jnp.maximum(n - 1, 0), n, jnp.minimum(n + 1, nb - 1))
    kv = [kv_ref[pl.ds(pl.multiple_of(s * A_BLOCK, A_BLOCK), A_BLOCK), :] for s in starts]
    col = lax.broadcasted_iota(jnp.int32, (A_BLOCK, 3 * A_BLOCK), 1)
    edge_ok = jnp.logical_and(jnp.logical_or(n > 0, col >= A_BLOCK),
                              jnp.logical_or(n < nb - 1, col < 2 * A_BLOCK))
    kv = jnp.concatenate(kv, axis=0)
    q = q_ref[...]
    lo = lax.broadcasted_iota(jnp.int32, (1, 2 * HEAD_DIM), 1) < HEAD_DIM
    sel = (jnp.where(lo, scale, 0.0).astype(BF16), jnp.where(lo, 0.0, scale).astype(BF16))
    pairs_per_kv = A_GROUP // 2
    kk, vv = [], []
    for kh in range(A_KV_HEADS):
        k1 = kv[:, kh * HEAD_DIM:(kh + 1) * HEAD_DIM]
        v1 = kv[:, A_KV_WIDTH + kh * HEAD_DIM:A_KV_WIDTH + (kh + 1) * HEAD_DIM]
        kk.append(jnp.concatenate([k1, k1], axis=1))
        vv.append(jnp.concatenate([v1, v1], axis=1))
    scores = []
    for pr in range(A_Q_HEADS // 2):
        q2 = q[:, pr * 2 * HEAD_DIM:(pr + 1) * 2 * HEAD_DIM]
        qq = jnp.concatenate([q2 * sel[0], q2 * sel[1]], axis=0)
        s2 = lax.dot_general(qq, kk[pr // pairs_per_kv], (((1,), (1,)), ((), ())),
                             preferred_element_type=F32)
        for i in range(2):
            scores.append(jnp.where(edge_ok, s2[i * A_BLOCK:(i + 1) * A_BLOCK] + bias_ref[2 * pr + i],
                                    NEG_INF))
    probs, dens = [], []
    for h, s in enumerate(scores):
        sink = sink_ref[h]
        mx = jnp.maximum(jnp.max(s, axis=-1, keepdims=True), sink)
        p = jnp.exp(s - mx)
        dens.append(jnp.sum(p, axis=-1, keepdims=True) + jnp.exp(sink - mx))
        probs.append(p.astype(BF16))
    for pr in range(A_Q_HEADS // 2):
        v2 = vv[pr // pairs_per_kv]
        oa = jnp.dot(probs[2 * pr], v2, preferred_element_type=F32) / dens[2 * pr]
        ob = jnp.dot(probs[2 * pr + 1], v2, preferred_element_type=F32) / dens[2 * pr + 1]
        o_ref[:, pr * 2 * HEAD_DIM:(pr + 1) * 2 * HEAD_DIM] = jnp.where(lo, oa, ob).astype(o_ref.dtype)


def window_attention(proj_a, t5_bias, sink):
    b, s, _ = proj_a.shape
    nb = s // A_BLOCK
    kvw = 2 * A_KV_WIDTH
    grid_spec = pltpu.PrefetchScalarGridSpec(
        num_scalar_prefetch=0,
        grid=(b, nb),
        in_specs=[pl.BlockSpec(memory_space=pltpu.SMEM),
                  pl.BlockSpec((None, A_BLOCK, A_WIDTH), lambda bi, n: (bi, n, 0)),
                  pl.BlockSpec((None, s, kvw), lambda bi, n: (bi, 0, A_WIDTH // kvw)),
                  pl.BlockSpec((A_Q_HEADS, A_BLOCK, 3 * A_BLOCK), lambda bi, n: (0, 0, 0))],
        out_specs=pl.BlockSpec((None, A_BLOCK, A_WIDTH), lambda bi, n: (bi, n, 0)),
    )
    return pl.pallas_call(
        functools.partial(_window_kernel, nb=nb),
        out_shape=jax.ShapeDtypeStruct((b, s, A_WIDTH), BF16),
        grid_spec=grid_spec,
        compiler_params=_params(("parallel", "arbitrary")),
        name="window_attention",
    )(sink.astype(F32), proj_a, proj_a, _window_bias_table(t5_bias))


NA_HEAD_GROUP = 16


def _na_bias_table(rpb, rows):
    kh = min(NA_WIN_H, rows)
    kw = NA_WIN_W
    c = jnp.arange(GRID_W)
    cs = jnp.clip(c - kw // 2, 0, GRID_W - kw)
    col_ok = (c[None] >= cs[:, None]) & (c[None] < cs[:, None] + kw)
    col_off = jnp.clip(c[None] - c[:, None], -(kw - 1), kw - 1) + kw - 1
    hi = lax.Precision.HIGHEST
    row_off = jnp.arange(kh)[None, :] - jnp.arange(kh)[:, None] + NA_WIN_H - 1
    row_sel = (row_off[:, :, None] == jnp.arange(2 * NA_WIN_H - 1)[None, None, :]).astype(F32)
    col_sel = (col_off[None] == jnp.arange(2 * kw - 1)[:, None, None]).astype(F32)
    by_row = jnp.einsum('hrc,dir->dhic', rpb.astype(F32), row_sel, precision=hi)
    bias = jnp.einsum('dhic,cqk->dhqik', by_row, col_sel, precision=hi)
    bias = jnp.where(col_ok[None, None, :, None, :], bias, NEG_INF)
    return bias.reshape(kh, rpb.shape[0], GRID_W, kh * GRID_W)


def _na_kernel(q_ref, k_ref, v_ref, bias_ref, o_ref, *, rows):
    r = pl.program_id(2)
    kh = min(NA_WIN_H, rows)
    scale = HEAD_DIM ** -0.5
    start = pl.multiple_of(jnp.clip(r - kh // 2, 0, rows - kh) * GRID_W, GRID_W)
    k = k_ref[pl.ds(start, kh * GRID_W), :]
    v = v_ref[pl.ds(start, kh * GRID_W), :]
    q = q_ref[...]
    lo = lax.broadcasted_iota(jnp.int32, (1, 2 * HEAD_DIM), 1) < HEAD_DIM
    sel = (jnp.where(lo, scale, 0.0).astype(BF16), jnp.where(lo, 0.0, scale).astype(BF16))
    n_pairs = NA_HEAD_GROUP // 2
    scores = []
    for pr in range(n_pairs):
        sl = slice(pr * 2 * HEAD_DIM, (pr + 1) * 2 * HEAD_DIM)
        q2 = q[:, sl]
        qq = jnp.concatenate([q2 * sel[0], q2 * sel[1]], axis=0)
        s2 = lax.dot_general(qq, k[:, sl], (((1,), (1,)), ((), ())), preferred_element_type=F32)
        for i in range(2):
            scores.append(s2[i * GRID_W:(i + 1) * GRID_W] + bias_ref[2 * pr + i])
    probs, dens = [], []
    for s in scores:
        mx = jnp.max(s, axis=-1, keepdims=True)
        p = jnp.exp(s - mx)
        dens.append(jnp.sum(p, axis=-1, keepdims=True))
        probs.append(p.astype(BF16))
    for pr in range(n_pairs):
        sl = slice(pr * 2 * HEAD_DIM, (pr + 1) * 2 * HEAD_DIM)
        oa = jnp.dot(probs[2 * pr], v[:, sl], preferred_element_type=F32) / dens[2 * pr]
        ob = jnp.dot(probs[2 * pr + 1], v[:, sl], preferred_element_type=F32) / dens[2 * pr + 1]
        o_ref[:, sl] = jnp.where(lo, oa, ob).astype(o_ref.dtype)


def neighbourhood_attention(proj_n, rpb):
    b, s, _ = proj_n.shape
    rows = s // GRID_W
    kh = min(NA_WIN_H, rows)
    gw = NA_HEAD_GROUP * HEAD_DIM
    ng = B_WIDTH // gw

    def bias_map(bi, g, r):
        return (r - jnp.clip(r - kh // 2, 0, rows - kh), g, 0, 0)

    return pl.pallas_call(
        functools.partial(_na_kernel, rows=rows),
        out_shape=jax.ShapeDtypeStruct((b, s, B_WIDTH), BF16),
        grid=(b, ng, rows),
        in_specs=[pl.BlockSpec((None, GRID_W, gw), lambda bi, g, r: (bi, r, g)),
                  pl.BlockSpec((None, s, gw), lambda bi, g, r: (bi, 0, ng + g)),
                  pl.BlockSpec((None, s, gw), lambda bi, g, r: (bi, 0, 2 * ng + g)),
                  pl.BlockSpec((None, NA_HEAD_GROUP, GRID_W, kh * GRID_W), bias_map)],
        out_specs=pl.BlockSpec((None, GRID_W, gw), lambda bi, g, r: (bi, r, g)),
        compiler_params=_params(("parallel", "parallel", "arbitrary"), big=True),
        name="neighbourhood_attention",
    )(proj_n, proj_n, proj_n, _na_bias_table(rpb, rows))


CONV_HALO = 16
CONV_COL_CHUNK = 512


def _hyena_in_kernel(prev_ref, cur_ref, next_ref, w_ref, b_ref, cw_ref, cb_ref, o_ref, wbf_ref, *,
                     tiles_per_seq):
    i = pl.program_id(1)

    @pl.when(i == 0)
    def _():
        wbf_ref[...] = w_ref[...].astype(BF16)

    tm = cur_ref.shape[0]
    rows = tm + 2 * CONV_HALO
    a = jnp.concatenate([prev_ref[...], cur_ref[...], next_ref[...]], axis=0)
    pos = i % tiles_per_seq
    row = lax.broadcasted_iota(jnp.int32, (rows, 1), 0)
    kill_up = jnp.logical_and(pos == 0, row == CONV_HALO)
    kill_dn = jnp.logical_and(pos == tiles_per_seq - 1, row == CONV_HALO + tm - 1)
    for c in range(o_ref.shape[1] // CONV_COL_CHUNK):
        cs = slice(c * CONV_COL_CHUNK, (c + 1) * CONV_COL_CHUNK)
        p = jnp.dot(a, wbf_ref[:, cs], preferred_element_type=F32) + b_ref[:, cs]
        up = jnp.where(kill_up, 0.0, pltpu.roll(p, 1, 0))
        dn = jnp.where(kill_dn, 0.0, pltpu.roll(p, rows - 1, 0))
        y = up * cw_ref[0:1, cs] + p * cw_ref[1:2, cs] + dn * cw_ref[2:3, cs] + cb_ref[:, cs]
        o_ref[:, cs] = y[CONV_HALO:CONV_HALO + tm]


def hyena_in_conv(hn, w_in, b_in, conv_w, conv_b, seq, tm=1024, tn=1024):
    m, k = hn.shape
    n = w_in.shape[1]
    hb = tm // CONV_HALO
    last_halo = m // CONV_HALO - 1
    return pl.pallas_call(
        functools.partial(_hyena_in_kernel, tiles_per_seq=seq // tm),
        out_shape=jax.ShapeDtypeStruct((m, n), F32),
        grid=(n // tn, m // tm),
        in_specs=[pl.BlockSpec((CONV_HALO, k), lambda j, i: (jnp.maximum(i * hb - 1, 0), 0)),
                  pl.BlockSpec((tm, k), lambda j, i: (i, 0)),
                  pl.BlockSpec((CONV_HALO, k), lambda j, i: (jnp.minimum((i + 1) * hb, last_halo), 0)),
                  pl.BlockSpec((k, tn), lambda j, i: (0, j)),
                  pl.BlockSpec((1, tn), lambda j, i: (0, j)),
                  pl.BlockSpec((3, tn), lambda j, i: (0, j)),
                  pl.BlockSpec((1, tn), lambda j, i: (0, j))],
        out_specs=pl.BlockSpec((tm, tn), lambda j, i: (i, j)),
        scratch_shapes=[pltpu.VMEM((k, tn), BF16)],
        compiler_params=_params(("parallel", "arbitrary"), big=True),
        name="hyena_in_conv",
    )(hn, hn, hn, w_in, b_in.reshape(1, n), conv_w, conv_b.reshape(1, n))


def _filter_stage1_kernel(z_ref, t_ref, fw1_ref, fb1_ref, fw2_ref, fb2_ref, fr_ref, w3f0_ref, w3f1_ref,
                          w3b0_ref, w3b1_ref, delta_ref, f1_ref, y_ref, hid_ref, *, seq):
    j = pl.program_id(0)

    @pl.when(pl.program_id(1) == 0)
    def _():
        hi = lax.Precision.HIGHEST
        fr = fr_ref[...]
        hid = jnp.sin(fr * (jnp.dot(z_ref[...], fw1_ref[...], preferred_element_type=F32, precision=hi)
                            + fb1_ref[...]))
        hid = jnp.sin(fr * (jnp.dot(hid, fw2_ref[...], preferred_element_type=F32, precision=hi)
                            + fb2_ref[...]))
        hid_ref[...] = hid.astype(BF16)

    hid_bf = hid_ref[...]
    r = lax.broadcasted_iota(jnp.int32, (hid_bf.shape[0], 1), 0)
    u = j * FFT_CHUNK + r // FFT_N2 + FFT_N1 * (r % FFT_N2)
    decay = jnp.where(u == seq, 0.0, jnp.exp(-t_ref[...] * delta_ref[...]))
    for o, (wf_ref, wb_ref) in enumerate(((w3f0_ref, w3b0_ref), (w3f1_ref, w3b1_ref))):
        fwd = jnp.dot(hid_bf, wf_ref[...].astype(BF16), preferred_element_type=F32)
        bwd = jnp.dot(hid_bf, wb_ref[...].astype(BF16), preferred_element_type=F32)
        taps = (jnp.where(u > seq, bwd, jnp.where(u == 0, fwd + bwd, fwd)) * decay).astype(BF16)
        for jj in range(FFT_CHUNK):
            x = taps[jj * FFT_N2:(jj + 1) * FFT_N2]
            y_ref[o, jj] = _pack_bf16_pairs(jnp.dot(f1_ref[jj], x, preferred_element_type=F32))


def hyena_filter_stage1(seq, fw1, fb1, fw2, fb2, freq, fw3, width, consts, cb=FFT_CB):
    _, f1_real, _, _, _ = consts
    n = 2 * seq
    n1, n2, ch = FFT_N1, FFT_N2, FFT_CHUNK
    assert fw3.shape[1] == 4 * width
    r = jnp.arange(n)
    u = (r // (ch * n2)) * ch + (r // n2) % ch + n1 * (r % n2)
    pos = jnp.where(u <= seq, u, n - u)
    pos = jnp.where(u == seq, 0, pos)
    bands = (HYENA_EMB - 1) // 2
    f = jnp.linspace(1e-4, bands - 1, bands, dtype=F32)[None]
    t = (pos.astype(F32) / (seq - 1))[:, None]
    w = ((2.0 * math.pi / seq) * pos.astype(F32))[:, None]
    z = jnp.concatenate([t, jnp.cos(f * w), -jnp.sin(f * w)], axis=-1)
    emb_pad = 40
    z = jnp.pad(z, ((0, 0), (0, emb_pad - HYENA_EMB)))
    fw1p = jnp.pad(fw1.astype(F32), ((0, emb_pad - HYENA_EMB), (0, 0)))
    fh = fw1.shape[1]
    max_decay = math.log(HYENA_DECAY_TARGET) / HYENA_FAST_PCT
    min_decay = math.log(HYENA_DECAY_TARGET) / HYENA_SLOW_PCT
    deltas = jnp.abs(jnp.linspace(min_decay, max_decay, width, dtype=F32))[None]
    ncb = width // cb
    rows = ch * n2

    def small(shape):
        return pl.BlockSpec(shape, lambda j, k: (0, 0))

    def w3_spec(blk):
        return pl.BlockSpec((fh, cb), lambda j, k: (0, blk * ncb + k))

    return pl.pallas_call(
        functools.partial(_filter_stage1_kernel, seq=seq),
        out_shape=jax.ShapeDtypeStruct((2, n1, 2 * n2, width // 2), jnp.uint32),
        grid=(n1 // ch, ncb),
        in_specs=[pl.BlockSpec((rows, emb_pad), lambda j, k: (j, 0)),
                  pl.BlockSpec((rows, 1), lambda j, k: (j, 0)),
                  small((emb_pad, fh)), small((1, fh)), small((fh, fh)), small((1, fh)), small((1, fh)),
                  w3_spec(0), w3_spec(1), w3_spec(2), w3_spec(3),
                  pl.BlockSpec((1, cb), lambda j, k: (0, k)),
                  pl.BlockSpec((ch, 2 * n2, n2), lambda j, k: (j, 0, 0))],
        out_specs=pl.BlockSpec((2, ch, 2 * n2, cb // 2), lambda j, k: (0, j, 0, k)),
        scratch_shapes=[pltpu.VMEM((rows, fh), BF16)],
        compiler_params=_params(("parallel", "arbitrary"), big=True),
        name="hyena_filter_stage1",
    )(z, t, fw1p, fb1.reshape(1, fh).astype(F32), fw2.astype(F32), fb2.reshape(1, fh).astype(F32),
      freq.reshape(1, fh).astype(F32), fw3, fw3, fw3, fw3, deltas, f1_real)


def _dft_constants():
    n1, n2 = FFT_N1, FFT_N2
    n = n1 * n2
    a1 = np.arange(n1)
    a2 = np.arange(n2)
    half = n2 // 2

    def cplx_block(w):
        return np.block([[w.real, -w.imag], [w.imag, w.real]])

    ang = (a2[None, :, None] * a2[None, None, :] / n2) + (a1[:, None, None] * a2[None, :, None] / n)
    w1 = np.exp(-2j * np.pi * ang)
    f1_pad = np.stack([cplx_block(w1[i][:, :half]) for i in range(n1)])
    f1_real = np.stack([np.concatenate([w1[i].real, w1[i].imag], axis=0) for i in range(n1)])
    w2 = np.exp(-2j * np.pi * (a1[:, None] * a1[None, :]) / n1)
    f2 = cplx_block(w2)
    ang = (a1[None, :, None] * a1[None, None, :] / n1) + (a2[:, None, None] * a1[None, :, None] / n)
    g2 = np.stack([cplx_block(m_) for m_ in np.exp(2j * np.pi * ang)])
    wg1 = np.exp(2j * np.pi * (a2[:half, None] * a2[None, :]) / n2) / n
    g1 = cplx_block(wg1)
    to = lambda x: jnp.asarray(x.astype(np.float32)).astype(BF16)
    return to(f1_pad), to(f1_real), to(f2), to(g2), to(g1)


def _fft_stage1_block(x_re, x_im, f_ref, y_ref):
    xr = pltpu.einshape("mjc->jmc", x_re)
    xi = pltpu.einshape("mjc->jmc", x_im)
    for j in range(FFT_CHUNK):
        xc = jnp.concatenate([xr[j], xi[j]], axis=0).astype(BF16)
        y_ref[j] = _pack_bf16_pairs(jnp.dot(f_ref[j], xc, preferred_element_type=F32))


def _fft_stage1_kernel(x_ref, f_ref, y_ref):
    _fft_stage1_block(x_ref[0], x_ref[1], f_ref, y_ref)


def _fft_stage2_kernel(yr_ref, yi_ref, f2_ref, h_ref):
    yr = pltpu.einshape("nkc->knc", yr_ref[...])
    yi = pltpu.einshape("nkc->knc", yi_ref[...])
    for j in range(FFT_CHUNK):
        yc = _unpack_bf16_pairs(jnp.concatenate([yr[j], yi[j]], axis=0))
        h_ref[j] = _pack_bf16_pairs(jnp.dot(f2_ref[...], yc, preferred_element_type=F32))


def _fft_mid_kernel(yr_ref, yi_ref, f2_ref, h_ref, g2_ref, q_ref):
    yr = pltpu.einshape("nkc->knc", yr_ref[...])
    yi = pltpu.einshape("nkc->knc", yi_ref[...])
    n1 = FFT_N1
    spec = [jnp.dot(f2_ref[...], _unpack_bf16_pairs(jnp.concatenate([yr[j], yi[j]], axis=0)),
                    preferred_element_type=F32) for j in range(FFT_CHUNK)]
    prod = []
    for j, z in enumerate(spec):
        hf = _unpack_pairs_f32(h_ref[j])
        zr, zi, hr, hi = z[:n1], z[n1:], hf[:n1], hf[n1:]
        prod.append(jnp.concatenate([zr * hr - zi * hi, zr * hi + zi * hr], axis=0).astype(BF16))
    for j, pc in enumerate(prod):
        q_ref[j] = _pack_bf16_pairs(jnp.dot(g2_ref[j], pc, preferred_element_type=F32))


def _fft_last_kernel(qr_ref, qi_ref, g1_ref, gate_ref, zf_ref, fb_ref, *rest, feeds_next):
    f1_ref, o_ref, y_ref = rest if feeds_next else (None, rest[0], None)
    qr = pltpu.einshape("kjc->jkc", qr_ref[...])
    qi = pltpu.einshape("kjc->jkc", qi_ref[...])
    half = FFT_N2 // 2
    conv = []
    for j in range(FFT_CHUNK):
        qc = _unpack_bf16_pairs(jnp.concatenate([qr[j], qi[j]], axis=0))
        conv.append(jnp.dot(g1_ref[...], qc, preferred_element_type=F32))
    conv = jnp.stack(conv, axis=0)
    fb = fb_ref[...]
    out = []
    for bi in range(2):
        cb = pltpu.einshape("jmc->mjc", conv[:, bi * half:(bi + 1) * half])
        out.append(gate_ref[bi] * (cb + zf_ref[bi] * fb))
        o_ref[bi] = out[bi]
    if feeds_next:
        _fft_stage1_block(out[0], out[1], f1_ref, y_ref)


def hyena_filter_spectrum(y, consts, cb=FFT_CB):
    _, _, f2, _, _ = consts
    no = y.shape[0]
    c = 2 * y.shape[-1]
    n1, n2, ch = FFT_N1, FFT_N2, FFT_CHUNK
    nk = n2 // ch
    return pl.pallas_call(
        _fft_stage2_kernel,
        out_shape=jax.ShapeDtypeStruct((no, n2, 2 * n1, c // 2), jnp.uint32),
        grid=(no, nk, c // cb),
        in_specs=[pl.BlockSpec((None, n1, ch, cb // 2), lambda o, j, k: (o, 0, j, k)),
                  pl.BlockSpec((None, n1, ch, cb // 2), lambda o, j, k: (o, 0, nk + j, k)),
                  pl.BlockSpec((2 * n1, 2 * n1), lambda o, j, k: (0, 0))],
        out_specs=pl.BlockSpec((None, ch, 2 * n1, cb // 2), lambda o, j, k: (o, j, 0, k)),
        compiler_params=_params(("parallel", "parallel", "parallel"), big=True),
        name="fft_filter_stage2",
    )(y, y, f2)


def hyena_long_conv_gate(zsrc, z_col, gate_src, gate_col, hf, order, fbias, consts, y=None,
                         feeds_next=False, cb=FFT_CB):
    f1_pad, _, f2, g2, g1 = consts
    n1, n2, ch = FFT_N1, FFT_N2, FFT_CHUNK
    c = 2 * hf.shape[-1]
    ncb = c // cb
    half = n2 // 2
    y_shape = jax.ShapeDtypeStruct((n1, 2 * n2, c // 2), jnp.uint32)
    f1_spec = pl.BlockSpec((ch, 2 * n2, n2), lambda j, k: (j, 0, 0))
    y_spec = pl.BlockSpec((ch, 2 * n2, cb // 2), lambda j, k: (j, 0, k))
    if y is None:
        y = pl.pallas_call(
            _fft_stage1_kernel,
            out_shape=y_shape,
            grid=(n1 // ch, ncb),
            in_specs=[pl.BlockSpec((2, half, ch, cb), lambda j, k: (0, 0, j, z_col * ncb + k)), f1_spec],
            out_specs=y_spec,
            compiler_params=_params(("parallel", "parallel"), big=True),
            name="fft_stage1",
        )(zsrc, f1_pad)
    nk = n2 // ch
    q = pl.pallas_call(
        _fft_mid_kernel,
        out_shape=jax.ShapeDtypeStruct((n2, 2 * n1, c // 2), jnp.uint32),
        grid=(nk, ncb),
        in_specs=[pl.BlockSpec((n1, ch, cb // 2), lambda j, k: (0, j, k)),
                  pl.BlockSpec((n1, ch, cb // 2), lambda j, k: (0, nk + j, k)),
                  pl.BlockSpec((2 * n1, 2 * n1), lambda j, k: (0, 0)),
                  pl.BlockSpec((None, ch, 2 * n1, cb // 2), lambda j, k: (order, j, 0, k)),
                  pl.BlockSpec((ch, 2 * n1, 2 * n1), lambda j, k: (j, 0, 0))],
        out_specs=pl.BlockSpec((ch, 2 * n1, cb // 2), lambda j, k: (j, 0, k)),
        compiler_params=_params(("parallel", "parallel"), big=True),
        name="fft_mid",
    )(y, y, f2, hf, g2)
    nj = n1 // ch
    in_specs = [pl.BlockSpec((n2, ch, cb // 2), lambda j, k: (0, j, k)),
                pl.BlockSpec((n2, ch, cb // 2), lambda j, k: (0, nj + j, k)),
                pl.BlockSpec((2 * half, 2 * n2), lambda j, k: (0, 0)),
                pl.BlockSpec((2, half, ch, cb), lambda j, k: (0, 0, j, gate_col * ncb + k)),
                pl.BlockSpec((2, half, ch, cb), lambda j, k: (0, 0, j, z_col * ncb + k)),
                pl.BlockSpec((None, 1, cb), lambda j, k: (order, 0, k))]
    args = [q, q, g1, gate_src, zsrc, fbias.reshape(fbias.shape[0], 1, c)]
    out_shape = jax.ShapeDtypeStruct((2, half, n1, c), F32)
    out_spec = pl.BlockSpec((2, half, ch, cb), lambda j, k: (0, 0, j, k))
    if feeds_next:
        in_specs.append(f1_spec)
        args.append(f1_pad)
        out_shape, out_spec = (out_shape, y_shape), (out_spec, y_spec)
    res = pl.pallas_call(
        functools.partial(_fft_last_kernel, feeds_next=feeds_next),
        out_shape=out_shape,
        grid=(nj, ncb),
        in_specs=in_specs,
        out_specs=out_spec,
        compiler_params=_params(("parallel", "parallel"), big=True),
        name="fft_last",
    )(*args)
    return res if feeds_next else (res, None)


def hyena_mixer(hn, h_res, w_in, b_in, conv_w, conv_b, fw1, fb1, fw2, fb2, freq, fw3, fbias, w_out,
                batch, seq):
    width = w_out.shape[0]
    assert batch == 2 and 2 * seq == FFT_N1 * FFT_N2
    sc = hyena_in_conv(hn, w_in, b_in, conv_w, conv_b, seq)
    consts = _dft_constants()
    hf = hyena_filter_spectrum(hyena_filter_stage1(seq, fw1, fb1, fw2, fb2, freq, fw3, width, consts),
                               consts)
    sc4 = sc.reshape(batch, FFT_N2 // 2, FFT_N1, 3 * width)
    zf1, y1 = hyena_long_conv_gate(sc4, 2, sc4, 0, hf, 0, fbias, consts, feeds_next=True)
    zf2, _ = hyena_long_conv_gate(zf1, 0, sc4, 1, hf, 1, fbias, consts, y=y1)
    return matmul([zf2.reshape(batch * seq, width)], w_out, res=h_res, tm=512, tn=1024,
                  name="hyena_out")


def moe_swiglu(hn2_packed, logits, wg, wu, wd):
    n = hn2_packed.shape[0]
    top_v, top_i = lax.top_k(logits, TOP_K)
    gates = jax.nn.softmax(top_v, axis=-1)
    e_flat = top_i.reshape(-1).astype(jnp.int32)
    nk = n * TOP_K
    onehot = (e_flat[:, None] == jnp.arange(N_EXPERTS, dtype=jnp.int32)[None]).astype(jnp.int32)
    csum = jnp.cumsum(onehot, axis=0)
    rank = jnp.take_along_axis(csum, e_flat[:, None], axis=1)[:, 0] - 1
    counts = csum[-1]
    padded = ((counts + MOE_TILE - 1) // MOE_TILE) * MOE_TILE
    pad_end = jnp.cumsum(padded)
    pad_start = pad_end - padded
    dest = pad_start[e_flat] + rank
    p_rows = nk + N_EXPERTS * MOE_TILE
    order = jnp.argsort(e_flat, stable=True).astype(jnp.int32)
    nf = wg.shape[2] // MOE_F_TILE
    sorted_tok = jnp.pad(order // TOP_K, (0, gather_rows_per_tile(MOE_TILE, nf)))
    start = jnp.cumsum(counts) - counts
    nt = p_rows // MOE_TILE
    tile_start = jnp.arange(nt, dtype=jnp.int32) * MOE_TILE
    tile_used = tile_start < pad_end[-1]
    tile_exp = jnp.minimum(jnp.searchsorted(pad_end, tile_start, side='right'), N_EXPERTS - 1).astype(jnp.int32)
    tile_rows = jnp.clip(counts[tile_exp] - (tile_start - pad_start[tile_exp]), 0, MOE_TILE)
    tile_rows = jnp.where(tile_used, tile_rows, 0).astype(jnp.int32)
    last_exp = tile_exp[jnp.maximum(jnp.sum(tile_used.astype(jnp.int32)) - 1, 0)]
    tile_src = jnp.where(tile_used, start[tile_exp] + (tile_start - pad_start[tile_exp]), 0)
    tile_exp = jnp.where(tile_used, tile_exp, last_exp)
    ys = swiglu(hn2_packed, wg, wu, wd, tile_exp, tile_rows, tile_src.astype(jnp.int32), sorted_tok,
                tf=MOE_F_TILE, sub_rows=FFN_SUB_ROWS, name="swiglu_experts")
    return ys, dest.reshape(n, TOP_K), gates


def kernel(x, p, ln_mix, ln_ffn, ln_ple, final_norm, t5_bias, w_attn_in, w_attn_out, attn_sink, na_rpb, w_ffn_gate, w_ffn_up, w_ffn_down, w_hy_in, b_hy_in, w_hy_conv, b_hy_conv, w_hy_f1, b_hy_f1, w_hy_f2, b_hy_f2, hy_freq, w_hy_f3, hy_bias, w_hy_out, w_router, w_exp_gate, w_exp_up, w_exp_down, w_ple_proj, w_ple_gate):
    batch, seq, d = x.shape
    n = batch * seq
    depth = ln_mix.shape[0]
    h = x.reshape(n, d)
    for i in range(depth):
        li = i // 2
        hn = rmsnorm(h, ln_mix[i], BF16)
        if i % 2 == 0:
            na_off = A_WIDTH + 2 * A_KV_WIDTH
            w_in = w_attn_in[li]
            proj_a = matmul([hn], w_in[:, :na_off], out_dtype=BF16, tn=na_off, name="attn_in_a")
            proj_n = matmul([hn], w_in[:, na_off:], out_dtype=BF16, tn=3 * B_WIDTH // 2, name="attn_in_n")
            oa = window_attention(proj_a.reshape(batch, seq, -1), t5_bias, attn_sink[li])
            ob = neighbourhood_attention(proj_n.reshape(batch, seq, -1), na_rpb[li])
            h = matmul([oa.reshape(n, A_WIDTH), ob.reshape(n, B_WIDTH)], w_attn_out[li], res=h,
                       name="attn_out")
            nt = n // MOE_TILE
            h = swiglu(h, w_ffn_gate, w_ffn_up, w_ffn_down,
                       jnp.full((nt,), li, jnp.int32), jnp.full((nt,), MOE_TILE, jnp.int32),
                       gain=ln_ffn[i], name="swiglu_dense")
        else:
            h = hyena_mixer(hn, h, w_hy_in[li], b_hy_in[li], w_hy_conv[li], b_hy_conv[li],
                            w_hy_f1[li], b_hy_f1[li], w_hy_f2[li], b_hy_f2[li], hy_freq[li],
                            w_hy_f3[li], hy_bias[li], w_hy_out[li], batch, seq)
            wr_pad = jnp.pad(w_router[li].astype(F32), ((0, 0), (0, V7X_LANES - N_EXPERTS)))
            hn2_packed, logits = rmsnorm_router(h, ln_ffn[i], wr_pad)
            ys, dest2, gates = moe_swiglu(hn2_packed, logits[:, :N_EXPERTS], w_exp_gate[li],
                                          w_exp_up[li], w_exp_down[li])
            h = moe_combine(h, ys, dest2, gates)
        h = ple(h, p.reshape(depth, n, -1), ln_ple[i], w_ple_gate, w_ple_proj, i)
    return rmsnorm(h, final_norm, F32).reshape(batch, seq, d)
```
